```python
import functools
import jax
import jax.numpy as jnp
from jax import lax
import numpy as np

D_MODEL = 1024
BATCH = 16
SEQ = 256
DEPTH = 2
DEC_BATCH = 4
DEC_SEQ = 2048
PAST_LEN = 512

GRID_W = 64
EPS = 1e-6
ROPE_BASE = 10000.0
Q_BLOCK = 128
NEG_INF = -1e30

FNET_GROUPS = 8
FNET_GROUP_DIM = 64
FNET_WIDTH = FNET_GROUPS * FNET_GROUP_DIM
MLA_HEADS = 8
MLA_Q_RANK = 384
MLA_KV_RANK = 128
MLA_NOPE = 64
MLA_ROPE = 32
MLA_V = 64
MLA_SCALE = (MLA_NOPE + MLA_ROPE) ** -0.5
SWA_HEADS = 8
SWA_KV_HEADS = 2
SWA_GROUP = SWA_HEADS // SWA_KV_HEADS
SWA_HEAD_DIM = 64
SWA_WINDOW = 128
SWA_SCALE = SWA_HEAD_DIM ** -0.5
IN_SPLITS = (FNET_WIDTH, MLA_Q_RANK, MLA_KV_RANK, MLA_ROPE,
             SWA_HEADS * SWA_HEAD_DIM, SWA_KV_HEADS * SWA_HEAD_DIM, SWA_KV_HEADS * SWA_HEAD_DIM)
IN_WIDTH = sum(IN_SPLITS)
IN_OFFSETS = tuple(int(v) for v in np.cumsum(IN_SPLITS)[:-1])
N_BRANCHES = 3
N_MOD = 6
N_EXPERTS = 64
N_EXPERT_GROUPS = 8
TOPK_GROUPS = 4
TOP_K = 6
EXPERT_FF = 256
SHARED_FF = 256
ROUTED_SCALE = 2.5

kernel_name = 'hybrid_dit_fnet_mla_swa_moe_step'


def _rms(x, g):
    xf = x.astype(jnp.float32)
    y = xf * lax.rsqrt(jnp.mean(xf * xf, axis=-1, keepdims=True) + EPS)
    return (y * g.astype(jnp.float32)).astype(x.dtype)


def _modulation(cond, ada_w, ada_b):
    m = jax.nn.silu(cond) @ ada_w + ada_b
    return jnp.split(m[..., None, :], N_MOD, axis=-1)


def _axial_rope(x):
    T, R = x.shape[1], x.shape[-1]
    rows = T // GRID_W
    n_freq = R // 4
    inv = ROPE_BASE ** (-jnp.arange(n_freq, dtype=jnp.float32) / n_freq)
    row = jnp.repeat(jnp.arange(rows, dtype=jnp.float32), GRID_W)
    col = (jnp.arange(rows * GRID_W) % GRID_W).astype(jnp.float32)
    ang = jnp.stack([row, col], axis=-1)[:, :, None] * inv
    bshape = (1, T) + (1,) * (x.ndim - 3) + (2, n_freq)
    cos = jnp.cos(ang).reshape(bshape)
    sin = jnp.sin(ang).reshape(bshape)
    xf = x.astype(jnp.float32).reshape(x.shape[:-1] + (2, 2, n_freq))
    x1, x2 = xf[..., 0, :], xf[..., 1, :]
    out = jnp.stack([x1 * cos - x2 * sin, x2 * cos + x1 * sin], axis=-2)
    return out.reshape(x.shape).astype(x.dtype)


def _attend(q, segs, scale, sink=None):
    scores = []
    for k, v, mask in segs:
        s = jnp.einsum('bqgrd,bkgd->bgrqk', q, k).astype(jnp.float32) * scale
        if mask is not None:
            s = jnp.where(mask, s, NEG_INF)
        scores.append(s)
    m = functools.reduce(jnp.maximum, [s.max(axis=-1) for s in scores])
    if sink is not None:
        sink_b = sink.astype(jnp.float32)[None, :, :, None]
        m = jnp.maximum(m, sink_b)
    den = 0.0
    acc = 0.0
    for s, (k, v, _) in zip(scores, segs):
        p = jnp.exp(s - m[..., None])
        den = den + p.sum(axis=-1)
        acc = acc + jnp.einsum('bgrqk,bkgd->bqgrd', p, v.astype(jnp.float32))
    if sink is not None:
        den = den + jnp.exp(sink_b - m)
    out = acc / jnp.transpose(den, (0, 3, 1, 2))[..., None]
    return out.astype(q.dtype)


def _sweep_query_blocks(fn, q):
    B, T = q.shape[:2]
    nb = T // Q_BLOCK
    qb = jnp.moveaxis(q.reshape((B, nb, Q_BLOCK) + q.shape[2:]), 1, 0)
    out = lax.map(lambda a: fn(a[0], a[1]), (jnp.arange(nb), qb))
    return jnp.moveaxis(out, 0, 1).reshape((B, T) + out.shape[3:])


def _windowed_latent_attention(q, k, v, k_ctx, v_ctx, sink):
    S = q.shape[1]
    pad = ((0, 0), (Q_BLOCK, Q_BLOCK), (0, 0), (0, 0))
    k_pad = jnp.pad(k, pad)
    v_pad = jnp.pad(v, pad)
    qi = jnp.arange(Q_BLOCK)[:, None]
    kj = jnp.arange(3 * Q_BLOCK)[None, :]
    rel = kj - Q_BLOCK - qi

    def block(b, qb):
        kb = lax.dynamic_slice_in_dim(k_pad, b * Q_BLOCK, 3 * Q_BLOCK, axis=1)
        vb = lax.dynamic_slice_in_dim(v_pad, b * Q_BLOCK, 3 * Q_BLOCK, axis=1)
        kpos = (b - 1) * Q_BLOCK + kj
        mask = (jnp.abs(rel) <= SWA_WINDOW) & (kpos >= 0) & (kpos < S)
        return _attend(qb, [(k_ctx, v_ctx, None), (kb, vb, mask)], SWA_SCALE, sink)

    return _sweep_query_blocks(block, q)


def _fourier_mix(xf):
    B, T, _ = xf.shape
    z = xf.astype(jnp.float32).reshape(B, T, FNET_GROUPS, FNET_GROUP_DIM)
    y = jnp.fft.fft2(z, axes=(1, 3), norm='ortho').real
    return y.reshape(B, T, FNET_WIDTH).astype(xf.dtype)


def _mla_queries(q_down, lw):
    cq = _rms(q_down, lw['q_norm'])
    q = (cq @ lw['w_uq']).reshape(cq.shape[:2] + (MLA_HEADS, MLA_NOPE + MLA_ROPE))
    return q[..., :MLA_NOPE], q[..., MLA_NOPE:]


def _mla_expand_kv(ckv, k_rope, w_ukv):
    kv = (ckv @ w_ukv).reshape(ckv.shape[:2] + (MLA_HEADS, MLA_NOPE + MLA_V))
    kr = jnp.broadcast_to(k_rope[:, :, None, :], k_rope.shape[:2] + (MLA_HEADS, MLA_ROPE))
    return jnp.concatenate([kv[..., :MLA_NOPE], kr], axis=-1), kv[..., MLA_NOPE:]


def _merge(h, br_f, br_m, br_s, lw):
    g_f, g_m, g_s = jnp.split(jax.nn.sigmoid(h @ lw['w_gate'] + lw['b_gate']), N_BRANCHES, axis=-1)
    return (g_f * br_f + g_m * br_m + g_s * br_s) @ lw['w_out']


def _mixer_context(h, lw):
    B, T, _ = h.shape
    f_in, q_down, kv_down, k_rope, sq, sk, sv = jnp.split(h @ lw['w_in'], IN_OFFSETS, axis=-1)
    br_f = _fourier_mix(f_in) @ lw['w_fnet']
    q_nope, q_rope = _mla_queries(q_down, lw)
    q = jnp.concatenate([q_nope, q_rope], axis=-1)[:, :, :, None, :]
    ckv = _rms(kv_down, lw['kv_norm'])
    k_m, v_m = _mla_expand_kv(ckv, k_rope, lw['w_ukv'])
    o_m = _sweep_query_blocks(lambda b, qb: _attend(qb, [(k_m, v_m, None)], MLA_SCALE), q)
    br_m = o_m.reshape(B, T, -1) @ lw['w_mla_o']
    qs = sq.reshape(B, T, SWA_KV_HEADS, SWA_GROUP, SWA_HEAD_DIM)
    ks = sk.reshape(B, T, SWA_KV_HEADS, SWA_HEAD_DIM)
    vs = sv.reshape(B, T, SWA_KV_HEADS, SWA_HEAD_DIM)
    sink = lw['swa_sink'].reshape(SWA_KV_HEADS, SWA_GROUP)
    o_s = _sweep_query_blocks(lambda b, qb: _attend(qb, [(ks, vs, None)], SWA_SCALE, sink), qs)
    br_s = o_s.reshape(B, T, -1) @ lw['w_swa_o']
    return _merge(h, br_f, br_m, br_s, lw), (ckv, k_rope, ks, vs)


def _mixer_latent(h, lw, ckv_ctx, kr_ctx, k_ctx, v_ctx):
    B, T, _ = h.shape
    f_in, q_down, kv_down, k_rope, sq, sk, sv = jnp.split(h @ lw['w_in'], IN_OFFSETS, axis=-1)
    br_f = _fourier_mix(f_in) @ lw['w_fnet']
    q_nope, q_rope = _mla_queries(q_down, lw)
    q = jnp.concatenate([q_nope, _axial_rope(q_rope)], axis=-1)[:, :, :, None, :]
    k_lat, v_lat = _mla_expand_kv(_rms(kv_down, lw['kv_norm']), _axial_rope(k_rope), lw['w_ukv'])
    k_c, v_c = _mla_expand_kv(ckv_ctx, kr_ctx, lw['w_ukv'])
    o_m = _sweep_query_blocks(
        lambda b, qb: _attend(qb, [(k_c, v_c, None), (k_lat, v_lat, None)], MLA_SCALE), q)
    br_m = o_m.reshape(B, T, -1) @ lw['w_mla_o']
    qs = _axial_rope(sq.reshape(B, T, SWA_KV_HEADS, SWA_GROUP, SWA_HEAD_DIM))
    ks = _axial_rope(sk.reshape(B, T, SWA_KV_HEADS, SWA_HEAD_DIM))
    vs = sv.reshape(B, T, SWA_KV_HEADS, SWA_HEAD_DIM)
    sink = lw['swa_sink'].reshape(SWA_KV_HEADS, SWA_GROUP)
    o_s = _windowed_latent_attention(qs, ks, vs, k_ctx, v_ctx, sink)
    br_s = o_s.reshape(B, T, -1) @ lw['w_swa_o']
    return _merge(h, br_f, br_m, br_s, lw), None


def _moe(h, lw):
    shp = h.shape
    x = h.reshape(-1, shp[-1])
    n = x.shape[0]
    scores = jax.nn.sigmoid((x @ lw['router_w']).astype(jnp.float32))
    biased = scores + lw['router_bias'].astype(jnp.float32)
    grp = biased.reshape(n, N_EXPERT_GROUPS, N_EXPERTS // N_EXPERT_GROUPS)
    grp_score = lax.top_k(grp, 2)[0].sum(axis=-1)
    _, gidx = lax.top_k(grp_score, TOPK_GROUPS)
    gmask = jax.nn.one_hot(gidx, N_EXPERT_GROUPS, dtype=jnp.float32).sum(axis=-2)
    emask = jnp.repeat(gmask, N_EXPERTS // N_EXPERT_GROUPS, axis=-1)
    _, eidx = lax.top_k(jnp.where(emask > 0, biased, NEG_INF), TOP_K)
    sel = jnp.take_along_axis(scores, eidx, axis=-1)
    wts = sel / sel.sum(axis=-1, keepdims=True) * ROUTED_SCALE
    gates = jnp.einsum('nk,nke->ne', wts, jax.nn.one_hot(eidx, N_EXPERTS, dtype=jnp.float32))
    hg = jnp.einsum('nd,edf->nef', x, lw['exp_w1'])
    hu = jnp.einsum('nd,edf->nef', x, lw['exp_w3'])
    act = jax.nn.silu(hg) * hu * gates[..., None].astype(hg.dtype)
    routed = jnp.einsum('nef,efd->nd', act, lw['exp_w2'])
    shared = (jax.nn.silu(x @ lw['shared_w1']) * (x @ lw['shared_w3'])) @ lw['shared_w2']
    return (routed + shared).astype(h.dtype).reshape(shp)


def _sublayers(x, cond, lw, mixer):
    sh1, sc1, g1, sh2, sc2, g2 = _modulation(cond, lw['ada_w'], lw['ada_b'])
    h = _rms(x, lw['norm_g'][0]) * (1.0 + sc1) + sh1
    mix, aux = mixer(h)
    x = x + g1 * _rms(mix, lw['norm_g'][1])
    h = _rms(x, lw['norm_g'][2]) * (1.0 + sc2) + sh2
    x = x + g2 * _rms(_moe(h, lw), lw['norm_g'][3])
    return x, aux


def setup_inputs(seed: int = 0) -> dict:
    key = jax.random.key(seed)
    keys = iter(jax.random.split(key, 40))

    def nrm(shape, scale=1.0):
        return scale * jax.random.normal(next(keys), shape, jnp.float32)

    def gain(shape):
        return 1.0 + nrm(shape, 0.01)

    D = D_MODEL
    return {
        'x_prompt': nrm((BATCH, SEQ, D)),
        'x_sample': nrm((DEC_BATCH, DEC_SEQ, D)),
        'cache_mla_ckv': nrm((DEC_BATCH, DEPTH, PAST_LEN, MLA_KV_RANK)),
        'cache_mla_krope': nrm((DEC_BATCH, DEPTH, PAST_LEN, MLA_ROPE)),
        'cache_swa_k': nrm((DEC_BATCH, DEPTH, PAST_LEN, SWA_KV_HEADS, SWA_HEAD_DIM)),
        'cache_swa_v': nrm((DEC_BATCH, DEPTH, PAST_LEN, SWA_KV_HEADS, SWA_HEAD_DIM)),
        'c': nrm((DEC_BATCH, D)),
        'c_ctx': nrm((D,)),
        'ada_w': nrm((DEPTH, D, N_MOD * D), 0.5 * D ** -0.5),
        'ada_b': nrm((DEPTH, N_MOD * D), 0.01),
        'norm_g': gain((DEPTH, 4, D)),
        'w_in': nrm((DEPTH, D, IN_WIDTH), D ** -0.5),
        'q_norm': gain((DEPTH, MLA_Q_RANK)),
        'kv_norm': gain((DEPTH, MLA_KV_RANK)),
        'w_fnet': nrm((DEPTH, FNET_WIDTH, D), FNET_WIDTH ** -0.5),
        'w_uq': nrm((DEPTH, MLA_Q_RANK, MLA_HEADS * (MLA_NOPE + MLA_ROPE)), MLA_Q_RANK ** -0.5),
        'w_ukv': nrm((DEPTH, MLA_KV_RANK, MLA_HEADS * (MLA_NOPE + MLA_V)), MLA_KV_RANK ** -0.5),
        'w_mla_o': nrm((DEPTH, MLA_HEADS * MLA_V, D), (MLA_HEADS * MLA_V) ** -0.5),
        'swa_sink': nrm((DEPTH, SWA_HEADS), 0.5),
        'w_swa_o': nrm((DEPTH, SWA_HEADS * SWA_HEAD_DIM, D), (SWA_HEADS * SWA_HEAD_DIM) ** -0.5),
        'w_gate': nrm((DEPTH, D, N_BRANCHES * D), D ** -0.5),
        'b_gate': nrm((DEPTH, N_BRANCHES * D), 0.01),
        'w_out': nrm((DEPTH, D, D), D ** -0.5),
        'router_w': nrm((DEPTH, D, N_EXPERTS), D ** -0.5),
        'router_bias': nrm((DEPTH, N_EXPERTS), 0.01),
        'exp_w1': nrm((DEPTH, N_EXPERTS, D, EXPERT_FF), D ** -0.5),
        'exp_w3': nrm((DEPTH, N_EXPERTS, D, EXPERT_FF), D ** -0.5),
        'exp_w2': nrm((DEPTH, N_EXPERTS, EXPERT_FF, D), EXPERT_FF ** -0.5),
        'shared_w1': nrm((DEPTH, D, SHARED_FF), D ** -0.5),
        'shared_w3': nrm((DEPTH, D, SHARED_FF), D ** -0.5),
        'shared_w2': nrm((DEPTH, SHARED_FF, D), SHARED_FF ** -0.5),
    }


def reference(x_prompt, x_sample, cache_mla_ckv, cache_mla_krope, cache_swa_k, cache_swa_v,
              c, c_ctx, ada_w, ada_b, norm_g, w_in, q_norm, kv_norm, w_fnet, w_uq, w_ukv,
              w_mla_o, swa_sink, w_swa_o, w_gate, b_gate, w_out, router_w, router_bias,
              exp_w1, exp_w3, exp_w2, shared_w1, shared_w3, shared_w2):
    y_p = x_prompt
    y_s = x_sample
    ckv_list, krope_list, k_list, v_list = [], [], [], []
    for l in range(DEPTH):
        lw = {
            'ada_w': ada_w[l], 'ada_b': ada_b[l], 'norm_g': norm_g[l], 'w_in': w_in[l],
            'q_norm': q_norm[l], 'kv_norm': kv_norm[l], 'w_fnet': w_fnet[l], 'w_uq': w_uq[l],
            'w_ukv': w_ukv[l], 'w_mla_o': w_mla_o[l], 'swa_sink': swa_sink[l],
            'w_swa_o': w_swa_o[l], 'w_gate': w_gate[l], 'b_gate': b_gate[l], 'w_out': w_out[l],
            'router_w': router_w[l], 'router_bias': router_bias[l], 'exp_w1': exp_w1[l],
            'exp_w3': exp_w3[l], 'exp_w2': exp_w2[l], 'shared_w1': shared_w1[l],
            'shared_w3': shared_w3[l], 'shared_w2': shared_w2[l],
        }
        y_p, (ckv, krope, ks, vs) = _sublayers(
            y_p, c_ctx, lw, functools.partial(_mixer_context, lw=lw))
        ckv_list.append(ckv)
        krope_list.append(krope)
        k_list.append(ks)
        v_list.append(vs)
        y_s, _ = _sublayers(
            y_s, c, lw,
            functools.partial(_mixer_latent, lw=lw, ckv_ctx=cache_mla_ckv[:, l],
                              kr_ctx=cache_mla_krope[:, l], k_ctx=cache_swa_k[:, l],
                              v_ctx=cache_swa_v[:, l]))
    new_mla_ckv = jnp.stack(ckv_list, axis=1)
    new_mla_krope = jnp.stack(krope_list, axis=1)
    new_swa_k = jnp.stack(k_list, axis=1)
    new_swa_v = jnp.stack(v_list, axis=1)
    return (y_p, y_s, new_mla_ckv, new_mla_krope, new_swa_k, new_swa_v)
```

```python
import functools
import math

import numpy as np
import jax
import jax.numpy as jnp
from jax import lax
from jax.experimental import pallas as pl
from jax.experimental.pallas import tpu as pltpu

D_MODEL = 1024
BATCH = 16
SEQ = 256
DEPTH = 2
DEC_BATCH = 4
DEC_SEQ = 2048
PAST_LEN = 512
GRID_W = 64
EPS = 1e-6
ROPE_BASE = 10000.0
NEG_INF = -1e30

FNET_GROUPS = 8
FNET_GROUP_DIM = 64
FNET_WIDTH = 512
MLA_HEADS = 8
MLA_Q_RANK = 384
MLA_KV_RANK = 128
MLA_NOPE = 64
MLA_ROPE = 32
MLA_V = 64
MLA_SCALE = (MLA_NOPE + MLA_ROPE) ** -0.5
SWA_HEADS = 8
SWA_KV_HEADS = 2
SWA_HEAD_DIM = 64
SWA_WINDOW = 128
SWA_SCALE = SWA_HEAD_DIM ** -0.5
N_MOD = 6
N_EXPERTS = 64
N_EXPERT_GROUPS = 8
TOPK_GROUPS = 4
TOP_K = 6
EXPERT_FF = 256
SHARED_FF = 256
ROUTED_SCALE = 2.5

LANES = 128
TM = 256
N_CTX = BATCH * SEQ
N_LAT = DEC_BATCH * DEC_SEQ
N_TOK = N_CTX + N_LAT
N_CACHE = DEC_BATCH * PAST_LEN
NT_CTX = N_CTX // TM
NT_LAT = N_LAT // TM
NT = N_TOK // TM
LAT_TILES = DEC_SEQ // TM
TE = 256
S_MAX = N_TOK * TOP_K + N_EXPERTS * TE
NTE = S_MAX // TE
VMEM_LIMIT = 56 * 1024 * 1024

A_F = (0, 512)
A_QD = (512, 896)
A_KV = (896, 1024)
A_SQ = (1024, 1536)
A_SQS = (1536, 2048)
A_SK = (2048, 2176)
A_SKS = (2176, 2304)
A_SV = (2304, 2432)
A_KR = (2432, 2560)
A_KRS = (2560, 2688)
W_IN_WIDE = 2688
TAB_W = 1280

F32 = jnp.float32
BF16 = jnp.bfloat16


def _cparams(n_axes, parallel=False):
    sem = ("parallel" if parallel else "arbitrary",) * n_axes
    return pltpu.CompilerParams(dimension_semantics=sem, vmem_limit_bytes=VMEM_LIMIT)


def _dot(a, b):
    return jnp.dot(a, b, preferred_element_type=F32)


def _dot_nt(a, b):
    return lax.dot_general(a, b, (((1,), (1,)), ((), ())), preferred_element_type=F32)


def _rms_rows(v, g):
    return v * lax.rsqrt(jnp.mean(v * v, axis=-1, keepdims=True) + EPS) * g


def _const_spec(shape):
    return pl.BlockSpec(shape, lambda *_: (0,) * len(shape))


def _tab_row_block(i):
    return jnp.where(i < NT_CTX, 0, 1 + (i - NT_CTX) % LAT_TILES)


def _mod_kernel(cond_ref, w_ref, b_ref, o_ref):
    c = cond_ref[...]
    a = (c * jax.nn.sigmoid(c)).astype(BF16)
    o_ref[...] = _dot(a, w_ref[...].astype(BF16)) + b_ref[...]


def _modulation(cond8, ada_w, ada_b):
    tn = 512
    nj = N_MOD * D_MODEL // tn
    return pl.pallas_call(
        _mod_kernel,
        grid=(DEPTH, nj),
        in_specs=[
            pl.BlockSpec((8, D_MODEL), lambda l, j: (0, 0)),
            pl.BlockSpec((None, D_MODEL, tn), lambda l, j: (l, 0, j)),
            pl.BlockSpec((None, 1, tn), lambda l, j: (l, 0, j)),
        ],
        out_specs=pl.BlockSpec((None, 8, tn), lambda l, j: (l, 0, j)),
        out_shape=jax.ShapeDtypeStruct((DEPTH, 8, N_MOD * D_MODEL), F32),
        compiler_params=_cparams(2),
        name="modulation",
    )(cond8, ada_w, ada_b.reshape(DEPTH, 1, N_MOD * D_MODEL))


def _stage_a_kernel(x_ref, mod_ref, g_ref, win_ref, wg_ref, bg_ref, qn_ref, kvn_ref, tab_ref,
                    fin_ref, cq_ref, ckv_ref, kr_ref, sq_ref, sk_ref, sv_ref, gates_ref):
    x = x_ref[...]
    h = (_rms_rows(x, g_ref[...]) * (1.0 + mod_ref[:, 1024:2048]) + mod_ref[:, 0:1024]).astype(BF16)

    def proj(seg):
        return _dot(h, win_ref[:, seg[0]:seg[1]])

    fin_ref[...] = proj(A_F).astype(BF16)
    cq_ref[...] = _rms_rows(proj(A_QD), qn_ref[...]).astype(BF16)
    ckv_ref[...] = _rms_rows(proj(A_KV), kvn_ref[...])
    cos64 = tab_ref[:, 0:512]
    sin64 = tab_ref[:, 512:1024]
    sq = proj(A_SQ) * cos64 + proj(A_SQS) * sin64
    sq_ref[...] = (sq * SWA_SCALE).astype(BF16)
    sk_ref[...] = proj(A_SK) * cos64[:, 0:128] + proj(A_SKS) * sin64[:, 0:128]
    sv_ref[...] = proj(A_SV)
    kr_ref[...] = proj(A_KR) * tab_ref[:, 1024:1152] + proj(A_KRS) * tab_ref[:, 1152:1280]
    for c in range(3):
        lo, hi = c * D_MODEL, (c + 1) * D_MODEL
        gates_ref[:, lo:hi] = jax.nn.sigmoid(_dot(h, wg_ref[:, lo:hi]) + bg_ref[:, lo:hi]).astype(BF16)


def _stage_a(x, modt, g0, w_in_wide, w_gate, b_gate, q_norm, kv_norm, tab):
    row = lambda w: pl.BlockSpec((TM, w), lambda i: (i, 0))
    outs = [(512, BF16), (MLA_Q_RANK, BF16), (128, F32), (128, F32), (512, BF16), (128, F32),
            (128, F32), (3 * D_MODEL, BF16)]
    return pl.pallas_call(
        _stage_a_kernel,
        grid=(NT,),
        in_specs=[
            row(D_MODEL),
            pl.BlockSpec((None, 1, N_MOD * D_MODEL), lambda i: (i, 0, 0)),
            _const_spec((1, D_MODEL)),
            _const_spec((D_MODEL, W_IN_WIDE)),
            _const_spec((D_MODEL, 3 * D_MODEL)),
            _const_spec((1, 3 * D_MODEL)),
            _const_spec((1, MLA_Q_RANK)),
            _const_spec((1, MLA_KV_RANK)),
            pl.BlockSpec((TM, TAB_W), lambda i: (_tab_row_block(i), 0)),
        ],
        out_specs=[row(w) for w, _ in outs],
        out_shape=[jax.ShapeDtypeStruct((N_TOK, w), dt) for w, dt in outs],
        compiler_params=_cparams(1),
        name="stage_a",
    )(x, modt, g0, w_in_wide, w_gate, b_gate, q_norm, kv_norm, tab)


def _fnet_kernel(t_len, scale, fin_ref, f_ref, bd_ref, o_ref, zz_ref):
    @pl.when(pl.program_id(1) == 0)
    def _():
        z = fin_ref[...]
        zz_ref[0:t_len, :] = _dot(z, bd_ref[:, 0:512]).astype(BF16)
        zz_ref[t_len:2 * t_len, :] = _dot(z, bd_ref[:, 512:1024]).astype(BF16)

    o_ref[...] = (_dot(f_ref[...], zz_ref[...]) * scale).astype(BF16)


def _fnet(fin, fmat, bd, n_batch, t_len, row_block0):
    scale = 1.0 / math.sqrt(t_len * FNET_GROUP_DIM)
    return pl.pallas_call(
        functools.partial(_fnet_kernel, t_len, scale),
        grid=(n_batch, t_len // TM),
        in_specs=[
            pl.BlockSpec((t_len, FNET_WIDTH), lambda b, i: (row_block0 + b, 0)),
            pl.BlockSpec((TM, 2 * t_len), lambda b, i: (i, 0)),
            _const_spec((FNET_WIDTH, 2 * FNET_WIDTH)),
        ],
        out_specs=pl.BlockSpec((TM, FNET_WIDTH), lambda b, i: (b * (t_len // TM) + i, 0)),
        out_shape=jax.ShapeDtypeStruct((n_batch * t_len, FNET_WIDTH), BF16),
        scratch_shapes=[pltpu.VMEM((2 * t_len, FNET_WIDTH), BF16)],
        compiler_params=_cparams(2),
        name=f"fnet_{t_len}",
    )(fin, fmat, bd)


def _mla_prep_kernel(cq_ref, ckv_ref, kr_ref, tab_ref, wqa_ref, wqb_ref, wk_ref, e_ref, wv_ref,
                     q_ref, k_ref, v_ref):
    cq = cq_ref[...]
    ckv = ckv_ref[...].astype(BF16)
    kr = kr_ref[...].astype(BF16)
    lane = lax.broadcasted_iota(jnp.int32, (1, LANES), 1)
    rope_lane = jnp.logical_and(lane >= MLA_NOPE, lane < MLA_NOPE + MLA_ROPE)
    cos_h = jnp.where(rope_lane, tab_ref[:, 0:128], 1.0)
    sin_h = tab_ref[:, 128:256]
    for hd in range(MLA_HEADS):
        lo, hi = hd * LANES, (hd + 1) * LANES
        q = _dot(cq, wqa_ref[:, lo:hi]) * cos_h + _dot(cq, wqb_ref[:, lo:hi]) * sin_h
        q_ref[:, lo:hi] = (q * MLA_SCALE).astype(BF16)
    k_ref[...] = (_dot(ckv, wk_ref[...]) + _dot(kr, e_ref[...])).astype(BF16)
    v_ref[...] = _dot(ckv, wv_ref[...]).astype(BF16)


def _mla_prep(cq_all, ckv_all, kr_all, tab, wqa, wqb, wk, e_mat, wv):
    n_rows = cq_all.shape[0]
    nt = n_rows // TM
    row = lambda w: pl.BlockSpec((TM, w), lambda i: (i, 0))

    def tab_map(i):
        return (jnp.where(i < NT, _tab_row_block(i), 0), 4)

    return pl.pallas_call(
        _mla_prep_kernel,
        grid=(nt,),
        in_specs=[
            row(MLA_Q_RANK), row(128), row(128),
            pl.BlockSpec((TM, 256), tab_map),
            _const_spec((MLA_Q_RANK, 1024)), _const_spec((MLA_Q_RANK, 1024)),
            _const_spec((128, 1024)), _const_spec((128, 1024)), _const_spec((128, 512)),
        ],
        out_specs=[row(1024), row(1024), row(512)],
        out_shape=[jax.ShapeDtypeStruct((n_rows, 1024), BF16),
                   jax.ShapeDtypeStruct((n_rows, 1024), BF16),
                   jax.ShapeDtypeStruct((n_rows, 512), BF16)],
        compiler_params=_cparams(1),
        name="mla_prep",
    )(cq_all, ckv_all, kr_all, tab, wqa, wqb, wk, e_mat, wv)


def _mla_attn_kernel(n_seg, q_ref, *refs):
    k_refs = refs[0:n_seg]
    v_refs = refs[n_seg:2 * n_seg]
    o_ref = refs[2 * n_seg]
    lane = lax.broadcasted_iota(jnp.int32, (1, LANES), 1)
    low = lane < MLA_V
    acc = None
    for hh in range(2):
        q = q_ref[:, hh * LANES:(hh + 1) * LANES]
        ss = [_dot_nt(q, k[:, hh * LANES:(hh + 1) * LANES]) for k in k_refs]
        m = functools.reduce(jnp.maximum, [s.max(axis=-1, keepdims=True) for s in ss])
        den = None
        o = None
        keep = low if hh == 0 else jnp.logical_not(low)
        for s, v_ref in zip(ss, v_refs):
            p = jnp.exp(s - m)
            ps = p.sum(axis=-1, keepdims=True)
            vm = jnp.where(keep, v_ref[...], jnp.zeros((), BF16))
            po = _dot(p.astype(BF16), vm)
            den = ps if den is None else den + ps
            o = po if o is None else o + po
        o = o / den
        acc = o if acc is None else acc + o
    o_ref[...] = acc.astype(BF16)


def _mla_attn(q_all, k_all, v_all, latent):
    if latent:
        n_b, n_q = DEC_BATCH, DEC_SEQ // TM
        q0 = NT_CTX
        kv_specs = [
            pl.BlockSpec((PAST_LEN, 256), lambda b, hp, i: (N_TOK // PAST_LEN + b, hp)),
            pl.BlockSpec((DEC_SEQ, 256), lambda b, hp, i: (N_CTX // DEC_SEQ + b, hp)),
            pl.BlockSpec((PAST_LEN, 128), lambda b, hp, i: (N_TOK // PAST_LEN + b, hp)),
            pl.BlockSpec((DEC_SEQ, 128), lambda b, hp, i: (N_CTX // DEC_SEQ + b, hp)),
        ]
        args = (q_all, k_all, k_all, v_all, v_all)
        n_seg = 2
    else:
        n_b, n_q = BATCH, 1
        q0 = 0
        kv_specs = [
            pl.BlockSpec((SEQ, 256), lambda b, hp, i: (b, hp)),
            pl.BlockSpec((SEQ, 128), lambda b, hp, i: (b, hp)),
        ]
        args = (q_all, k_all, v_all)
        n_seg = 1
    return pl.pallas_call(
        functools.partial(_mla_attn_kernel, n_seg),
        grid=(n_b, MLA_HEADS // 2, n_q),
        in_specs=[pl.BlockSpec((TM, 256), lambda b, hp, i: (q0 + b * n_q + i, hp))] + kv_specs,
        out_specs=pl.BlockSpec((TM, 128), lambda b, hp, i: (b * n_q + i, hp)),
        out_shape=jax.ShapeDtypeStruct((n_b * n_q * TM, MLA_HEADS * MLA_V), BF16),
        compiler_params=_cparams(3),
        name="mla_attn_lat" if latent else "mla_attn_ctx",
    )(*args)


def _swa_kernel(windowed, n_qb, sink_ref, q_ref, *refs):
    n_seg = 4 if windowed else 1
    k_refs = refs[0:n_seg]
    v_refs = refs[n_seg:2 * n_seg]
    o_ref = refs[2 * n_seg]
    tq = q_ref.shape[0]
    qb = pl.program_id(1)
    lane = lax.broadcasted_iota(jnp.int32, (1, LANES), 1)
    low = lane < SWA_HEAD_DIM
    high = jnp.logical_not(low)

    ks = [r[...] for r in k_refs]
    vs = [r[...] for r in v_refs]
    ks_sw = [pltpu.roll(a, SWA_HEAD_DIM, 1) for a in ks]
    vs_sw = [pltpu.roll(a, SWA_HEAD_DIM, 1) for a in vs]

    if windowed:
        qi = lax.broadcasted_iota(jnp.int32, (2 * tq, SWA_WINDOW), 0) % tq
        kj = lax.broadcasted_iota(jnp.int32, (2 * tq, SWA_WINDOW), 1)
        masks = [None,
                 jnp.logical_and(kj >= qi, qb > 0),
                 None,
                 jnp.logical_and(kj <= qi, qb < n_qb - 1)]
    else:
        masks = [None]
    top_rows = lax.broadcasted_iota(jnp.int32, (2 * tq, 1), 0) < tq

    for g in range(SWA_KV_HEADS):
        qs = jnp.concatenate([q_ref[:, 256 * g:256 * g + 128],
                              q_ref[:, 256 * g + 128:256 * g + 256]], axis=0)
        out = None
        for half in range(2):
            keep = low if half == 0 else high
            straight = (g == half)
            kh = [jnp.where(keep, a if straight else b, 0.0).astype(BF16) for a, b in zip(ks, ks_sw)]
            vh = [jnp.where(keep, a if straight else b, 0.0).astype(BF16) for a, b in zip(vs, vs_sw)]
            ss = []
            for kk, mk in zip(kh, masks):
                s = _dot_nt(qs, kk)
                if mk is not None:
                    s = jnp.where(mk, s, NEG_INF)
                ss.append(s)
            sink = jnp.where(top_rows, sink_ref[4 * g + half], sink_ref[4 * g + 2 + half])
            m = functools.reduce(jnp.maximum, [s.max(axis=-1, keepdims=True) for s in ss])
            m = jnp.maximum(m, sink)
            den = jnp.exp(sink - m)
            o = None
            for s, vv in zip(ss, vh):
                p = jnp.exp(s - m)
                den = den + p.sum(axis=-1, keepdims=True)
                po = _dot(p.astype(BF16), vv)
                o = po if o is None else o + po
            o = o / den
            out = o if out is None else out + o
        o_ref[:, 256 * g:256 * g + 128] = out[0:tq].astype(BF16)
        o_ref[:, 256 * g + 128:256 * g + 256] = out[tq:2 * tq].astype(BF16)


def _swa_attn(sink, sq, sk, sv, cache_k, cache_v, latent):
    smem = pl.BlockSpec(memory_space=pltpu.SMEM)
    if latent:
        tq = SWA_WINDOW
        n_b, n_qb = DEC_BATCH, DEC_SEQ // tq
        base = N_CTX // tq

        def prev(b, i):
            return (base + b * n_qb + jnp.maximum(i - 1, 0), 0)

        def cur(b, i):
            return (base + b * n_qb + i, 0)

        def nxt(b, i):
            return (base + b * n_qb + jnp.minimum(i + 1, n_qb - 1), 0)

        cache = pl.BlockSpec((None, PAST_LEN, 128), lambda b, i: (b, 0, 0))
        blk = lambda f: pl.BlockSpec((tq, 128), f)
        kv_specs = [cache, blk(prev), blk(cur), blk(nxt)] * 2
        args = (cache_k, sk, sk, sk, cache_v, sv, sv, sv)
        q_spec = pl.BlockSpec((tq, 512), cur)
        o_spec = pl.BlockSpec((tq, 512), lambda b, i: (b * n_qb + i, 0))
    else:
        tq = SEQ
        n_b, n_qb = BATCH, 1
        blk = pl.BlockSpec((tq, 128), lambda b, i: (b, 0))
        kv_specs = [blk, blk]
        args = (sk, sv)
        q_spec = pl.BlockSpec((tq, 512), lambda b, i: (b, 0))
        o_spec = q_spec
    return pl.pallas_call(
        functools.partial(_swa_kernel, latent, n_qb),
        grid=(n_b, n_qb),
        in_specs=[smem, q_spec] + kv_specs,
        out_specs=o_spec,
        out_shape=jax.ShapeDtypeStruct((n_b * n_qb * tq, 512), BF16),
        compiler_params=_cparams(2),
        name="swa_lat" if latent else "swa_ctx",
    )(sink, sq, *args)


def _stage_e_kernel(x_ref, mod_ref, g1_ref, g2_ref, fn_ref, om_ref, os_ref, gates_ref,
                    wf_ref, wm_ref, ws_ref, wo_ref, x1_ref, h2_ref):
    merged = (gates_ref[:, 0:1024].astype(F32) * _dot(fn_ref[...], wf_ref[...])
              + gates_ref[:, 1024:2048].astype(F32) * _dot(om_ref[...], wm_ref[...])
              + gates_ref[:, 2048:3072].astype(F32) * _dot(os_ref[...], ws_ref[...]))
    mix = _dot(merged.astype(BF16), wo_ref[...])
    x1 = x_ref[...] + mod_ref[:, 2048:3072] * _rms_rows(mix, g1_ref[...])
    x1_ref[...] = x1
    h2 = _rms_rows(x1, g2_ref[...]) * (1.0 + mod_ref[:, 4096:5120]) + mod_ref[:, 3072:4096]
    h2_ref[...] = h2.astype(BF16)


def _stage_e(x, modt, g1, g2, fn, om, osw, gates, wf, wm, ws, wo):
    row = lambda w: pl.BlockSpec((TM, w), lambda i: (i, 0))
    return pl.pallas_call(
        _stage_e_kernel,
        grid=(NT,),
        in_specs=[
            row(D_MODEL),
            pl.BlockSpec((None, 1, N_MOD * D_MODEL), lambda i: (i, 0, 0)),
            _const_spec((1, D_MODEL)), _const_spec((1, D_MODEL)),
            row(512), row(512), row(512), row(3 * D_MODEL),
            _const_spec((512, D_MODEL)), _const_spec((512, D_MODEL)), _const_spec((512, D_MODEL)),
            _const_spec((D_MODEL, D_MODEL)),
        ],
        out_specs=[row(D_MODEL), row(D_MODEL)],
        out_shape=[jax.ShapeDtypeStruct((N_TOK, D_MODEL), F32),
                   jax.ShapeDtypeStruct((N_TOK, D_MODEL), BF16)],
        compiler_params=_cparams(1),
        name="stage_e",
    )(x, modt, g1, g2, fn, om, osw, gates, wf, wm, ws, wo)


def _router_kernel(h_ref, rwt_ref, rb_ref, s1_ref, s3_ref, s2_ref, tri_ref,
                   shared_ref, eidx_ref, ew_ref, epos_ref, cnt_ref, carry_ref):
    @pl.when(pl.program_id(0) == 0)
    def _():
        carry_ref[...] = jnp.zeros_like(carry_ref)

    h = h_ref[...]
    act = jax.nn.silu(_dot(h, s1_ref[...])) * _dot(h, s3_ref[...])
    shared_ref[...] = _dot(act.astype(BF16), s2_ref[...])

    gsz = N_EXPERTS // N_EXPERT_GROUPS
    scores = jax.nn.sigmoid(_dot_nt(rwt_ref[...], h))
    biased = scores + rb_ref[...]
    mem = lax.broadcasted_iota(jnp.int32, (gsz, TM), 0).astype(F32)
    gs_rows = []
    for g in range(N_EXPERT_GROUPS):
        bg = biased[g * gsz:(g + 1) * gsz, :]
        m1 = bg.max(axis=0, keepdims=True)
        first = jnp.min(jnp.where(bg == m1, mem, float(gsz)), axis=0, keepdims=True)
        m2 = jnp.where(mem == first, -jnp.inf, bg).max(axis=0, keepdims=True)
        gs_rows.append(m1 + m2)
    gs = jnp.concatenate(gs_rows, axis=0)
    gid = lax.broadcasted_iota(jnp.int32, gs.shape, 0).astype(F32)
    gsel = jnp.zeros(gs.shape, F32)
    for _ in range(TOPK_GROUPS):
        mx = gs.max(axis=0, keepdims=True)
        pick = gid == jnp.min(jnp.where(gs == mx, gid, float(N_EXPERT_GROUPS)), axis=0, keepdims=True)
        gsel = jnp.where(pick, 1.0, gsel)
        gs = jnp.where(pick, -jnp.inf, gs)
    emask = jnp.concatenate(
        [jnp.broadcast_to(gsel[g:g + 1, :], (gsz, TM)) for g in range(N_EXPERT_GROUPS)], axis=0)
    cand = jnp.where(emask > 0.5, biased, NEG_INF)
    eid = lax.broadcasted_iota(jnp.int32, cand.shape, 0).astype(F32)
    picks = []
    self32 = jnp.zeros(cand.shape, F32)
    for _ in range(TOP_K):
        mx = cand.max(axis=0, keepdims=True)
        pick = eid == jnp.min(jnp.where(cand == mx, eid, float(N_EXPERTS)), axis=0, keepdims=True)
        picks.append(pick)
        self32 = jnp.where(pick, 1.0, self32)
        cand = jnp.where(pick, -jnp.inf, cand)
    pos = _dot(self32.astype(BF16), tri_ref[...]) + carry_ref[...]
    sel_scores = [jnp.sum(jnp.where(p, scores, 0.0), axis=0, keepdims=True) for p in picks]
    wsum = functools.reduce(lambda a, b: a + b, sel_scores)
    zero_f = jnp.zeros((2, TM), F32)
    eidx = [jnp.sum(jnp.where(p, eid, 0.0), axis=0, keepdims=True) for p in picks]
    epos = [jnp.sum(jnp.where(p, pos, 0.0), axis=0, keepdims=True) for p in picks]
    ew = [s / wsum * ROUTED_SCALE for s in sel_scores]
    eidx_ref[...] = jnp.concatenate(eidx + [zero_f], axis=0).astype(jnp.int32)
    epos_ref[...] = jnp.concatenate(epos + [zero_f], axis=0).astype(jnp.int32)
    ew_ref[...] = jnp.concatenate(ew + [zero_f], axis=0)
    total = carry_ref[...] + jnp.sum(self32, axis=1, keepdims=True)
    carry_ref[...] = total
    cnt_ref[...] = jnp.broadcast_to(total, cnt_ref.shape).astype(jnp.int32)


def _router(h2, rwt, rbias, s1, s3, s2, tri):
    col = lambda dt: (pl.BlockSpec((8, TM), lambda i: (0, i)), jax.ShapeDtypeStruct((8, N_TOK), dt))
    specs = [col(jnp.int32), col(F32), col(jnp.int32)]
    return pl.pallas_call(
        _router_kernel,
        grid=(NT,),
        in_specs=[
            pl.BlockSpec((TM, D_MODEL), lambda i: (i, 0)),
            _const_spec((N_EXPERTS, D_MODEL)), _const_spec((N_EXPERTS, 1)),
            _const_spec((D_MODEL, SHARED_FF)), _const_spec((D_MODEL, SHARED_FF)),
            _const_spec((SHARED_FF, D_MODEL)), _const_spec((TM, TM)),
        ],
        out_specs=[pl.BlockSpec((TM, D_MODEL), lambda i: (i, 0))] + [s for s, _ in specs]
        + [_const_spec((N_EXPERTS, LANES))],
        out_shape=[jax.ShapeDtypeStruct((N_TOK, D_MODEL), F32)] + [s for _, s in specs]
        + [jax.ShapeDtypeStruct((N_EXPERTS, LANES), jnp.int32)],
        scratch_shapes=[pltpu.VMEM((N_EXPERTS, 1), F32)],
        compiler_params=_cparams(1),
        name="router_shared",
    )(h2, rwt, rbias, s1, s3, s2, tri)


def _expert_kernel(te_ref, tv_ref, x_ref, w1_ref, w3_ref, w2_ref, o_ref):
    j = pl.program_id(0)

    @pl.when(tv_ref[j] == 1)
    def _():
        x = x_ref[...]
        hg = _dot(x, w1_ref[...].astype(BF16))
        hu = _dot(x, w3_ref[...].astype(BF16))
        act = (jax.nn.silu(hg) * hu).astype(BF16)
        o_ref[...] = _dot(act, w2_ref[...].astype(BF16))

    @pl.when(tv_ref[j] == 0)
    def _():
        o_ref[...] = jnp.zeros_like(o_ref)


def _experts(tile_expert, tile_valid, xs, w1, w3, w2):
    grid_spec = pltpu.PrefetchScalarGridSpec(
        num_scalar_prefetch=2,
        grid=(NTE,),
        in_specs=[
            pl.BlockSpec((TE, D_MODEL), lambda j, te, tv: (j, 0)),
            pl.BlockSpec((None, D_MODEL, EXPERT_FF), lambda j, te, tv: (te[j], 0, 0)),
            pl.BlockSpec((None, D_MODEL, EXPERT_FF), lambda j, te, tv: (te[j], 0, 0)),
            pl.BlockSpec((None, EXPERT_FF, D_MODEL), lambda j, te, tv: (te[j], 0, 0)),
        ],
        out_specs=pl.BlockSpec((TE, D_MODEL), lambda j, te, tv: (j, 0)),
    )
    return pl.pallas_call(
        _expert_kernel,
        grid_spec=grid_spec,
        out_shape=jax.ShapeDtypeStruct((S_MAX, D_MODEL), F32),
        compiler_params=_cparams(1),
        name="experts",
    )(tile_expert, tile_valid, xs, w1, w3, w2)


def _stage_g_kernel(x1_ref, mod_ref, g3_ref, routed_ref, shared_ref, o_ref):
    y = routed_ref[...] + shared_ref[...]
    o_ref[...] = x1_ref[...] + mod_ref[:, 5120:6144] * _rms_rows(y, g3_ref[...])


def _stage_g(x1, modt, g3, routed, shared):
    row = pl.BlockSpec((TM, D_MODEL), lambda i: (i, 0))
    return pl.pallas_call(
        _stage_g_kernel,
        grid=(NT,),
        in_specs=[row, pl.BlockSpec((None, 1, N_MOD * D_MODEL), lambda i: (i, 0, 0)),
                  _const_spec((1, D_MODEL)), row, row],
        out_specs=row,
        out_shape=jax.ShapeDtypeStruct((N_TOK, D_MODEL), F32),
        compiler_params=_cparams(1),
        name="stage_g",
    )(x1, modt, g3, routed, shared)


def _rope_tables():
    t = jnp.arange(DEC_SEQ)
    pos = jnp.stack([(t // GRID_W).astype(F32), (t % GRID_W).astype(F32)], axis=-1)

    def table(r):
        n_freq = r // 4
        inv = ROPE_BASE ** (-jnp.arange(n_freq, dtype=F32) / n_freq)
        ang = pos[:, :, None] * inv
        cos = jnp.cos(ang)
        sin = jnp.sin(ang)
        cos_t = jnp.stack([cos, cos], axis=2).reshape(DEC_SEQ, r)
        sin_t = jnp.stack([-sin, sin], axis=2).reshape(DEC_SEQ, r)
        return cos_t, sin_t

    c64, s64 = table(SWA_HEAD_DIM)
    c32, s32 = table(MLA_ROPE)
    lat = jnp.concatenate([jnp.tile(c64, (1, 8)), jnp.tile(s64, (1, 8)),
                           jnp.tile(c32, (1, 4)), jnp.tile(s32, (1, 4))], axis=1)
    ident = jnp.concatenate([jnp.ones((TM, 512), F32), jnp.zeros((TM, 512), F32),
                             jnp.ones((TM, 128), F32), jnp.zeros((TM, 128), F32)], axis=1)
    return jnp.concatenate([ident, lat], axis=0)


def _dft_pair(n):
    k = jnp.arange(n, dtype=jnp.int32)
    ang = ((k[:, None] * k[None, :]) % n).astype(F32) * (2.0 * math.pi / n)
    return jnp.cos(ang), jnp.sin(ang)


def _fnet_tables():
    c64, s64 = _dft_pair(FNET_GROUP_DIM)
    eye = jnp.eye(FNET_GROUPS, dtype=F32)
    bd = jnp.concatenate([jnp.kron(eye, c64), jnp.kron(eye, s64)], axis=1).astype(BF16)
    mats = []
    for t_len in (SEQ, DEC_SEQ):
        c, s = _dft_pair(t_len)
        mats.append(jnp.concatenate([c, -s], axis=1).astype(BF16))
    return bd, mats[0], mats[1]


def _swap_perm(width, half):
    return np.arange(width) ^ half


def _layer_weights(l, w_in, w_uq, w_ukv):
    w = w_in[l]
    kr = w[:, 1024:1056]
    sq = w[:, 1056:1568]
    sk = w[:, 1568:1696]
    pad96 = jnp.zeros((D_MODEL, 96), F32)
    wide = jnp.concatenate([
        w[:, 0:1024], sq, sq[:, _swap_perm(512, 16)], sk, sk[:, _swap_perm(128, 16)],
        w[:, 1696:1824], kr, pad96, kr[:, _swap_perm(32, 8)], pad96], axis=1).astype(BF16)

    uq = w_uq[l].reshape(MLA_Q_RANK, MLA_HEADS, MLA_NOPE + MLA_ROPE)
    rope_w = uq[:, :, MLA_NOPE:]
    z32 = jnp.zeros((MLA_Q_RANK, MLA_HEADS, 32), F32)
    z64 = jnp.zeros((MLA_Q_RANK, MLA_HEADS, 64), F32)
    wqa = jnp.concatenate([uq, z32], axis=2).reshape(MLA_Q_RANK, 1024).astype(BF16)
    wqb = jnp.concatenate([z64, rope_w[:, :, _swap_perm(32, 8)], z32], axis=2)
    wqb = wqb.reshape(MLA_Q_RANK, 1024).astype(BF16)
    ukv = w_ukv[l].reshape(MLA_KV_RANK, MLA_HEADS, MLA_NOPE + MLA_V)
    wk = jnp.concatenate([ukv[:, :, :MLA_NOPE], jnp.zeros((MLA_KV_RANK, MLA_HEADS, 64), F32)],
                         axis=2).reshape(MLA_KV_RANK, 1024).astype(BF16)
    wv = ukv[:, :, MLA_NOPE:].reshape(MLA_KV_RANK, 512).astype(BF16)
    return wide, wqa, wqb, wk, wv


def _rope_placement():
    e = np.zeros((128, 1024), np.float32)
    for hd in range(MLA_HEADS):
        for i in range(MLA_ROPE):
            e[i, hd * 128 + MLA_NOPE + i] = 1.0
    return jnp.asarray(e, BF16)


def _moe_dispatch_plan(eidx, epos, counts):
    padded = ((counts + TE - 1) // TE) * TE
    ends = jnp.cumsum(padded)
    offs = ends - padded
    slot = jnp.take(offs, eidx, axis=0) + epos
    starts = jnp.arange(NTE, dtype=jnp.int32) * TE
    tile_expert = jnp.minimum(jnp.searchsorted(ends, starts, side="right"), N_EXPERTS - 1)
    tile_valid = (starts < ends[-1]).astype(jnp.int32)
    return slot, tile_expert.astype(jnp.int32), tile_valid


def kernel(x_prompt, x_sample, cache_mla_ckv, cache_mla_krope, cache_swa_k, cache_swa_v, c, c_ctx,
           ada_w, ada_b, norm_g, w_in, q_norm, kv_norm, w_fnet, w_uq, w_ukv, w_mla_o, swa_sink,
           w_swa_o, w_gate, b_gate, w_out, router_w, router_bias, exp_w1, exp_w3, exp_w2,
           shared_w1, shared_w3, shared_w2):
    x = jnp.concatenate([x_prompt.reshape(N_CTX, D_MODEL), x_sample.reshape(N_LAT, D_MODEL)], axis=0)

    cond8 = jnp.concatenate([c_ctx[None, :], c, jnp.zeros((3, D_MODEL), F32)], axis=0)
    mod = _modulation(cond8, ada_w, ada_b)
    tile_cond = np.concatenate([np.zeros(NT_CTX, np.int32),
                                1 + np.arange(NT_LAT, dtype=np.int32) // LAT_TILES])

    tab = _rope_tables()
    bd, f_ctx, f_lat = _fnet_tables()
    e_mat = _rope_placement()
    tri = jnp.asarray(np.triu(np.ones((TM, TM), np.float32), 1), BF16)
    tok_ids = jnp.broadcast_to(jnp.arange(N_TOK, dtype=jnp.int32)[None, :], (TOP_K, N_TOK))

    new_ckv, new_kr, new_k, new_v = [], [], [], []
    for l in range(DEPTH):
        modt = mod[l][tile_cond][:, None, :]
        wide, wqa, wqb, wk, wv = _layer_weights(l, w_in, w_uq, w_ukv)
        ng = norm_g[l]

        fin, cq, ckv, kr, sq, sk, sv, gates = _stage_a(
            x, modt, ng[0:1], wide, w_gate[l].astype(BF16), b_gate[l][None, :],
            q_norm[l][None, :], kv_norm[l][None, :], tab)

        new_ckv.append(ckv[:N_CTX].reshape(BATCH, SEQ, MLA_KV_RANK))
        new_kr.append(kr[:N_CTX, :MLA_ROPE].reshape(BATCH, SEQ, MLA_ROPE))
        new_k.append(sk[:N_CTX].reshape(BATCH, SEQ, SWA_KV_HEADS, SWA_HEAD_DIM))
        new_v.append(sv[:N_CTX].reshape(BATCH, SEQ, SWA_KV_HEADS, SWA_HEAD_DIM))

        fn = jnp.concatenate([_fnet(fin, f_ctx, bd, BATCH, SEQ, 0),
                              _fnet(fin, f_lat, bd, DEC_BATCH, DEC_SEQ, N_CTX // DEC_SEQ)], axis=0)

        cq_all = jnp.concatenate([cq, jnp.zeros((N_CACHE, MLA_Q_RANK), BF16)], axis=0)
        ckv_all = jnp.concatenate([ckv, cache_mla_ckv[:, l].reshape(N_CACHE, MLA_KV_RANK)], axis=0)
        kr_cache = jnp.pad(cache_mla_krope[:, l].reshape(N_CACHE, MLA_ROPE), ((0, 0), (0, 96)))
        kr_all = jnp.concatenate([kr, kr_cache], axis=0)
        q_m, k_m, v_m = _mla_prep(cq_all, ckv_all, kr_all, tab, wqa, wqb, wk, e_mat, wv)
        om = jnp.concatenate([_mla_attn(q_m, k_m, v_m, latent=False),
                              _mla_attn(q_m, k_m, v_m, latent=True)], axis=0)

        ck = cache_swa_k[:, l].reshape(DEC_BATCH, PAST_LEN, 128)
        cv = cache_swa_v[:, l].reshape(DEC_BATCH, PAST_LEN, 128)
        osw = jnp.concatenate([_swa_attn(swa_sink[l], sq, sk, sv, ck, cv, latent=False),
                               _swa_attn(swa_sink[l], sq, sk, sv, ck, cv, latent=True)], axis=0)

        x1, h2 = _stage_e(x, modt, ng[1:2], ng[2:3], fn, om, osw, gates,
                          w_fnet[l].astype(BF16), w_mla_o[l].astype(BF16),
                          w_swa_o[l].astype(BF16), w_out[l].astype(BF16))

        shared, eidx, ew, epos, counts = _router(
            h2, router_w[l].T.astype(BF16), router_bias[l][:, None],
            shared_w1[l].astype(BF16), shared_w3[l].astype(BF16), shared_w2[l].astype(BF16), tri)
        slot, tile_expert, tile_valid = _moe_dispatch_plan(eidx[:TOP_K], epos[:TOP_K], counts[:, 0])
        sorted_tok = jnp.zeros((S_MAX,), jnp.int32).at[slot.reshape(-1)].set(tok_ids.reshape(-1))
        xs = jnp.take(h2, sorted_tok, axis=0)
        ys = _experts(tile_expert, tile_valid, xs, exp_w1[l], exp_w3[l], exp_w2[l])
        routed = jnp.sum(jnp.take(ys, slot, axis=0) * ew[:TOP_K, :, None], axis=0)

        x = _stage_g(x1, modt, ng[3:4], routed, shared)

    y_p = x[:N_CTX].reshape(BATCH, SEQ, D_MODEL)
    y_s = x[N_CTX:].reshape(DEC_BATCH, DEC_SEQ, D_MODEL)
    return (y_p, y_s, jnp.stack(new_ckv, axis=1), jnp.stack(new_kr, axis=1),
            jnp.stack(new_k, axis=1), jnp.stack(new_v, axis=1))
```

```python
import functools
import math

import numpy as np
import jax
import jax.numpy as jnp
from jax import lax
from jax.experimental import pallas as pl
from jax.experimental.pallas import tpu as pltpu
from jax.experimental.pallas import tpu_sc as plsc

D_MODEL = 1024
BATCH = 16
SEQ = 256
DEPTH = 2
DEC_BATCH = 4
DEC_SEQ = 2048
PAST_LEN = 512
GRID_W = 64
EPS = 1e-6
ROPE_BASE = 10000.0
NEG_INF = -1e30

FNET_GROUPS = 8
FNET_GROUP_DIM = 64
FNET_WIDTH = 512
MLA_HEADS = 8
MLA_Q_RANK = 384
MLA_KV_RANK = 128
MLA_NOPE = 64
MLA_ROPE = 32
MLA_V = 64
MLA_SCALE = (MLA_NOPE + MLA_ROPE) ** -0.5
SWA_HEADS = 8
SWA_KV_HEADS = 2
SWA_HEAD_DIM = 64
SWA_WINDOW = 128
SWA_SCALE = SWA_HEAD_DIM ** -0.5
N_MOD = 6
N_EXPERTS = 64
N_EXPERT_GROUPS = 8
TOPK_GROUPS = 4
TOP_K = 6
EXPERT_FF = 256
SHARED_FF = 256
ROUTED_SCALE = 2.5

LANES = 128
TM = 256
N_CTX = BATCH * SEQ
N_LAT = DEC_BATCH * DEC_SEQ
N_TOK = N_CTX + N_LAT
N_CACHE = DEC_BATCH * PAST_LEN
NT_CTX = N_CTX // TM
NT_LAT = N_LAT // TM
NT = N_TOK // TM
LAT_TILES = DEC_SEQ // TM
TE = 256
S_MAX = N_TOK * TOP_K + N_EXPERTS * TE
NTE = S_MAX // TE
VMEM_LIMIT = 56 * 1024 * 1024
PACKED = D_MODEL // 4
SC_ROWS = 128

A_F = (0, 512)
A_QD = (512, 896)
A_KV = (896, 1024)
A_SQ = (1024, 1536)
A_SQS = (1536, 2048)
A_SK = (2048, 2176)
A_SKS = (2176, 2304)
A_SV = (2304, 2432)
A_KR = (2432, 2560)
A_KRS = (2560, 2688)
W_IN_WIDE = 2688
TAB_W = 1280

F32 = jnp.float32
BF16 = jnp.bfloat16


def _cparams(n_axes, parallel=False):
    sem = ("parallel" if parallel else "arbitrary",) * n_axes
    return pltpu.CompilerParams(dimension_semantics=sem, vmem_limit_bytes=VMEM_LIMIT)


def _dot(a, b):
    return jnp.dot(a, b, preferred_element_type=F32)


def _dot_nt(a, b):
    return lax.dot_general(a, b, (((1,), (1,)), ((), ())), preferred_element_type=F32)


def _rms_rows(v, g):
    return v * lax.rsqrt(jnp.mean(v * v, axis=-1, keepdims=True) + EPS) * g


def _pack_rows(v):
    half = v.shape[1] // 2
    lo = lax.bitcast_convert_type(v[:, :half].astype(BF16).astype(F32), jnp.int32)
    hi = lax.bitcast_convert_type(v[:, half:].astype(BF16).astype(F32), jnp.int32)
    return jnp.bitwise_or(jnp.bitwise_and(hi, -65536), jnp.bitwise_and(jnp.right_shift(lo, 16), 65535))


def _unpack_rows(w):
    lo = lax.bitcast_convert_type(jnp.left_shift(w, 16), F32)
    hi = lax.bitcast_convert_type(jnp.bitwise_and(w, -65536), F32)
    return jnp.concatenate([lo, hi], axis=1)


def _pack_pair(v):
    half = v.shape[1] // 2
    return _pack_rows(v[:, :half]), _pack_rows(v[:, half:])


def _unpack_pair(a, b):
    return jnp.concatenate([_unpack_rows(a), _unpack_rows(b)], axis=1)


def _const_spec(shape):
    return pl.BlockSpec(shape, lambda *_: (0,) * len(shape))


def _tab_row_block(i):
    return jnp.where(i < NT_CTX, 0, 1 + (i - NT_CTX) % LAT_TILES)


def _mod_kernel(cond_ref, w_ref, b_ref, o_ref):
    c = cond_ref[...]
    a = (c * jax.nn.sigmoid(c)).astype(BF16)
    o_ref[...] = _dot(a, w_ref[...].astype(BF16)) + b_ref[...]


def _modulation(cond8, ada_w, ada_b):
    tn = 512
    nj = N_MOD * D_MODEL // tn
    return pl.pallas_call(
        _mod_kernel,
        grid=(DEPTH, nj),
        in_specs=[
            pl.BlockSpec((8, D_MODEL), lambda l, j: (0, 0)),
            pl.BlockSpec((None, D_MODEL, tn), lambda l, j: (l, 0, j)),
            pl.BlockSpec((None, 1, tn), lambda l, j: (l, 0, j)),
        ],
        out_specs=pl.BlockSpec((None, 8, tn), lambda l, j: (l, 0, j)),
        out_shape=jax.ShapeDtypeStruct((DEPTH, 8, N_MOD * D_MODEL), F32),
        compiler_params=_cparams(2),
        name="modulation",
    )(cond8, ada_w, ada_b.reshape(DEPTH, 1, N_MOD * D_MODEL))


def _stage_a_kernel(x_ref, mod_ref, g_ref, win_ref, wg_ref, bg_ref, qn_ref, kvn_ref, tab_ref,
                    fin_ref, cq_ref, ckv_ref, kr_ref, sq_ref, sk_ref, sv_ref, gates_ref):
    x = x_ref[...]
    h = (_rms_rows(x, g_ref[...]) * (1.0 + mod_ref[:, 1024:2048]) + mod_ref[:, 0:1024]).astype(BF16)

    def proj(seg):
        return _dot(h, win_ref[:, seg[0]:seg[1]])

    fin_ref[...] = proj(A_F).astype(BF16)
    cq_ref[...] = _rms_rows(proj(A_QD), qn_ref[...]).astype(BF16)
    ckv_ref[...] = _rms_rows(proj(A_KV), kvn_ref[...])
    cos64 = tab_ref[:, 0:512]
    sin64 = tab_ref[:, 512:1024]
    sq = proj(A_SQ) * cos64 + proj(A_SQS) * sin64
    sq_ref[...] = (sq * SWA_SCALE).astype(BF16)
    sk_ref[...] = proj(A_SK) * cos64[:, 0:128] + proj(A_SKS) * sin64[:, 0:128]
    sv_ref[...] = proj(A_SV)
    kr_ref[...] = proj(A_KR) * tab_ref[:, 1024:1152] + proj(A_KRS) * tab_ref[:, 1152:1280]
    for c in range(3):
        lo, hi = c * D_MODEL, (c + 1) * D_MODEL
        gates_ref[:, lo:hi] = jax.nn.sigmoid(_dot(h, wg_ref[:, lo:hi]) + bg_ref[:, lo:hi]).astype(BF16)


def _stage_a(x, modt, g0, w_in_wide, w_gate, b_gate, q_norm, kv_norm, tab):
    row = lambda w: pl.BlockSpec((TM, w), lambda i: (i, 0))
    outs = [(512, BF16), (MLA_Q_RANK, BF16), (128, F32), (128, F32), (512, BF16), (128, F32),
            (128, F32), (3 * D_MODEL, BF16)]
    return pl.pallas_call(
        _stage_a_kernel,
        grid=(NT,),
        in_specs=[
            row(D_MODEL),
            pl.BlockSpec((None, 1, N_MOD * D_MODEL), lambda i: (i, 0, 0)),
            _const_spec((1, D_MODEL)),
            _const_spec((D_MODEL, W_IN_WIDE)),
            _const_spec((D_MODEL, 3 * D_MODEL)),
            _const_spec((1, 3 * D_MODEL)),
            _const_spec((1, MLA_Q_RANK)),
            _const_spec((1, MLA_KV_RANK)),
            pl.BlockSpec((TM, TAB_W), lambda i: (_tab_row_block(i), 0)),
        ],
        out_specs=[row(w) for w, _ in outs],
        out_shape=[jax.ShapeDtypeStruct((N_TOK, w), dt) for w, dt in outs],
        compiler_params=_cparams(1),
        name="stage_a",
    )(x, modt, g0, w_in_wide, w_gate, b_gate, q_norm, kv_norm, tab)


def _fnet_kernel(t_len, scale, fin_ref, f_ref, bd_ref, o_ref, zz_ref):
    @pl.when(pl.program_id(1) == 0)
    def _():
        z = fin_ref[...]
        zz_ref[0:t_len, :] = _dot(z, bd_ref[:, 0:512]).astype(BF16)
        zz_ref[t_len:2 * t_len, :] = _dot(z, bd_ref[:, 512:1024]).astype(BF16)

    o_ref[...] = (_dot(f_ref[...], zz_ref[...]) * scale).astype(BF16)


def _fnet(fin, fmat, bd, n_batch, t_len, row_block0):
    scale = 1.0 / math.sqrt(t_len * FNET_GROUP_DIM)
    return pl.pallas_call(
        functools.partial(_fnet_kernel, t_len, scale),
        grid=(n_batch, t_len // TM),
        in_specs=[
            pl.BlockSpec((t_len, FNET_WIDTH), lambda b, i: (row_block0 + b, 0)),
            pl.BlockSpec((TM, 2 * t_len), lambda b, i: (i, 0)),
            _const_spec((FNET_WIDTH, 2 * FNET_WIDTH)),
        ],
        out_specs=pl.BlockSpec((TM, FNET_WIDTH), lambda b, i: (b * (t_len // TM) + i, 0)),
        out_shape=jax.ShapeDtypeStruct((n_batch * t_len, FNET_WIDTH), BF16),
        scratch_shapes=[pltpu.VMEM((2 * t_len, FNET_WIDTH), BF16)],
        compiler_params=_cparams(2),
        name=f"fnet_{t_len}",
    )(fin, fmat, bd)


def _mla_prep_kernel(cq_ref, ckv_ref, kr_ref, tab_ref, wqa_ref, wqb_ref, wk_ref, e_ref, wv_ref,
                     q_ref, k_ref, v_ref):
    cq = cq_ref[...]
    ckv = ckv_ref[...].astype(BF16)
    kr = kr_ref[...].astype(BF16)
    lane = lax.broadcasted_iota(jnp.int32, (1, LANES), 1)
    rope_lane = jnp.logical_and(lane >= MLA_NOPE, lane < MLA_NOPE + MLA_ROPE)
    cos_h = jnp.where(rope_lane, tab_ref[:, 0:128], 1.0)
    sin_h = tab_ref[:, 128:256]
    for hd in range(MLA_HEADS):
        lo, hi = hd * LANES, (hd + 1) * LANES
        q = _dot(cq, wqa_ref[:, lo:hi]) * cos_h + _dot(cq, wqb_ref[:, lo:hi]) * sin_h
        q_ref[:, lo:hi] = (q * MLA_SCALE).astype(BF16)
    k_ref[...] = (_dot(ckv, wk_ref[...]) + _dot(kr, e_ref[...])).astype(BF16)
    v_ref[...] = _dot(ckv, wv_ref[...]).astype(BF16)


def _mla_prep(cq_all, ckv_all, kr_all, tab, wqa, wqb, wk, e_mat, wv):
    n_rows = cq_all.shape[0]
    nt = n_rows // TM
    row = lambda w: pl.BlockSpec((TM, w), lambda i: (i, 0))

    def tab_map(i):
        return (jnp.where(i < NT, _tab_row_block(i), 0), 4)

    return pl.pallas_call(
        _mla_prep_kernel,
        grid=(nt,),
        in_specs=[
            row(MLA_Q_RANK), row(128), row(128),
            pl.BlockSpec((TM, 256), tab_map),
            _const_spec((MLA_Q_RANK, 1024)), _const_spec((MLA_Q_RANK, 1024)),
            _const_spec((128, 1024)), _const_spec((128, 1024)), _const_spec((128, 512)),
        ],
        out_specs=[row(1024), row(1024), row(512)],
        out_shape=[jax.ShapeDtypeStruct((n_rows, 1024), BF16),
                   jax.ShapeDtypeStruct((n_rows, 1024), BF16),
                   jax.ShapeDtypeStruct((n_rows, 512), BF16)],
        compiler_params=_cparams(1),
        name="mla_prep",
    )(cq_all, ckv_all, kr_all, tab, wqa, wqb, wk, e_mat, wv)


def _mla_attn_kernel(n_seg, q_ref, *refs):
    k_refs = refs[0:n_seg]
    v_refs = refs[n_seg:2 * n_seg]
    o_ref = refs[2 * n_seg]
    lane = lax.broadcasted_iota(jnp.int32, (1, LANES), 1)
    low = lane < MLA_V
    acc = None
    for hh in range(2):
        q = q_ref[:, hh * LANES:(hh + 1) * LANES]
        ss = [_dot_nt(q, k[:, hh * LANES:(hh + 1) * LANES]) for k in k_refs]
        m = functools.reduce(jnp.maximum, [s.max(axis=-1, keepdims=True) for s in ss])
        den = None
        o = None
        keep = low if hh == 0 else jnp.logical_not(low)
        for s, v_ref in zip(ss, v_refs):
            p = jnp.exp(s - m)
            ps = p.sum(axis=-1, keepdims=True)
            vm = jnp.where(keep, v_ref[...], jnp.zeros((), BF16))
            po = _dot(p.astype(BF16), vm)
            den = ps if den is None else den + ps
            o = po if o is None else o + po
        o = o / den
        acc = o if acc is None else acc + o
    o_ref[...] = acc.astype(BF16)


def _mla_attn(q_all, k_all, v_all, latent):
    if latent:
        n_b, n_q = DEC_BATCH, DEC_SEQ // TM
        q0 = NT_CTX
        kv_specs = [
            pl.BlockSpec((PAST_LEN, 256), lambda b, hp, i: (N_TOK // PAST_LEN + b, hp)),
            pl.BlockSpec((DEC_SEQ, 256), lambda b, hp, i: (N_CTX // DEC_SEQ + b, hp)),
            pl.BlockSpec((PAST_LEN, 128), lambda b, hp, i: (N_TOK // PAST_LEN + b, hp)),
            pl.BlockSpec((DEC_SEQ, 128), lambda b, hp, i: (N_CTX // DEC_SEQ + b, hp)),
        ]
        args = (q_all, k_all, k_all, v_all, v_all)
        n_seg = 2
    else:
        n_b, n_q = BATCH, 1
        q0 = 0
        kv_specs = [
            pl.BlockSpec((SEQ, 256), lambda b, hp, i: (b, hp)),
            pl.BlockSpec((SEQ, 128), lambda b, hp, i: (b, hp)),
        ]
        args = (q_all, k_all, v_all)
        n_seg = 1
    return pl.pallas_call(
        functools.partial(_mla_attn_kernel, n_seg),
        grid=(n_b, MLA_HEADS // 2, n_q),
        in_specs=[pl.BlockSpec((TM, 256), lambda b, hp, i: (q0 + b * n_q + i, hp))] + kv_specs,
        out_specs=pl.BlockSpec((TM, 128), lambda b, hp, i: (b * n_q + i, hp)),
        out_shape=jax.ShapeDtypeStruct((n_b * n_q * TM, MLA_HEADS * MLA_V), BF16),
        compiler_params=_cparams(3),
        name="mla_attn_lat" if latent else "mla_attn_ctx",
    )(*args)


def _swa_kernel(windowed, n_qb, sink_ref, q_ref, *refs):
    n_seg = 4 if windowed else 1
    k_refs = refs[0:n_seg]
    v_refs = refs[n_seg:2 * n_seg]
    o_ref = refs[2 * n_seg]
    tq = q_ref.shape[0]
    qb = pl.program_id(1)
    lane = lax.broadcasted_iota(jnp.int32, (1, LANES), 1)
    low = lane < SWA_HEAD_DIM
    high = jnp.logical_not(low)

    ks = [r[...] for r in k_refs]
    vs = [r[...] for r in v_refs]
    ks_sw = [pltpu.roll(a, SWA_HEAD_DIM, 1) for a in ks]
    vs_sw = [pltpu.roll(a, SWA_HEAD_DIM, 1) for a in vs]

    if windowed:
        qi = lax.broadcasted_iota(jnp.int32, (2 * tq, SWA_WINDOW), 0) % tq
        kj = lax.broadcasted_iota(jnp.int32, (2 * tq, SWA_WINDOW), 1)
        masks = [None,
                 jnp.logical_and(kj >= qi, qb > 0),
                 None,
                 jnp.logical_and(kj <= qi, qb < n_qb - 1)]
    else:
        masks = [None]
    top_rows = lax.broadcasted_iota(jnp.int32, (2 * tq, 1), 0) < tq

    for g in range(SWA_KV_HEADS):
        qs = jnp.concatenate([q_ref[:, 256 * g:256 * g + 128],
                              q_ref[:, 256 * g + 128:256 * g + 256]], axis=0)
        out = None
        for half in range(2):
            keep = low if half == 0 else high
            straight = (g == half)
            kh = [jnp.where(keep, a if straight else b, 0.0).astype(BF16) for a, b in zip(ks, ks_sw)]
            vh = [jnp.where(keep, a if straight else b, 0.0).astype(BF16) for a, b in zip(vs, vs_sw)]
            ss = []
            for kk, mk in zip(kh, masks):
                s = _dot_nt(qs, kk)
                if mk is not None:
                    s = jnp.where(mk, s, NEG_INF)
                ss.append(s)
            sink = jnp.where(top_rows, sink_ref[4 * g + half], sink_ref[4 * g + 2 + half])
            m = functools.reduce(jnp.maximum, [s.max(axis=-1, keepdims=True) for s in ss])
            m = jnp.maximum(m, sink)
            den = jnp.exp(sink - m)
            o = None
            for s, vv in zip(ss, vh):
                p = jnp.exp(s - m)
                den = den + p.sum(axis=-1, keepdims=True)
                po = _dot(p.astype(BF16), vv)
                o = po if o is None else o + po
            o = o / den
            out = o if out is None else out + o
        o_ref[:, 256 * g:256 * g + 128] = out[0:tq].astype(BF16)
        o_ref[:, 256 * g + 128:256 * g + 256] = out[tq:2 * tq].astype(BF16)


def _swa_attn(sink, sq, sk, sv, cache_k, cache_v, latent):
    smem = pl.BlockSpec(memory_space=pltpu.SMEM)
    if latent:
        tq = SWA_WINDOW
        n_b, n_qb = DEC_BATCH, DEC_SEQ // tq
        base = N_CTX // tq

        def prev(b, i):
            return (base + b * n_qb + jnp.maximum(i - 1, 0), 0)

        def cur(b, i):
            return (base + b * n_qb + i, 0)

        def nxt(b, i):
            return (base + b * n_qb + jnp.minimum(i + 1, n_qb - 1), 0)

        cache = pl.BlockSpec((None, PAST_LEN, 128), lambda b, i: (b, 0, 0))
        blk = lambda f: pl.BlockSpec((tq, 128), f)
        kv_specs = [cache, blk(prev), blk(cur), blk(nxt)] * 2
        args = (cache_k, sk, sk, sk, cache_v, sv, sv, sv)
        q_spec = pl.BlockSpec((tq, 512), cur)
        o_spec = pl.BlockSpec((tq, 512), lambda b, i: (b * n_qb + i, 0))
    else:
        tq = SEQ
        n_b, n_qb = BATCH, 1
        blk = pl.BlockSpec((tq, 128), lambda b, i: (b, 0))
        kv_specs = [blk, blk]
        args = (sk, sv)
        q_spec = pl.BlockSpec((tq, 512), lambda b, i: (b, 0))
        o_spec = q_spec
    return pl.pallas_call(
        functools.partial(_swa_kernel, latent, n_qb),
        grid=(n_b, n_qb),
        in_specs=[smem, q_spec] + kv_specs,
        out_specs=o_spec,
        out_shape=jax.ShapeDtypeStruct((n_b * n_qb * tq, 512), BF16),
        compiler_params=_cparams(2),
        name="swa_lat" if latent else "swa_ctx",
    )(sink, sq, *args)


def _stage_e_kernel(x_ref, mod_ref, g1_ref, g2_ref, fn_ref, om_ref, os_ref, gates_ref,
                    wf_ref, wm_ref, ws_ref, wo_ref, x1_ref, h2a_ref, h2b_ref):
    merged = (gates_ref[:, 0:1024].astype(F32) * _dot(fn_ref[...], wf_ref[...])
              + gates_ref[:, 1024:2048].astype(F32) * _dot(om_ref[...], wm_ref[...])
              + gates_ref[:, 2048:3072].astype(F32) * _dot(os_ref[...], ws_ref[...]))
    mix = _dot(merged.astype(BF16), wo_ref[...])
    x1 = x_ref[...] + mod_ref[:, 2048:3072] * _rms_rows(mix, g1_ref[...])
    x1_ref[...] = x1
    h2 = _rms_rows(x1, g2_ref[...]) * (1.0 + mod_ref[:, 4096:5120]) + mod_ref[:, 3072:4096]
    h2a_ref[...], h2b_ref[...] = _pack_pair(h2)


def _stage_e(x, modt, g1, g2, fn, om, osw, gates, wf, wm, ws, wo):
    row = lambda w: pl.BlockSpec((TM, w), lambda i: (i, 0))
    return pl.pallas_call(
        _stage_e_kernel,
        grid=(NT,),
        in_specs=[
            row(D_MODEL),
            pl.BlockSpec((None, 1, N_MOD * D_MODEL), lambda i: (i, 0, 0)),
            _const_spec((1, D_MODEL)), _const_spec((1, D_MODEL)),
            row(512), row(512), row(512), row(3 * D_MODEL),
            _const_spec((512, D_MODEL)), _const_spec((512, D_MODEL)), _const_spec((512, D_MODEL)),
            _const_spec((D_MODEL, D_MODEL)),
        ],
        out_specs=[row(D_MODEL), row(PACKED), row(PACKED)],
        out_shape=[jax.ShapeDtypeStruct((N_TOK, D_MODEL), F32),
                   jax.ShapeDtypeStruct((N_TOK, PACKED), jnp.int32),
                   jax.ShapeDtypeStruct((N_TOK, PACKED), jnp.int32)],
        compiler_params=_cparams(1),
        name="stage_e",
    )(x, modt, g1, g2, fn, om, osw, gates, wf, wm, ws, wo)


def _router_kernel(ha_ref, hb_ref, rwt_ref, rb_ref, s1_ref, s3_ref, s2_ref, tri_ref,
                   shared_ref, eidx_ref, ew_ref, epos_ref, cnt_ref, carry_ref):
    @pl.when(pl.program_id(0) == 0)
    def _():
        carry_ref[...] = jnp.zeros_like(carry_ref)

    h = _unpack_pair(ha_ref[...], hb_ref[...]).astype(BF16)
    act = jax.nn.silu(_dot(h, s1_ref[...])) * _dot(h, s3_ref[...])
    shared_ref[...] = _dot(act.astype(BF16), s2_ref[...])

    gsz = N_EXPERTS // N_EXPERT_GROUPS
    scores = jax.nn.sigmoid(_dot_nt(rwt_ref[...], h))
    biased = scores + rb_ref[...]
    mem = lax.broadcasted_iota(jnp.int32, (gsz, TM), 0).astype(F32)
    gs_rows = []
    for g in range(N_EXPERT_GROUPS):
        bg = biased[g * gsz:(g + 1) * gsz, :]
        m1 = bg.max(axis=0, keepdims=True)
        first = jnp.min(jnp.where(bg == m1, mem, float(gsz)), axis=0, keepdims=True)
        m2 = jnp.where(mem == first, -jnp.inf, bg).max(axis=0, keepdims=True)
        gs_rows.append(m1 + m2)
    gs = jnp.concatenate(gs_rows, axis=0)
    gid = lax.broadcasted_iota(jnp.int32, gs.shape, 0).astype(F32)
    gsel = jnp.zeros(gs.shape, F32)
    for _ in range(TOPK_GROUPS):
        mx = gs.max(axis=0, keepdims=True)
        pick = gid == jnp.min(jnp.where(gs == mx, gid, float(N_EXPERT_GROUPS)), axis=0, keepdims=True)
        gsel = jnp.where(pick, 1.0, gsel)
        gs = jnp.where(pick, -jnp.inf, gs)
    emask = jnp.concatenate(
        [jnp.broadcast_to(gsel[g:g + 1, :], (gsz, TM)) for g in range(N_EXPERT_GROUPS)], axis=0)
    cand = jnp.where(emask > 0.5, biased, NEG_INF)
    eid = lax.broadcasted_iota(jnp.int32, cand.shape, 0).astype(F32)
    picks = []
    self32 = jnp.zeros(cand.shape, F32)
    for _ in range(TOP_K):
        mx = cand.max(axis=0, keepdims=True)
        pick = eid == jnp.min(jnp.where(cand == mx, eid, float(N_EXPERTS)), axis=0, keepdims=True)
        picks.append(pick)
        self32 = jnp.where(pick, 1.0, self32)
        cand = jnp.where(pick, -jnp.inf, cand)
    pos = _dot(self32.astype(BF16), tri_ref[...]) + carry_ref[...]
    sel_scores = [jnp.sum(jnp.where(p, scores, 0.0), axis=0, keepdims=True) for p in picks]
    wsum = functools.reduce(lambda a, b: a + b, sel_scores)
    zero_f = jnp.zeros((2, TM), F32)
    eidx = [jnp.sum(jnp.where(p, eid, 0.0), axis=0, keepdims=True) for p in picks]
    epos = [jnp.sum(jnp.where(p, pos, 0.0), axis=0, keepdims=True) for p in picks]
    ew = [s / wsum * ROUTED_SCALE for s in sel_scores]
    eidx_ref[...] = jnp.concatenate(eidx + [zero_f], axis=0).astype(jnp.int32)
    epos_ref[...] = jnp.concatenate(epos + [zero_f], axis=0).astype(jnp.int32)
    ew_ref[...] = jnp.concatenate(ew + [zero_f], axis=0)
    total = carry_ref[...] + jnp.sum(self32, axis=1, keepdims=True)
    carry_ref[...] = total
    cnt_ref[...] = jnp.broadcast_to(total, cnt_ref.shape).astype(jnp.int32)


def _router(h2a, h2b, rwt, rbias, s1, s3, s2, tri):
    col = lambda dt: (pl.BlockSpec((8, TM), lambda i: (0, i)), jax.ShapeDtypeStruct((8, N_TOK), dt))
    specs = [col(jnp.int32), col(F32), col(jnp.int32)]
    return pl.pallas_call(
        _router_kernel,
        grid=(NT,),
        in_specs=[
            pl.BlockSpec((TM, PACKED), lambda i: (i, 0)),
            pl.BlockSpec((TM, PACKED), lambda i: (i, 0)),
            _const_spec((N_EXPERTS, D_MODEL)), _const_spec((N_EXPERTS, 1)),
            _const_spec((D_MODEL, SHARED_FF)), _const_spec((D_MODEL, SHARED_FF)),
            _const_spec((SHARED_FF, D_MODEL)), _const_spec((TM, TM)),
        ],
        out_specs=[pl.BlockSpec((TM, D_MODEL), lambda i: (i, 0))] + [s for s, _ in specs]
        + [_const_spec((N_EXPERTS, LANES))],
        out_shape=[jax.ShapeDtypeStruct((N_TOK, D_MODEL), F32)] + [s for _, s in specs]
        + [jax.ShapeDtypeStruct((N_EXPERTS, LANES), jnp.int32)],
        scratch_shapes=[pltpu.VMEM((N_EXPERTS, 1), F32)],
        compiler_params=_cparams(1),
        name="router_shared",
    )(h2a, h2b, rwt, rbias, s1, s3, s2, tri)


def _expert_kernel(te_ref, tv_ref, xa_ref, xb_ref, w1_ref, w3_ref, w2_ref, oa_ref, ob_ref):
    j = pl.program_id(0)

    @pl.when(tv_ref[j] == 1)
    def _():
        x = _unpack_pair(xa_ref[...], xb_ref[...]).astype(BF16)
        hg = _dot(x, w1_ref[...].astype(BF16))
        hu = _dot(x, w3_ref[...].astype(BF16))
        act = (jax.nn.silu(hg) * hu).astype(BF16)
        oa_ref[...], ob_ref[...] = _pack_pair(_dot(act, w2_ref[...].astype(BF16)))

    @pl.when(tv_ref[j] == 0)
    def _():
        oa_ref[...] = jnp.zeros_like(oa_ref)
        ob_ref[...] = jnp.zeros_like(ob_ref)


def _experts(layer, tile_expert, tile_valid, xsa, xsb, w1, w3, w2):
    slot_rows = pl.BlockSpec((TE, PACKED), lambda j, te, tv: (j, 0))
    grid_spec = pltpu.PrefetchScalarGridSpec(
        num_scalar_prefetch=2,
        grid=(NTE,),
        in_specs=[
            slot_rows, slot_rows,
            pl.BlockSpec((None, None, D_MODEL, EXPERT_FF), lambda j, te, tv: (layer, te[j], 0, 0)),
            pl.BlockSpec((None, None, D_MODEL, EXPERT_FF), lambda j, te, tv: (layer, te[j], 0, 0)),
            pl.BlockSpec((None, None, EXPERT_FF, D_MODEL), lambda j, te, tv: (layer, te[j], 0, 0)),
        ],
        out_specs=[slot_rows, slot_rows],
    )
    return pl.pallas_call(
        _expert_kernel,
        grid_spec=grid_spec,
        out_shape=[jax.ShapeDtypeStruct((S_MAX, PACKED), jnp.int32)] * 2,
        compiler_params=_cparams(1),
        name="experts",
    )(tile_expert, tile_valid, xsa, xsb, w1, w3, w2)


def _sc_mesh():
    return plsc.VectorSubcoreMesh(core_axis_name="c", subcore_axis_name="s")


def _sc_scatter_rows(rows, slot8):
    @functools.partial(pl.kernel, mesh=_sc_mesh(), scratch_types=[],
                       out_type=jax.ShapeDtypeStruct((S_MAX, PACKED), jnp.int32))
    def scatter(x_hbm, i_hbm, o_hbm):
        def body(x_vmem, i_vmem):
            for k in range(TOP_K):
                pltpu.sync_copy(x_vmem, o_hbm.at[i_vmem.at[k]])

        pltpu.emit_pipeline(
            body,
            grid=(N_TOK // SC_ROWS,),
            in_specs=[pl.BlockSpec((SC_ROWS, PACKED), lambda i: (i, 0)),
                      pl.BlockSpec((8, SC_ROWS), lambda i: (0, i))],
            out_specs=[],
            core_axis_name=("c", "s"),
            dimension_semantics=(pltpu.PARALLEL,),
        )(x_hbm, i_hbm)

    return scatter(rows, slot8)


def _sc_gather_rows(table, idx):
    n = idx.shape[1]

    @functools.partial(pl.kernel, mesh=_sc_mesh(), scratch_types=[],
                       out_type=jax.ShapeDtypeStruct((n, PACKED), jnp.int32))
    def gather(t_hbm, i_hbm, o_hbm):
        def body(i_vmem, o_vmem):
            pltpu.sync_copy(t_hbm.at[i_vmem.at[0]], o_vmem)

        pltpu.emit_pipeline(
            body,
            grid=(n // SC_ROWS,),
            in_specs=[pl.BlockSpec((1, SC_ROWS), lambda i: (0, i))],
            out_specs=[pl.BlockSpec((SC_ROWS, PACKED), lambda i: (i, 0))],
            core_axis_name=("c", "s"),
            dimension_semantics=(pltpu.PARALLEL,),
        )(i_hbm, o_hbm)

    return gather(table, idx)


def _stage_g_kernel(x1_ref, mod_ref, g3_ref, yga_ref, ygb_ref, ew_ref, shared_ref, o_ref):
    y = shared_ref[...]
    for k in range(TOP_K):
        y = y + ew_ref[:, k:k + 1] * _unpack_pair(yga_ref[k], ygb_ref[k])
    o_ref[...] = x1_ref[...] + mod_ref[:, 5120:6144] * _rms_rows(y, g3_ref[...])


def _stage_g(x1, modt, g3, yga, ygb, ew_rows, shared):
    row = pl.BlockSpec((TM, D_MODEL), lambda i: (i, 0))
    picked = pl.BlockSpec((TOP_K, TM, PACKED), lambda i: (0, i, 0))
    return pl.pallas_call(
        _stage_g_kernel,
        grid=(NT,),
        in_specs=[row, pl.BlockSpec((None, 1, N_MOD * D_MODEL), lambda i: (i, 0, 0)),
                  _const_spec((1, D_MODEL)), picked, picked,
                  pl.BlockSpec((TM, 8), lambda i: (i, 0)), row],
        out_specs=row,
        out_shape=jax.ShapeDtypeStruct((N_TOK, D_MODEL), F32),
        compiler_params=_cparams(1),
        name="stage_g",
    )(x1, modt, g3, yga, ygb, ew_rows, shared)


def _rope_tables():
    t = jnp.arange(DEC_SEQ)
    pos = jnp.stack([(t // GRID_W).astype(F32), (t % GRID_W).astype(F32)], axis=-1)

    def table(r):
        n_freq = r // 4
        inv = ROPE_BASE ** (-jnp.arange(n_freq, dtype=F32) / n_freq)
        ang = pos[:, :, None] * inv
        cos = jnp.cos(ang)
        sin = jnp.sin(ang)
        cos_t = jnp.stack([cos, cos], axis=2).reshape(DEC_SEQ, r)
        sin_t = jnp.stack([-sin, sin], axis=2).reshape(DEC_SEQ, r)
        return cos_t, sin_t

    c64, s64 = table(SWA_HEAD_DIM)
    c32, s32 = table(MLA_ROPE)
    lat = jnp.concatenate([jnp.tile(c64, (1, 8)), jnp.tile(s64, (1, 8)),
                           jnp.tile(c32, (1, 4)), jnp.tile(s32, (1, 4))], axis=1)
    ident = jnp.concatenate([jnp.ones((TM, 512), F32), jnp.zeros((TM, 512), F32),
                             jnp.ones((TM, 128), F32), jnp.zeros((TM, 128), F32)], axis=1)
    return jnp.concatenate([ident, lat], axis=0)


def _dft_pair(n):
    k = jnp.arange(n, dtype=jnp.int32)
    ang = ((k[:, None] * k[None, :]) % n).astype(F32) * (2.0 * math.pi / n)
    return jnp.cos(ang), jnp.sin(ang)


def _fnet_tables():
    c64, s64 = _dft_pair(FNET_GROUP_DIM)
    eye = jnp.eye(FNET_GROUPS, dtype=F32)
    bd = jnp.concatenate([jnp.kron(eye, c64), jnp.kron(eye, s64)], axis=1).astype(BF16)
    mats = []
    for t_len in (SEQ, DEC_SEQ):
        c, s = _dft_pair(t_len)
        mats.append(jnp.concatenate([c, -s], axis=1).astype(BF16))
    return bd, mats[0], mats[1]


def _swap_perm(width, half):
    return np.arange(width) ^ half


def _layer_weights(l, w_in, w_uq, w_ukv):
    w = w_in[l]
    kr = w[:, 1024:1056]
    sq = w[:, 1056:1568]
    sk = w[:, 1568:1696]
    pad96 = jnp.zeros((D_MODEL, 96), F32)
    wide = jnp.concatenate([
        w[:, 0:1024], sq, sq[:, _swap_perm(512, 16)], sk, sk[:, _swap_perm(128, 16)],
        w[:, 1696:1824], kr, pad96, kr[:, _swap_perm(32, 8)], pad96], axis=1).astype(BF16)

    uq = w_uq[l].reshape(MLA_Q_RANK, MLA_HEADS, MLA_NOPE + MLA_ROPE)
    rope_w = uq[:, :, MLA_NOPE:]
    z32 = jnp.zeros((MLA_Q_RANK, MLA_HEADS, 32), F32)
    z64 = jnp.zeros((MLA_Q_RANK, MLA_HEADS, 64), F32)
    wqa = jnp.concatenate([uq, z32], axis=2).reshape(MLA_Q_RANK, 1024).astype(BF16)
    wqb = jnp.concatenate([z64, rope_w[:, :, _swap_perm(32, 8)], z32], axis=2)
    wqb = wqb.reshape(MLA_Q_RANK, 1024).astype(BF16)
    ukv = w_ukv[l].reshape(MLA_KV_RANK, MLA_HEADS, MLA_NOPE + MLA_V)
    wk = jnp.concatenate([ukv[:, :, :MLA_NOPE], jnp.zeros((MLA_KV_RANK, MLA_HEADS, 64), F32)],
                         axis=2).reshape(MLA_KV_RANK, 1024).astype(BF16)
    wv = ukv[:, :, MLA_NOPE:].reshape(MLA_KV_RANK, 512).astype(BF16)
    return wide, wqa, wqb, wk, wv


def _rope_placement():
    e = np.zeros((128, 1024), np.float32)
    for hd in range(MLA_HEADS):
        for i in range(MLA_ROPE):
            e[i, hd * 128 + MLA_NOPE + i] = 1.0
    return jnp.asarray(e, BF16)


def _moe_dispatch_plan(eidx, epos, counts):
    padded = ((counts + TE - 1) // TE) * TE
    ends = jnp.cumsum(padded)
    offs = ends - padded
    slot = jnp.take(offs, eidx, axis=0) + epos
    starts = jnp.arange(NTE, dtype=jnp.int32) * TE
    tile_expert = jnp.sum((ends[None, :] <= starts[:, None]).astype(jnp.int32), axis=1)
    tile_expert = jnp.minimum(tile_expert, N_EXPERTS - 1)
    tile_valid = (starts < ends[-1]).astype(jnp.int32)
    return slot, tile_expert, tile_valid


def kernel(x_prompt, x_sample, cache_mla_ckv, cache_mla_krope, cache_swa_k, cache_swa_v, c, c_ctx,
           ada_w, ada_b, norm_g, w_in, q_norm, kv_norm, w_fnet, w_uq, w_ukv, w_mla_o, swa_sink,
           w_swa_o, w_gate, b_gate, w_out, router_w, router_bias, exp_w1, exp_w3, exp_w2,
           shared_w1, shared_w3, shared_w2):
    x = jnp.concatenate([x_prompt.reshape(N_CTX, D_MODEL), x_sample.reshape(N_LAT, D_MODEL)], axis=0)

    cond8 = jnp.concatenate([c_ctx[None, :], c, jnp.zeros((3, D_MODEL), F32)], axis=0)
    mod = _modulation(cond8, ada_w, ada_b)
    tile_cond = np.concatenate([np.zeros(NT_CTX, np.int32),
                                1 + np.arange(NT_LAT, dtype=np.int32) // LAT_TILES])

    tab = _rope_tables()
    bd, f_ctx, f_lat = _fnet_tables()
    e_mat = _rope_placement()
    tri = jnp.asarray(np.triu(np.ones((TM, TM), np.float32), 1), BF16)

    new_ckv, new_kr, new_k, new_v = [], [], [], []
    for l in range(DEPTH):
        modt = mod[l][tile_cond][:, None, :]
        wide, wqa, wqb, wk, wv = _layer_weights(l, w_in, w_uq, w_ukv)
        ng = norm_g[l]

        fin, cq, ckv, kr, sq, sk, sv, gates = _stage_a(
            x, modt, ng[0:1], wide, w_gate[l].astype(BF16), b_gate[l][None, :],
            q_norm[l][None, :], kv_norm[l][None, :], tab)

        new_ckv.append(ckv[:N_CTX].reshape(BATCH, SEQ, MLA_KV_RANK))
        new_kr.append(kr[:N_CTX, :MLA_ROPE].reshape(BATCH, SEQ, MLA_ROPE))
        new_k.append(sk[:N_CTX].reshape(BATCH, SEQ, SWA_KV_HEADS, SWA_HEAD_DIM))
        new_v.append(sv[:N_CTX].reshape(BATCH, SEQ, SWA_KV_HEADS, SWA_HEAD_DIM))

        fn = jnp.concatenate([_fnet(fin, f_ctx, bd, BATCH, SEQ, 0),
                              _fnet(fin, f_lat, bd, DEC_BATCH, DEC_SEQ, N_CTX // DEC_SEQ)], axis=0)

        cq_all = jnp.concatenate([cq, jnp.zeros((N_CACHE, MLA_Q_RANK), BF16)], axis=0)
        ckv_all = jnp.concatenate([ckv, cache_mla_ckv[:, l].reshape(N_CACHE, MLA_KV_RANK)], axis=0)
        kr_cache = jnp.pad(cache_mla_krope[:, l].reshape(N_CACHE, MLA_ROPE), ((0, 0), (0, 96)))
        kr_all = jnp.concatenate([kr, kr_cache], axis=0)
        q_m, k_m, v_m = _mla_prep(cq_all, ckv_all, kr_all, tab, wqa, wqb, wk, e_mat, wv)
        om = jnp.concatenate([_mla_attn(q_m, k_m, v_m, latent=False),
                              _mla_attn(q_m, k_m, v_m, latent=True)], axis=0)

        ck = cache_swa_k[:, l].reshape(DEC_BATCH, PAST_LEN, 128)
        cv = cache_swa_v[:, l].reshape(DEC_BATCH, PAST_LEN, 128)
        osw = jnp.concatenate([_swa_attn(swa_sink[l], sq, sk, sv, ck, cv, latent=False),
                               _swa_attn(swa_sink[l], sq, sk, sv, ck, cv, latent=True)], axis=0)

        x1, h2a, h2b = _stage_e(x, modt, ng[1:2], ng[2:3], fn, om, osw, gates,
                          w_fnet[l].astype(BF16), w_mla_o[l].astype(BF16),
                          w_swa_o[l].astype(BF16), w_out[l].astype(BF16))

        shared, eidx, ew, epos, counts = _router(
            h2a, h2b, router_w[l].T.astype(BF16), router_bias[l][:, None],
            shared_w1[l].astype(BF16), shared_w3[l].astype(BF16), shared_w2[l].astype(BF16), tri)
        slot, tile_expert, tile_valid = _moe_dispatch_plan(eidx, epos, counts[:, 0])
        xsa = _sc_scatter_rows(h2a, slot)
        xsb = _sc_scatter_rows(h2b, slot)
        ysa, ysb = _experts(l, tile_expert, tile_valid, xsa, xsb, exp_w1, exp_w3, exp_w2)
        picks = slot[:TOP_K].reshape(1, TOP_K * N_TOK)
        yga = _sc_gather_rows(ysa, picks).reshape(TOP_K, N_TOK, PACKED)
        ygb = _sc_gather_rows(ysb, picks).reshape(TOP_K, N_TOK, PACKED)
        x = _stage_g(x1, modt, ng[3:4], yga, ygb, ew.T, shared)

    y_p = x[:N_CTX].reshape(BATCH, SEQ, D_MODEL)
    y_s = x[N_CTX:].reshape(DEC_BATCH, DEC_SEQ, D_MODEL)
    return (y_p, y_s, jnp.stack(new_ckv, axis=1), jnp.stack(new_kr, axis=1),
            jnp.stack(new_k, axis=1), jnp.stack(new_v, axis=1))
```

```python
import functools
import math

import numpy as np
import jax
import jax.numpy as jnp
from jax import lax
from jax.experimental import pallas as pl
from jax.experimental.pallas import tpu as pltpu
from jax.experimental.pallas import tpu_sc as plsc

D_MODEL = 1024
BATCH = 16
SEQ = 256
DEPTH = 2
DEC_BATCH = 4
DEC_SEQ = 2048
PAST_LEN = 512
GRID_W = 64
EPS = 1e-6
ROPE_BASE = 10000.0
NEG_INF = -1e30

FNET_GROUPS = 8
FNET_GROUP_DIM = 64
FNET_WIDTH = 512
MLA_HEADS = 8
MLA_Q_RANK = 384
MLA_KV_RANK = 128
MLA_NOPE = 64
MLA_ROPE = 32
MLA_V = 64
MLA_SCALE = (MLA_NOPE + MLA_ROPE) ** -0.5
LOG2E = math.log2(math.e)
SWA_HEADS = 8
SWA_KV_HEADS = 2
SWA_HEAD_DIM = 64
SWA_WINDOW = 128
SWA_SCALE = SWA_HEAD_DIM ** -0.5
N_MOD = 6
N_EXPERTS = 64
N_EXPERT_GROUPS = 8
TOPK_GROUPS = 4
TOP_K = 6
EXPERT_FF = 256
SHARED_FF = 256
ROUTED_SCALE = 2.5

LANES = 128
TM = 256
N_CTX = BATCH * SEQ
N_LAT = DEC_BATCH * DEC_SEQ
N_TOK = N_CTX + N_LAT
N_CACHE = DEC_BATCH * PAST_LEN
NT_CTX = N_CTX // TM
NT_LAT = N_LAT // TM
NT = N_TOK // TM
LAT_TILES = DEC_SEQ // TM
TE = 256
S_MAX = N_TOK * TOP_K + N_EXPERTS * TE
NTE = S_MAX // TE
VMEM_LIMIT = 56 * 1024 * 1024
PACKED = D_MODEL // 4
SC_ROWS = 128
SC_CORES = 2
SC_SUBCORES = 16

A_F = (0, 512)
A_QD = (512, 896)
A_KV = (896, 1024)
A_SQ = (1024, 1536)
A_SQS = (1536, 2048)
A_SK = (2048, 2176)
A_SKS = (2176, 2304)
A_SV = (2304, 2432)
A_KR = (2432, 2560)
A_KRS = (2560, 2688)
W_IN_WIDE = 2688
TAB_W = 1280

F32 = jnp.float32
BF16 = jnp.bfloat16


def _cparams(n_axes, parallel=False):
    sem = ("parallel" if parallel else "arbitrary",) * n_axes
    return pltpu.CompilerParams(dimension_semantics=sem, vmem_limit_bytes=VMEM_LIMIT)


def _dot(a, b):
    return jnp.dot(a, b, preferred_element_type=F32)


def _dot_nt(a, b):
    return lax.dot_general(a, b, (((1,), (1,)), ((), ())), preferred_element_type=F32)


def _rms_rows(v, g):
    return v * lax.rsqrt(jnp.mean(v * v, axis=-1, keepdims=True) + EPS) * g


def _pack_rows(v):
    half = v.shape[1] // 2
    lo = lax.bitcast_convert_type(v[:, :half].astype(BF16).astype(F32), jnp.int32)
    hi = lax.bitcast_convert_type(v[:, half:].astype(BF16).astype(F32), jnp.int32)
    return jnp.bitwise_or(jnp.bitwise_and(hi, -65536), jnp.bitwise_and(jnp.right_shift(lo, 16), 65535))


def _unpack_rows(w):
    lo = lax.bitcast_convert_type(jnp.left_shift(w, 16), F32)
    hi = lax.bitcast_convert_type(jnp.bitwise_and(w, -65536), F32)
    return jnp.concatenate([lo, hi], axis=1)


def _pack_pair(v):
    half = v.shape[1] // 2
    return _pack_rows(v[:, :half]), _pack_rows(v[:, half:])


def _unpack_pair(a, b):
    return jnp.concatenate([_unpack_rows(a), _unpack_rows(b)], axis=1)


def _const_spec(shape):
    return pl.BlockSpec(shape, lambda *_: (0,) * len(shape))


def _tab_row_block(i):
    return jnp.where(i < NT_CTX, 0, 1 + (i - NT_CTX) % LAT_TILES)


def _mod_kernel(cond_ref, w_ref, b_ref, o_ref):
    c = cond_ref[...]
    a = (c * jax.nn.sigmoid(c)).astype(BF16)
    o_ref[...] = _dot(a, w_ref[...].astype(BF16)) + b_ref[...]


def _modulation(cond8, ada_w, ada_b):
    tn = 512
    nj = N_MOD * D_MODEL // tn
    return pl.pallas_call(
        _mod_kernel,
        grid=(DEPTH, nj),
        in_specs=[
            pl.BlockSpec((8, D_MODEL), lambda l, j: (0, 0)),
            pl.BlockSpec((None, D_MODEL, tn), lambda l, j: (l, 0, j)),
            pl.BlockSpec((None, 1, tn), lambda l, j: (l, 0, j)),
        ],
        out_specs=pl.BlockSpec((None, 8, tn), lambda l, j: (l, 0, j)),
        out_shape=jax.ShapeDtypeStruct((DEPTH, 8, N_MOD * D_MODEL), F32),
        compiler_params=_cparams(2),
        name="modulation",
    )(cond8, ada_w, ada_b.reshape(DEPTH, 1, N_MOD * D_MODEL))


def _mla_expand(cq, ckv, kr, cos32, sin32, wqa_ref, wqb_ref, wk_ref, e_ref, wv_ref, q_ref, k_ref, v_ref):
    if q_ref is not None:
        lane = lax.broadcasted_iota(jnp.int32, (1, LANES), 1)
        rope_lane = jnp.logical_and(lane >= MLA_NOPE, lane < MLA_NOPE + MLA_ROPE)
        cos_h = jnp.where(rope_lane, cos32, 1.0)
        for hd in range(MLA_HEADS):
            lo, hi = hd * LANES, (hd + 1) * LANES
            q = _dot(cq, wqa_ref[:, lo:hi]) * cos_h + _dot(cq, wqb_ref[:, lo:hi]) * sin32
            q_ref[:, lo:hi] = (q * (MLA_SCALE * LOG2E)).astype(BF16)
    k_ref[...] = (_dot(ckv, wk_ref[...]) + _dot(kr, e_ref[...])).astype(BF16)
    v_ref[...] = _dot(ckv, wv_ref[...]).astype(BF16)


def _stage_a_kernel(x_ref, mod_ref, g_ref, win_ref, wg_ref, bg_ref, qn_ref, kvn_ref, tab_ref,
                    wqa_ref, wqb_ref, wk_ref, e_ref, wv_ref,
                    fin_ref, ckv_ref, kr_ref, sq_ref, sk_ref, sv_ref, gates_ref,
                    qm_ref, km_ref, vm_ref):
    x = x_ref[...]
    h = (_rms_rows(x, g_ref[...]) * (1.0 + mod_ref[:, 1024:2048]) + mod_ref[:, 0:1024]).astype(BF16)

    def proj(seg):
        return _dot(h, win_ref[:, seg[0]:seg[1]])

    fin_ref[...] = proj(A_F).astype(BF16)
    cq = _rms_rows(proj(A_QD), qn_ref[...]).astype(BF16)
    ckv = _rms_rows(proj(A_KV), kvn_ref[...])
    ckv_ref[...] = ckv
    cos64 = tab_ref[:, 0:512]
    sin64 = tab_ref[:, 512:1024]
    sq = proj(A_SQ) * cos64 + proj(A_SQS) * sin64
    sq_ref[...] = (sq * (SWA_SCALE * LOG2E)).astype(BF16)
    sk_ref[...] = proj(A_SK) * cos64[:, 0:128] + proj(A_SKS) * sin64[:, 0:128]
    sv_ref[...] = proj(A_SV)
    cos32 = tab_ref[:, 1024:1152]
    sin32 = tab_ref[:, 1152:1280]
    kr = proj(A_KR) * cos32 + proj(A_KRS) * sin32
    kr_ref[...] = kr
    _mla_expand(cq, ckv.astype(BF16), kr.astype(BF16), cos32, sin32,
                wqa_ref, wqb_ref, wk_ref, e_ref, wv_ref, qm_ref, km_ref, vm_ref)
    for c in range(3):
        lo, hi = c * D_MODEL, (c + 1) * D_MODEL
        gates_ref[:, lo:hi] = jax.nn.sigmoid(_dot(h, wg_ref[:, lo:hi]) + bg_ref[:, lo:hi]).astype(BF16)


def _stage_a(x, modt, g0, w_in_wide, w_gate, b_gate, q_norm, kv_norm, tab, wqa, wqb, wk, e_mat, wv):
    row = lambda w: pl.BlockSpec((TM, w), lambda i: (i, 0))
    outs = [(512, BF16), (128, F32), (128, F32), (512, BF16), (128, F32), (128, F32),
            (3 * D_MODEL, BF16), (1024, BF16), (1024, BF16), (512, BF16)]
    return pl.pallas_call(
        _stage_a_kernel,
        grid=(NT,),
        in_specs=[
            row(D_MODEL),
            pl.BlockSpec((None, 1, N_MOD * D_MODEL), lambda i: (i, 0, 0)),
            _const_spec((1, D_MODEL)),
            _const_spec((D_MODEL, W_IN_WIDE)),
            _const_spec((D_MODEL, 3 * D_MODEL)),
            _const_spec((1, 3 * D_MODEL)),
            _const_spec((1, MLA_Q_RANK)),
            _const_spec((1, MLA_KV_RANK)),
            pl.BlockSpec((TM, TAB_W), lambda i: (_tab_row_block(i), 0)),
            _const_spec((MLA_Q_RANK, 1024)), _const_spec((MLA_Q_RANK, 1024)),
            _const_spec((128, 1024)), _const_spec((128, 1024)), _const_spec((128, 512)),
        ],
        out_specs=[row(w) for w, _ in outs],
        out_shape=[jax.ShapeDtypeStruct((N_TOK, w), dt) for w, dt in outs],
        compiler_params=_cparams(1),
        name="stage_a",
    )(x, modt, g0, w_in_wide, w_gate, b_gate, q_norm, kv_norm, tab, wqa, wqb, wk, e_mat, wv)


def _fnet_kernel(t_len, scale, fin_ref, f_ref, bd_ref, o_ref, zz_ref):
    @pl.when(pl.program_id(1) == 0)
    def _():
        z = fin_ref[...]
        zz_ref[0:t_len, :] = _dot(z, bd_ref[:, 0:512]).astype(BF16)
        zz_ref[t_len:2 * t_len, :] = _dot(z, bd_ref[:, 512:1024]).astype(BF16)

    o_ref[...] = (_dot(f_ref[...], zz_ref[...]) * scale).astype(BF16)


def _fnet(fin, fmat, bd, n_batch, t_len, row_block0):
    scale = 1.0 / math.sqrt(t_len * FNET_GROUP_DIM)
    return pl.pallas_call(
        functools.partial(_fnet_kernel, t_len, scale),
        grid=(n_batch, t_len // TM),
        in_specs=[
            pl.BlockSpec((t_len, FNET_WIDTH), lambda b, i: (row_block0 + b, 0)),
            pl.BlockSpec((TM, 2 * t_len), lambda b, i: (i, 0)),
            _const_spec((FNET_WIDTH, 2 * FNET_WIDTH)),
        ],
        out_specs=pl.BlockSpec((TM, FNET_WIDTH), lambda b, i: (b * (t_len // TM) + i, 0)),
        out_shape=jax.ShapeDtypeStruct((n_batch * t_len, FNET_WIDTH), BF16),
        scratch_shapes=[pltpu.VMEM((2 * t_len, FNET_WIDTH), BF16)],
        compiler_params=_cparams(2),
        name=f"fnet_{t_len}",
    )(fin, fmat, bd)


def _mla_cache_kernel(ckv_ref, kr_ref, wk_ref, e_ref, wv_ref, k_ref, v_ref):
    _mla_expand(None, ckv_ref[...].astype(BF16), kr_ref[...].astype(BF16), None, None,
                None, None, wk_ref, e_ref, wv_ref, None, k_ref, v_ref)


def _mla_cache_kv(ckv_cache, kr_cache, wk, e_mat, wv):
    row = lambda w: pl.BlockSpec((TM, w), lambda i: (i, 0))
    return pl.pallas_call(
        _mla_cache_kernel,
        grid=(N_CACHE // TM,),
        in_specs=[row(128), row(128),
                  _const_spec((128, 1024)), _const_spec((128, 1024)), _const_spec((128, 512))],
        out_specs=[row(1024), row(512)],
        out_shape=[jax.ShapeDtypeStruct((N_CACHE, 1024), BF16),
                   jax.ShapeDtypeStruct((N_CACHE, 512), BF16)],
        compiler_params=_cparams(1),
        name="mla_cache_kv",
    )(ckv_cache, kr_cache, wk, e_mat, wv)


def _mla_attn_kernel(n_seg, q_ref, *refs):
    k_refs = refs[0:n_seg]
    v_refs = refs[n_seg:2 * n_seg]
    o_ref = refs[2 * n_seg]
    lane = lax.broadcasted_iota(jnp.int32, (1, LANES), 1)
    low = lane < MLA_V
    outs = []
    for hh in range(2):
        q = q_ref[:, hh * LANES:(hh + 1) * LANES]
        ss = [_dot_nt(q, k[:, hh * LANES:(hh + 1) * LANES]) for k in k_refs]
        m = functools.reduce(jnp.maximum, [s.max(axis=-1, keepdims=True) for s in ss])
        keep = low if hh == 0 else jnp.logical_not(low)
        sum_lane = MLA_V if hh == 0 else 0
        po = None
        for s, v_ref in zip(ss, v_refs):
            v = v_ref[...]
            vm = jnp.where(lane == sum_lane, jnp.ones_like(v), jnp.where(keep, v, jnp.zeros_like(v)))
            t = _dot(jnp.exp2(s - m).astype(BF16), vm)
            po = t if po is None else po + t
        outs.append(po / po[:, sum_lane:sum_lane + 1])
    o_ref[...] = jnp.where(low, outs[0], outs[1]).astype(BF16)


def _mla_attn(q_all, k_all, v_all, k_cache, v_cache, latent):
    if latent:
        n_b, n_q = DEC_BATCH, DEC_SEQ // TM
        q0 = NT_CTX
        kv_specs = [
            pl.BlockSpec((PAST_LEN, 256), lambda b, hp, i: (b, hp)),
            pl.BlockSpec((DEC_SEQ, 256), lambda b, hp, i: (N_CTX // DEC_SEQ + b, hp)),
            pl.BlockSpec((PAST_LEN, 128), lambda b, hp, i: (b, hp)),
            pl.BlockSpec((DEC_SEQ, 128), lambda b, hp, i: (N_CTX // DEC_SEQ + b, hp)),
        ]
        args = (q_all, k_cache, k_all, v_cache, v_all)
        n_seg = 2
    else:
        n_b, n_q = BATCH, 1
        q0 = 0
        kv_specs = [
            pl.BlockSpec((SEQ, 256), lambda b, hp, i: (b, hp)),
            pl.BlockSpec((SEQ, 128), lambda b, hp, i: (b, hp)),
        ]
        args = (q_all, k_all, v_all)
        n_seg = 1
    return pl.pallas_call(
        functools.partial(_mla_attn_kernel, n_seg),
        grid=(n_b, MLA_HEADS // 2, n_q),
        in_specs=[pl.BlockSpec((TM, 256), lambda b, hp, i: (q0 + b * n_q + i, hp))] + kv_specs,
        out_specs=pl.BlockSpec((TM, 128), lambda b, hp, i: (b * n_q + i, hp)),
        out_shape=jax.ShapeDtypeStruct((n_b * n_q * TM, MLA_HEADS * MLA_V), BF16),
        compiler_params=_cparams(3),
        name="mla_attn_lat" if latent else "mla_attn_ctx",
    )(*args)


def _swa_kernel(windowed, n_qb, sink_ref, q_ref, *refs):
    n_seg = 4 if windowed else 1
    k_refs = refs[0:n_seg]
    v_refs = refs[n_seg:2 * n_seg]
    o_ref = refs[2 * n_seg]
    tq = q_ref.shape[0]
    qb = pl.program_id(1)
    lane = lax.broadcasted_iota(jnp.int32, (1, LANES), 1)
    low = lane < SWA_HEAD_DIM
    high = jnp.logical_not(low)

    ks = [r[...] for r in k_refs]
    vs = [r[...] for r in v_refs]
    ks_sw = [pltpu.roll(a, SWA_HEAD_DIM, 1) for a in ks]
    vs_sw = [pltpu.roll(a, SWA_HEAD_DIM, 1) for a in vs]

    if windowed:
        qi = lax.broadcasted_iota(jnp.int32, (2 * tq, SWA_WINDOW), 0) % tq
        kj = lax.broadcasted_iota(jnp.int32, (2 * tq, SWA_WINDOW), 1)
        masks = [None,
                 jnp.logical_and(kj >= qi, qb > 0),
                 None,
                 jnp.logical_and(kj <= qi, qb < n_qb - 1)]
    else:
        masks = [None]
    top_rows = lax.broadcasted_iota(jnp.int32, (2 * tq, 1), 0) < tq

    for g in range(SWA_KV_HEADS):
        qs = jnp.concatenate([q_ref[:, 256 * g:256 * g + 128],
                              q_ref[:, 256 * g + 128:256 * g + 256]], axis=0)
        halves = []
        for half in range(2):
            keep = low if half == 0 else high
            sum_lane = SWA_HEAD_DIM if half == 0 else 0
            straight = (g == half)
            kh = [jnp.where(keep, a if straight else b, 0.0).astype(BF16) for a, b in zip(ks, ks_sw)]
            vh = [jnp.where(lane == sum_lane, 1.0, jnp.where(keep, a if straight else b, 0.0)).astype(BF16)
                  for a, b in zip(vs, vs_sw)]
            ss = []
            for kk, mk in zip(kh, masks):
                s = _dot_nt(qs, kk)
                if mk is not None:
                    s = jnp.where(mk, s, NEG_INF)
                ss.append(s)
            sink = jnp.where(top_rows, sink_ref[4 * g + half], sink_ref[4 * g + 2 + half]) * LOG2E
            m = functools.reduce(jnp.maximum, [s.max(axis=-1, keepdims=True) for s in ss])
            m = jnp.maximum(m, sink)
            po = None
            for s, vv in zip(ss, vh):
                t = _dot(jnp.exp2(s - m).astype(BF16), vv)
                po = t if po is None else po + t
            halves.append(po / (po[:, sum_lane:sum_lane + 1] + jnp.exp2(sink - m)))
        out = jnp.where(low, halves[0], halves[1])
        o_ref[:, 256 * g:256 * g + 128] = out[0:tq].astype(BF16)
        o_ref[:, 256 * g + 128:256 * g + 256] = out[tq:2 * tq].astype(BF16)


def _swa_attn(sink, sq, sk, sv, cache_k, cache_v, latent):
    smem = pl.BlockSpec(memory_space=pltpu.SMEM)
    if latent:
        tq = SWA_WINDOW
        n_b, n_qb = DEC_BATCH, DEC_SEQ // tq
        base = N_CTX // tq

        def prev(b, i):
            return (base + b * n_qb + jnp.maximum(i - 1, 0), 0)

        def cur(b, i):
            return (base + b * n_qb + i, 0)

        def nxt(b, i):
            return (base + b * n_qb + jnp.minimum(i + 1, n_qb - 1), 0)

        cache = pl.BlockSpec((None, PAST_LEN, 128), lambda b, i: (b, 0, 0))
        blk = lambda f: pl.BlockSpec((tq, 128), f)
        kv_specs = [cache, blk(prev), blk(cur), blk(nxt)] * 2
        args = (cache_k, sk, sk, sk, cache_v, sv, sv, sv)
        q_spec = pl.BlockSpec((tq, 512), cur)
        o_spec = pl.BlockSpec((tq, 512), lambda b, i: (b * n_qb + i, 0))
    else:
        tq = SEQ
        n_b, n_qb = BATCH, 1
        blk = pl.BlockSpec((tq, 128), lambda b, i: (b, 0))
        kv_specs = [blk, blk]
        args = (sk, sv)
        q_spec = pl.BlockSpec((tq, 512), lambda b, i: (b, 0))
        o_spec = q_spec
    return pl.pallas_call(
        functools.partial(_swa_kernel, latent, n_qb),
        grid=(n_b, n_qb),
        in_specs=[smem, q_spec] + kv_specs,
        out_specs=o_spec,
        out_shape=jax.ShapeDtypeStruct((n_b * n_qb * tq, 512), BF16),
        compiler_params=_cparams(2),
        name="swa_lat" if latent else "swa_ctx",
    )(sink, sq, *args)


def _stage_e_kernel(x_ref, mod_ref, g1_ref, g2_ref, fnc_ref, fnl_ref, omc_ref, oml_ref, osc_ref, osl_ref,
                    gates_ref, wf_ref, wm_ref, ws_ref, wo_ref, x1_ref, h2a_ref, h2b_ref):
    is_ctx = pl.program_id(0) < NT_CTX
    fn = jnp.where(is_ctx, fnc_ref[...], fnl_ref[...])
    om = jnp.where(is_ctx, omc_ref[...], oml_ref[...])
    osw = jnp.where(is_ctx, osc_ref[...], osl_ref[...])
    merged = (gates_ref[:, 0:1024].astype(F32) * _dot(fn, wf_ref[...])
              + gates_ref[:, 1024:2048].astype(F32) * _dot(om, wm_ref[...])
              + gates_ref[:, 2048:3072].astype(F32) * _dot(osw, ws_ref[...]))
    mix = _dot(merged.astype(BF16), wo_ref[...])
    x1 = x_ref[...] + mod_ref[:, 2048:3072] * _rms_rows(mix, g1_ref[...])
    x1_ref[...] = x1
    h2 = _rms_rows(x1, g2_ref[...]) * (1.0 + mod_ref[:, 4096:5120]) + mod_ref[:, 3072:4096]
    h2a_ref[...], h2b_ref[...] = _pack_pair(h2)


def _stage_e(x, modt, g1, g2, mixed, gates, wf, wm, ws, wo):
    row = lambda w: pl.BlockSpec((TM, w), lambda i: (i, 0))
    ctx = pl.BlockSpec((TM, 512), lambda i: (jnp.minimum(i, NT_CTX - 1), 0))
    lat = pl.BlockSpec((TM, 512), lambda i: (jnp.maximum(i - NT_CTX, 0), 0))
    return pl.pallas_call(
        _stage_e_kernel,
        grid=(NT,),
        in_specs=[
            row(D_MODEL),
            pl.BlockSpec((None, 1, N_MOD * D_MODEL), lambda i: (i, 0, 0)),
            _const_spec((1, D_MODEL)), _const_spec((1, D_MODEL)),
            ctx, lat, ctx, lat, ctx, lat, row(3 * D_MODEL),
            _const_spec((512, D_MODEL)), _const_spec((512, D_MODEL)), _const_spec((512, D_MODEL)),
            _const_spec((D_MODEL, D_MODEL)),
        ],
        out_specs=[row(D_MODEL), row(PACKED), row(PACKED)],
        out_shape=[jax.ShapeDtypeStruct((N_TOK, D_MODEL), F32),
                   jax.ShapeDtypeStruct((N_TOK, PACKED), jnp.int32),
                   jax.ShapeDtypeStruct((N_TOK, PACKED), jnp.int32)],
        compiler_params=_cparams(1),
        name="stage_e",
    )(x, modt, g1, g2, *mixed, gates, wf, wm, ws, wo)


def _router_kernel(ha_ref, hb_ref, rwt_ref, rb_ref, s1_ref, s3_ref, s2_ref, tri_ref,
                   shared_ref, eidx_ref, ew_ref, epos_ref, cnt_ref, carry_ref):
    @pl.when(pl.program_id(0) == 0)
    def _():
        carry_ref[...] = jnp.zeros_like(carry_ref)

    h = _unpack_pair(ha_ref[...], hb_ref[...]).astype(BF16)
    act = jax.nn.silu(_dot(h, s1_ref[...])) * _dot(h, s3_ref[...])
    shared_ref[...] = _dot(act.astype(BF16), s2_ref[...])

    gsz = N_EXPERTS // N_EXPERT_GROUPS
    scores = jax.nn.sigmoid(_dot_nt(rwt_ref[...], h))
    biased = scores + rb_ref[...]
    mem = lax.broadcasted_iota(jnp.int32, (gsz, TM), 0).astype(F32)
    gs_rows = []
    for g in range(N_EXPERT_GROUPS):
        bg = biased[g * gsz:(g + 1) * gsz, :]
        m1 = bg.max(axis=0, keepdims=True)
        first = jnp.min(jnp.where(bg == m1, mem, float(gsz)), axis=0, keepdims=True)
        m2 = jnp.where(mem == first, -jnp.inf, bg).max(axis=0, keepdims=True)
        gs_rows.append(m1 + m2)
    gs = jnp.concatenate(gs_rows, axis=0)
    gid = lax.broadcasted_iota(jnp.int32, gs.shape, 0).astype(F32)
    gsel = jnp.zeros(gs.shape, F32)
    for _ in range(TOPK_GROUPS):
        mx = gs.max(axis=0, keepdims=True)
        pick = gid == jnp.min(jnp.where(gs == mx, gid, float(N_EXPERT_GROUPS)), axis=0, keepdims=True)
        gsel = jnp.where(pick, 1.0, gsel)
        gs = jnp.where(pick, -jnp.inf, gs)
    emask = jnp.concatenate(
        [jnp.broadcast_to(gsel[g:g + 1, :], (gsz, TM)) for g in range(N_EXPERT_GROUPS)], axis=0)
    cand = jnp.where(emask > 0.5, biased, NEG_INF)
    eid = lax.broadcasted_iota(jnp.int32, cand.shape, 0).astype(F32)
    picks = []
    self32 = jnp.zeros(cand.shape, F32)
    for _ in range(TOP_K):
        mx = cand.max(axis=0, keepdims=True)
        pick = eid == jnp.min(jnp.where(cand == mx, eid, float(N_EXPERTS)), axis=0, keepdims=True)
        picks.append(pick)
        self32 = jnp.where(pick, 1.0, self32)
        cand = jnp.where(pick, -jnp.inf, cand)
    pos = _dot(self32.astype(BF16), tri_ref[...]) + carry_ref[...]
    sel_scores = [jnp.sum(jnp.where(p, scores, 0.0), axis=0, keepdims=True) for p in picks]
    wsum = functools.reduce(lambda a, b: a + b, sel_scores)
    zero_f = jnp.zeros((2, TM), F32)
    eidx = [jnp.sum(jnp.where(p, eid, 0.0), axis=0, keepdims=True) for p in picks]
    epos = [jnp.sum(jnp.where(p, pos, 0.0), axis=0, keepdims=True) for p in picks]
    ew = [s / wsum * ROUTED_SCALE for s in sel_scores]
    eidx_ref[...] = jnp.concatenate(eidx + [zero_f], axis=0).astype(jnp.int32)
    epos_ref[...] = jnp.concatenate(epos + [zero_f], axis=0).astype(jnp.int32)
    ew_ref[...] = jnp.concatenate(ew + [zero_f], axis=0)
    total = carry_ref[...] + jnp.sum(self32, axis=1, keepdims=True)
    carry_ref[...] = total
    cnt_ref[...] = jnp.broadcast_to(total, cnt_ref.shape).astype(jnp.int32)


def _router(h2a, h2b, rwt, rbias, s1, s3, s2, tri):
    col = lambda dt: (pl.BlockSpec((8, TM), lambda i: (0, i)), jax.ShapeDtypeStruct((8, N_TOK), dt))
    specs = [col(jnp.int32), col(F32), col(jnp.int32)]
    return pl.pallas_call(
        _router_kernel,
        grid=(NT,),
        in_specs=[
            pl.BlockSpec((TM, PACKED), lambda i: (i, 0)),
            pl.BlockSpec((TM, PACKED), lambda i: (i, 0)),
            _const_spec((N_EXPERTS, D_MODEL)), _const_spec((N_EXPERTS, 1)),
            _const_spec((D_MODEL, SHARED_FF)), _const_spec((D_MODEL, SHARED_FF)),
            _const_spec((SHARED_FF, D_MODEL)), _const_spec((TM, TM)),
        ],
        out_specs=[pl.BlockSpec((TM, D_MODEL), lambda i: (i, 0))] + [s for s, _ in specs]
        + [_const_spec((N_EXPERTS, LANES))],
        out_shape=[jax.ShapeDtypeStruct((N_TOK, D_MODEL), F32)] + [s for _, s in specs]
        + [jax.ShapeDtypeStruct((N_EXPERTS, LANES), jnp.int32)],
        scratch_shapes=[pltpu.VMEM((N_EXPERTS, 1), F32)],
        compiler_params=_cparams(1),
        name="router_shared",
    )(h2a, h2b, rwt, rbias, s1, s3, s2, tri)


def _expert_kernel(te_ref, tv_ref, xa_ref, xb_ref, w1_ref, w3_ref, w2_ref, oa_ref, ob_ref,
                   w1b_ref, w3b_ref, w2b_ref):
    j = pl.program_id(0)
    new_expert = jnp.logical_or(j == 0, te_ref[j] != te_ref[jnp.maximum(j - 1, 0)])

    @pl.when(jnp.logical_and(tv_ref[j] == 1, new_expert))
    def _():
        w1b_ref[...] = w1_ref[...].astype(BF16)
        w3b_ref[...] = w3_ref[...].astype(BF16)
        w2b_ref[...] = w2_ref[...].astype(BF16)

    @pl.when(tv_ref[j] == 1)
    def _():
        x = _unpack_pair(xa_ref[...], xb_ref[...]).astype(BF16)
        hg = _dot(x, w1b_ref[...])
        hu = _dot(x, w3b_ref[...])
        act = (jax.nn.silu(hg) * hu).astype(BF16)
        oa_ref[...], ob_ref[...] = _pack_pair(_dot(act, w2b_ref[...]))

    @pl.when(tv_ref[j] == 0)
    def _():
        oa_ref[...] = jnp.zeros_like(oa_ref)
        ob_ref[...] = jnp.zeros_like(ob_ref)


def _experts(layer, tile_expert, tile_valid, xsa, xsb, w1, w3, w2):
    slot_rows = pl.BlockSpec((TE, PACKED), lambda j, te, tv: (j, 0))
    grid_spec = pltpu.PrefetchScalarGridSpec(
        num_scalar_prefetch=2,
        grid=(NTE,),
        in_specs=[
            slot_rows, slot_rows,
            pl.BlockSpec((None, None, D_MODEL, EXPERT_FF), lambda j, te, tv: (layer, te[j], 0, 0)),
            pl.BlockSpec((None, None, D_MODEL, EXPERT_FF), lambda j, te, tv: (layer, te[j], 0, 0)),
            pl.BlockSpec((None, None, EXPERT_FF, D_MODEL), lambda j, te, tv: (layer, te[j], 0, 0)),
        ],
        out_specs=[slot_rows, slot_rows],
        scratch_shapes=[pltpu.VMEM((D_MODEL, EXPERT_FF), BF16), pltpu.VMEM((D_MODEL, EXPERT_FF), BF16),
                        pltpu.VMEM((EXPERT_FF, D_MODEL), BF16)],
    )
    return pl.pallas_call(
        _expert_kernel,
        grid_spec=grid_spec,
        out_shape=[jax.ShapeDtypeStruct((S_MAX, PACKED), jnp.int32)] * 2,
        compiler_params=_cparams(1),
        name="experts",
    )(tile_expert, tile_valid, xsa, xsb, w1, w3, w2)


def _sc_mesh():
    return plsc.VectorSubcoreMesh(core_axis_name="c", subcore_axis_name="s",
                                  num_cores=SC_CORES, num_subcores=SC_SUBCORES)


def _sc_scatter_rows(rows, slot8):
    @functools.partial(pl.kernel, mesh=_sc_mesh(), scratch_types=[],
                       out_type=jax.ShapeDtypeStruct((S_MAX, PACKED), jnp.int32))
    def scatter(x_hbm, i_hbm, o_hbm):
        def body(x_vmem, i_vmem):
            for k in range(TOP_K):
                pltpu.sync_copy(x_vmem, o_hbm.at[i_vmem.at[k]])

        pltpu.emit_pipeline(
            body,
            grid=(N_TOK // SC_ROWS,),
            in_specs=[pl.BlockSpec((SC_ROWS, PACKED), lambda i: (i, 0)),
                      pl.BlockSpec((8, SC_ROWS), lambda i: (0, i))],
            out_specs=[],
            core_axis_name=("c", "s"),
            dimension_semantics=(pltpu.PARALLEL,),
        )(x_hbm, i_hbm)

    return scatter(rows, slot8)


def _sc_gather_rows(table, idx):
    n = idx.shape[1]

    @functools.partial(pl.kernel, mesh=_sc_mesh(), scratch_types=[],
                       out_type=jax.ShapeDtypeStruct((n, PACKED), jnp.int32))
    def gather(t_hbm, i_hbm, o_hbm):
        def body(i_vmem, o_vmem):
            pltpu.sync_copy(t_hbm.at[i_vmem.at[0]], o_vmem)

        pltpu.emit_pipeline(
            body,
            grid=(n // SC_ROWS,),
            in_specs=[pl.BlockSpec((1, SC_ROWS), lambda i: (0, i))],
            out_specs=[pl.BlockSpec((SC_ROWS, PACKED), lambda i: (i, 0))],
            core_axis_name=("c", "s"),
            dimension_semantics=(pltpu.PARALLEL,),
        )(i_hbm, o_hbm)

    return gather(table, idx)


def _stage_g_kernel(x1_ref, mod_ref, g3_ref, yga_ref, ygb_ref, ew_ref, shared_ref, o_ref):
    y = shared_ref[...]
    for k in range(TOP_K):
        y = y + ew_ref[:, k:k + 1] * _unpack_pair(yga_ref[k], ygb_ref[k])
    o_ref[...] = x1_ref[...] + mod_ref[:, 5120:6144] * _rms_rows(y, g3_ref[...])


def _stage_g(x1, modt, g3, yga, ygb, ew_rows, shared):
    row = pl.BlockSpec((TM, D_MODEL), lambda i: (i, 0))
    picked = pl.BlockSpec((TOP_K, TM, PACKED), lambda i: (0, i, 0))
    return pl.pallas_call(
        _stage_g_kernel,
        grid=(NT,),
        in_specs=[row, pl.BlockSpec((None, 1, N_MOD * D_MODEL), lambda i: (i, 0, 0)),
                  _const_spec((1, D_MODEL)), picked, picked,
                  pl.BlockSpec((TM, 8), lambda i: (i, 0)), row],
        out_specs=row,
        out_shape=jax.ShapeDtypeStruct((N_TOK, D_MODEL), F32),
        compiler_params=_cparams(1),
        name="stage_g",
    )(x1, modt, g3, yga, ygb, ew_rows, shared)


def _rope_tables():
    t = jnp.arange(DEC_SEQ)
    pos = jnp.stack([(t // GRID_W).astype(F32), (t % GRID_W).astype(F32)], axis=-1)

    def table(r):
        n_freq = r // 4
        inv = ROPE_BASE ** (-jnp.arange(n_freq, dtype=F32) / n_freq)
        ang = pos[:, :, None] * inv
        cos = jnp.cos(ang)
        sin = jnp.sin(ang)
        cos_t = jnp.stack([cos, cos], axis=2).reshape(DEC_SEQ, r)
        sin_t = jnp.stack([-sin, sin], axis=2).reshape(DEC_SEQ, r)
        return cos_t, sin_t

    c64, s64 = table(SWA_HEAD_DIM)
    c32, s32 = table(MLA_ROPE)
    lat = jnp.concatenate([jnp.tile(c64, (1, 8)), jnp.tile(s64, (1, 8)),
                           jnp.tile(c32, (1, 4)), jnp.tile(s32, (1, 4))], axis=1)
    ident = jnp.concatenate([jnp.ones((TM, 512), F32), jnp.zeros((TM, 512), F32),
                             jnp.ones((TM, 128), F32), jnp.zeros((TM, 128), F32)], axis=1)
    return jnp.concatenate([ident, lat], axis=0)


def _dft_pair(n):
    k = jnp.arange(n, dtype=jnp.int32)
    ang = ((k[:, None] * k[None, :]) % n).astype(F32) * (2.0 * math.pi / n)
    return jnp.cos(ang), jnp.sin(ang)


def _fnet_tables():
    c64, s64 = _dft_pair(FNET_GROUP_DIM)
    eye = jnp.eye(FNET_GROUPS, dtype=F32)
    bd = jnp.concatenate([jnp.kron(eye, c64), jnp.kron(eye, s64)], axis=1).astype(BF16)
    mats = []
    for t_len in (SEQ, DEC_SEQ):
        c, s = _dft_pair(t_len)
        mats.append(jnp.concatenate([c, -s], axis=1).astype(BF16))
    return bd, mats[0], mats[1]


def _swap_perm(width, half):
    return np.arange(width) ^ half


def _layer_weights(l, w_in, w_uq, w_ukv):
    w = w_in[l]
    kr = w[:, 1024:1056]
    sq = w[:, 1056:1568]
    sk = w[:, 1568:1696]
    pad96 = jnp.zeros((D_MODEL, 96), F32)
    wide = jnp.concatenate([
        w[:, 0:1024], sq, sq[:, _swap_perm(512, 16)], sk, sk[:, _swap_perm(128, 16)],
        w[:, 1696:1824], kr, pad96, kr[:, _swap_perm(32, 8)], pad96], axis=1).astype(BF16)

    uq = w_uq[l].reshape(MLA_Q_RANK, MLA_HEADS, MLA_NOPE + MLA_ROPE)
    rope_w = uq[:, :, MLA_NOPE:]
    z32 = jnp.zeros((MLA_Q_RANK, MLA_HEADS, 32), F32)
    z64 = jnp.zeros((MLA_Q_RANK, MLA_HEADS, 64), F32)
    wqa = jnp.concatenate([uq, z32], axis=2).reshape(MLA_Q_RANK, 1024).astype(BF16)
    wqb = jnp.concatenate([z64, rope_w[:, :, _swap_perm(32, 8)], z32], axis=2)
    wqb = wqb.reshape(MLA_Q_RANK, 1024).astype(BF16)
    ukv = w_ukv[l].reshape(MLA_KV_RANK, MLA_HEADS, MLA_NOPE + MLA_V)
    wk = jnp.concatenate([ukv[:, :, :MLA_NOPE], jnp.zeros((MLA_KV_RANK, MLA_HEADS, 64), F32)],
                         axis=2).reshape(MLA_KV_RANK, 1024).astype(BF16)
    wv = ukv[:, :, MLA_NOPE:].reshape(MLA_KV_RANK, 512).astype(BF16)
    return wide, wqa, wqb, wk, wv


def _rope_placement():
    e = np.zeros((128, 1024), np.float32)
    for hd in range(MLA_HEADS):
        for i in range(MLA_ROPE):
            e[i, hd * 128 + MLA_NOPE + i] = 1.0
    return jnp.asarray(e, BF16)


def _moe_dispatch_plan(eidx, epos, counts):
    padded = ((counts + TE - 1) // TE) * TE
    ends = jnp.cumsum(padded)
    offs = ends - padded
    ids = jnp.arange(N_EXPERTS, dtype=jnp.int32)
    picked_off = jnp.sum(jnp.where(eidx[:, :, None] == ids, offs, 0), axis=-1)
    slot = picked_off + epos
    starts = jnp.arange(NTE, dtype=jnp.int32) * TE
    tile_expert = jnp.sum((ends[None, :] <= starts[:, None]).astype(jnp.int32), axis=1)
    tile_expert = jnp.minimum(tile_expert, N_EXPERTS - 1)
    tile_valid = (starts < ends[-1]).astype(jnp.int32)
    return slot, tile_expert, tile_valid


def kernel(x_prompt, x_sample, cache_mla_ckv, cache_mla_krope, cache_swa_k, cache_swa_v, c, c_ctx,
           ada_w, ada_b, norm_g, w_in, q_norm, kv_norm, w_fnet, w_uq, w_ukv, w_mla_o, swa_sink,
           w_swa_o, w_gate, b_gate, w_out, router_w, router_bias, exp_w1, exp_w3, exp_w2,
           shared_w1, shared_w3, shared_w2):
    x = jnp.concatenate([x_prompt.reshape(N_CTX, D_MODEL), x_sample.reshape(N_LAT, D_MODEL)], axis=0)

    cond8 = jnp.concatenate([c_ctx[None, :], c, jnp.zeros((3, D_MODEL), F32)], axis=0)
    mod = _modulation(cond8, ada_w, ada_b)
    tile_cond = np.concatenate([np.zeros(NT_CTX, np.int32),
                                1 + np.arange(NT_LAT, dtype=np.int32) // LAT_TILES])

    tab = _rope_tables()
    bd, f_ctx, f_lat = _fnet_tables()
    e_mat = _rope_placement()
    tri = jnp.asarray(np.triu(np.ones((TM, TM), np.float32), 1), BF16)

    new_ckv, new_kr, new_k, new_v = [], [], [], []
    for l in range(DEPTH):
        modt = mod[l][tile_cond][:, None, :]
        wide, wqa, wqb, wk, wv = _layer_weights(l, w_in, w_uq, w_ukv)
        ng = norm_g[l]

        fin, ckv, kr, sq, sk, sv, gates, q_m, k_m, v_m = _stage_a(
            x, modt, ng[0:1], wide, w_gate[l].astype(BF16), b_gate[l][None, :],
            q_norm[l][None, :], kv_norm[l][None, :], tab, wqa, wqb, wk, e_mat, wv)

        new_ckv.append(ckv[:N_CTX].reshape(BATCH, SEQ, MLA_KV_RANK))
        new_kr.append(kr[:N_CTX, :MLA_ROPE].reshape(BATCH, SEQ, MLA_ROPE))
        new_k.append(sk[:N_CTX].reshape(BATCH, SEQ, SWA_KV_HEADS, SWA_HEAD_DIM))
        new_v.append(sv[:N_CTX].reshape(BATCH, SEQ, SWA_KV_HEADS, SWA_HEAD_DIM))

        fn = (_fnet(fin, f_ctx, bd, BATCH, SEQ, 0),
              _fnet(fin, f_lat, bd, DEC_BATCH, DEC_SEQ, N_CTX // DEC_SEQ))

        kr_cache = jnp.pad(cache_mla_krope[:, l].reshape(N_CACHE, MLA_ROPE), ((0, 0), (0, 96)))
        k_c, v_c = _mla_cache_kv(cache_mla_ckv[:, l].reshape(N_CACHE, MLA_KV_RANK), kr_cache,
                                 wk, e_mat, wv)
        om = (_mla_attn(q_m, k_m, v_m, k_c, v_c, latent=False),
              _mla_attn(q_m, k_m, v_m, k_c, v_c, latent=True))

        ck = cache_swa_k[:, l].reshape(DEC_BATCH, PAST_LEN, 128)
        cv = cache_swa_v[:, l].reshape(DEC_BATCH, PAST_LEN, 128)
        osw = (_swa_attn(swa_sink[l], sq, sk, sv, ck, cv, latent=False),
               _swa_attn(swa_sink[l], sq, sk, sv, ck, cv, latent=True))

        x1, h2a, h2b = _stage_e(x, modt, ng[1:2], ng[2:3], fn + om + osw, gates,
                                w_fnet[l].astype(BF16), w_mla_o[l].astype(BF16),
                                w_swa_o[l].astype(BF16), w_out[l].astype(BF16))

        shared, eidx, ew, epos, counts = _router(
            h2a, h2b, router_w[l].T.astype(BF16), router_bias[l][:, None],
            shared_w1[l].astype(BF16), shared_w3[l].astype(BF16), shared_w2[l].astype(BF16), tri)
        slot, tile_expert, tile_valid = _moe_dispatch_plan(eidx, epos, counts[:, 0])
        xsa = _sc_scatter_rows(h2a, slot)
        xsb = _sc_scatter_rows(h2b, slot)
        ysa, ysb = _experts(l, tile_expert, tile_valid, xsa, xsb, exp_w1, exp_w3, exp_w2)
        picks = slot[:TOP_K].reshape(1, TOP_K * N_TOK)
        yga = _sc_gather_rows(ysa, picks).reshape(TOP_K, N_TOK, PACKED)
        ygb = _sc_gather_rows(ysb, picks).reshape(TOP_K, N_TOK, PACKED)
        x = _stage_g(x1, modt, ng[3:4], yga, ygb, ew.T, shared)

    y_p = x[:N_CTX].reshape(BATCH, SEQ, D_MODEL)
    y_s = x[N_CTX:].reshape(DEC_BATCH, DEC_SEQ, D_MODEL)
    return (y_p, y_s, jnp.stack(new_ckv, axis=1), jnp.stack(new_kr, axis=1),
            jnp.stack(new_k, axis=1), jnp.stack(new_v, axis=1))
```

```python
import functools
import math

import numpy as np
import jax
import jax.numpy as jnp
from jax import lax
from jax.experimental import pallas as pl
from jax.experimental.pallas import tpu as pltpu
from jax.experimental.pallas import tpu_sc as plsc

D_MODEL = 1024
BATCH = 16
SEQ = 256
DEPTH = 2
DEC_BATCH = 4
DEC_SEQ = 2048
PAST_LEN = 512
GRID_W = 64
EPS = 1e-6
ROPE_BASE = 10000.0
NEG_INF = -1e30

FNET_GROUPS = 8
FNET_GROUP_DIM = 64
FNET_WIDTH = 512
MLA_HEADS = 8
MLA_Q_RANK = 384
MLA_KV_RANK = 128
MLA_NOPE = 64
MLA_ROPE = 32
MLA_V = 64
MLA_SCALE = (MLA_NOPE + MLA_ROPE) ** -0.5
LOG2E = math.log2(math.e)
SWA_HEADS = 8
SWA_KV_HEADS = 2
SWA_HEAD_DIM = 64
SWA_WINDOW = 128
SWA_SCALE = SWA_HEAD_DIM ** -0.5
N_MOD = 6
N_EXPERTS = 64
N_EXPERT_GROUPS = 8
TOPK_GROUPS = 4
TOP_K = 6
EXPERT_FF = 256
SHARED_FF = 256
ROUTED_SCALE = 2.5

LANES = 128
TM = 256
N_CTX = BATCH * SEQ
N_LAT = DEC_BATCH * DEC_SEQ
N_TOK = N_CTX + N_LAT
N_CACHE = DEC_BATCH * PAST_LEN
NT_CTX = N_CTX // TM
NT_LAT = N_LAT // TM
NT = N_TOK // TM
LAT_TILES = DEC_SEQ // TM
TE = 256
S_MAX = N_TOK * TOP_K + N_EXPERTS * TE
EXPERT_ROWS = 128
VMEM_LIMIT = 56 * 1024 * 1024
PACKED = D_MODEL // 4
SC_ROWS = 128
SC_CORES = 2
SC_SUBCORES = 16

A_F = (0, 512)
A_QD = (512, 896)
A_KV = (896, 1024)
A_SQ = (1024, 1536)
A_SQS = (1536, 2048)
A_SK = (2048, 2176)
A_SKS = (2176, 2304)
A_SV = (2304, 2432)
A_KR = (2432, 2560)
A_KRS = (2560, 2688)
W_IN_WIDE = 2688
TAB_W = 1280

F32 = jnp.float32
BF16 = jnp.bfloat16


def _cparams(n_axes, parallel=False):
    sem = ("parallel" if parallel else "arbitrary",) * n_axes
    return pltpu.CompilerParams(dimension_semantics=sem, vmem_limit_bytes=VMEM_LIMIT)


def _dot(a, b):
    return jnp.dot(a, b, preferred_element_type=F32)


def _dot_nt(a, b):
    return lax.dot_general(a, b, (((1,), (1,)), ((), ())), preferred_element_type=F32)


def _rms_rows(v, g):
    return v * lax.rsqrt(jnp.mean(v * v, axis=-1, keepdims=True) + EPS) * g


def _pack_rows(v):
    half = v.shape[1] // 2
    lo = lax.bitcast_convert_type(v[:, :half].astype(BF16).astype(F32), jnp.int32)
    hi = lax.bitcast_convert_type(v[:, half:].astype(BF16).astype(F32), jnp.int32)
    return jnp.bitwise_or(jnp.bitwise_and(hi, -65536), jnp.bitwise_and(jnp.right_shift(lo, 16), 65535))


def _unpack_rows(w):
    lo = lax.bitcast_convert_type(jnp.left_shift(w, 16), F32)
    hi = lax.bitcast_convert_type(jnp.bitwise_and(w, -65536), F32)
    return jnp.concatenate([lo, hi], axis=1)


def _pack_pair(v):
    half = v.shape[1] // 2
    return _pack_rows(v[:, :half]), _pack_rows(v[:, half:])


def _unpack_pair(a, b):
    return jnp.concatenate([_unpack_rows(a), _unpack_rows(b)], axis=1)


def _const_spec(shape):
    return pl.BlockSpec(shape, lambda *_: (0,) * len(shape))


def _tab_row_block(i):
    return jnp.where(i < NT_CTX, 0, 1 + (i - NT_CTX) % LAT_TILES)


def _mod_kernel(cond_ref, w_ref, b_ref, o_ref):
    c = cond_ref[...]
    a = (c * jax.nn.sigmoid(c)).astype(BF16)
    o_ref[...] = _dot(a, w_ref[...].astype(BF16)) + b_ref[...]


def _modulation(cond8, ada_w, ada_b):
    tn = 512
    nj = N_MOD * D_MODEL // tn
    return pl.pallas_call(
        _mod_kernel,
        grid=(DEPTH, nj),
        in_specs=[
            pl.BlockSpec((8, D_MODEL), lambda l, j: (0, 0)),
            pl.BlockSpec((None, D_MODEL, tn), lambda l, j: (l, 0, j)),
            pl.BlockSpec((None, 1, tn), lambda l, j: (l, 0, j)),
        ],
        out_specs=pl.BlockSpec((None, 8, tn), lambda l, j: (l, 0, j)),
        out_shape=jax.ShapeDtypeStruct((DEPTH, 8, N_MOD * D_MODEL), F32),
        compiler_params=_cparams(2),
        name="modulation",
    )(cond8, ada_w, ada_b.reshape(DEPTH, 1, N_MOD * D_MODEL))


def _mla_expand(cq, ckv, kr, cos32, sin32, wqa_ref, wqb_ref, wk_ref, e_ref, wv_ref, q_ref, k_ref, v_ref):
    if q_ref is not None:
        lane = lax.broadcasted_iota(jnp.int32, (1, LANES), 1)
        rope_lane = jnp.logical_and(lane >= MLA_NOPE, lane < MLA_NOPE + MLA_ROPE)
        cos_h = jnp.where(rope_lane, cos32, 1.0)
        for hd in range(MLA_HEADS):
            lo, hi = hd * LANES, (hd + 1) * LANES
            q = _dot(cq, wqa_ref[:, lo:hi]) * cos_h + _dot(cq, wqb_ref[:, lo:hi]) * sin32
            q_ref[:, lo:hi] = (q * (MLA_SCALE * LOG2E)).astype(BF16)
    k_ref[...] = (_dot(ckv, wk_ref[...]) + _dot(kr, e_ref[...])).astype(BF16)
    v_ref[...] = _dot(ckv, wv_ref[...]).astype(BF16)


def _stage_a_kernel(x_ref, mod_ref, g_ref, win_ref, wg_ref, bg_ref, qn_ref, kvn_ref, tab_ref,
                    wqa_ref, wqb_ref, wk_ref, e_ref, wv_ref,
                    fin_ref, ckv_ref, kr_ref, sq_ref, sk_ref, sv_ref, gates_ref,
                    qm_ref, km_ref, vm_ref):
    x = x_ref[...]
    h = (_rms_rows(x, g_ref[...]) * (1.0 + mod_ref[:, 1024:2048]) + mod_ref[:, 0:1024]).astype(BF16)

    def proj(seg):
        return _dot(h, win_ref[:, seg[0]:seg[1]])

    fin_ref[...] = proj(A_F).astype(BF16)
    cq = _rms_rows(proj(A_QD), qn_ref[...]).astype(BF16)
    ckv = _rms_rows(proj(A_KV), kvn_ref[...])
    ckv_ref[...] = ckv
    cos64 = tab_ref[:, 0:512]
    sin64 = tab_ref[:, 512:1024]
    sq = proj(A_SQ) * cos64 + proj(A_SQS) * sin64
    sq_ref[...] = (sq * (SWA_SCALE * LOG2E)).astype(BF16)
    sk_ref[...] = proj(A_SK) * cos64[:, 0:128] + proj(A_SKS) * sin64[:, 0:128]
    sv_ref[...] = proj(A_SV)
    cos32 = tab_ref[:, 1024:1152]
    sin32 = tab_ref[:, 1152:1280]
    kr = proj(A_KR) * cos32 + proj(A_KRS) * sin32
    kr_ref[...] = kr
    _mla_expand(cq, ckv.astype(BF16), kr.astype(BF16), cos32, sin32,
                wqa_ref, wqb_ref, wk_ref, e_ref, wv_ref, qm_ref, km_ref, vm_ref)
    for c in range(3):
        lo, hi = c * D_MODEL, (c + 1) * D_MODEL
        gates_ref[:, lo:hi] = jax.nn.sigmoid(_dot(h, wg_ref[:, lo:hi]) + bg_ref[:, lo:hi]).astype(BF16)


def _stage_a(x, modt, g0, w_in_wide, w_gate, b_gate, q_norm, kv_norm, tab, wqa, wqb, wk, e_mat, wv):
    row = lambda w: pl.BlockSpec((TM, w), lambda i: (i, 0))
    outs = [(512, BF16), (128, F32), (128, F32), (512, BF16), (128, F32), (128, F32),
            (3 * D_MODEL, BF16), (1024, BF16), (1024, BF16), (512, BF16)]
    return pl.pallas_call(
        _stage_a_kernel,
        grid=(NT,),
        in_specs=[
            row(D_MODEL),
            pl.BlockSpec((None, 1, N_MOD * D_MODEL), lambda i: (i, 0, 0)),
            _const_spec((1, D_MODEL)),
            _const_spec((D_MODEL, W_IN_WIDE)),
            _const_spec((D_MODEL, 3 * D_MODEL)),
            _const_spec((1, 3 * D_MODEL)),
            _const_spec((1, MLA_Q_RANK)),
            _const_spec((1, MLA_KV_RANK)),
            pl.BlockSpec((TM, TAB_W), lambda i: (_tab_row_block(i), 0)),
            _const_spec((MLA_Q_RANK, 1024)), _const_spec((MLA_Q_RANK, 1024)),
            _const_spec((128, 1024)), _const_spec((128, 1024)), _const_spec((128, 512)),
        ],
        out_specs=[row(w) for w, _ in outs],
        out_shape=[jax.ShapeDtypeStruct((N_TOK, w), dt) for w, dt in outs],
        compiler_params=_cparams(1),
        name="stage_a",
    )(x, modt, g0, w_in_wide, w_gate, b_gate, q_norm, kv_norm, tab, wqa, wqb, wk, e_mat, wv)


def _fnet_kernel(t_len, scale, fin_ref, f_ref, bd_ref, o_ref, zz_ref):
    @pl.when(pl.program_id(1) == 0)
    def _():
        z = fin_ref[...]
        zz_ref[0:t_len, :] = _dot(z, bd_ref[:, 0:512]).astype(BF16)
        zz_ref[t_len:2 * t_len, :] = _dot(z, bd_ref[:, 512:1024]).astype(BF16)

    o_ref[...] = (_dot(f_ref[...], zz_ref[...]) * scale).astype(BF16)


def _fnet(fin, fmat, bd, n_batch, t_len, row_block0):
    scale = 1.0 / math.sqrt(t_len * FNET_GROUP_DIM)
    return pl.pallas_call(
        functools.partial(_fnet_kernel, t_len, scale),
        grid=(n_batch, t_len // TM),
        in_specs=[
            pl.BlockSpec((t_len, FNET_WIDTH), lambda b, i: (row_block0 + b, 0)),
            pl.BlockSpec((TM, 2 * t_len), lambda b, i: (i, 0)),
            _const_spec((FNET_WIDTH, 2 * FNET_WIDTH)),
        ],
        out_specs=pl.BlockSpec((TM, FNET_WIDTH), lambda b, i: (b * (t_len // TM) + i, 0)),
        out_shape=jax.ShapeDtypeStruct((n_batch * t_len, FNET_WIDTH), BF16),
        scratch_shapes=[pltpu.VMEM((2 * t_len, FNET_WIDTH), BF16)],
        compiler_params=_cparams(2),
        name=f"fnet_{t_len}",
    )(fin, fmat, bd)


def _mla_cache_kernel(ckv_ref, kr_ref, wk_ref, e_ref, wv_ref, k_ref, v_ref):
    _mla_expand(None, ckv_ref[...].astype(BF16), kr_ref[...].astype(BF16), None, None,
                None, None, wk_ref, e_ref, wv_ref, None, k_ref, v_ref)


def _mla_cache_kv(ckv_cache, kr_cache, wk, e_mat, wv):
    row = lambda w: pl.BlockSpec((TM, w), lambda i: (i, 0))
    return pl.pallas_call(
        _mla_cache_kernel,
        grid=(N_CACHE // TM,),
        in_specs=[row(128), row(128),
                  _const_spec((128, 1024)), _const_spec((128, 1024)), _const_spec((128, 512))],
        out_specs=[row(1024), row(512)],
        out_shape=[jax.ShapeDtypeStruct((N_CACHE, 1024), BF16),
                   jax.ShapeDtypeStruct((N_CACHE, 512), BF16)],
        compiler_params=_cparams(1),
        name="mla_cache_kv",
    )(ckv_cache, kr_cache, wk, e_mat, wv)


def _mla_attn_kernel(n_seg, q_ref, *refs):
    k_refs = refs[0:n_seg]
    v_refs = refs[n_seg:2 * n_seg]
    o_ref = refs[2 * n_seg]
    lane = lax.broadcasted_iota(jnp.int32, (1, LANES), 1)
    low = lane < MLA_V
    outs = []
    for hh in range(2):
        q = q_ref[:, hh * LANES:(hh + 1) * LANES]
        ss = [_dot_nt(q, k[:, hh * LANES:(hh + 1) * LANES]) for k in k_refs]
        m = functools.reduce(jnp.maximum, [s.max(axis=-1, keepdims=True) for s in ss])
        keep = low if hh == 0 else jnp.logical_not(low)
        sum_lane = MLA_V if hh == 0 else 0
        po = None
        for s, v_ref in zip(ss, v_refs):
            v = v_ref[...]
            vm = jnp.where(lane == sum_lane, jnp.ones_like(v), jnp.where(keep, v, jnp.zeros_like(v)))
            t = _dot(jnp.exp2(s - m).astype(BF16), vm)
            po = t if po is None else po + t
        outs.append(po / po[:, sum_lane:sum_lane + 1])
    o_ref[...] = jnp.where(low, outs[0], outs[1]).astype(BF16)


def _mla_attn(q_all, k_all, v_all, k_cache, v_cache, latent):
    if latent:
        n_b, n_q = DEC_BATCH, DEC_SEQ // TM
        q0 = NT_CTX
        kv_specs = [
            pl.BlockSpec((PAST_LEN, 256), lambda b, hp, i: (b, hp)),
            pl.BlockSpec((DEC_SEQ, 256), lambda b, hp, i: (N_CTX // DEC_SEQ + b, hp)),
            pl.BlockSpec((PAST_LEN, 128), lambda b, hp, i: (b, hp)),
            pl.BlockSpec((DEC_SEQ, 128), lambda b, hp, i: (N_CTX // DEC_SEQ + b, hp)),
        ]
        args = (q_all, k_cache, k_all, v_cache, v_all)
        n_seg = 2
    else:
        n_b, n_q = BATCH, 1
        q0 = 0
        kv_specs = [
            pl.BlockSpec((SEQ, 256), lambda b, hp, i: (b, hp)),
            pl.BlockSpec((SEQ, 128), lambda b, hp, i: (b, hp)),
        ]
        args = (q_all, k_all, v_all)
        n_seg = 1
    return pl.pallas_call(
        functools.partial(_mla_attn_kernel, n_seg),
        grid=(n_b, MLA_HEADS // 2, n_q),
        in_specs=[pl.BlockSpec((TM, 256), lambda b, hp, i: (q0 + b * n_q + i, hp))] + kv_specs,
        out_specs=pl.BlockSpec((TM, 128), lambda b, hp, i: (b * n_q + i, hp)),
        out_shape=jax.ShapeDtypeStruct((n_b * n_q * TM, MLA_HEADS * MLA_V), BF16),
        compiler_params=_cparams(3),
        name="mla_attn_lat" if latent else "mla_attn_ctx",
    )(*args)


def _swa_kernel(windowed, n_qb, sink_ref, q_ref, *refs):
    n_seg = 4 if windowed else 1
    k_refs = refs[0:n_seg]
    v_refs = refs[n_seg:2 * n_seg]
    o_ref = refs[2 * n_seg]
    tq = q_ref.shape[0]
    qb = pl.program_id(1)
    lane = lax.broadcasted_iota(jnp.int32, (1, LANES), 1)
    low = lane < SWA_HEAD_DIM
    high = jnp.logical_not(low)

    ks = [r[...] for r in k_refs]
    vs = [r[...] for r in v_refs]
    ks_sw = [pltpu.roll(a, SWA_HEAD_DIM, 1) for a in ks]
    vs_sw = [pltpu.roll(a, SWA_HEAD_DIM, 1) for a in vs]

    if windowed:
        qi = lax.broadcasted_iota(jnp.int32, (2 * tq, SWA_WINDOW), 0) % tq
        kj = lax.broadcasted_iota(jnp.int32, (2 * tq, SWA_WINDOW), 1)
        masks = [None,
                 jnp.logical_and(kj >= qi, qb > 0),
                 None,
                 jnp.logical_and(kj <= qi, qb < n_qb - 1)]
    else:
        masks = [None]
    top_rows = lax.broadcasted_iota(jnp.int32, (2 * tq, 1), 0) < tq

    for g in range(SWA_KV_HEADS):
        qs = jnp.concatenate([q_ref[:, 256 * g:256 * g + 128],
                              q_ref[:, 256 * g + 128:256 * g + 256]], axis=0)
        halves = []
        for half in range(2):
            keep = low if half == 0 else high
            sum_lane = SWA_HEAD_DIM if half == 0 else 0
            straight = (g == half)
            kh = [jnp.where(keep, a if straight else b, 0.0).astype(BF16) for a, b in zip(ks, ks_sw)]
            vh = [jnp.where(lane == sum_lane, 1.0, jnp.where(keep, a if straight else b, 0.0)).astype(BF16)
                  for a, b in zip(vs, vs_sw)]
            ss = []
            for kk, mk in zip(kh, masks):
                s = _dot_nt(qs, kk)
                if mk is not None:
                    s = jnp.where(mk, s, NEG_INF)
                ss.append(s)
            sink = jnp.where(top_rows, sink_ref[4 * g + half], sink_ref[4 * g + 2 + half]) * LOG2E
            m = functools.reduce(jnp.maximum, [s.max(axis=-1, keepdims=True) for s in ss])
            m = jnp.maximum(m, sink)
            po = None
            for s, vv in zip(ss, vh):
                t = _dot(jnp.exp2(s - m).astype(BF16), vv)
                po = t if po is None else po + t
            halves.append(po / (po[:, sum_lane:sum_lane + 1] + jnp.exp2(sink - m)))
        out = jnp.where(low, halves[0], halves[1])
        o_ref[:, 256 * g:256 * g + 128] = out[0:tq].astype(BF16)
        o_ref[:, 256 * g + 128:256 * g + 256] = out[tq:2 * tq].astype(BF16)


def _swa_attn(sink, sq, sk, sv, cache_k, cache_v, latent):
    smem = pl.BlockSpec(memory_space=pltpu.SMEM)
    if latent:
        tq = SWA_WINDOW
        n_b, n_qb = DEC_BATCH, DEC_SEQ // tq
        base = N_CTX // tq

        def prev(b, i):
            return (base + b * n_qb + jnp.maximum(i - 1, 0), 0)

        def cur(b, i):
            return (base + b * n_qb + i, 0)

        def nxt(b, i):
            return (base + b * n_qb + jnp.minimum(i + 1, n_qb - 1), 0)

        cache = pl.BlockSpec((None, PAST_LEN, 128), lambda b, i: (b, 0, 0))
        blk = lambda f: pl.BlockSpec((tq, 128), f)
        kv_specs = [cache, blk(prev), blk(cur), blk(nxt)] * 2
        args = (cache_k, sk, sk, sk, cache_v, sv, sv, sv)
        q_spec = pl.BlockSpec((tq, 512), cur)
        o_spec = pl.BlockSpec((tq, 512), lambda b, i: (b * n_qb + i, 0))
    else:
        tq = SEQ
        n_b, n_qb = BATCH, 1
        blk = pl.BlockSpec((tq, 128), lambda b, i: (b, 0))
        kv_specs = [blk, blk]
        args = (sk, sv)
        q_spec = pl.BlockSpec((tq, 512), lambda b, i: (b, 0))
        o_spec = q_spec
    return pl.pallas_call(
        functools.partial(_swa_kernel, latent, n_qb),
        grid=(n_b, n_qb),
        in_specs=[smem, q_spec] + kv_specs,
        out_specs=o_spec,
        out_shape=jax.ShapeDtypeStruct((n_b * n_qb * tq, 512), BF16),
        compiler_params=_cparams(2),
        name="swa_lat" if latent else "swa_ctx",
    )(sink, sq, *args)


def _stage_e_kernel(x_ref, mod_ref, g1_ref, g2_ref, fnc_ref, fnl_ref, omc_ref, oml_ref, osc_ref, osl_ref,
                    gates_ref, wf_ref, wm_ref, ws_ref, wo_ref, x1_ref, h2a_ref, h2b_ref):
    is_ctx = pl.program_id(0) < NT_CTX
    fn = jnp.where(is_ctx, fnc_ref[...], fnl_ref[...])
    om = jnp.where(is_ctx, omc_ref[...], oml_ref[...])
    osw = jnp.where(is_ctx, osc_ref[...], osl_ref[...])
    merged = (gates_ref[:, 0:1024].astype(F32) * _dot(fn, wf_ref[...])
              + gates_ref[:, 1024:2048].astype(F32) * _dot(om, wm_ref[...])
              + gates_ref[:, 2048:3072].astype(F32) * _dot(osw, ws_ref[...]))
    mix = _dot(merged.astype(BF16), wo_ref[...])
    x1 = x_ref[...] + mod_ref[:, 2048:3072] * _rms_rows(mix, g1_ref[...])
    x1_ref[...] = x1
    h2 = _rms_rows(x1, g2_ref[...]) * (1.0 + mod_ref[:, 4096:5120]) + mod_ref[:, 3072:4096]
    h2a_ref[...], h2b_ref[...] = _pack_pair(h2)


def _stage_e(x, modt, g1, g2, mixed, gates, wf, wm, ws, wo):
    row = lambda w: pl.BlockSpec((TM, w), lambda i: (i, 0))
    ctx = pl.BlockSpec((TM, 512), lambda i: (jnp.minimum(i, NT_CTX - 1), 0))
    lat = pl.BlockSpec((TM, 512), lambda i: (jnp.maximum(i - NT_CTX, 0), 0))
    return pl.pallas_call(
        _stage_e_kernel,
        grid=(NT,),
        in_specs=[
            row(D_MODEL),
            pl.BlockSpec((None, 1, N_MOD * D_MODEL), lambda i: (i, 0, 0)),
            _const_spec((1, D_MODEL)), _const_spec((1, D_MODEL)),
            ctx, lat, ctx, lat, ctx, lat, row(3 * D_MODEL),
            _const_spec((512, D_MODEL)), _const_spec((512, D_MODEL)), _const_spec((512, D_MODEL)),
            _const_spec((D_MODEL, D_MODEL)),
        ],
        out_specs=[row(D_MODEL), row(PACKED), row(PACKED)],
        out_shape=[jax.ShapeDtypeStruct((N_TOK, D_MODEL), F32),
                   jax.ShapeDtypeStruct((N_TOK, PACKED), jnp.int32),
                   jax.ShapeDtypeStruct((N_TOK, PACKED), jnp.int32)],
        compiler_params=_cparams(1),
        name="stage_e",
    )(x, modt, g1, g2, *mixed, gates, wf, wm, ws, wo)


def _router_kernel(ha_ref, hb_ref, rwt_ref, rb_ref, s1_ref, s3_ref, s2_ref, tri_ref,
                   shared_ref, eidx_ref, ew_ref, epos_ref, cnt_ref, carry_ref):
    @pl.when(pl.program_id(0) == 0)
    def _():
        carry_ref[...] = jnp.zeros_like(carry_ref)

    h = _unpack_pair(ha_ref[...], hb_ref[...]).astype(BF16)
    act = jax.nn.silu(_dot(h, s1_ref[...])) * _dot(h, s3_ref[...])
    shared_ref[...] = _dot(act.astype(BF16), s2_ref[...])

    gsz = N_EXPERTS // N_EXPERT_GROUPS
    scores = jax.nn.sigmoid(_dot_nt(rwt_ref[...], h))
    biased = scores + rb_ref[...]
    mem = lax.broadcasted_iota(jnp.int32, (gsz, TM), 0).astype(F32)
    gs_rows = []
    for g in range(N_EXPERT_GROUPS):
        bg = biased[g * gsz:(g + 1) * gsz, :]
        m1 = bg.max(axis=0, keepdims=True)
        first = jnp.min(jnp.where(bg == m1, mem, float(gsz)), axis=0, keepdims=True)
        m2 = jnp.where(mem == first, -jnp.inf, bg).max(axis=0, keepdims=True)
        gs_rows.append(m1 + m2)
    gs = jnp.concatenate(gs_rows, axis=0)
    gid = lax.broadcasted_iota(jnp.int32, gs.shape, 0).astype(F32)
    gsel = jnp.zeros(gs.shape, F32)
    for _ in range(TOPK_GROUPS):
        mx = gs.max(axis=0, keepdims=True)
        pick = gid == jnp.min(jnp.where(gs == mx, gid, float(N_EXPERT_GROUPS)), axis=0, keepdims=True)
        gsel = jnp.where(pick, 1.0, gsel)
        gs = jnp.where(pick, -jnp.inf, gs)
    emask = jnp.concatenate(
        [jnp.broadcast_to(gsel[g:g + 1, :], (gsz, TM)) for g in range(N_EXPERT_GROUPS)], axis=0)
    cand = jnp.where(emask > 0.5, biased, NEG_INF)
    eid = lax.broadcasted_iota(jnp.int32, cand.shape, 0).astype(F32)
    picks = []
    self32 = jnp.zeros(cand.shape, F32)
    for _ in range(TOP_K):
        mx = cand.max(axis=0, keepdims=True)
        pick = eid == jnp.min(jnp.where(cand == mx, eid, float(N_EXPERTS)), axis=0, keepdims=True)
        picks.append(pick)
        self32 = jnp.where(pick, 1.0, self32)
        cand = jnp.where(pick, -jnp.inf, cand)
    pos = _dot(self32.astype(BF16), tri_ref[...]) + carry_ref[...]
    sel_scores = [jnp.sum(jnp.where(p, scores, 0.0), axis=0, keepdims=True) for p in picks]
    wsum = functools.reduce(lambda a, b: a + b, sel_scores)
    zero_f = jnp.zeros((2, TM), F32)
    eidx = [jnp.sum(jnp.where(p, eid, 0.0), axis=0, keepdims=True) for p in picks]
    epos = [jnp.sum(jnp.where(p, pos, 0.0), axis=0, keepdims=True) for p in picks]
    ew = [s / wsum * ROUTED_SCALE for s in sel_scores]
    eidx_ref[...] = jnp.concatenate(eidx + [zero_f], axis=0).astype(jnp.int32)
    epos_ref[...] = jnp.concatenate(epos + [zero_f], axis=0).astype(jnp.int32)
    ew_ref[...] = jnp.concatenate(ew + [zero_f], axis=0)
    total = carry_ref[...] + jnp.sum(self32, axis=1, keepdims=True)
    carry_ref[...] = total
    cnt_ref[...] = jnp.broadcast_to(total, cnt_ref.shape).astype(jnp.int32)


def _router(h2a, h2b, rwt, rbias, s1, s3, s2, tri):
    col = lambda dt: (pl.BlockSpec((8, TM), lambda i: (0, i)), jax.ShapeDtypeStruct((8, N_TOK), dt))
    specs = [col(jnp.int32), col(F32), col(jnp.int32)]
    return pl.pallas_call(
        _router_kernel,
        grid=(NT,),
        in_specs=[
            pl.BlockSpec((TM, PACKED), lambda i: (i, 0)),
            pl.BlockSpec((TM, PACKED), lambda i: (i, 0)),
            _const_spec((N_EXPERTS, D_MODEL)), _const_spec((N_EXPERTS, 1)),
            _const_spec((D_MODEL, SHARED_FF)), _const_spec((D_MODEL, SHARED_FF)),
            _const_spec((SHARED_FF, D_MODEL)), _const_spec((TM, TM)),
        ],
        out_specs=[pl.BlockSpec((TM, D_MODEL), lambda i: (i, 0))] + [s for s, _ in specs]
        + [_const_spec((N_EXPERTS, LANES))],
        out_shape=[jax.ShapeDtypeStruct((N_TOK, D_MODEL), F32)] + [s for _, s in specs]
        + [jax.ShapeDtypeStruct((N_EXPERTS, LANES), jnp.int32)],
        scratch_shapes=[pltpu.VMEM((N_EXPERTS, 1), F32)],
        compiler_params=_cparams(1),
        name="router_shared",
    )(h2a, h2b, rwt, rbias, s1, s3, s2, tri)


def _expert_kernel(first_ref, count_ref, xa_hbm, xb_hbm, w1_ref, w3_ref, w2_ref, oa_hbm, ob_hbm,
                   w1b_ref, w3b_ref, w2b_ref):
    e = pl.program_id(0)
    n_tiles = count_ref[e]

    @pl.when(n_tiles > 0)
    def _():
        w1b_ref[...] = w1_ref[...].astype(BF16)
        w3b_ref[...] = w3_ref[...].astype(BF16)
        w2b_ref[...] = w2_ref[...].astype(BF16)
        first = first_ref[e]

        def tile(xa_ref, xb_ref, oa_ref, ob_ref):
            for r in range(TE // EXPERT_ROWS):
                rows = pl.ds(r * EXPERT_ROWS, EXPERT_ROWS)
                x = _unpack_pair(xa_ref[rows, :], xb_ref[rows, :]).astype(BF16)
                hg = _dot(x, w1b_ref[...])
                hu = _dot(x, w3b_ref[...])
                act = (jax.nn.silu(hg) * hu).astype(BF16)
                oa_ref[rows, :], ob_ref[rows, :] = _pack_pair(_dot(act, w2b_ref[...]))

        slot_rows = pl.BlockSpec((TE, PACKED), lambda t: (first + t, 0))
        pltpu.emit_pipeline(tile, grid=(n_tiles,), in_specs=[slot_rows, slot_rows],
                            out_specs=[slot_rows, slot_rows])(xa_hbm, xb_hbm, oa_hbm, ob_hbm)


def _experts(layer, first_tile, tile_count, xsa, xsb, w1, w3, w2):
    anywhere = pl.BlockSpec(memory_space=pl.ANY)
    grid_spec = pltpu.PrefetchScalarGridSpec(
        num_scalar_prefetch=2,
        grid=(N_EXPERTS,),
        in_specs=[
            anywhere, anywhere,
            pl.BlockSpec((None, None, D_MODEL, EXPERT_FF), lambda e, ft, tc: (layer, e, 0, 0)),
            pl.BlockSpec((None, None, D_MODEL, EXPERT_FF), lambda e, ft, tc: (layer, e, 0, 0)),
            pl.BlockSpec((None, None, EXPERT_FF, D_MODEL), lambda e, ft, tc: (layer, e, 0, 0)),
        ],
        out_specs=[anywhere, anywhere],
        scratch_shapes=[pltpu.VMEM((D_MODEL, EXPERT_FF), BF16), pltpu.VMEM((D_MODEL, EXPERT_FF), BF16),
                        pltpu.VMEM((EXPERT_FF, D_MODEL), BF16)],
    )
    return pl.pallas_call(
        _expert_kernel,
        grid_spec=grid_spec,
        out_shape=[jax.ShapeDtypeStruct((S_MAX, PACKED), jnp.int32)] * 2,
        compiler_params=_cparams(1),
        name="experts",
    )(first_tile, tile_count, xsa, xsb, w1, w3, w2)


def _sc_mesh():
    return plsc.VectorSubcoreMesh(core_axis_name="c", subcore_axis_name="s",
                                  num_cores=SC_CORES, num_subcores=SC_SUBCORES)


def _sc_scatter_rows(rows, slot8):
    @functools.partial(pl.kernel, mesh=_sc_mesh(), scratch_types=[],
                       out_type=jax.ShapeDtypeStruct((S_MAX, PACKED), jnp.int32))
    def scatter(x_hbm, i_hbm, o_hbm):
        def body(x_vmem, i_vmem):
            for k in range(TOP_K):
                pltpu.sync_copy(x_vmem, o_hbm.at[i_vmem.at[k]])

        pltpu.emit_pipeline(
            body,
            grid=(N_TOK // SC_ROWS,),
            in_specs=[pl.BlockSpec((SC_ROWS, PACKED), lambda i: (i, 0)),
                      pl.BlockSpec((8, SC_ROWS), lambda i: (0, i))],
            out_specs=[],
            core_axis_name=("c", "s"),
            dimension_semantics=(pltpu.PARALLEL,),
        )(x_hbm, i_hbm)

    return scatter(rows, slot8)


def _sc_gather_rows(table, idx):
    n = idx.shape[1]

    @functools.partial(pl.kernel, mesh=_sc_mesh(), scratch_types=[],
                       out_type=jax.ShapeDtypeStruct((n, PACKED), jnp.int32))
    def gather(t_hbm, i_hbm, o_hbm):
        def body(i_vmem, o_vmem):
            pltpu.sync_copy(t_hbm.at[i_vmem.at[0]], o_vmem)

        pltpu.emit_pipeline(
            body,
            grid=(n // SC_ROWS,),
            in_specs=[pl.BlockSpec((1, SC_ROWS), lambda i: (0, i))],
            out_specs=[pl.BlockSpec((SC_ROWS, PACKED), lambda i: (i, 0))],
            core_axis_name=("c", "s"),
            dimension_semantics=(pltpu.PARALLEL,),
        )(i_hbm, o_hbm)

    return gather(table, idx)


def _stage_g_kernel(x1_ref, mod_ref, g3_ref, yga_ref, ygb_ref, ew_ref, shared_ref, o_ref):
    y = shared_ref[...]
    for k in range(TOP_K):
        y = y + ew_ref[:, k:k + 1] * _unpack_pair(yga_ref[k], ygb_ref[k])
    o_ref[...] = x1_ref[...] + mod_ref[:, 5120:6144] * _rms_rows(y, g3_ref[...])


def _stage_g(x1, modt, g3, yga, ygb, ew_rows, shared):
    row = pl.BlockSpec((TM, D_MODEL), lambda i: (i, 0))
    picked = pl.BlockSpec((TOP_K, TM, PACKED), lambda i: (0, i, 0))
    return pl.pallas_call(
        _stage_g_kernel,
        grid=(NT,),
        in_specs=[row, pl.BlockSpec((None, 1, N_MOD * D_MODEL), lambda i: (i, 0, 0)),
                  _const_spec((1, D_MODEL)), picked, picked,
                  pl.BlockSpec((TM, 8), lambda i: (i, 0)), row],
        out_specs=row,
        out_shape=jax.ShapeDtypeStruct((N_TOK, D_MODEL), F32),
        compiler_params=_cparams(1),
        name="stage_g",
    )(x1, modt, g3, yga, ygb, ew_rows, shared)


def _rope_tables():
    t = jnp.arange(DEC_SEQ)
    pos = jnp.stack([(t // GRID_W).astype(F32), (t % GRID_W).astype(F32)], axis=-1)

    def table(r):
        n_freq = r // 4
        inv = ROPE_BASE ** (-jnp.arange(n_freq, dtype=F32) / n_freq)
        ang = pos[:, :, None] * inv
        cos = jnp.cos(ang)
        sin = jnp.sin(ang)
        cos_t = jnp.stack([cos, cos], axis=2).reshape(DEC_SEQ, r)
        sin_t = jnp.stack([-sin, sin], axis=2).reshape(DEC_SEQ, r)
        return cos_t, sin_t

    c64, s64 = table(SWA_HEAD_DIM)
    c32, s32 = table(MLA_ROPE)
    lat = jnp.concatenate([jnp.tile(c64, (1, 8)), jnp.tile(s64, (1, 8)),
                           jnp.tile(c32, (1, 4)), jnp.tile(s32, (1, 4))], axis=1)
    ident = jnp.concatenate([jnp.ones((TM, 512), F32), jnp.zeros((TM, 512), F32),
                             jnp.ones((TM, 128), F32), jnp.zeros((TM, 128), F32)], axis=1)
    return jnp.concatenate([ident, lat], axis=0)


def _dft_pair(n):
    k = jnp.arange(n, dtype=jnp.int32)
    ang = ((k[:, None] * k[None, :]) % n).astype(F32) * (2.0 * math.pi / n)
    return jnp.cos(ang), jnp.sin(ang)


def _fnet_tables():
    c64, s64 = _dft_pair(FNET_GROUP_DIM)
    eye = jnp.eye(FNET_GROUPS, dtype=F32)
    bd = jnp.concatenate([jnp.kron(eye, c64), jnp.kron(eye, s64)], axis=1).astype(BF16)
    mats = []
    for t_len in (SEQ, DEC_SEQ):
        c, s = _dft_pair(t_len)
        mats.append(jnp.concatenate([c, -s], axis=1).astype(BF16))
    return bd, mats[0], mats[1]


def _swap_perm(width, half):
    return np.arange(width) ^ half


def _layer_weights(l, w_in, w_uq, w_ukv):
    w = w_in[l]
    kr = w[:, 1024:1056]
    sq = w[:, 1056:1568]
    sk = w[:, 1568:1696]
    pad96 = jnp.zeros((D_MODEL, 96), F32)
    wide = jnp.concatenate([
        w[:, 0:1024], sq, sq[:, _swap_perm(512, 16)], sk, sk[:, _swap_perm(128, 16)],
        w[:, 1696:1824], kr, pad96, kr[:, _swap_perm(32, 8)], pad96], axis=1).astype(BF16)

    uq = w_uq[l].reshape(MLA_Q_RANK, MLA_HEADS, MLA_NOPE + MLA_ROPE)
    rope_w = uq[:, :, MLA_NOPE:]
    z32 = jnp.zeros((MLA_Q_RANK, MLA_HEADS, 32), F32)
    z64 = jnp.zeros((MLA_Q_RANK, MLA_HEADS, 64), F32)
    wqa = jnp.concatenate([uq, z32], axis=2).reshape(MLA_Q_RANK, 1024).astype(BF16)
    wqb = jnp.concatenate([z64, rope_w[:, :, _swap_perm(32, 8)], z32], axis=2)
    wqb = wqb.reshape(MLA_Q_RANK, 1024).astype(BF16)
    ukv = w_ukv[l].reshape(MLA_KV_RANK, MLA_HEADS, MLA_NOPE + MLA_V)
    wk = jnp.concatenate([ukv[:, :, :MLA_NOPE], jnp.zeros((MLA_KV_RANK, MLA_HEADS, 64), F32)],
                         axis=2).reshape(MLA_KV_RANK, 1024).astype(BF16)
    wv = ukv[:, :, MLA_NOPE:].reshape(MLA_KV_RANK, 512).astype(BF16)
    return wide, wqa, wqb, wk, wv


def _rope_placement():
    e = np.zeros((128, 1024), np.float32)
    for hd in range(MLA_HEADS):
        for i in range(MLA_ROPE):
            e[i, hd * 128 + MLA_NOPE + i] = 1.0
    return jnp.asarray(e, BF16)


def _moe_dispatch_plan(eidx, epos, counts):
    padded = ((counts + TE - 1) // TE) * TE
    ends = jnp.cumsum(padded)
    offs = ends - padded
    ids = jnp.arange(N_EXPERTS, dtype=jnp.int32)
    picked_off = jnp.sum(jnp.where(eidx[:, :, None] == ids, offs, 0), axis=-1)
    slot = picked_off + epos
    return slot, offs // TE, padded // TE


def kernel(x_prompt, x_sample, cache_mla_ckv, cache_mla_krope, cache_swa_k, cache_swa_v, c, c_ctx,
           ada_w, ada_b, norm_g, w_in, q_norm, kv_norm, w_fnet, w_uq, w_ukv, w_mla_o, swa_sink,
           w_swa_o, w_gate, b_gate, w_out, router_w, router_bias, exp_w1, exp_w3, exp_w2,
           shared_w1, shared_w3, shared_w2):
    x = jnp.concatenate([x_prompt.reshape(N_CTX, D_MODEL), x_sample.reshape(N_LAT, D_MODEL)], axis=0)

    cond8 = jnp.concatenate([c_ctx[None, :], c, jnp.zeros((3, D_MODEL), F32)], axis=0)
    mod = _modulation(cond8, ada_w, ada_b)
    tile_cond = np.concatenate([np.zeros(NT_CTX, np.int32),
                                1 + np.arange(NT_LAT, dtype=np.int32) // LAT_TILES])

    tab = _rope_tables()
    bd, f_ctx, f_lat = _fnet_tables()
    e_mat = _rope_placement()
    tri = jnp.asarray(np.triu(np.ones((TM, TM), np.float32), 1), BF16)

    new_ckv, new_kr, new_k, new_v = [], [], [], []
    for l in range(DEPTH):
        modt = mod[l][tile_cond][:, None, :]
        wide, wqa, wqb, wk, wv = _layer_weights(l, w_in, w_uq, w_ukv)
        ng = norm_g[l]

        fin, ckv, kr, sq, sk, sv, gates, q_m, k_m, v_m = _stage_a(
            x, modt, ng[0:1], wide, w_gate[l].astype(BF16), b_gate[l][None, :],
            q_norm[l][None, :], kv_norm[l][None, :], tab, wqa, wqb, wk, e_mat, wv)

        new_ckv.append(ckv[:N_CTX].reshape(BATCH, SEQ, MLA_KV_RANK))
        new_kr.append(kr[:N_CTX, :MLA_ROPE].reshape(BATCH, SEQ, MLA_ROPE))
        new_k.append(sk[:N_CTX].reshape(BATCH, SEQ, SWA_KV_HEADS, SWA_HEAD_DIM))
        new_v.append(sv[:N_CTX].reshape(BATCH, SEQ, SWA_KV_HEADS, SWA_HEAD_DIM))

        fn = (_fnet(fin, f_ctx, bd, BATCH, SEQ, 0),
              _fnet(fin, f_lat, bd, DEC_BATCH, DEC_SEQ, N_CTX // DEC_SEQ))

        kr_cache = jnp.pad(cache_mla_krope[:, l].reshape(N_CACHE, MLA_ROPE), ((0, 0), (0, 96)))
        k_c, v_c = _mla_cache_kv(cache_mla_ckv[:, l].reshape(N_CACHE, MLA_KV_RANK), kr_cache,
                                 wk, e_mat, wv)
        om = (_mla_attn(q_m, k_m, v_m, k_c, v_c, latent=False),
              _mla_attn(q_m, k_m, v_m, k_c, v_c, latent=True))

        ck = cache_swa_k[:, l].reshape(DEC_BATCH, PAST_LEN, 128)
        cv = cache_swa_v[:, l].reshape(DEC_BATCH, PAST_LEN, 128)
        osw = (_swa_attn(swa_sink[l], sq, sk, sv, ck, cv, latent=False),
               _swa_attn(swa_sink[l], sq, sk, sv, ck, cv, latent=True))

        x1, h2a, h2b = _stage_e(x, modt, ng[1:2], ng[2:3], fn + om + osw, gates,
                                w_fnet[l].astype(BF16), w_mla_o[l].astype(BF16),
                                w_swa_o[l].astype(BF16), w_out[l].astype(BF16))

        shared, eidx, ew, epos, counts = _router(
            h2a, h2b, router_w[l].T.astype(BF16), router_bias[l][:, None],
            shared_w1[l].astype(BF16), shared_w3[l].astype(BF16), shared_w2[l].astype(BF16), tri)
        slot, first_tile, tile_count = _moe_dispatch_plan(eidx, epos, counts[:, 0])
        xsa = _sc_scatter_rows(h2a, slot)
        xsb = _sc_scatter_rows(h2b, slot)
        ysa, ysb = _experts(l, first_tile, tile_count, xsa, xsb, exp_w1, exp_w3, exp_w2)
        picks = slot[:TOP_K].reshape(1, TOP_K * N_TOK)
        yga = _sc_gather_rows(ysa, picks).reshape(TOP_K, N_TOK, PACKED)
        ygb = _sc_gather_rows(ysb, picks).reshape(TOP_K, N_TOK, PACKED)
        x = _stage_g(x1, modt, ng[3:4], yga, ygb, ew.T, shared)

    y_p = x[:N_CTX].reshape(BATCH, SEQ, D_MODEL)
    y_s = x[N_CTX:].reshape(DEC_BATCH, DEC_SEQ, D_MODEL)
    return (y_p, y_s, jnp.stack(new_ckv, axis=1), jnp.stack(new_kr, axis=1),
            jnp.stack(new_k, axis=1), jnp.stack(new_v, axis=1))
```

```python
import functools
import math

import numpy as np
import jax
import jax.numpy as jnp
from jax import lax
from jax.experimental import pallas as pl
from jax.experimental.pallas import tpu as pltpu
from jax.experimental.pallas import tpu_sc as plsc

D_MODEL = 1024
BATCH = 16
SEQ = 256
DEPTH = 2
DEC_BATCH = 4
DEC_SEQ = 2048
PAST_LEN = 512
GRID_W = 64
EPS = 1e-6
ROPE_BASE = 10000.0
NEG_INF = -1e30

FNET_GROUPS = 8
FNET_GROUP_DIM = 64
FNET_WIDTH = 512
MLA_HEADS = 8
MLA_Q_RANK = 384
MLA_KV_RANK = 128
MLA_NOPE = 64
MLA_ROPE = 32
MLA_V = 64
MLA_SCALE = (MLA_NOPE + MLA_ROPE) ** -0.5
LOG2E = math.log2(math.e)
SWA_HEADS = 8
SWA_KV_HEADS = 2
SWA_HEAD_DIM = 64
SWA_WINDOW = 128
SWA_SCALE = SWA_HEAD_DIM ** -0.5
N_MOD = 6
N_EXPERTS = 64
N_EXPERT_GROUPS = 8
TOPK_GROUPS = 4
TOP_K = 6
EXPERT_FF = 256
SHARED_FF = 256
ROUTED_SCALE = 2.5

LANES = 128
TM = 256
N_CTX = BATCH * SEQ
N_LAT = DEC_BATCH * DEC_SEQ
N_TOK = N_CTX + N_LAT
N_CACHE = DEC_BATCH * PAST_LEN
NT_CTX = N_CTX // TM
NT_LAT = N_LAT // TM
NT = N_TOK // TM
LAT_TILES = DEC_SEQ // TM
TE = 512
S_MAX = N_TOK * TOP_K + N_EXPERTS * TE
NTE = S_MAX // TE
EXPERT_ROWS = 256
VMEM_LIMIT = 56 * 1024 * 1024
PACKED = D_MODEL // 4
SC_ROWS = 128
SC_CORES = 2
SC_SUBCORES = 16

A_F = (0, 512)
A_QD = (512, 896)
A_KV = (896, 1024)
A_SQ = (1024, 1536)
A_SQS = (1536, 2048)
A_SK = (2048, 2176)
A_SKS = (2176, 2304)
A_SV = (2304, 2432)
A_KR = (2432, 2560)
A_KRS = (2560, 2688)
W_IN_WIDE = 2688
TAB_W = 1280

F32 = jnp.float32
BF16 = jnp.bfloat16


def _cparams(n_axes, parallel=False):
    sem = ("parallel" if parallel else "arbitrary",) * n_axes
    return pltpu.CompilerParams(dimension_semantics=sem, vmem_limit_bytes=VMEM_LIMIT)


def _dot(a, b):
    return jnp.dot(a, b, preferred_element_type=F32)


def _dot_nt(a, b):
    return lax.dot_general(a, b, (((1,), (1,)), ((), ())), preferred_element_type=F32)


def _rms_rows(v, g):
    return v * lax.rsqrt(jnp.mean(v * v, axis=-1, keepdims=True) + EPS) * g


def _pack_rows(v):
    half = v.shape[1] // 2
    lo = lax.bitcast_convert_type(v[:, :half].astype(BF16).astype(F32), jnp.int32)
    hi = lax.bitcast_convert_type(v[:, half:].astype(BF16).astype(F32), jnp.int32)
    return jnp.bitwise_or(jnp.bitwise_and(hi, -65536), jnp.bitwise_and(jnp.right_shift(lo, 16), 65535))


def _unpack_rows(w):
    lo = lax.bitcast_convert_type(jnp.left_shift(w, 16), F32)
    hi = lax.bitcast_convert_type(jnp.bitwise_and(w, -65536), F32)
    return jnp.concatenate([lo, hi], axis=1)


def _pack_pair(v):
    half = v.shape[1] // 2
    return _pack_rows(v[:, :half]), _pack_rows(v[:, half:])


def _unpack_pair(a, b):
    return jnp.concatenate([_unpack_rows(a), _unpack_rows(b)], axis=1)


def _const_spec(shape):
    return pl.BlockSpec(shape, lambda *_: (0,) * len(shape))


def _tab_row_block(i):
    return jnp.where(i < NT_CTX, 0, 1 + (i - NT_CTX) % LAT_TILES)


def _mod_kernel(cond_ref, w_ref, b_ref, o_ref):
    c = cond_ref[...]
    a = (c * jax.nn.sigmoid(c)).astype(BF16)
    o_ref[...] = _dot(a, w_ref[...].astype(BF16)) + b_ref[...]


def _modulation(cond8, ada_w, ada_b):
    tn = 512
    nj = N_MOD * D_MODEL // tn
    return pl.pallas_call(
        _mod_kernel,
        grid=(DEPTH, nj),
        in_specs=[
            pl.BlockSpec((8, D_MODEL), lambda l, j: (0, 0)),
            pl.BlockSpec((None, D_MODEL, tn), lambda l, j: (l, 0, j)),
            pl.BlockSpec((None, 1, tn), lambda l, j: (l, 0, j)),
        ],
        out_specs=pl.BlockSpec((None, 8, tn), lambda l, j: (l, 0, j)),
        out_shape=jax.ShapeDtypeStruct((DEPTH, 8, N_MOD * D_MODEL), F32),
        compiler_params=_cparams(2),
        name="modulation",
    )(cond8, ada_w, ada_b.reshape(DEPTH, 1, N_MOD * D_MODEL))


def _mla_expand(cq, ckv, kr, cos32, sin32, wqa_ref, wqb_ref, wk_ref, e_ref, wv_ref, q_ref, k_ref, v_ref):
    if q_ref is not None:
        lane = lax.broadcasted_iota(jnp.int32, (1, LANES), 1)
        rope_lane = jnp.logical_and(lane >= MLA_NOPE, lane < MLA_NOPE + MLA_ROPE)
        cos_h = jnp.where(rope_lane, cos32, 1.0)
        for hd in range(MLA_HEADS):
            lo, hi = hd * LANES, (hd + 1) * LANES
            q = _dot(cq, wqa_ref[:, lo:hi]) * cos_h + _dot(cq, wqb_ref[:, lo:hi]) * sin32
            q_ref[:, lo:hi] = (q * (MLA_SCALE * LOG2E)).astype(BF16)
    k_ref[...] = (_dot(ckv, wk_ref[...]) + _dot(kr, e_ref[...])).astype(BF16)
    v_ref[...] = _dot(ckv, wv_ref[...]).astype(BF16)


def _stage_a_kernel(x_ref, mod_ref, g_ref, win_ref, wg_ref, bg_ref, qn_ref, kvn_ref, tab_ref,
                    wqa_ref, wqb_ref, wk_ref, e_ref, wv_ref,
                    fin_ref, ckv_ref, kr_ref, sq_ref, sk_ref, sv_ref, gates_ref,
                    qm_ref, km_ref, vm_ref):
    x = x_ref[...]
    h = (_rms_rows(x, g_ref[...]) * (1.0 + mod_ref[:, 1024:2048]) + mod_ref[:, 0:1024]).astype(BF16)

    def proj(seg):
        return _dot(h, win_ref[:, seg[0]:seg[1]])

    fin_ref[...] = proj(A_F).astype(BF16)
    cq = _rms_rows(proj(A_QD), qn_ref[...]).astype(BF16)
    ckv = _rms_rows(proj(A_KV), kvn_ref[...])
    ckv_ref[...] = ckv
    cos64 = tab_ref[:, 0:512]
    sin64 = tab_ref[:, 512:1024]
    sq = proj(A_SQ) * cos64 + proj(A_SQS) * sin64
    sq_ref[...] = (sq * (SWA_SCALE * LOG2E)).astype(BF16)
    sk_ref[...] = proj(A_SK) * cos64[:, 0:128] + proj(A_SKS) * sin64[:, 0:128]
    sv_ref[...] = proj(A_SV)
    cos32 = tab_ref[:, 1024:1152]
    sin32 = tab_ref[:, 1152:1280]
    kr = proj(A_KR) * cos32 + proj(A_KRS) * sin32
    kr_ref[...] = kr
    _mla_expand(cq, ckv.astype(BF16), kr.astype(BF16), cos32, sin32,
                wqa_ref, wqb_ref, wk_ref, e_ref, wv_ref, qm_ref, km_ref, vm_ref)
    for c in range(3):
        lo, hi = c * D_MODEL, (c + 1) * D_MODEL
        gates_ref[:, lo:hi] = jax.nn.sigmoid(_dot(h, wg_ref[:, lo:hi]) + bg_ref[:, lo:hi]).astype(BF16)


def _stage_a(x, modt, g0, w_in_wide, w_gate, b_gate, q_norm, kv_norm, tab, wqa, wqb, wk, e_mat, wv):
    row = lambda w: pl.BlockSpec((TM, w), lambda i: (i, 0))
    outs = [(512, BF16), (128, F32), (128, F32), (512, BF16), (128, F32), (128, F32),
            (3 * D_MODEL, BF16), (1024, BF16), (1024, BF16), (512, BF16)]
    return pl.pallas_call(
        _stage_a_kernel,
        grid=(NT,),
        in_specs=[
            row(D_MODEL),
            pl.BlockSpec((None, 1, N_MOD * D_MODEL), lambda i: (i, 0, 0)),
            _const_spec((1, D_MODEL)),
            _const_spec((D_MODEL, W_IN_WIDE)),
            _const_spec((D_MODEL, 3 * D_MODEL)),
            _const_spec((1, 3 * D_MODEL)),
            _const_spec((1, MLA_Q_RANK)),
            _const_spec((1, MLA_KV_RANK)),
            pl.BlockSpec((TM, TAB_W), lambda i: (_tab_row_block(i), 0)),
            _const_spec((MLA_Q_RANK, 1024)), _const_spec((MLA_Q_RANK, 1024)),
            _const_spec((128, 1024)), _const_spec((128, 1024)), _const_spec((128, 512)),
        ],
        out_specs=[row(w) for w, _ in outs],
        out_shape=[jax.ShapeDtypeStruct((N_TOK, w), dt) for w, dt in outs],
        compiler_params=_cparams(1),
        name="stage_a",
    )(x, modt, g0, w_in_wide, w_gate, b_gate, q_norm, kv_norm, tab, wqa, wqb, wk, e_mat, wv)


def _fnet_kernel(t_len, scale, fin_ref, f_ref, bd_ref, o_ref, zz_ref):
    @pl.when(pl.program_id(1) == 0)
    def _():
        z = fin_ref[...]
        zz_ref[0:t_len, :] = _dot(z, bd_ref[:, 0:512]).astype(BF16)
        zz_ref[t_len:2 * t_len, :] = _dot(z, bd_ref[:, 512:1024]).astype(BF16)

    o_ref[...] = (_dot(f_ref[...], zz_ref[...]) * scale).astype(BF16)


def _fnet(fin, fmat, bd, n_batch, t_len, row_block0):
    scale = 1.0 / math.sqrt(t_len * FNET_GROUP_DIM)
    return pl.pallas_call(
        functools.partial(_fnet_kernel, t_len, scale),
        grid=(n_batch, t_len // TM),
        in_specs=[
            pl.BlockSpec((t_len, FNET_WIDTH), lambda b, i: (row_block0 + b, 0)),
            pl.BlockSpec((TM, 2 * t_len), lambda b, i: (i, 0)),
            _const_spec((FNET_WIDTH, 2 * FNET_WIDTH)),
        ],
        out_specs=pl.BlockSpec((TM, FNET_WIDTH), lambda b, i: (b * (t_len // TM) + i, 0)),
        out_shape=jax.ShapeDtypeStruct((n_batch * t_len, FNET_WIDTH), BF16),
        scratch_shapes=[pltpu.VMEM((2 * t_len, FNET_WIDTH), BF16)],
        compiler_params=_cparams(2),
        name=f"fnet_{t_len}",
    )(fin, fmat, bd)


def _mla_cache_kernel(ckv_ref, kr_ref, wk_ref, e_ref, wv_ref, k_ref, v_ref):
    _mla_expand(None, ckv_ref[...].astype(BF16), kr_ref[...].astype(BF16), None, None,
                None, None, wk_ref, e_ref, wv_ref, None, k_ref, v_ref)


def _mla_cache_kv(ckv_cache, kr_cache, wk, e_mat, wv):
    row = lambda w: pl.BlockSpec((TM, w), lambda i: (i, 0))
    return pl.pallas_call(
        _mla_cache_kernel,
        grid=(N_CACHE // TM,),
        in_specs=[row(128), row(128),
                  _const_spec((128, 1024)), _const_spec((128, 1024)), _const_spec((128, 512))],
        out_specs=[row(1024), row(512)],
        out_shape=[jax.ShapeDtypeStruct((N_CACHE, 1024), BF16),
                   jax.ShapeDtypeStruct((N_CACHE, 512), BF16)],
        compiler_params=_cparams(1),
        name="mla_cache_kv",
    )(ckv_cache, kr_cache, wk, e_mat, wv)


def _mla_attn_kernel(n_seg, q_ref, *refs):
    k_refs = refs[0:n_seg]
    v_refs = refs[n_seg:2 * n_seg]
    o_ref = refs[2 * n_seg]
    lane = lax.broadcasted_iota(jnp.int32, (1, LANES), 1)
    low = lane < MLA_V
    outs = []
    for hh in range(2):
        q = q_ref[:, hh * LANES:(hh + 1) * LANES]
        ss = [_dot_nt(q, k[:, hh * LANES:(hh + 1) * LANES]) for k in k_refs]
        m = functools.reduce(jnp.maximum, [s.max(axis=-1, keepdims=True) for s in ss])
        keep = low if hh == 0 else jnp.logical_not(low)
        sum_lane = MLA_V if hh == 0 else 0
        po = None
        for s, v_ref in zip(ss, v_refs):
            v = v_ref[...]
            vm = jnp.where(lane == sum_lane, jnp.ones_like(v), jnp.where(keep, v, jnp.zeros_like(v)))
            t = _dot(jnp.exp2(s - m).astype(BF16), vm)
            po = t if po is None else po + t
        outs.append(po / po[:, sum_lane:sum_lane + 1])
    o_ref[...] = jnp.where(low, outs[0], outs[1]).astype(BF16)


def _mla_attn(q_all, k_all, v_all, k_cache, v_cache, latent):
    if latent:
        n_b, n_q = DEC_BATCH, DEC_SEQ // TM
        q0 = NT_CTX
        kv_specs = [
            pl.BlockSpec((PAST_LEN, 256), lambda b, hp, i: (b, hp)),
            pl.BlockSpec((DEC_SEQ, 256), lambda b, hp, i: (N_CTX // DEC_SEQ + b, hp)),
            pl.BlockSpec((PAST_LEN, 128), lambda b, hp, i: (b, hp)),
            pl.BlockSpec((DEC_SEQ, 128), lambda b, hp, i: (N_CTX // DEC_SEQ + b, hp)),
        ]
        args = (q_all, k_cache, k_all, v_cache, v_all)
        n_seg = 2
    else:
        n_b, n_q = BATCH, 1
        q0 = 0
        kv_specs = [
            pl.BlockSpec((SEQ, 256), lambda b, hp, i: (b, hp)),
            pl.BlockSpec((SEQ, 128), lambda b, hp, i: (b, hp)),
        ]
        args = (q_all, k_all, v_all)
        n_seg = 1
    return pl.pallas_call(
        functools.partial(_mla_attn_kernel, n_seg),
        grid=(n_b, MLA_HEADS // 2, n_q),
        in_specs=[pl.BlockSpec((TM, 256), lambda b, hp, i: (q0 + b * n_q + i, hp))] + kv_specs,
        out_specs=pl.BlockSpec((TM, 128), lambda b, hp, i: (b * n_q + i, hp)),
        out_shape=jax.ShapeDtypeStruct((n_b * n_q * TM, MLA_HEADS * MLA_V), BF16),
        compiler_params=_cparams(3),
        name="mla_attn_lat" if latent else "mla_attn_ctx",
    )(*args)


def _swa_kernel(windowed, n_qb, sink_ref, q_ref, *refs):
    n_seg = 4 if windowed else 1
    k_refs = refs[0:n_seg]
    v_refs = refs[n_seg:2 * n_seg]
    o_ref = refs[2 * n_seg]
    tq = q_ref.shape[0]
    qb = pl.program_id(1)
    lane = lax.broadcasted_iota(jnp.int32, (1, LANES), 1)
    low = lane < SWA_HEAD_DIM
    high = jnp.logical_not(low)

    k_all = jnp.concatenate([r[...] for r in k_refs], axis=0)
    v_all = jnp.concatenate([r[...] for r in v_refs], axis=0)
    k_sw = pltpu.roll(k_all, SWA_HEAD_DIM, 1)
    v_sw = pltpu.roll(v_all, SWA_HEAD_DIM, 1)

    if windowed:
        qi = lax.broadcasted_iota(jnp.int32, (2 * tq, SWA_WINDOW), 0) % tq
        kj = lax.broadcasted_iota(jnp.int32, (2 * tq, SWA_WINDOW), 1)
        bias_prev = jnp.where(jnp.logical_and(kj >= qi, qb > 0), 0.0, NEG_INF)
        bias_next = jnp.where(jnp.logical_and(kj <= qi, qb < n_qb - 1), 0.0, NEG_INF)
    top_rows = lax.broadcasted_iota(jnp.int32, (2 * tq, 1), 0) < tq

    for g in range(SWA_KV_HEADS):
        qs = jnp.concatenate([q_ref[:, 256 * g:256 * g + 128],
                              q_ref[:, 256 * g + 128:256 * g + 256]], axis=0)
        halves = []
        for half in range(2):
            keep = low if half == 0 else high
            sum_lane = SWA_HEAD_DIM if half == 0 else 0
            straight = (g == half)
            kh = jnp.where(keep, k_all if straight else k_sw, 0.0).astype(BF16)
            vh = jnp.where(lane == sum_lane, 1.0,
                           jnp.where(keep, v_all if straight else v_sw, 0.0)).astype(BF16)
            s = _dot_nt(qs, kh)
            if windowed:
                c0, c1, c2 = PAST_LEN, PAST_LEN + SWA_WINDOW, PAST_LEN + 2 * SWA_WINDOW
                s = jnp.concatenate([s[:, :c0], s[:, c0:c1] + bias_prev, s[:, c1:c2],
                                     s[:, c2:] + bias_next], axis=1)
            sink = jnp.where(top_rows, sink_ref[4 * g + half], sink_ref[4 * g + 2 + half]) * LOG2E
            m = jnp.maximum(s.max(axis=-1, keepdims=True), sink)
            po = _dot(jnp.exp2(s - m).astype(BF16), vh)
            halves.append(po / (po[:, sum_lane:sum_lane + 1] + jnp.exp2(sink - m)))
        out = jnp.where(low, halves[0], halves[1])
        o_ref[:, 256 * g:256 * g + 128] = out[0:tq].astype(BF16)
        o_ref[:, 256 * g + 128:256 * g + 256] = out[tq:2 * tq].astype(BF16)


def _swa_attn(sink, sq, sk, sv, cache_k, cache_v, latent):
    smem = pl.BlockSpec(memory_space=pltpu.SMEM)
    if latent:
        tq = SWA_WINDOW
        n_b, n_qb = DEC_BATCH, DEC_SEQ // tq
        base = N_CTX // tq

        def prev(b, i):
            return (base + b * n_qb + jnp.maximum(i - 1, 0), 0)

        def cur(b, i):
            return (base + b * n_qb + i, 0)

        def nxt(b, i):
            return (base + b * n_qb + jnp.minimum(i + 1, n_qb - 1), 0)

        cache = pl.BlockSpec((None, PAST_LEN, 128), lambda b, i: (b, 0, 0))
        blk = lambda f: pl.BlockSpec((tq, 128), f)
        kv_specs = [cache, blk(prev), blk(cur), blk(nxt)] * 2
        args = (cache_k, sk, sk, sk, cache_v, sv, sv, sv)
        q_spec = pl.BlockSpec((tq, 512), cur)
        o_spec = pl.BlockSpec((tq, 512), lambda b, i: (b * n_qb + i, 0))
    else:
        tq = SEQ
        n_b, n_qb = BATCH, 1
        blk = pl.BlockSpec((tq, 128), lambda b, i: (b, 0))
        kv_specs = [blk, blk]
        args = (sk, sv)
        q_spec = pl.BlockSpec((tq, 512), lambda b, i: (b, 0))
        o_spec = q_spec
    return pl.pallas_call(
        functools.partial(_swa_kernel, latent, n_qb),
        grid=(n_b, n_qb),
        in_specs=[smem, q_spec] + kv_specs,
        out_specs=o_spec,
        out_shape=jax.ShapeDtypeStruct((n_b * n_qb * tq, 512), BF16),
        compiler_params=_cparams(2),
        name="swa_lat" if latent else "swa_ctx",
    )(sink, sq, *args)


def _stage_e_kernel(x_ref, mod_ref, g1_ref, g2_ref, fnc_ref, fnl_ref, omc_ref, oml_ref, osc_ref, osl_ref,
                    gates_ref, wf_ref, wm_ref, ws_ref, wo_ref, x1_ref, h2a_ref, h2b_ref):
    is_ctx = pl.program_id(0) < NT_CTX
    fn = jnp.where(is_ctx, fnc_ref[...], fnl_ref[...])
    om = jnp.where(is_ctx, omc_ref[...], oml_ref[...])
    osw = jnp.where(is_ctx, osc_ref[...], osl_ref[...])
    merged = (gates_ref[:, 0:1024].astype(F32) * _dot(fn, wf_ref[...])
              + gates_ref[:, 1024:2048].astype(F32) * _dot(om, wm_ref[...])
              + gates_ref[:, 2048:3072].astype(F32) * _dot(osw, ws_ref[...]))
    mix = _dot(merged.astype(BF16), wo_ref[...])
    x1 = x_ref[...] + mod_ref[:, 2048:3072] * _rms_rows(mix, g1_ref[...])
    x1_ref[...] = x1
    h2 = _rms_rows(x1, g2_ref[...]) * (1.0 + mod_ref[:, 4096:5120]) + mod_ref[:, 3072:4096]
    h2a_ref[...], h2b_ref[...] = _pack_pair(h2)


def _stage_e(x, modt, g1, g2, mixed, gates, wf, wm, ws, wo):
    row = lambda w: pl.BlockSpec((TM, w), lambda i: (i, 0))
    ctx = pl.BlockSpec((TM, 512), lambda i: (jnp.minimum(i, NT_CTX - 1), 0))
    lat = pl.BlockSpec((TM, 512), lambda i: (jnp.maximum(i - NT_CTX, 0), 0))
    return pl.pallas_call(
        _stage_e_kernel,
        grid=(NT,),
        in_specs=[
            row(D_MODEL),
            pl.BlockSpec((None, 1, N_MOD * D_MODEL), lambda i: (i, 0, 0)),
            _const_spec((1, D_MODEL)), _const_spec((1, D_MODEL)),
            ctx, lat, ctx, lat, ctx, lat, row(3 * D_MODEL),
            _const_spec((512, D_MODEL)), _const_spec((512, D_MODEL)), _const_spec((512, D_MODEL)),
            _const_spec((D_MODEL, D_MODEL)),
        ],
        out_specs=[row(D_MODEL), row(PACKED), row(PACKED)],
        out_shape=[jax.ShapeDtypeStruct((N_TOK, D_MODEL), F32),
                   jax.ShapeDtypeStruct((N_TOK, PACKED), jnp.int32),
                   jax.ShapeDtypeStruct((N_TOK, PACKED), jnp.int32)],
        compiler_params=_cparams(1),
        name="stage_e",
    )(x, modt, g1, g2, *mixed, gates, wf, wm, ws, wo)


def _router_kernel(ha_ref, hb_ref, rwt_ref, rb_ref, s1_ref, s3_ref, s2_ref, tri_ref,
                   shared_ref, eidx_ref, ew_ref, epos_ref, cnt_ref, carry_ref):
    @pl.when(pl.program_id(0) == 0)
    def _():
        carry_ref[...] = jnp.zeros_like(carry_ref)

    h = _unpack_pair(ha_ref[...], hb_ref[...]).astype(BF16)
    act = jax.nn.silu(_dot(h, s1_ref[...])) * _dot(h, s3_ref[...])
    shared_ref[...] = _dot(act.astype(BF16), s2_ref[...])

    gsz = N_EXPERTS // N_EXPERT_GROUPS
    scores = jax.nn.sigmoid(_dot_nt(rwt_ref[...], h))
    biased = scores + rb_ref[...]
    mem = lax.broadcasted_iota(jnp.int32, (gsz, TM), 0).astype(F32)
    gs_rows = []
    for g in range(N_EXPERT_GROUPS):
        bg = biased[g * gsz:(g + 1) * gsz, :]
        m1 = bg.max(axis=0, keepdims=True)
        first = jnp.min(jnp.where(bg == m1, mem, float(gsz)), axis=0, keepdims=True)
        m2 = jnp.where(mem == first, -jnp.inf, bg).max(axis=0, keepdims=True)
        gs_rows.append(m1 + m2)
    gs = jnp.concatenate(gs_rows, axis=0)
    gid = lax.broadcasted_iota(jnp.int32, gs.shape, 0).astype(F32)
    gsel = jnp.zeros(gs.shape, F32)
    for _ in range(TOPK_GROUPS):
        mx = gs.max(axis=0, keepdims=True)
        pick = gid == jnp.min(jnp.where(gs == mx, gid, float(N_EXPERT_GROUPS)), axis=0, keepdims=True)
        gsel = jnp.where(pick, 1.0, gsel)
        gs = jnp.where(pick, -jnp.inf, gs)
    emask = jnp.concatenate(
        [jnp.broadcast_to(gsel[g:g + 1, :], (gsz, TM)) for g in range(N_EXPERT_GROUPS)], axis=0)
    cand = jnp.where(emask > 0.5, biased, NEG_INF)
    eid = lax.broadcasted_iota(jnp.int32, cand.shape, 0).astype(F32)
    picks = []
    self32 = jnp.zeros(cand.shape, F32)
    for _ in range(TOP_K):
        mx = cand.max(axis=0, keepdims=True)
        pick = eid == jnp.min(jnp.where(cand == mx, eid, float(N_EXPERTS)), axis=0, keepdims=True)
        picks.append(pick)
        self32 = jnp.where(pick, 1.0, self32)
        cand = jnp.where(pick, -jnp.inf, cand)
    pos = _dot(self32.astype(BF16), tri_ref[...]) + carry_ref[...]
    sel_scores = [jnp.sum(jnp.where(p, scores, 0.0), axis=0, keepdims=True) for p in picks]
    wsum = functools.reduce(lambda a, b: a + b, sel_scores)
    zero_f = jnp.zeros((2, TM), F32)
    eidx = [jnp.sum(jnp.where(p, eid, 0.0), axis=0, keepdims=True) for p in picks]
    epos = [jnp.sum(jnp.where(p, pos, 0.0), axis=0, keepdims=True) for p in picks]
    ew = [s / wsum * ROUTED_SCALE for s in sel_scores]
    eidx_ref[...] = jnp.concatenate(eidx + [zero_f], axis=0).astype(jnp.int32)
    epos_ref[...] = jnp.concatenate(epos + [zero_f], axis=0).astype(jnp.int32)
    ew_ref[...] = jnp.concatenate(ew + [zero_f], axis=0)
    total = carry_ref[...] + jnp.sum(self32, axis=1, keepdims=True)
    carry_ref[...] = total
    cnt_ref[...] = jnp.broadcast_to(total, cnt_ref.shape).astype(jnp.int32)


def _router(h2a, h2b, rwt, rbias, s1, s3, s2, tri):
    col = lambda dt: (pl.BlockSpec((8, TM), lambda i: (0, i)), jax.ShapeDtypeStruct((8, N_TOK), dt))
    specs = [col(jnp.int32), col(F32), col(jnp.int32)]
    return pl.pallas_call(
        _router_kernel,
        grid=(NT,),
        in_specs=[
            pl.BlockSpec((TM, PACKED), lambda i: (i, 0)),
            pl.BlockSpec((TM, PACKED), lambda i: (i, 0)),
            _const_spec((N_EXPERTS, D_MODEL)), _const_spec((N_EXPERTS, 1)),
            _const_spec((D_MODEL, SHARED_FF)), _const_spec((D_MODEL, SHARED_FF)),
            _const_spec((SHARED_FF, D_MODEL)), _const_spec((TM, TM)),
        ],
        out_specs=[pl.BlockSpec((TM, D_MODEL), lambda i: (i, 0))] + [s for s, _ in specs]
        + [_const_spec((N_EXPERTS, LANES))],
        out_shape=[jax.ShapeDtypeStruct((N_TOK, D_MODEL), F32)] + [s for _, s in specs]
        + [jax.ShapeDtypeStruct((N_EXPERTS, LANES), jnp.int32)],
        scratch_shapes=[pltpu.VMEM((N_EXPERTS, 1), F32)],
        compiler_params=_cparams(1),
        name="router_shared",
    )(h2a, h2b, rwt, rbias, s1, s3, s2, tri)


def _expert_kernel(te_ref, tv_ref, xa_ref, xb_ref, w1_ref, w3_ref, w2_ref, oa_ref, ob_ref,
                   w1b_ref, w3b_ref, w2b_ref):
    j = pl.program_id(0)
    new_expert = jnp.logical_or(j == 0, te_ref[j] != te_ref[jnp.maximum(j - 1, 0)])

    @pl.when(jnp.logical_and(tv_ref[j] == 1, new_expert))
    def _():
        w1b_ref[...] = w1_ref[...].astype(BF16)
        w3b_ref[...] = w3_ref[...].astype(BF16)
        w2b_ref[...] = w2_ref[...].astype(BF16)

    @pl.when(tv_ref[j] == 1)
    def _():
        for r in range(TE // EXPERT_ROWS):
            rows = pl.ds(r * EXPERT_ROWS, EXPERT_ROWS)
            x = _unpack_pair(xa_ref[rows, :], xb_ref[rows, :]).astype(BF16)
            hg = _dot(x, w1b_ref[...])
            hu = _dot(x, w3b_ref[...])
            act = (jax.nn.silu(hg) * hu).astype(BF16)
            oa_ref[rows, :], ob_ref[rows, :] = _pack_pair(_dot(act, w2b_ref[...]))


def _experts(layer, tile_expert, tile_valid, xsa, xsb, w1, w3, w2):
    slot_rows = pl.BlockSpec((TE, PACKED), lambda j, te, tv: (j, 0))
    grid_spec = pltpu.PrefetchScalarGridSpec(
        num_scalar_prefetch=2,
        grid=(NTE,),
        in_specs=[
            slot_rows, slot_rows,
            pl.BlockSpec((None, None, D_MODEL, EXPERT_FF), lambda j, te, tv: (layer, te[j], 0, 0)),
            pl.BlockSpec((None, None, D_MODEL, EXPERT_FF), lambda j, te, tv: (layer, te[j], 0, 0)),
            pl.BlockSpec((None, None, EXPERT_FF, D_MODEL), lambda j, te, tv: (layer, te[j], 0, 0)),
        ],
        out_specs=[slot_rows, slot_rows],
        scratch_shapes=[pltpu.VMEM((D_MODEL, EXPERT_FF), BF16), pltpu.VMEM((D_MODEL, EXPERT_FF), BF16),
                        pltpu.VMEM((EXPERT_FF, D_MODEL), BF16)],
    )
    return pl.pallas_call(
        _expert_kernel,
        grid_spec=grid_spec,
        out_shape=[jax.ShapeDtypeStruct((S_MAX, PACKED), jnp.int32)] * 2,
        compiler_params=_cparams(1),
        name="experts",
    )(tile_expert, tile_valid, xsa, xsb, w1, w3, w2)


def _sc_mesh():
    return plsc.VectorSubcoreMesh(core_axis_name="c", subcore_axis_name="s",
                                  num_cores=SC_CORES, num_subcores=SC_SUBCORES)


def _sc_scatter_rows(rows, slot8):
    @functools.partial(pl.kernel, mesh=_sc_mesh(), scratch_types=[],
                       out_type=jax.ShapeDtypeStruct((S_MAX, PACKED), jnp.int32))
    def scatter(x_hbm, i_hbm, o_hbm):
        def body(x_vmem, i_vmem):
            for k in range(TOP_K):
                pltpu.sync_copy(x_vmem, o_hbm.at[i_vmem.at[k]])

        pltpu.emit_pipeline(
            body,
            grid=(N_TOK // SC_ROWS,),
            in_specs=[pl.BlockSpec((SC_ROWS, PACKED), lambda i: (i, 0)),
                      pl.BlockSpec((8, SC_ROWS), lambda i: (0, i))],
            out_specs=[],
            core_axis_name=("c", "s"),
            dimension_semantics=(pltpu.PARALLEL,),
        )(x_hbm, i_hbm)

    return scatter(rows, slot8)


def _sc_gather_rows(table, idx):
    n = idx.shape[1]

    @functools.partial(pl.kernel, mesh=_sc_mesh(), scratch_types=[],
                       out_type=jax.ShapeDtypeStruct((n, PACKED), jnp.int32))
    def gather(t_hbm, i_hbm, o_hbm):
        def body(i_vmem, o_vmem):
            pltpu.sync_copy(t_hbm.at[i_vmem.at[0]], o_vmem)

        pltpu.emit_pipeline(
            body,
            grid=(n // SC_ROWS,),
            in_specs=[pl.BlockSpec((1, SC_ROWS), lambda i: (0, i))],
            out_specs=[pl.BlockSpec((SC_ROWS, PACKED), lambda i: (i, 0))],
            core_axis_name=("c", "s"),
            dimension_semantics=(pltpu.PARALLEL,),
        )(i_hbm, o_hbm)

    return gather(table, idx)


def _stage_g_kernel(x1_ref, mod_ref, g3_ref, yga_ref, ygb_ref, ew_ref, shared_ref, o_ref):
    y = shared_ref[...]
    for k in range(TOP_K):
        y = y + ew_ref[:, k:k + 1] * _unpack_pair(yga_ref[k], ygb_ref[k])
    o_ref[...] = x1_ref[...] + mod_ref[:, 5120:6144] * _rms_rows(y, g3_ref[...])


def _stage_g(x1, modt, g3, yga, ygb, ew_rows, shared):
    row = pl.BlockSpec((TM, D_MODEL), lambda i: (i, 0))
    picked = pl.BlockSpec((TOP_K, TM, PACKED), lambda i: (0, i, 0))
    return pl.pallas_call(
        _stage_g_kernel,
        grid=(NT,),
        in_specs=[row, pl.BlockSpec((None, 1, N_MOD * D_MODEL), lambda i: (i, 0, 0)),
                  _const_spec((1, D_MODEL)), picked, picked,
                  pl.BlockSpec((TM, 8), lambda i: (i, 0)), row],
        out_specs=row,
        out_shape=jax.ShapeDtypeStruct((N_TOK, D_MODEL), F32),
        compiler_params=_cparams(1),
        name="stage_g",
    )(x1, modt, g3, yga, ygb, ew_rows, shared)


def _rope_tables():
    t = jnp.arange(DEC_SEQ)
    pos = jnp.stack([(t // GRID_W).astype(F32), (t % GRID_W).astype(F32)], axis=-1)

    def table(r):
        n_freq = r // 4
        inv = ROPE_BASE ** (-jnp.arange(n_freq, dtype=F32) / n_freq)
        ang = pos[:, :, None] * inv
        cos = jnp.cos(ang)
        sin = jnp.sin(ang)
        cos_t = jnp.stack([cos, cos], axis=2).reshape(DEC_SEQ, r)
        sin_t = jnp.stack([-sin, sin], axis=2).reshape(DEC_SEQ, r)
        return cos_t, sin_t

    c64, s64 = table(SWA_HEAD_DIM)
    c32, s32 = table(MLA_ROPE)
    lat = jnp.concatenate([jnp.tile(c64, (1, 8)), jnp.tile(s64, (1, 8)),
                           jnp.tile(c32, (1, 4)), jnp.tile(s32, (1, 4))], axis=1)
    ident = jnp.concatenate([jnp.ones((TM, 512), F32), jnp.zeros((TM, 512), F32),
                             jnp.ones((TM, 128), F32), jnp.zeros((TM, 128), F32)], axis=1)
    return jnp.concatenate([ident, lat], axis=0)


def _dft_pair(n):
    k = jnp.arange(n, dtype=jnp.int32)
    ang = ((k[:, None] * k[None, :]) % n).astype(F32) * (2.0 * math.pi / n)
    return jnp.cos(ang), jnp.sin(ang)


def _fnet_tables():
    c64, s64 = _dft_pair(FNET_GROUP_DIM)
    eye = jnp.eye(FNET_GROUPS, dtype=F32)
    bd = jnp.concatenate([jnp.kron(eye, c64), jnp.kron(eye, s64)], axis=1).astype(BF16)
    mats = []
    for t_len in (SEQ, DEC_SEQ):
        c, s = _dft_pair(t_len)
        mats.append(jnp.concatenate([c, -s], axis=1).astype(BF16))
    return bd, mats[0], mats[1]


def _swap_perm(width, half):
    return np.arange(width) ^ half


def _layer_weights(l, w_in, w_uq, w_ukv):
    w = w_in[l]
    kr = w[:, 1024:1056]
    sq = w[:, 1056:1568]
    sk = w[:, 1568:1696]
    pad96 = jnp.zeros((D_MODEL, 96), F32)
    wide = jnp.concatenate([
        w[:, 0:1024], sq, sq[:, _swap_perm(512, 16)], sk, sk[:, _swap_perm(128, 16)],
        w[:, 1696:1824], kr, pad96, kr[:, _swap_perm(32, 8)], pad96], axis=1).astype(BF16)

    uq = w_uq[l].reshape(MLA_Q_RANK, MLA_HEADS, MLA_NOPE + MLA_ROPE)
    rope_w = uq[:, :, MLA_NOPE:]
    z32 = jnp.zeros((MLA_Q_RANK, MLA_HEADS, 32), F32)
    z64 = jnp.zeros((MLA_Q_RANK, MLA_HEADS, 64), F32)
    wqa = jnp.concatenate([uq, z32], axis=2).reshape(MLA_Q_RANK, 1024).astype(BF16)
    wqb = jnp.concatenate([z64, rope_w[:, :, _swap_perm(32, 8)], z32], axis=2)
    wqb = wqb.reshape(MLA_Q_RANK, 1024).astype(BF16)
    ukv = w_ukv[l].reshape(MLA_KV_RANK, MLA_HEADS, MLA_NOPE + MLA_V)
    wk = jnp.concatenate([ukv[:, :, :MLA_NOPE], jnp.zeros((MLA_KV_RANK, MLA_HEADS, 64), F32)],
                         axis=2).reshape(MLA_KV_RANK, 1024).astype(BF16)
    wv = ukv[:, :, MLA_NOPE:].reshape(MLA_KV_RANK, 512).astype(BF16)
    return wide, wqa, wqb, wk, wv


def _rope_placement():
    e = np.zeros((128, 1024), np.float32)
    for hd in range(MLA_HEADS):
        for i in range(MLA_ROPE):
            e[i, hd * 128 + MLA_NOPE + i] = 1.0
    return jnp.asarray(e, BF16)


def _moe_dispatch_plan(eidx, epos, counts):
    padded = ((counts + TE - 1) // TE) * TE
    ends = jnp.cumsum(padded)
    offs = ends - padded
    ids = jnp.arange(N_EXPERTS, dtype=jnp.int32)
    picked_off = jnp.sum(jnp.where(eidx[:, :, None] == ids, offs, 0), axis=-1)
    slot = picked_off + epos
    starts = jnp.arange(NTE, dtype=jnp.int32) * TE
    tile_expert = jnp.sum((ends[None, :] <= starts[:, None]).astype(jnp.int32), axis=1)
    tile_expert = jnp.minimum(tile_expert, N_EXPERTS - 1)
    tile_valid = (starts < ends[-1]).astype(jnp.int32)
    return slot, tile_expert, tile_valid


def kernel(x_prompt, x_sample, cache_mla_ckv, cache_mla_krope, cache_swa_k, cache_swa_v, c, c_ctx,
           ada_w, ada_b, norm_g, w_in, q_norm, kv_norm, w_fnet, w_uq, w_ukv, w_mla_o, swa_sink,
           w_swa_o, w_gate, b_gate, w_out, router_w, router_bias, exp_w1, exp_w3, exp_w2,
           shared_w1, shared_w3, shared_w2):
    x = jnp.concatenate([x_prompt.reshape(N_CTX, D_MODEL), x_sample.reshape(N_LAT, D_MODEL)], axis=0)

    cond8 = jnp.concatenate([c_ctx[None, :], c, jnp.zeros((3, D_MODEL), F32)], axis=0)
    mod = _modulation(cond8, ada_w, ada_b)
    tile_cond = np.concatenate([np.zeros(NT_CTX, np.int32),
                                1 + np.arange(NT_LAT, dtype=np.int32) // LAT_TILES])

    tab = _rope_tables()
    bd, f_ctx, f_lat = _fnet_tables()
    e_mat = _rope_placement()
    tri = jnp.asarray(np.triu(np.ones((TM, TM), np.float32), 1), BF16)

    new_ckv, new_kr, new_k, new_v = [], [], [], []
    for l in range(DEPTH):
        modt = mod[l][tile_cond][:, None, :]
        wide, wqa, wqb, wk, wv = _layer_weights(l, w_in, w_uq, w_ukv)
        ng = norm_g[l]

        fin, ckv, kr, sq, sk, sv, gates, q_m, k_m, v_m = _stage_a(
            x, modt, ng[0:1], wide, w_gate[l].astype(BF16), b_gate[l][None, :],
            q_norm[l][None, :], kv_norm[l][None, :], tab, wqa, wqb, wk, e_mat, wv)

        new_ckv.append(ckv[:N_CTX].reshape(BATCH, SEQ, MLA_KV_RANK))
        new_kr.append(kr[:N_CTX, :MLA_ROPE].reshape(BATCH, SEQ, MLA_ROPE))
        new_k.append(sk[:N_CTX].reshape(BATCH, SEQ, SWA_KV_HEADS, SWA_HEAD_DIM))
        new_v.append(sv[:N_CTX].reshape(BATCH, SEQ, SWA_KV_HEADS, SWA_HEAD_DIM))

        fn = (_fnet(fin, f_ctx, bd, BATCH, SEQ, 0),
              _fnet(fin, f_lat, bd, DEC_BATCH, DEC_SEQ, N_CTX // DEC_SEQ))

        kr_cache = jnp.pad(cache_mla_krope[:, l].reshape(N_CACHE, MLA_ROPE), ((0, 0), (0, 96)))
        k_c, v_c = _mla_cache_kv(cache_mla_ckv[:, l].reshape(N_CACHE, MLA_KV_RANK), kr_cache,
                                 wk, e_mat, wv)
        om = (_mla_attn(q_m, k_m, v_m, k_c, v_c, latent=False),
              _mla_attn(q_m, k_m, v_m, k_c, v_c, latent=True))

        ck = cache_swa_k[:, l].reshape(DEC_BATCH, PAST_LEN, 128)
        cv = cache_swa_v[:, l].reshape(DEC_BATCH, PAST_LEN, 128)
        osw = (_swa_attn(swa_sink[l], sq, sk, sv, ck, cv, latent=False),
               _swa_attn(swa_sink[l], sq, sk, sv, ck, cv, latent=True))

        x1, h2a, h2b = _stage_e(x, modt, ng[1:2], ng[2:3], fn + om + osw, gates,
                                w_fnet[l].astype(BF16), w_mla_o[l].astype(BF16),
                                w_swa_o[l].astype(BF16), w_out[l].astype(BF16))

        shared, eidx, ew, epos, counts = _router(
            h2a, h2b, router_w[l].T.astype(BF16), router_bias[l][:, None],
            shared_w1[l].astype(BF16), shared_w3[l].astype(BF16), shared_w2[l].astype(BF16), tri)
        slot, tile_expert, tile_valid = _moe_dispatch_plan(eidx, epos, counts[:, 0])
        xsa = _sc_scatter_rows(h2a, slot)
        xsb = _sc_scatter_rows(h2b, slot)
        ysa, ysb = _experts(l, tile_expert, tile_valid, xsa, xsb, exp_w1, exp_w3, exp_w2)
        picks = slot[:TOP_K].reshape(1, TOP_K * N_TOK)
        yga = _sc_gather_rows(ysa, picks).reshape(TOP_K, N_TOK, PACKED)
        ygb = _sc_gather_rows(ysb, picks).reshape(TOP_K, N_TOK, PACKED)
        x = _stage_g(x1, modt, ng[3:4], yga, ygb, ew.T, shared)

    y_p = x[:N_CTX].reshape(BATCH, SEQ, D_MODEL)
    y_s = x[N_CTX:].reshape(DEC_BATCH, DEC_SEQ, D_MODEL)
    return (y_p, y_s, jnp.stack(new_ckv, axis=1), jnp.stack(new_kr, axis=1),
            jnp.stack(new_k, axis=1), jnp.stack(new_v, axis=1))
```

```python
import functools
import math

import numpy as np
import jax
import jax.numpy as jnp
from jax import lax
from jax.experimental import pallas as pl
from jax.experimental.pallas import tpu as pltpu
from jax.experimental.pallas import tpu_sc as plsc

D_MODEL = 1024
BATCH = 16
SEQ = 256
DEPTH = 2
DEC_BATCH = 4
DEC_SEQ = 2048
PAST_LEN = 512
GRID_W = 64
EPS = 1e-6
ROPE_BASE = 10000.0
NEG_INF = -1e30

FNET_GROUPS = 8
FNET_GROUP_DIM = 64
FNET_WIDTH = 512
MLA_HEADS = 8
MLA_Q_RANK = 384
MLA_KV_RANK = 128
MLA_NOPE = 64
MLA_ROPE = 32
MLA_V = 64
MLA_SCALE = (MLA_NOPE + MLA_ROPE) ** -0.5
LOG2E = math.log2(math.e)
SWA_HEADS = 8
SWA_KV_HEADS = 2
SWA_HEAD_DIM = 64
SWA_WINDOW = 128
SWA_SCALE = SWA_HEAD_DIM ** -0.5
N_MOD = 6
N_EXPERTS = 64
N_EXPERT_GROUPS = 8
TOPK_GROUPS = 4
TOP_K = 6
EXPERT_FF = 256
SHARED_FF = 256
ROUTED_SCALE = 2.5

LANES = 128
TM = 256
N_CTX = BATCH * SEQ
N_LAT = DEC_BATCH * DEC_SEQ
N_TOK = N_CTX + N_LAT
N_CACHE = DEC_BATCH * PAST_LEN
NT_CTX = N_CTX // TM
NT_LAT = N_LAT // TM
NT = N_TOK // TM
LAT_TILES = DEC_SEQ // TM
TB = 512
NB = N_TOK // TB
NB_CTX = N_CTX // TB
LAT_BLOCKS = DEC_SEQ // TB
TE = 512
S_MAX = N_TOK * TOP_K + N_EXPERTS * TE
NTE = S_MAX // TE
EXPERT_ROWS = 256
VMEM_LIMIT = 56 * 1024 * 1024
PACKED = D_MODEL // 4
SC_ROWS = 128
SC_CORES = 2
SC_SUBCORES = 16

A_F = (0, 512)
A_QD = (512, 896)
A_KV = (896, 1024)
A_SQ = (1024, 1536)
A_SQS = (1536, 2048)
A_SK = (2048, 2176)
A_SKS = (2176, 2304)
A_SV = (2304, 2432)
A_KR = (2432, 2560)
A_KRS = (2560, 2688)
W_IN_WIDE = 2688
TAB_W = 1280

F32 = jnp.float32
BF16 = jnp.bfloat16


def _cparams(n_axes, parallel=False):
    sem = ("parallel" if parallel else "arbitrary",) * n_axes
    return pltpu.CompilerParams(dimension_semantics=sem, vmem_limit_bytes=VMEM_LIMIT)


def _dot(a, b):
    return jnp.dot(a, b, preferred_element_type=F32)


def _dot_nt(a, b):
    return lax.dot_general(a, b, (((1,), (1,)), ((), ())), preferred_element_type=F32)


def _rms_rows(v, g):
    return v * lax.rsqrt(jnp.mean(v * v, axis=-1, keepdims=True) + EPS) * g


def _pack_rows(v):
    half = v.shape[1] // 2
    lo = lax.bitcast_convert_type(v[:, :half].astype(BF16).astype(F32), jnp.int32)
    hi = lax.bitcast_convert_type(v[:, half:].astype(BF16).astype(F32), jnp.int32)
    return jnp.bitwise_or(jnp.bitwise_and(hi, -65536), jnp.bitwise_and(jnp.right_shift(lo, 16), 65535))


def _unpack_rows(w):
    lo = lax.bitcast_convert_type(jnp.left_shift(w, 16), F32)
    hi = lax.bitcast_convert_type(jnp.bitwise_and(w, -65536), F32)
    return jnp.concatenate([lo, hi], axis=1)


def _pack_pair(v):
    half = v.shape[1] // 2
    return _pack_rows(v[:, :half]), _pack_rows(v[:, half:])


def _unpack_pair(a, b):
    return jnp.concatenate([_unpack_rows(a), _unpack_rows(b)], axis=1)


def _const_spec(shape):
    return pl.BlockSpec(shape, lambda *_: (0,) * len(shape))


def _tab_row_block(i):
    return jnp.where(i < NB_CTX, 0, 1 + (i - NB_CTX) % LAT_BLOCKS)


def _mod_kernel(cond_ref, w_ref, b_ref, o_ref):
    c = cond_ref[...]
    a = (c * jax.nn.sigmoid(c)).astype(BF16)
    o_ref[...] = _dot(a, w_ref[...].astype(BF16)) + b_ref[...]


def _modulation(cond8, ada_w, ada_b):
    tn = 512
    nj = N_MOD * D_MODEL // tn
    return pl.pallas_call(
        _mod_kernel,
        grid=(DEPTH, nj),
        in_specs=[
            pl.BlockSpec((8, D_MODEL), lambda l, j: (0, 0)),
            pl.BlockSpec((None, D_MODEL, tn), lambda l, j: (l, 0, j)),
            pl.BlockSpec((None, 1, tn), lambda l, j: (l, 0, j)),
        ],
        out_specs=pl.BlockSpec((None, 8, tn), lambda l, j: (l, 0, j)),
        out_shape=jax.ShapeDtypeStruct((DEPTH, 8, N_MOD * D_MODEL), F32),
        compiler_params=_cparams(2),
        name="modulation",
    )(cond8, ada_w, ada_b.reshape(DEPTH, 1, N_MOD * D_MODEL))


def _mla_expand(rows, cq, ckv, kr, cos32, sin32, wqa_ref, wqb_ref, wk_ref, e_ref, wv_ref,
                q_ref, k_ref, v_ref):
    if q_ref is not None:
        lane = lax.broadcasted_iota(jnp.int32, (1, LANES), 1)
        rope_lane = jnp.logical_and(lane >= MLA_NOPE, lane < MLA_NOPE + MLA_ROPE)
        cos_h = jnp.where(rope_lane, cos32, 1.0)
        for hd in range(MLA_HEADS):
            lo, hi = hd * LANES, (hd + 1) * LANES
            q = _dot(cq, wqa_ref[:, lo:hi]) * cos_h + _dot(cq, wqb_ref[:, lo:hi]) * sin32
            q_ref[rows, lo:hi] = (q * (MLA_SCALE * LOG2E)).astype(BF16)
    k_ref[rows, :] = (_dot(ckv, wk_ref[...]) + _dot(kr, e_ref[...])).astype(BF16)
    v_ref[rows, :] = _dot(ckv, wv_ref[...]).astype(BF16)


def _chunks():
    return [pl.ds(r * TM, TM) for r in range(TB // TM)]


def _stage_a_kernel(x_ref, mod_ref, g_ref, win_ref, wg_ref, bg_ref, qn_ref, kvn_ref, tab_ref,
                    wqa_ref, wqb_ref, wk_ref, e_ref, wv_ref,
                    fin_ref, ckv_ref, kr_ref, sq_ref, sk_ref, sv_ref, gates_ref,
                    qm_ref, km_ref, vm_ref):
    for rows in _chunks():
        x = x_ref[rows, :]
        h = (_rms_rows(x, g_ref[...]) * (1.0 + mod_ref[:, 1024:2048]) + mod_ref[:, 0:1024]).astype(BF16)

        def proj(seg):
            return _dot(h, win_ref[:, seg[0]:seg[1]])

        fin_ref[rows, :] = proj(A_F).astype(BF16)
        cq = _rms_rows(proj(A_QD), qn_ref[...]).astype(BF16)
        ckv = _rms_rows(proj(A_KV), kvn_ref[...])
        ckv_ref[rows, :] = ckv
        cos64 = tab_ref[rows, 0:512]
        sin64 = tab_ref[rows, 512:1024]
        sq = proj(A_SQ) * cos64 + proj(A_SQS) * sin64
        sq_ref[rows, :] = (sq * (SWA_SCALE * LOG2E)).astype(BF16)
        sk_ref[rows, :] = proj(A_SK) * cos64[:, 0:128] + proj(A_SKS) * sin64[:, 0:128]
        sv_ref[rows, :] = proj(A_SV)
        cos32 = tab_ref[rows, 1024:1152]
        sin32 = tab_ref[rows, 1152:1280]
        kr = proj(A_KR) * cos32 + proj(A_KRS) * sin32
        kr_ref[rows, :] = kr
        _mla_expand(rows, cq, ckv.astype(BF16), kr.astype(BF16), cos32, sin32,
                    wqa_ref, wqb_ref, wk_ref, e_ref, wv_ref, qm_ref, km_ref, vm_ref)
        for c in range(3):
            lo, hi = c * D_MODEL, (c + 1) * D_MODEL
            gates_ref[rows, lo:hi] = jax.nn.sigmoid(_dot(h, wg_ref[:, lo:hi]) + bg_ref[:, lo:hi]).astype(BF16)


def _stage_a(x, modt, g0, w_in_wide, w_gate, b_gate, q_norm, kv_norm, tab, wqa, wqb, wk, e_mat, wv):
    row = lambda w: pl.BlockSpec((TB, w), lambda i: (i, 0))
    outs = [(512, BF16), (128, F32), (128, F32), (512, BF16), (128, F32), (128, F32),
            (3 * D_MODEL, BF16), (1024, BF16), (1024, BF16), (512, BF16)]
    return pl.pallas_call(
        _stage_a_kernel,
        grid=(NB,),
        in_specs=[
            row(D_MODEL),
            pl.BlockSpec((None, 1, N_MOD * D_MODEL), lambda i: (i, 0, 0)),
            _const_spec((1, D_MODEL)),
            _const_spec((D_MODEL, W_IN_WIDE)),
            _const_spec((D_MODEL, 3 * D_MODEL)),
            _const_spec((1, 3 * D_MODEL)),
            _const_spec((1, MLA_Q_RANK)),
            _const_spec((1, MLA_KV_RANK)),
            pl.BlockSpec((TB, TAB_W), lambda i: (_tab_row_block(i), 0)),
            _const_spec((MLA_Q_RANK, 1024)), _const_spec((MLA_Q_RANK, 1024)),
            _const_spec((128, 1024)), _const_spec((128, 1024)), _const_spec((128, 512)),
        ],
        out_specs=[row(w) for w, _ in outs],
        out_shape=[jax.ShapeDtypeStruct((N_TOK, w), dt) for w, dt in outs],
        compiler_params=_cparams(1),
        name="stage_a",
    )(x, modt, g0, w_in_wide, w_gate, b_gate, q_norm, kv_norm, tab, wqa, wqb, wk, e_mat, wv)


def _fnet_kernel(t_len, scale, fin_ref, f_ref, bd_ref, o_ref, zz_ref):
    @pl.when(pl.program_id(1) == 0)
    def _():
        z = fin_ref[...]
        zz_ref[0:t_len, :] = _dot(z, bd_ref[:, 0:512]).astype(BF16)
        zz_ref[t_len:2 * t_len, :] = _dot(z, bd_ref[:, 512:1024]).astype(BF16)

    o_ref[...] = (_dot(f_ref[...], zz_ref[...]) * scale).astype(BF16)


def _fnet(fin, fmat, bd, n_batch, t_len, row_block0):
    scale = 1.0 / math.sqrt(t_len * FNET_GROUP_DIM)
    return pl.pallas_call(
        functools.partial(_fnet_kernel, t_len, scale),
        grid=(n_batch, t_len // TM),
        in_specs=[
            pl.BlockSpec((t_len, FNET_WIDTH), lambda b, i: (row_block0 + b, 0)),
            pl.BlockSpec((TM, 2 * t_len), lambda b, i: (i, 0)),
            _const_spec((FNET_WIDTH, 2 * FNET_WIDTH)),
        ],
        out_specs=pl.BlockSpec((TM, FNET_WIDTH), lambda b, i: (b * (t_len // TM) + i, 0)),
        out_shape=jax.ShapeDtypeStruct((n_batch * t_len, FNET_WIDTH), BF16),
        scratch_shapes=[pltpu.VMEM((2 * t_len, FNET_WIDTH), BF16)],
        compiler_params=_cparams(2),
        name=f"fnet_{t_len}",
    )(fin, fmat, bd)


def _mla_cache_kernel(ckv_ref, kr_ref, wk_ref, e_ref, wv_ref, k_ref, v_ref):
    _mla_expand(slice(None), None, ckv_ref[...].astype(BF16), kr_ref[...].astype(BF16), None, None,
                None, None, wk_ref, e_ref, wv_ref, None, k_ref, v_ref)


def _mla_cache_kv(ckv_cache, kr_cache, wk, e_mat, wv):
    row = lambda w: pl.BlockSpec((TM, w), lambda i: (i, 0))
    return pl.pallas_call(
        _mla_cache_kernel,
        grid=(N_CACHE // TM,),
        in_specs=[row(128), row(128),
                  _const_spec((128, 1024)), _const_spec((128, 1024)), _const_spec((128, 512))],
        out_specs=[row(1024), row(512)],
        out_shape=[jax.ShapeDtypeStruct((N_CACHE, 1024), BF16),
                   jax.ShapeDtypeStruct((N_CACHE, 512), BF16)],
        compiler_params=_cparams(1),
        name="mla_cache_kv",
    )(ckv_cache, kr_cache, wk, e_mat, wv)


def _mla_attn_kernel(n_seg, q_ref, *refs):
    k_refs = refs[0:n_seg]
    v_refs = refs[n_seg:2 * n_seg]
    o_ref = refs[2 * n_seg]
    lane = lax.broadcasted_iota(jnp.int32, (1, LANES), 1)
    low = lane < MLA_V
    outs = []
    for hh in range(2):
        q = q_ref[:, hh * LANES:(hh + 1) * LANES]
        ss = [_dot_nt(q, k[:, hh * LANES:(hh + 1) * LANES]) for k in k_refs]
        m = functools.reduce(jnp.maximum, [s.max(axis=-1, keepdims=True) for s in ss])
        keep = low if hh == 0 else jnp.logical_not(low)
        sum_lane = MLA_V if hh == 0 else 0
        po = None
        for s, v_ref in zip(ss, v_refs):
            v = v_ref[...]
            vm = jnp.where(lane == sum_lane, jnp.ones_like(v), jnp.where(keep, v, jnp.zeros_like(v)))
            t = _dot(jnp.exp2(s - m).astype(BF16), vm)
            po = t if po is None else po + t
        outs.append(po / po[:, sum_lane:sum_lane + 1])
    o_ref[...] = jnp.where(low, outs[0], outs[1]).astype(BF16)


def _mla_attn(q_all, k_all, v_all, k_cache, v_cache, latent):
    if latent:
        n_b, n_q = DEC_BATCH, DEC_SEQ // TM
        q0 = NT_CTX
        kv_specs = [
            pl.BlockSpec((PAST_LEN, 256), lambda b, hp, i: (b, hp)),
            pl.BlockSpec((DEC_SEQ, 256), lambda b, hp, i: (N_CTX // DEC_SEQ + b, hp)),
            pl.BlockSpec((PAST_LEN, 128), lambda b, hp, i: (b, hp)),
            pl.BlockSpec((DEC_SEQ, 128), lambda b, hp, i: (N_CTX // DEC_SEQ + b, hp)),
        ]
        args = (q_all, k_cache, k_all, v_cache, v_all)
        n_seg = 2
    else:
        n_b, n_q = BATCH, 1
        q0 = 0
        kv_specs = [
            pl.BlockSpec((SEQ, 256), lambda b, hp, i: (b, hp)),
            pl.BlockSpec((SEQ, 128), lambda b, hp, i: (b, hp)),
        ]
        args = (q_all, k_all, v_all)
        n_seg = 1
    return pl.pallas_call(
        functools.partial(_mla_attn_kernel, n_seg),
        grid=(n_b, MLA_HEADS // 2, n_q),
        in_specs=[pl.BlockSpec((TM, 256), lambda b, hp, i: (q0 + b * n_q + i, hp))] + kv_specs,
        out_specs=pl.BlockSpec((TM, 128), lambda b, hp, i: (b * n_q + i, hp)),
        out_shape=jax.ShapeDtypeStruct((n_b * n_q * TM, MLA_HEADS * MLA_V), BF16),
        compiler_params=_cparams(3),
        name="mla_attn_lat" if latent else "mla_attn_ctx",
    )(*args)


def _swa_kernel(windowed, n_qb, sink_ref, q_ref, *refs):
    n_seg = 4 if windowed else 1
    k_refs = refs[0:n_seg]
    v_refs = refs[n_seg:2 * n_seg]
    o_ref = refs[2 * n_seg]
    tq = q_ref.shape[0]
    qb = pl.program_id(1)
    lane = lax.broadcasted_iota(jnp.int32, (1, LANES), 1)
    low = lane < SWA_HEAD_DIM
    high = jnp.logical_not(low)

    k_all = jnp.concatenate([r[...] for r in k_refs], axis=0)
    v_all = jnp.concatenate([r[...] for r in v_refs], axis=0)
    k_sw = pltpu.roll(k_all, SWA_HEAD_DIM, 1)
    v_sw = pltpu.roll(v_all, SWA_HEAD_DIM, 1)

    if windowed:
        qi = lax.broadcasted_iota(jnp.int32, (2 * tq, SWA_WINDOW), 0) % tq
        kj = lax.broadcasted_iota(jnp.int32, (2 * tq, SWA_WINDOW), 1)
        bias_prev = jnp.where(jnp.logical_and(kj >= qi, qb > 0), 0.0, NEG_INF)
        bias_next = jnp.where(jnp.logical_and(kj <= qi, qb < n_qb - 1), 0.0, NEG_INF)
    top_rows = lax.broadcasted_iota(jnp.int32, (2 * tq, 1), 0) < tq

    for g in range(SWA_KV_HEADS):
        qs = jnp.concatenate([q_ref[:, 256 * g:256 * g + 128],
                              q_ref[:, 256 * g + 128:256 * g + 256]], axis=0)
        halves = []
        for half in range(2):
            keep = low if half == 0 else high
            sum_lane = SWA_HEAD_DIM if half == 0 else 0
            straight = (g == half)
            kh = jnp.where(keep, k_all if straight else k_sw, 0.0).astype(BF16)
            vh = jnp.where(lane == sum_lane, 1.0,
                           jnp.where(keep, v_all if straight else v_sw, 0.0)).astype(BF16)
            s = _dot_nt(qs, kh)
            if windowed:
                c0, c1, c2 = PAST_LEN, PAST_LEN + SWA_WINDOW, PAST_LEN + 2 * SWA_WINDOW
                s = jnp.concatenate([s[:, :c0], s[:, c0:c1] + bias_prev, s[:, c1:c2],
                                     s[:, c2:] + bias_next], axis=1)
            sink = jnp.where(top_rows, sink_ref[4 * g + half], sink_ref[4 * g + 2 + half]) * LOG2E
            m = jnp.maximum(s.max(axis=-1, keepdims=True), sink)
            po = _dot(jnp.exp2(s - m).astype(BF16), vh)
            halves.append(po / (po[:, sum_lane:sum_lane + 1] + jnp.exp2(sink - m)))
        out = jnp.where(low, halves[0], halves[1])
        o_ref[:, 256 * g:256 * g + 128] = out[0:tq].astype(BF16)
        o_ref[:, 256 * g + 128:256 * g + 256] = out[tq:2 * tq].astype(BF16)


def _swa_attn(sink, sq, sk, sv, cache_k, cache_v, latent):
    smem = pl.BlockSpec(memory_space=pltpu.SMEM)
    if latent:
        tq = SWA_WINDOW
        n_b, n_qb = DEC_BATCH, DEC_SEQ // tq
        base = N_CTX // tq

        def prev(b, i):
            return (base + b * n_qb + jnp.maximum(i - 1, 0), 0)

        def cur(b, i):
            return (base + b * n_qb + i, 0)

        def nxt(b, i):
            return (base + b * n_qb + jnp.minimum(i + 1, n_qb - 1), 0)

        cache = pl.BlockSpec((None, PAST_LEN, 128), lambda b, i: (b, 0, 0))
        blk = lambda f: pl.BlockSpec((tq, 128), f)
        kv_specs = [cache, blk(prev), blk(cur), blk(nxt)] * 2
        args = (cache_k, sk, sk, sk, cache_v, sv, sv, sv)
        q_spec = pl.BlockSpec((tq, 512), cur)
        o_spec = pl.BlockSpec((tq, 512), lambda b, i: (b * n_qb + i, 0))
    else:
        tq = SEQ
        n_b, n_qb = BATCH, 1
        blk = pl.BlockSpec((tq, 128), lambda b, i: (b, 0))
        kv_specs = [blk, blk]
        args = (sk, sv)
        q_spec = pl.BlockSpec((tq, 512), lambda b, i: (b, 0))
        o_spec = q_spec
    return pl.pallas_call(
        functools.partial(_swa_kernel, latent, n_qb),
        grid=(n_b, n_qb),
        in_specs=[smem, q_spec] + kv_specs,
        out_specs=o_spec,
        out_shape=jax.ShapeDtypeStruct((n_b * n_qb * tq, 512), BF16),
        compiler_params=_cparams(2),
        name="swa_lat" if latent else "swa_ctx",
    )(sink, sq, *args)


def _route(h, rwt_ref, rb_ref, tri_ref, carry):
    gsz = N_EXPERTS // N_EXPERT_GROUPS
    scores = jax.nn.sigmoid(_dot_nt(rwt_ref[...], h))
    biased = scores + rb_ref[...]
    mem = lax.broadcasted_iota(jnp.int32, (gsz, TM), 0).astype(F32)
    gs_rows = []
    for g in range(N_EXPERT_GROUPS):
        bg = biased[g * gsz:(g + 1) * gsz, :]
        m1 = bg.max(axis=0, keepdims=True)
        first = jnp.min(jnp.where(bg == m1, mem, float(gsz)), axis=0, keepdims=True)
        m2 = jnp.where(mem == first, -jnp.inf, bg).max(axis=0, keepdims=True)
        gs_rows.append(m1 + m2)
    gs = jnp.concatenate(gs_rows, axis=0)
    gid = lax.broadcasted_iota(jnp.int32, gs.shape, 0).astype(F32)
    gsel = jnp.zeros(gs.shape, F32)
    for _ in range(TOPK_GROUPS):
        mx = gs.max(axis=0, keepdims=True)
        pick = gid == jnp.min(jnp.where(gs == mx, gid, float(N_EXPERT_GROUPS)), axis=0, keepdims=True)
        gsel = jnp.where(pick, 1.0, gsel)
        gs = jnp.where(pick, -jnp.inf, gs)
    emask = jnp.concatenate(
        [jnp.broadcast_to(gsel[g:g + 1, :], (gsz, TM)) for g in range(N_EXPERT_GROUPS)], axis=0)
    cand = jnp.where(emask > 0.5, biased, NEG_INF)
    eid = lax.broadcasted_iota(jnp.int32, cand.shape, 0).astype(F32)
    picks = []
    self32 = jnp.zeros(cand.shape, F32)
    for _ in range(TOP_K):
        mx = cand.max(axis=0, keepdims=True)
        pick = eid == jnp.min(jnp.where(cand == mx, eid, float(N_EXPERTS)), axis=0, keepdims=True)
        picks.append(pick)
        self32 = jnp.where(pick, 1.0, self32)
        cand = jnp.where(pick, -jnp.inf, cand)
    pos = _dot(self32.astype(BF16), tri_ref[...]) + carry
    sel_scores = [jnp.sum(jnp.where(p, scores, 0.0), axis=0, keepdims=True) for p in picks]
    wsum = functools.reduce(lambda a, b: a + b, sel_scores)
    zero_f = jnp.zeros((2, TM), F32)
    eidx = [jnp.sum(jnp.where(p, eid, 0.0), axis=0, keepdims=True) for p in picks]
    epos = [jnp.sum(jnp.where(p, pos, 0.0), axis=0, keepdims=True) for p in picks]
    ew = [s / wsum * ROUTED_SCALE for s in sel_scores]
    return (jnp.concatenate(eidx + [zero_f], axis=0).astype(jnp.int32),
            jnp.concatenate(epos + [zero_f], axis=0).astype(jnp.int32),
            jnp.concatenate(ew + [zero_f], axis=0),
            carry + jnp.sum(self32, axis=1, keepdims=True))


def _stage_e_kernel(x_ref, mod_ref, g1_ref, g2_ref, fnc_ref, fnl_ref, omc_ref, oml_ref, osc_ref, osl_ref,
                    gates_ref, wf_ref, wm_ref, ws_ref, wo_ref, rwt_ref, rb_ref, tri_ref,
                    x1_ref, h2a_ref, h2b_ref, eidx_ref, epos_ref, ew_ref, cnt_ref, carry_ref):
    @pl.when(pl.program_id(0) == 0)
    def _():
        carry_ref[...] = jnp.zeros_like(carry_ref)

    is_ctx = pl.program_id(0) < NB_CTX
    carry = carry_ref[...]
    for r, rows in enumerate(_chunks()):
        fn = jnp.where(is_ctx, fnc_ref[rows, :], fnl_ref[rows, :])
        om = jnp.where(is_ctx, omc_ref[rows, :], oml_ref[rows, :])
        osw = jnp.where(is_ctx, osc_ref[rows, :], osl_ref[rows, :])
        merged = (gates_ref[rows, 0:1024].astype(F32) * _dot(fn, wf_ref[...])
                  + gates_ref[rows, 1024:2048].astype(F32) * _dot(om, wm_ref[...])
                  + gates_ref[rows, 2048:3072].astype(F32) * _dot(osw, ws_ref[...]))
        mix = _dot(merged.astype(BF16), wo_ref[...])
        x1 = x_ref[rows, :] + mod_ref[:, 2048:3072] * _rms_rows(mix, g1_ref[...])
        x1_ref[rows, :] = x1
        h2 = _rms_rows(x1, g2_ref[...]) * (1.0 + mod_ref[:, 4096:5120]) + mod_ref[:, 3072:4096]
        h2a_ref[rows, :], h2b_ref[rows, :] = _pack_pair(h2)
        cols = pl.ds(r * TM, TM)
        eidx_ref[:, cols], epos_ref[:, cols], ew_ref[:, cols], carry = _route(
            h2.astype(BF16), rwt_ref, rb_ref, tri_ref, carry)
    carry_ref[...] = carry
    cnt_ref[...] = jnp.broadcast_to(carry, cnt_ref.shape).astype(jnp.int32)


def _stage_e(x, modt, g1, g2, mixed, gates, wf, wm, ws, wo, rwt, rbias, tri):
    row = lambda w: pl.BlockSpec((TB, w), lambda i: (i, 0))
    ctx = pl.BlockSpec((TB, 512), lambda i: (jnp.minimum(i, NB_CTX - 1), 0))
    lat = pl.BlockSpec((TB, 512), lambda i: (jnp.maximum(i - NB_CTX, 0), 0))
    col = lambda dt: (pl.BlockSpec((8, TB), lambda i: (0, i)), jax.ShapeDtypeStruct((8, N_TOK), dt))
    picks = [col(jnp.int32), col(jnp.int32), col(F32)]
    return pl.pallas_call(
        _stage_e_kernel,
        grid=(NB,),
        in_specs=[
            row(D_MODEL),
            pl.BlockSpec((None, 1, N_MOD * D_MODEL), lambda i: (i, 0, 0)),
            _const_spec((1, D_MODEL)), _const_spec((1, D_MODEL)),
            ctx, lat, ctx, lat, ctx, lat, row(3 * D_MODEL),
            _const_spec((512, D_MODEL)), _const_spec((512, D_MODEL)), _const_spec((512, D_MODEL)),
            _const_spec((D_MODEL, D_MODEL)),
            _const_spec((N_EXPERTS, D_MODEL)), _const_spec((N_EXPERTS, 1)), _const_spec((TM, TM)),
        ],
        out_specs=[row(D_MODEL), row(PACKED), row(PACKED)] + [s for s, _ in picks]
        + [_const_spec((N_EXPERTS, LANES))],
        out_shape=[jax.ShapeDtypeStruct((N_TOK, D_MODEL), F32),
                   jax.ShapeDtypeStruct((N_TOK, PACKED), jnp.int32),
                   jax.ShapeDtypeStruct((N_TOK, PACKED), jnp.int32)] + [s for _, s in picks]
        + [jax.ShapeDtypeStruct((N_EXPERTS, LANES), jnp.int32)],
        scratch_shapes=[pltpu.VMEM((N_EXPERTS, 1), F32)],
        compiler_params=_cparams(1),
        name="stage_e",
    )(x, modt, g1, g2, *mixed, gates, wf, wm, ws, wo, rwt, rbias, tri)


def _expert_kernel(te_ref, tv_ref, xa_ref, xb_ref, w1_ref, w3_ref, w2_ref, oa_ref, ob_ref,
                   w1b_ref, w3b_ref, w2b_ref):
    j = pl.program_id(0)
    new_expert = jnp.logical_or(j == 0, te_ref[j] != te_ref[jnp.maximum(j - 1, 0)])

    @pl.when(jnp.logical_and(tv_ref[j] == 1, new_expert))
    def _():
        w1b_ref[...] = w1_ref[...].astype(BF16)
        w3b_ref[...] = w3_ref[...].astype(BF16)
        w2b_ref[...] = w2_ref[...].astype(BF16)

    @pl.when(tv_ref[j] == 1)
    def _():
        for r in range(TE // EXPERT_ROWS):
            rows = pl.ds(r * EXPERT_ROWS, EXPERT_ROWS)
            x = _unpack_pair(xa_ref[rows, :], xb_ref[rows, :]).astype(BF16)
            hg = _dot(x, w1b_ref[...])
            hu = _dot(x, w3b_ref[...])
            act = (jax.nn.silu(hg) * hu).astype(BF16)
            oa_ref[rows, :], ob_ref[rows, :] = _pack_pair(_dot(act, w2b_ref[...]))


def _experts(layer, tile_expert, tile_valid, xsa, xsb, w1, w3, w2):
    slot_rows = pl.BlockSpec((TE, PACKED), lambda j, te, tv: (j, 0))
    grid_spec = pltpu.PrefetchScalarGridSpec(
        num_scalar_prefetch=2,
        grid=(NTE,),
        in_specs=[
            slot_rows, slot_rows,
            pl.BlockSpec((None, None, D_MODEL, EXPERT_FF), lambda j, te, tv: (layer, te[j], 0, 0)),
            pl.BlockSpec((None, None, D_MODEL, EXPERT_FF), lambda j, te, tv: (layer, te[j], 0, 0)),
            pl.BlockSpec((None, None, EXPERT_FF, D_MODEL), lambda j, te, tv: (layer, te[j], 0, 0)),
        ],
        out_specs=[slot_rows, slot_rows],
        scratch_shapes=[pltpu.VMEM((D_MODEL, EXPERT_FF), BF16), pltpu.VMEM((D_MODEL, EXPERT_FF), BF16),
                        pltpu.VMEM((EXPERT_FF, D_MODEL), BF16)],
    )
    return pl.pallas_call(
        _expert_kernel,
        grid_spec=grid_spec,
        out_shape=[jax.ShapeDtypeStruct((S_MAX, PACKED), jnp.int32)] * 2,
        compiler_params=_cparams(1),
        name="experts",
    )(tile_expert, tile_valid, xsa, xsb, w1, w3, w2)


def _sc_mesh():
    return plsc.VectorSubcoreMesh(core_axis_name="c", subcore_axis_name="s",
                                  num_cores=SC_CORES, num_subcores=SC_SUBCORES)


def _sc_scatter_rows(rows, slot8):
    @functools.partial(pl.kernel, mesh=_sc_mesh(), scratch_types=[],
                       out_type=jax.ShapeDtypeStruct((S_MAX, PACKED), jnp.int32))
    def scatter(x_hbm, i_hbm, o_hbm):
        def body(x_vmem, i_vmem):
            for k in range(TOP_K):
                pltpu.sync_copy(x_vmem, o_hbm.at[i_vmem.at[k]])

        pltpu.emit_pipeline(
            body,
            grid=(N_TOK // SC_ROWS,),
            in_specs=[pl.BlockSpec((SC_ROWS, PACKED), lambda i: (i, 0)),
                      pl.BlockSpec((8, SC_ROWS), lambda i: (0, i))],
            out_specs=[],
            core_axis_name=("c", "s"),
            dimension_semantics=(pltpu.PARALLEL,),
        )(x_hbm, i_hbm)

    return scatter(rows, slot8)


def _sc_gather_rows(table, idx):
    n = idx.shape[1]

    @functools.partial(pl.kernel, mesh=_sc_mesh(), scratch_types=[],
                       out_type=jax.ShapeDtypeStruct((n, PACKED), jnp.int32))
    def gather(t_hbm, i_hbm, o_hbm):
        def body(i_vmem, o_vmem):
            pltpu.sync_copy(t_hbm.at[i_vmem.at[0]], o_vmem)

        pltpu.emit_pipeline(
            body,
            grid=(n // SC_ROWS,),
            in_specs=[pl.BlockSpec((1, SC_ROWS), lambda i: (0, i))],
            out_specs=[pl.BlockSpec((SC_ROWS, PACKED), lambda i: (i, 0))],
            core_axis_name=("c", "s"),
            dimension_semantics=(pltpu.PARALLEL,),
        )(i_hbm, o_hbm)

    return gather(table, idx)


def _stage_g_kernel(x1_ref, mod_ref, g3_ref, yga_ref, ygb_ref, ew_ref, ha_ref, hb_ref,
                    s1_ref, s3_ref, s2_ref, o_ref):
    for rows in _chunks():
        h = _unpack_pair(ha_ref[rows, :], hb_ref[rows, :]).astype(BF16)
        act = jax.nn.silu(_dot(h, s1_ref[...])) * _dot(h, s3_ref[...])
        y = _dot(act.astype(BF16), s2_ref[...])
        for k in range(TOP_K):
            y = y + ew_ref[rows, k:k + 1] * _unpack_pair(yga_ref[k, rows, :], ygb_ref[k, rows, :])
        o_ref[rows, :] = x1_ref[rows, :] + mod_ref[:, 5120:6144] * _rms_rows(y, g3_ref[...])


def _stage_g(x1, modt, g3, yga, ygb, ew_rows, h2a, h2b, s1, s3, s2):
    row = lambda w: pl.BlockSpec((TB, w), lambda i: (i, 0))
    picked = pl.BlockSpec((TOP_K, TB, PACKED), lambda i: (0, i, 0))
    return pl.pallas_call(
        _stage_g_kernel,
        grid=(NB,),
        in_specs=[row(D_MODEL), pl.BlockSpec((None, 1, N_MOD * D_MODEL), lambda i: (i, 0, 0)),
                  _const_spec((1, D_MODEL)), picked, picked, row(8), row(PACKED), row(PACKED),
                  _const_spec((D_MODEL, SHARED_FF)), _const_spec((D_MODEL, SHARED_FF)),
                  _const_spec((SHARED_FF, D_MODEL))],
        out_specs=row(D_MODEL),
        out_shape=jax.ShapeDtypeStruct((N_TOK, D_MODEL), F32),
        compiler_params=_cparams(1),
        name="stage_g",
    )(x1, modt, g3, yga, ygb, ew_rows, h2a, h2b, s1, s3, s2)


def _rope_tables():
    t = jnp.arange(DEC_SEQ)
    pos = jnp.stack([(t // GRID_W).astype(F32), (t % GRID_W).astype(F32)], axis=-1)

    def table(r):
        n_freq = r // 4
        inv = ROPE_BASE ** (-jnp.arange(n_freq, dtype=F32) / n_freq)
        ang = pos[:, :, None] * inv
        cos = jnp.cos(ang)
        sin = jnp.sin(ang)
        cos_t = jnp.stack([cos, cos], axis=2).reshape(DEC_SEQ, r)
        sin_t = jnp.stack([-sin, sin], axis=2).reshape(DEC_SEQ, r)
        return cos_t, sin_t

    c64, s64 = table(SWA_HEAD_DIM)
    c32, s32 = table(MLA_ROPE)
    lat = jnp.concatenate([jnp.tile(c64, (1, 8)), jnp.tile(s64, (1, 8)),
                           jnp.tile(c32, (1, 4)), jnp.tile(s32, (1, 4))], axis=1)
    ident = jnp.concatenate([jnp.ones((TB, 512), F32), jnp.zeros((TB, 512), F32),
                             jnp.ones((TB, 128), F32), jnp.zeros((TB, 128), F32)], axis=1)
    return jnp.concatenate([ident, lat], axis=0)


def _dft_pair(n):
    k = jnp.arange(n, dtype=jnp.int32)
    ang = ((k[:, None] * k[None, :]) % n).astype(F32) * (2.0 * math.pi / n)
    return jnp.cos(ang), jnp.sin(ang)


def _fnet_tables():
    c64, s64 = _dft_pair(FNET_GROUP_DIM)
    eye = jnp.eye(FNET_GROUPS, dtype=F32)
    bd = jnp.concatenate([jnp.kron(eye, c64), jnp.kron(eye, s64)], axis=1).astype(BF16)
    mats = []
    for t_len in (SEQ, DEC_SEQ):
        c, s = _dft_pair(t_len)
        mats.append(jnp.concatenate([c, -s], axis=1).astype(BF16))
    return bd, mats[0], mats[1]


def _swap_perm(width, half):
    return np.arange(width) ^ half


def _layer_weights(l, w_in, w_uq, w_ukv):
    w = w_in[l]
    kr = w[:, 1024:1056]
    sq = w[:, 1056:1568]
    sk = w[:, 1568:1696]
    pad96 = jnp.zeros((D_MODEL, 96), F32)
    wide = jnp.concatenate([
        w[:, 0:1024], sq, sq[:, _swap_perm(512, 16)], sk, sk[:, _swap_perm(128, 16)],
        w[:, 1696:1824], kr, pad96, kr[:, _swap_perm(32, 8)], pad96], axis=1).astype(BF16)

    uq = w_uq[l].reshape(MLA_Q_RANK, MLA_HEADS, MLA_NOPE + MLA_ROPE)
    rope_w = uq[:, :, MLA_NOPE:]
    z32 = jnp.zeros((MLA_Q_RANK, MLA_HEADS, 32), F32)
    z64 = jnp.zeros((MLA_Q_RANK, MLA_HEADS, 64), F32)
    wqa = jnp.concatenate([uq, z32], axis=2).reshape(MLA_Q_RANK, 1024).astype(BF16)
    wqb = jnp.concatenate([z64, rope_w[:, :, _swap_perm(32, 8)], z32], axis=2)
    wqb = wqb.reshape(MLA_Q_RANK, 1024).astype(BF16)
    ukv = w_ukv[l].reshape(MLA_KV_RANK, MLA_HEADS, MLA_NOPE + MLA_V)
    wk = jnp.concatenate([ukv[:, :, :MLA_NOPE], jnp.zeros((MLA_KV_RANK, MLA_HEADS, 64), F32)],
                         axis=2).reshape(MLA_KV_RANK, 1024).astype(BF16)
    wv = ukv[:, :, MLA_NOPE:].reshape(MLA_KV_RANK, 512).astype(BF16)
    return wide, wqa, wqb, wk, wv


def _rope_placement():
    e = np.zeros((128, 1024), np.float32)
    for hd in range(MLA_HEADS):
        for i in range(MLA_ROPE):
            e[i, hd * 128 + MLA_NOPE + i] = 1.0
    return jnp.asarray(e, BF16)


def _moe_dispatch_plan(eidx, epos, counts):
    padded = ((counts + TE - 1) // TE) * TE
    ends = jnp.cumsum(padded)
    offs = ends - padded
    ids = jnp.arange(N_EXPERTS, dtype=jnp.int32)
    picked_off = jnp.sum(jnp.where(eidx[:, :, None] == ids, offs, 0), axis=-1)
    slot = picked_off + epos
    starts = jnp.arange(NTE, dtype=jnp.int32) * TE
    tile_expert = jnp.sum((ends[None, :] <= starts[:, None]).astype(jnp.int32), axis=1)
    tile_expert = jnp.minimum(tile_expert, N_EXPERTS - 1)
    tile_valid = (starts < ends[-1]).astype(jnp.int32)
    return slot, tile_expert, tile_valid


def kernel(x_prompt, x_sample, cache_mla_ckv, cache_mla_krope, cache_swa_k, cache_swa_v, c, c_ctx,
           ada_w, ada_b, norm_g, w_in, q_norm, kv_norm, w_fnet, w_uq, w_ukv, w_mla_o, swa_sink,
           w_swa_o, w_gate, b_gate, w_out, router_w, router_bias, exp_w1, exp_w3, exp_w2,
           shared_w1, shared_w3, shared_w2):
    x = jnp.concatenate([x_prompt.reshape(N_CTX, D_MODEL), x_sample.reshape(N_LAT, D_MODEL)], axis=0)

    cond8 = jnp.concatenate([c_ctx[None, :], c, jnp.zeros((3, D_MODEL), F32)], axis=0)
    mod = _modulation(cond8, ada_w, ada_b)
    tile_cond = np.concatenate([np.zeros(NB_CTX, np.int32),
                                1 + np.arange(NB - NB_CTX, dtype=np.int32) // LAT_BLOCKS])

    tab = _rope_tables()
    bd, f_ctx, f_lat = _fnet_tables()
    e_mat = _rope_placement()
    tri = jnp.asarray(np.triu(np.ones((TM, TM), np.float32), 1), BF16)

    new_ckv, new_kr, new_k, new_v = [], [], [], []
    for l in range(DEPTH):
        modt = mod[l][tile_cond][:, None, :]
        wide, wqa, wqb, wk, wv = _layer_weights(l, w_in, w_uq, w_ukv)
        ng = norm_g[l]

        fin, ckv, kr, sq, sk, sv, gates, q_m, k_m, v_m = _stage_a(
            x, modt, ng[0:1], wide, w_gate[l].astype(BF16), b_gate[l][None, :],
            q_norm[l][None, :], kv_norm[l][None, :], tab, wqa, wqb, wk, e_mat, wv)

        new_ckv.append(ckv[:N_CTX].reshape(BATCH, SEQ, MLA_KV_RANK))
        new_kr.append(kr[:N_CTX, :MLA_ROPE].reshape(BATCH, SEQ, MLA_ROPE))
        new_k.append(sk[:N_CTX].reshape(BATCH, SEQ, SWA_KV_HEADS, SWA_HEAD_DIM))
        new_v.append(sv[:N_CTX].reshape(BATCH, SEQ, SWA_KV_HEADS, SWA_HEAD_DIM))

        fn = (_fnet(fin, f_ctx, bd, BATCH, SEQ, 0),
              _fnet(fin, f_lat, bd, DEC_BATCH, DEC_SEQ, N_CTX // DEC_SEQ))

        kr_cache = jnp.pad(cache_mla_krope[:, l].reshape(N_CACHE, MLA_ROPE), ((0, 0), (0, 96)))
        k_c, v_c = _mla_cache_kv(cache_mla_ckv[:, l].reshape(N_CACHE, MLA_KV_RANK), kr_cache,
                                 wk, e_mat, wv)
        om = (_mla_attn(q_m, k_m, v_m, k_c, v_c, latent=False),
              _mla_attn(q_m, k_m, v_m, k_c, v_c, latent=True))

        ck = cache_swa_k[:, l].reshape(DEC_BATCH, PAST_LEN, 128)
        cv = cache_swa_v[:, l].reshape(DEC_BATCH, PAST_LEN, 128)
        osw = (_swa_attn(swa_sink[l], sq, sk, sv, ck, cv, latent=False),
               _swa_attn(swa_sink[l], sq, sk, sv, ck, cv, latent=True))

        x1, h2a, h2b, eidx, epos, ew, counts = _stage_e(
            x, modt, ng[1:2], ng[2:3], fn + om + osw, gates,
            w_fnet[l].astype(BF16), w_mla_o[l].astype(BF16), w_swa_o[l].astype(BF16),
            w_out[l].astype(BF16), router_w[l].T.astype(BF16), router_bias[l][:, None], tri)
        slot, tile_expert, tile_valid = _moe_dispatch_plan(eidx, epos, counts[:, 0])
        xsa = _sc_scatter_rows(h2a, slot)
        xsb = _sc_scatter_rows(h2b, slot)
        ysa, ysb = _experts(l, tile_expert, tile_valid, xsa, xsb, exp_w1, exp_w3, exp_w2)
        picks = slot[:TOP_K].reshape(1, TOP_K * N_TOK)
        yga = _sc_gather_rows(ysa, picks).reshape(TOP_K, N_TOK, PACKED)
        ygb = _sc_gather_rows(ysb, picks).reshape(TOP_K, N_TOK, PACKED)
        x = _stage_g(x1, modt, ng[3:4], yga, ygb, ew.T, h2a, h2b, shared_w1[l].astype(BF16),
                     shared_w3[l].astype(BF16), shared_w2[l].astype(BF16))

    y_p = x[:N_CTX].reshape(BATCH, SEQ, D_MODEL)
    y_s = x[N_CTX:].reshape(DEC_BATCH, DEC_SEQ, D_MODEL)
    return (y_p, y_s, jnp.stack(new_ckv, axis=1), jnp.stack(new_kr, axis=1),
            jnp.stack(new_k, axis=1), jnp.stack(new_v, axis=1))
```

```python
import functools
import math

import numpy as np
import jax
import jax.numpy as jnp
from jax import lax
from jax.experimental import pallas as pl
from jax.experimental.pallas import tpu as pltpu
from jax.experimental.pallas import tpu_sc as plsc

D_MODEL = 1024
BATCH = 16
SEQ = 256
DEPTH = 2
DEC_BATCH = 4
DEC_SEQ = 2048
PAST_LEN = 512
GRID_W = 64
EPS = 1e-6
ROPE_BASE = 10000.0
NEG_INF = -1e30

FNET_GROUPS = 8
FNET_GROUP_DIM = 64
FNET_WIDTH = 512
MLA_HEADS = 8
MLA_Q_RANK = 384
MLA_KV_RANK = 128
MLA_NOPE = 64
MLA_ROPE = 32
MLA_V = 64
MLA_SCALE = (MLA_NOPE + MLA_ROPE) ** -0.5
LOG2E = math.log2(math.e)
SWA_HEADS = 8
SWA_KV_HEADS = 2
SWA_HEAD_DIM = 64
SWA_WINDOW = 128
SWA_SCALE = SWA_HEAD_DIM ** -0.5
N_MOD = 6
N_EXPERTS = 64
N_EXPERT_GROUPS = 8
TOPK_GROUPS = 4
TOP_K = 6
EXPERT_FF = 256
SHARED_FF = 256
ROUTED_SCALE = 2.5

LANES = 128
TM = 256
N_CTX = BATCH * SEQ
N_LAT = DEC_BATCH * DEC_SEQ
N_TOK = N_CTX + N_LAT
N_CACHE = DEC_BATCH * PAST_LEN
NT_CTX = N_CTX // TM
NT_LAT = N_LAT // TM
NT = N_TOK // TM
LAT_TILES = DEC_SEQ // TM
TB = 512
NB = N_TOK // TB
NB_CTX = N_CTX // TB
LAT_BLOCKS = DEC_SEQ // TB
MLA_LAT_TQ = 256
TE = 512
S_MAX = N_TOK * TOP_K + N_EXPERTS * TE
NTE = S_MAX // TE
EXPERT_ROWS = 256
VMEM_LIMIT = 56 * 1024 * 1024
PACKED = D_MODEL // 4
SC_ROWS = 128
SC_CORES = 2
SC_SUBCORES = 16

A_F = (0, 512)
A_QD = (512, 896)
A_KV = (896, 1024)
A_SQ = (1024, 1536)
A_SK = (1536, 1664)
A_SV = (1664, 1792)
A_KR = (1792, 1920)
W_IN_WIDE = 1920
TAB_W = 1280

F32 = jnp.float32
BF16 = jnp.bfloat16


def _cparams(n_axes, parallel=False):
    sem = ("parallel" if parallel else "arbitrary",) * n_axes
    return pltpu.CompilerParams(dimension_semantics=sem, vmem_limit_bytes=VMEM_LIMIT)


def _dot(a, b):
    return jnp.dot(a, b, preferred_element_type=F32)


def _dot_nt(a, b):
    return lax.dot_general(a, b, (((1,), (1,)), ((), ())), preferred_element_type=F32)


def _rms_rows(v, g):
    return v * lax.rsqrt(jnp.mean(v * v, axis=-1, keepdims=True) + EPS) * g


def _pack_rows(v):
    half = v.shape[1] // 2
    lo = lax.bitcast_convert_type(v[:, :half].astype(BF16).astype(F32), jnp.int32)
    hi = lax.bitcast_convert_type(v[:, half:].astype(BF16).astype(F32), jnp.int32)
    return jnp.bitwise_or(jnp.bitwise_and(hi, -65536), jnp.bitwise_and(jnp.right_shift(lo, 16), 65535))


def _unpack_rows(w):
    lo = lax.bitcast_convert_type(jnp.left_shift(w, 16), F32)
    hi = lax.bitcast_convert_type(jnp.bitwise_and(w, -65536), F32)
    return jnp.concatenate([lo, hi], axis=1)


def _pack_pair(v):
    half = v.shape[1] // 2
    return _pack_rows(v[:, :half]), _pack_rows(v[:, half:])


def _unpack_pair(a, b):
    return jnp.concatenate([_unpack_rows(a), _unpack_rows(b)], axis=1)


def _const_spec(shape):
    return pl.BlockSpec(shape, lambda *_: (0,) * len(shape))


def _layer_spec(shape, layer):
    return pl.BlockSpec((None,) + shape, lambda *_: (layer,) + (0,) * len(shape))


def _tab_row_block(i):
    return jnp.where(i < NB_CTX, 0, 1 + (i - NB_CTX) % LAT_BLOCKS)


def _mod_kernel(cond_ref, w_ref, b_ref, o_ref):
    c = cond_ref[...]
    a = (c * jax.nn.sigmoid(c)).astype(BF16)
    o_ref[...] = _dot(a, w_ref[...].astype(BF16)) + b_ref[...]


def _modulation(cond8, ada_w, ada_b):
    tn = 512
    nj = N_MOD * D_MODEL // tn
    return pl.pallas_call(
        _mod_kernel,
        grid=(DEPTH, nj),
        in_specs=[
            pl.BlockSpec((8, D_MODEL), lambda l, j: (0, 0)),
            pl.BlockSpec((None, D_MODEL, tn), lambda l, j: (l, 0, j)),
            pl.BlockSpec((None, 1, tn), lambda l, j: (l, 0, j)),
        ],
        out_specs=pl.BlockSpec((None, 8, tn), lambda l, j: (l, 0, j)),
        out_shape=jax.ShapeDtypeStruct((DEPTH, 8, N_MOD * D_MODEL), F32),
        compiler_params=_cparams(2),
        name="modulation",
    )(cond8, ada_w, ada_b.reshape(DEPTH, 1, N_MOD * D_MODEL))


def _half_swap(x, half):
    n = x.shape[1]
    lane = lax.broadcasted_iota(jnp.int32, (1, n), 1)
    return jnp.where((lane & half) == 0, pltpu.roll(x, n - half, 1), pltpu.roll(x, half, 1))


def _mla_expand(rows, cq, ckv, kr, cos32, sin32, wqa_ref, wk_ref, e_ref, wv_ref,
                q_ref, k_ref, v_ref):
    if q_ref is not None:
        lane = lax.broadcasted_iota(jnp.int32, (1, LANES), 1)
        rope_lane = jnp.logical_and(lane >= MLA_NOPE, lane < MLA_NOPE + MLA_ROPE)
        cos_h = jnp.where(rope_lane, cos32, 1.0)
        sin_h = jnp.where(rope_lane, sin32, 0.0)
        for hd in range(MLA_HEADS):
            lo, hi = hd * LANES, (hd + 1) * LANES
            q = _dot(cq, wqa_ref[:, lo:hi])
            q = q * cos_h + _half_swap(q, MLA_ROPE // 4) * sin_h
            q_ref[rows, lo:hi] = (q * (MLA_SCALE * LOG2E)).astype(BF16)
    k_ref[rows, :] = (_dot(ckv, wk_ref[...]) + _dot(kr, e_ref[...])).astype(BF16)
    v_ref[rows, :] = _dot(ckv, wv_ref[...]).astype(BF16)


def _chunks():
    return [pl.ds(r * TM, TM) for r in range(TB // TM)]


def _stage_a_kernel(x_ref, mod_ref, g_ref, win_ref, wg_ref, bg_ref, qn_ref, kvn_ref, tab_ref,
                    wqa_ref, wk_ref, e_ref, wv_ref,
                    fin_ref, ckv_ref, kr_ref, sq_ref, sk_ref, sv_ref, gates_ref,
                    qm_ref, km_ref, vm_ref):
    for rows in _chunks():
        x = x_ref[rows, :]
        h = (_rms_rows(x, g_ref[...]) * (1.0 + mod_ref[:, 1024:2048]) + mod_ref[:, 0:1024]).astype(BF16)

        def proj(seg):
            return _dot(h, win_ref[:, seg[0]:seg[1]])

        fin_ref[rows, :] = proj(A_F).astype(BF16)
        cq = _rms_rows(proj(A_QD), qn_ref[...]).astype(BF16)
        ckv = _rms_rows(proj(A_KV), kvn_ref[...])
        ckv_ref[rows, :] = ckv
        cos64 = tab_ref[rows, 0:512]
        sin64 = tab_ref[rows, 512:1024]
        sq = proj(A_SQ)
        sq = sq * cos64 + _half_swap(sq, SWA_HEAD_DIM // 4) * sin64
        sq_ref[rows, :] = (sq * (SWA_SCALE * LOG2E)).astype(BF16)
        sk = proj(A_SK)
        sk_ref[rows, :] = sk * cos64[:, 0:128] + _half_swap(sk, SWA_HEAD_DIM // 4) * sin64[:, 0:128]
        sv_ref[rows, :] = proj(A_SV)
        cos32 = tab_ref[rows, 1024:1152]
        sin32 = tab_ref[rows, 1152:1280]
        kr = proj(A_KR)
        kr = kr * cos32 + _half_swap(kr, MLA_ROPE // 4) * sin32
        kr_ref[rows, :] = kr
        _mla_expand(rows, cq, ckv.astype(BF16), kr.astype(BF16), cos32, sin32,
                    wqa_ref, wk_ref, e_ref, wv_ref, qm_ref, km_ref, vm_ref)
        for c in range(3):
            lo, hi = c * D_MODEL, (c + 1) * D_MODEL
            gates_ref[rows, lo:hi] = jax.nn.sigmoid(_dot(h, wg_ref[:, lo:hi]) + bg_ref[:, lo:hi]).astype(BF16)


def _stage_a(layer, x, modt, g0, w_in_wide, w_gate, b_gate, q_norm, kv_norm, tab, wqa, wk, e_mat, wv):
    row = lambda w: pl.BlockSpec((TB, w), lambda i: (i, 0))
    outs = [(512, BF16), (128, F32), (128, F32), (512, BF16), (128, F32), (128, F32),
            (3 * D_MODEL, BF16), (1024, BF16), (1024, BF16), (512, BF16)]
    return pl.pallas_call(
        _stage_a_kernel,
        grid=(NB,),
        in_specs=[
            row(D_MODEL),
            pl.BlockSpec((None, 1, N_MOD * D_MODEL), lambda i: (i, 0, 0)),
            _const_spec((1, D_MODEL)),
            _const_spec((D_MODEL, W_IN_WIDE)),
            _layer_spec((D_MODEL, 3 * D_MODEL), layer),
            _const_spec((1, 3 * D_MODEL)),
            _const_spec((1, MLA_Q_RANK)),
            _const_spec((1, MLA_KV_RANK)),
            pl.BlockSpec((TB, TAB_W), lambda i: (_tab_row_block(i), 0)),
            _const_spec((MLA_Q_RANK, 1024)),
            _const_spec((128, 1024)), _const_spec((128, 1024)), _const_spec((128, 512)),
        ],
        out_specs=[row(w) for w, _ in outs],
        out_shape=[jax.ShapeDtypeStruct((N_TOK, w), dt) for w, dt in outs],
        compiler_params=_cparams(1),
        name="stage_a",
    )(x, modt, g0, w_in_wide, w_gate, b_gate, q_norm, kv_norm, tab, wqa, wk, e_mat, wv)


def _fnet_kernel(t_len, scale, fin_ref, f_ref, bd_ref, o_ref, zz_ref):
    @pl.when(pl.program_id(1) == 0)
    def _():
        z = fin_ref[...]
        zz_ref[0:t_len, :] = _dot(z, bd_ref[:, 0:512]).astype(BF16)
        zz_ref[t_len:2 * t_len, :] = _dot(z, bd_ref[:, 512:1024]).astype(BF16)

    o_ref[...] = (_dot(f_ref[...], zz_ref[...]) * scale).astype(BF16)


def _fnet(fin, fmat, bd, n_batch, t_len, row_block0):
    scale = 1.0 / math.sqrt(t_len * FNET_GROUP_DIM)
    return pl.pallas_call(
        functools.partial(_fnet_kernel, t_len, scale),
        grid=(n_batch, t_len // TM),
        in_specs=[
            pl.BlockSpec((t_len, FNET_WIDTH), lambda b, i: (row_block0 + b, 0)),
            pl.BlockSpec((TM, 2 * t_len), lambda b, i: (i, 0)),
            _const_spec((FNET_WIDTH, 2 * FNET_WIDTH)),
        ],
        out_specs=pl.BlockSpec((TM, FNET_WIDTH), lambda b, i: (b * (t_len // TM) + i, 0)),
        out_shape=jax.ShapeDtypeStruct((n_batch * t_len, FNET_WIDTH), BF16),
        scratch_shapes=[pltpu.VMEM((2 * t_len, FNET_WIDTH), BF16)],
        compiler_params=_cparams(2),
        name=f"fnet_{t_len}",
    )(fin, fmat, bd)


def _mla_cache_kernel(ckv_ref, kr_ref, wk_ref, e_ref, wv_ref, k_ref, v_ref):
    _mla_expand(slice(None), None, ckv_ref[...].astype(BF16), kr_ref[...].astype(BF16), None, None,
                None, wk_ref, e_ref, wv_ref, None, k_ref, v_ref)


def _mla_cache_kv(ckv_cache, kr_cache, wk, e_mat, wv):
    row = lambda w: pl.BlockSpec((TM, w), lambda i: (i, 0))
    return pl.pallas_call(
        _mla_cache_kernel,
        grid=(N_CACHE // TM,),
        in_specs=[row(128), row(128),
                  _const_spec((128, 1024)), _const_spec((128, 1024)), _const_spec((128, 512))],
        out_specs=[row(1024), row(512)],
        out_shape=[jax.ShapeDtypeStruct((N_CACHE, 1024), BF16),
                   jax.ShapeDtypeStruct((N_CACHE, 512), BF16)],
        compiler_params=_cparams(1),
        name="mla_cache_kv",
    )(ckv_cache, kr_cache, wk, e_mat, wv)


def _mla_attn_kernel(n_seg, q_ref, *refs):
    k_refs = refs[0:n_seg]
    v_refs = refs[n_seg:2 * n_seg]
    o_ref = refs[2 * n_seg]
    lane = lax.broadcasted_iota(jnp.int32, (1, LANES), 1)
    low = lane < MLA_V
    outs = []
    for hh in range(2):
        q = q_ref[:, hh * LANES:(hh + 1) * LANES]
        ss = [_dot_nt(q, k[:, hh * LANES:(hh + 1) * LANES]) for k in k_refs]
        m = functools.reduce(jnp.maximum, [s.max(axis=-1, keepdims=True) for s in ss])
        keep = low if hh == 0 else jnp.logical_not(low)
        sum_lane = MLA_V if hh == 0 else 0
        po = None
        for s, v_ref in zip(ss, v_refs):
            v = v_ref[...]
            vm = jnp.where(lane == sum_lane, jnp.ones_like(v), jnp.where(keep, v, jnp.zeros_like(v)))
            t = _dot(jnp.exp2(s - m).astype(BF16), vm)
            po = t if po is None else po + t
        outs.append(po / po[:, sum_lane:sum_lane + 1])
    o_ref[...] = jnp.where(low, outs[0], outs[1]).astype(BF16)


def _mla_attn(q_all, k_all, v_all, k_cache, v_cache, latent):
    if latent:
        tq = MLA_LAT_TQ
        n_b, n_q = DEC_BATCH, DEC_SEQ // tq
        q0 = N_CTX // tq
        kv_specs = [
            pl.BlockSpec((PAST_LEN, 256), lambda b, hp, i: (b, hp)),
            pl.BlockSpec((DEC_SEQ, 256), lambda b, hp, i: (N_CTX // DEC_SEQ + b, hp)),
            pl.BlockSpec((PAST_LEN, 128), lambda b, hp, i: (b, hp)),
            pl.BlockSpec((DEC_SEQ, 128), lambda b, hp, i: (N_CTX // DEC_SEQ + b, hp)),
        ]
        args = (q_all, k_cache, k_all, v_cache, v_all)
        n_seg = 2
    else:
        tq = SEQ
        n_b, n_q = BATCH, 1
        q0 = 0
        kv_specs = [
            pl.BlockSpec((SEQ, 256), lambda b, hp, i: (b, hp)),
            pl.BlockSpec((SEQ, 128), lambda b, hp, i: (b, hp)),
        ]
        args = (q_all, k_all, v_all)
        n_seg = 1
    return pl.pallas_call(
        functools.partial(_mla_attn_kernel, n_seg),
        grid=(n_b, MLA_HEADS // 2, n_q),
        in_specs=[pl.BlockSpec((tq, 256), lambda b, hp, i: (q0 + b * n_q + i, hp))] + kv_specs,
        out_specs=pl.BlockSpec((tq, 128), lambda b, hp, i: (b * n_q + i, hp)),
        out_shape=jax.ShapeDtypeStruct((n_b * n_q * tq, MLA_HEADS * MLA_V), BF16),
        compiler_params=_cparams(3),
        name="mla_attn_lat" if latent else "mla_attn_ctx",
    )(*args)


def _swa_kernel(windowed, n_qb, sink_ref, q_ref, *refs):
    n_seg = 4 if windowed else 1
    k_refs = refs[0:n_seg]
    v_refs = refs[n_seg:2 * n_seg]
    o_ref = refs[2 * n_seg]
    tq = q_ref.shape[0]
    qb = pl.program_id(1)
    lane = lax.broadcasted_iota(jnp.int32, (1, LANES), 1)
    low = lane < SWA_HEAD_DIM
    high = jnp.logical_not(low)

    k_all = jnp.concatenate([r[...] for r in k_refs], axis=0)
    v_all = jnp.concatenate([r[...] for r in v_refs], axis=0)
    k_sw = pltpu.roll(k_all, SWA_HEAD_DIM, 1)
    v_sw = pltpu.roll(v_all, SWA_HEAD_DIM, 1)

    if windowed:
        qi = lax.broadcasted_iota(jnp.int32, (2 * tq, SWA_WINDOW), 0) % tq
        kj = lax.broadcasted_iota(jnp.int32, (2 * tq, SWA_WINDOW), 1)
        bias_prev = jnp.where(jnp.logical_and(kj >= qi, qb > 0), 0.0, NEG_INF)
        bias_next = jnp.where(jnp.logical_and(kj <= qi, qb < n_qb - 1), 0.0, NEG_INF)
    top_rows = lax.broadcasted_iota(jnp.int32, (2 * tq, 1), 0) < tq

    for g in range(SWA_KV_HEADS):
        qs = jnp.concatenate([q_ref[:, 256 * g:256 * g + 128],
                              q_ref[:, 256 * g + 128:256 * g + 256]], axis=0)
        halves = []
        for half in range(2):
            keep = low if half == 0 else high
            sum_lane = SWA_HEAD_DIM if half == 0 else 0
            straight = (g == half)
            kh = jnp.where(keep, k_all if straight else k_sw, 0.0).astype(BF16)
            vh = jnp.where(lane == sum_lane, 1.0,
                           jnp.where(keep, v_all if straight else v_sw, 0.0)).astype(BF16)
            s = _dot_nt(qs, kh)
            if windowed:
                c0, c1, c2 = PAST_LEN, PAST_LEN + SWA_WINDOW, PAST_LEN + 2 * SWA_WINDOW
                s = jnp.concatenate([s[:, :c0], s[:, c0:c1] + bias_prev, s[:, c1:c2],
                                     s[:, c2:] + bias_next], axis=1)
            sink = jnp.where(top_rows, sink_ref[4 * g + half], sink_ref[4 * g + 2 + half]) * LOG2E
            m = jnp.maximum(s.max(axis=-1, keepdims=True), sink)
            po = _dot(jnp.exp2(s - m).astype(BF16), vh)
            halves.append(po / (po[:, sum_lane:sum_lane + 1] + jnp.exp2(sink - m)))
        out = jnp.where(low, halves[0], halves[1])
        o_ref[:, 256 * g:256 * g + 128] = out[0:tq].astype(BF16)
        o_ref[:, 256 * g + 128:256 * g + 256] = out[tq:2 * tq].astype(BF16)


def _swa_attn(sink, sq, sk, sv, cache_k, cache_v, latent):
    smem = pl.BlockSpec(memory_space=pltpu.SMEM)
    if latent:
        tq = SWA_WINDOW
        n_b, n_qb = DEC_BATCH, DEC_SEQ // tq
        base = N_CTX // tq

        def prev(b, i):
            return (base + b * n_qb + jnp.maximum(i - 1, 0), 0)

        def cur(b, i):
            return (base + b * n_qb + i, 0)

        def nxt(b, i):
            return (base + b * n_qb + jnp.minimum(i + 1, n_qb - 1), 0)

        cache = pl.BlockSpec((None, PAST_LEN, 128), lambda b, i: (b, 0, 0))
        blk = lambda f: pl.BlockSpec((tq, 128), f)
        kv_specs = [cache, blk(prev), blk(cur), blk(nxt)] * 2
        args = (cache_k, sk, sk, sk, cache_v, sv, sv, sv)
        q_spec = pl.BlockSpec((tq, 512), cur)
        o_spec = pl.BlockSpec((tq, 512), lambda b, i: (b * n_qb + i, 0))
    else:
        tq = SEQ
        n_b, n_qb = BATCH, 1
        blk = pl.BlockSpec((tq, 128), lambda b, i: (b, 0))
        kv_specs = [blk, blk]
        args = (sk, sv)
        q_spec = pl.BlockSpec((tq, 512), lambda b, i: (b, 0))
        o_spec = q_spec
    return pl.pallas_call(
        functools.partial(_swa_kernel, latent, n_qb),
        grid=(n_b, n_qb),
        in_specs=[smem, q_spec] + kv_specs,
        out_specs=o_spec,
        out_shape=jax.ShapeDtypeStruct((n_b * n_qb * tq, 512), BF16),
        compiler_params=_cparams(2),
        name="swa_lat" if latent else "swa_ctx",
    )(sink, sq, *args)


def _route(h, rwt_ref, rb_ref, tri_ref, carry):
    gsz = N_EXPERTS // N_EXPERT_GROUPS
    scores = jax.nn.sigmoid(_dot_nt(rwt_ref[...], h))
    biased = scores + rb_ref[...]
    mem = lax.broadcasted_iota(jnp.int32, (gsz, TM), 0).astype(F32)
    gs_rows = []
    for g in range(N_EXPERT_GROUPS):
        bg = biased[g * gsz:(g + 1) * gsz, :]
        m1 = bg.max(axis=0, keepdims=True)
        first = jnp.min(jnp.where(bg == m1, mem, float(gsz)), axis=0, keepdims=True)
        m2 = jnp.where(mem == first, -jnp.inf, bg).max(axis=0, keepdims=True)
        gs_rows.append(m1 + m2)
    gs = jnp.concatenate(gs_rows, axis=0)
    gid = lax.broadcasted_iota(jnp.int32, gs.shape, 0).astype(F32)
    gsel = jnp.zeros(gs.shape, F32)
    for _ in range(TOPK_GROUPS):
        mx = gs.max(axis=0, keepdims=True)
        pick = gid == jnp.min(jnp.where(gs == mx, gid, float(N_EXPERT_GROUPS)), axis=0, keepdims=True)
        gsel = jnp.where(pick, 1.0, gsel)
        gs = jnp.where(pick, -jnp.inf, gs)
    emask = jnp.concatenate(
        [jnp.broadcast_to(gsel[g:g + 1, :], (gsz, TM)) for g in range(N_EXPERT_GROUPS)], axis=0)
    cand = jnp.where(emask > 0.5, biased, NEG_INF)
    eid = lax.broadcasted_iota(jnp.int32, cand.shape, 0).astype(F32)
    picks = []
    self32 = jnp.zeros(cand.shape, F32)
    for _ in range(TOP_K):
        mx = cand.max(axis=0, keepdims=True)
        pick = eid == jnp.min(jnp.where(cand == mx, eid, float(N_EXPERTS)), axis=0, keepdims=True)
        picks.append(pick)
        self32 = jnp.where(pick, 1.0, self32)
        cand = jnp.where(pick, -jnp.inf, cand)
    pos = _dot(self32.astype(BF16), tri_ref[...]) + carry
    sel_scores = [jnp.sum(jnp.where(p, scores, 0.0), axis=0, keepdims=True) for p in picks]
    wsum = functools.reduce(lambda a, b: a + b, sel_scores)
    zero_f = jnp.zeros((2, TM), F32)
    eidx = [jnp.sum(jnp.where(p, eid, 0.0), axis=0, keepdims=True) for p in picks]
    epos = [jnp.sum(jnp.where(p, pos, 0.0), axis=0, keepdims=True) for p in picks]
    ew = [s / wsum * ROUTED_SCALE for s in sel_scores]
    return (jnp.concatenate(eidx + [zero_f], axis=0).astype(jnp.int32),
            jnp.concatenate(epos + [zero_f], axis=0).astype(jnp.int32),
            jnp.concatenate(ew + [zero_f], axis=0),
            carry + jnp.sum(self32, axis=1, keepdims=True))


def _stage_e_kernel(x_ref, mod_ref, g1_ref, g2_ref, fnc_ref, fnl_ref, omc_ref, oml_ref, osc_ref, osl_ref,
                    gates_ref, wf_ref, wm_ref, ws_ref, wo_ref, rwt_ref, rb_ref, tri_ref,
                    x1_ref, h2a_ref, h2b_ref, eidx_ref, epos_ref, ew_ref, cnt_ref, carry_ref):
    @pl.when(pl.program_id(0) == 0)
    def _():
        carry_ref[...] = jnp.zeros_like(carry_ref)

    is_ctx = pl.program_id(0) < NB_CTX
    carry = carry_ref[...]
    for r, rows in enumerate(_chunks()):
        fn = jnp.where(is_ctx, fnc_ref[rows, :], fnl_ref[rows, :])
        om = jnp.where(is_ctx, omc_ref[rows, :], oml_ref[rows, :])
        osw = jnp.where(is_ctx, osc_ref[rows, :], osl_ref[rows, :])
        merged = (gates_ref[rows, 0:1024].astype(F32) * _dot(fn, wf_ref[...])
                  + gates_ref[rows, 1024:2048].astype(F32) * _dot(om, wm_ref[...])
                  + gates_ref[rows, 2048:3072].astype(F32) * _dot(osw, ws_ref[...]))
        mix = _dot(merged.astype(BF16), wo_ref[...])
        x1 = x_ref[rows, :] + mod_ref[:, 2048:3072] * _rms_rows(mix, g1_ref[...])
        x1_ref[rows, :] = x1
        h2 = _rms_rows(x1, g2_ref[...]) * (1.0 + mod_ref[:, 4096:5120]) + mod_ref[:, 3072:4096]
        h2a_ref[rows, :], h2b_ref[rows, :] = _pack_pair(h2)
        cols = pl.ds(r * TM, TM)
        eidx_ref[:, cols], epos_ref[:, cols], ew_ref[:, cols], carry = _route(
            h2.astype(BF16), rwt_ref, rb_ref, tri_ref, carry)
    carry_ref[...] = carry
    cnt_ref[...] = jnp.broadcast_to(carry, cnt_ref.shape).astype(jnp.int32)


def _stage_e(layer, x, modt, g1, g2, mixed, gates, wf, wm, ws, wo, rwt, rbias, tri):
    row = lambda w: pl.BlockSpec((TB, w), lambda i: (i, 0))
    ctx = pl.BlockSpec((TB, 512), lambda i: (jnp.minimum(i, NB_CTX - 1), 0))
    lat = pl.BlockSpec((TB, 512), lambda i: (jnp.maximum(i - NB_CTX, 0), 0))
    col = lambda dt: (pl.BlockSpec((8, TB), lambda i: (0, i)), jax.ShapeDtypeStruct((8, N_TOK), dt))
    picks = [col(jnp.int32), col(jnp.int32), col(F32)]
    return pl.pallas_call(
        _stage_e_kernel,
        grid=(NB,),
        in_specs=[
            row(D_MODEL),
            pl.BlockSpec((None, 1, N_MOD * D_MODEL), lambda i: (i, 0, 0)),
            _const_spec((1, D_MODEL)), _const_spec((1, D_MODEL)),
            ctx, lat, ctx, lat, ctx, lat, row(3 * D_MODEL),
            _layer_spec((512, D_MODEL), layer), _layer_spec((512, D_MODEL), layer),
            _layer_spec((512, D_MODEL), layer), _layer_spec((D_MODEL, D_MODEL), layer),
            _const_spec((N_EXPERTS, D_MODEL)), _const_spec((N_EXPERTS, 1)), _const_spec((TM, TM)),
        ],
        out_specs=[row(D_MODEL), row(PACKED), row(PACKED)] + [s for s, _ in picks]
        + [_const_spec((N_EXPERTS, LANES))],
        out_shape=[jax.ShapeDtypeStruct((N_TOK, D_MODEL), F32),
                   jax.ShapeDtypeStruct((N_TOK, PACKED), jnp.int32),
                   jax.ShapeDtypeStruct((N_TOK, PACKED), jnp.int32)] + [s for _, s in picks]
        + [jax.ShapeDtypeStruct((N_EXPERTS, LANES), jnp.int32)],
        scratch_shapes=[pltpu.VMEM((N_EXPERTS, 1), F32)],
        compiler_params=_cparams(1),
        name="stage_e",
    )(x, modt, g1, g2, *mixed, gates, wf, wm, ws, wo, rwt, rbias, tri)


def _expert_kernel(te_ref, tv_ref, xa_ref, xb_ref, w1_ref, w3_ref, w2_ref, oa_ref, ob_ref,
                   w1b_ref, w3b_ref, w2b_ref):
    j = pl.program_id(0)
    new_expert = jnp.logical_or(j == 0, te_ref[j] != te_ref[jnp.maximum(j - 1, 0)])

    @pl.when(jnp.logical_and(tv_ref[j] == 1, new_expert))
    def _():
        w1b_ref[...] = w1_ref[...].astype(BF16)
        w3b_ref[...] = w3_ref[...].astype(BF16)
        w2b_ref[...] = w2_ref[...].astype(BF16)

    @pl.when(tv_ref[j] == 1)
    def _():
        for r in range(TE // EXPERT_ROWS):
            rows = pl.ds(r * EXPERT_ROWS, EXPERT_ROWS)
            x = _unpack_pair(xa_ref[rows, :], xb_ref[rows, :]).astype(BF16)
            hg = _dot(x, w1b_ref[...])
            hu = _dot(x, w3b_ref[...])
            act = (jax.nn.silu(hg) * hu).astype(BF16)
            oa_ref[rows, :], ob_ref[rows, :] = _pack_pair(_dot(act, w2b_ref[...]))


def _experts(layer, tile_expert, tile_valid, xsa, xsb, w1, w3, w2):
    slot_rows = pl.BlockSpec((TE, PACKED), lambda j, te, tv: (j, 0))
    grid_spec = pltpu.PrefetchScalarGridSpec(
        num_scalar_prefetch=2,
        grid=(NTE,),
        in_specs=[
            slot_rows, slot_rows,
            pl.BlockSpec((None, None, D_MODEL, EXPERT_FF), lambda j, te, tv: (layer, te[j], 0, 0)),
            pl.BlockSpec((None, None, D_MODEL, EXPERT_FF), lambda j, te, tv: (layer, te[j], 0, 0)),
            pl.BlockSpec((None, None, EXPERT_FF, D_MODEL), lambda j, te, tv: (layer, te[j], 0, 0)),
        ],
        out_specs=[slot_rows, slot_rows],
        scratch_shapes=[pltpu.VMEM((D_MODEL, EXPERT_FF), BF16), pltpu.VMEM((D_MODEL, EXPERT_FF), BF16),
                        pltpu.VMEM((EXPERT_FF, D_MODEL), BF16)],
    )
    return pl.pallas_call(
        _expert_kernel,
        grid_spec=grid_spec,
        out_shape=[jax.ShapeDtypeStruct((S_MAX, PACKED), jnp.int32)] * 2,
        compiler_params=_cparams(1),
        name="experts",
    )(tile_expert, tile_valid, xsa, xsb, w1, w3, w2)


def _sc_mesh():
    return plsc.VectorSubcoreMesh(core_axis_name="c", subcore_axis_name="s",
                                  num_cores=SC_CORES, num_subcores=SC_SUBCORES)


def _sc_scatter_rows(rows, slot8):
    @functools.partial(pl.kernel, mesh=_sc_mesh(), scratch_types=[],
                       out_type=jax.ShapeDtypeStruct((S_MAX, PACKED), jnp.int32))
    def scatter(x_hbm, i_hbm, o_hbm):
        def body(x_vmem, i_vmem):
            for k in range(TOP_K):
                pltpu.sync_copy(x_vmem, o_hbm.at[i_vmem.at[k]])

        pltpu.emit_pipeline(
            body,
            grid=(N_TOK // SC_ROWS,),
            in_specs=[pl.BlockSpec((SC_ROWS, PACKED), lambda i: (i, 0)),
                      pl.BlockSpec((8, SC_ROWS), lambda i: (0, i))],
            out_specs=[],
            core_axis_name=("c", "s"),
            dimension_semantics=(pltpu.PARALLEL,),
        )(x_hbm, i_hbm)

    return scatter(rows, slot8)


def _sc_gather_rows(table, idx):
    n = idx.shape[1]

    @functools.partial(pl.kernel, mesh=_sc_mesh(), scratch_types=[],
                       out_type=jax.ShapeDtypeStruct((n, PACKED), jnp.int32))
    def gather(t_hbm, i_hbm, o_hbm):
        def body(i_vmem, o_vmem):
            pltpu.sync_copy(t_hbm.at[i_vmem.at[0]], o_vmem)

        pltpu.emit_pipeline(
            body,
            grid=(n // SC_ROWS,),
            in_specs=[pl.BlockSpec((1, SC_ROWS), lambda i: (0, i))],
            out_specs=[pl.BlockSpec((SC_ROWS, PACKED), lambda i: (i, 0))],
            core_axis_name=("c", "s"),
            dimension_semantics=(pltpu.PARALLEL,),
        )(i_hbm, o_hbm)

    return gather(table, idx)


def _stage_g_kernel(x1_ref, mod_ref, g3_ref, yga_ref, ygb_ref, ew_ref, ha_ref, hb_ref,
                    s1_ref, s3_ref, s2_ref, o_ref):
    for rows in _chunks():
        h = _unpack_pair(ha_ref[rows, :], hb_ref[rows, :]).astype(BF16)
        act = jax.nn.silu(_dot(h, s1_ref[...])) * _dot(h, s3_ref[...])
        y = _dot(act.astype(BF16), s2_ref[...])
        for k in range(TOP_K):
            y = y + ew_ref[rows, k:k + 1] * _unpack_pair(yga_ref[k, rows, :], ygb_ref[k, rows, :])
        o_ref[rows, :] = x1_ref[rows, :] + mod_ref[:, 5120:6144] * _rms_rows(y, g3_ref[...])


def _stage_g(layer, x1, modt, g3, yga, ygb, ew_rows, h2a, h2b, s1, s3, s2):
    row = lambda w: pl.BlockSpec((TB, w), lambda i: (i, 0))
    picked = pl.BlockSpec((TOP_K, TB, PACKED), lambda i: (0, i, 0))
    return pl.pallas_call(
        _stage_g_kernel,
        grid=(NB,),
        in_specs=[row(D_MODEL), pl.BlockSpec((None, 1, N_MOD * D_MODEL), lambda i: (i, 0, 0)),
                  _const_spec((1, D_MODEL)), picked, picked, row(8), row(PACKED), row(PACKED),
                  _layer_spec((D_MODEL, SHARED_FF), layer), _layer_spec((D_MODEL, SHARED_FF), layer),
                  _layer_spec((SHARED_FF, D_MODEL), layer)],
        out_specs=row(D_MODEL),
        out_shape=jax.ShapeDtypeStruct((N_TOK, D_MODEL), F32),
        compiler_params=_cparams(1),
        name="stage_g",
    )(x1, modt, g3, yga, ygb, ew_rows, h2a, h2b, s1, s3, s2)


def _rope_tables():
    t = jnp.arange(DEC_SEQ)
    pos = jnp.stack([(t // GRID_W).astype(F32), (t % GRID_W).astype(F32)], axis=-1)

    def table(r):
        n_freq = r // 4
        inv = ROPE_BASE ** (-jnp.arange(n_freq, dtype=F32) / n_freq)
        ang = pos[:, :, None] * inv
        cos = jnp.cos(ang)
        sin = jnp.sin(ang)
        cos_t = jnp.stack([cos, cos], axis=2).reshape(DEC_SEQ, r)
        sin_t = jnp.stack([-sin, sin], axis=2).reshape(DEC_SEQ, r)
        return cos_t, sin_t

    c64, s64 = table(SWA_HEAD_DIM)
    c32, s32 = table(MLA_ROPE)
    lat = jnp.concatenate([jnp.tile(c64, (1, 8)), jnp.tile(s64, (1, 8)),
                           jnp.tile(c32, (1, 4)), jnp.tile(s32, (1, 4))], axis=1)
    ident = jnp.concatenate([jnp.ones((TB, 512), F32), jnp.zeros((TB, 512), F32),
                             jnp.ones((TB, 128), F32), jnp.zeros((TB, 128), F32)], axis=1)
    return jnp.concatenate([ident, lat], axis=0)


def _dft_pair(n):
    k = jnp.arange(n, dtype=jnp.int32)
    ang = ((k[:, None] * k[None, :]) % n).astype(F32) * (2.0 * math.pi / n)
    return jnp.cos(ang), jnp.sin(ang)


def _fnet_tables():
    c64, s64 = _dft_pair(FNET_GROUP_DIM)
    eye = jnp.eye(FNET_GROUPS, dtype=F32)
    bd = jnp.concatenate([jnp.kron(eye, c64), jnp.kron(eye, s64)], axis=1).astype(BF16)
    mats = []
    for t_len in (SEQ, DEC_SEQ):
        c, s = _dft_pair(t_len)
        mats.append(jnp.concatenate([c, -s], axis=1).astype(BF16))
    return bd, mats[0], mats[1]


def _layer_weights(l, w_in, w_uq, w_ukv):
    w = w_in[l]
    wide = jnp.concatenate([w[:, 0:1024], w[:, 1056:1824], w[:, 1024:1056],
                            jnp.zeros((D_MODEL, 96), F32)], axis=1).astype(BF16)

    uq = w_uq[l].reshape(MLA_Q_RANK, MLA_HEADS, MLA_NOPE + MLA_ROPE)
    z32 = jnp.zeros((MLA_Q_RANK, MLA_HEADS, 32), F32)
    wqa = jnp.concatenate([uq, z32], axis=2).reshape(MLA_Q_RANK, 1024).astype(BF16)
    ukv = w_ukv[l].reshape(MLA_KV_RANK, MLA_HEADS, MLA_NOPE + MLA_V)
    wk = jnp.concatenate([ukv[:, :, :MLA_NOPE], jnp.zeros((MLA_KV_RANK, MLA_HEADS, 64), F32)],
                         axis=2).reshape(MLA_KV_RANK, 1024).astype(BF16)
    wv = ukv[:, :, MLA_NOPE:].reshape(MLA_KV_RANK, 512).astype(BF16)
    return wide, wqa, wk, wv


def _rope_placement():
    e = np.zeros((128, 1024), np.float32)
    for hd in range(MLA_HEADS):
        for i in range(MLA_ROPE):
            e[i, hd * 128 + MLA_NOPE + i] = 1.0
    return jnp.asarray(e, BF16)


def _moe_dispatch_plan(eidx, epos, counts):
    padded = ((counts + TE - 1) // TE) * TE
    ends = jnp.cumsum(padded)
    offs = ends - padded
    ids = jnp.arange(N_EXPERTS, dtype=jnp.int32)
    picked_off = jnp.sum(jnp.where(eidx[:, :, None] == ids, offs, 0), axis=-1)
    slot = picked_off + epos
    starts = jnp.arange(NTE, dtype=jnp.int32) * TE
    tile_expert = jnp.sum((ends[None, :] <= starts[:, None]).astype(jnp.int32), axis=1)
    tile_expert = jnp.minimum(tile_expert, N_EXPERTS - 1)
    tile_valid = (starts < ends[-1]).astype(jnp.int32)
    return slot, tile_expert, tile_valid


def kernel(x_prompt, x_sample, cache_mla_ckv, cache_mla_krope, cache_swa_k, cache_swa_v, c, c_ctx,
           ada_w, ada_b, norm_g, w_in, q_norm, kv_norm, w_fnet, w_uq, w_ukv, w_mla_o, swa_sink,
           w_swa_o, w_gate, b_gate, w_out, router_w, router_bias, exp_w1, exp_w3, exp_w2,
           shared_w1, shared_w3, shared_w2):
    x = jnp.concatenate([x_prompt.reshape(N_CTX, D_MODEL), x_sample.reshape(N_LAT, D_MODEL)], axis=0)

    cond8 = jnp.concatenate([c_ctx[None, :], c, jnp.zeros((3, D_MODEL), F32)], axis=0)
    mod = _modulation(cond8, ada_w, ada_b)
    tile_cond = np.concatenate([np.zeros(NB_CTX, np.int32),
                                1 + np.arange(NB - NB_CTX, dtype=np.int32) // LAT_BLOCKS])

    tab = _rope_tables()
    bd, f_ctx, f_lat = _fnet_tables()
    e_mat = _rope_placement()
    tri = jnp.asarray(np.triu(np.ones((TM, TM), np.float32), 1), BF16)
    w_gate_b, w_fnet_b, w_mla_o_b, w_swa_o_b, w_out_b, sw1_b, sw3_b, sw2_b = (
        w.astype(BF16) for w in (w_gate, w_fnet, w_mla_o, w_swa_o, w_out, shared_w1, shared_w3, shared_w2))

    new_ckv, new_kr, new_k, new_v = [], [], [], []
    for l in range(DEPTH):
        modt = mod[l][tile_cond][:, None, :]
        wide, wqa, wk, wv = _layer_weights(l, w_in, w_uq, w_ukv)
        ng = norm_g[l]

        fin, ckv, kr, sq, sk, sv, gates, q_m, k_m, v_m = _stage_a(
            l, x, modt, ng[0:1], wide, w_gate_b, b_gate[l][None, :],
            q_norm[l][None, :], kv_norm[l][None, :], tab, wqa, wk, e_mat, wv)

        new_ckv.append(ckv[:N_CTX].reshape(BATCH, SEQ, MLA_KV_RANK))
        new_kr.append(kr[:N_CTX, :MLA_ROPE].reshape(BATCH, SEQ, MLA_ROPE))
        new_k.append(sk[:N_CTX].reshape(BATCH, SEQ, SWA_KV_HEADS, SWA_HEAD_DIM))
        new_v.append(sv[:N_CTX].reshape(BATCH, SEQ, SWA_KV_HEADS, SWA_HEAD_DIM))

        fn = (_fnet(fin, f_ctx, bd, BATCH, SEQ, 0),
              _fnet(fin, f_lat, bd, DEC_BATCH, DEC_SEQ, N_CTX // DEC_SEQ))

        kr_cache = jnp.pad(cache_mla_krope[:, l].reshape(N_CACHE, MLA_ROPE), ((0, 0), (0, 96)))
        k_c, v_c = _mla_cache_kv(cache_mla_ckv[:, l].reshape(N_CACHE, MLA_KV_RANK), kr_cache,
                                 wk, e_mat, wv)
        om = (_mla_attn(q_m, k_m, v_m, k_c, v_c, latent=False),
              _mla_attn(q_m, k_m, v_m, k_c, v_c, latent=True))

        ck = cache_swa_k[:, l].reshape(DEC_BATCH, PAST_LEN, 128)
        cv = cache_swa_v[:, l].reshape(DEC_BATCH, PAST_LEN, 128)
        osw = (_swa_attn(swa_sink[l], sq, sk, sv, ck, cv, latent=False),
               _swa_attn(swa_sink[l], sq, sk, sv, ck, cv, latent=True))

        x1, h2a, h2b, eidx, epos, ew, counts = _stage_e(
            l, x, modt, ng[1:2], ng[2:3], fn + om + osw, gates,
            w_fnet_b, w_mla_o_b, w_swa_o_b, w_out_b,
            router_w[l].T.astype(BF16), router_bias[l][:, None], tri)
        slot, tile_expert, tile_valid = _moe_dispatch_plan(eidx, epos, counts[:, 0])
        xsa = _sc_scatter_rows(h2a, slot)
        xsb = _sc_scatter_rows(h2b, slot)
        ysa, ysb = _experts(l, tile_expert, tile_valid, xsa, xsb, exp_w1, exp_w3, exp_w2)
        picks = slot[:TOP_K].reshape(1, TOP_K * N_TOK)
        yga = _sc_gather_rows(ysa, picks).reshape(TOP_K, N_TOK, PACKED)
        ygb = _sc_gather_rows(ysb, picks).reshape(TOP_K, N_TOK, PACKED)
        x = _stage_g(l, x1, modt, ng[3:4], yga, ygb, ew.T, h2a, h2b, sw1_b, sw3_b, sw2_b)

    y_p = x[:N_CTX].reshape(BATCH, SEQ, D_MODEL)
    y_s = x[N_CTX:].reshape(DEC_BATCH, DEC_SEQ, D_MODEL)
    return (y_p, y_s, jnp.stack(new_ckv, axis=1), jnp.stack(new_kr, axis=1),
            jnp.stack(new_k, axis=1), jnp.stack(new_v, axis=1))
```

```python
import functools
import math

import numpy as np
import jax
import jax.numpy as jnp
from jax import lax
from jax.experimental import pallas as pl
from jax.experimental.pallas import tpu as pltpu
from jax.experimental.pallas import tpu_sc as plsc

D_MODEL = 1024
BATCH = 16
SEQ = 256
DEPTH = 2
DEC_BATCH = 4
DEC_SEQ = 2048
PAST_LEN = 512
GRID_W = 64
EPS = 1e-6
ROPE_BASE = 10000.0
NEG_INF = -1e30

FNET_GROUPS = 8
FNET_GROUP_DIM = 64
FNET_WIDTH = 512
MLA_HEADS = 8
MLA_Q_RANK = 384
MLA_KV_RANK = 128
MLA_NOPE = 64
MLA_ROPE = 32
MLA_V = 64
MLA_SCALE = (MLA_NOPE + MLA_ROPE) ** -0.5
LOG2E = math.log2(math.e)
SWA_HEADS = 8
SWA_KV_HEADS = 2
SWA_HEAD_DIM = 64
SWA_WINDOW = 128
SWA_SCALE = SWA_HEAD_DIM ** -0.5
N_MOD = 6
N_EXPERTS = 64
N_EXPERT_GROUPS = 8
TOPK_GROUPS = 4
TOP_K = 6
EXPERT_FF = 256
SHARED_FF = 256
ROUTED_SCALE = 2.5

LANES = 128
TM = 256
N_CTX = BATCH * SEQ
N_LAT = DEC_BATCH * DEC_SEQ
N_TOK = N_CTX + N_LAT
N_CACHE = DEC_BATCH * PAST_LEN
NT_CTX = N_CTX // TM
NT_LAT = N_LAT // TM
NT = N_TOK // TM
LAT_TILES = DEC_SEQ // TM
TB = 512
NB = N_TOK // TB
NB_CTX = N_CTX // TB
LAT_BLOCKS = DEC_SEQ // TB
MLA_LAT_TQ = 256
TE = 512
S_MAX = N_TOK * TOP_K + N_EXPERTS * TE
NTE = S_MAX // TE
EXPERT_ROWS = 256
VMEM_LIMIT = 56 * 1024 * 1024
PACKED = D_MODEL // 4
SC_ROWS = 128
SC_CORES = 2
SC_SUBCORES = 16

A_F = (0, 512)
A_QD = (512, 896)
A_KV = (896, 1024)
A_SQ = (1024, 1536)
A_SK = (1536, 1664)
A_SV = (1664, 1792)
A_KR = (1792, 1920)
W_IN_WIDE = 1920
TAB_W = 1280

F32 = jnp.float32
BF16 = jnp.bfloat16


def _cparams(n_axes, parallel=False):
    sem = ("parallel" if parallel else "arbitrary",) * n_axes
    return pltpu.CompilerParams(dimension_semantics=sem, vmem_limit_bytes=VMEM_LIMIT)


def _dot(a, b):
    return jnp.dot(a, b, preferred_element_type=F32)


def _dot_nt(a, b):
    return lax.dot_general(a, b, (((1,), (1,)), ((), ())), preferred_element_type=F32)


def _rms_rows(v, g):
    return v * lax.rsqrt(jnp.mean(v * v, axis=-1, keepdims=True) + EPS) * g


def _pack_rows(v):
    half = v.shape[1] // 2
    lo = lax.bitcast_convert_type(v[:, :half].astype(BF16).astype(F32), jnp.int32)
    hi = lax.bitcast_convert_type(v[:, half:].astype(BF16).astype(F32), jnp.int32)
    return jnp.bitwise_or(jnp.bitwise_and(hi, -65536), jnp.bitwise_and(jnp.right_shift(lo, 16), 65535))


def _unpack_rows(w):
    lo = lax.bitcast_convert_type(jnp.left_shift(w, 16), F32)
    hi = lax.bitcast_convert_type(jnp.bitwise_and(w, -65536), F32)
    return jnp.concatenate([lo, hi], axis=1)


def _pack_pair(v):
    half = v.shape[1] // 2
    return _pack_rows(v[:, :half]), _pack_rows(v[:, half:])


def _unpack_pair(a, b):
    return jnp.concatenate([_unpack_rows(a), _unpack_rows(b)], axis=1)


def _const_spec(shape):
    return pl.BlockSpec(shape, lambda *_: (0,) * len(shape))


def _layer_spec(shape, layer):
    return pl.BlockSpec((None,) + shape, lambda *_: (layer,) + (0,) * len(shape))


def _ctx_rows(width):
    return pl.BlockSpec((TB, width), lambda i: (jnp.minimum(i, NB_CTX - 1), 0))


def _lat_rows(width):
    return pl.BlockSpec((TB, width), lambda i: (jnp.maximum(i - NB_CTX, 0), 0))


def _tab_row_block(i):
    return jnp.where(i < NB_CTX, 0, 1 + (i - NB_CTX) % LAT_BLOCKS)


def _mod_kernel(cond_ref, w_ref, b_ref, o_ref):
    c = cond_ref[...]
    a = (c * jax.nn.sigmoid(c)).astype(BF16)
    o_ref[...] = _dot(a, w_ref[...].astype(BF16)) + b_ref[...]


def _modulation(cond8, ada_w, ada_b):
    tn = 512
    nj = N_MOD * D_MODEL // tn
    return pl.pallas_call(
        _mod_kernel,
        grid=(DEPTH, nj),
        in_specs=[
            pl.BlockSpec((8, D_MODEL), lambda l, j: (0, 0)),
            pl.BlockSpec((None, D_MODEL, tn), lambda l, j: (l, 0, j)),
            pl.BlockSpec((None, 1, tn), lambda l, j: (l, 0, j)),
        ],
        out_specs=pl.BlockSpec((None, 8, tn), lambda l, j: (l, 0, j)),
        out_shape=jax.ShapeDtypeStruct((DEPTH, 8, N_MOD * D_MODEL), F32),
        compiler_params=_cparams(2),
        name="modulation",
    )(cond8, ada_w, ada_b.reshape(DEPTH, 1, N_MOD * D_MODEL))


def _half_swap(x, half):
    n = x.shape[1]
    lane = lax.broadcasted_iota(jnp.int32, (1, n), 1)
    return jnp.where((lane & half) == 0, pltpu.roll(x, n - half, 1), pltpu.roll(x, half, 1))


def _mla_expand(rows, cq, ckv, kr, cos32, sin32, wqa_ref, wk_ref, e_ref, wv_ref,
                q_ref, k_ref, v_ref):
    if q_ref is not None:
        lane = lax.broadcasted_iota(jnp.int32, (1, LANES), 1)
        rope_lane = jnp.logical_and(lane >= MLA_NOPE, lane < MLA_NOPE + MLA_ROPE)
        cos_h = jnp.where(rope_lane, cos32, 1.0)
        sin_h = jnp.where(rope_lane, sin32, 0.0)
        for hd in range(MLA_HEADS):
            lo, hi = hd * LANES, (hd + 1) * LANES
            q = _dot(cq, wqa_ref[:, lo:hi])
            q = q * cos_h + _half_swap(q, MLA_ROPE // 4) * sin_h
            q_ref[rows, lo:hi] = (q * (MLA_SCALE * LOG2E)).astype(BF16)
    k_ref[rows, :] = (_dot(ckv, wk_ref[...]) + _dot(kr, e_ref[...])).astype(BF16)
    v_ref[rows, :] = _dot(ckv, wv_ref[...]).astype(BF16)


def _chunks():
    return [pl.ds(r * TM, TM) for r in range(TB // TM)]


def _stage_a_kernel(xc_ref, xl_ref, mod_ref, g_ref, win_ref, wg_ref, bg_ref, qn_ref, kvn_ref, tab_ref,
                    wqa_ref, wk_ref, e_ref, wv_ref,
                    fin_ref, ckv_ref, kr_ref, sq_ref, sk_ref, sv_ref, gates_ref,
                    qm_ref, km_ref, vm_ref):
    is_ctx = pl.program_id(0) < NB_CTX
    for rows in _chunks():
        x = jnp.where(is_ctx, xc_ref[rows, :], xl_ref[rows, :])
        h = (_rms_rows(x, g_ref[...]) * (1.0 + mod_ref[:, 1024:2048]) + mod_ref[:, 0:1024]).astype(BF16)

        def proj(seg):
            return _dot(h, win_ref[:, seg[0]:seg[1]])

        fin_ref[rows, :] = proj(A_F).astype(BF16)
        cq = _rms_rows(proj(A_QD), qn_ref[...]).astype(BF16)
        ckv = _rms_rows(proj(A_KV), kvn_ref[...])
        ckv_ref[rows, :] = ckv
        cos64 = tab_ref[rows, 0:512]
        sin64 = tab_ref[rows, 512:1024]
        sq = proj(A_SQ)
        sq = sq * cos64 + _half_swap(sq, SWA_HEAD_DIM // 4) * sin64
        sq_ref[rows, :] = (sq * (SWA_SCALE * LOG2E)).astype(BF16)
        sk = proj(A_SK)
        sk_ref[rows, :] = sk * cos64[:, 0:128] + _half_swap(sk, SWA_HEAD_DIM // 4) * sin64[:, 0:128]
        sv_ref[rows, :] = proj(A_SV)
        cos32 = tab_ref[rows, 1024:1152]
        sin32 = tab_ref[rows, 1152:1280]
        kr = proj(A_KR)
        kr = kr * cos32 + _half_swap(kr, MLA_ROPE // 4) * sin32
        kr_ref[rows, :] = kr
        _mla_expand(rows, cq, ckv.astype(BF16), kr.astype(BF16), cos32, sin32,
                    wqa_ref, wk_ref, e_ref, wv_ref, qm_ref, km_ref, vm_ref)
        for c in range(3):
            lo, hi = c * D_MODEL, (c + 1) * D_MODEL
            gates_ref[rows, lo:hi] = jax.nn.sigmoid(_dot(h, wg_ref[:, lo:hi]) + bg_ref[:, lo:hi]).astype(BF16)


def _stage_a(layer, xc, xl, modt, g0, w_in_wide, w_gate, b_gate, q_norm, kv_norm, tab, wqa, wk, e_mat, wv):
    row = lambda w: pl.BlockSpec((TB, w), lambda i: (i, 0))
    outs = [(512, BF16), (128, F32), (128, F32), (512, BF16), (128, F32), (128, F32),
            (3 * D_MODEL, BF16), (1024, BF16), (1024, BF16), (512, BF16)]
    return pl.pallas_call(
        _stage_a_kernel,
        grid=(NB,),
        in_specs=[
            _ctx_rows(D_MODEL), _lat_rows(D_MODEL),
            pl.BlockSpec((None, 1, N_MOD * D_MODEL), lambda i: (i, 0, 0)),
            _const_spec((1, D_MODEL)),
            _const_spec((D_MODEL, W_IN_WIDE)),
            _layer_spec((D_MODEL, 3 * D_MODEL), layer),
            _const_spec((1, 3 * D_MODEL)),
            _const_spec((1, MLA_Q_RANK)),
            _const_spec((1, MLA_KV_RANK)),
            pl.BlockSpec((TB, TAB_W), lambda i: (_tab_row_block(i), 0)),
            _const_spec((MLA_Q_RANK, 1024)),
            _const_spec((128, 1024)), _const_spec((128, 1024)), _const_spec((128, 512)),
        ],
        out_specs=[row(w) for w, _ in outs],
        out_shape=[jax.ShapeDtypeStruct((N_TOK, w), dt) for w, dt in outs],
        compiler_params=_cparams(1),
        name="stage_a",
    )(xc, xl, modt, g0, w_in_wide, w_gate, b_gate, q_norm, kv_norm, tab, wqa, wk, e_mat, wv)


def _fnet_kernel(t_len, scale, fin_ref, f_ref, bd_ref, o_ref, zz_ref):
    @pl.when(pl.program_id(1) == 0)
    def _():
        z = fin_ref[...]
        zz_ref[0:t_len, :] = _dot(z, bd_ref[:, 0:512]).astype(BF16)
        zz_ref[t_len:2 * t_len, :] = _dot(z, bd_ref[:, 512:1024]).astype(BF16)

    o_ref[...] = (_dot(f_ref[...], zz_ref[...]) * scale).astype(BF16)


def _fnet(fin, fmat, bd, n_batch, t_len, row_block0):
    scale = 1.0 / math.sqrt(t_len * FNET_GROUP_DIM)
    return pl.pallas_call(
        functools.partial(_fnet_kernel, t_len, scale),
        grid=(n_batch, t_len // TM),
        in_specs=[
            pl.BlockSpec((t_len, FNET_WIDTH), lambda b, i: (row_block0 + b, 0)),
            pl.BlockSpec((TM, 2 * t_len), lambda b, i: (i, 0)),
            _const_spec((FNET_WIDTH, 2 * FNET_WIDTH)),
        ],
        out_specs=pl.BlockSpec((TM, FNET_WIDTH), lambda b, i: (b * (t_len // TM) + i, 0)),
        out_shape=jax.ShapeDtypeStruct((n_batch * t_len, FNET_WIDTH), BF16),
        scratch_shapes=[pltpu.VMEM((2 * t_len, FNET_WIDTH), BF16)],
        compiler_params=_cparams(2),
        name=f"fnet_{t_len}",
    )(fin, fmat, bd)


def _mla_cache_kernel(ckv_ref, kr_ref, wk_ref, e_ref, wv_ref, k_ref, v_ref):
    _mla_expand(slice(None), None, ckv_ref[...].astype(BF16), kr_ref[...].astype(BF16), None, None,
                None, wk_ref, e_ref, wv_ref, None, k_ref, v_ref)


def _mla_cache_kv(ckv_cache, kr_cache, wk, e_mat, wv):
    row = lambda w: pl.BlockSpec((TM, w), lambda i: (i, 0))
    return pl.pallas_call(
        _mla_cache_kernel,
        grid=(N_CACHE // TM,),
        in_specs=[row(128), row(128),
                  _const_spec((128, 1024)), _const_spec((128, 1024)), _const_spec((128, 512))],
        out_specs=[row(1024), row(512)],
        out_shape=[jax.ShapeDtypeStruct((N_CACHE, 1024), BF16),
                   jax.ShapeDtypeStruct((N_CACHE, 512), BF16)],
        compiler_params=_cparams(1),
        name="mla_cache_kv",
    )(ckv_cache, kr_cache, wk, e_mat, wv)


def _mla_attn_kernel(n_seg, q_ref, *refs):
    k_refs = refs[0:n_seg]
    v_refs = refs[n_seg:2 * n_seg]
    o_ref = refs[2 * n_seg]
    lane = lax.broadcasted_iota(jnp.int32, (1, LANES), 1)
    low = lane < MLA_V
    outs = []
    for hh in range(2):
        q = q_ref[:, hh * LANES:(hh + 1) * LANES]
        ss = [_dot_nt(q, k[:, hh * LANES:(hh + 1) * LANES]) for k in k_refs]
        m = functools.reduce(jnp.maximum, [s.max(axis=-1, keepdims=True) for s in ss])
        keep = low if hh == 0 else jnp.logical_not(low)
        sum_lane = MLA_V if hh == 0 else 0
        po = None
        for s, v_ref in zip(ss, v_refs):
            v = v_ref[...]
            vm = jnp.where(lane == sum_lane, jnp.ones_like(v), jnp.where(keep, v, jnp.zeros_like(v)))
            t = _dot(jnp.exp2(s - m).astype(BF16), vm)
            po = t if po is None else po + t
        outs.append(po / po[:, sum_lane:sum_lane + 1])
    o_ref[...] = jnp.where(low, outs[0], outs[1]).astype(BF16)


def _mla_attn(q_all, k_all, v_all, k_cache, v_cache, latent):
    if latent:
        tq = MLA_LAT_TQ
        n_b, n_q = DEC_BATCH, DEC_SEQ // tq
        q0 = N_CTX // tq
        kv_specs = [
            pl.BlockSpec((PAST_LEN, 256), lambda b, hp, i: (b, hp)),
            pl.BlockSpec((DEC_SEQ, 256), lambda b, hp, i: (N_CTX // DEC_SEQ + b, hp)),
            pl.BlockSpec((PAST_LEN, 128), lambda b, hp, i: (b, hp)),
            pl.BlockSpec((DEC_SEQ, 128), lambda b, hp, i: (N_CTX // DEC_SEQ + b, hp)),
        ]
        args = (q_all, k_cache, k_all, v_cache, v_all)
        n_seg = 2
    else:
        tq = SEQ
        n_b, n_q = BATCH, 1
        q0 = 0
        kv_specs = [
            pl.BlockSpec((SEQ, 256), lambda b, hp, i: (b, hp)),
            pl.BlockSpec((SEQ, 128), lambda b, hp, i: (b, hp)),
        ]
        args = (q_all, k_all, v_all)
        n_seg = 1
    return pl.pallas_call(
        functools.partial(_mla_attn_kernel, n_seg),
        grid=(n_b, MLA_HEADS // 2, n_q),
        in_specs=[pl.BlockSpec((tq, 256), lambda b, hp, i: (q0 + b * n_q + i, hp))] + kv_specs,
        out_specs=pl.BlockSpec((tq, 128), lambda b, hp, i: (b * n_q + i, hp)),
        out_shape=jax.ShapeDtypeStruct((n_b * n_q * tq, MLA_HEADS * MLA_V), BF16),
        compiler_params=_cparams(3),
        name="mla_attn_lat" if latent else "mla_attn_ctx",
    )(*args)


def _swa_kernel(windowed, n_qb, sink_ref, q_ref, *refs):
    n_seg = 4 if windowed else 1
    k_refs = refs[0:n_seg]
    v_refs = refs[n_seg:2 * n_seg]
    o_ref = refs[2 * n_seg]
    tq = q_ref.shape[0]
    qb = pl.program_id(1)
    lane = lax.broadcasted_iota(jnp.int32, (1, LANES), 1)
    low = lane < SWA_HEAD_DIM
    high = jnp.logical_not(low)

    k_all = jnp.concatenate([r[...] for r in k_refs], axis=0)
    v_all = jnp.concatenate([r[...] for r in v_refs], axis=0)
    k_sw = pltpu.roll(k_all, SWA_HEAD_DIM, 1)
    v_sw = pltpu.roll(v_all, SWA_HEAD_DIM, 1)

    if windowed:
        qi = lax.broadcasted_iota(jnp.int32, (2 * tq, SWA_WINDOW), 0) % tq
        kj = lax.broadcasted_iota(jnp.int32, (2 * tq, SWA_WINDOW), 1)
        bias_prev = jnp.where(jnp.logical_and(kj >= qi, qb > 0), 0.0, NEG_INF)
        bias_next = jnp.where(jnp.logical_and(kj <= qi, qb < n_qb - 1), 0.0, NEG_INF)
    top_rows = lax.broadcasted_iota(jnp.int32, (2 * tq, 1), 0) < tq

    for g in range(SWA_KV_HEADS):
        qs = jnp.concatenate([q_ref[:, 256 * g:256 * g + 128],
                              q_ref[:, 256 * g + 128:256 * g + 256]], axis=0)
        halves = []
        for half in range(2):
            keep = low if half == 0 else high
            sum_lane = SWA_HEAD_DIM if half == 0 else 0
            straight = (g == half)
            kh = jnp.where(keep, k_all if straight else k_sw, 0.0).astype(BF16)
            vh = jnp.where(lane == sum_lane, 1.0,
                           jnp.where(keep, v_all if straight else v_sw, 0.0)).astype(BF16)
            s = _dot_nt(qs, kh)
            if windowed:
                c0, c1, c2 = PAST_LEN, PAST_LEN + SWA_WINDOW, PAST_LEN + 2 * SWA_WINDOW
                s = jnp.concatenate([s[:, :c0], s[:, c0:c1] + bias_prev, s[:, c1:c2],
                                     s[:, c2:] + bias_next], axis=1)
            sink = jnp.where(top_rows, sink_ref[4 * g + half], sink_ref[4 * g + 2 + half]) * LOG2E
            m = jnp.maximum(s.max(axis=-1, keepdims=True), sink)
            po = _dot(jnp.exp2(s - m).astype(BF16), vh)
            halves.append(po / (po[:, sum_lane:sum_lane + 1] + jnp.exp2(sink - m)))
        out = jnp.where(low, halves[0], halves[1])
        o_ref[:, 256 * g:256 * g + 128] = out[0:tq].astype(BF16)
        o_ref[:, 256 * g + 128:256 * g + 256] = out[tq:2 * tq].astype(BF16)


def _swa_attn(sink, sq, sk, sv, cache_k, cache_v, latent):
    smem = pl.BlockSpec(memory_space=pltpu.SMEM)
    if latent:
        tq = SWA_WINDOW
        n_b, n_qb = DEC_BATCH, DEC_SEQ // tq
        base = N_CTX // tq

        def prev(b, i):
            return (base + b * n_qb + jnp.maximum(i - 1, 0), 0)

        def cur(b, i):
            return (base + b * n_qb + i, 0)

        def nxt(b, i):
            return (base + b * n_qb + jnp.minimum(i + 1, n_qb - 1), 0)

        cache = pl.BlockSpec((None, PAST_LEN, 128), lambda b, i: (b, 0, 0))
        blk = lambda f: pl.BlockSpec((tq, 128), f)
        kv_specs = [cache, blk(prev), blk(cur), blk(nxt)] * 2
        args = (cache_k, sk, sk, sk, cache_v, sv, sv, sv)
        q_spec = pl.BlockSpec((tq, 512), cur)
        o_spec = pl.BlockSpec((tq, 512), lambda b, i: (b * n_qb + i, 0))
    else:
        tq = SEQ
        n_b, n_qb = BATCH, 1
        blk = pl.BlockSpec((tq, 128), lambda b, i: (b, 0))
        kv_specs = [blk, blk]
        args = (sk, sv)
        q_spec = pl.BlockSpec((tq, 512), lambda b, i: (b, 0))
        o_spec = q_spec
    return pl.pallas_call(
        functools.partial(_swa_kernel, latent, n_qb),
        grid=(n_b, n_qb),
        in_specs=[smem, q_spec] + kv_specs,
        out_specs=o_spec,
        out_shape=jax.ShapeDtypeStruct((n_b * n_qb * tq, 512), BF16),
        compiler_params=_cparams(2),
        name="swa_lat" if latent else "swa_ctx",
    )(sink, sq, *args)


def _route(h, rwt_ref, rb_ref, tri_ref, carry):
    gsz = N_EXPERTS // N_EXPERT_GROUPS
    scores = jax.nn.sigmoid(_dot_nt(rwt_ref[...], h))
    biased = scores + rb_ref[...]
    mem = lax.broadcasted_iota(jnp.int32, (gsz, TM), 0).astype(F32)
    gs_rows = []
    for g in range(N_EXPERT_GROUPS):
        bg = biased[g * gsz:(g + 1) * gsz, :]
        m1 = bg.max(axis=0, keepdims=True)
        first = jnp.min(jnp.where(bg == m1, mem, float(gsz)), axis=0, keepdims=True)
        m2 = jnp.where(mem == first, -jnp.inf, bg).max(axis=0, keepdims=True)
        gs_rows.append(m1 + m2)
    gs = jnp.concatenate(gs_rows, axis=0)
    gid = lax.broadcasted_iota(jnp.int32, gs.shape, 0).astype(F32)
    gsel = jnp.zeros(gs.shape, F32)
    for _ in range(TOPK_GROUPS):
        mx = gs.max(axis=0, keepdims=True)
        pick = gid == jnp.min(jnp.where(gs == mx, gid, float(N_EXPERT_GROUPS)), axis=0, keepdims=True)
        gsel = jnp.where(pick, 1.0, gsel)
        gs = jnp.where(pick, -jnp.inf, gs)
    emask = jnp.concatenate(
        [jnp.broadcast_to(gsel[g:g + 1, :], (gsz, TM)) for g in range(N_EXPERT_GROUPS)], axis=0)
    cand = jnp.where(emask > 0.5, biased, NEG_INF)
    eid = lax.broadcasted_iota(jnp.int32, cand.shape, 0).astype(F32)
    picks = []
    self32 = jnp.zeros(cand.shape, F32)
    for _ in range(TOP_K):
        mx = cand.max(axis=0, keepdims=True)
        pick = eid == jnp.min(jnp.where(cand == mx, eid, float(N_EXPERTS)), axis=0, keepdims=True)
        picks.append(pick)
        self32 = jnp.where(pick, 1.0, self32)
        cand = jnp.where(pick, -jnp.inf, cand)
    pos = _dot(self32.astype(BF16), tri_ref[...]) + carry
    sel_scores = [jnp.sum(jnp.where(p, scores, 0.0), axis=0, keepdims=True) for p in picks]
    wsum = functools.reduce(lambda a, b: a + b, sel_scores)
    zero_f = jnp.zeros((2, TM), F32)
    eidx = [jnp.sum(jnp.where(p, eid, 0.0), axis=0, keepdims=True) for p in picks]
    epos = [jnp.sum(jnp.where(p, pos, 0.0), axis=0, keepdims=True) for p in picks]
    ew = [s / wsum * ROUTED_SCALE for s in sel_scores]
    return (jnp.concatenate(eidx + [zero_f], axis=0).astype(jnp.int32),
            jnp.concatenate(epos + [zero_f], axis=0).astype(jnp.int32),
            jnp.concatenate(ew + [zero_f], axis=0),
            carry + jnp.sum(self32, axis=1, keepdims=True))


def _stage_e_kernel(xc_ref, xl_ref, mod_ref, g1_ref, g2_ref, fnc_ref, fnl_ref, omc_ref, oml_ref, osc_ref, osl_ref,
                    gates_ref, wf_ref, wm_ref, ws_ref, wo_ref, rwt_ref, rb_ref, tri_ref,
                    x1_ref, h2a_ref, h2b_ref, eidx_ref, epos_ref, ew_ref, cnt_ref, carry_ref):
    @pl.when(pl.program_id(0) == 0)
    def _():
        carry_ref[...] = jnp.zeros_like(carry_ref)

    is_ctx = pl.program_id(0) < NB_CTX
    carry = carry_ref[...]
    for r, rows in enumerate(_chunks()):
        fn = jnp.where(is_ctx, fnc_ref[rows, :], fnl_ref[rows, :])
        om = jnp.where(is_ctx, omc_ref[rows, :], oml_ref[rows, :])
        osw = jnp.where(is_ctx, osc_ref[rows, :], osl_ref[rows, :])
        merged = (gates_ref[rows, 0:1024].astype(F32) * _dot(fn, wf_ref[...])
                  + gates_ref[rows, 1024:2048].astype(F32) * _dot(om, wm_ref[...])
                  + gates_ref[rows, 2048:3072].astype(F32) * _dot(osw, ws_ref[...]))
        mix = _dot(merged.astype(BF16), wo_ref[...])
        x = jnp.where(is_ctx, xc_ref[rows, :], xl_ref[rows, :])
        x1 = x + mod_ref[:, 2048:3072] * _rms_rows(mix, g1_ref[...])
        x1_ref[rows, :] = x1
        h2 = _rms_rows(x1, g2_ref[...]) * (1.0 + mod_ref[:, 4096:5120]) + mod_ref[:, 3072:4096]
        h2a_ref[rows, :], h2b_ref[rows, :] = _pack_pair(h2)
        cols = pl.ds(r * TM, TM)
        eidx_ref[:, cols], epos_ref[:, cols], ew_ref[:, cols], carry = _route(
            h2.astype(BF16), rwt_ref, rb_ref, tri_ref, carry)
    carry_ref[...] = carry
    cnt_ref[...] = jnp.broadcast_to(carry, cnt_ref.shape).astype(jnp.int32)


def _stage_e(layer, xc, xl, modt, g1, g2, mixed, gates, wf, wm, ws, wo, rwt, rbias, tri):
    row = lambda w: pl.BlockSpec((TB, w), lambda i: (i, 0))
    ctx, lat = _ctx_rows(512), _lat_rows(512)
    col = lambda dt: (pl.BlockSpec((8, TB), lambda i: (0, i)), jax.ShapeDtypeStruct((8, N_TOK), dt))
    picks = [col(jnp.int32), col(jnp.int32), col(F32)]
    return pl.pallas_call(
        _stage_e_kernel,
        grid=(NB,),
        in_specs=[
            _ctx_rows(D_MODEL), _lat_rows(D_MODEL),
            pl.BlockSpec((None, 1, N_MOD * D_MODEL), lambda i: (i, 0, 0)),
            _const_spec((1, D_MODEL)), _const_spec((1, D_MODEL)),
            ctx, lat, ctx, lat, ctx, lat, row(3 * D_MODEL),
            _layer_spec((512, D_MODEL), layer), _layer_spec((512, D_MODEL), layer),
            _layer_spec((512, D_MODEL), layer), _layer_spec((D_MODEL, D_MODEL), layer),
            _const_spec((N_EXPERTS, D_MODEL)), _const_spec((N_EXPERTS, 1)), _const_spec((TM, TM)),
        ],
        out_specs=[row(D_MODEL), row(PACKED), row(PACKED)] + [s for s, _ in picks]
        + [_const_spec((N_EXPERTS, LANES))],
        out_shape=[jax.ShapeDtypeStruct((N_TOK, D_MODEL), F32),
                   jax.ShapeDtypeStruct((N_TOK, PACKED), jnp.int32),
                   jax.ShapeDtypeStruct((N_TOK, PACKED), jnp.int32)] + [s for _, s in picks]
        + [jax.ShapeDtypeStruct((N_EXPERTS, LANES), jnp.int32)],
        scratch_shapes=[pltpu.VMEM((N_EXPERTS, 1), F32)],
        compiler_params=_cparams(1),
        name="stage_e",
    )(xc, xl, modt, g1, g2, *mixed, gates, wf, wm, ws, wo, rwt, rbias, tri)


def _expert_kernel(te_ref, tv_ref, xa_ref, xb_ref, w1_ref, w3_ref, w2_ref, oa_ref, ob_ref,
                   w1b_ref, w3b_ref, w2b_ref):
    j = pl.program_id(0)
    new_expert = jnp.logical_or(j == 0, te_ref[j] != te_ref[jnp.maximum(j - 1, 0)])

    @pl.when(jnp.logical_and(tv_ref[j] == 1, new_expert))
    def _():
        w1b_ref[...] = w1_ref[...].astype(BF16)
        w3b_ref[...] = w3_ref[...].astype(BF16)
        w2b_ref[...] = w2_ref[...].astype(BF16)

    @pl.when(tv_ref[j] == 1)
    def _():
        for r in range(TE // EXPERT_ROWS):
            rows = pl.ds(r * EXPERT_ROWS, EXPERT_ROWS)
            x = _unpack_pair(xa_ref[rows, :], xb_ref[rows, :]).astype(BF16)
            hg = _dot(x, w1b_ref[...])
            hu = _dot(x, w3b_ref[...])
            act = (jax.nn.silu(hg) * hu).astype(BF16)
            oa_ref[rows, :], ob_ref[rows, :] = _pack_pair(_dot(act, w2b_ref[...]))


def _experts(layer, tile_expert, tile_valid, xsa, xsb, w1, w3, w2):
    slot_rows = pl.BlockSpec((TE, PACKED), lambda j, te, tv: (j, 0))
    grid_spec = pltpu.PrefetchScalarGridSpec(
        num_scalar_prefetch=2,
        grid=(NTE,),
        in_specs=[
            slot_rows, slot_rows,
            pl.BlockSpec((None, None, D_MODEL, EXPERT_FF), lambda j, te, tv: (layer, te[j], 0, 0)),
            pl.BlockSpec((None, None, D_MODEL, EXPERT_FF), lambda j, te, tv: (layer, te[j], 0, 0)),
            pl.BlockSpec((None, None, EXPERT_FF, D_MODEL), lambda j, te, tv: (layer, te[j], 0, 0)),
        ],
        out_specs=[slot_rows, slot_rows],
        scratch_shapes=[pltpu.VMEM((D_MODEL, EXPERT_FF), BF16), pltpu.VMEM((D_MODEL, EXPERT_FF), BF16),
                        pltpu.VMEM((EXPERT_FF, D_MODEL), BF16)],
    )
    return pl.pallas_call(
        _expert_kernel,
        grid_spec=grid_spec,
        out_shape=[jax.ShapeDtypeStruct((S_MAX, PACKED), jnp.int32)] * 2,
        compiler_params=_cparams(1),
        name="experts",
    )(tile_expert, tile_valid, xsa, xsb, w1, w3, w2)


def _sc_mesh():
    return plsc.VectorSubcoreMesh(core_axis_name="c", subcore_axis_name="s",
                                  num_cores=SC_CORES, num_subcores=SC_SUBCORES)


def _sc_scatter_rows(rows, slot8):
    @functools.partial(pl.kernel, mesh=_sc_mesh(), scratch_types=[],
                       out_type=jax.ShapeDtypeStruct((S_MAX, PACKED), jnp.int32))
    def scatter(x_hbm, i_hbm, o_hbm):
        def body(x_vmem, i_vmem):
            for k in range(TOP_K):
                pltpu.sync_copy(x_vmem, o_hbm.at[i_vmem.at[k]])

        pltpu.emit_pipeline(
            body,
            grid=(N_TOK // SC_ROWS,),
            in_specs=[pl.BlockSpec((SC_ROWS, PACKED), lambda i: (i, 0)),
                      pl.BlockSpec((8, SC_ROWS), lambda i: (0, i))],
            out_specs=[],
            core_axis_name=("c", "s"),
            dimension_semantics=(pltpu.PARALLEL,),
        )(x_hbm, i_hbm)

    return scatter(rows, slot8)


def _sc_gather_rows(table, idx):
    n = idx.shape[1]

    @functools.partial(pl.kernel, mesh=_sc_mesh(), scratch_types=[],
                       out_type=jax.ShapeDtypeStruct((n, PACKED), jnp.int32))
    def gather(t_hbm, i_hbm, o_hbm):
        def body(i_vmem, o_vmem):
            pltpu.sync_copy(t_hbm.at[i_vmem.at[0]], o_vmem)

        pltpu.emit_pipeline(
            body,
            grid=(n // SC_ROWS,),
            in_specs=[pl.BlockSpec((1, SC_ROWS), lambda i: (0, i))],
            out_specs=[pl.BlockSpec((SC_ROWS, PACKED), lambda i: (i, 0))],
            core_axis_name=("c", "s"),
            dimension_semantics=(pltpu.PARALLEL,),
        )(i_hbm, o_hbm)

    return gather(table, idx)


def _stage_g_kernel(x1_ref, mod_ref, g3_ref, yga_ref, ygb_ref, ew_ref, ha_ref, hb_ref,
                    s1_ref, s3_ref, s2_ref, oc_ref, ol_ref):
    is_ctx = pl.program_id(0) < NB_CTX
    for rows in _chunks():
        h = _unpack_pair(ha_ref[rows, :], hb_ref[rows, :]).astype(BF16)
        act = jax.nn.silu(_dot(h, s1_ref[...])) * _dot(h, s3_ref[...])
        y = _dot(act.astype(BF16), s2_ref[...])
        for k in range(TOP_K):
            y = y + ew_ref[rows, k:k + 1] * _unpack_pair(yga_ref[k, rows, :], ygb_ref[k, rows, :])
        out = x1_ref[rows, :] + mod_ref[:, 5120:6144] * _rms_rows(y, g3_ref[...])

        @pl.when(is_ctx)
        def _():
            oc_ref[rows, :] = out

        @pl.when(jnp.logical_not(is_ctx))
        def _():
            ol_ref[rows, :] = out


def _stage_g(layer, x1, modt, g3, yga, ygb, ew_rows, h2a, h2b, s1, s3, s2):
    row = lambda w: pl.BlockSpec((TB, w), lambda i: (i, 0))
    picked = pl.BlockSpec((TOP_K, TB, PACKED), lambda i: (0, i, 0))
    return pl.pallas_call(
        _stage_g_kernel,
        grid=(NB,),
        in_specs=[row(D_MODEL), pl.BlockSpec((None, 1, N_MOD * D_MODEL), lambda i: (i, 0, 0)),
                  _const_spec((1, D_MODEL)), picked, picked, row(8), row(PACKED), row(PACKED),
                  _layer_spec((D_MODEL, SHARED_FF), layer), _layer_spec((D_MODEL, SHARED_FF), layer),
                  _layer_spec((SHARED_FF, D_MODEL), layer)],
        out_specs=[_ctx_rows(D_MODEL), _lat_rows(D_MODEL)],
        out_shape=[jax.ShapeDtypeStruct((N_CTX, D_MODEL), F32),
                   jax.ShapeDtypeStruct((N_LAT, D_MODEL), F32)],
        compiler_params=_cparams(1),
        name="stage_g",
    )(x1, modt, g3, yga, ygb, ew_rows, h2a, h2b, s1, s3, s2)


def _rope_tables():
    t = np.arange(DEC_SEQ)
    pos = np.stack([(t // GRID_W), (t % GRID_W)], axis=-1).astype(np.float32)

    def table(r):
        n_freq = r // 4
        inv = np.float32(ROPE_BASE) ** (-np.arange(n_freq, dtype=np.float32) / np.float32(n_freq))
        ang = pos[:, :, None] * inv.astype(np.float32)
        cos = np.cos(ang)
        sin = np.sin(ang)
        cos_t = np.stack([cos, cos], axis=2).reshape(DEC_SEQ, r)
        sin_t = np.stack([-sin, sin], axis=2).reshape(DEC_SEQ, r)
        return cos_t, sin_t

    c64, s64 = table(SWA_HEAD_DIM)
    c32, s32 = table(MLA_ROPE)
    lat = np.concatenate([np.tile(c64, (1, 8)), np.tile(s64, (1, 8)),
                          np.tile(c32, (1, 4)), np.tile(s32, (1, 4))], axis=1)
    ident = np.concatenate([np.ones((TB, 512)), np.zeros((TB, 512)),
                            np.ones((TB, 128)), np.zeros((TB, 128))], axis=1)
    return jnp.asarray(np.concatenate([ident, lat], axis=0).astype(np.float32))


def _dft_pair(n):
    k = np.arange(n, dtype=np.int64)
    ang = ((k[:, None] * k[None, :]) % n).astype(np.float64) * (2.0 * math.pi / n)
    return np.cos(ang), np.sin(ang)


def _fnet_tables():
    c64, s64 = _dft_pair(FNET_GROUP_DIM)
    eye = np.eye(FNET_GROUPS)
    bd = np.concatenate([np.kron(eye, c64), np.kron(eye, s64)], axis=1)
    mats = []
    for t_len in (SEQ, DEC_SEQ):
        c, s = _dft_pair(t_len)
        mats.append(np.concatenate([c, -s], axis=1))
    return tuple(jnp.asarray(m.astype(np.float32).astype(BF16)) for m in (bd, mats[0], mats[1]))


def _layer_weights(l, w_in, w_uq, w_ukv):
    w = w_in[l]
    wide = jnp.concatenate([w[:, 0:1024], w[:, 1056:1824], w[:, 1024:1056],
                            jnp.zeros((D_MODEL, 96), F32)], axis=1).astype(BF16)

    uq = w_uq[l].reshape(MLA_Q_RANK, MLA_HEADS, MLA_NOPE + MLA_ROPE)
    z32 = jnp.zeros((MLA_Q_RANK, MLA_HEADS, 32), F32)
    wqa = jnp.concatenate([uq, z32], axis=2).reshape(MLA_Q_RANK, 1024).astype(BF16)
    ukv = w_ukv[l].reshape(MLA_KV_RANK, MLA_HEADS, MLA_NOPE + MLA_V)
    wk = jnp.concatenate([ukv[:, :, :MLA_NOPE], jnp.zeros((MLA_KV_RANK, MLA_HEADS, 64), F32)],
                         axis=2).reshape(MLA_KV_RANK, 1024).astype(BF16)
    wv = ukv[:, :, MLA_NOPE:].reshape(MLA_KV_RANK, 512).astype(BF16)
    return wide, wqa, wk, wv


def _rope_placement():
    e = np.zeros((128, 1024), np.float32)
    for hd in range(MLA_HEADS):
        for i in range(MLA_ROPE):
            e[i, hd * 128 + MLA_NOPE + i] = 1.0
    return jnp.asarray(e, BF16)


def _moe_dispatch_plan(eidx, epos, counts):
    padded = ((counts + TE - 1) // TE) * TE
    ends = jnp.cumsum(padded)
    offs = ends - padded
    ids = jnp.arange(N_EXPERTS, dtype=jnp.int32)
    picked_off = jnp.sum(jnp.where(eidx[:, :, None] == ids, offs, 0), axis=-1)
    slot = picked_off + epos
    starts = jnp.arange(NTE, dtype=jnp.int32) * TE
    tile_expert = jnp.sum((ends[None, :] <= starts[:, None]).astype(jnp.int32), axis=1)
    tile_expert = jnp.minimum(tile_expert, N_EXPERTS - 1)
    tile_valid = (starts < ends[-1]).astype(jnp.int32)
    return slot, tile_expert, tile_valid


def kernel(x_prompt, x_sample, cache_mla_ckv, cache_mla_krope, cache_swa_k, cache_swa_v, c, c_ctx,
           ada_w, ada_b, norm_g, w_in, q_norm, kv_norm, w_fnet, w_uq, w_ukv, w_mla_o, swa_sink,
           w_swa_o, w_gate, b_gate, w_out, router_w, router_bias, exp_w1, exp_w3, exp_w2,
           shared_w1, shared_w3, shared_w2):
    xc = x_prompt.reshape(N_CTX, D_MODEL)
    xl = x_sample.reshape(N_LAT, D_MODEL)

    cond8 = jnp.concatenate([c_ctx[None, :], c, jnp.zeros((3, D_MODEL), F32)], axis=0)
    mod = _modulation(cond8, ada_w, ada_b)
    tile_cond = np.concatenate([np.zeros(NB_CTX, np.int32),
                                1 + np.arange(NB - NB_CTX, dtype=np.int32) // LAT_BLOCKS])

    tab = _rope_tables()
    bd, f_ctx, f_lat = _fnet_tables()
    e_mat = _rope_placement()
    tri = jnp.asarray(np.triu(np.ones((TM, TM), np.float32), 1), BF16)
    w_gate_b, w_fnet_b, w_mla_o_b, w_swa_o_b, w_out_b, sw1_b, sw3_b, sw2_b = (
        w.astype(BF16) for w in (w_gate, w_fnet, w_mla_o, w_swa_o, w_out, shared_w1, shared_w3, shared_w2))

    new_ckv, new_kr, new_k, new_v = [], [], [], []
    for l in range(DEPTH):
        modt = mod[l][tile_cond][:, None, :]
        wide, wqa, wk, wv = _layer_weights(l, w_in, w_uq, w_ukv)
        ng = norm_g[l]

        fin, ckv, kr, sq, sk, sv, gates, q_m, k_m, v_m = _stage_a(
            l, xc, xl, modt, ng[0:1], wide, w_gate_b, b_gate[l][None, :],
            q_norm[l][None, :], kv_norm[l][None, :], tab, wqa, wk, e_mat, wv)

        new_ckv.append(ckv[:N_CTX].reshape(BATCH, SEQ, MLA_KV_RANK))
        new_kr.append(kr[:N_CTX, :MLA_ROPE].reshape(BATCH, SEQ, MLA_ROPE))
        new_k.append(sk[:N_CTX].reshape(BATCH, SEQ, SWA_KV_HEADS, SWA_HEAD_DIM))
        new_v.append(sv[:N_CTX].reshape(BATCH, SEQ, SWA_KV_HEADS, SWA_HEAD_DIM))

        fn = (_fnet(fin, f_ctx, bd, BATCH, SEQ, 0),
              _fnet(fin, f_lat, bd, DEC_BATCH, DEC_SEQ, N_CTX // DEC_SEQ))

        kr_cache = jnp.pad(cache_mla_krope[:, l].reshape(N_CACHE, MLA_ROPE), ((0, 0), (0, 96)))
        k_c, v_c = _mla_cache_kv(cache_mla_ckv[:, l].reshape(N_CACHE, MLA_KV_RANK), kr_cache,
                                 wk, e_mat, wv)
        om = (_mla_attn(q_m, k_m, v_m, k_c, v_c, latent=False),
              _mla_attn(q_m, k_m, v_m, k_c, v_c, latent=True))

        ck = cache_swa_k[:, l].reshape(DEC_BATCH, PAST_LEN, 128)
        cv = cache_swa_v[:, l].reshape(DEC_BATCH, PAST_LEN, 128)
        osw = (_swa_attn(swa_sink[l], sq, sk, sv, ck, cv, latent=False),
               _swa_attn(swa_sink[l], sq, sk, sv, ck, cv, latent=True))

        x1, h2a, h2b, eidx, epos, ew, counts = _stage_e(
            l, xc, xl, modt, ng[1:2], ng[2:3], fn + om + osw, gates,
            w_fnet_b, w_mla_o_b, w_swa_o_b, w_out_b,
            router_w[l].T.astype(BF16), router_bias[l][:, None], tri)
        slot, tile_expert, tile_valid = _moe_dispatch_plan(eidx, epos, counts[:, 0])
        xsa = _sc_scatter_rows(h2a, slot)
        xsb = _sc_scatter_rows(h2b, slot)
        ysa, ysb = _experts(l, tile_expert, tile_valid, xsa, xsb, exp_w1, exp_w3, exp_w2)
        picks = slot[:TOP_K].reshape(1, TOP_K * N_TOK)
        yga = _sc_gather_rows(ysa, picks).reshape(TOP_K, N_TOK, PACKED)
        ygb = _sc_gather_rows(ysb, picks).reshape(TOP_K, N_TOK, PACKED)
        xc, xl = _stage_g(l, x1, modt, ng[3:4], yga, ygb, ew.T, h2a, h2b, sw1_b, sw3_b, sw2_b)

    y_p = xc.reshape(BATCH, SEQ, D_MODEL)
    y_s = xl.reshape(DEC_BATCH, DEC_SEQ, D_MODEL)
    return (y_p, y_s, jnp.stack(new_ckv, axis=1), jnp.stack(new_kr, axis=1),
            jnp.stack(new_k, axis=1), jnp.stack(new_v, axis=1))
```

```python
import functools
import math

import numpy as np
import jax
import jax.numpy as jnp
from jax import lax
from jax.experimental import pallas as pl
from jax.experimental.pallas import tpu as pltpu
from jax.experimental.pallas import tpu_sc as plsc

D_MODEL = 1024
BATCH = 16
SEQ = 256
DEPTH = 2
DEC_BATCH = 4
DEC_SEQ = 2048
PAST_LEN = 512
GRID_W = 64
EPS = 1e-6
ROPE_BASE = 10000.0
NEG_INF = -1e30

FNET_GROUPS = 8
FNET_GROUP_DIM = 64
FNET_WIDTH = 512
MLA_HEADS = 8
MLA_Q_RANK = 384
MLA_KV_RANK = 128
MLA_NOPE = 64
MLA_ROPE = 32
MLA_V = 64
MLA_SCALE = (MLA_NOPE + MLA_ROPE) ** -0.5
LOG2E = math.log2(math.e)
SWA_HEADS = 8
SWA_KV_HEADS = 2
SWA_HEAD_DIM = 64
SWA_WINDOW = 128
SWA_SCALE = SWA_HEAD_DIM ** -0.5
N_MOD = 6
N_EXPERTS = 64
N_EXPERT_GROUPS = 8
TOPK_GROUPS = 4
TOP_K = 6
EXPERT_FF = 256
SHARED_FF = 256
ROUTED_SCALE = 2.5

LANES = 128
TM = 256
N_CTX = BATCH * SEQ
N_LAT = DEC_BATCH * DEC_SEQ
N_TOK = N_CTX + N_LAT
N_CACHE = DEC_BATCH * PAST_LEN
NT_CTX = N_CTX // TM
NT_LAT = N_LAT // TM
NT = N_TOK // TM
LAT_TILES = DEC_SEQ // TM
TB = 512
NB = N_TOK // TB
NB_CTX = N_CTX // TB
LAT_BLOCKS = DEC_SEQ // TB
MLA_LAT_TQ = 256
FNET_ROWS = 1024
TE = 512
S_MAX = N_TOK * TOP_K + N_EXPERTS * TE
NTE = S_MAX // TE
EXPERT_ROWS = 256
VMEM_LIMIT = 56 * 1024 * 1024
PACKED = D_MODEL // 4
SC_ROWS = 128
SC_CORES = 2
SC_SUBCORES = 16

A_F = (0, 512)
A_QD = (512, 896)
A_KV = (896, 1024)
A_SQ = (1024, 1536)
A_SK = (1536, 1664)
A_SV = (1664, 1792)
A_KR = (1792, 1920)
W_IN_WIDE = 1920
TAB_W = 1280

F32 = jnp.float32
BF16 = jnp.bfloat16


def _cparams(n_axes, parallel=False):
    sem = ("parallel" if parallel else "arbitrary",) * n_axes
    return pltpu.CompilerParams(dimension_semantics=sem, vmem_limit_bytes=VMEM_LIMIT)


def _dot(a, b):
    return jnp.dot(a, b, preferred_element_type=F32)


def _dot_nt(a, b):
    return lax.dot_general(a, b, (((1,), (1,)), ((), ())), preferred_element_type=F32)


def _rms_rows(v, g):
    return v * lax.rsqrt(jnp.mean(v * v, axis=-1, keepdims=True) + EPS) * g


def _pack_rows(v):
    half = v.shape[1] // 2
    lo = lax.bitcast_convert_type(v[:, :half].astype(BF16).astype(F32), jnp.int32)
    hi = lax.bitcast_convert_type(v[:, half:].astype(BF16).astype(F32), jnp.int32)
    return jnp.bitwise_or(jnp.bitwise_and(hi, -65536), jnp.bitwise_and(jnp.right_shift(lo, 16), 65535))


def _unpack_rows(w):
    lo = lax.bitcast_convert_type(jnp.left_shift(w, 16), F32)
    hi = lax.bitcast_convert_type(jnp.bitwise_and(w, -65536), F32)
    return jnp.concatenate([lo, hi], axis=1)


def _pack_pair(v):
    half = v.shape[1] // 2
    return _pack_rows(v[:, :half]), _pack_rows(v[:, half:])


def _unpack_pair(a, b):
    return jnp.concatenate([_unpack_rows(a), _unpack_rows(b)], axis=1)


def _const_spec(shape):
    return pl.BlockSpec(shape, lambda *_: (0,) * len(shape))


def _layer_spec(shape, layer):
    return pl.BlockSpec((None,) + shape, lambda *_: (layer,) + (0,) * len(shape))


def _ctx_rows(width):
    return pl.BlockSpec((TB, width), lambda i: (jnp.minimum(i, NB_CTX - 1), 0))


def _lat_rows(width):
    return pl.BlockSpec((TB, width), lambda i: (jnp.maximum(i - NB_CTX, 0), 0))


def _tab_row_block(i):
    return jnp.where(i < NB_CTX, 0, 1 + (i - NB_CTX) % LAT_BLOCKS)


def _mod_kernel(cond_ref, w_ref, b_ref, o_ref):
    c = cond_ref[...]
    a = (c * jax.nn.sigmoid(c)).astype(BF16)
    o_ref[...] = _dot(a, w_ref[...].astype(BF16)) + b_ref[...]


def _modulation(cond8, ada_w, ada_b):
    tn = 512
    nj = N_MOD * D_MODEL // tn
    return pl.pallas_call(
        _mod_kernel,
        grid=(DEPTH, nj),
        in_specs=[
            pl.BlockSpec((8, D_MODEL), lambda l, j: (0, 0)),
            pl.BlockSpec((None, D_MODEL, tn), lambda l, j: (l, 0, j)),
            pl.BlockSpec((None, 1, tn), lambda l, j: (l, 0, j)),
        ],
        out_specs=pl.BlockSpec((None, 8, tn), lambda l, j: (l, 0, j)),
        out_shape=jax.ShapeDtypeStruct((DEPTH, 8, N_MOD * D_MODEL), F32),
        compiler_params=_cparams(2),
        name="modulation",
    )(cond8, ada_w, ada_b.reshape(DEPTH, 1, N_MOD * D_MODEL))


def _half_swap(x, half):
    n = x.shape[1]
    lane = lax.broadcasted_iota(jnp.int32, (1, n), 1)
    return jnp.where((lane & half) == 0, pltpu.roll(x, n - half, 1), pltpu.roll(x, half, 1))


def _mla_expand(rows, cq, ckv, kr, cos32, sin32, wqa_ref, wk_ref, e_ref, wv_ref,
                q_ref, k_ref, v_ref):
    if q_ref is not None:
        lane = lax.broadcasted_iota(jnp.int32, (1, LANES), 1)
        rope_lane = jnp.logical_and(lane >= MLA_NOPE, lane < MLA_NOPE + MLA_ROPE)
        cos_h = jnp.where(rope_lane, cos32, 1.0)
        sin_h = jnp.where(rope_lane, sin32, 0.0)
        for hd in range(MLA_HEADS):
            lo, hi = hd * LANES, (hd + 1) * LANES
            q = _dot(cq, wqa_ref[:, lo:hi])
            q = q * cos_h + _half_swap(q, MLA_ROPE // 4) * sin_h
            q_ref[rows, lo:hi] = (q * (MLA_SCALE * LOG2E)).astype(BF16)
    k_ref[rows, :] = (_dot(ckv, wk_ref[...]) + _dot(kr, e_ref[...])).astype(BF16)
    v_ref[rows, :] = _dot(ckv, wv_ref[...]).astype(BF16)


def _chunks():
    return [pl.ds(r * TM, TM) for r in range(TB // TM)]


def _stage_a_kernel(xc_ref, xl_ref, mod_ref, g_ref, win_ref, wg_ref, bg_ref, qn_ref, kvn_ref, tab_ref,
                    wqa_ref, wk_ref, e_ref, wv_ref,
                    fin_ref, ckv_ref, kr_ref, sq_ref, sk_ref, sv_ref, gates_ref,
                    qm_ref, km_ref, vm_ref):
    is_ctx = pl.program_id(0) < NB_CTX
    for rows in _chunks():
        x = jnp.where(is_ctx, xc_ref[rows, :], xl_ref[rows, :])
        h = (_rms_rows(x, g_ref[...]) * (1.0 + mod_ref[:, 1024:2048]) + mod_ref[:, 0:1024]).astype(BF16)

        def proj(seg):
            return _dot(h, win_ref[:, seg[0]:seg[1]])

        fin_ref[rows, :] = proj(A_F).astype(BF16)
        cq = _rms_rows(proj(A_QD), qn_ref[...]).astype(BF16)
        ckv = _rms_rows(proj(A_KV), kvn_ref[...])
        ckv_ref[rows, :] = ckv
        cos64 = tab_ref[rows, 0:512]
        sin64 = tab_ref[rows, 512:1024]
        sq = proj(A_SQ)
        sq = sq * cos64 + _half_swap(sq, SWA_HEAD_DIM // 4) * sin64
        sq_ref[rows, :] = (sq * (SWA_SCALE * LOG2E)).astype(BF16)
        sk = proj(A_SK)
        sk_ref[rows, :] = sk * cos64[:, 0:128] + _half_swap(sk, SWA_HEAD_DIM // 4) * sin64[:, 0:128]
        sv_ref[rows, :] = proj(A_SV)
        cos32 = tab_ref[rows, 1024:1152]
        sin32 = tab_ref[rows, 1152:1280]
        kr = proj(A_KR)
        kr = kr * cos32 + _half_swap(kr, MLA_ROPE // 4) * sin32
        kr_ref[rows, :] = kr
        _mla_expand(rows, cq, ckv.astype(BF16), kr.astype(BF16), cos32, sin32,
                    wqa_ref, wk_ref, e_ref, wv_ref, qm_ref, km_ref, vm_ref)
        for c in range(3):
            lo, hi = c * D_MODEL, (c + 1) * D_MODEL
            gates_ref[rows, lo:hi] = jax.nn.sigmoid(_dot(h, wg_ref[:, lo:hi]) + bg_ref[:, lo:hi]).astype(BF16)


def _stage_a(layer, xc, xl, modt, g0, w_in_wide, w_gate, b_gate, q_norm, kv_norm, tab, wqa, wk, e_mat, wv):
    row = lambda w: pl.BlockSpec((TB, w), lambda i: (i, 0))
    outs = [(512, BF16), (128, F32), (128, F32), (512, BF16), (128, F32), (128, F32),
            (3 * D_MODEL, BF16), (1024, BF16), (1024, BF16), (512, BF16)]
    return pl.pallas_call(
        _stage_a_kernel,
        grid=(NB,),
        in_specs=[
            _ctx_rows(D_MODEL), _lat_rows(D_MODEL),
            pl.BlockSpec((None, 1, N_MOD * D_MODEL), lambda i: (i, 0, 0)),
            _const_spec((1, D_MODEL)),
            _const_spec((D_MODEL, W_IN_WIDE)),
            _layer_spec((D_MODEL, 3 * D_MODEL), layer),
            _const_spec((1, 3 * D_MODEL)),
            _const_spec((1, MLA_Q_RANK)),
            _const_spec((1, MLA_KV_RANK)),
            pl.BlockSpec((TB, TAB_W), lambda i: (_tab_row_block(i), 0)),
            _const_spec((MLA_Q_RANK, 1024)),
            _const_spec((128, 1024)), _const_spec((128, 1024)), _const_spec((128, 512)),
        ],
        out_specs=[row(w) for w, _ in outs],
        out_shape=[jax.ShapeDtypeStruct((N_TOK, w), dt) for w, dt in outs],
        compiler_params=_cparams(1),
        name="stage_a",
    )(xc, xl, modt, g0, w_in_wide, w_gate, b_gate, q_norm, kv_norm, tab, wqa, wk, e_mat, wv)


def _fnet_kernel(t_len, scale, fin_ref, f_ref, bd_ref, o_ref, zz_ref):
    @pl.when(pl.program_id(1) == 0)
    def _():
        z = fin_ref[...]
        zz_ref[0:t_len, :] = _dot(z, bd_ref[:, 0:512]).astype(BF16)
        zz_ref[t_len:2 * t_len, :] = _dot(z, bd_ref[:, 512:1024]).astype(BF16)

    o_ref[...] = (_dot(f_ref[...], zz_ref[...]) * scale).astype(BF16)


def _fnet(fin, fmat, bd, n_batch, t_len, row_block0):
    scale = 1.0 / math.sqrt(t_len * FNET_GROUP_DIM)
    ft = min(t_len, FNET_ROWS)
    return pl.pallas_call(
        functools.partial(_fnet_kernel, t_len, scale),
        grid=(n_batch, t_len // ft),
        in_specs=[
            pl.BlockSpec((t_len, FNET_WIDTH), lambda b, i: (row_block0 + b, 0)),
            pl.BlockSpec((ft, 2 * t_len), lambda b, i: (i, 0)),
            _const_spec((FNET_WIDTH, 2 * FNET_WIDTH)),
        ],
        out_specs=pl.BlockSpec((ft, FNET_WIDTH), lambda b, i: (b * (t_len // ft) + i, 0)),
        out_shape=jax.ShapeDtypeStruct((n_batch * t_len, FNET_WIDTH), BF16),
        scratch_shapes=[pltpu.VMEM((2 * t_len, FNET_WIDTH), BF16)],
        compiler_params=_cparams(2),
        name=f"fnet_{t_len}",
    )(fin, fmat, bd)


def _mla_cache_kernel(ckv_ref, kr_ref, wk_ref, e_ref, wv_ref, k_ref, v_ref):
    _mla_expand(slice(None), None, ckv_ref[...].astype(BF16), kr_ref[...].astype(BF16), None, None,
                None, wk_ref, e_ref, wv_ref, None, k_ref, v_ref)


def _mla_cache_kv(ckv_cache, kr_cache, wk, e_mat, wv):
    row = lambda w: pl.BlockSpec((TM, w), lambda i: (i, 0))
    return pl.pallas_call(
        _mla_cache_kernel,
        grid=(N_CACHE // TM,),
        in_specs=[row(128), row(128),
                  _const_spec((128, 1024)), _const_spec((128, 1024)), _const_spec((128, 512))],
        out_specs=[row(1024), row(512)],
        out_shape=[jax.ShapeDtypeStruct((N_CACHE, 1024), BF16),
                   jax.ShapeDtypeStruct((N_CACHE, 512), BF16)],
        compiler_params=_cparams(1),
        name="mla_cache_kv",
    )(ckv_cache, kr_cache, wk, e_mat, wv)


def _mla_attn_kernel(n_seg, pairs, q_ref, *refs):
    k_refs = refs[0:n_seg]
    v_refs = refs[n_seg:2 * n_seg]
    o_ref = refs[2 * n_seg]
    lane = lax.broadcasted_iota(jnp.int32, (1, LANES), 1)
    low = lane < MLA_V
    for pr in range(pairs):
        outs = []
        for hh in range(2):
            hd = 2 * pr + hh
            q = q_ref[:, hd * LANES:(hd + 1) * LANES]
            ss = [_dot_nt(q, k[:, hd * LANES:(hd + 1) * LANES]) for k in k_refs]
            m = functools.reduce(jnp.maximum, [s.max(axis=-1, keepdims=True) for s in ss])
            keep = low if hh == 0 else jnp.logical_not(low)
            sum_lane = MLA_V if hh == 0 else 0
            po = None
            for s, v_ref in zip(ss, v_refs):
                v = v_ref[:, pr * LANES:(pr + 1) * LANES]
                vm = jnp.where(lane == sum_lane, jnp.ones_like(v), jnp.where(keep, v, jnp.zeros_like(v)))
                t = _dot(jnp.exp2(s - m).astype(BF16), vm)
                po = t if po is None else po + t
            outs.append(po / po[:, sum_lane:sum_lane + 1])
        o_ref[:, pr * LANES:(pr + 1) * LANES] = jnp.where(low, outs[0], outs[1]).astype(BF16)


def _mla_attn(q_all, k_all, v_all, k_cache, v_cache, latent):
    if latent:
        tq, pairs = MLA_LAT_TQ, 1
        n_b, n_q = DEC_BATCH, DEC_SEQ // tq
        q0 = N_CTX // tq
        kv_specs = [
            pl.BlockSpec((PAST_LEN, 256), lambda b, hp, i: (b, hp)),
            pl.BlockSpec((DEC_SEQ, 256), lambda b, hp, i: (N_CTX // DEC_SEQ + b, hp)),
            pl.BlockSpec((PAST_LEN, 128), lambda b, hp, i: (b, hp)),
            pl.BlockSpec((DEC_SEQ, 128), lambda b, hp, i: (N_CTX // DEC_SEQ + b, hp)),
        ]
        args = (q_all, k_cache, k_all, v_cache, v_all)
        n_seg = 2
    else:
        tq, pairs = SEQ, MLA_HEADS // 2
        n_b, n_q = BATCH, 1
        q0 = 0
        kv_specs = [
            pl.BlockSpec((SEQ, 256 * pairs), lambda b, hp, i: (b, hp)),
            pl.BlockSpec((SEQ, 128 * pairs), lambda b, hp, i: (b, hp)),
        ]
        args = (q_all, k_all, v_all)
        n_seg = 1
    return pl.pallas_call(
        functools.partial(_mla_attn_kernel, n_seg, pairs),
        grid=(n_b, MLA_HEADS // (2 * pairs), n_q),
        in_specs=[pl.BlockSpec((tq, 256 * pairs), lambda b, hp, i: (q0 + b * n_q + i, hp))] + kv_specs,
        out_specs=pl.BlockSpec((tq, 128 * pairs), lambda b, hp, i: (b * n_q + i, hp)),
        out_shape=jax.ShapeDtypeStruct((n_b * n_q * tq, MLA_HEADS * MLA_V), BF16),
        compiler_params=_cparams(3),
        name="mla_attn_lat" if latent else "mla_attn_ctx",
    )(*args)


def _swa_kernel(windowed, n_qb, sink_ref, q_ref, *refs):
    n_seg = 4 if windowed else 1
    k_refs = refs[0:n_seg]
    v_refs = refs[n_seg:2 * n_seg]
    o_ref = refs[2 * n_seg]
    tq = q_ref.shape[0]
    qb = pl.program_id(1)
    lane = lax.broadcasted_iota(jnp.int32, (1, LANES), 1)
    low = lane < SWA_HEAD_DIM
    high = jnp.logical_not(low)

    k_all = jnp.concatenate([r[...] for r in k_refs], axis=0)
    v_all = jnp.concatenate([r[...] for r in v_refs], axis=0)
    k_sw = pltpu.roll(k_all, SWA_HEAD_DIM, 1)
    v_sw = pltpu.roll(v_all, SWA_HEAD_DIM, 1)

    if windowed:
        qi = lax.broadcasted_iota(jnp.int32, (2 * tq, SWA_WINDOW), 0) % tq
        kj = lax.broadcasted_iota(jnp.int32, (2 * tq, SWA_WINDOW), 1)
        bias_prev = jnp.where(jnp.logical_and(kj >= qi, qb > 0), 0.0, NEG_INF)
        bias_next = jnp.where(jnp.logical_and(kj <= qi, qb < n_qb - 1), 0.0, NEG_INF)
    top_rows = lax.broadcasted_iota(jnp.int32, (2 * tq, 1), 0) < tq

    for g in range(SWA_KV_HEADS):
        qs = jnp.concatenate([q_ref[:, 256 * g:256 * g + 128],
                              q_ref[:, 256 * g + 128:256 * g + 256]], axis=0)
        halves = []
        for half in range(2):
            keep = low if half == 0 else high
            sum_lane = SWA_HEAD_DIM if half == 0 else 0
            straight = (g == half)
            kh = jnp.where(keep, k_all if straight else k_sw, 0.0).astype(BF16)
            vh = jnp.where(lane == sum_lane, 1.0,
                           jnp.where(keep, v_all if straight else v_sw, 0.0)).astype(BF16)
            s = _dot_nt(qs, kh)
            if windowed:
                c0, c1, c2 = PAST_LEN, PAST_LEN + SWA_WINDOW, PAST_LEN + 2 * SWA_WINDOW
                s = jnp.concatenate([s[:, :c0], s[:, c0:c1] + bias_prev, s[:, c1:c2],
                                     s[:, c2:] + bias_next], axis=1)
            sink = jnp.where(top_rows, sink_ref[4 * g + half], sink_ref[4 * g + 2 + half]) * LOG2E
            m = jnp.maximum(s.max(axis=-1, keepdims=True), sink)
            po = _dot(jnp.exp2(s - m).astype(BF16), vh)
            halves.append(po / (po[:, sum_lane:sum_lane + 1] + jnp.exp2(sink - m)))
        out = jnp.where(low, halves[0], halves[1])
        o_ref[:, 256 * g:256 * g + 128] = out[0:tq].astype(BF16)
        o_ref[:, 256 * g + 128:256 * g + 256] = out[tq:2 * tq].astype(BF16)


def _swa_attn(sink, sq, sk, sv, cache_k, cache_v, latent):
    smem = pl.BlockSpec(memory_space=pltpu.SMEM)
    if latent:
        tq = SWA_WINDOW
        n_b, n_qb = DEC_BATCH, DEC_SEQ // tq
        base = N_CTX // tq

        def prev(b, i):
            return (base + b * n_qb + jnp.maximum(i - 1, 0), 0)

        def cur(b, i):
            return (base + b * n_qb + i, 0)

        def nxt(b, i):
            return (base + b * n_qb + jnp.minimum(i + 1, n_qb - 1), 0)

        cache = pl.BlockSpec((None, PAST_LEN, 128), lambda b, i: (b, 0, 0))
        blk = lambda f: pl.BlockSpec((tq, 128), f)
        kv_specs = [cache, blk(prev), blk(cur), blk(nxt)] * 2
        args = (cache_k, sk, sk, sk, cache_v, sv, sv, sv)
        q_spec = pl.BlockSpec((tq, 512), cur)
        o_spec = pl.BlockSpec((tq, 512), lambda b, i: (b * n_qb + i, 0))
    else:
        tq = SEQ
        n_b, n_qb = BATCH, 1
        blk = pl.BlockSpec((tq, 128), lambda b, i: (b, 0))
        kv_specs = [blk, blk]
        args = (sk, sv)
        q_spec = pl.BlockSpec((tq, 512), lambda b, i: (b, 0))
        o_spec = q_spec
    return pl.pallas_call(
        functools.partial(_swa_kernel, latent, n_qb),
        grid=(n_b, n_qb),
        in_specs=[smem, q_spec] + kv_specs,
        out_specs=o_spec,
        out_shape=jax.ShapeDtypeStruct((n_b * n_qb * tq, 512), BF16),
        compiler_params=_cparams(2),
        name="swa_lat" if latent else "swa_ctx",
    )(sink, sq, *args)


def _route(h, rwt_ref, rb_ref, tri_ref, carry):
    gsz = N_EXPERTS // N_EXPERT_GROUPS
    scores = jax.nn.sigmoid(_dot_nt(rwt_ref[...], h))
    biased = scores + rb_ref[...]
    mem = lax.broadcasted_iota(jnp.int32, (gsz, TM), 0).astype(F32)
    gs_rows = []
    for g in range(N_EXPERT_GROUPS):
        bg = biased[g * gsz:(g + 1) * gsz, :]
        m1 = bg.max(axis=0, keepdims=True)
        first = jnp.min(jnp.where(bg == m1, mem, float(gsz)), axis=0, keepdims=True)
        m2 = jnp.where(mem == first, -jnp.inf, bg).max(axis=0, keepdims=True)
        gs_rows.append(m1 + m2)
    gs = jnp.concatenate(gs_rows, axis=0)
    gid = lax.broadcasted_iota(jnp.int32, gs.shape, 0).astype(F32)
    gsel = jnp.zeros(gs.shape, F32)
    for _ in range(TOPK_GROUPS):
        mx = gs.max(axis=0, keepdims=True)
        pick = gid == jnp.min(jnp.where(gs == mx, gid, float(N_EXPERT_GROUPS)), axis=0, keepdims=True)
        gsel = jnp.where(pick, 1.0, gsel)
        gs = jnp.where(pick, -jnp.inf, gs)
    emask = jnp.concatenate(
        [jnp.broadcast_to(gsel[g:g + 1, :], (gsz, TM)) for g in range(N_EXPERT_GROUPS)], axis=0)
    cand = jnp.where(emask > 0.5, biased, NEG_INF)
    eid = lax.broadcasted_iota(jnp.int32, cand.shape, 0).astype(F32)
    picks = []
    self32 = jnp.zeros(cand.shape, F32)
    for _ in range(TOP_K):
        mx = cand.max(axis=0, keepdims=True)
        pick = eid == jnp.min(jnp.where(cand == mx, eid, float(N_EXPERTS)), axis=0, keepdims=True)
        picks.append(pick)
        self32 = jnp.where(pick, 1.0, self32)
        cand = jnp.where(pick, -jnp.inf, cand)
    pos = _dot(self32.astype(BF16), tri_ref[...]) + carry
    sel_scores = [jnp.sum(jnp.where(p, scores, 0.0), axis=0, keepdims=True) for p in picks]
    wsum = functools.reduce(lambda a, b: a + b, sel_scores)
    zero_f = jnp.zeros((2, TM), F32)
    eidx = [jnp.sum(jnp.where(p, eid, 0.0), axis=0, keepdims=True) for p in picks]
    epos = [jnp.sum(jnp.where(p, pos, 0.0), axis=0, keepdims=True) for p in picks]
    ew = [s / wsum * ROUTED_SCALE for s in sel_scores]
    return (jnp.concatenate(eidx + [zero_f], axis=0).astype(jnp.int32),
            jnp.concatenate(epos + [zero_f], axis=0).astype(jnp.int32),
            jnp.concatenate(ew + [zero_f], axis=0),
            carry + jnp.sum(self32, axis=1, keepdims=True))


def _stage_e_kernel(xc_ref, xl_ref, mod_ref, g1_ref, g2_ref, fnc_ref, fnl_ref, omc_ref, oml_ref, osc_ref, osl_ref,
                    gates_ref, wf_ref, wm_ref, ws_ref, wo_ref, rwt_ref, rb_ref, tri_ref,
                    x1_ref, h2a_ref, h2b_ref, eidx_ref, epos_ref, ew_ref, cnt_ref, carry_ref):
    @pl.when(pl.program_id(0) == 0)
    def _():
        carry_ref[...] = jnp.zeros_like(carry_ref)

    is_ctx = pl.program_id(0) < NB_CTX
    carry = carry_ref[...]
    for r, rows in enumerate(_chunks()):
        fn = jnp.where(is_ctx, fnc_ref[rows, :], fnl_ref[rows, :])
        om = jnp.where(is_ctx, omc_ref[rows, :], oml_ref[rows, :])
        osw = jnp.where(is_ctx, osc_ref[rows, :], osl_ref[rows, :])
        merged = (gates_ref[rows, 0:1024].astype(F32) * _dot(fn, wf_ref[...])
                  + gates_ref[rows, 1024:2048].astype(F32) * _dot(om, wm_ref[...])
                  + gates_ref[rows, 2048:3072].astype(F32) * _dot(osw, ws_ref[...]))
        mix = _dot(merged.astype(BF16), wo_ref[...])
        x = jnp.where(is_ctx, xc_ref[rows, :], xl_ref[rows, :])
        x1 = x + mod_ref[:, 2048:3072] * _rms_rows(mix, g1_ref[...])
        x1_ref[rows, :] = x1
        h2 = _rms_rows(x1, g2_ref[...]) * (1.0 + mod_ref[:, 4096:5120]) + mod_ref[:, 3072:4096]
        h2a_ref[rows, :], h2b_ref[rows, :] = _pack_pair(h2)
        cols = pl.ds(r * TM, TM)
        eidx_ref[:, cols], epos_ref[:, cols], ew_ref[:, cols], carry = _route(
            h2.astype(BF16), rwt_ref, rb_ref, tri_ref, carry)
    carry_ref[...] = carry
    cnt_ref[...] = jnp.broadcast_to(carry, cnt_ref.shape).astype(jnp.int32)


def _stage_e(layer, xc, xl, modt, g1, g2, mixed, gates, wf, wm, ws, wo, rwt, rbias, tri):
    row = lambda w: pl.BlockSpec((TB, w), lambda i: (i, 0))
    ctx, lat = _ctx_rows(512), _lat_rows(512)
    col = lambda dt: (pl.BlockSpec((8, TB), lambda i: (0, i)), jax.ShapeDtypeStruct((8, N_TOK), dt))
    picks = [col(jnp.int32), col(jnp.int32), col(F32)]
    return pl.pallas_call(
        _stage_e_kernel,
        grid=(NB,),
        in_specs=[
            _ctx_rows(D_MODEL), _lat_rows(D_MODEL),
            pl.BlockSpec((None, 1, N_MOD * D_MODEL), lambda i: (i, 0, 0)),
            _const_spec((1, D_MODEL)), _const_spec((1, D_MODEL)),
            ctx, lat, ctx, lat, ctx, lat, row(3 * D_MODEL),
            _layer_spec((512, D_MODEL), layer), _layer_spec((512, D_MODEL), layer),
            _layer_spec((512, D_MODEL), layer), _layer_spec((D_MODEL, D_MODEL), layer),
            _const_spec((N_EXPERTS, D_MODEL)), _const_spec((N_EXPERTS, 1)), _const_spec((TM, TM)),
        ],
        out_specs=[row(D_MODEL), row(PACKED), row(PACKED)] + [s for s, _ in picks]
        + [_const_spec((N_EXPERTS, LANES))],
        out_shape=[jax.ShapeDtypeStruct((N_TOK, D_MODEL), F32),
                   jax.ShapeDtypeStruct((N_TOK, PACKED), jnp.int32),
                   jax.ShapeDtypeStruct((N_TOK, PACKED), jnp.int32)] + [s for _, s in picks]
        + [jax.ShapeDtypeStruct((N_EXPERTS, LANES), jnp.int32)],
        scratch_shapes=[pltpu.VMEM((N_EXPERTS, 1), F32)],
        compiler_params=_cparams(1),
        name="stage_e",
    )(xc, xl, modt, g1, g2, *mixed, gates, wf, wm, ws, wo, rwt, rbias, tri)


def _expert_kernel(te_ref, tv_ref, xa_ref, xb_ref, w1_ref, w3_ref, w2_ref, oa_ref, ob_ref,
                   w1b_ref, w3b_ref, w2b_ref):
    j = pl.program_id(0)
    new_expert = jnp.logical_or(j == 0, te_ref[j] != te_ref[jnp.maximum(j - 1, 0)])

    @pl.when(jnp.logical_and(tv_ref[j] > 0, new_expert))
    def _():
        w1b_ref[...] = w1_ref[...].astype(BF16)
        w3b_ref[...] = w3_ref[...].astype(BF16)
        w2b_ref[...] = w2_ref[...].astype(BF16)

    def run(n_chunks):
        for r in range(n_chunks):
            rows = pl.ds(r * EXPERT_ROWS, EXPERT_ROWS)
            x = _unpack_pair(xa_ref[rows, :], xb_ref[rows, :]).astype(BF16)
            hg = _dot(x, w1b_ref[...])
            hu = _dot(x, w3b_ref[...])
            act = (jax.nn.silu(hg) * hu).astype(BF16)
            oa_ref[rows, :], ob_ref[rows, :] = _pack_pair(_dot(act, w2b_ref[...]))

    for n_chunks in range(1, TE // EXPERT_ROWS + 1):
        pl.when(tv_ref[j] == n_chunks)(functools.partial(run, n_chunks))


def _experts(layer, tile_expert, tile_valid, xsa, xsb, w1, w3, w2):
    slot_rows = pl.BlockSpec((TE, PACKED), lambda j, te, tv: (j, 0))
    grid_spec = pltpu.PrefetchScalarGridSpec(
        num_scalar_prefetch=2,
        grid=(NTE,),
        in_specs=[
            slot_rows, slot_rows,
            pl.BlockSpec((None, None, D_MODEL, EXPERT_FF), lambda j, te, tv: (layer, te[j], 0, 0)),
            pl.BlockSpec((None, None, D_MODEL, EXPERT_FF), lambda j, te, tv: (layer, te[j], 0, 0)),
            pl.BlockSpec((None, None, EXPERT_FF, D_MODEL), lambda j, te, tv: (layer, te[j], 0, 0)),
        ],
        out_specs=[slot_rows, slot_rows],
        scratch_shapes=[pltpu.VMEM((D_MODEL, EXPERT_FF), BF16), pltpu.VMEM((D_MODEL, EXPERT_FF), BF16),
                        pltpu.VMEM((EXPERT_FF, D_MODEL), BF16)],
    )
    return pl.pallas_call(
        _expert_kernel,
        grid_spec=grid_spec,
        out_shape=[jax.ShapeDtypeStruct((S_MAX, PACKED), jnp.int32)] * 2,
        compiler_params=_cparams(1),
        name="experts",
    )(tile_expert, tile_valid, xsa, xsb, w1, w3, w2)


def _sc_mesh():
    return plsc.VectorSubcoreMesh(core_axis_name="c", subcore_axis_name="s",
                                  num_cores=SC_CORES, num_subcores=SC_SUBCORES)


def _sc_scatter_rows(rows, slot8):
    @functools.partial(pl.kernel, mesh=_sc_mesh(), scratch_types=[],
                       out_type=jax.ShapeDtypeStruct((S_MAX, PACKED), jnp.int32))
    def scatter(x_hbm, i_hbm, o_hbm):
        def body(x_vmem, i_vmem):
            for k in range(TOP_K):
                pltpu.sync_copy(x_vmem, o_hbm.at[i_vmem.at[k]])

        pltpu.emit_pipeline(
            body,
            grid=(N_TOK // SC_ROWS,),
            in_specs=[pl.BlockSpec((SC_ROWS, PACKED), lambda i: (i, 0)),
                      pl.BlockSpec((8, SC_ROWS), lambda i: (0, i))],
            out_specs=[],
            core_axis_name=("c", "s"),
            dimension_semantics=(pltpu.PARALLEL,),
        )(x_hbm, i_hbm)

    return scatter(rows, slot8)


def _sc_gather_rows(table, idx):
    n = idx.shape[1]

    @functools.partial(pl.kernel, mesh=_sc_mesh(), scratch_types=[],
                       out_type=jax.ShapeDtypeStruct((n, PACKED), jnp.int32))
    def gather(t_hbm, i_hbm, o_hbm):
        def body(i_vmem, o_vmem):
            pltpu.sync_copy(t_hbm.at[i_vmem.at[0]], o_vmem)

        pltpu.emit_pipeline(
            body,
            grid=(n // SC_ROWS,),
            in_specs=[pl.BlockSpec((1, SC_ROWS), lambda i: (0, i))],
            out_specs=[pl.BlockSpec((SC_ROWS, PACKED), lambda i: (i, 0))],
            core_axis_name=("c", "s"),
            dimension_semantics=(pltpu.PARALLEL,),
        )(i_hbm, o_hbm)

    return gather(table, idx)


def _stage_g_kernel(x1_ref, mod_ref, g3_ref, yga_ref, ygb_ref, ew_ref, ha_ref, hb_ref,
                    s1_ref, s3_ref, s2_ref, oc_ref, ol_ref):
    is_ctx = pl.program_id(0) < NB_CTX
    for rows in _chunks():
        h = _unpack_pair(ha_ref[rows, :], hb_ref[rows, :]).astype(BF16)
        act = jax.nn.silu(_dot(h, s1_ref[...])) * _dot(h, s3_ref[...])
        y = _dot(act.astype(BF16), s2_ref[...])
        for k in range(TOP_K):
            y = y + ew_ref[rows, k:k + 1] * _unpack_pair(yga_ref[k, rows, :], ygb_ref[k, rows, :])
        out = x1_ref[rows, :] + mod_ref[:, 5120:6144] * _rms_rows(y, g3_ref[...])

        @pl.when(is_ctx)
        def _():
            oc_ref[rows, :] = out

        @pl.when(jnp.logical_not(is_ctx))
        def _():
            ol_ref[rows, :] = out


def _stage_g(layer, x1, modt, g3, yga, ygb, ew_rows, h2a, h2b, s1, s3, s2):
    row = lambda w: pl.BlockSpec((TB, w), lambda i: (i, 0))
    picked = pl.BlockSpec((TOP_K, TB, PACKED), lambda i: (0, i, 0))
    return pl.pallas_call(
        _stage_g_kernel,
        grid=(NB,),
        in_specs=[row(D_MODEL), pl.BlockSpec((None, 1, N_MOD * D_MODEL), lambda i: (i, 0, 0)),
                  _const_spec((1, D_MODEL)), picked, picked, row(8), row(PACKED), row(PACKED),
                  _layer_spec((D_MODEL, SHARED_FF), layer), _layer_spec((D_MODEL, SHARED_FF), layer),
                  _layer_spec((SHARED_FF, D_MODEL), layer)],
        out_specs=[_ctx_rows(D_MODEL), _lat_rows(D_MODEL)],
        out_shape=[jax.ShapeDtypeStruct((N_CTX, D_MODEL), F32),
                   jax.ShapeDtypeStruct((N_LAT, D_MODEL), F32)],
        compiler_params=_cparams(1),
        name="stage_g",
    )(x1, modt, g3, yga, ygb, ew_rows, h2a, h2b, s1, s3, s2)


def _rope_tables():
    t = np.arange(DEC_SEQ)
    pos = np.stack([(t // GRID_W), (t % GRID_W)], axis=-1).astype(np.float32)

    def table(r):
        n_freq = r // 4
        inv = np.float32(ROPE_BASE) ** (-np.arange(n_freq, dtype=np.float32) / np.float32(n_freq))
        ang = pos[:, :, None] * inv.astype(np.float32)
        cos = np.cos(ang)
        sin = np.sin(ang)
        cos_t = np.stack([cos, cos], axis=2).reshape(DEC_SEQ, r)
        sin_t = np.stack([-sin, sin], axis=2).reshape(DEC_SEQ, r)
        return cos_t, sin_t

    c64, s64 = table(SWA_HEAD_DIM)
    c32, s32 = table(MLA_ROPE)
    lat = np.concatenate([np.tile(c64, (1, 8)), np.tile(s64, (1, 8)),
                          np.tile(c32, (1, 4)), np.tile(s32, (1, 4))], axis=1)
    ident = np.concatenate([np.ones((TB, 512)), np.zeros((TB, 512)),
                            np.ones((TB, 128)), np.zeros((TB, 128))], axis=1)
    return jnp.asarray(np.concatenate([ident, lat], axis=0).astype(np.float32))


def _dft_pair(n):
    k = np.arange(n, dtype=np.int64)
    ang = ((k[:, None] * k[None, :]) % n).astype(np.float64) * (2.0 * math.pi / n)
    return np.cos(ang), np.sin(ang)


def _fnet_tables():
    c64, s64 = _dft_pair(FNET_GROUP_DIM)
    eye = np.eye(FNET_GROUPS)
    bd = np.concatenate([np.kron(eye, c64), np.kron(eye, s64)], axis=1)
    mats = []
    for t_len in (SEQ, DEC_SEQ):
        c, s = _dft_pair(t_len)
        mats.append(np.concatenate([c, -s], axis=1))
    return tuple(jnp.asarray(m.astype(np.float32).astype(BF16)) for m in (bd, mats[0], mats[1]))


def _layer_weights(l, w_in, w_uq, w_ukv):
    w = w_in[l]
    wide = jnp.concatenate([w[:, 0:1024], w[:, 1056:1824], w[:, 1024:1056],
                            jnp.zeros((D_MODEL, 96), F32)], axis=1).astype(BF16)

    uq = w_uq[l].reshape(MLA_Q_RANK, MLA_HEADS, MLA_NOPE + MLA_ROPE)
    z32 = jnp.zeros((MLA_Q_RANK, MLA_HEADS, 32), F32)
    wqa = jnp.concatenate([uq, z32], axis=2).reshape(MLA_Q_RANK, 1024).astype(BF16)
    ukv = w_ukv[l].reshape(MLA_KV_RANK, MLA_HEADS, MLA_NOPE + MLA_V)
    wk = jnp.concatenate([ukv[:, :, :MLA_NOPE], jnp.zeros((MLA_KV_RANK, MLA_HEADS, 64), F32)],
                         axis=2).reshape(MLA_KV_RANK, 1024).astype(BF16)
    wv = ukv[:, :, MLA_NOPE:].reshape(MLA_KV_RANK, 512).astype(BF16)
    return wide, wqa, wk, wv


def _rope_placement():
    e = np.zeros((128, 1024), np.float32)
    for hd in range(MLA_HEADS):
        for i in range(MLA_ROPE):
            e[i, hd * 128 + MLA_NOPE + i] = 1.0
    return jnp.asarray(e, BF16)


def _moe_dispatch_plan(eidx, epos, counts):
    padded = ((counts + TE - 1) // TE) * TE
    ends = jnp.cumsum(padded)
    offs = ends - padded
    ids = jnp.arange(N_EXPERTS, dtype=jnp.int32)
    picked_off = jnp.sum(jnp.where(eidx[:, :, None] == ids, offs, 0), axis=-1)
    slot = picked_off + epos
    starts = jnp.arange(NTE, dtype=jnp.int32) * TE
    tile_expert = jnp.sum((ends[None, :] <= starts[:, None]).astype(jnp.int32), axis=1)
    tile_expert = jnp.minimum(tile_expert, N_EXPERTS - 1)
    pick = tile_expert[:, None] == ids[None, :]
    last_real = jnp.sum(jnp.where(pick, (offs + counts)[None, :], 0), axis=1)
    n_real = jnp.clip(last_real - starts, 0, TE)
    n_real = jnp.where(starts < ends[-1], n_real, 0)
    tile_chunks = (n_real + EXPERT_ROWS - 1) // EXPERT_ROWS
    return slot, tile_expert, tile_chunks.astype(jnp.int32)


def kernel(x_prompt, x_sample, cache_mla_ckv, cache_mla_krope, cache_swa_k, cache_swa_v, c, c_ctx,
           ada_w, ada_b, norm_g, w_in, q_norm, kv_norm, w_fnet, w_uq, w_ukv, w_mla_o, swa_sink,
           w_swa_o, w_gate, b_gate, w_out, router_w, router_bias, exp_w1, exp_w3, exp_w2,
           shared_w1, shared_w3, shared_w2):
    xc = x_prompt.reshape(N_CTX, D_MODEL)
    xl = x_sample.reshape(N_LAT, D_MODEL)

    cond8 = jnp.concatenate([c_ctx[None, :], c, jnp.zeros((3, D_MODEL), F32)], axis=0)
    mod = _modulation(cond8, ada_w, ada_b)
    tile_cond = np.concatenate([np.zeros(NB_CTX, np.int32),
                                1 + np.arange(NB - NB_CTX, dtype=np.int32) // LAT_BLOCKS])

    tab = _rope_tables()
    bd, f_ctx, f_lat = _fnet_tables()
    e_mat = _rope_placement()
    tri = jnp.asarray(np.triu(np.ones((TM, TM), np.float32), 1), BF16)
    w_gate_b, w_fnet_b, w_mla_o_b, w_swa_o_b, w_out_b, sw1_b, sw3_b, sw2_b = (
        w.astype(BF16) for w in (w_gate, w_fnet, w_mla_o, w_swa_o, w_out, shared_w1, shared_w3, shared_w2))

    new_ckv, new_kr, new_k, new_v = [], [], [], []
    for l in range(DEPTH):
        modt = mod[l][tile_cond][:, None, :]
        wide, wqa, wk, wv = _layer_weights(l, w_in, w_uq, w_ukv)
        ng = norm_g[l]

        fin, ckv, kr, sq, sk, sv, gates, q_m, k_m, v_m = _stage_a(
            l, xc, xl, modt, ng[0:1], wide, w_gate_b, b_gate[l][None, :],
            q_norm[l][None, :], kv_norm[l][None, :], tab, wqa, wk, e_mat, wv)

        new_ckv.append(ckv[:N_CTX].reshape(BATCH, SEQ, MLA_KV_RANK))
        new_kr.append(kr[:N_CTX, :MLA_ROPE].reshape(BATCH, SEQ, MLA_ROPE))
        new_k.append(sk[:N_CTX].reshape(BATCH, SEQ, SWA_KV_HEADS, SWA_HEAD_DIM))
        new_v.append(sv[:N_CTX].reshape(BATCH, SEQ, SWA_KV_HEADS, SWA_HEAD_DIM))

        fn = (_fnet(fin, f_ctx, bd, BATCH, SEQ, 0),
              _fnet(fin, f_lat, bd, DEC_BATCH, DEC_SEQ, N_CTX // DEC_SEQ))

        kr_cache = jnp.pad(cache_mla_krope[:, l].reshape(N_CACHE, MLA_ROPE), ((0, 0), (0, 96)))
        k_c, v_c = _mla_cache_kv(cache_mla_ckv[:, l].reshape(N_CACHE, MLA_KV_RANK), kr_cache,
                                 wk, e_mat, wv)
        om = (_mla_attn(q_m, k_m, v_m, k_c, v_c, latent=False),
              _mla_attn(q_m, k_m, v_m, k_c, v_c, latent=True))

        ck = cache_swa_k[:, l].reshape(DEC_BATCH, PAST_LEN, 128)
        cv = cache_swa_v[:, l].reshape(DEC_BATCH, PAST_LEN, 128)
        osw = (_swa_attn(swa_sink[l], sq, sk, sv, ck, cv, latent=False),
               _swa_attn(swa_sink[l], sq, sk, sv, ck, cv, latent=True))

        x1, h2a, h2b, eidx, epos, ew, counts = _stage_e(
            l, xc, xl, modt, ng[1:2], ng[2:3], fn + om + osw, gates,
            w_fnet_b, w_mla_o_b, w_swa_o_b, w_out_b,
            router_w[l].T.astype(BF16), router_bias[l][:, None], tri)
        slot, tile_expert, tile_valid = _moe_dispatch_plan(eidx, epos, counts[:, 0])
        xsa = _sc_scatter_rows(h2a, slot)
        xsb = _sc_scatter_rows(h2b, slot)
        ysa, ysb = _experts(l, tile_expert, tile_valid, xsa, xsb, exp_w1, exp_w3, exp_w2)
        picks = slot[:TOP_K].reshape(1, TOP_K * N_TOK)
        yga = _sc_gather_rows(ysa, picks).reshape(TOP_K, N_TOK, PACKED)
        ygb = _sc_gather_rows(ysb, picks).reshape(TOP_K, N_TOK, PACKED)
        xc, xl = _stage_g(l, x1, modt, ng[3:4], yga, ygb, ew.T, h2a, h2b, sw1_b, sw3_b, sw2_b)

    y_p = xc.reshape(BATCH, SEQ, D_MODEL)
    y_s = xl.reshape(DEC_BATCH, DEC_SEQ, D_MODEL)
    return (y_p, y_s, jnp.stack(new_ckv, axis=1), jnp.stack(new_kr, axis=1),
            jnp.stack(new_k, axis=1), jnp.stack(new_v, axis=1))
```

```python
import functools
import math

import numpy as np
import jax
import jax.numpy as jnp
from jax import lax
from jax.experimental import pallas as pl
from jax.experimental.pallas import tpu as pltpu
from jax.experimental.pallas import tpu_sc as plsc

D_MODEL = 1024
BATCH = 16
SEQ = 256
DEPTH = 2
DEC_BATCH = 4
DEC_SEQ = 2048
PAST_LEN = 512
GRID_W = 64
EPS = 1e-6
ROPE_BASE = 10000.0
NEG_INF = -1e30

FNET_GROUPS = 8
FNET_GROUP_DIM = 64
FNET_WIDTH = 512
MLA_HEADS = 8
MLA_Q_RANK = 384
MLA_KV_RANK = 128
MLA_NOPE = 64
MLA_ROPE = 32
MLA_V = 64
MLA_SCALE = (MLA_NOPE + MLA_ROPE) ** -0.5
LOG2E = math.log2(math.e)
SWA_HEADS = 8
SWA_KV_HEADS = 2
SWA_HEAD_DIM = 64
SWA_WINDOW = 128
SWA_SCALE = SWA_HEAD_DIM ** -0.5
N_MOD = 6
N_EXPERTS = 64
N_EXPERT_GROUPS = 8
TOPK_GROUPS = 4
TOP_K = 6
EXPERT_FF = 256
SHARED_FF = 256
ROUTED_SCALE = 2.5

LANES = 128
TM = 256
N_CTX = BATCH * SEQ
N_LAT = DEC_BATCH * DEC_SEQ
N_TOK = N_CTX + N_LAT
N_CACHE = DEC_BATCH * PAST_LEN
NT_CTX = N_CTX // TM
NT_LAT = N_LAT // TM
NT = N_TOK // TM
LAT_TILES = DEC_SEQ // TM
TB = 512
NB = N_TOK // TB
NB_CTX = N_CTX // TB
LAT_BLOCKS = DEC_SEQ // TB
MLA_LAT_TQ = 256
MLA_LAT_PAIRS = 4
FNET_ROWS = 1024
TE = 512
S_MAX = N_TOK * TOP_K + N_EXPERTS * TE
NTE = S_MAX // TE
EXPERT_ROWS = 256
VMEM_LIMIT = 56 * 1024 * 1024
PACKED = D_MODEL // 4
SC_ROWS = 128
SC_CORES = 2
SC_SUBCORES = 16

A_F = (0, 512)
A_QD = (512, 896)
A_KV = (896, 1024)
A_SQ = (1024, 1536)
A_SK = (1536, 1664)
A_SV = (1664, 1792)
A_KR = (1792, 1920)
W_IN_WIDE = 1920
TAB_W = 1280

F32 = jnp.float32
BF16 = jnp.bfloat16


def _cparams(n_axes, parallel=False):
    sem = ("parallel" if parallel else "arbitrary",) * n_axes
    return pltpu.CompilerParams(dimension_semantics=sem, vmem_limit_bytes=VMEM_LIMIT)


def _dot(a, b):
    return jnp.dot(a, b, preferred_element_type=F32)


def _dot_nt(a, b):
    return lax.dot_general(a, b, (((1,), (1,)), ((), ())), preferred_element_type=F32)


def _rms_rows(v, g):
    return v * lax.rsqrt(jnp.mean(v * v, axis=-1, keepdims=True) + EPS) * g


def _pack_rows(v):
    half = v.shape[1] // 2
    lo = lax.bitcast_convert_type(v[:, :half].astype(BF16).astype(F32), jnp.int32)
    hi = lax.bitcast_convert_type(v[:, half:].astype(BF16).astype(F32), jnp.int32)
    return jnp.bitwise_or(jnp.bitwise_and(hi, -65536), jnp.bitwise_and(jnp.right_shift(lo, 16), 65535))


def _unpack_rows(w):
    lo = lax.bitcast_convert_type(jnp.left_shift(w, 16), F32)
    hi = lax.bitcast_convert_type(jnp.bitwise_and(w, -65536), F32)
    return jnp.concatenate([lo, hi], axis=1)


def _pack_pair(v):
    half = v.shape[1] // 2
    return _pack_rows(v[:, :half]), _pack_rows(v[:, half:])


def _unpack_pair(a, b):
    return jnp.concatenate([_unpack_rows(a), _unpack_rows(b)], axis=1)


def _const_spec(shape):
    return pl.BlockSpec(shape, lambda *_: (0,) * len(shape))


def _layer_spec(shape, layer):
    return pl.BlockSpec((None,) + shape, lambda *_: (layer,) + (0,) * len(shape))


def _ctx_rows(width):
    return pl.BlockSpec((TB, width), lambda i: (jnp.minimum(i, NB_CTX - 1), 0))


def _lat_rows(width):
    return pl.BlockSpec((TB, width), lambda i: (jnp.maximum(i - NB_CTX, 0), 0))


def _tab_row_block(i):
    return jnp.where(i < NB_CTX, 0, 1 + (i - NB_CTX) % LAT_BLOCKS)


def _mod_kernel(cond_ref, w_ref, b_ref, o_ref):
    c = cond_ref[...]
    a = (c * jax.nn.sigmoid(c)).astype(BF16)
    o_ref[...] = _dot(a, w_ref[...].astype(BF16)) + b_ref[...]


def _modulation(cond8, ada_w, ada_b):
    tn = 512
    nj = N_MOD * D_MODEL // tn
    return pl.pallas_call(
        _mod_kernel,
        grid=(DEPTH, nj),
        in_specs=[
            pl.BlockSpec((8, D_MODEL), lambda l, j: (0, 0)),
            pl.BlockSpec((None, D_MODEL, tn), lambda l, j: (l, 0, j)),
            pl.BlockSpec((None, 1, tn), lambda l, j: (l, 0, j)),
        ],
        out_specs=pl.BlockSpec((None, 8, tn), lambda l, j: (l, 0, j)),
        out_shape=jax.ShapeDtypeStruct((DEPTH, 8, N_MOD * D_MODEL), F32),
        compiler_params=_cparams(2),
        name="modulation",
    )(cond8, ada_w, ada_b.reshape(DEPTH, 1, N_MOD * D_MODEL))


def _half_swap(x, half):
    n = x.shape[1]
    lane = lax.broadcasted_iota(jnp.int32, (1, n), 1)
    return jnp.where((lane & half) == 0, pltpu.roll(x, n - half, 1), pltpu.roll(x, half, 1))


def _mla_expand(rows, cq, ckv, kr, cos32, sin32, wqa_ref, wk_ref, e_ref, wv_ref,
                q_ref, k_ref, v_ref):
    if q_ref is not None:
        lane = lax.broadcasted_iota(jnp.int32, (1, LANES), 1)
        rope_lane = jnp.logical_and(lane >= MLA_NOPE, lane < MLA_NOPE + MLA_ROPE)
        cos_h = jnp.where(rope_lane, cos32, 1.0)
        sin_h = jnp.where(rope_lane, sin32, 0.0)
        for hd in range(MLA_HEADS):
            lo, hi = hd * LANES, (hd + 1) * LANES
            q = _dot(cq, wqa_ref[:, lo:hi])
            q = q * cos_h + _half_swap(q, MLA_ROPE // 4) * sin_h
            q_ref[rows, lo:hi] = (q * (MLA_SCALE * LOG2E)).astype(BF16)
    k_ref[rows, :] = (_dot(ckv, wk_ref[...]) + _dot(kr, e_ref[...])).astype(BF16)
    v_ref[rows, :] = _dot(ckv, wv_ref[...]).astype(BF16)


def _chunks():
    return [pl.ds(r * TM, TM) for r in range(TB // TM)]


def _stage_a_kernel(xc_ref, xl_ref, mod_ref, g_ref, win_ref, wg_ref, bg_ref, qn_ref, kvn_ref, tab_ref,
                    wqa_ref, wk_ref, e_ref, wv_ref,
                    fin_ref, ckv_ref, kr_ref, sq_ref, sk_ref, sv_ref, gates_ref,
                    qm_ref, km_ref, vm_ref):
    is_ctx = pl.program_id(0) < NB_CTX
    for rows in _chunks():
        x = jnp.where(is_ctx, xc_ref[rows, :], xl_ref[rows, :])
        h = (_rms_rows(x, g_ref[...]) * (1.0 + mod_ref[:, 1024:2048]) + mod_ref[:, 0:1024]).astype(BF16)

        def proj(seg):
            return _dot(h, win_ref[:, seg[0]:seg[1]])

        fin_ref[rows, :] = proj(A_F).astype(BF16)
        cq = _rms_rows(proj(A_QD), qn_ref[...]).astype(BF16)
        ckv = _rms_rows(proj(A_KV), kvn_ref[...])
        ckv_ref[rows, :] = ckv
        cos64 = tab_ref[rows, 0:512]
        sin64 = tab_ref[rows, 512:1024]
        sq = proj(A_SQ)
        sq = sq * cos64 + _half_swap(sq, SWA_HEAD_DIM // 4) * sin64
        sq_ref[rows, :] = (sq * (SWA_SCALE * LOG2E)).astype(BF16)
        sk = proj(A_SK)
        sk_ref[rows, :] = sk * cos64[:, 0:128] + _half_swap(sk, SWA_HEAD_DIM // 4) * sin64[:, 0:128]
        sv_ref[rows, :] = proj(A_SV)
        cos32 = tab_ref[rows, 1024:1152]
        sin32 = tab_ref[rows, 1152:1280]
        kr = proj(A_KR)
        kr = kr * cos32 + _half_swap(kr, MLA_ROPE // 4) * sin32
        kr_ref[rows, :] = kr
        _mla_expand(rows, cq, ckv.astype(BF16), kr.astype(BF16), cos32, sin32,
                    wqa_ref, wk_ref, e_ref, wv_ref, qm_ref, km_ref, vm_ref)
        for c in range(3):
            lo, hi = c * D_MODEL, (c + 1) * D_MODEL
            gates_ref[rows, lo:hi] = jax.nn.sigmoid(_dot(h, wg_ref[:, lo:hi]) + bg_ref[:, lo:hi]).astype(BF16)


def _stage_a(layer, xc, xl, modt, g0, w_in_wide, w_gate, b_gate, q_norm, kv_norm, tab, wqa, wk, e_mat, wv):
    row = lambda w: pl.BlockSpec((TB, w), lambda i: (i, 0))
    outs = [(512, BF16), (128, F32), (128, F32), (512, BF16), (128, F32), (128, F32),
            (3 * D_MODEL, BF16), (1024, BF16), (1024, BF16), (512, BF16)]
    return pl.pallas_call(
        _stage_a_kernel,
        grid=(NB,),
        in_specs=[
            _ctx_rows(D_MODEL), _lat_rows(D_MODEL),
            pl.BlockSpec((None, 1, N_MOD * D_MODEL), lambda i: (i, 0, 0)),
            _const_spec((1, D_MODEL)),
            _const_spec((D_MODEL, W_IN_WIDE)),
            _layer_spec((D_MODEL, 3 * D_MODEL), layer),
            _const_spec((1, 3 * D_MODEL)),
            _const_spec((1, MLA_Q_RANK)),
            _const_spec((1, MLA_KV_RANK)),
            pl.BlockSpec((TB, TAB_W), lambda i: (_tab_row_block(i), 0)),
            _const_spec((MLA_Q_RANK, 1024)),
            _const_spec((128, 1024)), _const_spec((128, 1024)), _const_spec((128, 512)),
        ],
        out_specs=[row(w) for w, _ in outs],
        out_shape=[jax.ShapeDtypeStruct((N_TOK, w), dt) for w, dt in outs],
        compiler_params=_cparams(1),
        name="stage_a",
    )(xc, xl, modt, g0, w_in_wide, w_gate, b_gate, q_norm, kv_norm, tab, wqa, wk, e_mat, wv)


def _fnet_kernel(t_len, scale, fin_ref, f_ref, bd_ref, o_ref, zz_ref):
    @pl.when(pl.program_id(1) == 0)
    def _():
        z = fin_ref[...]
        zz_ref[0:t_len, :] = _dot(z, bd_ref[:, 0:512]).astype(BF16)
        zz_ref[t_len:2 * t_len, :] = _dot(z, bd_ref[:, 512:1024]).astype(BF16)

    o_ref[...] = (_dot(f_ref[...], zz_ref[...]) * scale).astype(BF16)


def _fnet(fin, fmat, bd, n_batch, t_len, row_block0):
    scale = 1.0 / math.sqrt(t_len * FNET_GROUP_DIM)
    ft = min(t_len, FNET_ROWS)
    return pl.pallas_call(
        functools.partial(_fnet_kernel, t_len, scale),
        grid=(n_batch, t_len // ft),
        in_specs=[
            pl.BlockSpec((t_len, FNET_WIDTH), lambda b, i: (row_block0 + b, 0)),
            pl.BlockSpec((ft, 2 * t_len), lambda b, i: (i, 0)),
            _const_spec((FNET_WIDTH, 2 * FNET_WIDTH)),
        ],
        out_specs=pl.BlockSpec((ft, FNET_WIDTH), lambda b, i: (b * (t_len // ft) + i, 0)),
        out_shape=jax.ShapeDtypeStruct((n_batch * t_len, FNET_WIDTH), BF16),
        scratch_shapes=[pltpu.VMEM((2 * t_len, FNET_WIDTH), BF16)],
        compiler_params=_cparams(2),
        name=f"fnet_{t_len}",
    )(fin, fmat, bd)


def _mla_cache_kernel(ckv_ref, kr_ref, wk_ref, e_ref, wv_ref, k_ref, v_ref):
    _mla_expand(slice(None), None, ckv_ref[...].astype(BF16), kr_ref[...].astype(BF16), None, None,
                None, wk_ref, e_ref, wv_ref, None, k_ref, v_ref)


def _mla_cache_kv(ckv_cache, kr_cache, wk, e_mat, wv):
    row = lambda w: pl.BlockSpec((TM, w), lambda i: (i, 0))
    return pl.pallas_call(
        _mla_cache_kernel,
        grid=(N_CACHE // TM,),
        in_specs=[row(128), row(128),
                  _const_spec((128, 1024)), _const_spec((128, 1024)), _const_spec((128, 512))],
        out_specs=[row(1024), row(512)],
        out_shape=[jax.ShapeDtypeStruct((N_CACHE, 1024), BF16),
                   jax.ShapeDtypeStruct((N_CACHE, 512), BF16)],
        compiler_params=_cparams(1),
        name="mla_cache_kv",
    )(ckv_cache, kr_cache, wk, e_mat, wv)


def _mla_attn_kernel(n_seg, pairs, q_ref, *refs):
    k_refs = refs[0:n_seg]
    v_refs = refs[n_seg:2 * n_seg]
    o_ref = refs[2 * n_seg]
    lane = lax.broadcasted_iota(jnp.int32, (1, LANES), 1)
    low = lane < MLA_V
    for pr in range(pairs):
        outs = []
        for hh in range(2):
            hd = 2 * pr + hh
            q = q_ref[:, hd * LANES:(hd + 1) * LANES]
            ss = [_dot_nt(q, k[:, hd * LANES:(hd + 1) * LANES]) for k in k_refs]
            m = functools.reduce(jnp.maximum, [s.max(axis=-1, keepdims=True) for s in ss])
            keep = low if hh == 0 else jnp.logical_not(low)
            sum_lane = MLA_V if hh == 0 else 0
            po = None
            for s, v_ref in zip(ss, v_refs):
                v = v_ref[:, pr * LANES:(pr + 1) * LANES]
                vm = jnp.where(lane == sum_lane, jnp.ones_like(v), jnp.where(keep, v, jnp.zeros_like(v)))
                t = _dot(jnp.exp2(s - m).astype(BF16), vm)
                po = t if po is None else po + t
            outs.append(po / po[:, sum_lane:sum_lane + 1])
        o_ref[:, pr * LANES:(pr + 1) * LANES] = jnp.where(low, outs[0], outs[1]).astype(BF16)


def _mla_attn(q_all, k_all, v_all, k_cache, v_cache, latent):
    if latent:
        tq, pairs = MLA_LAT_TQ, MLA_LAT_PAIRS
        n_b, n_q = DEC_BATCH, DEC_SEQ // tq
        q0 = N_CTX // tq
        kv_specs = [
            pl.BlockSpec((PAST_LEN, 256 * pairs), lambda b, hp, i: (b, hp)),
            pl.BlockSpec((DEC_SEQ, 256 * pairs), lambda b, hp, i: (N_CTX // DEC_SEQ + b, hp)),
            pl.BlockSpec((PAST_LEN, 128 * pairs), lambda b, hp, i: (b, hp)),
            pl.BlockSpec((DEC_SEQ, 128 * pairs), lambda b, hp, i: (N_CTX // DEC_SEQ + b, hp)),
        ]
        args = (q_all, k_cache, k_all, v_cache, v_all)
        n_seg = 2
    else:
        tq, pairs = SEQ, MLA_HEADS // 2
        n_b, n_q = BATCH, 1
        q0 = 0
        kv_specs = [
            pl.BlockSpec((SEQ, 256 * pairs), lambda b, hp, i: (b, hp)),
            pl.BlockSpec((SEQ, 128 * pairs), lambda b, hp, i: (b, hp)),
        ]
        args = (q_all, k_all, v_all)
        n_seg = 1
    return pl.pallas_call(
        functools.partial(_mla_attn_kernel, n_seg, pairs),
        grid=(n_b, MLA_HEADS // (2 * pairs), n_q),
        in_specs=[pl.BlockSpec((tq, 256 * pairs), lambda b, hp, i: (q0 + b * n_q + i, hp))] + kv_specs,
        out_specs=pl.BlockSpec((tq, 128 * pairs), lambda b, hp, i: (b * n_q + i, hp)),
        out_shape=jax.ShapeDtypeStruct((n_b * n_q * tq, MLA_HEADS * MLA_V), BF16),
        compiler_params=_cparams(3),
        name="mla_attn_lat" if latent else "mla_attn_ctx",
    )(*args)


def _swa_kernel(windowed, n_qb, sink_ref, q_ref, *refs):
    n_seg = 4 if windowed else 1
    k_refs = refs[0:n_seg]
    v_refs = refs[n_seg:2 * n_seg]
    o_ref = refs[2 * n_seg]
    tq = q_ref.shape[0]
    qb = pl.program_id(1)
    lane = lax.broadcasted_iota(jnp.int32, (1, LANES), 1)
    low = lane < SWA_HEAD_DIM
    high = jnp.logical_not(low)

    k_all = jnp.concatenate([r[...] for r in k_refs], axis=0)
    v_all = jnp.concatenate([r[...] for r in v_refs], axis=0)
    k_sw = pltpu.roll(k_all, SWA_HEAD_DIM, 1)
    v_sw = pltpu.roll(v_all, SWA_HEAD_DIM, 1)

    if windowed:
        qi = lax.broadcasted_iota(jnp.int32, (2 * tq, SWA_WINDOW), 0) % tq
        kj = lax.broadcasted_iota(jnp.int32, (2 * tq, SWA_WINDOW), 1)
        bias_prev = jnp.where(jnp.logical_and(kj >= qi, qb > 0), 0.0, NEG_INF)
        bias_next = jnp.where(jnp.logical_and(kj <= qi, qb < n_qb - 1), 0.0, NEG_INF)
    top_rows = lax.broadcasted_iota(jnp.int32, (2 * tq, 1), 0) < tq

    for g in range(SWA_KV_HEADS):
        qs = jnp.concatenate([q_ref[:, 256 * g:256 * g + 128],
                              q_ref[:, 256 * g + 128:256 * g + 256]], axis=0)
        halves = []
        for half in range(2):
            keep = low if half == 0 else high
            sum_lane = SWA_HEAD_DIM if half == 0 else 0
            straight = (g == half)
            kh = jnp.where(keep, k_all if straight else k_sw, 0.0).astype(BF16)
            vh = jnp.where(lane == sum_lane, 1.0,
                           jnp.where(keep, v_all if straight else v_sw, 0.0)).astype(BF16)
            s = _dot_nt(qs, kh)
            if windowed:
                c0, c1, c2 = PAST_LEN, PAST_LEN + SWA_WINDOW, PAST_LEN + 2 * SWA_WINDOW
                s = jnp.concatenate([s[:, :c0], s[:, c0:c1] + bias_prev, s[:, c1:c2],
                                     s[:, c2:] + bias_next], axis=1)
            sink = jnp.where(top_rows, sink_ref[4 * g + half], sink_ref[4 * g + 2 + half]) * LOG2E
            m = jnp.maximum(s.max(axis=-1, keepdims=True), sink)
            po = _dot(jnp.exp2(s - m).astype(BF16), vh)
            halves.append(po / (po[:, sum_lane:sum_lane + 1] + jnp.exp2(sink - m)))
        out = jnp.where(low, halves[0], halves[1])
        o_ref[:, 256 * g:256 * g + 128] = out[0:tq].astype(BF16)
        o_ref[:, 256 * g + 128:256 * g + 256] = out[tq:2 * tq].astype(BF16)


def _swa_attn(sink, sq, sk, sv, cache_k, cache_v, latent):
    smem = pl.BlockSpec(memory_space=pltpu.SMEM)
    if latent:
        tq = SWA_WINDOW
        n_b, n_qb = DEC_BATCH, DEC_SEQ // tq
        base = N_CTX // tq

        def prev(b, i):
            return (base + b * n_qb + jnp.maximum(i - 1, 0), 0)

        def cur(b, i):
            return (base + b * n_qb + i, 0)

        def nxt(b, i):
            return (base + b * n_qb + jnp.minimum(i + 1, n_qb - 1), 0)

        cache = pl.BlockSpec((None, PAST_LEN, 128), lambda b, i: (b, 0, 0))
        blk = lambda f: pl.BlockSpec((tq, 128), f)
        kv_specs = [cache, blk(prev), blk(cur), blk(nxt)] * 2
        args = (cache_k, sk, sk, sk, cache_v, sv, sv, sv)
        q_spec = pl.BlockSpec((tq, 512), cur)
        o_spec = pl.BlockSpec((tq, 512), lambda b, i: (b * n_qb + i, 0))
    else:
        tq = SEQ
        n_b, n_qb = BATCH, 1
        blk = pl.BlockSpec((tq, 128), lambda b, i: (b, 0))
        kv_specs = [blk, blk]
        args = (sk, sv)
        q_spec = pl.BlockSpec((tq, 512), lambda b, i: (b, 0))
        o_spec = q_spec
    return pl.pallas_call(
        functools.partial(_swa_kernel, latent, n_qb),
        grid=(n_b, n_qb),
        in_specs=[smem, q_spec] + kv_specs,
        out_specs=o_spec,
        out_shape=jax.ShapeDtypeStruct((n_b * n_qb * tq, 512), BF16),
        compiler_params=_cparams(2),
        name="swa_lat" if latent else "swa_ctx",
    )(sink, sq, *args)


def _route(h, rwt_ref, rb_ref, tri_ref, carry):
    gsz = N_EXPERTS // N_EXPERT_GROUPS
    scores = jax.nn.sigmoid(_dot_nt(rwt_ref[...], h))
    biased = scores + rb_ref[...]
    mem = lax.broadcasted_iota(jnp.int32, (gsz, TM), 0).astype(F32)
    gs_rows = []
    for g in range(N_EXPERT_GROUPS):
        bg = biased[g * gsz:(g + 1) * gsz, :]
        m1 = bg.max(axis=0, keepdims=True)
        first = jnp.min(jnp.where(bg == m1, mem, float(gsz)), axis=0, keepdims=True)
        m2 = jnp.where(mem == first, -jnp.inf, bg).max(axis=0, keepdims=True)
        gs_rows.append(m1 + m2)
    gs = jnp.concatenate(gs_rows, axis=0)
    gid = lax.broadcasted_iota(jnp.int32, gs.shape, 0).astype(F32)
    gsel = jnp.zeros(gs.shape, F32)
    for _ in range(TOPK_GROUPS):
        mx = gs.max(axis=0, keepdims=True)
        pick = gid == jnp.min(jnp.where(gs == mx, gid, float(N_EXPERT_GROUPS)), axis=0, keepdims=True)
        gsel = jnp.where(pick, 1.0, gsel)
        gs = jnp.where(pick, -jnp.inf, gs)
    emask = jnp.concatenate(
        [jnp.broadcast_to(gsel[g:g + 1, :], (gsz, TM)) for g in range(N_EXPERT_GROUPS)], axis=0)
    cand = jnp.where(emask > 0.5, biased, NEG_INF)
    eid = lax.broadcasted_iota(jnp.int32, cand.shape, 0).astype(F32)
    picks = []
    self32 = jnp.zeros(cand.shape, F32)
    for _ in range(TOP_K):
        mx = cand.max(axis=0, keepdims=True)
        pick = eid == jnp.min(jnp.where(cand == mx, eid, float(N_EXPERTS)), axis=0, keepdims=True)
        picks.append(pick)
        self32 = jnp.where(pick, 1.0, self32)
        cand = jnp.where(pick, -jnp.inf, cand)
    pos = _dot(self32.astype(BF16), tri_ref[...]) + carry
    sel_scores = [jnp.sum(jnp.where(p, scores, 0.0), axis=0, keepdims=True) for p in picks]
    wsum = functools.reduce(lambda a, b: a + b, sel_scores)
    zero_f = jnp.zeros((2, TM), F32)
    eidx = [jnp.sum(jnp.where(p, eid, 0.0), axis=0, keepdims=True) for p in picks]
    epos = [jnp.sum(jnp.where(p, pos, 0.0), axis=0, keepdims=True) for p in picks]
    ew = [s / wsum * ROUTED_SCALE for s in sel_scores]
    return (jnp.concatenate(eidx + [zero_f], axis=0).astype(jnp.int32),
            jnp.concatenate(epos + [zero_f], axis=0).astype(jnp.int32),
            jnp.concatenate(ew + [zero_f], axis=0),
            carry + jnp.sum(self32, axis=1, keepdims=True))


def _stage_e_kernel(xc_ref, xl_ref, mod_ref, g1_ref, g2_ref, fnc_ref, fnl_ref, omc_ref, oml_ref, osc_ref, osl_ref,
                    gates_ref, wf_ref, wm_ref, ws_ref, wo_ref, rwt_ref, rb_ref, tri_ref,
                    x1_ref, h2a_ref, h2b_ref, eidx_ref, epos_ref, ew_ref, cnt_ref, carry_ref):
    @pl.when(pl.program_id(0) == 0)
    def _():
        carry_ref[...] = jnp.zeros_like(carry_ref)

    is_ctx = pl.program_id(0) < NB_CTX
    carry = carry_ref[...]
    for r, rows in enumerate(_chunks()):
        fn = jnp.where(is_ctx, fnc_ref[rows, :], fnl_ref[rows, :])
        om = jnp.where(is_ctx, omc_ref[rows, :], oml_ref[rows, :])
        osw = jnp.where(is_ctx, osc_ref[rows, :], osl_ref[rows, :])
        merged = (gates_ref[rows, 0:1024].astype(F32) * _dot(fn, wf_ref[...])
                  + gates_ref[rows, 1024:2048].astype(F32) * _dot(om, wm_ref[...])
                  + gates_ref[rows, 2048:3072].astype(F32) * _dot(osw, ws_ref[...]))
        mix = _dot(merged.astype(BF16), wo_ref[...])
        x = jnp.where(is_ctx, xc_ref[rows, :], xl_ref[rows, :])
        x1 = x + mod_ref[:, 2048:3072] * _rms_rows(mix, g1_ref[...])
        x1_ref[rows, :] = x1
        h2 = _rms_rows(x1, g2_ref[...]) * (1.0 + mod_ref[:, 4096:5120]) + mod_ref[:, 3072:4096]
        h2a_ref[rows, :], h2b_ref[rows, :] = _pack_pair(h2)
        cols = pl.ds(r * TM, TM)
        eidx_ref[:, cols], epos_ref[:, cols], ew_ref[:, cols], carry = _route(
            h2.astype(BF16), rwt_ref, rb_ref, tri_ref, carry)
    carry_ref[...] = carry
    cnt_ref[...] = jnp.broadcast_to(carry, cnt_ref.shape).astype(jnp.int32)


def _stage_e(layer, xc, xl, modt, g1, g2, mixed, gates, wf, wm, ws, wo, rwt, rbias, tri):
    row = lambda w: pl.BlockSpec((TB, w), lambda i: (i, 0))
    ctx, lat = _ctx_rows(512), _lat_rows(512)
    col = lambda dt: (pl.BlockSpec((8, TB), lambda i: (0, i)), jax.ShapeDtypeStruct((8, N_TOK), dt))
    picks = [col(jnp.int32), col(jnp.int32), col(F32)]
    return pl.pallas_call(
        _stage_e_kernel,
        grid=(NB,),
        in_specs=[
            _ctx_rows(D_MODEL), _lat_rows(D_MODEL),
            pl.BlockSpec((None, 1, N_MOD * D_MODEL), lambda i: (i, 0, 0)),
            _const_spec((1, D_MODEL)), _const_spec((1, D_MODEL)),
            ctx, lat, ctx, lat, ctx, lat, row(3 * D_MODEL),
            _layer_spec((512, D_MODEL), layer), _layer_spec((512, D_MODEL), layer),
            _layer_spec((512, D_MODEL), layer), _layer_spec((D_MODEL, D_MODEL), layer),
            _const_spec((N_EXPERTS, D_MODEL)), _const_spec((N_EXPERTS, 1)), _const_spec((TM, TM)),
        ],
        out_specs=[row(D_MODEL), row(PACKED), row(PACKED)] + [s for s, _ in picks]
        + [_const_spec((N_EXPERTS, LANES))],
        out_shape=[jax.ShapeDtypeStruct((N_TOK, D_MODEL), F32),
                   jax.ShapeDtypeStruct((N_TOK, PACKED), jnp.int32),
                   jax.ShapeDtypeStruct((N_TOK, PACKED), jnp.int32)] + [s for _, s in picks]
        + [jax.ShapeDtypeStruct((N_EXPERTS, LANES), jnp.int32)],
        scratch_shapes=[pltpu.VMEM((N_EXPERTS, 1), F32)],
        compiler_params=_cparams(1),
        name="stage_e",
    )(xc, xl, modt, g1, g2, *mixed, gates, wf, wm, ws, wo, rwt, rbias, tri)


def _expert_kernel(te_ref, tv_ref, xa_ref, xb_ref, w1_ref, w3_ref, w2_ref, oa_ref, ob_ref,
                   w1b_ref, w3b_ref, w2b_ref):
    j = pl.program_id(0)
    new_expert = jnp.logical_or(j == 0, te_ref[j] != te_ref[jnp.maximum(j - 1, 0)])

    @pl.when(jnp.logical_and(tv_ref[j] > 0, new_expert))
    def _():
        w1b_ref[...] = w1_ref[...].astype(BF16)
        w3b_ref[...] = w3_ref[...].astype(BF16)
        w2b_ref[...] = w2_ref[...].astype(BF16)

    def run(n_chunks):
        for r in range(n_chunks):
            rows = pl.ds(r * EXPERT_ROWS, EXPERT_ROWS)
            x = _unpack_pair(xa_ref[rows, :], xb_ref[rows, :]).astype(BF16)
            hg = _dot(x, w1b_ref[...])
            hu = _dot(x, w3b_ref[...])
            act = (jax.nn.silu(hg) * hu).astype(BF16)
            oa_ref[rows, :], ob_ref[rows, :] = _pack_pair(_dot(act, w2b_ref[...]))

    for n_chunks in range(1, TE // EXPERT_ROWS + 1):
        pl.when(tv_ref[j] == n_chunks)(functools.partial(run, n_chunks))


def _experts(layer, tile_expert, tile_valid, xsa, xsb, w1, w3, w2):
    slot_rows = pl.BlockSpec((TE, PACKED), lambda j, te, tv: (j, 0))
    grid_spec = pltpu.PrefetchScalarGridSpec(
        num_scalar_prefetch=2,
        grid=(NTE,),
        in_specs=[
            slot_rows, slot_rows,
            pl.BlockSpec((None, None, D_MODEL, EXPERT_FF), lambda j, te, tv: (layer, te[j], 0, 0)),
            pl.BlockSpec((None, None, D_MODEL, EXPERT_FF), lambda j, te, tv: (layer, te[j], 0, 0)),
            pl.BlockSpec((None, None, EXPERT_FF, D_MODEL), lambda j, te, tv: (layer, te[j], 0, 0)),
        ],
        out_specs=[slot_rows, slot_rows],
        scratch_shapes=[pltpu.VMEM((D_MODEL, EXPERT_FF), BF16), pltpu.VMEM((D_MODEL, EXPERT_FF), BF16),
                        pltpu.VMEM((EXPERT_FF, D_MODEL), BF16)],
    )
    return pl.pallas_call(
        _expert_kernel,
        grid_spec=grid_spec,
        out_shape=[jax.ShapeDtypeStruct((S_MAX, PACKED), jnp.int32)] * 2,
        compiler_params=_cparams(1),
        name="experts",
    )(tile_expert, tile_valid, xsa, xsb, w1, w3, w2)


def _sc_mesh():
    return plsc.VectorSubcoreMesh(core_axis_name="c", subcore_axis_name="s",
                                  num_cores=SC_CORES, num_subcores=SC_SUBCORES)


def _sc_scatter_rows(rows, slot8):
    @functools.partial(pl.kernel, mesh=_sc_mesh(), scratch_types=[pltpu.SemaphoreType.DMA],
                       out_type=jax.ShapeDtypeStruct((S_MAX, PACKED), jnp.int32))
    def scatter(x_hbm, i_hbm, o_hbm, sem):
        def body(x_vmem, i_vmem):
            copies = [pltpu.async_copy(x_vmem, o_hbm.at[i_vmem.at[k]], sem) for k in range(TOP_K)]
            for cp in copies:
                cp.wait()

        pltpu.emit_pipeline(
            body,
            grid=(N_TOK // SC_ROWS,),
            in_specs=[pl.BlockSpec((SC_ROWS, PACKED), lambda i: (i, 0)),
                      pl.BlockSpec((8, SC_ROWS), lambda i: (0, i))],
            out_specs=[],
            core_axis_name=("c", "s"),
            dimension_semantics=(pltpu.PARALLEL,),
        )(x_hbm, i_hbm)

    return scatter(rows, slot8)


def _sc_gather_rows(table, idx):
    n = idx.shape[1]

    @functools.partial(pl.kernel, mesh=_sc_mesh(), scratch_types=[],
                       out_type=jax.ShapeDtypeStruct((n, PACKED), jnp.int32))
    def gather(t_hbm, i_hbm, o_hbm):
        def body(i_vmem, o_vmem):
            pltpu.sync_copy(t_hbm.at[i_vmem.at[0]], o_vmem)

        pltpu.emit_pipeline(
            body,
            grid=(n // SC_ROWS,),
            in_specs=[pl.BlockSpec((1, SC_ROWS), lambda i: (0, i))],
            out_specs=[pl.BlockSpec((SC_ROWS, PACKED), lambda i: (i, 0))],
            core_axis_name=("c", "s"),
            dimension_semantics=(pltpu.PARALLEL,),
        )(i_hbm, o_hbm)

    return gather(table, idx)


def _stage_g_kernel(x1_ref, mod_ref, g3_ref, yga_ref, ygb_ref, ew_ref, ha_ref, hb_ref,
                    s1_ref, s3_ref, s2_ref, oc_ref, ol_ref):
    is_ctx = pl.program_id(0) < NB_CTX
    for rows in _chunks():
        h = _unpack_pair(ha_ref[rows, :], hb_ref[rows, :]).astype(BF16)
        act = jax.nn.silu(_dot(h, s1_ref[...])) * _dot(h, s3_ref[...])
        y = _dot(act.astype(BF16), s2_ref[...])
        for k in range(TOP_K):
            y = y + ew_ref[rows, k:k + 1] * _unpack_pair(yga_ref[k, rows, :], ygb_ref[k, rows, :])
        out = x1_ref[rows, :] + mod_ref[:, 5120:6144] * _rms_rows(y, g3_ref[...])

        @pl.when(is_ctx)
        def _():
            oc_ref[rows, :] = out

        @pl.when(jnp.logical_not(is_ctx))
        def _():
            ol_ref[rows, :] = out


def _stage_g(layer, x1, modt, g3, yga, ygb, ew_rows, h2a, h2b, s1, s3, s2):
    row = lambda w: pl.BlockSpec((TB, w), lambda i: (i, 0))
    picked = pl.BlockSpec((TOP_K, TB, PACKED), lambda i: (0, i, 0))
    return pl.pallas_call(
        _stage_g_kernel,
        grid=(NB,),
        in_specs=[row(D_MODEL), pl.BlockSpec((None, 1, N_MOD * D_MODEL), lambda i: (i, 0, 0)),
                  _const_spec((1, D_MODEL)), picked, picked, row(8), row(PACKED), row(PACKED),
                  _layer_spec((D_MODEL, SHARED_FF), layer), _layer_spec((D_MODEL, SHARED_FF), layer),
                  _layer_spec((SHARED_FF, D_MODEL), layer)],
        out_specs=[_ctx_rows(D_MODEL), _lat_rows(D_MODEL)],
        out_shape=[jax.ShapeDtypeStruct((N_CTX, D_MODEL), F32),
                   jax.ShapeDtypeStruct((N_LAT, D_MODEL), F32)],
        compiler_params=_cparams(1),
        name="stage_g",
    )(x1, modt, g3, yga, ygb, ew_rows, h2a, h2b, s1, s3, s2)


def _rope_tables():
    t = np.arange(DEC_SEQ)
    pos = np.stack([(t // GRID_W), (t % GRID_W)], axis=-1).astype(np.float32)

    def table(r):
        n_freq = r // 4
        inv = np.float32(ROPE_BASE) ** (-np.arange(n_freq, dtype=np.float32) / np.float32(n_freq))
        ang = pos[:, :, None] * inv.astype(np.float32)
        cos = np.cos(ang)
        sin = np.sin(ang)
        cos_t = np.stack([cos, cos], axis=2).reshape(DEC_SEQ, r)
        sin_t = np.stack([-sin, sin], axis=2).reshape(DEC_SEQ, r)
        return cos_t, sin_t

    c64, s64 = table(SWA_HEAD_DIM)
    c32, s32 = table(MLA_ROPE)
    lat = np.concatenate([np.tile(c64, (1, 8)), np.tile(s64, (1, 8)),
                          np.tile(c32, (1, 4)), np.tile(s32, (1, 4))], axis=1)
    ident = np.concatenate([np.ones((TB, 512)), np.zeros((TB, 512)),
                            np.ones((TB, 128)), np.zeros((TB, 128))], axis=1)
    return jnp.asarray(np.concatenate([ident, lat], axis=0).astype(np.float32))


def _dft_pair(n):
    k = np.arange(n, dtype=np.int64)
    ang = ((k[:, None] * k[None, :]) % n).astype(np.float64) * (2.0 * math.pi / n)
    return np.cos(ang), np.sin(ang)


def _fnet_tables():
    c64, s64 = _dft_pair(FNET_GROUP_DIM)
    eye = np.eye(FNET_GROUPS)
    bd = np.concatenate([np.kron(eye, c64), np.kron(eye, s64)], axis=1)
    mats = []
    for t_len in (SEQ, DEC_SEQ):
        c, s = _dft_pair(t_len)
        mats.append(np.concatenate([c, -s], axis=1))
    return tuple(jnp.asarray(m.astype(np.float32).astype(BF16)) for m in (bd, mats[0], mats[1]))


def _layer_weights(l, w_in, w_uq, w_ukv):
    w = w_in[l]
    wide = jnp.concatenate([w[:, 0:1024], w[:, 1056:1824], w[:, 1024:1056],
                            jnp.zeros((D_MODEL, 96), F32)], axis=1).astype(BF16)

    uq = w_uq[l].reshape(MLA_Q_RANK, MLA_HEADS, MLA_NOPE + MLA_ROPE)
    z32 = jnp.zeros((MLA_Q_RANK, MLA_HEADS, 32), F32)
    wqa = jnp.concatenate([uq, z32], axis=2).reshape(MLA_Q_RANK, 1024).astype(BF16)
    ukv = w_ukv[l].reshape(MLA_KV_RANK, MLA_HEADS, MLA_NOPE + MLA_V)
    wk = jnp.concatenate([ukv[:, :, :MLA_NOPE], jnp.zeros((MLA_KV_RANK, MLA_HEADS, 64), F32)],
                         axis=2).reshape(MLA_KV_RANK, 1024).astype(BF16)
    wv = ukv[:, :, MLA_NOPE:].reshape(MLA_KV_RANK, 512).astype(BF16)
    return wide, wqa, wk, wv


def _rope_placement():
    e = np.zeros((128, 1024), np.float32)
    for hd in range(MLA_HEADS):
        for i in range(MLA_ROPE):
            e[i, hd * 128 + MLA_NOPE + i] = 1.0
    return jnp.asarray(e, BF16)


def _moe_dispatch_plan(eidx, epos, counts):
    padded = ((counts + TE - 1) // TE) * TE
    ends = jnp.cumsum(padded)
    offs = ends - padded
    ids = jnp.arange(N_EXPERTS, dtype=jnp.int32)
    picked_off = jnp.sum(jnp.where(eidx[:, :, None] == ids, offs, 0), axis=-1)
    slot = picked_off + epos
    starts = jnp.arange(NTE, dtype=jnp.int32) * TE
    tile_expert = jnp.sum((ends[None, :] <= starts[:, None]).astype(jnp.int32), axis=1)
    tile_expert = jnp.minimum(tile_expert, N_EXPERTS - 1)
    pick = tile_expert[:, None] == ids[None, :]
    last_real = jnp.sum(jnp.where(pick, (offs + counts)[None, :], 0), axis=1)
    n_real = jnp.clip(last_real - starts, 0, TE)
    n_real = jnp.where(starts < ends[-1], n_real, 0)
    tile_chunks = (n_real + EXPERT_ROWS - 1) // EXPERT_ROWS
    return slot, tile_expert, tile_chunks.astype(jnp.int32)


def kernel(x_prompt, x_sample, cache_mla_ckv, cache_mla_krope, cache_swa_k, cache_swa_v, c, c_ctx,
           ada_w, ada_b, norm_g, w_in, q_norm, kv_norm, w_fnet, w_uq, w_ukv, w_mla_o, swa_sink,
           w_swa_o, w_gate, b_gate, w_out, router_w, router_bias, exp_w1, exp_w3, exp_w2,
           shared_w1, shared_w3, shared_w2):
    xc = x_prompt.reshape(N_CTX, D_MODEL)
    xl = x_sample.reshape(N_LAT, D_MODEL)

    cond8 = jnp.concatenate([c_ctx[None, :], c, jnp.zeros((3, D_MODEL), F32)], axis=0)
    mod = _modulation(cond8, ada_w, ada_b)
    tile_cond = np.concatenate([np.zeros(NB_CTX, np.int32),
                                1 + np.arange(NB - NB_CTX, dtype=np.int32) // LAT_BLOCKS])

    tab = _rope_tables()
    bd, f_ctx, f_lat = _fnet_tables()
    e_mat = _rope_placement()
    tri = jnp.asarray(np.triu(np.ones((TM, TM), np.float32), 1), BF16)
    w_gate_b, w_fnet_b, w_mla_o_b, w_swa_o_b, w_out_b, sw1_b, sw3_b, sw2_b = (
        w.astype(BF16) for w in (w_gate, w_fnet, w_mla_o, w_swa_o, w_out, shared_w1, shared_w3, shared_w2))

    new_ckv, new_kr, new_k, new_v = [], [], [], []
    for l in range(DEPTH):
        modt = mod[l][tile_cond][:, None, :]
        wide, wqa, wk, wv = _layer_weights(l, w_in, w_uq, w_ukv)
        ng = norm_g[l]

        fin, ckv, kr, sq, sk, sv, gates, q_m, k_m, v_m = _stage_a(
            l, xc, xl, modt, ng[0:1], wide, w_gate_b, b_gate[l][None, :],
            q_norm[l][None, :], kv_norm[l][None, :], tab, wqa, wk, e_mat, wv)

        new_ckv.append(ckv[:N_CTX].reshape(BATCH, SEQ, MLA_KV_RANK))
        new_kr.append(kr[:N_CTX, :MLA_ROPE].reshape(BATCH, SEQ, MLA_ROPE))
        new_k.append(sk[:N_CTX].reshape(BATCH, SEQ, SWA_KV_HEADS, SWA_HEAD_DIM))
        new_v.append(sv[:N_CTX].reshape(BATCH, SEQ, SWA_KV_HEADS, SWA_HEAD_DIM))

        fn = (_fnet(fin, f_ctx, bd, BATCH, SEQ, 0),
              _fnet(fin, f_lat, bd, DEC_BATCH, DEC_SEQ, N_CTX // DEC_SEQ))

        kr_cache = jnp.pad(cache_mla_krope[:, l].reshape(N_CACHE, MLA_ROPE), ((0, 0), (0, 96)))
        k_c, v_c = _mla_cache_kv(cache_mla_ckv[:, l].reshape(N_CACHE, MLA_KV_RANK), kr_cache,
                                 wk, e_mat, wv)
        om = (_mla_attn(q_m, k_m, v_m, k_c, v_c, latent=False),
              _mla_attn(q_m, k_m, v_m, k_c, v_c, latent=True))

        ck = cache_swa_k[:, l].reshape(DEC_BATCH, PAST_LEN, 128)
        cv = cache_swa_v[:, l].reshape(DEC_BATCH, PAST_LEN, 128)
        osw = (_swa_attn(swa_sink[l], sq, sk, sv, ck, cv, latent=False),
               _swa_attn(swa_sink[l], sq, sk, sv, ck, cv, latent=True))

        x1, h2a, h2b, eidx, epos, ew, counts = _stage_e(
            l, xc, xl, modt, ng[1:2], ng[2:3], fn + om + osw, gates,
            w_fnet_b, w_mla_o_b, w_swa_o_b, w_out_b,
            router_w[l].T.astype(BF16), router_bias[l][:, None], tri)
        slot, tile_expert, tile_valid = _moe_dispatch_plan(eidx, epos, counts[:, 0])
        xsa = _sc_scatter_rows(h2a, slot)
        xsb = _sc_scatter_rows(h2b, slot)
        ysa, ysb = _experts(l, tile_expert, tile_valid, xsa, xsb, exp_w1, exp_w3, exp_w2)
        picks = slot[:TOP_K].reshape(1, TOP_K * N_TOK)
        yga = _sc_gather_rows(ysa, picks).reshape(TOP_K, N_TOK, PACKED)
        ygb = _sc_gather_rows(ysb, picks).reshape(TOP_K, N_TOK, PACKED)
        xc, xl = _stage_g(l, x1, modt, ng[3:4], yga, ygb, ew.T, h2a, h2b, sw1_b, sw3_b, sw2_b)

    y_p = xc.reshape(BATCH, SEQ, D_MODEL)
    y_s = xl.reshape(DEC_BATCH, DEC_SEQ, D_MODEL)
    return (y_p, y_s, jnp.stack(new_ckv, axis=1), jnp.stack(new_kr, axis=1),
            jnp.stack(new_k, axis=1), jnp.stack(new_v, axis=1))
```

```python
import functools
import math

import numpy as np
import jax
import jax.numpy as jnp
from jax import lax
from jax.experimental import pallas as pl
from jax.experimental.pallas import tpu as pltpu
from jax.experimental.pallas import tpu_sc as plsc

D_MODEL = 1024
BATCH = 16
SEQ = 256
DEPTH = 2
DEC_BATCH = 4
DEC_SEQ = 2048
PAST_LEN = 512
GRID_W = 64
EPS = 1e-6
ROPE_BASE = 10000.0
NEG_INF = -1e30

FNET_GROUPS = 8
FNET_GROUP_DIM = 64
FNET_WIDTH = 512
MLA_HEADS = 8
MLA_Q_RANK = 384
MLA_KV_RANK = 128
MLA_NOPE = 64
MLA_ROPE = 32
MLA_V = 64
MLA_SCALE = (MLA_NOPE + MLA_ROPE) ** -0.5
LOG2E = math.log2(math.e)
SWA_HEADS = 8
SWA_KV_HEADS = 2
SWA_HEAD_DIM = 64
SWA_WINDOW = 128
SWA_SCALE = SWA_HEAD_DIM ** -0.5
N_MOD = 6
N_EXPERTS = 64
N_EXPERT_GROUPS = 8
TOPK_GROUPS = 4
TOP_K = 6
EXPERT_FF = 256
SHARED_FF = 256
ROUTED_SCALE = 2.5

LANES = 128
TM = 256
N_CTX = BATCH * SEQ
N_LAT = DEC_BATCH * DEC_SEQ
N_TOK = N_CTX + N_LAT
N_CACHE = DEC_BATCH * PAST_LEN
NT_CTX = N_CTX // TM
NT_LAT = N_LAT // TM
NT = N_TOK // TM
LAT_TILES = DEC_SEQ // TM
TB = 512
NB = N_TOK // TB
NB_CTX = N_CTX // TB
LAT_BLOCKS = DEC_SEQ // TB
MLA_LAT_TQ = 256
MLA_LAT_PAIRS = 4
FNET_ROWS = 1024
TE = 512
S_MAX = N_TOK * TOP_K + N_EXPERTS * TE
NTE = S_MAX // TE
EXPERT_ROWS = 256
VMEM_LIMIT = 56 * 1024 * 1024
PACKED = D_MODEL // 4
SC_ROWS = 128
SC_CORES = 2
SC_SUBCORES = 16

A_F = (0, 512)
A_QD = (512, 896)
A_KV = (896, 1024)
A_SQ = (1024, 1536)
A_SK = (1536, 1664)
A_SV = (1664, 1792)
A_KR = (1792, 1920)
W_IN_WIDE = 1920
TAB_W = 1280

F32 = jnp.float32
BF16 = jnp.bfloat16


def _cparams(n_axes, parallel=False):
    sem = ("parallel" if parallel else "arbitrary",) * n_axes
    return pltpu.CompilerParams(dimension_semantics=sem, vmem_limit_bytes=VMEM_LIMIT)


def _dot(a, b):
    return jnp.dot(a, b, preferred_element_type=F32)


def _dot_nt(a, b):
    return lax.dot_general(a, b, (((1,), (1,)), ((), ())), preferred_element_type=F32)


def _rms_rows(v, g):
    return v * lax.rsqrt(jnp.mean(v * v, axis=-1, keepdims=True) + EPS) * g


def _pack_rows(v):
    half = v.shape[1] // 2
    lo = lax.bitcast_convert_type(v[:, :half].astype(BF16).astype(F32), jnp.int32)
    hi = lax.bitcast_convert_type(v[:, half:].astype(BF16).astype(F32), jnp.int32)
    return jnp.bitwise_or(jnp.bitwise_and(hi, -65536), jnp.bitwise_and(jnp.right_shift(lo, 16), 65535))


def _unpack_rows(w):
    lo = lax.bitcast_convert_type(jnp.left_shift(w, 16), F32)
    hi = lax.bitcast_convert_type(jnp.bitwise_and(w, -65536), F32)
    return jnp.concatenate([lo, hi], axis=1)


def _pack_pair(v):
    half = v.shape[1] // 2
    return _pack_rows(v[:, :half]), _pack_rows(v[:, half:])


def _unpack_pair(a, b):
    return jnp.concatenate([_unpack_rows(a), _unpack_rows(b)], axis=1)


def _const_spec(shape):
    return pl.BlockSpec(shape, lambda *_: (0,) * len(shape))


def _layer_spec(shape, layer):
    return pl.BlockSpec((None,) + shape, lambda *_: (layer,) + (0,) * len(shape))


def _ctx_rows(width):
    return pl.BlockSpec((TB, width), lambda i: (jnp.minimum(i, NB_CTX - 1), 0))


def _lat_rows(width):
    return pl.BlockSpec((TB, width), lambda i: (jnp.maximum(i - NB_CTX, 0), 0))


def _tab_row_block(i):
    return jnp.where(i < NB_CTX, 0, 1 + (i - NB_CTX) % LAT_BLOCKS)


def _mod_kernel(cond_ref, w_ref, b_ref, o_ref):
    c = cond_ref[...]
    a = (c * jax.nn.sigmoid(c)).astype(BF16)
    o_ref[...] = _dot(a, w_ref[...].astype(BF16)) + b_ref[...]


def _modulation(cond8, ada_w, ada_b):
    tn = 512
    nj = N_MOD * D_MODEL // tn
    return pl.pallas_call(
        _mod_kernel,
        grid=(DEPTH, nj),
        in_specs=[
            pl.BlockSpec((8, D_MODEL), lambda l, j: (0, 0)),
            pl.BlockSpec((None, D_MODEL, tn), lambda l, j: (l, 0, j)),
            pl.BlockSpec((None, 1, tn), lambda l, j: (l, 0, j)),
        ],
        out_specs=pl.BlockSpec((None, 8, tn), lambda l, j: (l, 0, j)),
        out_shape=jax.ShapeDtypeStruct((DEPTH, 8, N_MOD * D_MODEL), F32),
        compiler_params=_cparams(2),
        name="modulation",
    )(cond8, ada_w, ada_b.reshape(DEPTH, 1, N_MOD * D_MODEL))


def _half_swap(x, half):
    n = x.shape[1]
    lane = lax.broadcasted_iota(jnp.int32, (1, n), 1)
    return jnp.where((lane & half) == 0, pltpu.roll(x, n - half, 1), pltpu.roll(x, half, 1))


def _mla_expand(rows, cq, ckv, kr, cos32, sin32, wqa_ref, wk_ref, e_ref, wv_ref,
                q_ref, k_ref, v_ref):
    if q_ref is not None:
        lane = lax.broadcasted_iota(jnp.int32, (1, LANES), 1)
        rope_lane = jnp.logical_and(lane >= MLA_NOPE, lane < MLA_NOPE + MLA_ROPE)
        cos_h = jnp.where(rope_lane, cos32, 1.0)
        sin_h = jnp.where(rope_lane, sin32, 0.0)
        for hd in range(MLA_HEADS):
            lo, hi = hd * LANES, (hd + 1) * LANES
            q = _dot(cq, wqa_ref[:, lo:hi])
            q = q * cos_h + _half_swap(q, MLA_ROPE // 4) * sin_h
            q_ref[rows, lo:hi] = (q * (MLA_SCALE * LOG2E)).astype(BF16)
    k_ref[rows, :] = (_dot(ckv, wk_ref[...]) + _dot(kr, e_ref[...])).astype(BF16)
    v_ref[rows, :] = _dot(ckv, wv_ref[...]).astype(BF16)


def _chunks():
    return [pl.ds(r * TM, TM) for r in range(TB // TM)]


def _stage_a_kernel(xc_ref, xl_ref, mod_ref, g_ref, win_ref, wg_ref, bg_ref, qn_ref, kvn_ref, tab_ref,
                    wqa_ref, wk_ref, e_ref, wv_ref,
                    fin_ref, ckv_ref, kr_ref, sq_ref, sk_ref, sv_ref, gates_ref,
                    qm_ref, km_ref, vm_ref):
    is_ctx = pl.program_id(0) < NB_CTX
    for rows in _chunks():
        x = jnp.where(is_ctx, xc_ref[rows, :], xl_ref[rows, :])
        h = (_rms_rows(x, g_ref[...]) * (1.0 + mod_ref[:, 1024:2048]) + mod_ref[:, 0:1024]).astype(BF16)

        def proj(seg):
            return _dot(h, win_ref[:, seg[0]:seg[1]])

        fin_ref[rows, :] = proj(A_F).astype(BF16)
        cq = _rms_rows(proj(A_QD), qn_ref[...]).astype(BF16)
        ckv = _rms_rows(proj(A_KV), kvn_ref[...])
        ckv_ref[rows, :] = ckv
        cos64 = tab_ref[rows, 0:512]
        sin64 = tab_ref[rows, 512:1024]
        sq = proj(A_SQ)
        sq = sq * cos64 + _half_swap(sq, SWA_HEAD_DIM // 4) * sin64
        sq_ref[rows, :] = (sq * (SWA_SCALE * LOG2E)).astype(BF16)
        sk = proj(A_SK)
        sk_ref[rows, :] = sk * cos64[:, 0:128] + _half_swap(sk, SWA_HEAD_DIM // 4) * sin64[:, 0:128]
        sv_ref[rows, :] = proj(A_SV)
        cos32 = tab_ref[rows, 1024:1152]
        sin32 = tab_ref[rows, 1152:1280]
        kr = proj(A_KR)
        kr = kr * cos32 + _half_swap(kr, MLA_ROPE // 4) * sin32
        kr_ref[rows, :] = kr
        _mla_expand(rows, cq, ckv.astype(BF16), kr.astype(BF16), cos32, sin32,
                    wqa_ref, wk_ref, e_ref, wv_ref, qm_ref, km_ref, vm_ref)
        for c in range(3):
            lo, hi = c * D_MODEL, (c + 1) * D_MODEL
            gates_ref[rows, lo:hi] = jax.nn.sigmoid(_dot(h, wg_ref[:, lo:hi]) + bg_ref[:, lo:hi]).astype(BF16)


def _stage_a(layer, xc, xl, modt, g0, w_in_wide, w_gate, b_gate, q_norm, kv_norm, tab, wqa, wk, e_mat, wv):
    row = lambda w: pl.BlockSpec((TB, w), lambda i: (i, 0))
    outs = [(512, BF16), (128, F32), (128, F32), (512, BF16), (128, F32), (128, F32),
            (3 * D_MODEL, BF16), (1024, BF16), (1024, BF16), (512, BF16)]
    return pl.pallas_call(
        _stage_a_kernel,
        grid=(NB,),
        in_specs=[
            _ctx_rows(D_MODEL), _lat_rows(D_MODEL),
            pl.BlockSpec((None, 1, N_MOD * D_MODEL), lambda i: (i, 0, 0)),
            _const_spec((1, D_MODEL)),
            _const_spec((D_MODEL, W_IN_WIDE)),
            _layer_spec((D_MODEL, 3 * D_MODEL), layer),
            _const_spec((1, 3 * D_MODEL)),
            _const_spec((1, MLA_Q_RANK)),
            _const_spec((1, MLA_KV_RANK)),
            pl.BlockSpec((TB, TAB_W), lambda i: (_tab_row_block(i), 0)),
            _const_spec((MLA_Q_RANK, 1024)),
            _const_spec((128, 1024)), _const_spec((128, 1024)), _const_spec((128, 512)),
        ],
        out_specs=[row(w) for w, _ in outs],
        out_shape=[jax.ShapeDtypeStruct((N_TOK, w), dt) for w, dt in outs],
        compiler_params=_cparams(1),
        name="stage_a",
    )(xc, xl, modt, g0, w_in_wide, w_gate, b_gate, q_norm, kv_norm, tab, wqa, wk, e_mat, wv)


def _fnet_kernel(t_len, scale, fin_ref, f_ref, bd_ref, o_ref, zz_ref):
    @pl.when(pl.program_id(1) == 0)
    def _():
        z = fin_ref[...]
        zz_ref[0:t_len, :] = _dot(z, bd_ref[:, 0:512]).astype(BF16)
        zz_ref[t_len:2 * t_len, :] = _dot(z, bd_ref[:, 512:1024]).astype(BF16)

    o_ref[...] = (_dot(f_ref[...], zz_ref[...]) * scale).astype(BF16)


def _fnet(fin, fmat, bd, n_batch, t_len, row_block0):
    scale = 1.0 / math.sqrt(t_len * FNET_GROUP_DIM)
    ft = min(t_len, FNET_ROWS)
    return pl.pallas_call(
        functools.partial(_fnet_kernel, t_len, scale),
        grid=(n_batch, t_len // ft),
        in_specs=[
            pl.BlockSpec((t_len, FNET_WIDTH), lambda b, i: (row_block0 + b, 0)),
            pl.BlockSpec((ft, 2 * t_len), lambda b, i: (i, 0)),
            _const_spec((FNET_WIDTH, 2 * FNET_WIDTH)),
        ],
        out_specs=pl.BlockSpec((ft, FNET_WIDTH), lambda b, i: (b * (t_len // ft) + i, 0)),
        out_shape=jax.ShapeDtypeStruct((n_batch * t_len, FNET_WIDTH), BF16),
        scratch_shapes=[pltpu.VMEM((2 * t_len, FNET_WIDTH), BF16)],
        compiler_params=_cparams(2),
        name=f"fnet_{t_len}",
    )(fin, fmat, bd)


def _mla_cache_kernel(ckv_ref, kr_ref, wk_ref, e_ref, wv_ref, k_ref, v_ref):
    _mla_expand(slice(None), None, ckv_ref[...].astype(BF16), kr_ref[...].astype(BF16), None, None,
                None, wk_ref, e_ref, wv_ref, None, k_ref, v_ref)


def _mla_cache_kv(ckv_cache, kr_cache, wk, e_mat, wv):
    row = lambda w: pl.BlockSpec((TM, w), lambda i: (i, 0))
    return pl.pallas_call(
        _mla_cache_kernel,
        grid=(N_CACHE // TM,),
        in_specs=[row(128), row(128),
                  _const_spec((128, 1024)), _const_spec((128, 1024)), _const_spec((128, 512))],
        out_specs=[row(1024), row(512)],
        out_shape=[jax.ShapeDtypeStruct((N_CACHE, 1024), BF16),
                   jax.ShapeDtypeStruct((N_CACHE, 512), BF16)],
        compiler_params=_cparams(1),
        name="mla_cache_kv",
    )(ckv_cache, kr_cache, wk, e_mat, wv)


def _mla_attn_kernel(n_seg, pairs, q_ref, *refs):
    k_refs = refs[0:n_seg]
    v_refs = refs[n_seg:2 * n_seg]
    o_ref = refs[2 * n_seg]
    lane = lax.broadcasted_iota(jnp.int32, (1, LANES), 1)
    low = lane < MLA_V
    for pr in range(pairs):
        outs = []
        for hh in range(2):
            hd = 2 * pr + hh
            q = q_ref[:, hd * LANES:(hd + 1) * LANES]
            ss = [_dot_nt(q, k[:, hd * LANES:(hd + 1) * LANES]) for k in k_refs]
            m = functools.reduce(jnp.maximum, [s.max(axis=-1, keepdims=True) for s in ss])
            keep = low if hh == 0 else jnp.logical_not(low)
            sum_lane = MLA_V if hh == 0 else 0
            po = None
            for s, v_ref in zip(ss, v_refs):
                v = v_ref[:, pr * LANES:(pr + 1) * LANES]
                vm = jnp.where(lane == sum_lane, jnp.ones_like(v), jnp.where(keep, v, jnp.zeros_like(v)))
                t = _dot(jnp.exp2(s - m).astype(BF16), vm)
                po = t if po is None else po + t
            outs.append(po / po[:, sum_lane:sum_lane + 1])
        o_ref[:, pr * LANES:(pr + 1) * LANES] = jnp.where(low, outs[0], outs[1]).astype(BF16)


def _mla_attn(q_all, k_all, v_all, k_cache, v_cache, latent):
    if latent:
        tq, pairs = MLA_LAT_TQ, MLA_LAT_PAIRS
        n_b, n_q = DEC_BATCH, DEC_SEQ // tq
        q0 = N_CTX // tq
        kv_specs = [
            pl.BlockSpec((PAST_LEN, 256 * pairs), lambda b, hp, i: (b, hp)),
            pl.BlockSpec((DEC_SEQ, 256 * pairs), lambda b, hp, i: (N_CTX // DEC_SEQ + b, hp)),
            pl.BlockSpec((PAST_LEN, 128 * pairs), lambda b, hp, i: (b, hp)),
            pl.BlockSpec((DEC_SEQ, 128 * pairs), lambda b, hp, i: (N_CTX // DEC_SEQ + b, hp)),
        ]
        args = (q_all, k_cache, k_all, v_cache, v_all)
        n_seg = 2
    else:
        tq, pairs = SEQ, MLA_HEADS // 2
        n_b, n_q = BATCH, 1
        q0 = 0
        kv_specs = [
            pl.BlockSpec((SEQ, 256 * pairs), lambda b, hp, i: (b, hp)),
            pl.BlockSpec((SEQ, 128 * pairs), lambda b, hp, i: (b, hp)),
        ]
        args = (q_all, k_all, v_all)
        n_seg = 1
    return pl.pallas_call(
        functools.partial(_mla_attn_kernel, n_seg, pairs),
        grid=(n_b, MLA_HEADS // (2 * pairs), n_q),
        in_specs=[pl.BlockSpec((tq, 256 * pairs), lambda b, hp, i: (q0 + b * n_q + i, hp))] + kv_specs,
        out_specs=pl.BlockSpec((tq, 128 * pairs), lambda b, hp, i: (b * n_q + i, hp)),
        out_shape=jax.ShapeDtypeStruct((n_b * n_q * tq, MLA_HEADS * MLA_V), BF16),
        compiler_params=_cparams(3),
        name="mla_attn_lat" if latent else "mla_attn_ctx",
    )(*args)


def _swa_kernel(windowed, n_qb, sink_ref, q_ref, *refs):
    n_seg = 4 if windowed else 1
    k_refs = refs[0:n_seg]
    v_refs = refs[n_seg:2 * n_seg]
    o_ref = refs[2 * n_seg]
    tq = q_ref.shape[0]
    qb = pl.program_id(1)
    lane = lax.broadcasted_iota(jnp.int32, (1, LANES), 1)
    low = lane < SWA_HEAD_DIM
    high = jnp.logical_not(low)

    k_all = jnp.concatenate([r[...] for r in k_refs], axis=0)
    v_all = jnp.concatenate([r[...] for r in v_refs], axis=0)
    k_sw = pltpu.roll(k_all, SWA_HEAD_DIM, 1)
    v_sw = pltpu.roll(v_all, SWA_HEAD_DIM, 1)

    if windowed:
        qi = lax.broadcasted_iota(jnp.int32, (2 * tq, SWA_WINDOW), 0) % tq
        kj = lax.broadcasted_iota(jnp.int32, (2 * tq, SWA_WINDOW), 1)
        bias_prev = jnp.where(jnp.logical_and(kj >= qi, qb > 0), 0.0, NEG_INF)
        bias_next = jnp.where(jnp.logical_and(kj <= qi, qb < n_qb - 1), 0.0, NEG_INF)
    top_rows = lax.broadcasted_iota(jnp.int32, (2 * tq, 1), 0) < tq

    for g in range(SWA_KV_HEADS):
        qs = jnp.concatenate([q_ref[:, 256 * g:256 * g + 128],
                              q_ref[:, 256 * g + 128:256 * g + 256]], axis=0)
        halves = []
        for half in range(2):
            keep = low if half == 0 else high
            sum_lane = SWA_HEAD_DIM if half == 0 else 0
            straight = (g == half)
            kh = jnp.where(keep, k_all if straight else k_sw, 0.0).astype(BF16)
            vh = jnp.where(lane == sum_lane, 1.0,
                           jnp.where(keep, v_all if straight else v_sw, 0.0)).astype(BF16)
            s = _dot_nt(qs, kh)
            if windowed:
                c0, c1, c2 = PAST_LEN, PAST_LEN + SWA_WINDOW, PAST_LEN + 2 * SWA_WINDOW
                s = jnp.concatenate([s[:, :c0], s[:, c0:c1] + bias_prev, s[:, c1:c2],
                                     s[:, c2:] + bias_next], axis=1)
            sink = jnp.where(top_rows, sink_ref[4 * g + half], sink_ref[4 * g + 2 + half]) * LOG2E
            m = jnp.maximum(s.max(axis=-1, keepdims=True), sink)
            po = _dot(jnp.exp2(s - m).astype(BF16), vh)
            halves.append(po / (po[:, sum_lane:sum_lane + 1] + jnp.exp2(sink - m)))
        out = jnp.where(low, halves[0], halves[1])
        o_ref[:, 256 * g:256 * g + 128] = out[0:tq].astype(BF16)
        o_ref[:, 256 * g + 128:256 * g + 256] = out[tq:2 * tq].astype(BF16)


def _swa_attn(sink, sq, sk, sv, cache_k, cache_v, latent):
    smem = pl.BlockSpec(memory_space=pltpu.SMEM)
    if latent:
        tq = SWA_WINDOW
        n_b, n_qb = DEC_BATCH, DEC_SEQ // tq
        base = N_CTX // tq

        def prev(b, i):
            return (base + b * n_qb + jnp.maximum(i - 1, 0), 0)

        def cur(b, i):
            return (base + b * n_qb + i, 0)

        def nxt(b, i):
            return (base + b * n_qb + jnp.minimum(i + 1, n_qb - 1), 0)

        cache = pl.BlockSpec((None, PAST_LEN, 128), lambda b, i: (b, 0, 0))
        blk = lambda f: pl.BlockSpec((tq, 128), f)
        kv_specs = [cache, blk(prev), blk(cur), blk(nxt)] * 2
        args = (cache_k, sk, sk, sk, cache_v, sv, sv, sv)
        q_spec = pl.BlockSpec((tq, 512), cur)
        o_spec = pl.BlockSpec((tq, 512), lambda b, i: (b * n_qb + i, 0))
    else:
        tq = SEQ
        n_b, n_qb = BATCH, 1
        blk = pl.BlockSpec((tq, 128), lambda b, i: (b, 0))
        kv_specs = [blk, blk]
        args = (sk, sv)
        q_spec = pl.BlockSpec((tq, 512), lambda b, i: (b, 0))
        o_spec = q_spec
    return pl.pallas_call(
        functools.partial(_swa_kernel, latent, n_qb),
        grid=(n_b, n_qb),
        in_specs=[smem, q_spec] + kv_specs,
        out_specs=o_spec,
        out_shape=jax.ShapeDtypeStruct((n_b * n_qb * tq, 512), BF16),
        compiler_params=_cparams(2),
        name="swa_lat" if latent else "swa_ctx",
    )(sink, sq, *args)


def _route(h, rwt_ref, rb_ref, tri_ref, carry):
    gsz = N_EXPERTS // N_EXPERT_GROUPS
    scores = jax.nn.sigmoid(_dot_nt(rwt_ref[...], h))
    biased = scores + rb_ref[...]
    mem = lax.broadcasted_iota(jnp.int32, (gsz, TM), 0).astype(F32)
    gs_rows = []
    for g in range(N_EXPERT_GROUPS):
        bg = biased[g * gsz:(g + 1) * gsz, :]
        m1 = bg.max(axis=0, keepdims=True)
        first = jnp.min(jnp.where(bg == m1, mem, float(gsz)), axis=0, keepdims=True)
        m2 = jnp.where(mem == first, -jnp.inf, bg).max(axis=0, keepdims=True)
        gs_rows.append(m1 + m2)
    gs = jnp.concatenate(gs_rows, axis=0)
    gid = lax.broadcasted_iota(jnp.int32, gs.shape, 0).astype(F32)
    gsel = jnp.zeros(gs.shape, F32)
    for _ in range(TOPK_GROUPS):
        mx = gs.max(axis=0, keepdims=True)
        pick = gid == jnp.min(jnp.where(gs == mx, gid, float(N_EXPERT_GROUPS)), axis=0, keepdims=True)
        gsel = jnp.where(pick, 1.0, gsel)
        gs = jnp.where(pick, -jnp.inf, gs)
    emask = jnp.concatenate(
        [jnp.broadcast_to(gsel[g:g + 1, :], (gsz, TM)) for g in range(N_EXPERT_GROUPS)], axis=0)
    cand = jnp.where(emask > 0.5, biased, NEG_INF)
    eid = lax.broadcasted_iota(jnp.int32, cand.shape, 0).astype(F32)
    picks = []
    self32 = jnp.zeros(cand.shape, F32)
    for _ in range(TOP_K):
        mx = cand.max(axis=0, keepdims=True)
        pick = eid == jnp.min(jnp.where(cand == mx, eid, float(N_EXPERTS)), axis=0, keepdims=True)
        picks.append(pick)
        self32 = jnp.where(pick, 1.0, self32)
        cand = jnp.where(pick, -jnp.inf, cand)
    pos = _dot(self32.astype(BF16), tri_ref[...]) + carry
    sel_scores = [jnp.sum(jnp.where(p, scores, 0.0), axis=0, keepdims=True) for p in picks]
    wsum = functools.reduce(lambda a, b: a + b, sel_scores)
    zero_f = jnp.zeros((2, TM), F32)
    eidx = [jnp.sum(jnp.where(p, eid, 0.0), axis=0, keepdims=True) for p in picks]
    epos = [jnp.sum(jnp.where(p, pos, 0.0), axis=0, keepdims=True) for p in picks]
    ew = [s / wsum * ROUTED_SCALE for s in sel_scores]
    return (jnp.concatenate(eidx + [zero_f], axis=0).astype(jnp.int32),
            jnp.concatenate(epos + [zero_f], axis=0).astype(jnp.int32),
            jnp.concatenate(ew + [zero_f], axis=0),
            carry + jnp.sum(self32, axis=1, keepdims=True))


def _stage_e_kernel(xc_ref, xl_ref, mod_ref, g1_ref, g2_ref, fnc_ref, fnl_ref, omc_ref, oml_ref, osc_ref, osl_ref,
                    gates_ref, wf_ref, wm_ref, ws_ref, wo_ref, rwt_ref, rb_ref, tri_ref,
                    x1_ref, h2a_ref, h2b_ref, eidx_ref, epos_ref, ew_ref, cnt_ref, carry_ref):
    @pl.when(pl.program_id(0) == 0)
    def _():
        carry_ref[...] = jnp.zeros_like(carry_ref)

    is_ctx = pl.program_id(0) < NB_CTX
    carry = carry_ref[...]
    for r, rows in enumerate(_chunks()):
        fn = jnp.where(is_ctx, fnc_ref[rows, :], fnl_ref[rows, :])
        om = jnp.where(is_ctx, omc_ref[rows, :], oml_ref[rows, :])
        osw = jnp.where(is_ctx, osc_ref[rows, :], osl_ref[rows, :])
        merged = (gates_ref[rows, 0:1024].astype(F32) * _dot(fn, wf_ref[...])
                  + gates_ref[rows, 1024:2048].astype(F32) * _dot(om, wm_ref[...])
                  + gates_ref[rows, 2048:3072].astype(F32) * _dot(osw, ws_ref[...]))
        mix = _dot(merged.astype(BF16), wo_ref[...])
        x = jnp.where(is_ctx, xc_ref[rows, :], xl_ref[rows, :])
        x1 = x + mod_ref[:, 2048:3072] * _rms_rows(mix, g1_ref[...])
        x1_ref[rows, :] = x1
        h2 = _rms_rows(x1, g2_ref[...]) * (1.0 + mod_ref[:, 4096:5120]) + mod_ref[:, 3072:4096]
        h2a_ref[rows, :], h2b_ref[rows, :] = _pack_pair(h2)
        cols = pl.ds(r * TM, TM)
        eidx_ref[:, cols], epos_ref[:, cols], ew_ref[:, cols], carry = _route(
            h2.astype(BF16), rwt_ref, rb_ref, tri_ref, carry)
    carry_ref[...] = carry
    cnt_ref[...] = jnp.broadcast_to(carry, cnt_ref.shape).astype(jnp.int32)


def _stage_e(layer, xc, xl, modt, g1, g2, mixed, gates, wf, wm, ws, wo, rwt, rbias, tri):
    row = lambda w: pl.BlockSpec((TB, w), lambda i: (i, 0))
    ctx, lat = _ctx_rows(512), _lat_rows(512)
    col = lambda dt: (pl.BlockSpec((8, TB), lambda i: (0, i)), jax.ShapeDtypeStruct((8, N_TOK), dt))
    picks = [col(jnp.int32), col(jnp.int32), col(F32)]
    return pl.pallas_call(
        _stage_e_kernel,
        grid=(NB,),
        in_specs=[
            _ctx_rows(D_MODEL), _lat_rows(D_MODEL),
            pl.BlockSpec((None, 1, N_MOD * D_MODEL), lambda i: (i, 0, 0)),
            _const_spec((1, D_MODEL)), _const_spec((1, D_MODEL)),
            ctx, lat, ctx, lat, ctx, lat, row(3 * D_MODEL),
            _layer_spec((512, D_MODEL), layer), _layer_spec((512, D_MODEL), layer),
            _layer_spec((512, D_MODEL), layer), _layer_spec((D_MODEL, D_MODEL), layer),
            _const_spec((N_EXPERTS, D_MODEL)), _const_spec((N_EXPERTS, 1)), _const_spec((TM, TM)),
        ],
        out_specs=[row(D_MODEL), row(PACKED), row(PACKED)] + [s for s, _ in picks]
        + [_const_spec((N_EXPERTS, LANES))],
        out_shape=[jax.ShapeDtypeStruct((N_TOK, D_MODEL), F32),
                   jax.ShapeDtypeStruct((N_TOK, PACKED), jnp.int32),
                   jax.ShapeDtypeStruct((N_TOK, PACKED), jnp.int32)] + [s for _, s in picks]
        + [jax.ShapeDtypeStruct((N_EXPERTS, LANES), jnp.int32)],
        scratch_shapes=[pltpu.VMEM((N_EXPERTS, 1), F32)],
        compiler_params=_cparams(1),
        name="stage_e",
    )(xc, xl, modt, g1, g2, *mixed, gates, wf, wm, ws, wo, rwt, rbias, tri)


def _expert_kernel(te_ref, tv_ref, par_ref, xa_ref, xb_ref, w1_ref, w3_ref, w2_ref, oa_ref, ob_ref,
                   w1b_ref, w3b_ref, w2b_ref):
    s = pl.program_id(0)
    t = jnp.maximum(s - 1, 0)
    prev = jnp.maximum(s - 2, 0)

    starts_expert = jnp.logical_or(s == 0, te_ref[s] != te_ref[t])

    @pl.when(jnp.logical_and(tv_ref[s] > 0, starts_expert))
    def _():
        w1b_ref[par_ref[s]] = w1_ref[...].astype(BF16)
        w3b_ref[par_ref[s]] = w3_ref[...].astype(BF16)

    live = jnp.logical_and(s > 0, tv_ref[t] > 0)

    @pl.when(jnp.logical_and(live, jnp.logical_or(s == 1, te_ref[t] != te_ref[prev])))
    def _():
        w2b_ref[...] = w2_ref[...].astype(BF16)

    def run(n_chunks):
        p = par_ref[t]
        for r in range(n_chunks):
            rows = pl.ds(r * EXPERT_ROWS, EXPERT_ROWS)
            x = _unpack_pair(xa_ref[rows, :], xb_ref[rows, :]).astype(BF16)
            hg = _dot(x, w1b_ref[p])
            hu = _dot(x, w3b_ref[p])
            act = (jax.nn.silu(hg) * hu).astype(BF16)
            oa_ref[rows, :], ob_ref[rows, :] = _pack_pair(_dot(act, w2b_ref[...]))

    for n_chunks in range(1, TE // EXPERT_ROWS + 1):
        pl.when(jnp.logical_and(live, tv_ref[t] == n_chunks))(functools.partial(run, n_chunks))


def _experts(layer, tile_expert, tile_chunks, tile_parity, xsa, xsb, w1, w3, w2):
    slot_rows = pl.BlockSpec((TE, PACKED), lambda s, te, tv, par: (jnp.maximum(s - 1, 0), 0))
    ahead = lambda s, te, tv, par: (layer, te[s], 0, 0)
    current = lambda s, te, tv, par: (layer, te[jnp.maximum(s - 1, 0)], 0, 0)
    grid_spec = pltpu.PrefetchScalarGridSpec(
        num_scalar_prefetch=3,
        grid=(NTE + 1,),
        in_specs=[
            slot_rows, slot_rows,
            pl.BlockSpec((None, None, D_MODEL, EXPERT_FF), ahead),
            pl.BlockSpec((None, None, D_MODEL, EXPERT_FF), ahead),
            pl.BlockSpec((None, None, EXPERT_FF, D_MODEL), current),
        ],
        out_specs=[slot_rows, slot_rows],
        scratch_shapes=[pltpu.VMEM((2, D_MODEL, EXPERT_FF), BF16), pltpu.VMEM((2, D_MODEL, EXPERT_FF), BF16),
                        pltpu.VMEM((EXPERT_FF, D_MODEL), BF16)],
    )
    return pl.pallas_call(
        _expert_kernel,
        grid_spec=grid_spec,
        out_shape=[jax.ShapeDtypeStruct((S_MAX, PACKED), jnp.int32)] * 2,
        compiler_params=_cparams(1),
        name="experts",
    )(tile_expert, tile_chunks, tile_parity, xsa, xsb, w1, w3, w2)


def _sc_mesh():
    return plsc.VectorSubcoreMesh(core_axis_name="c", subcore_axis_name="s",
                                  num_cores=SC_CORES, num_subcores=SC_SUBCORES)


def _sc_scatter_rows(rows, slot8):
    @functools.partial(pl.kernel, mesh=_sc_mesh(), scratch_types=[pltpu.SemaphoreType.DMA],
                       out_type=jax.ShapeDtypeStruct((S_MAX, PACKED), jnp.int32))
    def scatter(x_hbm, i_hbm, o_hbm, sem):
        def body(x_vmem, i_vmem):
            copies = [pltpu.async_copy(x_vmem, o_hbm.at[i_vmem.at[k]], sem) for k in range(TOP_K)]
            for cp in copies:
                cp.wait()

        pltpu.emit_pipeline(
            body,
            grid=(N_TOK // SC_ROWS,),
            in_specs=[pl.BlockSpec((SC_ROWS, PACKED), lambda i: (i, 0)),
                      pl.BlockSpec((8, SC_ROWS), lambda i: (0, i))],
            out_specs=[],
            core_axis_name=("c", "s"),
            dimension_semantics=(pltpu.PARALLEL,),
        )(x_hbm, i_hbm)

    return scatter(rows, slot8)


def _sc_gather_rows(table, idx):
    n = idx.shape[1]

    @functools.partial(pl.kernel, mesh=_sc_mesh(), scratch_types=[],
                       out_type=jax.ShapeDtypeStruct((n, PACKED), jnp.int32))
    def gather(t_hbm, i_hbm, o_hbm):
        def body(i_vmem, o_vmem):
            pltpu.sync_copy(t_hbm.at[i_vmem.at[0]], o_vmem)

        pltpu.emit_pipeline(
            body,
            grid=(n // SC_ROWS,),
            in_specs=[pl.BlockSpec((1, SC_ROWS), lambda i: (0, i))],
            out_specs=[pl.BlockSpec((SC_ROWS, PACKED), lambda i: (i, 0))],
            core_axis_name=("c", "s"),
            dimension_semantics=(pltpu.PARALLEL,),
        )(i_hbm, o_hbm)

    return gather(table, idx)


def _stage_g_kernel(x1_ref, mod_ref, g3_ref, yga_ref, ygb_ref, ew_ref, ha_ref, hb_ref,
                    s1_ref, s3_ref, s2_ref, oc_ref, ol_ref):
    is_ctx = pl.program_id(0) < NB_CTX
    for rows in _chunks():
        h = _unpack_pair(ha_ref[rows, :], hb_ref[rows, :]).astype(BF16)
        act = jax.nn.silu(_dot(h, s1_ref[...])) * _dot(h, s3_ref[...])
        y = _dot(act.astype(BF16), s2_ref[...])
        for k in range(TOP_K):
            y = y + ew_ref[rows, k:k + 1] * _unpack_pair(yga_ref[k, rows, :], ygb_ref[k, rows, :])
        out = x1_ref[rows, :] + mod_ref[:, 5120:6144] * _rms_rows(y, g3_ref[...])

        @pl.when(is_ctx)
        def _():
            oc_ref[rows, :] = out

        @pl.when(jnp.logical_not(is_ctx))
        def _():
            ol_ref[rows, :] = out


def _stage_g(layer, x1, modt, g3, yga, ygb, ew_rows, h2a, h2b, s1, s3, s2):
    row = lambda w: pl.BlockSpec((TB, w), lambda i: (i, 0))
    picked = pl.BlockSpec((TOP_K, TB, PACKED), lambda i: (0, i, 0))
    return pl.pallas_call(
        _stage_g_kernel,
        grid=(NB,),
        in_specs=[row(D_MODEL), pl.BlockSpec((None, 1, N_MOD * D_MODEL), lambda i: (i, 0, 0)),
                  _const_spec((1, D_MODEL)), picked, picked, row(8), row(PACKED), row(PACKED),
                  _layer_spec((D_MODEL, SHARED_FF), layer), _layer_spec((D_MODEL, SHARED_FF), layer),
                  _layer_spec((SHARED_FF, D_MODEL), layer)],
        out_specs=[_ctx_rows(D_MODEL), _lat_rows(D_MODEL)],
        out_shape=[jax.ShapeDtypeStruct((N_CTX, D_MODEL), F32),
                   jax.ShapeDtypeStruct((N_LAT, D_MODEL), F32)],
        compiler_params=_cparams(1),
        name="stage_g",
    )(x1, modt, g3, yga, ygb, ew_rows, h2a, h2b, s1, s3, s2)


def _rope_tables():
    t = np.arange(DEC_SEQ)
    pos = np.stack([(t // GRID_W), (t % GRID_W)], axis=-1).astype(np.float32)

    def table(r):
        n_freq = r // 4
        inv = np.float32(ROPE_BASE) ** (-np.arange(n_freq, dtype=np.float32) / np.float32(n_freq))
        ang = pos[:, :, None] * inv.astype(np.float32)
        cos = np.cos(ang)
        sin = np.sin(ang)
        cos_t = np.stack([cos, cos], axis=2).reshape(DEC_SEQ, r)
        sin_t = np.stack([-sin, sin], axis=2).reshape(DEC_SEQ, r)
        return cos_t, sin_t

    c64, s64 = table(SWA_HEAD_DIM)
    c32, s32 = table(MLA_ROPE)
    lat = np.concatenate([np.tile(c64, (1, 8)), np.tile(s64, (1, 8)),
                          np.tile(c32, (1, 4)), np.tile(s32, (1, 4))], axis=1)
    ident = np.concatenate([np.ones((TB, 512)), np.zeros((TB, 512)),
                            np.ones((TB, 128)), np.zeros((TB, 128))], axis=1)
    return jnp.asarray(np.concatenate([ident, lat], axis=0).astype(np.float32))


def _dft_pair(n):
    k = np.arange(n, dtype=np.int64)
    ang = ((k[:, None] * k[None, :]) % n).astype(np.float64) * (2.0 * math.pi / n)
    return np.cos(ang), np.sin(ang)


def _fnet_tables():
    c64, s64 = _dft_pair(FNET_GROUP_DIM)
    eye = np.eye(FNET_GROUPS)
    bd = np.concatenate([np.kron(eye, c64), np.kron(eye, s64)], axis=1)
    mats = []
    for t_len in (SEQ, DEC_SEQ):
        c, s = _dft_pair(t_len)
        mats.append(np.concatenate([c, -s], axis=1))
    return tuple(jnp.asarray(m.astype(np.float32).astype(BF16)) for m in (bd, mats[0], mats[1]))


def _layer_weights(l, w_in, w_uq, w_ukv):
    w = w_in[l]
    wide = jnp.concatenate([w[:, 0:1024], w[:, 1056:1824], w[:, 1024:1056],
                            jnp.zeros((D_MODEL, 96), F32)], axis=1).astype(BF16)

    uq = w_uq[l].reshape(MLA_Q_RANK, MLA_HEADS, MLA_NOPE + MLA_ROPE)
    z32 = jnp.zeros((MLA_Q_RANK, MLA_HEADS, 32), F32)
    wqa = jnp.concatenate([uq, z32], axis=2).reshape(MLA_Q_RANK, 1024).astype(BF16)
    ukv = w_ukv[l].reshape(MLA_KV_RANK, MLA_HEADS, MLA_NOPE + MLA_V)
    wk = jnp.concatenate([ukv[:, :, :MLA_NOPE], jnp.zeros((MLA_KV_RANK, MLA_HEADS, 64), F32)],
                         axis=2).reshape(MLA_KV_RANK, 1024).astype(BF16)
    wv = ukv[:, :, MLA_NOPE:].reshape(MLA_KV_RANK, 512).astype(BF16)
    return wide, wqa, wk, wv


def _rope_placement():
    e = np.zeros((128, 1024), np.float32)
    for hd in range(MLA_HEADS):
        for i in range(MLA_ROPE):
            e[i, hd * 128 + MLA_NOPE + i] = 1.0
    return jnp.asarray(e, BF16)


def _moe_dispatch_plan(eidx, epos, counts):
    padded = ((counts + TE - 1) // TE) * TE
    ends = jnp.cumsum(padded)
    offs = ends - padded
    ids = jnp.arange(N_EXPERTS, dtype=jnp.int32)
    picked_off = jnp.sum(jnp.where(eidx[:, :, None] == ids, offs, 0), axis=-1)
    slot = picked_off + epos
    starts = jnp.arange(NTE, dtype=jnp.int32) * TE
    tile_expert = jnp.sum((ends[None, :] <= starts[:, None]).astype(jnp.int32), axis=1)
    tile_expert = jnp.minimum(tile_expert, N_EXPERTS - 1)
    pick = tile_expert[:, None] == ids[None, :]
    last_real = jnp.sum(jnp.where(pick, (offs + counts)[None, :], 0), axis=1)
    n_real = jnp.clip(last_real - starts, 0, TE)
    n_real = jnp.where(starts < ends[-1], n_real, 0)
    tile_chunks = ((n_real + EXPERT_ROWS - 1) // EXPERT_ROWS).astype(jnp.int32)
    rank = jnp.cumsum((counts > 0).astype(jnp.int32)) - 1
    tile_parity = jnp.sum(jnp.where(pick, rank[None, :], 0), axis=1) % 2
    close = lambda a, v: jnp.concatenate([a, jnp.full((1,), v, jnp.int32)])
    return (slot, close(tile_expert, N_EXPERTS - 1), close(tile_chunks, 0),
            close(tile_parity.astype(jnp.int32), 0))


def kernel(x_prompt, x_sample, cache_mla_ckv, cache_mla_krope, cache_swa_k, cache_swa_v, c, c_ctx,
           ada_w, ada_b, norm_g, w_in, q_norm, kv_norm, w_fnet, w_uq, w_ukv, w_mla_o, swa_sink,
           w_swa_o, w_gate, b_gate, w_out, router_w, router_bias, exp_w1, exp_w3, exp_w2,
           shared_w1, shared_w3, shared_w2):
    xc = x_prompt.reshape(N_CTX, D_MODEL)
    xl = x_sample.reshape(N_LAT, D_MODEL)

    cond8 = jnp.concatenate([c_ctx[None, :], c, jnp.zeros((3, D_MODEL), F32)], axis=0)
    mod = _modulation(cond8, ada_w, ada_b)
    tile_cond = np.concatenate([np.zeros(NB_CTX, np.int32),
                                1 + np.arange(NB - NB_CTX, dtype=np.int32) // LAT_BLOCKS])

    tab = _rope_tables()
    bd, f_ctx, f_lat = _fnet_tables()
    e_mat = _rope_placement()
    tri = jnp.asarray(np.triu(np.ones((TM, TM), np.float32), 1), BF16)
    w_gate_b, w_fnet_b, w_mla_o_b, w_swa_o_b, w_out_b, sw1_b, sw3_b, sw2_b = (
        w.astype(BF16) for w in (w_gate, w_fnet, w_mla_o, w_swa_o, w_out, shared_w1, shared_w3, shared_w2))

    new_ckv, new_kr, new_k, new_v = [], [], [], []
    for l in range(DEPTH):
        modt = mod[l][tile_cond][:, None, :]
        wide, wqa, wk, wv = _layer_weights(l, w_in, w_uq, w_ukv)
        ng = norm_g[l]

        fin, ckv, kr, sq, sk, sv, gates, q_m, k_m, v_m = _stage_a(
            l, xc, xl, modt, ng[0:1], wide, w_gate_b, b_gate[l][None, :],
            q_norm[l][None, :], kv_norm[l][None, :], tab, wqa, wk, e_mat, wv)

        new_ckv.append(ckv[:N_CTX].reshape(BATCH, SEQ, MLA_KV_RANK))
        new_kr.append(kr[:N_CTX, :MLA_ROPE].reshape(BATCH, SEQ, MLA_ROPE))
        new_k.append(sk[:N_CTX].reshape(BATCH, SEQ, SWA_KV_HEADS, SWA_HEAD_DIM))
        new_v.append(sv[:N_CTX].reshape(BATCH, SEQ, SWA_KV_HEADS, SWA_HEAD_DIM))

        fn = (_fnet(fin, f_ctx, bd, BATCH, SEQ, 0),
              _fnet(fin, f_lat, bd, DEC_BATCH, DEC_SEQ, N_CTX // DEC_SEQ))

        kr_cache = jnp.pad(cache_mla_krope[:, l].reshape(N_CACHE, MLA_ROPE), ((0, 0), (0, 96)))
        k_c, v_c = _mla_cache_kv(cache_mla_ckv[:, l].reshape(N_CACHE, MLA_KV_RANK), kr_cache,
                                 wk, e_mat, wv)
        om = (_mla_attn(q_m, k_m, v_m, k_c, v_c, latent=False),
              _mla_attn(q_m, k_m, v_m, k_c, v_c, latent=True))

        ck = cache_swa_k[:, l].reshape(DEC_BATCH, PAST_LEN, 128)
        cv = cache_swa_v[:, l].reshape(DEC_BATCH, PAST_LEN, 128)
        osw = (_swa_attn(swa_sink[l], sq, sk, sv, ck, cv, latent=False),
               _swa_attn(swa_sink[l], sq, sk, sv, ck, cv, latent=True))

        x1, h2a, h2b, eidx, epos, ew, counts = _stage_e(
            l, xc, xl, modt, ng[1:2], ng[2:3], fn + om + osw, gates,
            w_fnet_b, w_mla_o_b, w_swa_o_b, w_out_b,
            router_w[l].T.astype(BF16), router_bias[l][:, None], tri)
        slot, tile_expert, tile_chunks, tile_parity = _moe_dispatch_plan(eidx, epos, counts[:, 0])
        xsa = _sc_scatter_rows(h2a, slot)
        xsb = _sc_scatter_rows(h2b, slot)
        ysa, ysb = _experts(l, tile_expert, tile_chunks, tile_parity, xsa, xsb, exp_w1, exp_w3, exp_w2)
        picks = slot[:TOP_K].reshape(1, TOP_K * N_TOK)
        yga = _sc_gather_rows(ysa, picks).reshape(TOP_K, N_TOK, PACKED)
        ygb = _sc_gather_rows(ysb, picks).reshape(TOP_K, N_TOK, PACKED)
        xc, xl = _stage_g(l, x1, modt, ng[3:4], yga, ygb, ew.T, h2a, h2b, sw1_b, sw3_b, sw2_b)

    y_p = xc.reshape(BATCH, SEQ, D_MODEL)
    y_s = xl.reshape(DEC_BATCH, DEC_SEQ, D_MODEL)
    return (y_p, y_s, jnp.stack(new_ckv, axis=1), jnp.stack(new_kr, axis=1),
            jnp.stack(new_k, axis=1), jnp.stack(new_v, axis=1))
```

```python
import functools
import math

import numpy as np
import jax
import jax.numpy as jnp
from jax import lax
from jax.experimental import pallas as pl
from jax.experimental.pallas import tpu as pltpu
from jax.experimental.pallas import tpu_sc as plsc

D_MODEL = 1024
BATCH = 16
SEQ = 256
DEPTH = 2
DEC_BATCH = 4
DEC_SEQ = 2048
PAST_LEN = 512
GRID_W = 64
EPS = 1e-6
ROPE_BASE = 10000.0
NEG_INF = -1e30

FNET_GROUPS = 8
FNET_GROUP_DIM = 64
FNET_WIDTH = 512
MLA_HEADS = 8
MLA_Q_RANK = 384
MLA_KV_RANK = 128
MLA_NOPE = 64
MLA_ROPE = 32
MLA_V = 64
MLA_SCALE = (MLA_NOPE + MLA_ROPE) ** -0.5
LOG2E = math.log2(math.e)
SWA_HEADS = 8
SWA_KV_HEADS = 2
SWA_HEAD_DIM = 64
SWA_WINDOW = 128
SWA_SCALE = SWA_HEAD_DIM ** -0.5
N_MOD = 6
N_EXPERTS = 64
N_EXPERT_GROUPS = 8
TOPK_GROUPS = 4
TOP_K = 6
EXPERT_FF = 256
SHARED_FF = 256
ROUTED_SCALE = 2.5

LANES = 128
TM = 256
N_CTX = BATCH * SEQ
N_LAT = DEC_BATCH * DEC_SEQ
N_TOK = N_CTX + N_LAT
N_CACHE = DEC_BATCH * PAST_LEN
NT_CTX = N_CTX // TM
NT_LAT = N_LAT // TM
NT = N_TOK // TM
LAT_TILES = DEC_SEQ // TM
TB = 512
NB = N_TOK // TB
NB_CTX = N_CTX // TB
LAT_BLOCKS = DEC_SEQ // TB
MLA_LAT_TQ = 256
MLA_LAT_PAIRS = 4
FNET_ROWS = 1024
TE = 512
S_MAX = N_TOK * TOP_K + N_EXPERTS * TE
NTE = S_MAX // TE
EXPERT_ROWS = 256
VMEM_LIMIT = 56 * 1024 * 1024
PACKED = D_MODEL // 4
SC_ROWS = 128
SC_CORES = 2
SC_SUBCORES = 16

A_F = (0, 512)
A_QD = (512, 896)
A_KV = (896, 1024)
A_SQ = (1024, 1536)
A_SK = (1536, 1664)
A_SV = (1664, 1792)
A_KR = (1792, 1920)
W_IN_WIDE = 1920
TAB_W = 1280

F32 = jnp.float32
BF16 = jnp.bfloat16


def _cparams(n_axes, parallel=False):
    sem = ("parallel" if parallel else "arbitrary",) * n_axes
    return pltpu.CompilerParams(dimension_semantics=sem, vmem_limit_bytes=VMEM_LIMIT)


def _dot(a, b):
    return jnp.dot(a, b, preferred_element_type=F32)


def _dot_nt(a, b):
    return lax.dot_general(a, b, (((1,), (1,)), ((), ())), preferred_element_type=F32)


def _rms_rows(v, g):
    return v * lax.rsqrt(jnp.mean(v * v, axis=-1, keepdims=True) + EPS) * g


def _pack_rows(v):
    half = v.shape[1] // 2
    lo = lax.bitcast_convert_type(v[:, :half].astype(BF16).astype(F32), jnp.int32)
    hi = lax.bitcast_convert_type(v[:, half:].astype(BF16).astype(F32), jnp.int32)
    return jnp.bitwise_or(jnp.bitwise_and(hi, -65536), jnp.bitwise_and(jnp.right_shift(lo, 16), 65535))


def _unpack_rows(w):
    lo = lax.bitcast_convert_type(jnp.left_shift(w, 16), F32)
    hi = lax.bitcast_convert_type(jnp.bitwise_and(w, -65536), F32)
    return jnp.concatenate([lo, hi], axis=1)


def _pack_pair(v):
    half = v.shape[1] // 2
    return _pack_rows(v[:, :half]), _pack_rows(v[:, half:])


def _unpack_pair(a, b):
    return jnp.concatenate([_unpack_rows(a), _unpack_rows(b)], axis=1)


def _const_spec(shape):
    return pl.BlockSpec(shape, lambda *_: (0,) * len(shape))


def _layer_spec(shape, layer):
    return pl.BlockSpec((None,) + shape, lambda *_: (layer,) + (0,) * len(shape))


def _ctx_rows(width):
    return pl.BlockSpec((TB, width), lambda i: (jnp.minimum(i, NB_CTX - 1), 0))


def _lat_rows(width):
    return pl.BlockSpec((TB, width), lambda i: (jnp.maximum(i - NB_CTX, 0), 0))


def _tab_row_block(i):
    return jnp.where(i < NB_CTX, 0, 1 + (i - NB_CTX) % LAT_BLOCKS)


def _mod_kernel(cond_ref, w_ref, b_ref, o_ref):
    c = cond_ref[...]
    a = (c * jax.nn.sigmoid(c)).astype(BF16)
    o_ref[...] = _dot(a, w_ref[...].astype(BF16)) + b_ref[...]


def _modulation(cond8, ada_w, ada_b):
    tn = 512
    nj = N_MOD * D_MODEL // tn
    return pl.pallas_call(
        _mod_kernel,
        grid=(DEPTH, nj),
        in_specs=[
            pl.BlockSpec((8, D_MODEL), lambda l, j: (0, 0)),
            pl.BlockSpec((None, D_MODEL, tn), lambda l, j: (l, 0, j)),
            pl.BlockSpec((None, 1, tn), lambda l, j: (l, 0, j)),
        ],
        out_specs=pl.BlockSpec((None, 8, tn), lambda l, j: (l, 0, j)),
        out_shape=jax.ShapeDtypeStruct((DEPTH, 8, N_MOD * D_MODEL), F32),
        compiler_params=_cparams(2),
        name="modulation",
    )(cond8, ada_w, ada_b.reshape(DEPTH, 1, N_MOD * D_MODEL))


def _half_swap(x, half):
    n = x.shape[1]
    lane = lax.broadcasted_iota(jnp.int32, (1, n), 1)
    return jnp.where((lane & half) == 0, pltpu.roll(x, n - half, 1), pltpu.roll(x, half, 1))


def _mla_expand(rows, cq, ckv, kr, cos32, sin32, wqa_ref, wk_ref, e_ref, wv_ref,
                q_ref, k_ref, v_ref):
    if q_ref is not None:
        lane = lax.broadcasted_iota(jnp.int32, (1, LANES), 1)
        rope_lane = jnp.logical_and(lane >= MLA_NOPE, lane < MLA_NOPE + MLA_ROPE)
        cos_h = jnp.where(rope_lane, cos32, 1.0)
        sin_h = jnp.where(rope_lane, sin32, 0.0)
        for hd in range(MLA_HEADS):
            lo, hi = hd * LANES, (hd + 1) * LANES
            q = _dot(cq, wqa_ref[:, lo:hi])
            q = q * cos_h + _half_swap(q, MLA_ROPE // 4) * sin_h
            q_ref[rows, lo:hi] = (q * (MLA_SCALE * LOG2E)).astype(BF16)
    k_ref[rows, :] = (_dot(ckv, wk_ref[...]) + _dot(kr, e_ref[...])).astype(BF16)
    v_ref[rows, :] = _dot(ckv, wv_ref[...]).astype(BF16)


def _chunks():
    return [pl.ds(r * TM, TM) for r in range(TB // TM)]


def _stage_a_kernel(xc_ref, xl_ref, mod_ref, g_ref, win_ref, wg_ref, bg_ref, qn_ref, kvn_ref, tab_ref,
                    wqa_ref, wk_ref, e_ref, wv_ref,
                    fin_ref, ckv_ref, kr_ref, sq_ref, sk_ref, sv_ref, gates_ref,
                    qm_ref, km_ref, vm_ref):
    is_ctx = pl.program_id(0) < NB_CTX
    for rows in _chunks():
        x = jnp.where(is_ctx, xc_ref[rows, :], xl_ref[rows, :])
        h = (_rms_rows(x, g_ref[...]) * (1.0 + mod_ref[:, 1024:2048]) + mod_ref[:, 0:1024]).astype(BF16)

        def proj(seg):
            return _dot(h, win_ref[:, seg[0]:seg[1]])

        fin_ref[rows, :] = proj(A_F).astype(BF16)
        cq = _rms_rows(proj(A_QD), qn_ref[...]).astype(BF16)
        ckv = _rms_rows(proj(A_KV), kvn_ref[...])
        ckv_ref[rows, :] = ckv
        cos64 = tab_ref[rows, 0:512]
        sin64 = tab_ref[rows, 512:1024]
        sq = proj(A_SQ)
        sq = sq * cos64 + _half_swap(sq, SWA_HEAD_DIM // 4) * sin64
        sq_ref[rows, :] = (sq * (SWA_SCALE * LOG2E)).astype(BF16)
        sk = proj(A_SK)
        sk_ref[rows, :] = sk * cos64[:, 0:128] + _half_swap(sk, SWA_HEAD_DIM // 4) * sin64[:, 0:128]
        sv_ref[rows, :] = proj(A_SV)
        cos32 = tab_ref[rows, 1024:1152]
        sin32 = tab_ref[rows, 1152:1280]
        kr = proj(A_KR)
        kr = kr * cos32 + _half_swap(kr, MLA_ROPE // 4) * sin32
        kr_ref[rows, :] = kr
        _mla_expand(rows, cq, ckv.astype(BF16), kr.astype(BF16), cos32, sin32,
                    wqa_ref, wk_ref, e_ref, wv_ref, qm_ref, km_ref, vm_ref)
        for c in range(3):
            lo, hi = c * D_MODEL, (c + 1) * D_MODEL
            gates_ref[rows, lo:hi] = jax.nn.sigmoid(_dot(h, wg_ref[:, lo:hi]) + bg_ref[:, lo:hi]).astype(BF16)


def _stage_a(layer, xc, xl, modt, g0, w_in_wide, w_gate, b_gate, q_norm, kv_norm, tab, wqa, wk, e_mat, wv):
    row = lambda w: pl.BlockSpec((TB, w), lambda i: (i, 0))
    outs = [(512, BF16), (128, F32), (128, F32), (512, BF16), (128, F32), (128, F32),
            (3 * D_MODEL, BF16), (1024, BF16), (1024, BF16), (512, BF16)]
    return pl.pallas_call(
        _stage_a_kernel,
        grid=(NB,),
        in_specs=[
            _ctx_rows(D_MODEL), _lat_rows(D_MODEL),
            pl.BlockSpec((None, 1, N_MOD * D_MODEL), lambda i: (i, 0, 0)),
            _const_spec((1, D_MODEL)),
            _const_spec((D_MODEL, W_IN_WIDE)),
            _layer_spec((D_MODEL, 3 * D_MODEL), layer),
            _const_spec((1, 3 * D_MODEL)),
            _const_spec((1, MLA_Q_RANK)),
            _const_spec((1, MLA_KV_RANK)),
            pl.BlockSpec((TB, TAB_W), lambda i: (_tab_row_block(i), 0)),
            _const_spec((MLA_Q_RANK, 1024)),
            _const_spec((128, 1024)), _const_spec((128, 1024)), _const_spec((128, 512)),
        ],
        out_specs=[row(w) for w, _ in outs],
        out_shape=[jax.ShapeDtypeStruct((N_TOK, w), dt) for w, dt in outs],
        compiler_params=_cparams(1),
        name="stage_a",
    )(xc, xl, modt, g0, w_in_wide, w_gate, b_gate, q_norm, kv_norm, tab, wqa, wk, e_mat, wv)


def _fnet_kernel(t_len, scale, fin_ref, f_ref, bd_ref, o_ref, zz_ref):
    @pl.when(pl.program_id(1) == 0)
    def _():
        z = fin_ref[...]
        zz_ref[0:t_len, :] = _dot(z, bd_ref[:, 0:512]).astype(BF16)
        zz_ref[t_len:2 * t_len, :] = _dot(z, bd_ref[:, 512:1024]).astype(BF16)

    o_ref[...] = (_dot(f_ref[...], zz_ref[...]) * scale).astype(BF16)


def _fnet(fin, fmat, bd, n_batch, t_len, row_block0):
    scale = 1.0 / math.sqrt(t_len * FNET_GROUP_DIM)
    ft = min(t_len, FNET_ROWS)
    return pl.pallas_call(
        functools.partial(_fnet_kernel, t_len, scale),
        grid=(n_batch, t_len // ft),
        in_specs=[
            pl.BlockSpec((t_len, FNET_WIDTH), lambda b, i: (row_block0 + b, 0)),
            pl.BlockSpec((ft, 2 * t_len), lambda b, i: (i, 0)),
            _const_spec((FNET_WIDTH, 2 * FNET_WIDTH)),
        ],
        out_specs=pl.BlockSpec((ft, FNET_WIDTH), lambda b, i: (b * (t_len // ft) + i, 0)),
        out_shape=jax.ShapeDtypeStruct((n_batch * t_len, FNET_WIDTH), BF16),
        scratch_shapes=[pltpu.VMEM((2 * t_len, FNET_WIDTH), BF16)],
        compiler_params=_cparams(2),
        name=f"fnet_{t_len}",
    )(fin, fmat, bd)


def _mla_cache_kernel(ckv_ref, kr_ref, wk_ref, e_ref, wv_ref, k_ref, v_ref):
    _mla_expand(slice(None), None, ckv_ref[...].astype(BF16), kr_ref[...].astype(BF16), None, None,
                None, wk_ref, e_ref, wv_ref, None, k_ref, v_ref)


def _mla_cache_kv(ckv_cache, kr_cache, wk, e_mat, wv):
    row = lambda w: pl.BlockSpec((TM, w), lambda i: (i, 0))
    return pl.pallas_call(
        _mla_cache_kernel,
        grid=(N_CACHE // TM,),
        in_specs=[row(128), row(128),
                  _const_spec((128, 1024)), _const_spec((128, 1024)), _const_spec((128, 512))],
        out_specs=[row(1024), row(512)],
        out_shape=[jax.ShapeDtypeStruct((N_CACHE, 1024), BF16),
                   jax.ShapeDtypeStruct((N_CACHE, 512), BF16)],
        compiler_params=_cparams(1),
        name="mla_cache_kv",
    )(ckv_cache, kr_cache, wk, e_mat, wv)


def _mla_attn_kernel(n_seg, pairs, q_ref, *refs):
    k_refs = refs[0:n_seg]
    v_refs = refs[n_seg:2 * n_seg]
    o_ref = refs[2 * n_seg]
    lane = lax.broadcasted_iota(jnp.int32, (1, LANES), 1)
    low = lane < MLA_V
    for pr in range(pairs):
        outs = []
        for hh in range(2):
            hd = 2 * pr + hh
            q = q_ref[:, hd * LANES:(hd + 1) * LANES]
            ss = [_dot_nt(q, k[:, hd * LANES:(hd + 1) * LANES]) for k in k_refs]
            m = functools.reduce(jnp.maximum, [s.max(axis=-1, keepdims=True) for s in ss])
            keep = low if hh == 0 else jnp.logical_not(low)
            sum_lane = MLA_V if hh == 0 else 0
            po = None
            for s, v_ref in zip(ss, v_refs):
                v = v_ref[:, pr * LANES:(pr + 1) * LANES]
                vm = jnp.where(lane == sum_lane, jnp.ones_like(v), jnp.where(keep, v, jnp.zeros_like(v)))
                t = _dot(jnp.exp2(s - m).astype(BF16), vm)
                po = t if po is None else po + t
            outs.append(po / po[:, sum_lane:sum_lane + 1])
        o_ref[:, pr * LANES:(pr + 1) * LANES] = jnp.where(low, outs[0], outs[1]).astype(BF16)


def _mla_attn(q_all, k_all, v_all, k_cache, v_cache, latent):
    if latent:
        tq, pairs = MLA_LAT_TQ, MLA_LAT_PAIRS
        n_b, n_q = DEC_BATCH, DEC_SEQ // tq
        q0 = N_CTX // tq
        kv_specs = [
            pl.BlockSpec((PAST_LEN, 256 * pairs), lambda b, hp, i: (b, hp)),
            pl.BlockSpec((DEC_SEQ, 256 * pairs), lambda b, hp, i: (N_CTX // DEC_SEQ + b, hp)),
            pl.BlockSpec((PAST_LEN, 128 * pairs), lambda b, hp, i: (b, hp)),
            pl.BlockSpec((DEC_SEQ, 128 * pairs), lambda b, hp, i: (N_CTX // DEC_SEQ + b, hp)),
        ]
        args = (q_all, k_cache, k_all, v_cache, v_all)
        n_seg = 2
    else:
        tq, pairs = SEQ, MLA_HEADS // 2
        n_b, n_q = BATCH, 1
        q0 = 0
        kv_specs = [
            pl.BlockSpec((SEQ, 256 * pairs), lambda b, hp, i: (b, hp)),
            pl.BlockSpec((SEQ, 128 * pairs), lambda b, hp, i: (b, hp)),
        ]
        args = (q_all, k_all, v_all)
        n_seg = 1
    return pl.pallas_call(
        functools.partial(_mla_attn_kernel, n_seg, pairs),
        grid=(n_b, MLA_HEADS // (2 * pairs), n_q),
        in_specs=[pl.BlockSpec((tq, 256 * pairs), lambda b, hp, i: (q0 + b * n_q + i, hp))] + kv_specs,
        out_specs=pl.BlockSpec((tq, 128 * pairs), lambda b, hp, i: (b * n_q + i, hp)),
        out_shape=jax.ShapeDtypeStruct((n_b * n_q * tq, MLA_HEADS * MLA_V), BF16),
        compiler_params=_cparams(3),
        name="mla_attn_lat" if latent else "mla_attn_ctx",
    )(*args)


def _swa_kernel(windowed, n_qb, sink_ref, q_ref, *refs):
    n_seg = 4 if windowed else 1
    k_refs = refs[0:n_seg]
    v_refs = refs[n_seg:2 * n_seg]
    o_ref = refs[2 * n_seg]
    tq = q_ref.shape[0]
    qb = pl.program_id(1)
    lane = lax.broadcasted_iota(jnp.int32, (1, LANES), 1)
    low = lane < SWA_HEAD_DIM
    high = jnp.logical_not(low)

    k_all = jnp.concatenate([r[...] for r in k_refs], axis=0)
    v_all = jnp.concatenate([r[...] for r in v_refs], axis=0)
    k_sw = pltpu.roll(k_all, SWA_HEAD_DIM, 1)
    v_sw = pltpu.roll(v_all, SWA_HEAD_DIM, 1)

    if windowed:
        qi = lax.broadcasted_iota(jnp.int32, (2 * tq, SWA_WINDOW), 0) % tq
        kj = lax.broadcasted_iota(jnp.int32, (2 * tq, SWA_WINDOW), 1)
        bias_prev = jnp.where(jnp.logical_and(kj >= qi, qb > 0), 0.0, NEG_INF)
        bias_next = jnp.where(jnp.logical_and(kj <= qi, qb < n_qb - 1), 0.0, NEG_INF)
    top_rows = lax.broadcasted_iota(jnp.int32, (2 * tq, 1), 0) < tq

    for g in range(SWA_KV_HEADS):
        qs = jnp.concatenate([q_ref[:, 256 * g:256 * g + 128],
                              q_ref[:, 256 * g + 128:256 * g + 256]], axis=0)
        halves = []
        for half in range(2):
            keep = low if half == 0 else high
            sum_lane = SWA_HEAD_DIM if half == 0 else 0
            straight = (g == half)
            kh = jnp.where(keep, k_all if straight else k_sw, 0.0).astype(BF16)
            vh = jnp.where(lane == sum_lane, 1.0,
                           jnp.where(keep, v_all if straight else v_sw, 0.0)).astype(BF16)
            s = _dot_nt(qs, kh)
            if windowed:
                c0, c1, c2 = PAST_LEN, PAST_LEN + SWA_WINDOW, PAST_LEN + 2 * SWA_WINDOW
                s = jnp.concatenate([s[:, :c0], s[:, c0:c1] + bias_prev, s[:, c1:c2],
                                     s[:, c2:] + bias_next], axis=1)
            sink = jnp.where(top_rows, sink_ref[4 * g + half], sink_ref[4 * g + 2 + half]) * LOG2E
            m = jnp.maximum(s.max(axis=-1, keepdims=True), sink)
            po = _dot(jnp.exp2(s - m).astype(BF16), vh)
            halves.append(po / (po[:, sum_lane:sum_lane + 1] + jnp.exp2(sink - m)))
        out = jnp.where(low, halves[0], halves[1])
        o_ref[:, 256 * g:256 * g + 128] = out[0:tq].astype(BF16)
        o_ref[:, 256 * g + 128:256 * g + 256] = out[tq:2 * tq].astype(BF16)


def _swa_attn(sink, sq, sk, sv, cache_k, cache_v, latent):
    smem = pl.BlockSpec(memory_space=pltpu.SMEM)
    if latent:
        tq = SWA_WINDOW
        n_b, n_qb = DEC_BATCH, DEC_SEQ // tq
        base = N_CTX // tq

        def prev(b, i):
            return (base + b * n_qb + jnp.maximum(i - 1, 0), 0)

        def cur(b, i):
            return (base + b * n_qb + i, 0)

        def nxt(b, i):
            return (base + b * n_qb + jnp.minimum(i + 1, n_qb - 1), 0)

        cache = pl.BlockSpec((None, PAST_LEN, 128), lambda b, i: (b, 0, 0))
        blk = lambda f: pl.BlockSpec((tq, 128), f)
        kv_specs = [cache, blk(prev), blk(cur), blk(nxt)] * 2
        args = (cache_k, sk, sk, sk, cache_v, sv, sv, sv)
        q_spec = pl.BlockSpec((tq, 512), cur)
        o_spec = pl.BlockSpec((tq, 512), lambda b, i: (b * n_qb + i, 0))
    else:
        tq = SEQ
        n_b, n_qb = BATCH, 1
        blk = pl.BlockSpec((tq, 128), lambda b, i: (b, 0))
        kv_specs = [blk, blk]
        args = (sk, sv)
        q_spec = pl.BlockSpec((tq, 512), lambda b, i: (b, 0))
        o_spec = q_spec
    return pl.pallas_call(
        functools.partial(_swa_kernel, latent, n_qb),
        grid=(n_b, n_qb),
        in_specs=[smem, q_spec] + kv_specs,
        out_specs=o_spec,
        out_shape=jax.ShapeDtypeStruct((n_b * n_qb * tq, 512), BF16),
        compiler_params=_cparams(2),
        name="swa_lat" if latent else "swa_ctx",
    )(sink, sq, *args)


def _route(h, rwt_ref, rb_ref, tri_ref, carry):
    gsz = N_EXPERTS // N_EXPERT_GROUPS
    scores = jax.nn.sigmoid(_dot_nt(rwt_ref[...], h))
    biased = scores + rb_ref[...]
    mem = lax.broadcasted_iota(jnp.int32, (gsz, TM), 0).astype(F32)
    gs_rows = []
    for g in range(N_EXPERT_GROUPS):
        bg = biased[g * gsz:(g + 1) * gsz, :]
        m1 = bg.max(axis=0, keepdims=True)
        first = jnp.min(jnp.where(bg == m1, mem, float(gsz)), axis=0, keepdims=True)
        m2 = jnp.where(mem == first, -jnp.inf, bg).max(axis=0, keepdims=True)
        gs_rows.append(m1 + m2)
    gs = jnp.concatenate(gs_rows, axis=0)
    gid = lax.broadcasted_iota(jnp.int32, gs.shape, 0).astype(F32)
    gsel = jnp.zeros(gs.shape, F32)
    for _ in range(TOPK_GROUPS):
        mx = gs.max(axis=0, keepdims=True)
        pick = gid == jnp.min(jnp.where(gs == mx, gid, float(N_EXPERT_GROUPS)), axis=0, keepdims=True)
        gsel = jnp.where(pick, 1.0, gsel)
        gs = jnp.where(pick, -jnp.inf, gs)
    emask = jnp.concatenate(
        [jnp.broadcast_to(gsel[g:g + 1, :], (gsz, TM)) for g in range(N_EXPERT_GROUPS)], axis=0)
    cand = jnp.where(emask > 0.5, biased, NEG_INF)
    eid = lax.broadcasted_iota(jnp.int32, cand.shape, 0).astype(F32)
    picks = []
    self32 = jnp.zeros(cand.shape, F32)
    for _ in range(TOP_K):
        mx = cand.max(axis=0, keepdims=True)
        pick = eid == jnp.min(jnp.where(cand == mx, eid, float(N_EXPERTS)), axis=0, keepdims=True)
        picks.append(pick)
        self32 = jnp.where(pick, 1.0, self32)
        cand = jnp.where(pick, -jnp.inf, cand)
    pos = _dot(self32.astype(BF16), tri_ref[...]) + carry
    sel_scores = [jnp.sum(jnp.where(p, scores, 0.0), axis=0, keepdims=True) for p in picks]
    wsum = functools.reduce(lambda a, b: a + b, sel_scores)
    zero_f = jnp.zeros((2, TM), F32)
    eidx = [jnp.sum(jnp.where(p, eid, 0.0), axis=0, keepdims=True) for p in picks]
    epos = [jnp.sum(jnp.where(p, pos, 0.0), axis=0, keepdims=True) for p in picks]
    ew = [s / wsum * ROUTED_SCALE for s in sel_scores]
    return (jnp.concatenate(eidx + [zero_f], axis=0).astype(jnp.int32),
            jnp.concatenate(epos + [zero_f], axis=0).astype(jnp.int32),
            jnp.concatenate(ew + [zero_f], axis=0),
            carry + jnp.sum(self32, axis=1, keepdims=True))


def _stage_e_kernel(xc_ref, xl_ref, mod_ref, g1_ref, g2_ref, fnc_ref, fnl_ref, omc_ref, oml_ref, osc_ref, osl_ref,
                    gates_ref, wf_ref, wm_ref, ws_ref, wo_ref, rwt_ref, rb_ref, tri_ref,
                    x1_ref, h2a_ref, h2b_ref, eidx_ref, epos_ref, ew_ref, cnt_ref, carry_ref):
    @pl.when(pl.program_id(0) == 0)
    def _():
        carry_ref[...] = jnp.zeros_like(carry_ref)

    is_ctx = pl.program_id(0) < NB_CTX
    carry = carry_ref[...]
    for r, rows in enumerate(_chunks()):
        fn = jnp.where(is_ctx, fnc_ref[rows, :], fnl_ref[rows, :])
        om = jnp.where(is_ctx, omc_ref[rows, :], oml_ref[rows, :])
        osw = jnp.where(is_ctx, osc_ref[rows, :], osl_ref[rows, :])
        merged = (gates_ref[rows, 0:1024].astype(F32) * _dot(fn, wf_ref[...])
                  + gates_ref[rows, 1024:2048].astype(F32) * _dot(om, wm_ref[...])
                  + gates_ref[rows, 2048:3072].astype(F32) * _dot(osw, ws_ref[...]))
        mix = _dot(merged.astype(BF16), wo_ref[...])
        x = jnp.where(is_ctx, xc_ref[rows, :], xl_ref[rows, :])
        x1 = x + mod_ref[:, 2048:3072] * _rms_rows(mix, g1_ref[...])
        x1_ref[rows, :] = x1
        h2 = _rms_rows(x1, g2_ref[...]) * (1.0 + mod_ref[:, 4096:5120]) + mod_ref[:, 3072:4096]
        h2a_ref[rows, :], h2b_ref[rows, :] = _pack_pair(h2)
        cols = pl.ds(r * TM, TM)
        eidx_ref[:, cols], epos_ref[:, cols], ew_ref[:, cols], carry = _route(
            h2.astype(BF16), rwt_ref, rb_ref, tri_ref, carry)
    carry_ref[...] = carry
    cnt_ref[...] = jnp.broadcast_to(carry, cnt_ref.shape).astype(jnp.int32)


def _stage_e(layer, xc, xl, modt, g1, g2, mixed, gates, wf, wm, ws, wo, rwt, rbias, tri):
    row = lambda w: pl.BlockSpec((TB, w), lambda i: (i, 0))
    ctx, lat = _ctx_rows(512), _lat_rows(512)
    col = lambda dt: (pl.BlockSpec((8, TB), lambda i: (0, i)), jax.ShapeDtypeStruct((8, N_TOK), dt))
    picks = [col(jnp.int32), col(jnp.int32), col(F32)]
    return pl.pallas_call(
        _stage_e_kernel,
        grid=(NB,),
        in_specs=[
            _ctx_rows(D_MODEL), _lat_rows(D_MODEL),
            pl.BlockSpec((None, 1, N_MOD * D_MODEL), lambda i: (i, 0, 0)),
            _const_spec((1, D_MODEL)), _const_spec((1, D_MODEL)),
            ctx, lat, ctx, lat, ctx, lat, row(3 * D_MODEL),
            _layer_spec((512, D_MODEL), layer), _layer_spec((512, D_MODEL), layer),
            _layer_spec((512, D_MODEL), layer), _layer_spec((D_MODEL, D_MODEL), layer),
            _const_spec((N_EXPERTS, D_MODEL)), _const_spec((N_EXPERTS, 1)), _const_spec((TM, TM)),
        ],
        out_specs=[row(D_MODEL), row(PACKED), row(PACKED)] + [s for s, _ in picks]
        + [_const_spec((N_EXPERTS, LANES))],
        out_shape=[jax.ShapeDtypeStruct((N_TOK, D_MODEL), F32),
                   jax.ShapeDtypeStruct((N_TOK, PACKED), jnp.int32),
                   jax.ShapeDtypeStruct((N_TOK, PACKED), jnp.int32)] + [s for _, s in picks]
        + [jax.ShapeDtypeStruct((N_EXPERTS, LANES), jnp.int32)],
        scratch_shapes=[pltpu.VMEM((N_EXPERTS, 1), F32)],
        compiler_params=_cparams(1),
        name="stage_e",
    )(xc, xl, modt, g1, g2, *mixed, gates, wf, wm, ws, wo, rwt, rbias, tri)


def _expert_kernel(te_ref, tv_ref, par_ref, xa_ref, xb_ref, w1_ref, w3_ref, w2_ref, oa_ref, ob_ref,
                   w1b_ref, w3b_ref, w2b_ref):
    s = pl.program_id(0)
    t = jnp.maximum(s - 1, 0)
    prev = jnp.maximum(s - 2, 0)

    starts_expert = jnp.logical_or(s == 0, te_ref[s] != te_ref[t])

    @pl.when(jnp.logical_and(tv_ref[s] > 0, starts_expert))
    def _():
        w1b_ref[par_ref[s]] = w1_ref[...].astype(BF16)
        w3b_ref[par_ref[s]] = w3_ref[...].astype(BF16)

    live = jnp.logical_and(s > 0, tv_ref[t] > 0)

    @pl.when(jnp.logical_and(live, jnp.logical_or(s == 1, te_ref[t] != te_ref[prev])))
    def _():
        w2b_ref[...] = w2_ref[...].astype(BF16)

    def run(n_chunks):
        p = par_ref[t]
        for r in range(n_chunks):
            rows = pl.ds(r * EXPERT_ROWS, EXPERT_ROWS)
            x = _unpack_pair(xa_ref[rows, :], xb_ref[rows, :]).astype(BF16)
            hg = _dot(x, w1b_ref[p])
            hu = _dot(x, w3b_ref[p])
            act = (jax.nn.silu(hg) * hu).astype(BF16)
            oa_ref[rows, :], ob_ref[rows, :] = _pack_pair(_dot(act, w2b_ref[...]))

    for n_chunks in range(1, TE // EXPERT_ROWS + 1):
        pl.when(jnp.logical_and(live, tv_ref[t] == n_chunks))(functools.partial(run, n_chunks))


def _experts(layer, tile_expert, tile_chunks, tile_parity, xsa, xsb, w1, w3, w2):
    slot_rows = pl.BlockSpec((TE, PACKED), lambda s, te, tv, par: (jnp.maximum(s - 1, 0), 0))
    ahead = lambda s, te, tv, par: (layer, te[s], 0, 0)
    current = lambda s, te, tv, par: (layer, te[jnp.maximum(s - 1, 0)], 0, 0)
    grid_spec = pltpu.PrefetchScalarGridSpec(
        num_scalar_prefetch=3,
        grid=(NTE + 1,),
        in_specs=[
            slot_rows, slot_rows,
            pl.BlockSpec((None, None, D_MODEL, EXPERT_FF), ahead),
            pl.BlockSpec((None, None, D_MODEL, EXPERT_FF), ahead),
            pl.BlockSpec((None, None, EXPERT_FF, D_MODEL), current),
        ],
        out_specs=[slot_rows, slot_rows],
        scratch_shapes=[pltpu.VMEM((2, D_MODEL, EXPERT_FF), BF16), pltpu.VMEM((2, D_MODEL, EXPERT_FF), BF16),
                        pltpu.VMEM((EXPERT_FF, D_MODEL), BF16)],
    )
    return pl.pallas_call(
        _expert_kernel,
        grid_spec=grid_spec,
        out_shape=[jax.ShapeDtypeStruct((S_MAX, PACKED), jnp.int32)] * 2,
        compiler_params=_cparams(1),
        name="experts",
    )(tile_expert, tile_chunks, tile_parity, xsa, xsb, w1, w3, w2)


def _sc_mesh():
    return plsc.VectorSubcoreMesh(core_axis_name="c", subcore_axis_name="s",
                                  num_cores=SC_CORES, num_subcores=SC_SUBCORES)


def _sc_scatter_rows(rows, slot8):
    @functools.partial(pl.kernel, mesh=_sc_mesh(), scratch_types=[pltpu.SemaphoreType.DMA],
                       out_type=jax.ShapeDtypeStruct((S_MAX, PACKED), jnp.int32))
    def scatter(x_hbm, i_hbm, o_hbm, sem):
        def body(x_vmem, i_vmem):
            copies = [pltpu.async_copy(x_vmem, o_hbm.at[i_vmem.at[k]], sem) for k in range(TOP_K)]
            for cp in copies:
                cp.wait()

        pltpu.emit_pipeline(
            body,
            grid=(N_TOK // SC_ROWS,),
            in_specs=[pl.BlockSpec((SC_ROWS, PACKED), lambda i: (i, 0)),
                      pl.BlockSpec((8, SC_ROWS), lambda i: (0, i))],
            out_specs=[],
            core_axis_name=("c", "s"),
            dimension_semantics=(pltpu.PARALLEL,),
        )(x_hbm, i_hbm)

    return scatter(rows, slot8)


def _sc_gather_rows(table, idx):
    n = idx.shape[1]

    @functools.partial(pl.kernel, mesh=_sc_mesh(), scratch_types=[],
                       out_type=jax.ShapeDtypeStruct((n, PACKED), jnp.int32))
    def gather(t_hbm, i_hbm, o_hbm):
        def body(i_vmem, o_vmem):
            pltpu.sync_copy(t_hbm.at[i_vmem.at[0]], o_vmem)

        pltpu.emit_pipeline(
            body,
            grid=(n // SC_ROWS,),
            in_specs=[pl.BlockSpec((1, SC_ROWS), lambda i: (0, i))],
            out_specs=[pl.BlockSpec((SC_ROWS, PACKED), lambda i: (i, 0))],
            core_axis_name=("c", "s"),
            dimension_semantics=(pltpu.PARALLEL,),
        )(i_hbm, o_hbm)

    return gather(table, idx)


def _shared_kernel(ha_ref, hb_ref, s1_ref, s3_ref, s2_ref, o_ref):
    for rows in _chunks():
        h = _unpack_pair(ha_ref[rows, :], hb_ref[rows, :]).astype(BF16)
        act = jax.nn.silu(_dot(h, s1_ref[...])) * _dot(h, s3_ref[...])
        o_ref[rows, :] = _dot(act.astype(BF16), s2_ref[...]).astype(BF16)


def _shared_ffn(layer, h2a, h2b, s1, s3, s2):
    row = lambda w: pl.BlockSpec((TB, w), lambda i: (i, 0))
    return pl.pallas_call(
        _shared_kernel,
        grid=(NB,),
        in_specs=[row(PACKED), row(PACKED),
                  _layer_spec((D_MODEL, SHARED_FF), layer), _layer_spec((D_MODEL, SHARED_FF), layer),
                  _layer_spec((SHARED_FF, D_MODEL), layer)],
        out_specs=row(D_MODEL),
        out_shape=jax.ShapeDtypeStruct((N_TOK, D_MODEL), BF16),
        compiler_params=_cparams(1),
        name="shared_ffn",
    )(h2a, h2b, s1, s3, s2)


def _stage_g_kernel(x1_ref, mod_ref, g3_ref, yga_ref, ygb_ref, ew_ref, shared_ref, oc_ref, ol_ref):
    is_ctx = pl.program_id(0) < NB_CTX
    for rows in _chunks():
        y = shared_ref[rows, :].astype(F32)
        for k in range(TOP_K):
            y = y + ew_ref[rows, k:k + 1] * _unpack_pair(yga_ref[k, rows, :], ygb_ref[k, rows, :])
        out = x1_ref[rows, :] + mod_ref[:, 5120:6144] * _rms_rows(y, g3_ref[...])

        @pl.when(is_ctx)
        def _():
            oc_ref[rows, :] = out

        @pl.when(jnp.logical_not(is_ctx))
        def _():
            ol_ref[rows, :] = out


def _stage_g(x1, modt, g3, yga, ygb, ew_rows, shared):
    row = lambda w: pl.BlockSpec((TB, w), lambda i: (i, 0))
    picked = pl.BlockSpec((TOP_K, TB, PACKED), lambda i: (0, i, 0))
    return pl.pallas_call(
        _stage_g_kernel,
        grid=(NB,),
        in_specs=[row(D_MODEL), pl.BlockSpec((None, 1, N_MOD * D_MODEL), lambda i: (i, 0, 0)),
                  _const_spec((1, D_MODEL)), picked, picked, row(8), row(D_MODEL)],
        out_specs=[_ctx_rows(D_MODEL), _lat_rows(D_MODEL)],
        out_shape=[jax.ShapeDtypeStruct((N_CTX, D_MODEL), F32),
                   jax.ShapeDtypeStruct((N_LAT, D_MODEL), F32)],
        compiler_params=_cparams(1),
        name="stage_g",
    )(x1, modt, g3, yga, ygb, ew_rows, shared)


def _rope_tables():
    t = np.arange(DEC_SEQ)
    pos = np.stack([(t // GRID_W), (t % GRID_W)], axis=-1).astype(np.float32)

    def table(r):
        n_freq = r // 4
        inv = np.float32(ROPE_BASE) ** (-np.arange(n_freq, dtype=np.float32) / np.float32(n_freq))
        ang = pos[:, :, None] * inv.astype(np.float32)
        cos = np.cos(ang)
        sin = np.sin(ang)
        cos_t = np.stack([cos, cos], axis=2).reshape(DEC_SEQ, r)
        sin_t = np.stack([-sin, sin], axis=2).reshape(DEC_SEQ, r)
        return cos_t, sin_t

    c64, s64 = table(SWA_HEAD_DIM)
    c32, s32 = table(MLA_ROPE)
    lat = np.concatenate([np.tile(c64, (1, 8)), np.tile(s64, (1, 8)),
                          np.tile(c32, (1, 4)), np.tile(s32, (1, 4))], axis=1)
    ident = np.concatenate([np.ones((TB, 512)), np.zeros((TB, 512)),
                            np.ones((TB, 128)), np.zeros((TB, 128))], axis=1)
    return jnp.asarray(np.concatenate([ident, lat], axis=0).astype(np.float32))


def _dft_pair(n):
    k = np.arange(n, dtype=np.int64)
    ang = ((k[:, None] * k[None, :]) % n).astype(np.float64) * (2.0 * math.pi / n)
    return np.cos(ang), np.sin(ang)


def _fnet_tables():
    c64, s64 = _dft_pair(FNET_GROUP_DIM)
    eye = np.eye(FNET_GROUPS)
    bd = np.concatenate([np.kron(eye, c64), np.kron(eye, s64)], axis=1)
    mats = []
    for t_len in (SEQ, DEC_SEQ):
        c, s = _dft_pair(t_len)
        mats.append(np.concatenate([c, -s], axis=1))
    return tuple(jnp.asarray(m.astype(np.float32).astype(BF16)) for m in (bd, mats[0], mats[1]))


def _layer_weights(l, w_in, w_uq, w_ukv):
    w = w_in[l]
    wide = jnp.concatenate([w[:, 0:1024], w[:, 1056:1824], w[:, 1024:1056],
                            jnp.zeros((D_MODEL, 96), F32)], axis=1).astype(BF16)

    uq = w_uq[l].reshape(MLA_Q_RANK, MLA_HEADS, MLA_NOPE + MLA_ROPE)
    z32 = jnp.zeros((MLA_Q_RANK, MLA_HEADS, 32), F32)
    wqa = jnp.concatenate([uq, z32], axis=2).reshape(MLA_Q_RANK, 1024).astype(BF16)
    ukv = w_ukv[l].reshape(MLA_KV_RANK, MLA_HEADS, MLA_NOPE + MLA_V)
    wk = jnp.concatenate([ukv[:, :, :MLA_NOPE], jnp.zeros((MLA_KV_RANK, MLA_HEADS, 64), F32)],
                         axis=2).reshape(MLA_KV_RANK, 1024).astype(BF16)
    wv = ukv[:, :, MLA_NOPE:].reshape(MLA_KV_RANK, 512).astype(BF16)
    return wide, wqa, wk, wv


def _rope_placement():
    e = np.zeros((128, 1024), np.float32)
    for hd in range(MLA_HEADS):
        for i in range(MLA_ROPE):
            e[i, hd * 128 + MLA_NOPE + i] = 1.0
    return jnp.asarray(e, BF16)


def _moe_dispatch_plan(eidx, epos, counts):
    padded = ((counts + TE - 1) // TE) * TE
    ends = jnp.cumsum(padded)
    offs = ends - padded
    ids = jnp.arange(N_EXPERTS, dtype=jnp.int32)
    picked_off = jnp.sum(jnp.where(eidx[:, :, None] == ids, offs, 0), axis=-1)
    slot = picked_off + epos
    starts = jnp.arange(NTE, dtype=jnp.int32) * TE
    tile_expert = jnp.sum((ends[None, :] <= starts[:, None]).astype(jnp.int32), axis=1)
    tile_expert = jnp.minimum(tile_expert, N_EXPERTS - 1)
    pick = tile_expert[:, None] == ids[None, :]
    last_real = jnp.sum(jnp.where(pick, (offs + counts)[None, :], 0), axis=1)
    n_real = jnp.clip(last_real - starts, 0, TE)
    n_real = jnp.where(starts < ends[-1], n_real, 0)
    tile_chunks = ((n_real + EXPERT_ROWS - 1) // EXPERT_ROWS).astype(jnp.int32)
    rank = jnp.cumsum((counts > 0).astype(jnp.int32)) - 1
    tile_parity = jnp.sum(jnp.where(pick, rank[None, :], 0), axis=1) % 2
    close = lambda a, v: jnp.concatenate([a, jnp.full((1,), v, jnp.int32)])
    return (slot, close(tile_expert, N_EXPERTS - 1), close(tile_chunks, 0),
            close(tile_parity.astype(jnp.int32), 0))


def kernel(x_prompt, x_sample, cache_mla_ckv, cache_mla_krope, cache_swa_k, cache_swa_v, c, c_ctx,
           ada_w, ada_b, norm_g, w_in, q_norm, kv_norm, w_fnet, w_uq, w_ukv, w_mla_o, swa_sink,
           w_swa_o, w_gate, b_gate, w_out, router_w, router_bias, exp_w1, exp_w3, exp_w2,
           shared_w1, shared_w3, shared_w2):
    xc = x_prompt.reshape(N_CTX, D_MODEL)
    xl = x_sample.reshape(N_LAT, D_MODEL)

    cond8 = jnp.concatenate([c_ctx[None, :], c, jnp.zeros((3, D_MODEL), F32)], axis=0)
    mod = _modulation(cond8, ada_w, ada_b)
    tile_cond = np.concatenate([np.zeros(NB_CTX, np.int32),
                                1 + np.arange(NB - NB_CTX, dtype=np.int32) // LAT_BLOCKS])

    tab = _rope_tables()
    bd, f_ctx, f_lat = _fnet_tables()
    e_mat = _rope_placement()
    tri = jnp.asarray(np.triu(np.ones((TM, TM), np.float32), 1), BF16)
    w_gate_b, w_fnet_b, w_mla_o_b, w_swa_o_b, w_out_b, sw1_b, sw3_b, sw2_b = (
        w.astype(BF16) for w in (w_gate, w_fnet, w_mla_o, w_swa_o, w_out, shared_w1, shared_w3, shared_w2))

    new_ckv, new_kr, new_k, new_v = [], [], [], []
    for l in range(DEPTH):
        modt = mod[l][tile_cond][:, None, :]
        wide, wqa, wk, wv = _layer_weights(l, w_in, w_uq, w_ukv)
        ng = norm_g[l]

        fin, ckv, kr, sq, sk, sv, gates, q_m, k_m, v_m = _stage_a(
            l, xc, xl, modt, ng[0:1], wide, w_gate_b, b_gate[l][None, :],
            q_norm[l][None, :], kv_norm[l][None, :], tab, wqa, wk, e_mat, wv)

        new_ckv.append(ckv[:N_CTX].reshape(BATCH, SEQ, MLA_KV_RANK))
        new_kr.append(kr[:N_CTX, :MLA_ROPE].reshape(BATCH, SEQ, MLA_ROPE))
        new_k.append(sk[:N_CTX].reshape(BATCH, SEQ, SWA_KV_HEADS, SWA_HEAD_DIM))
        new_v.append(sv[:N_CTX].reshape(BATCH, SEQ, SWA_KV_HEADS, SWA_HEAD_DIM))

        fn = (_fnet(fin, f_ctx, bd, BATCH, SEQ, 0),
              _fnet(fin, f_lat, bd, DEC_BATCH, DEC_SEQ, N_CTX // DEC_SEQ))

        kr_cache = jnp.pad(cache_mla_krope[:, l].reshape(N_CACHE, MLA_ROPE), ((0, 0), (0, 96)))
        k_c, v_c = _mla_cache_kv(cache_mla_ckv[:, l].reshape(N_CACHE, MLA_KV_RANK), kr_cache,
                                 wk, e_mat, wv)
        om = (_mla_attn(q_m, k_m, v_m, k_c, v_c, latent=False),
              _mla_attn(q_m, k_m, v_m, k_c, v_c, latent=True))

        ck = cache_swa_k[:, l].reshape(DEC_BATCH, PAST_LEN, 128)
        cv = cache_swa_v[:, l].reshape(DEC_BATCH, PAST_LEN, 128)
        osw = (_swa_attn(swa_sink[l], sq, sk, sv, ck, cv, latent=False),
               _swa_attn(swa_sink[l], sq, sk, sv, ck, cv, latent=True))

        x1, h2a, h2b, eidx, epos, ew, counts = _stage_e(
            l, xc, xl, modt, ng[1:2], ng[2:3], fn + om + osw, gates,
            w_fnet_b, w_mla_o_b, w_swa_o_b, w_out_b,
            router_w[l].T.astype(BF16), router_bias[l][:, None], tri)
        slot, tile_expert, tile_chunks, tile_parity = _moe_dispatch_plan(eidx, epos, counts[:, 0])
        xsa = _sc_scatter_rows(h2a, slot)
        xsb = _sc_scatter_rows(h2b, slot)
        shared = _shared_ffn(l, h2a, h2b, sw1_b, sw3_b, sw2_b)
        ysa, ysb = _experts(l, tile_expert, tile_chunks, tile_parity, xsa, xsb, exp_w1, exp_w3, exp_w2)
        picks = slot[:TOP_K].reshape(1, TOP_K * N_TOK)
        yga = _sc_gather_rows(ysa, picks).reshape(TOP_K, N_TOK, PACKED)
        ygb = _sc_gather_rows(ysb, picks).reshape(TOP_K, N_TOK, PACKED)
        xc, xl = _stage_g(x1, modt, ng[3:4], yga, ygb, ew.T, shared)

    y_p = xc.reshape(BATCH, SEQ, D_MODEL)
    y_s = xl.reshape(DEC_BATCH, DEC_SEQ, D_MODEL)
    return (y_p, y_s, jnp.stack(new_ckv, axis=1), jnp.stack(new_kr, axis=1),
            jnp.stack(new_k, axis=1), jnp.stack(new_v, axis=1))
```

```python
import functools
import math

import numpy as np
import jax
import jax.numpy as jnp
from jax import lax
from jax.experimental import pallas as pl
from jax.experimental.pallas import tpu as pltpu
from jax.experimental.pallas import tpu_sc as plsc

D_MODEL = 1024
BATCH = 16
SEQ = 256
DEPTH = 2
DEC_BATCH = 4
DEC_SEQ = 2048
PAST_LEN = 512
GRID_W = 64
EPS = 1e-6
ROPE_BASE = 10000.0
NEG_INF = -1e30

FNET_GROUPS = 8
FNET_GROUP_DIM = 64
FNET_WIDTH = 512
MLA_HEADS = 8
MLA_Q_RANK = 384
MLA_KV_RANK = 128
MLA_NOPE = 64
MLA_ROPE = 32
MLA_V = 64
MLA_SCALE = (MLA_NOPE + MLA_ROPE) ** -0.5
LOG2E = math.log2(math.e)
SWA_HEADS = 8
SWA_KV_HEADS = 2
SWA_HEAD_DIM = 64
SWA_WINDOW = 128
SWA_SCALE = SWA_HEAD_DIM ** -0.5
N_MOD = 6
N_EXPERTS = 64
N_EXPERT_GROUPS = 8
TOPK_GROUPS = 4
TOP_K = 6
EXPERT_FF = 256
SHARED_FF = 256
ROUTED_SCALE = 2.5

LANES = 128
TM = 256
N_CTX = BATCH * SEQ
N_LAT = DEC_BATCH * DEC_SEQ
N_TOK = N_CTX + N_LAT
N_CACHE = DEC_BATCH * PAST_LEN
NT_CTX = N_CTX // TM
NT_LAT = N_LAT // TM
NT = N_TOK // TM
LAT_TILES = DEC_SEQ // TM
TB = 512
NB = N_TOK // TB
NB_CTX = N_CTX // TB
LAT_BLOCKS = DEC_SEQ // TB
MLA_LAT_TQ = 256
MLA_LAT_PAIRS = 4
FNET_ROWS = 1024
TE = 512
S_MAX = N_TOK * TOP_K + N_EXPERTS * TE
NTE = S_MAX // TE
EXPERT_ROWS = 256
VMEM_LIMIT = 56 * 1024 * 1024
PACKED = D_MODEL // 4
SC_ROWS = 128
SC_CORES = 2
SC_SUBCORES = 16

A_F = (0, 512)
A_QD = (512, 896)
A_KV = (896, 1024)
A_SQ = (1024, 1536)
A_SK = (1536, 1664)
A_SV = (1664, 1792)
A_KR = (1792, 1920)
W_IN_WIDE = 1920
TAB_W = 1280

F32 = jnp.float32
BF16 = jnp.bfloat16


def _cparams(n_axes, parallel=False):
    sem = ("parallel" if parallel else "arbitrary",) * n_axes
    return pltpu.CompilerParams(dimension_semantics=sem, vmem_limit_bytes=VMEM_LIMIT)


def _dot(a, b):
    return jnp.dot(a, b, preferred_element_type=F32)


def _dot_nt(a, b):
    return lax.dot_general(a, b, (((1,), (1,)), ((), ())), preferred_element_type=F32)


def _rms_rows(v, g):
    return v * lax.rsqrt(jnp.mean(v * v, axis=-1, keepdims=True) + EPS) * g


def _pack_rows(v):
    half = v.shape[1] // 2
    lo = lax.bitcast_convert_type(v[:, :half].astype(BF16).astype(F32), jnp.int32)
    hi = lax.bitcast_convert_type(v[:, half:].astype(BF16).astype(F32), jnp.int32)
    return jnp.bitwise_or(jnp.bitwise_and(hi, -65536), jnp.bitwise_and(jnp.right_shift(lo, 16), 65535))


def _unpack_rows(w):
    lo = lax.bitcast_convert_type(jnp.left_shift(w, 16), F32)
    hi = lax.bitcast_convert_type(jnp.bitwise_and(w, -65536), F32)
    return jnp.concatenate([lo, hi], axis=1)


def _pack_pair(v):
    half = v.shape[1] // 2
    return _pack_rows(v[:, :half]), _pack_rows(v[:, half:])


def _unpack_pair(a, b):
    return jnp.concatenate([_unpack_rows(a), _unpack_rows(b)], axis=1)


def _const_spec(shape):
    return pl.BlockSpec(shape, lambda *_: (0,) * len(shape))


def _layer_spec(shape, layer):
    return pl.BlockSpec((None,) + shape, lambda *_: (layer,) + (0,) * len(shape))


def _ctx_rows(width):
    return pl.BlockSpec((TB, width), lambda i: (jnp.minimum(i, NB_CTX - 1), 0))


def _lat_rows(width):
    return pl.BlockSpec((TB, width), lambda i: (jnp.maximum(i - NB_CTX, 0), 0))


def _tab_row_block(i):
    return jnp.where(i < NB_CTX, 0, 1 + (i - NB_CTX) % LAT_BLOCKS)


def _mod_kernel(cond_ref, w_ref, b_ref, o_ref):
    c = cond_ref[...]
    a = (c * jax.nn.sigmoid(c)).astype(BF16)
    o_ref[...] = _dot(a, w_ref[...].astype(BF16)) + b_ref[...]


def _modulation(cond8, ada_w, ada_b):
    tn = 512
    nj = N_MOD * D_MODEL // tn
    return pl.pallas_call(
        _mod_kernel,
        grid=(DEPTH, nj),
        in_specs=[
            pl.BlockSpec((8, D_MODEL), lambda l, j: (0, 0)),
            pl.BlockSpec((None, D_MODEL, tn), lambda l, j: (l, 0, j)),
            pl.BlockSpec((None, 1, tn), lambda l, j: (l, 0, j)),
        ],
        out_specs=pl.BlockSpec((None, 8, tn), lambda l, j: (l, 0, j)),
        out_shape=jax.ShapeDtypeStruct((DEPTH, 8, N_MOD * D_MODEL), F32),
        compiler_params=_cparams(2),
        name="modulation",
    )(cond8, ada_w, ada_b.reshape(DEPTH, 1, N_MOD * D_MODEL))


def _half_swap(x, half):
    n = x.shape[1]
    lane = lax.broadcasted_iota(jnp.int32, (1, n), 1)
    return jnp.where((lane & half) == 0, pltpu.roll(x, n - half, 1), pltpu.roll(x, half, 1))


def _mla_expand(rows, cq, ckv, kr, cos32, sin32, wqa_ref, wk_ref, e_ref, wv_ref,
                q_ref, k_ref, v_ref):
    if q_ref is not None:
        lane = lax.broadcasted_iota(jnp.int32, (1, LANES), 1)
        rope_lane = jnp.logical_and(lane >= MLA_NOPE, lane < MLA_NOPE + MLA_ROPE)
        cos_h = jnp.where(rope_lane, cos32, 1.0)
        sin_h = jnp.where(rope_lane, sin32, 0.0)
        for hd in range(MLA_HEADS):
            lo, hi = hd * LANES, (hd + 1) * LANES
            q = _dot(cq, wqa_ref[:, lo:hi])
            q = q * cos_h + _half_swap(q, MLA_ROPE // 4) * sin_h
            q_ref[rows, lo:hi] = (q * (MLA_SCALE * LOG2E)).astype(BF16)
    k_ref[rows, :] = (_dot(ckv, wk_ref[...]) + _dot(kr, e_ref[...])).astype(BF16)
    v_ref[rows, :] = _dot(ckv, wv_ref[...]).astype(BF16)


def _chunks():
    return [pl.ds(r * TM, TM) for r in range(TB // TM)]


def _stage_a_kernel(xc_ref, xl_ref, mod_ref, g_ref, win_ref, wg_ref, bg_ref, qn_ref, kvn_ref, tab_ref,
                    wqa_ref, wk_ref, e_ref, wv_ref,
                    fin_ref, ckv_ref, kr_ref, sq_ref, sk_ref, sv_ref, gates_ref,
                    qm_ref, km_ref, vm_ref):
    is_ctx = pl.program_id(0) < NB_CTX
    for rows in _chunks():
        x = jnp.where(is_ctx, xc_ref[rows, :], xl_ref[rows, :])
        h = (_rms_rows(x, g_ref[...]) * (1.0 + mod_ref[:, 1024:2048]) + mod_ref[:, 0:1024]).astype(BF16)

        def proj(seg):
            return _dot(h, win_ref[:, seg[0]:seg[1]])

        fin_ref[rows, :] = proj(A_F).astype(BF16)
        cq = _rms_rows(proj(A_QD), qn_ref[...]).astype(BF16)
        ckv = _rms_rows(proj(A_KV), kvn_ref[...])
        ckv_ref[rows, :] = ckv
        cos64 = tab_ref[rows, 0:512]
        sin64 = tab_ref[rows, 512:1024]
        sq = proj(A_SQ)
        sq = sq * cos64 + _half_swap(sq, SWA_HEAD_DIM // 4) * sin64
        sq_ref[rows, :] = (sq * (SWA_SCALE * LOG2E)).astype(BF16)
        sk = proj(A_SK)
        sk_ref[rows, :] = sk * cos64[:, 0:128] + _half_swap(sk, SWA_HEAD_DIM // 4) * sin64[:, 0:128]
        sv_ref[rows, :] = proj(A_SV)
        cos32 = tab_ref[rows, 1024:1152]
        sin32 = tab_ref[rows, 1152:1280]
        kr = proj(A_KR)
        kr = kr * cos32 + _half_swap(kr, MLA_ROPE // 4) * sin32
        kr_ref[rows, :] = kr
        _mla_expand(rows, cq, ckv.astype(BF16), kr.astype(BF16), cos32, sin32,
                    wqa_ref, wk_ref, e_ref, wv_ref, qm_ref, km_ref, vm_ref)
        for c in range(3):
            lo, hi = c * D_MODEL, (c + 1) * D_MODEL
            gates_ref[rows, lo:hi] = jax.nn.sigmoid(_dot(h, wg_ref[:, lo:hi]) + bg_ref[:, lo:hi]).astype(BF16)


def _stage_a(layer, xc, xl, modt, g0, w_in_wide, w_gate, b_gate, q_norm, kv_norm, tab, wqa, wk, e_mat, wv):
    row = lambda w: pl.BlockSpec((TB, w), lambda i: (i, 0))
    outs = [(512, BF16), (128, F32), (128, F32), (512, BF16), (128, F32), (128, F32),
            (3 * D_MODEL, BF16), (1024, BF16), (1024, BF16), (512, BF16)]
    return pl.pallas_call(
        _stage_a_kernel,
        grid=(NB,),
        in_specs=[
            _ctx_rows(D_MODEL), _lat_rows(D_MODEL),
            pl.BlockSpec((None, 1, N_MOD * D_MODEL), lambda i: (i, 0, 0)),
            _const_spec((1, D_MODEL)),
            _const_spec((D_MODEL, W_IN_WIDE)),
            _layer_spec((D_MODEL, 3 * D_MODEL), layer),
            _const_spec((1, 3 * D_MODEL)),
            _const_spec((1, MLA_Q_RANK)),
            _const_spec((1, MLA_KV_RANK)),
            pl.BlockSpec((TB, TAB_W), lambda i: (_tab_row_block(i), 0)),
            _const_spec((MLA_Q_RANK, 1024)),
            _const_spec((128, 1024)), _const_spec((128, 1024)), _const_spec((128, 512)),
        ],
        out_specs=[row(w) for w, _ in outs],
        out_shape=[jax.ShapeDtypeStruct((N_TOK, w), dt) for w, dt in outs],
        compiler_params=_cparams(1),
        name="stage_a",
    )(xc, xl, modt, g0, w_in_wide, w_gate, b_gate, q_norm, kv_norm, tab, wqa, wk, e_mat, wv)


def _fnet_kernel(t_len, scale, fin_ref, f_ref, bd_ref, o_ref, zz_ref):
    @pl.when(pl.program_id(1) == 0)
    def _():
        z = fin_ref[...]
        zz_ref[0:t_len, :] = _dot(z, bd_ref[:, 0:512]).astype(BF16)
        zz_ref[t_len:2 * t_len, :] = _dot(z, bd_ref[:, 512:1024]).astype(BF16)

    o_ref[...] = (_dot(f_ref[...], zz_ref[...]) * scale).astype(BF16)


def _fnet(fin, fmat, bd, n_batch, t_len, row_block0):
    scale = 1.0 / math.sqrt(t_len * FNET_GROUP_DIM)
    ft = min(t_len, FNET_ROWS)
    return pl.pallas_call(
        functools.partial(_fnet_kernel, t_len, scale),
        grid=(n_batch, t_len // ft),
        in_specs=[
            pl.BlockSpec((t_len, FNET_WIDTH), lambda b, i: (row_block0 + b, 0)),
            pl.BlockSpec((ft, 2 * t_len), lambda b, i: (i, 0)),
            _const_spec((FNET_WIDTH, 2 * FNET_WIDTH)),
        ],
        out_specs=pl.BlockSpec((ft, FNET_WIDTH), lambda b, i: (b * (t_len // ft) + i, 0)),
        out_shape=jax.ShapeDtypeStruct((n_batch * t_len, FNET_WIDTH), BF16),
        scratch_shapes=[pltpu.VMEM((2 * t_len, FNET_WIDTH), BF16)],
        compiler_params=_cparams(2),
        name=f"fnet_{t_len}",
    )(fin, fmat, bd)


def _mla_cache_kernel(ckv_ref, kr_ref, wk_ref, e_ref, wv_ref, k_ref, v_ref):
    _mla_expand(slice(None), None, ckv_ref[...].astype(BF16), kr_ref[...].astype(BF16), None, None,
                None, wk_ref, e_ref, wv_ref, None, k_ref, v_ref)


def _mla_cache_kv(ckv_cache, kr_cache, wk, e_mat, wv):
    row = lambda w: pl.BlockSpec((TM, w), lambda i: (i, 0))
    return pl.pallas_call(
        _mla_cache_kernel,
        grid=(N_CACHE // TM,),
        in_specs=[row(128), row(128),
                  _const_spec((128, 1024)), _const_spec((128, 1024)), _const_spec((128, 512))],
        out_specs=[row(1024), row(512)],
        out_shape=[jax.ShapeDtypeStruct((N_CACHE, 1024), BF16),
                   jax.ShapeDtypeStruct((N_CACHE, 512), BF16)],
        compiler_params=_cparams(1),
        name="mla_cache_kv",
    )(ckv_cache, kr_cache, wk, e_mat, wv)


def _mla_attn_kernel(n_seg, pairs, q_ref, *refs):
    k_refs = refs[0:n_seg]
    v_refs = refs[n_seg:2 * n_seg]
    o_ref = refs[2 * n_seg]
    lane = lax.broadcasted_iota(jnp.int32, (1, LANES), 1)
    low = lane < MLA_V
    for pr in range(pairs):
        outs = []
        for hh in range(2):
            hd = 2 * pr + hh
            q = q_ref[:, hd * LANES:(hd + 1) * LANES]
            ss = [_dot_nt(q, k[:, hd * LANES:(hd + 1) * LANES]) for k in k_refs]
            m = functools.reduce(jnp.maximum, [s.max(axis=-1, keepdims=True) for s in ss])
            keep = low if hh == 0 else jnp.logical_not(low)
            sum_lane = MLA_V if hh == 0 else 0
            po = None
            for s, v_ref in zip(ss, v_refs):
                v = v_ref[:, pr * LANES:(pr + 1) * LANES]
                vm = jnp.where(lane == sum_lane, jnp.ones_like(v), jnp.where(keep, v, jnp.zeros_like(v)))
                t = _dot(jnp.exp2(s - m).astype(BF16), vm)
                po = t if po is None else po + t
            outs.append(po / po[:, sum_lane:sum_lane + 1])
        o_ref[:, pr * LANES:(pr + 1) * LANES] = jnp.where(low, outs[0], outs[1]).astype(BF16)


def _mla_attn(q_all, k_all, v_all, k_cache, v_cache, latent):
    if latent:
        tq, pairs = MLA_LAT_TQ, MLA_LAT_PAIRS
        n_b, n_q = DEC_BATCH, DEC_SEQ // tq
        q0 = N_CTX // tq
        kv_specs = [
            pl.BlockSpec((PAST_LEN, 256 * pairs), lambda b, hp, i: (b, hp)),
            pl.BlockSpec((DEC_SEQ, 256 * pairs), lambda b, hp, i: (N_CTX // DEC_SEQ + b, hp)),
            pl.BlockSpec((PAST_LEN, 128 * pairs), lambda b, hp, i: (b, hp)),
            pl.BlockSpec((DEC_SEQ, 128 * pairs), lambda b, hp, i: (N_CTX // DEC_SEQ + b, hp)),
        ]
        args = (q_all, k_cache, k_all, v_cache, v_all)
        n_seg = 2
    else:
        tq, pairs = SEQ, MLA_HEADS // 2
        n_b, n_q = BATCH, 1
        q0 = 0
        kv_specs = [
            pl.BlockSpec((SEQ, 256 * pairs), lambda b, hp, i: (b, hp)),
            pl.BlockSpec((SEQ, 128 * pairs), lambda b, hp, i: (b, hp)),
        ]
        args = (q_all, k_all, v_all)
        n_seg = 1
    return pl.pallas_call(
        functools.partial(_mla_attn_kernel, n_seg, pairs),
        grid=(n_b, MLA_HEADS // (2 * pairs), n_q),
        in_specs=[pl.BlockSpec((tq, 256 * pairs), lambda b, hp, i: (q0 + b * n_q + i, hp))] + kv_specs,
        out_specs=pl.BlockSpec((tq, 128 * pairs), lambda b, hp, i: (b * n_q + i, hp)),
        out_shape=jax.ShapeDtypeStruct((n_b * n_q * tq, MLA_HEADS * MLA_V), BF16),
        compiler_params=_cparams(3),
        name="mla_attn_lat" if latent else "mla_attn_ctx",
    )(*args)


def _swa_kernel(windowed, n_steps, sink_ref, q_ref, *refs):
    n_seg = 4 if windowed else 1
    k_refs = refs[0:n_seg]
    v_refs = refs[n_seg:2 * n_seg]
    o_ref = refs[2 * n_seg]
    step = pl.program_id(1)
    lane = lax.broadcasted_iota(jnp.int32, (1, LANES), 1)
    low = lane < SWA_HEAD_DIM
    high = jnp.logical_not(low)

    k_all = jnp.concatenate([r[...] for r in k_refs], axis=0)
    v_all = jnp.concatenate([r[...] for r in v_refs], axis=0)
    k_sw = pltpu.roll(k_all, SWA_HEAD_DIM, 1)
    v_sw = pltpu.roll(v_all, SWA_HEAD_DIM, 1)

    if windowed:
        tq = SWA_WINDOW
        qi = lax.broadcasted_iota(jnp.int32, (2 * tq, tq), 0) % tq
        kj = lax.broadcasted_iota(jnp.int32, (2 * tq, tq), 1)
        after = kj >= qi
        before = kj <= qi
        biases = [(jnp.where(jnp.logical_and(after, step > 0), 0.0, NEG_INF),
                   jnp.where(before, 0.0, NEG_INF)),
                  (jnp.where(after, 0.0, NEG_INF),
                   jnp.where(jnp.logical_and(before, step < n_steps - 1), 0.0, NEG_INF))]
    else:
        tq = q_ref.shape[0]
    n_sub = q_ref.shape[0] // tq
    top_rows = lax.broadcasted_iota(jnp.int32, (2 * tq, 1), 0) < tq

    for g in range(SWA_KV_HEADS):
        kh, vh = [], []
        for half in range(2):
            keep = low if half == 0 else high
            sum_lane = SWA_HEAD_DIM if half == 0 else 0
            straight = (g == half)
            kh.append(jnp.where(keep, k_all if straight else k_sw, 0.0).astype(BF16))
            vh.append(jnp.where(lane == sum_lane, 1.0,
                                jnp.where(keep, v_all if straight else v_sw, 0.0)).astype(BF16))
        for sub in range(n_sub):
            rows = slice(sub * tq, (sub + 1) * tq)
            qs = jnp.concatenate([q_ref[rows, 256 * g:256 * g + 128],
                                  q_ref[rows, 256 * g + 128:256 * g + 256]], axis=0)
            halves = []
            for half in range(2):
                sum_lane = SWA_HEAD_DIM if half == 0 else 0
                ks, vs = kh[half], vh[half]
                if windowed:
                    w0 = PAST_LEN + sub * tq
                    if sub == 0:
                        ks, vs = ks[0:w0 + 3 * tq], vs[0:w0 + 3 * tq]
                    else:
                        ks = jnp.concatenate([ks[0:PAST_LEN], ks[w0:w0 + 3 * tq]], axis=0)
                        vs = jnp.concatenate([vs[0:PAST_LEN], vs[w0:w0 + 3 * tq]], axis=0)
                s = _dot_nt(qs, ks)
                if windowed:
                    c0, c1, c2 = PAST_LEN, PAST_LEN + tq, PAST_LEN + 2 * tq
                    s = jnp.concatenate([s[:, :c0], s[:, c0:c1] + biases[sub][0], s[:, c1:c2],
                                         s[:, c2:] + biases[sub][1]], axis=1)
                sink = jnp.where(top_rows, sink_ref[4 * g + half], sink_ref[4 * g + 2 + half]) * LOG2E
                m = jnp.maximum(s.max(axis=-1, keepdims=True), sink)
                po = _dot(jnp.exp2(s - m).astype(BF16), vs)
                halves.append(po / (po[:, sum_lane:sum_lane + 1] + jnp.exp2(sink - m)))
            out = jnp.where(low, halves[0], halves[1])
            o_ref[rows, 256 * g:256 * g + 128] = out[0:tq].astype(BF16)
            o_ref[rows, 256 * g + 128:256 * g + 256] = out[tq:2 * tq].astype(BF16)


def _swa_attn(sink, sq, sk, sv, cache_k, cache_v, latent):
    smem = pl.BlockSpec(memory_space=pltpu.SMEM)
    if latent:
        tq = 2 * SWA_WINDOW
        n_b, n_qb = DEC_BATCH, DEC_SEQ // tq
        base = N_CTX // tq
        last = DEC_SEQ // SWA_WINDOW - 1

        def prev(b, i):
            return (2 * (base + b * n_qb) + jnp.maximum(2 * i - 1, 0), 0)

        def cur(b, i):
            return (base + b * n_qb + i, 0)

        def nxt(b, i):
            return (2 * (base + b * n_qb) + jnp.minimum(2 * i + 2, last), 0)

        cache = pl.BlockSpec((None, PAST_LEN, 128), lambda b, i: (b, 0, 0))
        edge = lambda f: pl.BlockSpec((SWA_WINDOW, 128), f)
        kv_specs = [cache, edge(prev), pl.BlockSpec((tq, 128), cur), edge(nxt)] * 2
        args = (cache_k, sk, sk, sk, cache_v, sv, sv, sv)
        q_spec = pl.BlockSpec((tq, 512), cur)
        o_spec = pl.BlockSpec((tq, 512), lambda b, i: (b * n_qb + i, 0))
    else:
        tq = SEQ
        n_b, n_qb = BATCH, 1
        blk = pl.BlockSpec((tq, 128), lambda b, i: (b, 0))
        kv_specs = [blk, blk]
        args = (sk, sv)
        q_spec = pl.BlockSpec((tq, 512), lambda b, i: (b, 0))
        o_spec = q_spec
    return pl.pallas_call(
        functools.partial(_swa_kernel, latent, n_qb),
        grid=(n_b, n_qb),
        in_specs=[smem, q_spec] + kv_specs,
        out_specs=o_spec,
        out_shape=jax.ShapeDtypeStruct((n_b * n_qb * tq, 512), BF16),
        compiler_params=_cparams(2),
        name="swa_lat" if latent else "swa_ctx",
    )(sink, sq, *args)


def _route(h, rwt_ref, rb_ref, tri_ref, carry):
    gsz = N_EXPERTS // N_EXPERT_GROUPS
    scores = jax.nn.sigmoid(_dot_nt(rwt_ref[...], h))
    biased = scores + rb_ref[...]
    mem = lax.broadcasted_iota(jnp.int32, (gsz, TM), 0).astype(F32)
    gs_rows = []
    for g in range(N_EXPERT_GROUPS):
        bg = biased[g * gsz:(g + 1) * gsz, :]
        m1 = bg.max(axis=0, keepdims=True)
        first = jnp.min(jnp.where(bg == m1, mem, float(gsz)), axis=0, keepdims=True)
        m2 = jnp.where(mem == first, -jnp.inf, bg).max(axis=0, keepdims=True)
        gs_rows.append(m1 + m2)
    gs = jnp.concatenate(gs_rows, axis=0)
    gid = lax.broadcasted_iota(jnp.int32, gs.shape, 0).astype(F32)
    gsel = jnp.zeros(gs.shape, F32)
    for _ in range(TOPK_GROUPS):
        mx = gs.max(axis=0, keepdims=True)
        pick = gid == jnp.min(jnp.where(gs == mx, gid, float(N_EXPERT_GROUPS)), axis=0, keepdims=True)
        gsel = jnp.where(pick, 1.0, gsel)
        gs = jnp.where(pick, -jnp.inf, gs)
    emask = jnp.concatenate(
        [jnp.broadcast_to(gsel[g:g + 1, :], (gsz, TM)) for g in range(N_EXPERT_GROUPS)], axis=0)
    cand = jnp.where(emask > 0.5, biased, NEG_INF)
    eid = lax.broadcasted_iota(jnp.int32, cand.shape, 0).astype(F32)
    picks = []
    self32 = jnp.zeros(cand.shape, F32)
    for _ in range(TOP_K):
        mx = cand.max(axis=0, keepdims=True)
        pick = eid == jnp.min(jnp.where(cand == mx, eid, float(N_EXPERTS)), axis=0, keepdims=True)
        picks.append(pick)
        self32 = jnp.where(pick, 1.0, self32)
        cand = jnp.where(pick, -jnp.inf, cand)
    pos = _dot(self32.astype(BF16), tri_ref[...]) + carry
    sel_scores = [jnp.sum(jnp.where(p, scores, 0.0), axis=0, keepdims=True) for p in picks]
    wsum = functools.reduce(lambda a, b: a + b, sel_scores)
    zero_f = jnp.zeros((2, TM), F32)
    eidx = [jnp.sum(jnp.where(p, eid, 0.0), axis=0, keepdims=True) for p in picks]
    epos = [jnp.sum(jnp.where(p, pos, 0.0), axis=0, keepdims=True) for p in picks]
    ew = [s / wsum * ROUTED_SCALE for s in sel_scores]
    return (jnp.concatenate(eidx + [zero_f], axis=0).astype(jnp.int32),
            jnp.concatenate(epos + [zero_f], axis=0).astype(jnp.int32),
            jnp.concatenate(ew + [zero_f], axis=0),
            carry + jnp.sum(self32, axis=1, keepdims=True))


def _stage_e_kernel(xc_ref, xl_ref, mod_ref, g1_ref, g2_ref, fnc_ref, fnl_ref, omc_ref, oml_ref, osc_ref, osl_ref,
                    gates_ref, wf_ref, wm_ref, ws_ref, wo_ref, rwt_ref, rb_ref, tri_ref,
                    x1_ref, h2a_ref, h2b_ref, eidx_ref, epos_ref, ew_ref, cnt_ref, carry_ref):
    @pl.when(pl.program_id(0) == 0)
    def _():
        carry_ref[...] = jnp.zeros_like(carry_ref)

    is_ctx = pl.program_id(0) < NB_CTX
    carry = carry_ref[...]
    for r, rows in enumerate(_chunks()):
        fn = jnp.where(is_ctx, fnc_ref[rows, :], fnl_ref[rows, :])
        om = jnp.where(is_ctx, omc_ref[rows, :], oml_ref[rows, :])
        osw = jnp.where(is_ctx, osc_ref[rows, :], osl_ref[rows, :])
        merged = (gates_ref[rows, 0:1024].astype(F32) * _dot(fn, wf_ref[...])
                  + gates_ref[rows, 1024:2048].astype(F32) * _dot(om, wm_ref[...])
                  + gates_ref[rows, 2048:3072].astype(F32) * _dot(osw, ws_ref[...]))
        mix = _dot(merged.astype(BF16), wo_ref[...])
        x = jnp.where(is_ctx, xc_ref[rows, :], xl_ref[rows, :])
        x1 = x + mod_ref[:, 2048:3072] * _rms_rows(mix, g1_ref[...])
        x1_ref[rows, :] = x1
        h2 = _rms_rows(x1, g2_ref[...]) * (1.0 + mod_ref[:, 4096:5120]) + mod_ref[:, 3072:4096]
        h2a_ref[rows, :], h2b_ref[rows, :] = _pack_pair(h2)
        cols = pl.ds(r * TM, TM)
        eidx_ref[:, cols], epos_ref[:, cols], ew_ref[:, cols], carry = _route(
            h2.astype(BF16), rwt_ref, rb_ref, tri_ref, carry)
    carry_ref[...] = carry
    cnt_ref[...] = jnp.broadcast_to(carry, cnt_ref.shape).astype(jnp.int32)


def _stage_e(layer, xc, xl, modt, g1, g2, mixed, gates, wf, wm, ws, wo, rwt, rbias, tri):
    row = lambda w: pl.BlockSpec((TB, w), lambda i: (i, 0))
    ctx, lat = _ctx_rows(512), _lat_rows(512)
    col = lambda dt: (pl.BlockSpec((8, TB), lambda i: (0, i)), jax.ShapeDtypeStruct((8, N_TOK), dt))
    picks = [col(jnp.int32), col(jnp.int32), col(F32)]
    return pl.pallas_call(
        _stage_e_kernel,
        grid=(NB,),
        in_specs=[
            _ctx_rows(D_MODEL), _lat_rows(D_MODEL),
            pl.BlockSpec((None, 1, N_MOD * D_MODEL), lambda i: (i, 0, 0)),
            _const_spec((1, D_MODEL)), _const_spec((1, D_MODEL)),
            ctx, lat, ctx, lat, ctx, lat, row(3 * D_MODEL),
            _layer_spec((512, D_MODEL), layer), _layer_spec((512, D_MODEL), layer),
            _layer_spec((512, D_MODEL), layer), _layer_spec((D_MODEL, D_MODEL), layer),
            _const_spec((N_EXPERTS, D_MODEL)), _const_spec((N_EXPERTS, 1)), _const_spec((TM, TM)),
        ],
        out_specs=[row(D_MODEL), row(PACKED), row(PACKED)] + [s for s, _ in picks]
        + [_const_spec((N_EXPERTS, LANES))],
        out_shape=[jax.ShapeDtypeStruct((N_TOK, D_MODEL), F32),
                   jax.ShapeDtypeStruct((N_TOK, PACKED), jnp.int32),
                   jax.ShapeDtypeStruct((N_TOK, PACKED), jnp.int32)] + [s for _, s in picks]
        + [jax.ShapeDtypeStruct((N_EXPERTS, LANES), jnp.int32)],
        scratch_shapes=[pltpu.VMEM((N_EXPERTS, 1), F32)],
        compiler_params=_cparams(1),
        name="stage_e",
    )(xc, xl, modt, g1, g2, *mixed, gates, wf, wm, ws, wo, rwt, rbias, tri)


def _expert_kernel(te_ref, tv_ref, par_ref, xa_ref, xb_ref, w1_ref, w3_ref, w2_ref, oa_ref, ob_ref,
                   w1b_ref, w3b_ref, w2b_ref):
    s = pl.program_id(0)
    t = jnp.maximum(s - 1, 0)
    prev = jnp.maximum(s - 2, 0)

    starts_expert = jnp.logical_or(s == 0, te_ref[s] != te_ref[t])

    @pl.when(jnp.logical_and(tv_ref[s] > 0, starts_expert))
    def _():
        w1b_ref[par_ref[s]] = w1_ref[...].astype(BF16)
        w3b_ref[par_ref[s]] = w3_ref[...].astype(BF16)

    live = jnp.logical_and(s > 0, tv_ref[t] > 0)

    @pl.when(jnp.logical_and(live, jnp.logical_or(s == 1, te_ref[t] != te_ref[prev])))
    def _():
        w2b_ref[...] = w2_ref[...].astype(BF16)

    def run(n_chunks):
        p = par_ref[t]
        for r in range(n_chunks):
            rows = pl.ds(r * EXPERT_ROWS, EXPERT_ROWS)
            x = _unpack_pair(xa_ref[rows, :], xb_ref[rows, :]).astype(BF16)
            hg = _dot(x, w1b_ref[p])
            hu = _dot(x, w3b_ref[p])
            act = (jax.nn.silu(hg) * hu).astype(BF16)
            oa_ref[rows, :], ob_ref[rows, :] = _pack_pair(_dot(act, w2b_ref[...]))

    for n_chunks in range(1, TE // EXPERT_ROWS + 1):
        pl.when(jnp.logical_and(live, tv_ref[t] == n_chunks))(functools.partial(run, n_chunks))


def _experts(layer, tile_expert, tile_chunks, tile_parity, xsa, xsb, w1, w3, w2):
    slot_rows = pl.BlockSpec((TE, PACKED), lambda s, te, tv, par: (jnp.maximum(s - 1, 0), 0))
    ahead = lambda s, te, tv, par: (layer, te[s], 0, 0)
    current = lambda s, te, tv, par: (layer, te[jnp.maximum(s - 1, 0)], 0, 0)
    grid_spec = pltpu.PrefetchScalarGridSpec(
        num_scalar_prefetch=3,
        grid=(NTE + 1,),
        in_specs=[
            slot_rows, slot_rows,
            pl.BlockSpec((None, None, D_MODEL, EXPERT_FF), ahead),
            pl.BlockSpec((None, None, D_MODEL, EXPERT_FF), ahead),
            pl.BlockSpec((None, None, EXPERT_FF, D_MODEL), current),
        ],
        out_specs=[slot_rows, slot_rows],
        scratch_shapes=[pltpu.VMEM((2, D_MODEL, EXPERT_FF), BF16), pltpu.VMEM((2, D_MODEL, EXPERT_FF), BF16),
                        pltpu.VMEM((EXPERT_FF, D_MODEL), BF16)],
    )
    return pl.pallas_call(
        _expert_kernel,
        grid_spec=grid_spec,
        out_shape=[jax.ShapeDtypeStruct((S_MAX, PACKED), jnp.int32)] * 2,
        compiler_params=_cparams(1),
        name="experts",
    )(tile_expert, tile_chunks, tile_parity, xsa, xsb, w1, w3, w2)


def _sc_mesh():
    return plsc.VectorSubcoreMesh(core_axis_name="c", subcore_axis_name="s",
                                  num_cores=SC_CORES, num_subcores=SC_SUBCORES)


def _sc_scatter_rows(rows, slot8):
    @functools.partial(pl.kernel, mesh=_sc_mesh(), scratch_types=[pltpu.SemaphoreType.DMA],
                       out_type=jax.ShapeDtypeStruct((S_MAX, PACKED), jnp.int32))
    def scatter(x_hbm, i_hbm, o_hbm, sem):
        def body(x_vmem, i_vmem):
            copies = [pltpu.async_copy(x_vmem, o_hbm.at[i_vmem.at[k]], sem) for k in range(TOP_K)]
            for cp in copies:
                cp.wait()

        pltpu.emit_pipeline(
            body,
            grid=(N_TOK // SC_ROWS,),
            in_specs=[pl.BlockSpec((SC_ROWS, PACKED), lambda i: (i, 0)),
                      pl.BlockSpec((8, SC_ROWS), lambda i: (0, i))],
            out_specs=[],
            core_axis_name=("c", "s"),
            dimension_semantics=(pltpu.PARALLEL,),
        )(x_hbm, i_hbm)

    return scatter(rows, slot8)


def _sc_gather_rows(table, idx):
    n = idx.shape[1]

    @functools.partial(pl.kernel, mesh=_sc_mesh(), scratch_types=[],
                       out_type=jax.ShapeDtypeStruct((n, PACKED), jnp.int32))
    def gather(t_hbm, i_hbm, o_hbm):
        def body(i_vmem, o_vmem):
            pltpu.sync_copy(t_hbm.at[i_vmem.at[0]], o_vmem)

        pltpu.emit_pipeline(
            body,
            grid=(n // SC_ROWS,),
            in_specs=[pl.BlockSpec((1, SC_ROWS), lambda i: (0, i))],
            out_specs=[pl.BlockSpec((SC_ROWS, PACKED), lambda i: (i, 0))],
            core_axis_name=("c", "s"),
            dimension_semantics=(pltpu.PARALLEL,),
        )(i_hbm, o_hbm)

    return gather(table, idx)


def _stage_g_kernel(x1_ref, mod_ref, g3_ref, yga_ref, ygb_ref, ew_ref, ha_ref, hb_ref,
                    s1_ref, s3_ref, s2_ref, oc_ref, ol_ref):
    is_ctx = pl.program_id(0) < NB_CTX
    for rows in _chunks():
        h = _unpack_pair(ha_ref[rows, :], hb_ref[rows, :]).astype(BF16)
        act = jax.nn.silu(_dot(h, s1_ref[...])) * _dot(h, s3_ref[...])
        y = _dot(act.astype(BF16), s2_ref[...])
        for k in range(TOP_K):
            y = y + ew_ref[rows, k:k + 1] * _unpack_pair(yga_ref[k, rows, :], ygb_ref[k, rows, :])
        out = x1_ref[rows, :] + mod_ref[:, 5120:6144] * _rms_rows(y, g3_ref[...])

        @pl.when(is_ctx)
        def _():
            oc_ref[rows, :] = out

        @pl.when(jnp.logical_not(is_ctx))
        def _():
            ol_ref[rows, :] = out


def _stage_g(layer, x1, modt, g3, yga, ygb, ew_rows, h2a, h2b, s1, s3, s2):
    row = lambda w: pl.BlockSpec((TB, w), lambda i: (i, 0))
    picked = pl.BlockSpec((TOP_K, TB, PACKED), lambda i: (0, i, 0))
    return pl.pallas_call(
        _stage_g_kernel,
        grid=(NB,),
        in_specs=[row(D_MODEL), pl.BlockSpec((None, 1, N_MOD * D_MODEL), lambda i: (i, 0, 0)),
                  _const_spec((1, D_MODEL)), picked, picked, row(8), row(PACKED), row(PACKED),
                  _layer_spec((D_MODEL, SHARED_FF), layer), _layer_spec((D_MODEL, SHARED_FF), layer),
                  _layer_spec((SHARED_FF, D_MODEL), layer)],
        out_specs=[_ctx_rows(D_MODEL), _lat_rows(D_MODEL)],
        out_shape=[jax.ShapeDtypeStruct((N_CTX, D_MODEL), F32),
                   jax.ShapeDtypeStruct((N_LAT, D_MODEL), F32)],
        compiler_params=_cparams(1),
        name="stage_g",
    )(x1, modt, g3, yga, ygb, ew_rows, h2a, h2b, s1, s3, s2)


def _rope_tables():
    t = np.arange(DEC_SEQ)
    pos = np.stack([(t // GRID_W), (t % GRID_W)], axis=-1).astype(np.float32)

    def table(r):
        n_freq = r // 4
        inv = np.float32(ROPE_BASE) ** (-np.arange(n_freq, dtype=np.float32) / np.float32(n_freq))
        ang = pos[:, :, None] * inv.astype(np.float32)
        cos = np.cos(ang)
        sin = np.sin(ang)
        cos_t = np.stack([cos, cos], axis=2).reshape(DEC_SEQ, r)
        sin_t = np.stack([-sin, sin], axis=2).reshape(DEC_SEQ, r)
        return cos_t, sin_t

    c64, s64 = table(SWA_HEAD_DIM)
    c32, s32 = table(MLA_ROPE)
    lat = np.concatenate([np.tile(c64, (1, 8)), np.tile(s64, (1, 8)),
                          np.tile(c32, (1, 4)), np.tile(s32, (1, 4))], axis=1)
    ident = np.concatenate([np.ones((TB, 512)), np.zeros((TB, 512)),
                            np.ones((TB, 128)), np.zeros((TB, 128))], axis=1)
    return jnp.asarray(np.concatenate([ident, lat], axis=0).astype(np.float32))


def _dft_pair(n):
    k = np.arange(n, dtype=np.int64)
    ang = ((k[:, None] * k[None, :]) % n).astype(np.float64) * (2.0 * math.pi / n)
    return np.cos(ang), np.sin(ang)


def _fnet_tables():
    c64, s64 = _dft_pair(FNET_GROUP_DIM)
    eye = np.eye(FNET_GROUPS)
    bd = np.concatenate([np.kron(eye, c64), np.kron(eye, s64)], axis=1)
    mats = []
    for t_len in (SEQ, DEC_SEQ):
        c, s = _dft_pair(t_len)
        mats.append(np.concatenate([c, -s], axis=1))
    return tuple(jnp.asarray(m.astype(np.float32).astype(BF16)) for m in (bd, mats[0], mats[1]))


def _layer_weights(l, w_in, w_uq, w_ukv):
    w = w_in[l]
    wide = jnp.concatenate([w[:, 0:1024], w[:, 1056:1824], w[:, 1024:1056],
                            jnp.zeros((D_MODEL, 96), F32)], axis=1).astype(BF16)

    uq = w_uq[l].reshape(MLA_Q_RANK, MLA_HEADS, MLA_NOPE + MLA_ROPE)
    z32 = jnp.zeros((MLA_Q_RANK, MLA_HEADS, 32), F32)
    wqa = jnp.concatenate([uq, z32], axis=2).reshape(MLA_Q_RANK, 1024).astype(BF16)
    ukv = w_ukv[l].reshape(MLA_KV_RANK, MLA_HEADS, MLA_NOPE + MLA_V)
    wk = jnp.concatenate([ukv[:, :, :MLA_NOPE], jnp.zeros((MLA_KV_RANK, MLA_HEADS, 64), F32)],
                         axis=2).reshape(MLA_KV_RANK, 1024).astype(BF16)
    wv = ukv[:, :, MLA_NOPE:].reshape(MLA_KV_RANK, 512).astype(BF16)
    return wide, wqa, wk, wv


def _rope_placement():
    e = np.zeros((128, 1024), np.float32)
    for hd in range(MLA_HEADS):
        for i in range(MLA_ROPE):
            e[i, hd * 128 + MLA_NOPE + i] = 1.0
    return jnp.asarray(e, BF16)


def _moe_dispatch_plan(eidx, epos, counts):
    padded = ((counts + TE - 1) // TE) * TE
    ends = jnp.cumsum(padded)
    offs = ends - padded
    ids = jnp.arange(N_EXPERTS, dtype=jnp.int32)
    picked_off = jnp.sum(jnp.where(eidx[:, :, None] == ids, offs, 0), axis=-1)
    slot = picked_off + epos
    starts = jnp.arange(NTE, dtype=jnp.int32) * TE
    tile_expert = jnp.sum((ends[None, :] <= starts[:, None]).astype(jnp.int32), axis=1)
    tile_expert = jnp.minimum(tile_expert, N_EXPERTS - 1)
    pick = tile_expert[:, None] == ids[None, :]
    last_real = jnp.sum(jnp.where(pick, (offs + counts)[None, :], 0), axis=1)
    n_real = jnp.clip(last_real - starts, 0, TE)
    n_real = jnp.where(starts < ends[-1], n_real, 0)
    tile_chunks = ((n_real + EXPERT_ROWS - 1) // EXPERT_ROWS).astype(jnp.int32)
    rank = jnp.cumsum((counts > 0).astype(jnp.int32)) - 1
    tile_parity = jnp.sum(jnp.where(pick, rank[None, :], 0), axis=1) % 2
    close = lambda a, v: jnp.concatenate([a, jnp.full((1,), v, jnp.int32)])
    return (slot, close(tile_expert, N_EXPERTS - 1), close(tile_chunks, 0),
            close(tile_parity.astype(jnp.int32), 0))


def kernel(x_prompt, x_sample, cache_mla_ckv, cache_mla_krope, cache_swa_k, cache_swa_v, c, c_ctx,
           ada_w, ada_b, norm_g, w_in, q_norm, kv_norm, w_fnet, w_uq, w_ukv, w_mla_o, swa_sink,
           w_swa_o, w_gate, b_gate, w_out, router_w, router_bias, exp_w1, exp_w3, exp_w2,
           shared_w1, shared_w3, shared_w2):
    xc = x_prompt.reshape(N_CTX, D_MODEL)
    xl = x_sample.reshape(N_LAT, D_MODEL)

    cond8 = jnp.concatenate([c_ctx[None, :], c, jnp.zeros((3, D_MODEL), F32)], axis=0)
    mod = _modulation(cond8, ada_w, ada_b)
    tile_cond = np.concatenate([np.zeros(NB_CTX, np.int32),
                                1 + np.arange(NB - NB_CTX, dtype=np.int32) // LAT_BLOCKS])

    tab = _rope_tables()
    bd, f_ctx, f_lat = _fnet_tables()
    e_mat = _rope_placement()
    tri = jnp.asarray(np.triu(np.ones((TM, TM), np.float32), 1), BF16)
    w_gate_b, w_fnet_b, w_mla_o_b, w_swa_o_b, w_out_b, sw1_b, sw3_b, sw2_b = (
        w.astype(BF16) for w in (w_gate, w_fnet, w_mla_o, w_swa_o, w_out, shared_w1, shared_w3, shared_w2))

    new_ckv, new_kr, new_k, new_v = [], [], [], []
    for l in range(DEPTH):
        modt = mod[l][tile_cond][:, None, :]
        wide, wqa, wk, wv = _layer_weights(l, w_in, w_uq, w_ukv)
        ng = norm_g[l]

        fin, ckv, kr, sq, sk, sv, gates, q_m, k_m, v_m = _stage_a(
            l, xc, xl, modt, ng[0:1], wide, w_gate_b, b_gate[l][None, :],
            q_norm[l][None, :], kv_norm[l][None, :], tab, wqa, wk, e_mat, wv)

        new_ckv.append(ckv[:N_CTX].reshape(BATCH, SEQ, MLA_KV_RANK))
        new_kr.append(kr[:N_CTX, :MLA_ROPE].reshape(BATCH, SEQ, MLA_ROPE))
        new_k.append(sk[:N_CTX].reshape(BATCH, SEQ, SWA_KV_HEADS, SWA_HEAD_DIM))
        new_v.append(sv[:N_CTX].reshape(BATCH, SEQ, SWA_KV_HEADS, SWA_HEAD_DIM))

        fn = (_fnet(fin, f_ctx, bd, BATCH, SEQ, 0),
              _fnet(fin, f_lat, bd, DEC_BATCH, DEC_SEQ, N_CTX // DEC_SEQ))

        kr_cache = jnp.pad(cache_mla_krope[:, l].reshape(N_CACHE, MLA_ROPE), ((0, 0), (0, 96)))
        k_c, v_c = _mla_cache_kv(cache_mla_ckv[:, l].reshape(N_CACHE, MLA_KV_RANK), kr_cache,
                                 wk, e_mat, wv)
        om = (_mla_attn(q_m, k_m, v_m, k_c, v_c, latent=False),
              _mla_attn(q_m, k_m, v_m, k_c, v_c, latent=True))

        ck = cache_swa_k[:, l].reshape(DEC_BATCH, PAST_LEN, 128)
        cv = cache_swa_v[:, l].reshape(DEC_BATCH, PAST_LEN, 128)
        osw = (_swa_attn(swa_sink[l], sq, sk, sv, ck, cv, latent=False),
               _swa_attn(swa_sink[l], sq, sk, sv, ck, cv, latent=True))

        x1, h2a, h2b, eidx, epos, ew, counts = _stage_e(
            l, xc, xl, modt, ng[1:2], ng[2:3], fn + om + osw, gates,
            w_fnet_b, w_mla_o_b, w_swa_o_b, w_out_b,
            router_w[l].T.astype(BF16), router_bias[l][:, None], tri)
        slot, tile_expert, tile_chunks, tile_parity = _moe_dispatch_plan(eidx, epos, counts[:, 0])
        xsa = _sc_scatter_rows(h2a, slot)
        xsb = _sc_scatter_rows(h2b, slot)
        ysa, ysb = _experts(l, tile_expert, tile_chunks, tile_parity, xsa, xsb, exp_w1, exp_w3, exp_w2)
        picks = slot[:TOP_K].reshape(1, TOP_K * N_TOK)
        yga = _sc_gather_rows(ysa, picks).reshape(TOP_K, N_TOK, PACKED)
        ygb = _sc_gather_rows(ysb, picks).reshape(TOP_K, N_TOK, PACKED)
        xc, xl = _stage_g(l, x1, modt, ng[3:4], yga, ygb, ew.T, h2a, h2b, sw1_b, sw3_b, sw2_b)

    y_p = xc.reshape(BATCH, SEQ, D_MODEL)
    y_s = xl.reshape(DEC_BATCH, DEC_SEQ, D_MODEL)
    return (y_p, y_s, jnp.stack(new_ckv, axis=1), jnp.stack(new_kr, axis=1),
            jnp.stack(new_k, axis=1), jnp.stack(new_v, axis=1))
```

```python
import functools
import math

import numpy as np
import jax
import jax.numpy as jnp
from jax import lax
from jax.experimental import pallas as pl
from jax.experimental.pallas import tpu as pltpu
from jax.experimental.pallas import tpu_sc as plsc

D_MODEL = 1024
BATCH = 16
SEQ = 256
DEPTH = 2
DEC_BATCH = 4
DEC_SEQ = 2048
PAST_LEN = 512
GRID_W = 64
EPS = 1e-6
ROPE_BASE = 10000.0
NEG_INF = -1e30

FNET_GROUPS = 8
FNET_GROUP_DIM = 64
FNET_WIDTH = 512
MLA_HEADS = 8
MLA_Q_RANK = 384
MLA_KV_RANK = 128
MLA_NOPE = 64
MLA_ROPE = 32
MLA_V = 64
MLA_SCALE = (MLA_NOPE + MLA_ROPE) ** -0.5
LOG2E = math.log2(math.e)
SWA_HEADS = 8
SWA_KV_HEADS = 2
SWA_HEAD_DIM = 64
SWA_WINDOW = 128
SWA_SCALE = SWA_HEAD_DIM ** -0.5
N_MOD = 6
N_EXPERTS = 64
N_EXPERT_GROUPS = 8
TOPK_GROUPS = 4
TOP_K = 6
EXPERT_FF = 256
SHARED_FF = 256
ROUTED_SCALE = 2.5

LANES = 128
TM = 256
N_CTX = BATCH * SEQ
N_LAT = DEC_BATCH * DEC_SEQ
N_TOK = N_CTX + N_LAT
N_CACHE = DEC_BATCH * PAST_LEN
NT_CTX = N_CTX // TM
NT_LAT = N_LAT // TM
NT = N_TOK // TM
LAT_TILES = DEC_SEQ // TM
TB = 512
NB = N_TOK // TB
NB_CTX = N_CTX // TB
LAT_BLOCKS = DEC_SEQ // TB
MLA_LAT_TQ = 256
MLA_LAT_PAIRS = 4
FNET_ROWS = 1024
TE = 512
S_MAX = N_TOK * TOP_K + N_EXPERTS * TE
NTE = S_MAX // TE
EXPERT_ROWS = 256
VMEM_LIMIT = 56 * 1024 * 1024
PACKED = D_MODEL // 4
SC_ROWS = 128
SC_CORES = 2
SC_SUBCORES = 16

A_F = (0, 512)
A_QD = (512, 896)
A_KV = (896, 1024)
A_SQ = (1024, 1536)
A_SK = (1536, 1664)
A_SV = (1664, 1792)
A_KR = (1792, 1920)
W_IN_WIDE = 1920
TAB_W = 1280

F32 = jnp.float32
BF16 = jnp.bfloat16


def _cparams(n_axes, parallel=False):
    sem = ("parallel" if parallel else "arbitrary",) * n_axes
    return pltpu.CompilerParams(dimension_semantics=sem, vmem_limit_bytes=VMEM_LIMIT)


def _dot(a, b):
    return jnp.dot(a, b, preferred_element_type=F32)


def _dot_nt(a, b):
    return lax.dot_general(a, b, (((1,), (1,)), ((), ())), preferred_element_type=F32)


def _rms_rows(v, g):
    return v * lax.rsqrt(jnp.mean(v * v, axis=-1, keepdims=True) + EPS) * g


def _pack_rows(v):
    half = v.shape[1] // 2
    lo = lax.bitcast_convert_type(v[:, :half].astype(BF16).astype(F32), jnp.int32)
    hi = lax.bitcast_convert_type(v[:, half:].astype(BF16).astype(F32), jnp.int32)
    return jnp.bitwise_or(jnp.bitwise_and(hi, -65536), jnp.bitwise_and(jnp.right_shift(lo, 16), 65535))


def _unpack_rows(w):
    lo = lax.bitcast_convert_type(jnp.left_shift(w, 16), F32)
    hi = lax.bitcast_convert_type(jnp.bitwise_and(w, -65536), F32)
    return jnp.concatenate([lo, hi], axis=1)


def _pack_pair(v):
    half = v.shape[1] // 2
    return _pack_rows(v[:, :half]), _pack_rows(v[:, half:])


def _unpack_pair(a, b):
    return jnp.concatenate([_unpack_rows(a), _unpack_rows(b)], axis=1)


def _const_spec(shape):
    return pl.BlockSpec(shape, lambda *_: (0,) * len(shape))


def _layer_spec(shape, layer):
    return pl.BlockSpec((None,) + shape, lambda *_: (layer,) + (0,) * len(shape))


def _ctx_rows(width):
    return pl.BlockSpec((TB, width), lambda i: (jnp.minimum(i, NB_CTX - 1), 0))


def _lat_rows(width):
    return pl.BlockSpec((TB, width), lambda i: (jnp.maximum(i - NB_CTX, 0), 0))


def _tab_row_block(i):
    return jnp.where(i < NB_CTX, 0, 1 + (i - NB_CTX) % LAT_BLOCKS)


def _mod_kernel(cond_ref, w_ref, b_ref, o_ref):
    c = cond_ref[...]
    a = (c * jax.nn.sigmoid(c)).astype(BF16)
    o_ref[...] = _dot(a, w_ref[...].astype(BF16)) + b_ref[...]


def _modulation(cond8, ada_w, ada_b):
    tn = 512
    nj = N_MOD * D_MODEL // tn
    return pl.pallas_call(
        _mod_kernel,
        grid=(DEPTH, nj),
        in_specs=[
            pl.BlockSpec((8, D_MODEL), lambda l, j: (0, 0)),
            pl.BlockSpec((None, D_MODEL, tn), lambda l, j: (l, 0, j)),
            pl.BlockSpec((None, 1, tn), lambda l, j: (l, 0, j)),
        ],
        out_specs=pl.BlockSpec((None, 8, tn), lambda l, j: (l, 0, j)),
        out_shape=jax.ShapeDtypeStruct((DEPTH, 8, N_MOD * D_MODEL), F32),
        compiler_params=_cparams(2),
        name="modulation",
    )(cond8, ada_w, ada_b.reshape(DEPTH, 1, N_MOD * D_MODEL))


def _half_swap(x, half):
    n = x.shape[1]
    lane = lax.broadcasted_iota(jnp.int32, (1, n), 1)
    return jnp.where((lane & half) == 0, pltpu.roll(x, n - half, 1), pltpu.roll(x, half, 1))


def _mla_expand(rows, cq, ckv, kr, cos32, sin32, wqa_ref, wk_ref, e_ref, wv_ref,
                q_ref, k_ref, v_ref):
    if q_ref is not None:
        lane = lax.broadcasted_iota(jnp.int32, (1, LANES), 1)
        rope_lane = jnp.logical_and(lane >= MLA_NOPE, lane < MLA_NOPE + MLA_ROPE)
        cos_h = jnp.where(rope_lane, cos32, 1.0)
        sin_h = jnp.where(rope_lane, sin32, 0.0)
        for hd in range(MLA_HEADS):
            lo, hi = hd * LANES, (hd + 1) * LANES
            q = _dot(cq, wqa_ref[:, lo:hi])
            q = q * cos_h + _half_swap(q, MLA_ROPE // 4) * sin_h
            q_ref[rows, lo:hi] = (q * (MLA_SCALE * LOG2E)).astype(BF16)
    k_ref[rows, :] = (_dot(ckv, wk_ref[...]) + _dot(kr, e_ref[...])).astype(BF16)
    v_ref[rows, :] = _dot(ckv, wv_ref[...]).astype(BF16)


def _chunks():
    return [pl.ds(r * TM, TM) for r in range(TB // TM)]


def _stage_a_kernel(xc_ref, xl_ref, mod_ref, g_ref, win_ref, wg_ref, bg_ref, qn_ref, kvn_ref, tab_ref,
                    wqa_ref, wk_ref, e_ref, wv_ref,
                    fin_ref, ckv_ref, kr_ref, sq_ref, sk_ref, sv_ref, gates_ref,
                    qm_ref, km_ref, vm_ref):
    is_ctx = pl.program_id(0) < NB_CTX
    for rows in _chunks():
        x = jnp.where(is_ctx, xc_ref[rows, :], xl_ref[rows, :])
        h = (_rms_rows(x, g_ref[...]) * (1.0 + mod_ref[:, 1024:2048]) + mod_ref[:, 0:1024]).astype(BF16)

        def proj(seg):
            return _dot(h, win_ref[:, seg[0]:seg[1]])

        fin_ref[rows, :] = proj(A_F).astype(BF16)
        cq = _rms_rows(proj(A_QD), qn_ref[...]).astype(BF16)
        ckv = _rms_rows(proj(A_KV), kvn_ref[...])
        ckv_ref[rows, :] = ckv
        cos64 = tab_ref[rows, 0:512]
        sin64 = tab_ref[rows, 512:1024]
        sq = proj(A_SQ)
        sq = sq * cos64 + _half_swap(sq, SWA_HEAD_DIM // 4) * sin64
        sq_ref[rows, :] = (sq * (SWA_SCALE * LOG2E)).astype(BF16)
        sk = proj(A_SK)
        sk_ref[rows, :] = sk * cos64[:, 0:128] + _half_swap(sk, SWA_HEAD_DIM // 4) * sin64[:, 0:128]
        sv_ref[rows, :] = proj(A_SV)
        cos32 = tab_ref[rows, 1024:1152]
        sin32 = tab_ref[rows, 1152:1280]
        kr = proj(A_KR)
        kr = kr * cos32 + _half_swap(kr, MLA_ROPE // 4) * sin32
        kr_ref[rows, :] = kr
        _mla_expand(rows, cq, ckv.astype(BF16), kr.astype(BF16), cos32, sin32,
                    wqa_ref, wk_ref, e_ref, wv_ref, qm_ref, km_ref, vm_ref)
        for c in range(3):
            lo, hi = c * D_MODEL, (c + 1) * D_MODEL
            gates_ref[rows, lo:hi] = jax.nn.sigmoid(_dot(h, wg_ref[:, lo:hi]) + bg_ref[:, lo:hi]).astype(BF16)


def _stage_a(layer, xc, xl, modt, g0, w_in_wide, w_gate, b_gate, q_norm, kv_norm, tab, wqa, wk, e_mat, wv):
    row = lambda w: pl.BlockSpec((TB, w), lambda i: (i, 0))
    outs = [(512, BF16), (128, F32), (128, F32), (512, BF16), (128, F32), (128, F32),
            (3 * D_MODEL, BF16), (1024, BF16), (1024, BF16), (512, BF16)]
    return pl.pallas_call(
        _stage_a_kernel,
        grid=(NB,),
        in_specs=[
            _ctx_rows(D_MODEL), _lat_rows(D_MODEL),
            pl.BlockSpec((None, 1, N_MOD * D_MODEL), lambda i: (i, 0, 0)),
            _const_spec((1, D_MODEL)),
            _const_spec((D_MODEL, W_IN_WIDE)),
            _layer_spec((D_MODEL, 3 * D_MODEL), layer),
            _const_spec((1, 3 * D_MODEL)),
            _const_spec((1, MLA_Q_RANK)),
            _const_spec((1, MLA_KV_RANK)),
            pl.BlockSpec((TB, TAB_W), lambda i: (_tab_row_block(i), 0)),
            _const_spec((MLA_Q_RANK, 1024)),
            _const_spec((128, 1024)), _const_spec((128, 1024)), _const_spec((128, 512)),
        ],
        out_specs=[row(w) for w, _ in outs],
        out_shape=[jax.ShapeDtypeStruct((N_TOK, w), dt) for w, dt in outs],
        compiler_params=_cparams(1),
        name="stage_a",
    )(xc, xl, modt, g0, w_in_wide, w_gate, b_gate, q_norm, kv_norm, tab, wqa, wk, e_mat, wv)


def _fnet_kernel(t_len, scale, fin_ref, f_ref, bd_ref, o_ref, zz_ref):
    @pl.when(pl.program_id(1) == 0)
    def _():
        z = fin_ref[...]
        zz_ref[0:t_len, :] = _dot(z, bd_ref[:, 0:512]).astype(BF16)
        zz_ref[t_len:2 * t_len, :] = _dot(z, bd_ref[:, 512:1024]).astype(BF16)

    o_ref[...] = (_dot(f_ref[...], zz_ref[...]) * scale).astype(BF16)


def _fnet(fin, fmat, bd, n_batch, t_len, row_block0):
    scale = 1.0 / math.sqrt(t_len * FNET_GROUP_DIM)
    ft = min(t_len, FNET_ROWS)
    return pl.pallas_call(
        functools.partial(_fnet_kernel, t_len, scale),
        grid=(n_batch, t_len // ft),
        in_specs=[
            pl.BlockSpec((t_len, FNET_WIDTH), lambda b, i: (row_block0 + b, 0)),
            pl.BlockSpec((ft, 2 * t_len), lambda b, i: (i, 0)),
            _const_spec((FNET_WIDTH, 2 * FNET_WIDTH)),
        ],
        out_specs=pl.BlockSpec((ft, FNET_WIDTH), lambda b, i: (b * (t_len // ft) + i, 0)),
        out_shape=jax.ShapeDtypeStruct((n_batch * t_len, FNET_WIDTH), BF16),
        scratch_shapes=[pltpu.VMEM((2 * t_len, FNET_WIDTH), BF16)],
        compiler_params=_cparams(2),
        name=f"fnet_{t_len}",
    )(fin, fmat, bd)


def _mla_cache_kernel(ckv_ref, kr_ref, wk_ref, e_ref, wv_ref, k_ref, v_ref):
    _mla_expand(slice(None), None, ckv_ref[...].astype(BF16), kr_ref[...].astype(BF16), None, None,
                None, wk_ref, e_ref, wv_ref, None, k_ref, v_ref)


def _mla_cache_kv(ckv_cache, kr_cache, wk, e_mat, wv):
    row = lambda w: pl.BlockSpec((TM, w), lambda i: (i, 0))
    return pl.pallas_call(
        _mla_cache_kernel,
        grid=(N_CACHE // TM,),
        in_specs=[row(128), row(128),
                  _const_spec((128, 1024)), _const_spec((128, 1024)), _const_spec((128, 512))],
        out_specs=[row(1024), row(512)],
        out_shape=[jax.ShapeDtypeStruct((N_CACHE, 1024), BF16),
                   jax.ShapeDtypeStruct((N_CACHE, 512), BF16)],
        compiler_params=_cparams(1),
        name="mla_cache_kv",
    )(ckv_cache, kr_cache, wk, e_mat, wv)


def _mla_attn_kernel(n_seg, pairs, q_ref, *refs):
    k_refs = refs[0:n_seg]
    v_refs = refs[n_seg:2 * n_seg]
    o_ref = refs[2 * n_seg]
    lane = lax.broadcasted_iota(jnp.int32, (1, LANES), 1)
    low = lane < MLA_V
    for pr in range(pairs):
        outs = []
        for hh in range(2):
            hd = 2 * pr + hh
            q = q_ref[:, hd * LANES:(hd + 1) * LANES]
            ss = [_dot_nt(q, k[:, hd * LANES:(hd + 1) * LANES]) for k in k_refs]
            m = functools.reduce(jnp.maximum, [s.max(axis=-1, keepdims=True) for s in ss])
            keep = low if hh == 0 else jnp.logical_not(low)
            sum_lane = MLA_V if hh == 0 else 0
            po = None
            for s, v_ref in zip(ss, v_refs):
                v = v_ref[:, pr * LANES:(pr + 1) * LANES]
                vm = jnp.where(lane == sum_lane, jnp.ones_like(v), jnp.where(keep, v, jnp.zeros_like(v)))
                t = _dot(jnp.exp2(s - m).astype(BF16), vm)
                po = t if po is None else po + t
            outs.append(po / po[:, sum_lane:sum_lane + 1])
        o_ref[:, pr * LANES:(pr + 1) * LANES] = jnp.where(low, outs[0], outs[1]).astype(BF16)


def _mla_attn(q_all, k_all, v_all, k_cache, v_cache, latent):
    if latent:
        tq, pairs = MLA_LAT_TQ, MLA_LAT_PAIRS
        n_b, n_q = DEC_BATCH, DEC_SEQ // tq
        q0 = N_CTX // tq
        kv_specs = [
            pl.BlockSpec((PAST_LEN, 256 * pairs), lambda b, hp, i: (b, hp)),
            pl.BlockSpec((DEC_SEQ, 256 * pairs), lambda b, hp, i: (N_CTX // DEC_SEQ + b, hp)),
            pl.BlockSpec((PAST_LEN, 128 * pairs), lambda b, hp, i: (b, hp)),
            pl.BlockSpec((DEC_SEQ, 128 * pairs), lambda b, hp, i: (N_CTX // DEC_SEQ + b, hp)),
        ]
        args = (q_all, k_cache, k_all, v_cache, v_all)
        n_seg = 2
    else:
        tq, pairs = SEQ, MLA_HEADS // 2
        n_b, n_q = BATCH, 1
        q0 = 0
        kv_specs = [
            pl.BlockSpec((SEQ, 256 * pairs), lambda b, hp, i: (b, hp)),
            pl.BlockSpec((SEQ, 128 * pairs), lambda b, hp, i: (b, hp)),
        ]
        args = (q_all, k_all, v_all)
        n_seg = 1
    return pl.pallas_call(
        functools.partial(_mla_attn_kernel, n_seg, pairs),
        grid=(n_b, MLA_HEADS // (2 * pairs), n_q),
        in_specs=[pl.BlockSpec((tq, 256 * pairs), lambda b, hp, i: (q0 + b * n_q + i, hp))] + kv_specs,
        out_specs=pl.BlockSpec((tq, 128 * pairs), lambda b, hp, i: (b * n_q + i, hp)),
        out_shape=jax.ShapeDtypeStruct((n_b * n_q * tq, MLA_HEADS * MLA_V), BF16),
        compiler_params=_cparams(3),
        name="mla_attn_lat" if latent else "mla_attn_ctx",
    )(*args)


def _swa_kernel(windowed, n_steps, sink_ref, q_ref, *refs):
    n_seg = 4 if windowed else 1
    k_refs = refs[0:n_seg]
    v_refs = refs[n_seg:2 * n_seg]
    o_ref = refs[2 * n_seg]
    step = pl.program_id(1)
    lane = lax.broadcasted_iota(jnp.int32, (1, LANES), 1)
    low = lane < SWA_HEAD_DIM
    high = jnp.logical_not(low)

    k_all = jnp.concatenate([r[...] for r in k_refs], axis=0)
    v_all = jnp.concatenate([r[...] for r in v_refs], axis=0)
    k_sw = pltpu.roll(k_all, SWA_HEAD_DIM, 1)
    v_sw = pltpu.roll(v_all, SWA_HEAD_DIM, 1)

    if windowed:
        tq = SWA_WINDOW
        qi = lax.broadcasted_iota(jnp.int32, (2 * tq, tq), 0) % tq
        kj = lax.broadcasted_iota(jnp.int32, (2 * tq, tq), 1)
        after = kj >= qi
        before = kj <= qi
        biases = [(jnp.where(jnp.logical_and(after, step > 0), 0.0, NEG_INF),
                   jnp.where(before, 0.0, NEG_INF)),
                  (jnp.where(after, 0.0, NEG_INF),
                   jnp.where(jnp.logical_and(before, step < n_steps - 1), 0.0, NEG_INF))]
    else:
        tq = q_ref.shape[0]
    n_sub = q_ref.shape[0] // tq
    top_rows = lax.broadcasted_iota(jnp.int32, (2 * tq, 1), 0) < tq

    for g in range(SWA_KV_HEADS):
        kh, vh = [], []
        for half in range(2):
            keep = low if half == 0 else high
            sum_lane = SWA_HEAD_DIM if half == 0 else 0
            straight = (g == half)
            kh.append(jnp.where(keep, k_all if straight else k_sw, 0.0).astype(BF16))
            vh.append(jnp.where(lane == sum_lane, 1.0,
                                jnp.where(keep, v_all if straight else v_sw, 0.0)).astype(BF16))
        for sub in range(n_sub):
            rows = slice(sub * tq, (sub + 1) * tq)
            qs = jnp.concatenate([q_ref[rows, 256 * g:256 * g + 128],
                                  q_ref[rows, 256 * g + 128:256 * g + 256]], axis=0)
            halves = []
            for half in range(2):
                sum_lane = SWA_HEAD_DIM if half == 0 else 0
                ks, vs = kh[half], vh[half]
                if windowed:
                    w0 = PAST_LEN + sub * tq
                    if sub == 0:
                        ks, vs = ks[0:w0 + 3 * tq], vs[0:w0 + 3 * tq]
                    else:
                        ks = jnp.concatenate([ks[0:PAST_LEN], ks[w0:w0 + 3 * tq]], axis=0)
                        vs = jnp.concatenate([vs[0:PAST_LEN], vs[w0:w0 + 3 * tq]], axis=0)
                s = _dot_nt(qs, ks)
                if windowed:
                    c0, c1, c2 = PAST_LEN, PAST_LEN + tq, PAST_LEN + 2 * tq
                    s = jnp.concatenate([s[:, :c0], s[:, c0:c1] + biases[sub][0], s[:, c1:c2],
                                         s[:, c2:] + biases[sub][1]], axis=1)
                sink = jnp.where(top_rows, sink_ref[4 * g + half], sink_ref[4 * g + 2 + half]) * LOG2E
                m = jnp.maximum(s.max(axis=-1, keepdims=True), sink)
                po = _dot(jnp.exp2(s - m).astype(BF16), vs)
                halves.append(po / (po[:, sum_lane:sum_lane + 1] + jnp.exp2(sink - m)))
            out = jnp.where(low, halves[0], halves[1])
            o_ref[rows, 256 * g:256 * g + 128] = out[0:tq].astype(BF16)
            o_ref[rows, 256 * g + 128:256 * g + 256] = out[tq:2 * tq].astype(BF16)


def _swa_attn(sink, sq, sk, sv, cache_k, cache_v, latent):
    smem = pl.BlockSpec(memory_space=pltpu.SMEM)
    if latent:
        tq = 2 * SWA_WINDOW
        n_b, n_qb = DEC_BATCH, DEC_SEQ // tq
        base = N_CTX // tq
        last = DEC_SEQ // SWA_WINDOW - 1

        def prev(b, i):
            return (2 * (base + b * n_qb) + jnp.maximum(2 * i - 1, 0), 0)

        def cur(b, i):
            return (base + b * n_qb + i, 0)

        def nxt(b, i):
            return (2 * (base + b * n_qb) + jnp.minimum(2 * i + 2, last), 0)

        cache = pl.BlockSpec((None, PAST_LEN, 128), lambda b, i: (b, 0, 0))
        edge = lambda f: pl.BlockSpec((SWA_WINDOW, 128), f)
        kv_specs = [cache, edge(prev), pl.BlockSpec((tq, 128), cur), edge(nxt)] * 2
        args = (cache_k, sk, sk, sk, cache_v, sv, sv, sv)
        q_spec = pl.BlockSpec((tq, 512), cur)
        o_spec = pl.BlockSpec((tq, 512), lambda b, i: (b * n_qb + i, 0))
    else:
        tq = SEQ
        n_b, n_qb = BATCH, 1
        blk = pl.BlockSpec((tq, 128), lambda b, i: (b, 0))
        kv_specs = [blk, blk]
        args = (sk, sv)
        q_spec = pl.BlockSpec((tq, 512), lambda b, i: (b, 0))
        o_spec = q_spec
    return pl.pallas_call(
        functools.partial(_swa_kernel, latent, n_qb),
        grid=(n_b, n_qb),
        in_specs=[smem, q_spec] + kv_specs,
        out_specs=o_spec,
        out_shape=jax.ShapeDtypeStruct((n_b * n_qb * tq, 512), BF16),
        compiler_params=_cparams(2),
        name="swa_lat" if latent else "swa_ctx",
    )(sink, sq, *args)


def _route(h, rwt_ref, rb_ref, tri_ref, carry):
    gsz = N_EXPERTS // N_EXPERT_GROUPS
    scores = jax.nn.sigmoid(_dot_nt(rwt_ref[...], h))
    biased = scores + rb_ref[...]
    mem = lax.broadcasted_iota(jnp.int32, (gsz, TM), 0).astype(F32)
    gs_rows = []
    for g in range(N_EXPERT_GROUPS):
        bg = biased[g * gsz:(g + 1) * gsz, :]
        m1 = bg.max(axis=0, keepdims=True)
        first = jnp.min(jnp.where(bg == m1, mem, float(gsz)), axis=0, keepdims=True)
        m2 = jnp.where(mem == first, -jnp.inf, bg).max(axis=0, keepdims=True)
        gs_rows.append(m1 + m2)
    gs = jnp.concatenate(gs_rows, axis=0)
    gid = lax.broadcasted_iota(jnp.int32, gs.shape, 0).astype(F32)
    gsel = jnp.zeros(gs.shape, F32)
    for _ in range(TOPK_GROUPS):
        mx = gs.max(axis=0, keepdims=True)
        pick = gid == jnp.min(jnp.where(gs == mx, gid, float(N_EXPERT_GROUPS)), axis=0, keepdims=True)
        gsel = jnp.where(pick, 1.0, gsel)
        gs = jnp.where(pick, -jnp.inf, gs)
    emask = jnp.concatenate(
        [jnp.broadcast_to(gsel[g:g + 1, :], (gsz, TM)) for g in range(N_EXPERT_GROUPS)], axis=0)
    cand = jnp.where(emask > 0.5, biased, NEG_INF)
    eid = lax.broadcasted_iota(jnp.int32, cand.shape, 0).astype(F32)
    picks = []
    self32 = jnp.zeros(cand.shape, F32)
    for _ in range(TOP_K):
        mx = cand.max(axis=0, keepdims=True)
        pick = eid == jnp.min(jnp.where(cand == mx, eid, float(N_EXPERTS)), axis=0, keepdims=True)
        picks.append(pick)
        self32 = jnp.where(pick, 1.0, self32)
        cand = jnp.where(pick, -jnp.inf, cand)
    pos = _dot(self32.astype(BF16), tri_ref[...]) + carry
    sel_scores = [jnp.sum(jnp.where(p, scores, 0.0), axis=0, keepdims=True) for p in picks]
    wsum = functools.reduce(lambda a, b: a + b, sel_scores)
    zero_f = jnp.zeros((2, TM), F32)
    eidx = [jnp.sum(jnp.where(p, eid, 0.0), axis=0, keepdims=True) for p in picks]
    epos = [jnp.sum(jnp.where(p, pos, 0.0), axis=0, keepdims=True) for p in picks]
    ew = [s / wsum * ROUTED_SCALE for s in sel_scores]
    return (jnp.concatenate(eidx + [zero_f], axis=0).astype(jnp.int32),
            jnp.concatenate(epos + [zero_f], axis=0).astype(jnp.int32),
            jnp.concatenate(ew + [zero_f], axis=0),
            carry + jnp.sum(self32, axis=1, keepdims=True))


def _stage_e_kernel(xc_ref, xl_ref, mod_ref, g1_ref, g2_ref, fnc_ref, fnl_ref, omc_ref, oml_ref, osc_ref, osl_ref,
                    gates_ref, wf_ref, wm_ref, ws_ref, wo_ref, rwt_ref, rb_ref, tri_ref,
                    x1_ref, h2_ref, eidx_ref, epos_ref, ew_ref, cnt_ref, carry_ref):
    @pl.when(pl.program_id(0) == 0)
    def _():
        carry_ref[...] = jnp.zeros_like(carry_ref)

    is_ctx = pl.program_id(0) < NB_CTX
    carry = carry_ref[...]
    for r, rows in enumerate(_chunks()):
        fn = jnp.where(is_ctx, fnc_ref[rows, :], fnl_ref[rows, :])
        om = jnp.where(is_ctx, omc_ref[rows, :], oml_ref[rows, :])
        osw = jnp.where(is_ctx, osc_ref[rows, :], osl_ref[rows, :])
        merged = (gates_ref[rows, 0:1024].astype(F32) * _dot(fn, wf_ref[...])
                  + gates_ref[rows, 1024:2048].astype(F32) * _dot(om, wm_ref[...])
                  + gates_ref[rows, 2048:3072].astype(F32) * _dot(osw, ws_ref[...]))
        mix = _dot(merged.astype(BF16), wo_ref[...])
        x = jnp.where(is_ctx, xc_ref[rows, :], xl_ref[rows, :])
        x1 = x + mod_ref[:, 2048:3072] * _rms_rows(mix, g1_ref[...])
        x1_ref[rows, :] = x1
        h2 = _rms_rows(x1, g2_ref[...]) * (1.0 + mod_ref[:, 4096:5120]) + mod_ref[:, 3072:4096]
        h2_ref[0, rows, :], h2_ref[1, rows, :] = _pack_pair(h2)
        cols = pl.ds(r * TM, TM)
        eidx_ref[:, cols], epos_ref[:, cols], ew_ref[:, cols], carry = _route(
            h2.astype(BF16), rwt_ref, rb_ref, tri_ref, carry)
    carry_ref[...] = carry
    cnt_ref[...] = jnp.broadcast_to(carry, cnt_ref.shape).astype(jnp.int32)


def _stage_e(layer, xc, xl, modt, g1, g2, mixed, gates, wf, wm, ws, wo, rwt, rbias, tri):
    row = lambda w: pl.BlockSpec((TB, w), lambda i: (i, 0))
    ctx, lat = _ctx_rows(512), _lat_rows(512)
    col = lambda dt: (pl.BlockSpec((8, TB), lambda i: (0, i)), jax.ShapeDtypeStruct((8, N_TOK), dt))
    picks = [col(jnp.int32), col(jnp.int32), col(F32)]
    return pl.pallas_call(
        _stage_e_kernel,
        grid=(NB,),
        in_specs=[
            _ctx_rows(D_MODEL), _lat_rows(D_MODEL),
            pl.BlockSpec((None, 1, N_MOD * D_MODEL), lambda i: (i, 0, 0)),
            _const_spec((1, D_MODEL)), _const_spec((1, D_MODEL)),
            ctx, lat, ctx, lat, ctx, lat, row(3 * D_MODEL),
            _layer_spec((512, D_MODEL), layer), _layer_spec((512, D_MODEL), layer),
            _layer_spec((512, D_MODEL), layer), _layer_spec((D_MODEL, D_MODEL), layer),
            _const_spec((N_EXPERTS, D_MODEL)), _const_spec((N_EXPERTS, 1)), _const_spec((TM, TM)),
        ],
        out_specs=[row(D_MODEL), pl.BlockSpec((2, TB, PACKED), lambda i: (0, i, 0))] + [s for s, _ in picks]
        + [_const_spec((N_EXPERTS, LANES))],
        out_shape=[jax.ShapeDtypeStruct((N_TOK, D_MODEL), F32),
                   jax.ShapeDtypeStruct((2, N_TOK, PACKED), jnp.int32)] + [s for _, s in picks]
        + [jax.ShapeDtypeStruct((N_EXPERTS, LANES), jnp.int32)],
        scratch_shapes=[pltpu.VMEM((N_EXPERTS, 1), F32)],
        compiler_params=_cparams(1),
        name="stage_e",
    )(xc, xl, modt, g1, g2, *mixed, gates, wf, wm, ws, wo, rwt, rbias, tri)


def _expert_kernel(layer, te_ref, tv_ref, par_ref, nxt_ref, x_ref, w1_hbm, w3_hbm, w2_hbm, o_ref,
                   w1f_ref, w3f_ref, w2f_ref, w1b_ref, w3b_ref, w2b_ref, sem):
    j = pl.program_id(0)
    valid = tv_ref[j] > 0
    first = jnp.logical_and(valid, jnp.logical_or(j == 0, te_ref[j] != te_ref[jnp.maximum(j - 1, 0)]))

    def copies(expert, slot):
        return [pltpu.make_async_copy(w_hbm.at[layer, expert], buf.at[slot], sem.at[slot, i])
                for i, (w_hbm, buf) in enumerate(((w1_hbm, w1f_ref), (w3_hbm, w3f_ref), (w2_hbm, w2f_ref)))]

    @pl.when(jnp.logical_and(valid, j == 0))
    def _():
        for cp in copies(te_ref[0], par_ref[0]):
            cp.start()

    @pl.when(first)
    def _():
        slot = par_ref[j]
        for cp in copies(te_ref[j], slot):
            cp.wait()

        @pl.when(nxt_ref[j] >= 0)
        def _():
            for cp in copies(nxt_ref[j], 1 - slot):
                cp.start()

        w1b_ref[...] = w1f_ref[slot].astype(BF16)
        w3b_ref[...] = w3f_ref[slot].astype(BF16)
        w2b_ref[...] = w2f_ref[slot].astype(BF16)

    def run(n_chunks):
        for r in range(n_chunks):
            rows = pl.ds(r * EXPERT_ROWS, EXPERT_ROWS)
            x = _unpack_pair(x_ref[0, rows, :], x_ref[1, rows, :]).astype(BF16)
            hg = _dot(x, w1b_ref[...])
            hu = _dot(x, w3b_ref[...])
            act = (jax.nn.silu(hg) * hu).astype(BF16)
            o_ref[0, rows, :], o_ref[1, rows, :] = _pack_pair(_dot(act, w2b_ref[...]))

    for n_chunks in range(1, TE // EXPERT_ROWS + 1):
        pl.when(tv_ref[j] == n_chunks)(functools.partial(run, n_chunks))


def _experts(layer, tile_expert, tile_chunks, tile_slot, tile_next, xs, w1, w3, w2):
    slot_rows = pl.BlockSpec((2, TE, PACKED), lambda j, te, tv, par, nxt: (0, j, 0))
    anywhere = pl.BlockSpec(memory_space=pl.ANY)
    grid_spec = pltpu.PrefetchScalarGridSpec(
        num_scalar_prefetch=4,
        grid=(NTE,),
        in_specs=[slot_rows, anywhere, anywhere, anywhere],
        out_specs=slot_rows,
        scratch_shapes=[pltpu.VMEM((2, D_MODEL, EXPERT_FF), F32), pltpu.VMEM((2, D_MODEL, EXPERT_FF), F32),
                        pltpu.VMEM((2, EXPERT_FF, D_MODEL), F32),
                        pltpu.VMEM((D_MODEL, EXPERT_FF), BF16), pltpu.VMEM((D_MODEL, EXPERT_FF), BF16),
                        pltpu.VMEM((EXPERT_FF, D_MODEL), BF16),
                        pltpu.SemaphoreType.DMA((2, 3))],
    )
    return pl.pallas_call(
        functools.partial(_expert_kernel, layer),
        grid_spec=grid_spec,
        out_shape=jax.ShapeDtypeStruct((2, S_MAX, PACKED), jnp.int32),
        compiler_params=_cparams(1),
        name="experts",
    )(tile_expert, tile_chunks, tile_slot, tile_next, xs, w1, w3, w2)


def _sc_mesh():
    return plsc.VectorSubcoreMesh(core_axis_name="c", subcore_axis_name="s",
                                  num_cores=SC_CORES, num_subcores=SC_SUBCORES)


def _sc_scatter_rows(rows, slot8):
    @functools.partial(pl.kernel, mesh=_sc_mesh(), scratch_types=[pltpu.SemaphoreType.DMA],
                       out_type=jax.ShapeDtypeStruct((2, S_MAX, PACKED), jnp.int32))
    def scatter(x_hbm, i_hbm, o_hbm, sem):
        for h in range(2):
            dst = o_hbm.at[h]

            def body(x_vmem, i_vmem, dst=dst):
                copies = [pltpu.async_copy(x_vmem, dst.at[i_vmem.at[k]], sem) for k in range(TOP_K)]
                for cp in copies:
                    cp.wait()

            pltpu.emit_pipeline(
                body,
                grid=(N_TOK // SC_ROWS,),
                in_specs=[pl.BlockSpec((SC_ROWS, PACKED), lambda i: (i, 0)),
                          pl.BlockSpec((8, SC_ROWS), lambda i: (0, i))],
                out_specs=[],
                core_axis_name=("c", "s"),
                dimension_semantics=(pltpu.PARALLEL,),
            )(x_hbm.at[h], i_hbm)

    return scatter(rows, slot8)


def _sc_gather_rows(table, idx):
    n = idx.shape[1]

    @functools.partial(pl.kernel, mesh=_sc_mesh(), scratch_types=[],
                       out_type=jax.ShapeDtypeStruct((2, n, PACKED), jnp.int32))
    def gather(t_hbm, i_hbm, o_hbm):
        for h in range(2):
            src = t_hbm.at[h]

            def body(i_vmem, o_vmem, src=src):
                pltpu.sync_copy(src.at[i_vmem.at[0]], o_vmem)

            pltpu.emit_pipeline(
                body,
                grid=(n // SC_ROWS,),
                in_specs=[pl.BlockSpec((1, SC_ROWS), lambda i: (0, i))],
                out_specs=[pl.BlockSpec((SC_ROWS, PACKED), lambda i: (i, 0))],
                core_axis_name=("c", "s"),
                dimension_semantics=(pltpu.PARALLEL,),
            )(i_hbm, o_hbm.at[h])

    return gather(table, idx)


def _stage_g_kernel(x1_ref, mod_ref, g3_ref, yg_ref, ew_ref, h2_ref,
                    s1_ref, s3_ref, s2_ref, oc_ref, ol_ref):
    is_ctx = pl.program_id(0) < NB_CTX
    for rows in _chunks():
        h = _unpack_pair(h2_ref[0, rows, :], h2_ref[1, rows, :]).astype(BF16)
        act = jax.nn.silu(_dot(h, s1_ref[...])) * _dot(h, s3_ref[...])
        y = _dot(act.astype(BF16), s2_ref[...])
        for k in range(TOP_K):
            y = y + ew_ref[rows, k:k + 1] * _unpack_pair(yg_ref[0, k, rows, :], yg_ref[1, k, rows, :])
        out = x1_ref[rows, :] + mod_ref[:, 5120:6144] * _rms_rows(y, g3_ref[...])

        @pl.when(is_ctx)
        def _():
            oc_ref[rows, :] = out

        @pl.when(jnp.logical_not(is_ctx))
        def _():
            ol_ref[rows, :] = out


def _stage_g(layer, x1, modt, g3, yg, ew_rows, h2, s1, s3, s2):
    row = lambda w: pl.BlockSpec((TB, w), lambda i: (i, 0))
    picked = pl.BlockSpec((2, TOP_K, TB, PACKED), lambda i: (0, 0, i, 0))
    return pl.pallas_call(
        _stage_g_kernel,
        grid=(NB,),
        in_specs=[row(D_MODEL), pl.BlockSpec((None, 1, N_MOD * D_MODEL), lambda i: (i, 0, 0)),
                  _const_spec((1, D_MODEL)), picked, row(8),
                  pl.BlockSpec((2, TB, PACKED), lambda i: (0, i, 0)),
                  _layer_spec((D_MODEL, SHARED_FF), layer), _layer_spec((D_MODEL, SHARED_FF), layer),
                  _layer_spec((SHARED_FF, D_MODEL), layer)],
        out_specs=[_ctx_rows(D_MODEL), _lat_rows(D_MODEL)],
        out_shape=[jax.ShapeDtypeStruct((N_CTX, D_MODEL), F32),
                   jax.ShapeDtypeStruct((N_LAT, D_MODEL), F32)],
        compiler_params=_cparams(1),
        name="stage_g",
    )(x1, modt, g3, yg, ew_rows, h2, s1, s3, s2)


def _rope_tables():
    t = np.arange(DEC_SEQ)
    pos = np.stack([(t // GRID_W), (t % GRID_W)], axis=-1).astype(np.float32)

    def table(r):
        n_freq = r // 4
        inv = np.float32(ROPE_BASE) ** (-np.arange(n_freq, dtype=np.float32) / np.float32(n_freq))
        ang = pos[:, :, None] * inv.astype(np.float32)
        cos = np.cos(ang)
        sin = np.sin(ang)
        cos_t = np.stack([cos, cos], axis=2).reshape(DEC_SEQ, r)
        sin_t = np.stack([-sin, sin], axis=2).reshape(DEC_SEQ, r)
        return cos_t, sin_t

    c64, s64 = table(SWA_HEAD_DIM)
    c32, s32 = table(MLA_ROPE)
    lat = np.concatenate([np.tile(c64, (1, 8)), np.tile(s64, (1, 8)),
                          np.tile(c32, (1, 4)), np.tile(s32, (1, 4))], axis=1)
    ident = np.concatenate([np.ones((TB, 512)), np.zeros((TB, 512)),
                            np.ones((TB, 128)), np.zeros((TB, 128))], axis=1)
    return jnp.asarray(np.concatenate([ident, lat], axis=0).astype(np.float32))


def _dft_pair(n):
    k = np.arange(n, dtype=np.int64)
    ang = ((k[:, None] * k[None, :]) % n).astype(np.float64) * (2.0 * math.pi / n)
    return np.cos(ang), np.sin(ang)


def _fnet_tables():
    c64, s64 = _dft_pair(FNET_GROUP_DIM)
    eye = np.eye(FNET_GROUPS)
    bd = np.concatenate([np.kron(eye, c64), np.kron(eye, s64)], axis=1)
    mats = []
    for t_len in (SEQ, DEC_SEQ):
        c, s = _dft_pair(t_len)
        mats.append(np.concatenate([c, -s], axis=1))
    return tuple(jnp.asarray(m.astype(np.float32).astype(BF16)) for m in (bd, mats[0], mats[1]))


def _layer_weights(l, w_in, w_uq, w_ukv):
    w = w_in[l]
    wide = jnp.concatenate([w[:, 0:1024], w[:, 1056:1824], w[:, 1024:1056],
                            jnp.zeros((D_MODEL, 96), F32)], axis=1).astype(BF16)

    uq = w_uq[l].reshape(MLA_Q_RANK, MLA_HEADS, MLA_NOPE + MLA_ROPE)
    z32 = jnp.zeros((MLA_Q_RANK, MLA_HEADS, 32), F32)
    wqa = jnp.concatenate([uq, z32], axis=2).reshape(MLA_Q_RANK, 1024).astype(BF16)
    ukv = w_ukv[l].reshape(MLA_KV_RANK, MLA_HEADS, MLA_NOPE + MLA_V)
    wk = jnp.concatenate([ukv[:, :, :MLA_NOPE], jnp.zeros((MLA_KV_RANK, MLA_HEADS, 64), F32)],
                         axis=2).reshape(MLA_KV_RANK, 1024).astype(BF16)
    wv = ukv[:, :, MLA_NOPE:].reshape(MLA_KV_RANK, 512).astype(BF16)
    return wide, wqa, wk, wv


def _rope_placement():
    e = np.zeros((128, 1024), np.float32)
    for hd in range(MLA_HEADS):
        for i in range(MLA_ROPE):
            e[i, hd * 128 + MLA_NOPE + i] = 1.0
    return jnp.asarray(e, BF16)


def _moe_dispatch_plan(eidx, epos, counts):
    padded = ((counts + TE - 1) // TE) * TE
    ends = jnp.cumsum(padded)
    offs = ends - padded
    ids = jnp.arange(N_EXPERTS, dtype=jnp.int32)
    picked_off = jnp.sum(jnp.where(eidx[:, :, None] == ids, offs, 0), axis=-1)
    slot = picked_off + epos
    starts = jnp.arange(NTE, dtype=jnp.int32) * TE
    tile_expert = jnp.sum((ends[None, :] <= starts[:, None]).astype(jnp.int32), axis=1)
    tile_expert = jnp.minimum(tile_expert, N_EXPERTS - 1)
    pick = tile_expert[:, None] == ids[None, :]
    last_real = jnp.sum(jnp.where(pick, (offs + counts)[None, :], 0), axis=1)
    n_real = jnp.clip(last_real - starts, 0, TE)
    n_real = jnp.where(starts < ends[-1], n_real, 0)
    tile_chunks = ((n_real + EXPERT_ROWS - 1) // EXPERT_ROWS).astype(jnp.int32)
    used = counts > 0
    rank = jnp.cumsum(used.astype(jnp.int32)) - 1
    tile_slot = jnp.sum(jnp.where(pick, rank[None, :], 0), axis=1) % 2
    later_used = jnp.logical_and(ids[None, :] > ids[:, None], used[None, :])
    next_used = jnp.min(jnp.where(later_used, ids[None, :], N_EXPERTS), axis=1)
    next_used = jnp.where(next_used < N_EXPERTS, next_used, -1)
    tile_next = jnp.sum(jnp.where(pick, next_used[None, :], 0), axis=1)
    return slot, tile_expert, tile_chunks, tile_slot.astype(jnp.int32), tile_next.astype(jnp.int32)


def kernel(x_prompt, x_sample, cache_mla_ckv, cache_mla_krope, cache_swa_k, cache_swa_v, c, c_ctx,
           ada_w, ada_b, norm_g, w_in, q_norm, kv_norm, w_fnet, w_uq, w_ukv, w_mla_o, swa_sink,
           w_swa_o, w_gate, b_gate, w_out, router_w, router_bias, exp_w1, exp_w3, exp_w2,
           shared_w1, shared_w3, shared_w2):
    xc = x_prompt.reshape(N_CTX, D_MODEL)
    xl = x_sample.reshape(N_LAT, D_MODEL)

    cond8 = jnp.concatenate([c_ctx[None, :], c, jnp.zeros((3, D_MODEL), F32)], axis=0)
    mod = _modulation(cond8, ada_w, ada_b)
    tile_cond = np.concatenate([np.zeros(NB_CTX, np.int32),
                                1 + np.arange(NB - NB_CTX, dtype=np.int32) // LAT_BLOCKS])

    tab = _rope_tables()
    bd, f_ctx, f_lat = _fnet_tables()
    e_mat = _rope_placement()
    tri = jnp.asarray(np.triu(np.ones((TM, TM), np.float32), 1), BF16)
    w_gate_b, w_fnet_b, w_mla_o_b, w_swa_o_b, w_out_b, sw1_b, sw3_b, sw2_b = (
        w.astype(BF16) for w in (w_gate, w_fnet, w_mla_o, w_swa_o, w_out, shared_w1, shared_w3, shared_w2))

    new_ckv, new_kr, new_k, new_v = [], [], [], []
    for l in range(DEPTH):
        modt = mod[l][tile_cond][:, None, :]
        wide, wqa, wk, wv = _layer_weights(l, w_in, w_uq, w_ukv)
        ng = norm_g[l]

        fin, ckv, kr, sq, sk, sv, gates, q_m, k_m, v_m = _stage_a(
            l, xc, xl, modt, ng[0:1], wide, w_gate_b, b_gate[l][None, :],
            q_norm[l][None, :], kv_norm[l][None, :], tab, wqa, wk, e_mat, wv)

        new_ckv.append(ckv[:N_CTX].reshape(BATCH, SEQ, MLA_KV_RANK))
        new_kr.append(kr[:N_CTX, :MLA_ROPE].reshape(BATCH, SEQ, MLA_ROPE))
        new_k.append(sk[:N_CTX].reshape(BATCH, SEQ, SWA_KV_HEADS, SWA_HEAD_DIM))
        new_v.append(sv[:N_CTX].reshape(BATCH, SEQ, SWA_KV_HEADS, SWA_HEAD_DIM))

        fn = (_fnet(fin, f_ctx, bd, BATCH, SEQ, 0),
              _fnet(fin, f_lat, bd, DEC_BATCH, DEC_SEQ, N_CTX // DEC_SEQ))

        kr_cache = jnp.pad(cache_mla_krope[:, l].reshape(N_CACHE, MLA_ROPE), ((0, 0), (0, 96)))
        k_c, v_c = _mla_cache_kv(cache_mla_ckv[:, l].reshape(N_CACHE, MLA_KV_RANK), kr_cache,
                                 wk, e_mat, wv)
        om = (_mla_attn(q_m, k_m, v_m, k_c, v_c, latent=False),
              _mla_attn(q_m, k_m, v_m, k_c, v_c, latent=True))

        ck = cache_swa_k[:, l].reshape(DEC_BATCH, PAST_LEN, 128)
        cv = cache_swa_v[:, l].reshape(DEC_BATCH, PAST_LEN, 128)
        osw = (_swa_attn(swa_sink[l], sq, sk, sv, ck, cv, latent=False),
               _swa_attn(swa_sink[l], sq, sk, sv, ck, cv, latent=True))

        x1, h2, eidx, epos, ew, counts = _stage_e(
            l, xc, xl, modt, ng[1:2], ng[2:3], fn + om + osw, gates,
            w_fnet_b, w_mla_o_b, w_swa_o_b, w_out_b,
            router_w[l].T.astype(BF16), router_bias[l][:, None], tri)
        slot, tile_expert, tile_chunks, tile_slot, tile_next = _moe_dispatch_plan(eidx, epos, counts[:, 0])
        xs = _sc_scatter_rows(h2, slot)
        ys = _experts(l, tile_expert, tile_chunks, tile_slot, tile_next, xs, exp_w1, exp_w3, exp_w2)
        picks = slot[:TOP_K].reshape(1, TOP_K * N_TOK)
        yg = _sc_gather_rows(ys, picks).reshape(2, TOP_K, N_TOK, PACKED)
        xc, xl = _stage_g(l, x1, modt, ng[3:4], yg, ew.T, h2, sw1_b, sw3_b, sw2_b)

    y_p = xc.reshape(BATCH, SEQ, D_MODEL)
    y_s = xl.reshape(DEC_BATCH, DEC_SEQ, D_MODEL)
    return (y_p, y_s, jnp.stack(new_ckv, axis=1), jnp.stack(new_kr, axis=1),
            jnp.stack(new_k, axis=1), jnp.stack(new_v, axis=1))
```

```python
import functools
import math

import numpy as np
import jax
import jax.numpy as jnp
from jax import lax
from jax.experimental import pallas as pl
from jax.experimental.pallas import tpu as pltpu
from jax.experimental.pallas import tpu_sc as plsc

D_MODEL = 1024
BATCH = 16
SEQ = 256
DEPTH = 2
DEC_BATCH = 4
DEC_SEQ = 2048
PAST_LEN = 512
GRID_W = 64
EPS = 1e-6
ROPE_BASE = 10000.0
NEG_INF = -1e30

FNET_GROUPS = 8
FNET_GROUP_DIM = 64
FNET_WIDTH = 512
MLA_HEADS = 8
MLA_Q_RANK = 384
MLA_KV_RANK = 128
MLA_NOPE = 64
MLA_ROPE = 32
MLA_V = 64
MLA_SCALE = (MLA_NOPE + MLA_ROPE) ** -0.5
LOG2E = math.log2(math.e)
SWA_HEADS = 8
SWA_KV_HEADS = 2
SWA_HEAD_DIM = 64
SWA_WINDOW = 128
SWA_SCALE = SWA_HEAD_DIM ** -0.5
N_MOD = 6
N_EXPERTS = 64
N_EXPERT_GROUPS = 8
TOPK_GROUPS = 4
TOP_K = 6
EXPERT_FF = 256
SHARED_FF = 256
ROUTED_SCALE = 2.5

LANES = 128
TM = 256
N_CTX = BATCH * SEQ
N_LAT = DEC_BATCH * DEC_SEQ
N_TOK = N_CTX + N_LAT
N_CACHE = DEC_BATCH * PAST_LEN
NT_CTX = N_CTX // TM
NT_LAT = N_LAT // TM
NT = N_TOK // TM
LAT_TILES = DEC_SEQ // TM
TB = 512
NB = N_TOK // TB
NB_CTX = N_CTX // TB
LAT_BLOCKS = DEC_SEQ // TB
MLA_LAT_TQ = 256
MLA_LAT_PAIRS = 4
FNET_ROWS = 1024
TE = 512
S_MAX = N_TOK * TOP_K + N_EXPERTS * TE
NTE = S_MAX // TE
EXPERT_ROWS = 256
VMEM_LIMIT = 56 * 1024 * 1024
PACKED = D_MODEL // 4
SC_ROWS = 128
SC_CORES = 2
SC_SUBCORES = 16

A_F = (0, 512)
A_QD = (512, 896)
A_KV = (896, 1024)
A_SQ = (1024, 1536)
A_SK = (1536, 1664)
A_SV = (1664, 1792)
A_KR = (1792, 1920)
W_IN_WIDE = 1920
TAB_W = 1280

F32 = jnp.float32
BF16 = jnp.bfloat16


def _cparams(n_axes, parallel=False):
    sem = ("parallel" if parallel else "arbitrary",) * n_axes
    return pltpu.CompilerParams(dimension_semantics=sem, vmem_limit_bytes=VMEM_LIMIT)


def _dot(a, b):
    return jnp.dot(a, b, preferred_element_type=F32)


def _dot_nt(a, b):
    return lax.dot_general(a, b, (((1,), (1,)), ((), ())), preferred_element_type=F32)


def _rms_rows(v, g):
    return v * lax.rsqrt(jnp.mean(v * v, axis=-1, keepdims=True) + EPS) * g


def _pack_rows(v):
    half = v.shape[1] // 2
    lo = lax.bitcast_convert_type(v[:, :half].astype(BF16).astype(F32), jnp.int32)
    hi = lax.bitcast_convert_type(v[:, half:].astype(BF16).astype(F32), jnp.int32)
    return jnp.bitwise_or(jnp.bitwise_and(hi, -65536), jnp.bitwise_and(jnp.right_shift(lo, 16), 65535))


def _unpack_rows(w):
    lo = lax.bitcast_convert_type(jnp.left_shift(w, 16), F32)
    hi = lax.bitcast_convert_type(jnp.bitwise_and(w, -65536), F32)
    return jnp.concatenate([lo, hi], axis=1)


def _pack_pair(v):
    half = v.shape[1] // 2
    return _pack_rows(v[:, :half]), _pack_rows(v[:, half:])


def _unpack_pair(a, b):
    return jnp.concatenate([_unpack_rows(a), _unpack_rows(b)], axis=1)


def _const_spec(shape):
    return pl.BlockSpec(shape, lambda *_: (0,) * len(shape))


def _layer_spec(shape, layer):
    return pl.BlockSpec((None,) + shape, lambda *_: (layer,) + (0,) * len(shape))


def _ctx_rows(width):
    return pl.BlockSpec((TB, width), lambda i: (jnp.minimum(i, NB_CTX - 1), 0))


def _lat_rows(width):
    return pl.BlockSpec((TB, width), lambda i: (jnp.maximum(i - NB_CTX, 0), 0))


def _tab_row_block(i):
    return jnp.where(i < NB_CTX, 0, 1 + (i - NB_CTX) % LAT_BLOCKS)


def _mod_kernel(cond_ref, w_ref, b_ref, o_ref):
    c = cond_ref[...]
    a = (c * jax.nn.sigmoid(c)).astype(BF16)
    o_ref[...] = _dot(a, w_ref[...].astype(BF16)) + b_ref[...]


def _modulation(cond8, ada_w, ada_b):
    tn = 512
    nj = N_MOD * D_MODEL // tn
    return pl.pallas_call(
        _mod_kernel,
        grid=(DEPTH, nj),
        in_specs=[
            pl.BlockSpec((8, D_MODEL), lambda l, j: (0, 0)),
            pl.BlockSpec((None, D_MODEL, tn), lambda l, j: (l, 0, j)),
            pl.BlockSpec((None, 1, tn), lambda l, j: (l, 0, j)),
        ],
        out_specs=pl.BlockSpec((None, 8, tn), lambda l, j: (l, 0, j)),
        out_shape=jax.ShapeDtypeStruct((DEPTH, 8, N_MOD * D_MODEL), F32),
        compiler_params=_cparams(2),
        name="modulation",
    )(cond8, ada_w, ada_b.reshape(DEPTH, 1, N_MOD * D_MODEL))


def _half_swap(x, half):
    n = x.shape[1]
    lane = lax.broadcasted_iota(jnp.int32, (1, n), 1)
    return jnp.where((lane & half) == 0, pltpu.roll(x, n - half, 1), pltpu.roll(x, half, 1))


def _mla_expand(rows, cq, ckv, kr, cos32, sin32, wqa_ref, wk_ref, e_ref, wv_ref,
                q_ref, k_ref, v_ref):
    if q_ref is not None:
        lane = lax.broadcasted_iota(jnp.int32, (1, LANES), 1)
        rope_lane = jnp.logical_and(lane >= MLA_NOPE, lane < MLA_NOPE + MLA_ROPE)
        cos_h = jnp.where(rope_lane, cos32, 1.0)
        sin_h = jnp.where(rope_lane, sin32, 0.0)
        for hd in range(MLA_HEADS):
            lo, hi = hd * LANES, (hd + 1) * LANES
            q = _dot(cq, wqa_ref[:, lo:hi])
            q = q * cos_h + _half_swap(q, MLA_ROPE // 4) * sin_h
            q_ref[rows, lo:hi] = (q * (MLA_SCALE * LOG2E)).astype(BF16)
    k_ref[rows, :] = (_dot(ckv, wk_ref[...]) + _dot(kr, e_ref[...])).astype(BF16)
    v_ref[rows, :] = _dot(ckv, wv_ref[...]).astype(BF16)


def _chunks():
    return [pl.ds(r * TM, TM) for r in range(TB // TM)]


def _stage_a_kernel(xc_ref, xl_ref, mod_ref, g_ref, win_ref, qn_ref, kvn_ref, tab_ref,
                    wqa_ref, wk_ref, e_ref, wv_ref,
                    fin_ref, ckv_ref, kr_ref, sq_ref, sk_ref, sv_ref, h_ref,
                    qm_ref, km_ref, vm_ref):
    is_ctx = pl.program_id(0) < NB_CTX
    for rows in _chunks():
        x = jnp.where(is_ctx, xc_ref[rows, :], xl_ref[rows, :])
        h = (_rms_rows(x, g_ref[...]) * (1.0 + mod_ref[:, 1024:2048]) + mod_ref[:, 0:1024]).astype(BF16)
        h_ref[rows, :] = h

        def proj(seg):
            return _dot(h, win_ref[:, seg[0]:seg[1]])

        fin_ref[rows, :] = proj(A_F).astype(BF16)
        cq = _rms_rows(proj(A_QD), qn_ref[...]).astype(BF16)
        ckv = _rms_rows(proj(A_KV), kvn_ref[...])
        ckv_ref[rows, :] = ckv
        cos64 = tab_ref[rows, 0:512]
        sin64 = tab_ref[rows, 512:1024]
        sq = proj(A_SQ)
        sq = sq * cos64 + _half_swap(sq, SWA_HEAD_DIM // 4) * sin64
        sq_ref[rows, :] = (sq * (SWA_SCALE * LOG2E)).astype(BF16)
        sk = proj(A_SK)
        sk_ref[rows, :] = sk * cos64[:, 0:128] + _half_swap(sk, SWA_HEAD_DIM // 4) * sin64[:, 0:128]
        sv_ref[rows, :] = proj(A_SV)
        cos32 = tab_ref[rows, 1024:1152]
        sin32 = tab_ref[rows, 1152:1280]
        kr = proj(A_KR)
        kr = kr * cos32 + _half_swap(kr, MLA_ROPE // 4) * sin32
        kr_ref[rows, :] = kr
        _mla_expand(rows, cq, ckv.astype(BF16), kr.astype(BF16), cos32, sin32,
                    wqa_ref, wk_ref, e_ref, wv_ref, qm_ref, km_ref, vm_ref)


def _stage_a(xc, xl, modt, g0, w_in_wide, q_norm, kv_norm, tab, wqa, wk, e_mat, wv):
    row = lambda w: pl.BlockSpec((TB, w), lambda i: (i, 0))
    outs = [(512, BF16), (128, F32), (128, F32), (512, BF16), (128, F32), (128, F32),
            (D_MODEL, BF16), (1024, BF16), (1024, BF16), (512, BF16)]
    return pl.pallas_call(
        _stage_a_kernel,
        grid=(NB,),
        in_specs=[
            _ctx_rows(D_MODEL), _lat_rows(D_MODEL),
            pl.BlockSpec((None, 1, N_MOD * D_MODEL), lambda i: (i, 0, 0)),
            _const_spec((1, D_MODEL)),
            _const_spec((D_MODEL, W_IN_WIDE)),
            _const_spec((1, MLA_Q_RANK)),
            _const_spec((1, MLA_KV_RANK)),
            pl.BlockSpec((TB, TAB_W), lambda i: (_tab_row_block(i), 0)),
            _const_spec((MLA_Q_RANK, 1024)),
            _const_spec((128, 1024)), _const_spec((128, 1024)), _const_spec((128, 512)),
        ],
        out_specs=[row(w) for w, _ in outs],
        out_shape=[jax.ShapeDtypeStruct((N_TOK, w), dt) for w, dt in outs],
        compiler_params=_cparams(1),
        name="stage_a",
    )(xc, xl, modt, g0, w_in_wide, q_norm, kv_norm, tab, wqa, wk, e_mat, wv)


def _fnet_kernel(t_len, scale, fin_ref, f_ref, bd_ref, o_ref, zz_ref):
    @pl.when(pl.program_id(1) == 0)
    def _():
        z = fin_ref[...]
        zz_ref[0:t_len, :] = _dot(z, bd_ref[:, 0:512]).astype(BF16)
        zz_ref[t_len:2 * t_len, :] = _dot(z, bd_ref[:, 512:1024]).astype(BF16)

    o_ref[...] = (_dot(f_ref[...], zz_ref[...]) * scale).astype(BF16)


def _fnet(fin, fmat, bd, n_batch, t_len, row_block0):
    scale = 1.0 / math.sqrt(t_len * FNET_GROUP_DIM)
    ft = min(t_len, FNET_ROWS)
    return pl.pallas_call(
        functools.partial(_fnet_kernel, t_len, scale),
        grid=(n_batch, t_len // ft),
        in_specs=[
            pl.BlockSpec((t_len, FNET_WIDTH), lambda b, i: (row_block0 + b, 0)),
            pl.BlockSpec((ft, 2 * t_len), lambda b, i: (i, 0)),
            _const_spec((FNET_WIDTH, 2 * FNET_WIDTH)),
        ],
        out_specs=pl.BlockSpec((ft, FNET_WIDTH), lambda b, i: (b * (t_len // ft) + i, 0)),
        out_shape=jax.ShapeDtypeStruct((n_batch * t_len, FNET_WIDTH), BF16),
        scratch_shapes=[pltpu.VMEM((2 * t_len, FNET_WIDTH), BF16)],
        compiler_params=_cparams(2),
        name=f"fnet_{t_len}",
    )(fin, fmat, bd)


def _mla_cache_kernel(ckv_ref, kr_ref, wk_ref, e_ref, wv_ref, k_ref, v_ref):
    _mla_expand(slice(None), None, ckv_ref[...].astype(BF16), kr_ref[...].astype(BF16), None, None,
                None, wk_ref, e_ref, wv_ref, None, k_ref, v_ref)


def _mla_cache_kv(ckv_cache, kr_cache, wk, e_mat, wv):
    row = lambda w: pl.BlockSpec((TM, w), lambda i: (i, 0))
    return pl.pallas_call(
        _mla_cache_kernel,
        grid=(N_CACHE // TM,),
        in_specs=[row(128), row(128),
                  _const_spec((128, 1024)), _const_spec((128, 1024)), _const_spec((128, 512))],
        out_specs=[row(1024), row(512)],
        out_shape=[jax.ShapeDtypeStruct((N_CACHE, 1024), BF16),
                   jax.ShapeDtypeStruct((N_CACHE, 512), BF16)],
        compiler_params=_cparams(1),
        name="mla_cache_kv",
    )(ckv_cache, kr_cache, wk, e_mat, wv)


def _mla_attn_kernel(n_seg, pairs, q_ref, *refs):
    k_refs = refs[0:n_seg]
    v_refs = refs[n_seg:2 * n_seg]
    o_ref = refs[2 * n_seg]
    lane = lax.broadcasted_iota(jnp.int32, (1, LANES), 1)
    low = lane < MLA_V
    for pr in range(pairs):
        outs = []
        for hh in range(2):
            hd = 2 * pr + hh
            q = q_ref[:, hd * LANES:(hd + 1) * LANES]
            ss = [_dot_nt(q, k[:, hd * LANES:(hd + 1) * LANES]) for k in k_refs]
            m = functools.reduce(jnp.maximum, [s.max(axis=-1, keepdims=True) for s in ss])
            keep = low if hh == 0 else jnp.logical_not(low)
            sum_lane = MLA_V if hh == 0 else 0
            po = None
            for s, v_ref in zip(ss, v_refs):
                v = v_ref[:, pr * LANES:(pr + 1) * LANES]
                vm = jnp.where(lane == sum_lane, jnp.ones_like(v), jnp.where(keep, v, jnp.zeros_like(v)))
                t = _dot(jnp.exp2(s - m).astype(BF16), vm)
                po = t if po is None else po + t
            outs.append(po / po[:, sum_lane:sum_lane + 1])
        o_ref[:, pr * LANES:(pr + 1) * LANES] = jnp.where(low, outs[0], outs[1]).astype(BF16)


def _mla_attn(q_all, k_all, v_all, k_cache, v_cache, latent):
    if latent:
        tq, pairs = MLA_LAT_TQ, MLA_LAT_PAIRS
        n_b, n_q = DEC_BATCH, DEC_SEQ // tq
        q0 = N_CTX // tq
        kv_specs = [
            pl.BlockSpec((PAST_LEN, 256 * pairs), lambda b, hp, i: (b, hp)),
            pl.BlockSpec((DEC_SEQ, 256 * pairs), lambda b, hp, i: (N_CTX // DEC_SEQ + b, hp)),
            pl.BlockSpec((PAST_LEN, 128 * pairs), lambda b, hp, i: (b, hp)),
            pl.BlockSpec((DEC_SEQ, 128 * pairs), lambda b, hp, i: (N_CTX // DEC_SEQ + b, hp)),
        ]
        args = (q_all, k_cache, k_all, v_cache, v_all)
        n_seg = 2
    else:
        tq, pairs = SEQ, MLA_HEADS // 2
        n_b, n_q = BATCH, 1
        q0 = 0
        kv_specs = [
            pl.BlockSpec((SEQ, 256 * pairs), lambda b, hp, i: (b, hp)),
            pl.BlockSpec((SEQ, 128 * pairs), lambda b, hp, i: (b, hp)),
        ]
        args = (q_all, k_all, v_all)
        n_seg = 1
    return pl.pallas_call(
        functools.partial(_mla_attn_kernel, n_seg, pairs),
        grid=(n_b, MLA_HEADS // (2 * pairs), n_q),
        in_specs=[pl.BlockSpec((tq, 256 * pairs), lambda b, hp, i: (q0 + b * n_q + i, hp))] + kv_specs,
        out_specs=pl.BlockSpec((tq, 128 * pairs), lambda b, hp, i: (b * n_q + i, hp)),
        out_shape=jax.ShapeDtypeStruct((n_b * n_q * tq, MLA_HEADS * MLA_V), BF16),
        compiler_params=_cparams(3),
        name="mla_attn_lat" if latent else "mla_attn_ctx",
    )(*args)


def _swa_kernel(windowed, n_steps, sink_ref, q_ref, *refs):
    n_seg = 4 if windowed else 1
    k_refs = refs[0:n_seg]
    v_refs = refs[n_seg:2 * n_seg]
    o_ref = refs[2 * n_seg]
    step = pl.program_id(1)
    lane = lax.broadcasted_iota(jnp.int32, (1, LANES), 1)
    low = lane < SWA_HEAD_DIM
    high = jnp.logical_not(low)

    k_all = jnp.concatenate([r[...] for r in k_refs], axis=0)
    v_all = jnp.concatenate([r[...] for r in v_refs], axis=0)
    k_sw = pltpu.roll(k_all, SWA_HEAD_DIM, 1)
    v_sw = pltpu.roll(v_all, SWA_HEAD_DIM, 1)

    if windowed:
        tq = SWA_WINDOW
        qi = lax.broadcasted_iota(jnp.int32, (2 * tq, tq), 0) % tq
        kj = lax.broadcasted_iota(jnp.int32, (2 * tq, tq), 1)
        after = kj >= qi
        before = kj <= qi
        biases = [(jnp.where(jnp.logical_and(after, step > 0), 0.0, NEG_INF),
                   jnp.where(before, 0.0, NEG_INF)),
                  (jnp.where(after, 0.0, NEG_INF),
                   jnp.where(jnp.logical_and(before, step < n_steps - 1), 0.0, NEG_INF))]
    else:
        tq = q_ref.shape[0]
    n_sub = q_ref.shape[0] // tq
    top_rows = lax.broadcasted_iota(jnp.int32, (2 * tq, 1), 0) < tq

    for g in range(SWA_KV_HEADS):
        kh, vh = [], []
        for half in range(2):
            keep = low if half == 0 else high
            sum_lane = SWA_HEAD_DIM if half == 0 else 0
            straight = (g == half)
            kh.append(jnp.where(keep, k_all if straight else k_sw, 0.0).astype(BF16))
            vh.append(jnp.where(lane == sum_lane, 1.0,
                                jnp.where(keep, v_all if straight else v_sw, 0.0)).astype(BF16))
        for sub in range(n_sub):
            rows = slice(sub * tq, (sub + 1) * tq)
            qs = jnp.concatenate([q_ref[rows, 256 * g:256 * g + 128],
                                  q_ref[rows, 256 * g + 128:256 * g + 256]], axis=0)
            halves = []
            for half in range(2):
                sum_lane = SWA_HEAD_DIM if half == 0 else 0
                ks, vs = kh[half], vh[half]
                if windowed:
                    w0 = PAST_LEN + sub * tq
                    if sub == 0:
                        ks, vs = ks[0:w0 + 3 * tq], vs[0:w0 + 3 * tq]
                    else:
                        ks = jnp.concatenate([ks[0:PAST_LEN], ks[w0:w0 + 3 * tq]], axis=0)
                        vs = jnp.concatenate([vs[0:PAST_LEN], vs[w0:w0 + 3 * tq]], axis=0)
                s = _dot_nt(qs, ks)
                if windowed:
                    c0, c1, c2 = PAST_LEN, PAST_LEN + tq, PAST_LEN + 2 * tq
                    s = jnp.concatenate([s[:, :c0], s[:, c0:c1] + biases[sub][0], s[:, c1:c2],
                                         s[:, c2:] + biases[sub][1]], axis=1)
                sink = jnp.where(top_rows, sink_ref[4 * g + half], sink_ref[4 * g + 2 + half]) * LOG2E
                m = jnp.maximum(s.max(axis=-1, keepdims=True), sink)
                po = _dot(jnp.exp2(s - m).astype(BF16), vs)
                halves.append(po / (po[:, sum_lane:sum_lane + 1] + jnp.exp2(sink - m)))
            out = jnp.where(low, halves[0], halves[1])
            o_ref[rows, 256 * g:256 * g + 128] = out[0:tq].astype(BF16)
            o_ref[rows, 256 * g + 128:256 * g + 256] = out[tq:2 * tq].astype(BF16)


def _swa_attn(sink, sq, sk, sv, cache_k, cache_v, latent):
    smem = pl.BlockSpec(memory_space=pltpu.SMEM)
    if latent:
        tq = 2 * SWA_WINDOW
        n_b, n_qb = DEC_BATCH, DEC_SEQ // tq
        base = N_CTX // tq
        last = DEC_SEQ // SWA_WINDOW - 1

        def prev(b, i):
            return (2 * (base + b * n_qb) + jnp.maximum(2 * i - 1, 0), 0)

        def cur(b, i):
            return (base + b * n_qb + i, 0)

        def nxt(b, i):
            return (2 * (base + b * n_qb) + jnp.minimum(2 * i + 2, last), 0)

        cache = pl.BlockSpec((None, PAST_LEN, 128), lambda b, i: (b, 0, 0))
        edge = lambda f: pl.BlockSpec((SWA_WINDOW, 128), f)
        kv_specs = [cache, edge(prev), pl.BlockSpec((tq, 128), cur), edge(nxt)] * 2
        args = (cache_k, sk, sk, sk, cache_v, sv, sv, sv)
        q_spec = pl.BlockSpec((tq, 512), cur)
        o_spec = pl.BlockSpec((tq, 512), lambda b, i: (b * n_qb + i, 0))
    else:
        tq = SEQ
        n_b, n_qb = BATCH, 1
        blk = pl.BlockSpec((tq, 128), lambda b, i: (b, 0))
        kv_specs = [blk, blk]
        args = (sk, sv)
        q_spec = pl.BlockSpec((tq, 512), lambda b, i: (b, 0))
        o_spec = q_spec
    return pl.pallas_call(
        functools.partial(_swa_kernel, latent, n_qb),
        grid=(n_b, n_qb),
        in_specs=[smem, q_spec] + kv_specs,
        out_specs=o_spec,
        out_shape=jax.ShapeDtypeStruct((n_b * n_qb * tq, 512), BF16),
        compiler_params=_cparams(2),
        name="swa_lat" if latent else "swa_ctx",
    )(sink, sq, *args)


def _route(h, rwt_ref, rb_ref, tri_ref, carry):
    gsz = N_EXPERTS // N_EXPERT_GROUPS
    scores = jax.nn.sigmoid(_dot_nt(rwt_ref[...], h))
    biased = scores + rb_ref[...]
    mem = lax.broadcasted_iota(jnp.int32, (gsz, TM), 0).astype(F32)
    gs_rows = []
    for g in range(N_EXPERT_GROUPS):
        bg = biased[g * gsz:(g + 1) * gsz, :]
        m1 = bg.max(axis=0, keepdims=True)
        first = jnp.min(jnp.where(bg == m1, mem, float(gsz)), axis=0, keepdims=True)
        m2 = jnp.where(mem == first, -jnp.inf, bg).max(axis=0, keepdims=True)
        gs_rows.append(m1 + m2)
    gs = jnp.concatenate(gs_rows, axis=0)
    gid = lax.broadcasted_iota(jnp.int32, gs.shape, 0).astype(F32)
    gsel = jnp.zeros(gs.shape, F32)
    for _ in range(TOPK_GROUPS):
        mx = gs.max(axis=0, keepdims=True)
        pick = gid == jnp.min(jnp.where(gs == mx, gid, float(N_EXPERT_GROUPS)), axis=0, keepdims=True)
        gsel = jnp.where(pick, 1.0, gsel)
        gs = jnp.where(pick, -jnp.inf, gs)
    emask = jnp.concatenate(
        [jnp.broadcast_to(gsel[g:g + 1, :], (gsz, TM)) for g in range(N_EXPERT_GROUPS)], axis=0)
    cand = jnp.where(emask > 0.5, biased, NEG_INF)
    eid = lax.broadcasted_iota(jnp.int32, cand.shape, 0).astype(F32)
    picks = []
    self32 = jnp.zeros(cand.shape, F32)
    for _ in range(TOP_K):
        mx = cand.max(axis=0, keepdims=True)
        pick = eid == jnp.min(jnp.where(cand == mx, eid, float(N_EXPERTS)), axis=0, keepdims=True)
        picks.append(pick)
        self32 = jnp.where(pick, 1.0, self32)
        cand = jnp.where(pick, -jnp.inf, cand)
    pos = _dot(self32.astype(BF16), tri_ref[...]) + carry
    sel_scores = [jnp.sum(jnp.where(p, scores, 0.0), axis=0, keepdims=True) for p in picks]
    wsum = functools.reduce(lambda a, b: a + b, sel_scores)
    zero_f = jnp.zeros((2, TM), F32)
    eidx = [jnp.sum(jnp.where(p, eid, 0.0), axis=0, keepdims=True) for p in picks]
    epos = [jnp.sum(jnp.where(p, pos, 0.0), axis=0, keepdims=True) for p in picks]
    ew = [s / wsum * ROUTED_SCALE for s in sel_scores]
    return (jnp.concatenate(eidx + [zero_f], axis=0).astype(jnp.int32),
            jnp.concatenate(epos + [zero_f], axis=0).astype(jnp.int32),
            jnp.concatenate(ew + [zero_f], axis=0),
            carry + jnp.sum(self32, axis=1, keepdims=True))


def _stage_e_kernel(xc_ref, xl_ref, mod_ref, g1_ref, g2_ref, fnc_ref, fnl_ref, omc_ref, oml_ref, osc_ref, osl_ref,
                    h_ref, wg_ref, bg_ref, wf_ref, wm_ref, ws_ref, wo_ref, rwt_ref, rb_ref, tri_ref,
                    x1_ref, h2_ref, eidx_ref, epos_ref, ew_ref, cnt_ref, carry_ref):
    @pl.when(pl.program_id(0) == 0)
    def _():
        carry_ref[...] = jnp.zeros_like(carry_ref)

    is_ctx = pl.program_id(0) < NB_CTX
    carry = carry_ref[...]
    for r, rows in enumerate(_chunks()):
        fn = jnp.where(is_ctx, fnc_ref[rows, :], fnl_ref[rows, :])
        om = jnp.where(is_ctx, omc_ref[rows, :], oml_ref[rows, :])
        osw = jnp.where(is_ctx, osc_ref[rows, :], osl_ref[rows, :])
        h = h_ref[rows, :]

        def gate(c):
            lo, hi = c * D_MODEL, (c + 1) * D_MODEL
            return jax.nn.sigmoid(_dot(h, wg_ref[:, lo:hi]) + bg_ref[:, lo:hi])

        merged = (gate(0) * _dot(fn, wf_ref[...]) + gate(1) * _dot(om, wm_ref[...])
                  + gate(2) * _dot(osw, ws_ref[...]))
        mix = _dot(merged.astype(BF16), wo_ref[...])
        x = jnp.where(is_ctx, xc_ref[rows, :], xl_ref[rows, :])
        x1 = x + mod_ref[:, 2048:3072] * _rms_rows(mix, g1_ref[...])
        x1_ref[rows, :] = x1
        h2 = _rms_rows(x1, g2_ref[...]) * (1.0 + mod_ref[:, 4096:5120]) + mod_ref[:, 3072:4096]
        h2_ref[0, rows, :], h2_ref[1, rows, :] = _pack_pair(h2)
        cols = pl.ds(r * TM, TM)
        eidx_ref[:, cols], epos_ref[:, cols], ew_ref[:, cols], carry = _route(
            h2.astype(BF16), rwt_ref, rb_ref, tri_ref, carry)
    carry_ref[...] = carry
    cnt_ref[...] = jnp.broadcast_to(carry, cnt_ref.shape).astype(jnp.int32)


def _stage_e(layer, xc, xl, modt, g1, g2, mixed, h, w_gate, b_gate, wf, wm, ws, wo, rwt, rbias, tri):
    row = lambda w: pl.BlockSpec((TB, w), lambda i: (i, 0))
    ctx, lat = _ctx_rows(512), _lat_rows(512)
    col = lambda dt: (pl.BlockSpec((8, TB), lambda i: (0, i)), jax.ShapeDtypeStruct((8, N_TOK), dt))
    picks = [col(jnp.int32), col(jnp.int32), col(F32)]
    return pl.pallas_call(
        _stage_e_kernel,
        grid=(NB,),
        in_specs=[
            _ctx_rows(D_MODEL), _lat_rows(D_MODEL),
            pl.BlockSpec((None, 1, N_MOD * D_MODEL), lambda i: (i, 0, 0)),
            _const_spec((1, D_MODEL)), _const_spec((1, D_MODEL)),
            ctx, lat, ctx, lat, ctx, lat, row(D_MODEL),
            _layer_spec((D_MODEL, 3 * D_MODEL), layer), _const_spec((1, 3 * D_MODEL)),
            _layer_spec((512, D_MODEL), layer), _layer_spec((512, D_MODEL), layer),
            _layer_spec((512, D_MODEL), layer), _layer_spec((D_MODEL, D_MODEL), layer),
            _const_spec((N_EXPERTS, D_MODEL)), _const_spec((N_EXPERTS, 1)), _const_spec((TM, TM)),
        ],
        out_specs=[row(D_MODEL), pl.BlockSpec((2, TB, PACKED), lambda i: (0, i, 0))] + [s for s, _ in picks]
        + [_const_spec((N_EXPERTS, LANES))],
        out_shape=[jax.ShapeDtypeStruct((N_TOK, D_MODEL), F32),
                   jax.ShapeDtypeStruct((2, N_TOK, PACKED), jnp.int32)] + [s for _, s in picks]
        + [jax.ShapeDtypeStruct((N_EXPERTS, LANES), jnp.int32)],
        scratch_shapes=[pltpu.VMEM((N_EXPERTS, 1), F32)],
        compiler_params=_cparams(1),
        name="stage_e",
    )(xc, xl, modt, g1, g2, *mixed, h, w_gate, b_gate, wf, wm, ws, wo, rwt, rbias, tri)


def _expert_kernel(layer, te_ref, tv_ref, par_ref, nxt_ref, x_ref, w1_hbm, w3_hbm, w2_hbm, o_ref,
                   w1f_ref, w3f_ref, w2f_ref, w1b_ref, w3b_ref, w2b_ref, sem):
    j = pl.program_id(0)
    valid = tv_ref[j] > 0
    first = jnp.logical_and(valid, jnp.logical_or(j == 0, te_ref[j] != te_ref[jnp.maximum(j - 1, 0)]))

    def copies(expert, slot):
        return [pltpu.make_async_copy(w_hbm.at[layer, expert], buf.at[slot], sem.at[slot, i])
                for i, (w_hbm, buf) in enumerate(((w1_hbm, w1f_ref), (w3_hbm, w3f_ref), (w2_hbm, w2f_ref)))]

    @pl.when(jnp.logical_and(valid, j == 0))
    def _():
        for cp in copies(te_ref[0], par_ref[0]):
            cp.start()

    @pl.when(first)
    def _():
        slot = par_ref[j]
        for cp in copies(te_ref[j], slot):
            cp.wait()

        @pl.when(nxt_ref[j] >= 0)
        def _():
            for cp in copies(nxt_ref[j], 1 - slot):
                cp.start()

        w1b_ref[...] = w1f_ref[slot].astype(BF16)
        w3b_ref[...] = w3f_ref[slot].astype(BF16)
        w2b_ref[...] = w2f_ref[slot].astype(BF16)

    def run(n_chunks):
        for r in range(n_chunks):
            rows = pl.ds(r * EXPERT_ROWS, EXPERT_ROWS)
            x = _unpack_pair(x_ref[0, rows, :], x_ref[1, rows, :]).astype(BF16)
            hg = _dot(x, w1b_ref[...])
            hu = _dot(x, w3b_ref[...])
            act = (jax.nn.silu(hg) * hu).astype(BF16)
            o_ref[0, rows, :], o_ref[1, rows, :] = _pack_pair(_dot(act, w2b_ref[...]))

    for n_chunks in range(1, TE // EXPERT_ROWS + 1):
        pl.when(tv_ref[j] == n_chunks)(functools.partial(run, n_chunks))


def _experts(layer, tile_expert, tile_chunks, tile_slot, tile_next, xs, w1, w3, w2):
    slot_rows = pl.BlockSpec((2, TE, PACKED), lambda j, te, tv, par, nxt: (0, j, 0))
    anywhere = pl.BlockSpec(memory_space=pl.ANY)
    grid_spec = pltpu.PrefetchScalarGridSpec(
        num_scalar_prefetch=4,
        grid=(NTE,),
        in_specs=[slot_rows, anywhere, anywhere, anywhere],
        out_specs=slot_rows,
        scratch_shapes=[pltpu.VMEM((2, D_MODEL, EXPERT_FF), F32), pltpu.VMEM((2, D_MODEL, EXPERT_FF), F32),
                        pltpu.VMEM((2, EXPERT_FF, D_MODEL), F32),
                        pltpu.VMEM((D_MODEL, EXPERT_FF), BF16), pltpu.VMEM((D_MODEL, EXPERT_FF), BF16),
                        pltpu.VMEM((EXPERT_FF, D_MODEL), BF16),
                        pltpu.SemaphoreType.DMA((2, 3))],
    )
    return pl.pallas_call(
        functools.partial(_expert_kernel, layer),
        grid_spec=grid_spec,
        out_shape=jax.ShapeDtypeStruct((2, S_MAX, PACKED), jnp.int32),
        compiler_params=_cparams(1),
        name="experts",
    )(tile_expert, tile_chunks, tile_slot, tile_next, xs, w1, w3, w2)


def _sc_mesh():
    return plsc.VectorSubcoreMesh(core_axis_name="c", subcore_axis_name="s",
                                  num_cores=SC_CORES, num_subcores=SC_SUBCORES)


def _sc_scatter_rows(rows, slot8):
    @functools.partial(pl.kernel, mesh=_sc_mesh(), scratch_types=[pltpu.SemaphoreType.DMA],
                       out_type=jax.ShapeDtypeStruct((2, S_MAX, PACKED), jnp.int32))
    def scatter(x_hbm, i_hbm, o_hbm, sem):
        for h in range(2):
            dst = o_hbm.at[h]

            def body(x_vmem, i_vmem, dst=dst):
                copies = [pltpu.async_copy(x_vmem, dst.at[i_vmem.at[k]], sem) for k in range(TOP_K)]
                for cp in copies:
                    cp.wait()

            pltpu.emit_pipeline(
                body,
                grid=(N_TOK // SC_ROWS,),
                in_specs=[pl.BlockSpec((SC_ROWS, PACKED), lambda i: (i, 0)),
                          pl.BlockSpec((8, SC_ROWS), lambda i: (0, i))],
                out_specs=[],
                core_axis_name=("c", "s"),
                dimension_semantics=(pltpu.PARALLEL,),
            )(x_hbm.at[h], i_hbm)

    return scatter(rows, slot8)


def _sc_gather_rows(table, idx):
    n = idx.shape[1]

    @functools.partial(pl.kernel, mesh=_sc_mesh(), scratch_types=[],
                       out_type=jax.ShapeDtypeStruct((2, n, PACKED), jnp.int32))
    def gather(t_hbm, i_hbm, o_hbm):
        for h in range(2):
            src = t_hbm.at[h]

            def body(i_vmem, o_vmem, src=src):
                pltpu.sync_copy(src.at[i_vmem.at[0]], o_vmem)

            pltpu.emit_pipeline(
                body,
                grid=(n // SC_ROWS,),
                in_specs=[pl.BlockSpec((1, SC_ROWS), lambda i: (0, i))],
                out_specs=[pl.BlockSpec((SC_ROWS, PACKED), lambda i: (i, 0))],
                core_axis_name=("c", "s"),
                dimension_semantics=(pltpu.PARALLEL,),
            )(i_hbm, o_hbm.at[h])

    return gather(table, idx)


def _stage_g_kernel(x1_ref, mod_ref, g3_ref, yg_ref, ew_ref, h2_ref,
                    s1_ref, s3_ref, s2_ref, oc_ref, ol_ref):
    is_ctx = pl.program_id(0) < NB_CTX
    for rows in _chunks():
        h = _unpack_pair(h2_ref[0, rows, :], h2_ref[1, rows, :]).astype(BF16)
        act = jax.nn.silu(_dot(h, s1_ref[...])) * _dot(h, s3_ref[...])
        y = _dot(act.astype(BF16), s2_ref[...])
        for k in range(TOP_K):
            y = y + ew_ref[rows, k:k + 1] * _unpack_pair(yg_ref[0, k, rows, :], yg_ref[1, k, rows, :])
        out = x1_ref[rows, :] + mod_ref[:, 5120:6144] * _rms_rows(y, g3_ref[...])

        @pl.when(is_ctx)
        def _():
            oc_ref[rows, :] = out

        @pl.when(jnp.logical_not(is_ctx))
        def _():
            ol_ref[rows, :] = out


def _stage_g(layer, x1, modt, g3, yg, ew_rows, h2, s1, s3, s2):
    row = lambda w: pl.BlockSpec((TB, w), lambda i: (i, 0))
    picked = pl.BlockSpec((2, TOP_K, TB, PACKED), lambda i: (0, 0, i, 0))
    return pl.pallas_call(
        _stage_g_kernel,
        grid=(NB,),
        in_specs=[row(D_MODEL), pl.BlockSpec((None, 1, N_MOD * D_MODEL), lambda i: (i, 0, 0)),
                  _const_spec((1, D_MODEL)), picked, row(8),
                  pl.BlockSpec((2, TB, PACKED), lambda i: (0, i, 0)),
                  _layer_spec((D_MODEL, SHARED_FF), layer), _layer_spec((D_MODEL, SHARED_FF), layer),
                  _layer_spec((SHARED_FF, D_MODEL), layer)],
        out_specs=[_ctx_rows(D_MODEL), _lat_rows(D_MODEL)],
        out_shape=[jax.ShapeDtypeStruct((N_CTX, D_MODEL), F32),
                   jax.ShapeDtypeStruct((N_LAT, D_MODEL), F32)],
        compiler_params=_cparams(1),
        name="stage_g",
    )(x1, modt, g3, yg, ew_rows, h2, s1, s3, s2)


def _rope_tables():
    t = np.arange(DEC_SEQ)
    pos = np.stack([(t // GRID_W), (t % GRID_W)], axis=-1).astype(np.float32)

    def table(r):
        n_freq = r // 4
        inv = np.float32(ROPE_BASE) ** (-np.arange(n_freq, dtype=np.float32) / np.float32(n_freq))
        ang = pos[:, :, None] * inv.astype(np.float32)
        cos = np.cos(ang)
        sin = np.sin(ang)
        cos_t = np.stack([cos, cos], axis=2).reshape(DEC_SEQ, r)
        sin_t = np.stack([-sin, sin], axis=2).reshape(DEC_SEQ, r)
        return cos_t, sin_t

    c64, s64 = table(SWA_HEAD_DIM)
    c32, s32 = table(MLA_ROPE)
    lat = np.concatenate([np.tile(c64, (1, 8)), np.tile(s64, (1, 8)),
                          np.tile(c32, (1, 4)), np.tile(s32, (1, 4))], axis=1)
    ident = np.concatenate([np.ones((TB, 512)), np.zeros((TB, 512)),
                            np.ones((TB, 128)), np.zeros((TB, 128))], axis=1)
    return jnp.asarray(np.concatenate([ident, lat], axis=0).astype(np.float32))


def _dft_pair(n):
    k = np.arange(n, dtype=np.int64)
    ang = ((k[:, None] * k[None, :]) % n).astype(np.float64) * (2.0 * math.pi / n)
    return np.cos(ang), np.sin(ang)


def _fnet_tables():
    c64, s64 = _dft_pair(FNET_GROUP_DIM)
    eye = np.eye(FNET_GROUPS)
    bd = np.concatenate([np.kron(eye, c64), np.kron(eye, s64)], axis=1)
    mats = []
    for t_len in (SEQ, DEC_SEQ):
        c, s = _dft_pair(t_len)
        mats.append(np.concatenate([c, -s], axis=1))
    return tuple(jnp.asarray(m.astype(np.float32).astype(BF16)) for m in (bd, mats[0], mats[1]))


def _layer_weights(l, w_in, w_uq, w_ukv):
    w = w_in[l]
    wide = jnp.concatenate([w[:, 0:1024], w[:, 1056:1824], w[:, 1024:1056],
                            jnp.zeros((D_MODEL, 96), F32)], axis=1).astype(BF16)

    uq = w_uq[l].reshape(MLA_Q_RANK, MLA_HEADS, MLA_NOPE + MLA_ROPE)
    z32 = jnp.zeros((MLA_Q_RANK, MLA_HEADS, 32), F32)
    wqa = jnp.concatenate([uq, z32], axis=2).reshape(MLA_Q_RANK, 1024).astype(BF16)
    ukv = w_ukv[l].reshape(MLA_KV_RANK, MLA_HEADS, MLA_NOPE + MLA_V)
    wk = jnp.concatenate([ukv[:, :, :MLA_NOPE], jnp.zeros((MLA_KV_RANK, MLA_HEADS, 64), F32)],
                         axis=2).reshape(MLA_KV_RANK, 1024).astype(BF16)
    wv = ukv[:, :, MLA_NOPE:].reshape(MLA_KV_RANK, 512).astype(BF16)
    return wide, wqa, wk, wv


def _rope_placement():
    e = np.zeros((128, 1024), np.float32)
    for hd in range(MLA_HEADS):
        for i in range(MLA_ROPE):
            e[i, hd * 128 + MLA_NOPE + i] = 1.0
    return jnp.asarray(e, BF16)


def _moe_dispatch_plan(eidx, epos, counts):
    padded = ((counts + TE - 1) // TE) * TE
    ends = jnp.cumsum(padded)
    offs = ends - padded
    ids = jnp.arange(N_EXPERTS, dtype=jnp.int32)
    picked_off = jnp.sum(jnp.where(eidx[:, :, None] == ids, offs, 0), axis=-1)
    slot = picked_off + epos
    starts = jnp.arange(NTE, dtype=jnp.int32) * TE
    tile_expert = jnp.sum((ends[None, :] <= starts[:, None]).astype(jnp.int32), axis=1)
    tile_expert = jnp.minimum(tile_expert, N_EXPERTS - 1)
    pick = tile_expert[:, None] == ids[None, :]
    last_real = jnp.sum(jnp.where(pick, (offs + counts)[None, :], 0), axis=1)
    n_real = jnp.clip(last_real - starts, 0, TE)
    n_real = jnp.where(starts < ends[-1], n_real, 0)
    tile_chunks = ((n_real + EXPERT_ROWS - 1) // EXPERT_ROWS).astype(jnp.int32)
    used = counts > 0
    rank = jnp.cumsum(used.astype(jnp.int32)) - 1
    tile_slot = jnp.sum(jnp.where(pick, rank[None, :], 0), axis=1) % 2
    later_used = jnp.logical_and(ids[None, :] > ids[:, None], used[None, :])
    next_used = jnp.min(jnp.where(later_used, ids[None, :], N_EXPERTS), axis=1)
    next_used = jnp.where(next_used < N_EXPERTS, next_used, -1)
    tile_next = jnp.sum(jnp.where(pick, next_used[None, :], 0), axis=1)
    return slot, tile_expert, tile_chunks, tile_slot.astype(jnp.int32), tile_next.astype(jnp.int32)


def kernel(x_prompt, x_sample, cache_mla_ckv, cache_mla_krope, cache_swa_k, cache_swa_v, c, c_ctx,
           ada_w, ada_b, norm_g, w_in, q_norm, kv_norm, w_fnet, w_uq, w_ukv, w_mla_o, swa_sink,
           w_swa_o, w_gate, b_gate, w_out, router_w, router_bias, exp_w1, exp_w3, exp_w2,
           shared_w1, shared_w3, shared_w2):
    xc = x_prompt.reshape(N_CTX, D_MODEL)
    xl = x_sample.reshape(N_LAT, D_MODEL)

    cond8 = jnp.concatenate([c_ctx[None, :], c, jnp.zeros((3, D_MODEL), F32)], axis=0)
    mod = _modulation(cond8, ada_w, ada_b)
    tile_cond = np.concatenate([np.zeros(NB_CTX, np.int32),
                                1 + np.arange(NB - NB_CTX, dtype=np.int32) // LAT_BLOCKS])

    tab = _rope_tables()
    bd, f_ctx, f_lat = _fnet_tables()
    e_mat = _rope_placement()
    tri = jnp.asarray(np.triu(np.ones((TM, TM), np.float32), 1), BF16)
    w_gate_b, w_fnet_b, w_mla_o_b, w_swa_o_b, w_out_b, sw1_b, sw3_b, sw2_b = (
        w.astype(BF16) for w in (w_gate, w_fnet, w_mla_o, w_swa_o, w_out, shared_w1, shared_w3, shared_w2))

    new_ckv, new_kr, new_k, new_v = [], [], [], []
    for l in range(DEPTH):
        modt = mod[l][tile_cond][:, None, :]
        wide, wqa, wk, wv = _layer_weights(l, w_in, w_uq, w_ukv)
        ng = norm_g[l]

        fin, ckv, kr, sq, sk, sv, h1, q_m, k_m, v_m = _stage_a(
            xc, xl, modt, ng[0:1], wide, q_norm[l][None, :], kv_norm[l][None, :], tab, wqa, wk, e_mat, wv)

        new_ckv.append(ckv[:N_CTX].reshape(BATCH, SEQ, MLA_KV_RANK))
        new_kr.append(kr[:N_CTX, :MLA_ROPE].reshape(BATCH, SEQ, MLA_ROPE))
        new_k.append(sk[:N_CTX].reshape(BATCH, SEQ, SWA_KV_HEADS, SWA_HEAD_DIM))
        new_v.append(sv[:N_CTX].reshape(BATCH, SEQ, SWA_KV_HEADS, SWA_HEAD_DIM))

        fn = (_fnet(fin, f_ctx, bd, BATCH, SEQ, 0),
              _fnet(fin, f_lat, bd, DEC_BATCH, DEC_SEQ, N_CTX // DEC_SEQ))

        kr_cache = jnp.pad(cache_mla_krope[:, l].reshape(N_CACHE, MLA_ROPE), ((0, 0), (0, 96)))
        k_c, v_c = _mla_cache_kv(cache_mla_ckv[:, l].reshape(N_CACHE, MLA_KV_RANK), kr_cache,
                                 wk, e_mat, wv)
        om = (_mla_attn(q_m, k_m, v_m, k_c, v_c, latent=False),
              _mla_attn(q_m, k_m, v_m, k_c, v_c, latent=True))

        ck = cache_swa_k[:, l].reshape(DEC_BATCH, PAST_LEN, 128)
        cv = cache_swa_v[:, l].reshape(DEC_BATCH, PAST_LEN, 128)
        osw = (_swa_attn(swa_sink[l], sq, sk, sv, ck, cv, latent=False),
               _swa_attn(swa_sink[l], sq, sk, sv, ck, cv, latent=True))

        x1, h2, eidx, epos, ew, counts = _stage_e(
            l, xc, xl, modt, ng[1:2], ng[2:3], fn + om + osw, h1, w_gate_b, b_gate[l][None, :],
            w_fnet_b, w_mla_o_b, w_swa_o_b, w_out_b,
            router_w[l].T.astype(BF16), router_bias[l][:, None], tri)
        slot, tile_expert, tile_chunks, tile_slot, tile_next = _moe_dispatch_plan(eidx, epos, counts[:, 0])
        xs = _sc_scatter_rows(h2, slot)
        ys = _experts(l, tile_expert, tile_chunks, tile_slot, tile_next, xs, exp_w1, exp_w3, exp_w2)
        picks = slot[:TOP_K].reshape(1, TOP_K * N_TOK)
        yg = _sc_gather_rows(ys, picks).reshape(2, TOP_K, N_TOK, PACKED)
        xc, xl = _stage_g(l, x1, modt, ng[3:4], yg, ew.T, h2, sw1_b, sw3_b, sw2_b)

    y_p = xc.reshape(BATCH, SEQ, D_MODEL)
    y_s = xl.reshape(DEC_BATCH, DEC_SEQ, D_MODEL)
    return (y_p, y_s, jnp.stack(new_ckv, axis=1), jnp.stack(new_kr, axis=1),
            jnp.stack(new_k, axis=1), jnp.stack(new_v, axis=1))
```

```python
import functools
import math

import numpy as np
import jax
import jax.numpy as jnp
from jax import lax
from jax.experimental import pallas as pl
from jax.experimental.pallas import tpu as pltpu
from jax.experimental.pallas import tpu_sc as plsc

D_MODEL = 1024
BATCH = 16
SEQ = 256
DEPTH = 2
DEC_BATCH = 4
DEC_SEQ = 2048
PAST_LEN = 512
GRID_W = 64
EPS = 1e-6
ROPE_BASE = 10000.0
NEG_INF = -1e30

FNET_GROUPS = 8
FNET_GROUP_DIM = 64
FNET_WIDTH = 512
MLA_HEADS = 8
MLA_Q_RANK = 384
MLA_KV_RANK = 128
MLA_NOPE = 64
MLA_ROPE = 32
MLA_V = 64
MLA_SCALE = (MLA_NOPE + MLA_ROPE) ** -0.5
LOG2E = math.log2(math.e)
SWA_KV_HEADS = 2
SWA_HEAD_DIM = 64
SWA_WINDOW = 128
SWA_SCALE = SWA_HEAD_DIM ** -0.5
N_MOD = 6
N_EXPERTS = 64
N_EXPERT_GROUPS = 8
TOPK_GROUPS = 4
TOP_K = 6
EXPERT_FF = 256
SHARED_FF = 256
ROUTED_SCALE = 2.5

LANES = 128
TM = 256
N_CTX = BATCH * SEQ
N_LAT = DEC_BATCH * DEC_SEQ
N_TOK = N_CTX + N_LAT
N_CACHE = DEC_BATCH * PAST_LEN
TB = 512
NB = N_TOK // TB
NB_CTX = N_CTX // TB
LAT_BLOCKS = DEC_SEQ // TB
MLA_LAT_TQ = 256
MLA_LAT_PAIRS = 4
FNET_ROWS = 1024
TE = 512
S_MAX = N_TOK * TOP_K + N_EXPERTS * TE
NTE = S_MAX // TE
EXPERT_ROWS = 256
VMEM_LIMIT = 56 * 1024 * 1024
PACKED = D_MODEL // 4
SC_ROWS = 128
SC_CORES = 2
SC_SUBCORES = 16

A_F = (0, 512)
A_QD = (512, 896)
A_KV = (896, 1024)
A_SQ = (1024, 1536)
A_SK = (1536, 1664)
A_SV = (1664, 1792)
A_KR = (1792, 1920)
W_IN_WIDE = 1920
TAB_W = 1280

F32 = jnp.float32
BF16 = jnp.bfloat16


def _cparams(n_axes):
    return pltpu.CompilerParams(dimension_semantics=("arbitrary",) * n_axes, vmem_limit_bytes=VMEM_LIMIT)


def _dot(a, b):
    return jnp.dot(a, b, preferred_element_type=F32)


def _dot_nt(a, b):
    return lax.dot_general(a, b, (((1,), (1,)), ((), ())), preferred_element_type=F32)


def _rms_rows(v, g):
    return v * lax.rsqrt(jnp.mean(v * v, axis=-1, keepdims=True) + EPS) * g


def _pack_rows(v):
    half = v.shape[1] // 2
    lo = lax.bitcast_convert_type(v[:, :half].astype(BF16).astype(F32), jnp.int32)
    hi = lax.bitcast_convert_type(v[:, half:].astype(BF16).astype(F32), jnp.int32)
    return jnp.bitwise_or(jnp.bitwise_and(hi, -65536), jnp.bitwise_and(jnp.right_shift(lo, 16), 65535))


def _unpack_rows(w):
    lo = lax.bitcast_convert_type(jnp.left_shift(w, 16), F32)
    hi = lax.bitcast_convert_type(jnp.bitwise_and(w, -65536), F32)
    return jnp.concatenate([lo, hi], axis=1)


def _pack_pair(v):
    half = v.shape[1] // 2
    return _pack_rows(v[:, :half]), _pack_rows(v[:, half:])


def _unpack_pair(a, b):
    return jnp.concatenate([_unpack_rows(a), _unpack_rows(b)], axis=1)


def _const_spec(shape):
    return pl.BlockSpec(shape, lambda *_: (0,) * len(shape))


def _layer_spec(shape, layer):
    return pl.BlockSpec((None,) + shape, lambda *_: (layer,) + (0,) * len(shape))


def _ctx_rows(width):
    return pl.BlockSpec((TB, width), lambda i: (jnp.minimum(i, NB_CTX - 1), 0))


def _lat_rows(width):
    return pl.BlockSpec((TB, width), lambda i: (jnp.maximum(i - NB_CTX, 0), 0))


def _tab_row_block(i):
    return jnp.where(i < NB_CTX, 0, 1 + (i - NB_CTX) % LAT_BLOCKS)


def _mod_kernel(cond_ref, w_ref, b_ref, o_ref):
    c = cond_ref[...]
    a = (c * jax.nn.sigmoid(c)).astype(BF16)
    o_ref[...] = _dot(a, w_ref[...].astype(BF16)) + b_ref[...]


def _modulation(cond8, ada_w, ada_b):
    tn = 512
    nj = N_MOD * D_MODEL // tn
    return pl.pallas_call(
        _mod_kernel,
        grid=(DEPTH, nj),
        in_specs=[
            pl.BlockSpec((8, D_MODEL), lambda l, j: (0, 0)),
            pl.BlockSpec((None, D_MODEL, tn), lambda l, j: (l, 0, j)),
            pl.BlockSpec((None, 1, tn), lambda l, j: (l, 0, j)),
        ],
        out_specs=pl.BlockSpec((None, 8, tn), lambda l, j: (l, 0, j)),
        out_shape=jax.ShapeDtypeStruct((DEPTH, 8, N_MOD * D_MODEL), F32),
        compiler_params=_cparams(2),
        name="modulation",
    )(cond8, ada_w, ada_b.reshape(DEPTH, 1, N_MOD * D_MODEL))


def _half_swap(x, half):
    n = x.shape[1]
    lane = lax.broadcasted_iota(jnp.int32, (1, n), 1)
    return jnp.where((lane & half) == 0, pltpu.roll(x, n - half, 1), pltpu.roll(x, half, 1))


def _mla_expand(rows, cq, ckv, kr, cos32, sin32, wqa_ref, wk_ref, e_ref, wv_ref,
                q_ref, k_ref, v_ref):
    if q_ref is not None:
        lane = lax.broadcasted_iota(jnp.int32, (1, LANES), 1)
        rope_lane = jnp.logical_and(lane >= MLA_NOPE, lane < MLA_NOPE + MLA_ROPE)
        cos_h = jnp.where(rope_lane, cos32, 1.0)
        sin_h = jnp.where(rope_lane, sin32, 0.0)
        for hd in range(MLA_HEADS):
            lo, hi = hd * LANES, (hd + 1) * LANES
            q = _dot(cq, wqa_ref[:, lo:hi])
            q = q * cos_h + _half_swap(q, MLA_ROPE // 4) * sin_h
            q_ref[rows, lo:hi] = (q * (MLA_SCALE * LOG2E)).astype(BF16)
    k_ref[rows, :] = (_dot(ckv, wk_ref[...]) + _dot(kr, e_ref[...])).astype(BF16)
    v_ref[rows, :] = _dot(ckv, wv_ref[...]).astype(BF16)


def _chunks():
    return [pl.ds(r * TM, TM) for r in range(TB // TM)]


def _stage_a_kernel(xc_ref, xl_ref, mod_ref, g_ref, win_ref, qn_ref, kvn_ref, tab_ref,
                    wqa_ref, wk_ref, e_ref, wv_ref,
                    fin_ref, ckv_ref, kr_ref, sq_ref, sk_ref, sv_ref, h_ref,
                    qm_ref, km_ref, vm_ref):
    is_ctx = pl.program_id(0) < NB_CTX
    for rows in _chunks():
        x = jnp.where(is_ctx, xc_ref[rows, :], xl_ref[rows, :])
        h = (_rms_rows(x, g_ref[...]) * (1.0 + mod_ref[:, 1024:2048]) + mod_ref[:, 0:1024]).astype(BF16)
        h_ref[rows, :] = h

        def proj(seg):
            return _dot(h, win_ref[:, seg[0]:seg[1]])

        fin_ref[rows, :] = proj(A_F).astype(BF16)
        cq = _rms_rows(proj(A_QD), qn_ref[...]).astype(BF16)
        ckv = _rms_rows(proj(A_KV), kvn_ref[...])
        cos64 = tab_ref[rows, 0:512]
        sin64 = tab_ref[rows, 512:1024]
        sq = proj(A_SQ)
        sq = sq * cos64 + _half_swap(sq, SWA_HEAD_DIM // 4) * sin64
        sq_ref[rows, :] = (sq * (SWA_SCALE * LOG2E)).astype(BF16)
        sk = proj(A_SK)
        sk_ref[rows, :] = sk * cos64[:, 0:128] + _half_swap(sk, SWA_HEAD_DIM // 4) * sin64[:, 0:128]
        sv_ref[rows, :] = proj(A_SV)
        cos32 = tab_ref[rows, 1024:1152]
        sin32 = tab_ref[rows, 1152:1280]
        kr = proj(A_KR)
        kr = kr * cos32 + _half_swap(kr, MLA_ROPE // 4) * sin32

        @pl.when(is_ctx)
        def _():
            ckv_ref[rows, :] = ckv
            kr_ref[rows, :] = kr

        _mla_expand(rows, cq, ckv.astype(BF16), kr.astype(BF16), cos32, sin32,
                    wqa_ref, wk_ref, e_ref, wv_ref, qm_ref, km_ref, vm_ref)


def _stage_a(xc, xl, modt, g0, w_in_wide, q_norm, kv_norm, tab, wqa, wk, e_mat, wv):
    row = lambda w: pl.BlockSpec((TB, w), lambda i: (i, 0))
    every = lambda w, dt: (row(w), jax.ShapeDtypeStruct((N_TOK, w), dt))
    ctx_only = lambda w, dt: (_ctx_rows(w), jax.ShapeDtypeStruct((N_CTX, w), dt))
    outs = [every(512, BF16), ctx_only(128, F32), ctx_only(128, F32), every(512, BF16), every(128, F32),
            every(128, F32), every(D_MODEL, BF16), every(1024, BF16), every(1024, BF16), every(512, BF16)]
    return pl.pallas_call(
        _stage_a_kernel,
        grid=(NB,),
        in_specs=[
            _ctx_rows(D_MODEL), _lat_rows(D_MODEL),
            pl.BlockSpec((None, 1, N_MOD * D_MODEL), lambda i: (i, 0, 0)),
            _const_spec((1, D_MODEL)),
            _const_spec((D_MODEL, W_IN_WIDE)),
            _const_spec((1, MLA_Q_RANK)),
            _const_spec((1, MLA_KV_RANK)),
            pl.BlockSpec((TB, TAB_W), lambda i: (_tab_row_block(i), 0)),
            _const_spec((MLA_Q_RANK, 1024)),
            _const_spec((128, 1024)), _const_spec((128, 1024)), _const_spec((128, 512)),
        ],
        out_specs=[s for s, _ in outs],
        out_shape=[s for _, s in outs],
        compiler_params=_cparams(1),
        name="stage_a",
    )(xc, xl, modt, g0, w_in_wide, q_norm, kv_norm, tab, wqa, wk, e_mat, wv)


def _fnet_kernel(t_len, scale, fin_ref, f_ref, bd_ref, o_ref, zz_ref):
    @pl.when(pl.program_id(1) == 0)
    def _():
        z = fin_ref[...]
        zz_ref[0:t_len, :] = _dot(z, bd_ref[:, 0:512]).astype(BF16)
        zz_ref[t_len:2 * t_len, :] = _dot(z, bd_ref[:, 512:1024]).astype(BF16)

    o_ref[...] = (_dot(f_ref[...], zz_ref[...]) * scale).astype(BF16)


def _fnet(fin, fmat, bd, n_batch, t_len, row_block0):
    scale = 1.0 / math.sqrt(t_len * FNET_GROUP_DIM)
    ft = min(t_len, FNET_ROWS)
    return pl.pallas_call(
        functools.partial(_fnet_kernel, t_len, scale),
        grid=(n_batch, t_len // ft),
        in_specs=[
            pl.BlockSpec((t_len, FNET_WIDTH), lambda b, i: (row_block0 + b, 0)),
            pl.BlockSpec((ft, 2 * t_len), lambda b, i: (i, 0)),
            _const_spec((FNET_WIDTH, 2 * FNET_WIDTH)),
        ],
        out_specs=pl.BlockSpec((ft, FNET_WIDTH), lambda b, i: (b * (t_len // ft) + i, 0)),
        out_shape=jax.ShapeDtypeStruct((n_batch * t_len, FNET_WIDTH), BF16),
        scratch_shapes=[pltpu.VMEM((2 * t_len, FNET_WIDTH), BF16)],
        compiler_params=_cparams(2),
        name=f"fnet_{t_len}",
    )(fin, fmat, bd)


def _mla_cache_kernel(ckv_ref, kr_ref, wk_ref, e_ref, wv_ref, k_ref, v_ref):
    _mla_expand(slice(None), None, ckv_ref[...].astype(BF16), kr_ref[...].astype(BF16), None, None,
                None, wk_ref, e_ref, wv_ref, None, k_ref, v_ref)


def _mla_cache_kv(ckv_cache, kr_cache, wk, e_mat, wv):
    row = lambda w: pl.BlockSpec((TM, w), lambda i: (i, 0))
    return pl.pallas_call(
        _mla_cache_kernel,
        grid=(N_CACHE // TM,),
        in_specs=[row(128), row(128),
                  _const_spec((128, 1024)), _const_spec((128, 1024)), _const_spec((128, 512))],
        out_specs=[row(1024), row(512)],
        out_shape=[jax.ShapeDtypeStruct((N_CACHE, 1024), BF16),
                   jax.ShapeDtypeStruct((N_CACHE, 512), BF16)],
        compiler_params=_cparams(1),
        name="mla_cache_kv",
    )(ckv_cache, kr_cache, wk, e_mat, wv)


def _mla_attn_kernel(n_seg, pairs, q_ref, *refs):
    k_refs = refs[0:n_seg]
    v_refs = refs[n_seg:2 * n_seg]
    o_ref = refs[2 * n_seg]
    lane = lax.broadcasted_iota(jnp.int32, (1, LANES), 1)
    low = lane < MLA_V
    for pr in range(pairs):
        outs = []
        for hh in range(2):
            hd = 2 * pr + hh
            q = q_ref[:, hd * LANES:(hd + 1) * LANES]
            ss = [_dot_nt(q, k[:, hd * LANES:(hd + 1) * LANES]) for k in k_refs]
            m = functools.reduce(jnp.maximum, [s.max(axis=-1, keepdims=True) for s in ss])
            keep = low if hh == 0 else jnp.logical_not(low)
            sum_lane = MLA_V if hh == 0 else 0
            po = None
            for s, v_ref in zip(ss, v_refs):
                v = v_ref[:, pr * LANES:(pr + 1) * LANES]
                vm = jnp.where(lane == sum_lane, jnp.ones_like(v), jnp.where(keep, v, jnp.zeros_like(v)))
                t = _dot(jnp.exp2(s - m).astype(BF16), vm)
                po = t if po is None else po + t
            outs.append(po / po[:, sum_lane:sum_lane + 1])
        o_ref[:, pr * LANES:(pr + 1) * LANES] = jnp.where(low, outs[0], outs[1]).astype(BF16)


def _mla_attn(q_all, k_all, v_all, k_cache, v_cache, latent):
    if latent:
        tq, pairs = MLA_LAT_TQ, MLA_LAT_PAIRS
        n_b, n_q = DEC_BATCH, DEC_SEQ // tq
        q0 = N_CTX // tq
        kv_specs = [
            pl.BlockSpec((PAST_LEN, 256 * pairs), lambda b, hp, i: (b, hp)),
            pl.BlockSpec((DEC_SEQ, 256 * pairs), lambda b, hp, i: (N_CTX // DEC_SEQ + b, hp)),
            pl.BlockSpec((PAST_LEN, 128 * pairs), lambda b, hp, i: (b, hp)),
            pl.BlockSpec((DEC_SEQ, 128 * pairs), lambda b, hp, i: (N_CTX // DEC_SEQ + b, hp)),
        ]
        args = (q_all, k_cache, k_all, v_cache, v_all)
        n_seg = 2
    else:
        tq, pairs = SEQ, MLA_HEADS // 2
        n_b, n_q = BATCH, 1
        q0 = 0
        kv_specs = [
            pl.BlockSpec((SEQ, 256 * pairs), lambda b, hp, i: (b, hp)),
            pl.BlockSpec((SEQ, 128 * pairs), lambda b, hp, i: (b, hp)),
        ]
        args = (q_all, k_all, v_all)
        n_seg = 1
    return pl.pallas_call(
        functools.partial(_mla_attn_kernel, n_seg, pairs),
        grid=(n_b, MLA_HEADS // (2 * pairs), n_q),
        in_specs=[pl.BlockSpec((tq, 256 * pairs), lambda b, hp, i: (q0 + b * n_q + i, hp))] + kv_specs,
        out_specs=pl.BlockSpec((tq, 128 * pairs), lambda b, hp, i: (b * n_q + i, hp)),
        out_shape=jax.ShapeDtypeStruct((n_b * n_q * tq, MLA_HEADS * MLA_V), BF16),
        compiler_params=_cparams(3),
        name="mla_attn_lat" if latent else "mla_attn_ctx",
    )(*args)


def _swa_kernel(windowed, n_steps, sink_ref, q_ref, *refs):
    n_seg = 4 if windowed else 1
    k_refs = refs[0:n_seg]
    v_refs = refs[n_seg:2 * n_seg]
    o_ref = refs[2 * n_seg]
    step = pl.program_id(1)
    lane = lax.broadcasted_iota(jnp.int32, (1, LANES), 1)
    low = lane < SWA_HEAD_DIM
    high = jnp.logical_not(low)

    k_all = jnp.concatenate([r[...] for r in k_refs], axis=0)
    v_all = jnp.concatenate([r[...] for r in v_refs], axis=0)
    k_sw = pltpu.roll(k_all, SWA_HEAD_DIM, 1)
    v_sw = pltpu.roll(v_all, SWA_HEAD_DIM, 1)

    if windowed:
        tq = SWA_WINDOW
        qi = lax.broadcasted_iota(jnp.int32, (2 * tq, tq), 0) % tq
        kj = lax.broadcasted_iota(jnp.int32, (2 * tq, tq), 1)
        after = kj >= qi
        before = kj <= qi
        biases = [(jnp.where(jnp.logical_and(after, step > 0), 0.0, NEG_INF),
                   jnp.where(before, 0.0, NEG_INF)),
                  (jnp.where(after, 0.0, NEG_INF),
                   jnp.where(jnp.logical_and(before, step < n_steps - 1), 0.0, NEG_INF))]
    else:
        tq = q_ref.shape[0]
    n_sub = q_ref.shape[0] // tq
    top_rows = lax.broadcasted_iota(jnp.int32, (2 * tq, 1), 0) < tq

    for g in range(SWA_KV_HEADS):
        kh, vh = [], []
        for half in range(2):
            keep = low if half == 0 else high
            sum_lane = SWA_HEAD_DIM if half == 0 else 0
            straight = (g == half)
            kh.append(jnp.where(keep, k_all if straight else k_sw, 0.0).astype(BF16))
            vh.append(jnp.where(lane == sum_lane, 1.0,
                                jnp.where(keep, v_all if straight else v_sw, 0.0)).astype(BF16))
        for sub in range(n_sub):
            rows = slice(sub * tq, (sub + 1) * tq)
            qs = jnp.concatenate([q_ref[rows, 256 * g:256 * g + 128],
                                  q_ref[rows, 256 * g + 128:256 * g + 256]], axis=0)
            halves = []
            for half in range(2):
                sum_lane = SWA_HEAD_DIM if half == 0 else 0
                ks, vs = kh[half], vh[half]
                if windowed:
                    w0 = PAST_LEN + sub * tq
                    if sub == 0:
                        ks, vs = ks[0:w0 + 3 * tq], vs[0:w0 + 3 * tq]
                    else:
                        ks = jnp.concatenate([ks[0:PAST_LEN], ks[w0:w0 + 3 * tq]], axis=0)
                        vs = jnp.concatenate([vs[0:PAST_LEN], vs[w0:w0 + 3 * tq]], axis=0)
                s = _dot_nt(qs, ks)
                if windowed:
                    c0, c1, c2 = PAST_LEN, PAST_LEN + tq, PAST_LEN + 2 * tq
                    s = jnp.concatenate([s[:, :c0], s[:, c0:c1] + biases[sub][0], s[:, c1:c2],
                                         s[:, c2:] + biases[sub][1]], axis=1)
                sink = jnp.where(top_rows, sink_ref[4 * g + half], sink_ref[4 * g + 2 + half]) * LOG2E
                m = jnp.maximum(s.max(axis=-1, keepdims=True), sink)
                po = _dot(jnp.exp2(s - m).astype(BF16), vs)
                halves.append(po / (po[:, sum_lane:sum_lane + 1] + jnp.exp2(sink - m)))
            out = jnp.where(low, halves[0], halves[1])
            o_ref[rows, 256 * g:256 * g + 128] = out[0:tq].astype(BF16)
            o_ref[rows, 256 * g + 128:256 * g + 256] = out[tq:2 * tq].astype(BF16)


def _swa_attn(sink, sq, sk, sv, cache_k, cache_v, latent):
    smem = pl.BlockSpec(memory_space=pltpu.SMEM)
    if latent:
        tq = 2 * SWA_WINDOW
        n_b, n_qb = DEC_BATCH, DEC_SEQ // tq
        base = N_CTX // tq
        last = DEC_SEQ // SWA_WINDOW - 1

        def prev(b, i):
            return (2 * (base + b * n_qb) + jnp.maximum(2 * i - 1, 0), 0)

        def cur(b, i):
            return (base + b * n_qb + i, 0)

        def nxt(b, i):
            return (2 * (base + b * n_qb) + jnp.minimum(2 * i + 2, last), 0)

        cache = pl.BlockSpec((None, PAST_LEN, 128), lambda b, i: (b, 0, 0))
        edge = lambda f: pl.BlockSpec((SWA_WINDOW, 128), f)
        kv_specs = [cache, edge(prev), pl.BlockSpec((tq, 128), cur), edge(nxt)] * 2
        args = (cache_k, sk, sk, sk, cache_v, sv, sv, sv)
        q_spec = pl.BlockSpec((tq, 512), cur)
        o_spec = pl.BlockSpec((tq, 512), lambda b, i: (b * n_qb + i, 0))
    else:
        tq = SEQ
        n_b, n_qb = BATCH, 1
        blk = pl.BlockSpec((tq, 128), lambda b, i: (b, 0))
        kv_specs = [blk, blk]
        args = (sk, sv)
        q_spec = pl.BlockSpec((tq, 512), lambda b, i: (b, 0))
        o_spec = q_spec
    return pl.pallas_call(
        functools.partial(_swa_kernel, latent, n_qb),
        grid=(n_b, n_qb),
        in_specs=[smem, q_spec] + kv_specs,
        out_specs=o_spec,
        out_shape=jax.ShapeDtypeStruct((n_b * n_qb * tq, 512), BF16),
        compiler_params=_cparams(2),
        name="swa_lat" if latent else "swa_ctx",
    )(sink, sq, *args)


def _route(h, rwt_ref, rb_ref, tri_ref, carry):
    n = h.shape[0]
    gsz = N_EXPERTS // N_EXPERT_GROUPS
    scores = jax.nn.sigmoid(_dot_nt(rwt_ref[...], h))
    biased = scores + rb_ref[...]
    mem = lax.broadcasted_iota(jnp.int32, (gsz, n), 0).astype(F32)
    gs_rows = []
    for g in range(N_EXPERT_GROUPS):
        bg = biased[g * gsz:(g + 1) * gsz, :]
        m1 = bg.max(axis=0, keepdims=True)
        first = jnp.min(jnp.where(bg == m1, mem, float(gsz)), axis=0, keepdims=True)
        m2 = jnp.where(mem == first, -jnp.inf, bg).max(axis=0, keepdims=True)
        gs_rows.append(m1 + m2)
    gs = jnp.concatenate(gs_rows, axis=0)
    gid = lax.broadcasted_iota(jnp.int32, gs.shape, 0).astype(F32)
    gsel = jnp.zeros(gs.shape, F32)
    for _ in range(TOPK_GROUPS):
        mx = gs.max(axis=0, keepdims=True)
        pick = gid == jnp.min(jnp.where(gs == mx, gid, float(N_EXPERT_GROUPS)), axis=0, keepdims=True)
        gsel = jnp.where(pick, 1.0, gsel)
        gs = jnp.where(pick, -jnp.inf, gs)
    emask = jnp.concatenate(
        [jnp.broadcast_to(gsel[g:g + 1, :], (gsz, n)) for g in range(N_EXPERT_GROUPS)], axis=0)
    cand = jnp.where(emask > 0.5, biased, NEG_INF)
    eid = lax.broadcasted_iota(jnp.int32, cand.shape, 0).astype(F32)
    picks = []
    self32 = jnp.zeros(cand.shape, F32)
    for _ in range(TOP_K):
        mx = cand.max(axis=0, keepdims=True)
        pick = eid == jnp.min(jnp.where(cand == mx, eid, float(N_EXPERTS)), axis=0, keepdims=True)
        picks.append(pick)
        self32 = jnp.where(pick, 1.0, self32)
        cand = jnp.where(pick, -jnp.inf, cand)
    pos = _dot(self32.astype(BF16), tri_ref[...]) + carry
    sel_scores = [jnp.sum(jnp.where(p, scores, 0.0), axis=0, keepdims=True) for p in picks]
    wsum = functools.reduce(lambda a, b: a + b, sel_scores)
    zero_f = jnp.zeros((2, n), F32)
    eidx = [jnp.sum(jnp.where(p, eid, 0.0), axis=0, keepdims=True) for p in picks]
    epos = [jnp.sum(jnp.where(p, pos, 0.0), axis=0, keepdims=True) for p in picks]
    ew = [s / wsum * ROUTED_SCALE for s in sel_scores]
    return (jnp.concatenate(eidx + [zero_f], axis=0).astype(jnp.int32),
            jnp.concatenate(epos + [zero_f], axis=0).astype(jnp.int32),
            jnp.concatenate(ew + [zero_f], axis=0),
            carry + jnp.sum(self32, axis=1, keepdims=True))


def _stage_e_kernel(xc_ref, xl_ref, mod_ref, g1_ref, g2_ref, fnc_ref, fnl_ref, omc_ref, oml_ref, osc_ref, osl_ref,
                    h_ref, wg_ref, bg_ref, wf_ref, wm_ref, ws_ref, wo_ref, rwt_ref, rb_ref, tri_ref,
                    x1_ref, h2_ref, eidx_ref, epos_ref, ew_ref, cnt_ref, carry_ref, h2b_ref):
    @pl.when(pl.program_id(0) == 0)
    def _():
        carry_ref[...] = jnp.zeros_like(carry_ref)

    is_ctx = pl.program_id(0) < NB_CTX
    carry = carry_ref[...]
    for rows in _chunks():
        fn = jnp.where(is_ctx, fnc_ref[rows, :], fnl_ref[rows, :])
        om = jnp.where(is_ctx, omc_ref[rows, :], oml_ref[rows, :])
        osw = jnp.where(is_ctx, osc_ref[rows, :], osl_ref[rows, :])
        h = h_ref[rows, :]

        def gate(c):
            lo, hi = c * D_MODEL, (c + 1) * D_MODEL
            return jax.nn.sigmoid(_dot(h, wg_ref[:, lo:hi]) + bg_ref[:, lo:hi])

        merged = (gate(0) * _dot(fn, wf_ref[...]) + gate(1) * _dot(om, wm_ref[...])
                  + gate(2) * _dot(osw, ws_ref[...]))
        mix = _dot(merged.astype(BF16), wo_ref[...])
        x = jnp.where(is_ctx, xc_ref[rows, :], xl_ref[rows, :])
        x1 = x + mod_ref[:, 2048:3072] * _rms_rows(mix, g1_ref[...])
        x1_ref[rows, :] = x1
        h2 = _rms_rows(x1, g2_ref[...]) * (1.0 + mod_ref[:, 4096:5120]) + mod_ref[:, 3072:4096]
        h2_ref[0, rows, :], h2_ref[1, rows, :] = _pack_pair(h2)
        h2b_ref[rows, :] = h2.astype(BF16)
    eidx_ref[...], epos_ref[...], ew_ref[...], carry = _route(h2b_ref[...], rwt_ref, rb_ref, tri_ref, carry)
    carry_ref[...] = carry
    cnt_ref[...] = jnp.broadcast_to(carry, cnt_ref.shape).astype(jnp.int32)


def _stage_e(layer, xc, xl, modt, g1, g2, mixed, h, w_gate, b_gate, wf, wm, ws, wo, rwt, rbias, tri):
    row = lambda w: pl.BlockSpec((TB, w), lambda i: (i, 0))
    ctx, lat = _ctx_rows(512), _lat_rows(512)
    col = lambda dt: (pl.BlockSpec((8, TB), lambda i: (0, i)), jax.ShapeDtypeStruct((8, N_TOK), dt))
    picks = [col(jnp.int32), col(jnp.int32), col(F32)]
    return pl.pallas_call(
        _stage_e_kernel,
        grid=(NB,),
        in_specs=[
            _ctx_rows(D_MODEL), _lat_rows(D_MODEL),
            pl.BlockSpec((None, 1, N_MOD * D_MODEL), lambda i: (i, 0, 0)),
            _const_spec((1, D_MODEL)), _const_spec((1, D_MODEL)),
            ctx, lat, ctx, lat, ctx, lat, row(D_MODEL),
            _layer_spec((D_MODEL, 3 * D_MODEL), layer), _const_spec((1, 3 * D_MODEL)),
            _layer_spec((512, D_MODEL), layer), _layer_spec((512, D_MODEL), layer),
            _layer_spec((512, D_MODEL), layer), _layer_spec((D_MODEL, D_MODEL), layer),
            _const_spec((N_EXPERTS, D_MODEL)), _const_spec((N_EXPERTS, 1)), _const_spec((TB, TB)),
        ],
        out_specs=[row(D_MODEL), pl.BlockSpec((2, TB, PACKED), lambda i: (0, i, 0))] + [s for s, _ in picks]
        + [_const_spec((N_EXPERTS, LANES))],
        out_shape=[jax.ShapeDtypeStruct((N_TOK, D_MODEL), F32),
                   jax.ShapeDtypeStruct((2, N_TOK, PACKED), jnp.int32)] + [s for _, s in picks]
        + [jax.ShapeDtypeStruct((N_EXPERTS, LANES), jnp.int32)],
        scratch_shapes=[pltpu.VMEM((N_EXPERTS, 1), F32), pltpu.VMEM((TB, D_MODEL), BF16)],
        compiler_params=_cparams(1),
        name="stage_e",
    )(xc, xl, modt, g1, g2, *mixed, h, w_gate, b_gate, wf, wm, ws, wo, rwt, rbias, tri)


def _expert_kernel(layer, te_ref, tv_ref, par_ref, nxt_ref, x_ref, w1_hbm, w3_hbm, w2_hbm, o_ref,
                   w1f_ref, w3f_ref, w2f_ref, w1b_ref, w3b_ref, w2b_ref, sem):
    j = pl.program_id(0)
    valid = tv_ref[j] > 0
    first = jnp.logical_and(valid, jnp.logical_or(j == 0, te_ref[j] != te_ref[jnp.maximum(j - 1, 0)]))

    def copies(expert, slot):
        return [pltpu.make_async_copy(w_hbm.at[layer, expert], buf.at[slot], sem.at[slot, i])
                for i, (w_hbm, buf) in enumerate(((w1_hbm, w1f_ref), (w3_hbm, w3f_ref), (w2_hbm, w2f_ref)))]

    @pl.when(jnp.logical_and(valid, j == 0))
    def _():
        for cp in copies(te_ref[0], par_ref[0]):
            cp.start()

    @pl.when(first)
    def _():
        slot = par_ref[j]
        for cp in copies(te_ref[j], slot):
            cp.wait()

        @pl.when(nxt_ref[j] >= 0)
        def _():
            for cp in copies(nxt_ref[j], 1 - slot):
                cp.start()

        w1b_ref[...] = w1f_ref[slot].astype(BF16)
        w3b_ref[...] = w3f_ref[slot].astype(BF16)
        w2b_ref[...] = w2f_ref[slot].astype(BF16)

    def run(n_chunks):
        for r in range(n_chunks):
            rows = pl.ds(r * EXPERT_ROWS, EXPERT_ROWS)
            x = _unpack_pair(x_ref[0, rows, :], x_ref[1, rows, :]).astype(BF16)
            hg = _dot(x, w1b_ref[...])
            hu = _dot(x, w3b_ref[...])
            act = (jax.nn.silu(hg) * hu).astype(BF16)
            o_ref[0, rows, :], o_ref[1, rows, :] = _pack_pair(_dot(act, w2b_ref[...]))

    for n_chunks in range(1, TE // EXPERT_ROWS + 1):
        pl.when(tv_ref[j] == n_chunks)(functools.partial(run, n_chunks))


def _experts(layer, tile_expert, tile_chunks, tile_slot, tile_next, xs, w1, w3, w2):
    slot_rows = pl.BlockSpec((2, TE, PACKED), lambda j, te, tv, par, nxt: (0, j, 0))
    anywhere = pl.BlockSpec(memory_space=pl.ANY)
    grid_spec = pltpu.PrefetchScalarGridSpec(
        num_scalar_prefetch=4,
        grid=(NTE,),
        in_specs=[slot_rows, anywhere, anywhere, anywhere],
        out_specs=slot_rows,
        scratch_shapes=[pltpu.VMEM((2, D_MODEL, EXPERT_FF), F32), pltpu.VMEM((2, D_MODEL, EXPERT_FF), F32),
                        pltpu.VMEM((2, EXPERT_FF, D_MODEL), F32),
                        pltpu.VMEM((D_MODEL, EXPERT_FF), BF16), pltpu.VMEM((D_MODEL, EXPERT_FF), BF16),
                        pltpu.VMEM((EXPERT_FF, D_MODEL), BF16),
                        pltpu.SemaphoreType.DMA((2, 3))],
    )
    return pl.pallas_call(
        functools.partial(_expert_kernel, layer),
        grid_spec=grid_spec,
        out_shape=jax.ShapeDtypeStruct((2, S_MAX, PACKED), jnp.int32),
        compiler_params=_cparams(1),
        name="experts",
    )(tile_expert, tile_chunks, tile_slot, tile_next, xs, w1, w3, w2)


def _sc_mesh():
    return plsc.VectorSubcoreMesh(core_axis_name="c", subcore_axis_name="s",
                                  num_cores=SC_CORES, num_subcores=SC_SUBCORES)


def _sc_scatter_rows(rows, slot8):
    @functools.partial(pl.kernel, mesh=_sc_mesh(), scratch_types=[pltpu.SemaphoreType.DMA],
                       out_type=jax.ShapeDtypeStruct((2, S_MAX, PACKED), jnp.int32))
    def scatter(x_hbm, i_hbm, o_hbm, sem):
        for h in range(2):
            dst = o_hbm.at[h]

            def body(x_vmem, i_vmem, dst=dst):
                copies = [pltpu.async_copy(x_vmem, dst.at[i_vmem.at[k]], sem) for k in range(TOP_K)]
                for cp in copies:
                    cp.wait()

            pltpu.emit_pipeline(
                body,
                grid=(N_TOK // SC_ROWS,),
                in_specs=[pl.BlockSpec((SC_ROWS, PACKED), lambda i: (i, 0)),
                          pl.BlockSpec((8, SC_ROWS), lambda i: (0, i))],
                out_specs=[],
                core_axis_name=("c", "s"),
                dimension_semantics=(pltpu.PARALLEL,),
            )(x_hbm.at[h], i_hbm)

    return scatter(rows, slot8)


def _sc_gather_rows(table, idx):
    n = idx.shape[1]

    @functools.partial(pl.kernel, mesh=_sc_mesh(), scratch_types=[],
                       out_type=jax.ShapeDtypeStruct((2, n, PACKED), jnp.int32))
    def gather(t_hbm, i_hbm, o_hbm):
        for h in range(2):
            src = t_hbm.at[h]

            def body(i_vmem, o_vmem, src=src):
                pltpu.sync_copy(src.at[i_vmem.at[0]], o_vmem)

            pltpu.emit_pipeline(
                body,
                grid=(n // SC_ROWS,),
                in_specs=[pl.BlockSpec((1, SC_ROWS), lambda i: (0, i))],
                out_specs=[pl.BlockSpec((SC_ROWS, PACKED), lambda i: (i, 0))],
                core_axis_name=("c", "s"),
                dimension_semantics=(pltpu.PARALLEL,),
            )(i_hbm, o_hbm.at[h])

    return gather(table, idx)


def _stage_g_kernel(x1_ref, mod_ref, g3_ref, yg_ref, ew_ref, h2_ref,
                    s1_ref, s3_ref, s2_ref, oc_ref, ol_ref):
    is_ctx = pl.program_id(0) < NB_CTX
    for rows in _chunks():
        h = _unpack_pair(h2_ref[0, rows, :], h2_ref[1, rows, :]).astype(BF16)
        act = jax.nn.silu(_dot(h, s1_ref[...])) * _dot(h, s3_ref[...])
        y = _dot(act.astype(BF16), s2_ref[...])
        for k in range(TOP_K):
            y = y + ew_ref[rows, k:k + 1] * _unpack_pair(yg_ref[0, k, rows, :], yg_ref[1, k, rows, :])
        out = x1_ref[rows, :] + mod_ref[:, 5120:6144] * _rms_rows(y, g3_ref[...])

        @pl.when(is_ctx)
        def _():
            oc_ref[rows, :] = out

        @pl.when(jnp.logical_not(is_ctx))
        def _():
            ol_ref[rows, :] = out


def _stage_g(layer, x1, modt, g3, yg, ew_rows, h2, s1, s3, s2):
    row = lambda w: pl.BlockSpec((TB, w), lambda i: (i, 0))
    picked = pl.BlockSpec((2, TOP_K, TB, PACKED), lambda i: (0, 0, i, 0))
    return pl.pallas_call(
        _stage_g_kernel,
        grid=(NB,),
        in_specs=[row(D_MODEL), pl.BlockSpec((None, 1, N_MOD * D_MODEL), lambda i: (i, 0, 0)),
                  _const_spec((1, D_MODEL)), picked, row(8),
                  pl.BlockSpec((2, TB, PACKED), lambda i: (0, i, 0)),
                  _layer_spec((D_MODEL, SHARED_FF), layer), _layer_spec((D_MODEL, SHARED_FF), layer),
                  _layer_spec((SHARED_FF, D_MODEL), layer)],
        out_specs=[_ctx_rows(D_MODEL), _lat_rows(D_MODEL)],
        out_shape=[jax.ShapeDtypeStruct((N_CTX, D_MODEL), F32),
                   jax.ShapeDtypeStruct((N_LAT, D_MODEL), F32)],
        compiler_params=_cparams(1),
        name="stage_g",
    )(x1, modt, g3, yg, ew_rows, h2, s1, s3, s2)


def _rope_tables():
    t = np.arange(DEC_SEQ)
    pos = np.stack([(t // GRID_W), (t % GRID_W)], axis=-1).astype(np.float32)

    def table(r):
        n_freq = r // 4
        inv = np.float32(ROPE_BASE) ** (-np.arange(n_freq, dtype=np.float32) / np.float32(n_freq))
        ang = pos[:, :, None] * inv.astype(np.float32)
        cos = np.cos(ang)
        sin = np.sin(ang)
        cos_t = np.stack([cos, cos], axis=2).reshape(DEC_SEQ, r)
        sin_t = np.stack([-sin, sin], axis=2).reshape(DEC_SEQ, r)
        return cos_t, sin_t

    c64, s64 = table(SWA_HEAD_DIM)
    c32, s32 = table(MLA_ROPE)
    lat = np.concatenate([np.tile(c64, (1, 8)), np.tile(s64, (1, 8)),
                          np.tile(c32, (1, 4)), np.tile(s32, (1, 4))], axis=1)
    ident = np.concatenate([np.ones((TB, 512)), np.zeros((TB, 512)),
                            np.ones((TB, 128)), np.zeros((TB, 128))], axis=1)
    return jnp.asarray(np.concatenate([ident, lat], axis=0).astype(np.float32))


def _dft_pair(n):
    k = np.arange(n, dtype=np.int64)
    ang = ((k[:, None] * k[None, :]) % n).astype(np.float64) * (2.0 * math.pi / n)
    return np.cos(ang), np.sin(ang)


def _fnet_tables():
    c64, s64 = _dft_pair(FNET_GROUP_DIM)
    eye = np.eye(FNET_GROUPS)
    bd = np.concatenate([np.kron(eye, c64), np.kron(eye, s64)], axis=1)
    mats = []
    for t_len in (SEQ, DEC_SEQ):
        c, s = _dft_pair(t_len)
        mats.append(np.concatenate([c, -s], axis=1))
    return tuple(jnp.asarray(m.astype(np.float32).astype(BF16)) for m in (bd, mats[0], mats[1]))


def _layer_weights(l, w_in, w_uq, w_ukv):
    w = w_in[l]
    wide = jnp.concatenate([w[:, 0:1024], w[:, 1056:1824], w[:, 1024:1056],
                            jnp.zeros((D_MODEL, 96), F32)], axis=1).astype(BF16)

    uq = w_uq[l].reshape(MLA_Q_RANK, MLA_HEADS, MLA_NOPE + MLA_ROPE)
    z32 = jnp.zeros((MLA_Q_RANK, MLA_HEADS, 32), F32)
    wqa = jnp.concatenate([uq, z32], axis=2).reshape(MLA_Q_RANK, 1024).astype(BF16)
    ukv = w_ukv[l].reshape(MLA_KV_RANK, MLA_HEADS, MLA_NOPE + MLA_V)
    wk = jnp.concatenate([ukv[:, :, :MLA_NOPE], jnp.zeros((MLA_KV_RANK, MLA_HEADS, 64), F32)],
                         axis=2).reshape(MLA_KV_RANK, 1024).astype(BF16)
    wv = ukv[:, :, MLA_NOPE:].reshape(MLA_KV_RANK, 512).astype(BF16)
    return wide, wqa, wk, wv


def _rope_placement():
    e = np.zeros((128, 1024), np.float32)
    for hd in range(MLA_HEADS):
        for i in range(MLA_ROPE):
            e[i, hd * 128 + MLA_NOPE + i] = 1.0
    return jnp.asarray(e, BF16)


def _moe_dispatch_plan(eidx, epos, counts):
    padded = ((counts + TE - 1) // TE) * TE
    ends = jnp.cumsum(padded)
    offs = ends - padded
    ids = jnp.arange(N_EXPERTS, dtype=jnp.int32)
    picked_off = jnp.sum(jnp.where(eidx[:, :, None] == ids, offs, 0), axis=-1)
    slot = picked_off + epos
    starts = jnp.arange(NTE, dtype=jnp.int32) * TE
    tile_expert = jnp.sum((ends[None, :] <= starts[:, None]).astype(jnp.int32), axis=1)
    tile_expert = jnp.minimum(tile_expert, N_EXPERTS - 1)
    pick = tile_expert[:, None] == ids[None, :]
    last_real = jnp.sum(jnp.where(pick, (offs + counts)[None, :], 0), axis=1)
    n_real = jnp.clip(last_real - starts, 0, TE)
    n_real = jnp.where(starts < ends[-1], n_real, 0)
    tile_chunks = ((n_real + EXPERT_ROWS - 1) // EXPERT_ROWS).astype(jnp.int32)
    used = counts > 0
    rank = jnp.cumsum(used.astype(jnp.int32)) - 1
    tile_slot = jnp.sum(jnp.where(pick, rank[None, :], 0), axis=1) % 2
    later_used = jnp.logical_and(ids[None, :] > ids[:, None], used[None, :])
    next_used = jnp.min(jnp.where(later_used, ids[None, :], N_EXPERTS), axis=1)
    next_used = jnp.where(next_used < N_EXPERTS, next_used, -1)
    tile_next = jnp.sum(jnp.where(pick, next_used[None, :], 0), axis=1)
    return slot, tile_expert, tile_chunks, tile_slot.astype(jnp.int32), tile_next.astype(jnp.int32)


def kernel(x_prompt, x_sample, cache_mla_ckv, cache_mla_krope, cache_swa_k, cache_swa_v, c, c_ctx,
           ada_w, ada_b, norm_g, w_in, q_norm, kv_norm, w_fnet, w_uq, w_ukv, w_mla_o, swa_sink,
           w_swa_o, w_gate, b_gate, w_out, router_w, router_bias, exp_w1, exp_w3, exp_w2,
           shared_w1, shared_w3, shared_w2):
    xc = x_prompt.reshape(N_CTX, D_MODEL)
    xl = x_sample.reshape(N_LAT, D_MODEL)

    cond8 = jnp.concatenate([c_ctx[None, :], c, jnp.zeros((3, D_MODEL), F32)], axis=0)
    mod = _modulation(cond8, ada_w, ada_b)
    tile_cond = np.concatenate([np.zeros(NB_CTX, np.int32),
                                1 + np.arange(NB - NB_CTX, dtype=np.int32) // LAT_BLOCKS])

    tab = _rope_tables()
    bd, f_ctx, f_lat = _fnet_tables()
    e_mat = _rope_placement()
    tri = jnp.asarray(np.triu(np.ones((TB, TB), np.float32), 1), BF16)
    w_gate_b, w_fnet_b, w_mla_o_b, w_swa_o_b, w_out_b, sw1_b, sw3_b, sw2_b = (
        w.astype(BF16) for w in (w_gate, w_fnet, w_mla_o, w_swa_o, w_out, shared_w1, shared_w3, shared_w2))

    new_ckv, new_kr, new_k, new_v = [], [], [], []
    for l in range(DEPTH):
        modt = mod[l][tile_cond][:, None, :]
        wide, wqa, wk, wv = _layer_weights(l, w_in, w_uq, w_ukv)
        ng = norm_g[l]

        fin, ckv, kr, sq, sk, sv, h1, q_m, k_m, v_m = _stage_a(
            xc, xl, modt, ng[0:1], wide, q_norm[l][None, :], kv_norm[l][None, :], tab, wqa, wk, e_mat, wv)

        new_ckv.append(ckv.reshape(BATCH, SEQ, MLA_KV_RANK))
        new_kr.append(kr[:, :MLA_ROPE].reshape(BATCH, SEQ, MLA_ROPE))
        new_k.append(sk[:N_CTX].reshape(BATCH, SEQ, SWA_KV_HEADS, SWA_HEAD_DIM))
        new_v.append(sv[:N_CTX].reshape(BATCH, SEQ, SWA_KV_HEADS, SWA_HEAD_DIM))

        fn = (_fnet(fin, f_ctx, bd, BATCH, SEQ, 0),
              _fnet(fin, f_lat, bd, DEC_BATCH, DEC_SEQ, N_CTX // DEC_SEQ))

        kr_cache = jnp.pad(cache_mla_krope[:, l].reshape(N_CACHE, MLA_ROPE), ((0, 0), (0, 96)))
        k_c, v_c = _mla_cache_kv(cache_mla_ckv[:, l].reshape(N_CACHE, MLA_KV_RANK), kr_cache,
                                 wk, e_mat, wv)
        om = (_mla_attn(q_m, k_m, v_m, k_c, v_c, latent=False),
              _mla_attn(q_m, k_m, v_m, k_c, v_c, latent=True))

        ck = cache_swa_k[:, l].reshape(DEC_BATCH, PAST_LEN, 128)
        cv = cache_swa_v[:, l].reshape(DEC_BATCH, PAST_LEN, 128)
        osw = (_swa_attn(swa_sink[l], sq, sk, sv, ck, cv, latent=False),
               _swa_attn(swa_sink[l], sq, sk, sv, ck, cv, latent=True))

        x1, h2, eidx, epos, ew, counts = _stage_e(
            l, xc, xl, modt, ng[1:2], ng[2:3], fn + om + osw, h1, w_gate_b, b_gate[l][None, :],
            w_fnet_b, w_mla_o_b, w_swa_o_b, w_out_b,
            router_w[l].T.astype(BF16), router_bias[l][:, None], tri)
        slot, tile_expert, tile_chunks, tile_slot, tile_next = _moe_dispatch_plan(eidx, epos, counts[:, 0])
        xs = _sc_scatter_rows(h2, slot)
        ys = _experts(l, tile_expert, tile_chunks, tile_slot, tile_next, xs, exp_w1, exp_w3, exp_w2)
        picks = slot[:TOP_K].reshape(1, TOP_K * N_TOK)
        yg = _sc_gather_rows(ys, picks).reshape(2, TOP_K, N_TOK, PACKED)
        xc, xl = _stage_g(l, x1, modt, ng[3:4], yg, ew.T, h2, sw1_b, sw3_b, sw2_b)

    y_p = xc.reshape(BATCH, SEQ, D_MODEL)
    y_s = xl.reshape(DEC_BATCH, DEC_SEQ, D_MODEL)
    return (y_p, y_s, jnp.stack(new_ckv, axis=1), jnp.stack(new_kr, axis=1),
            jnp.stack(new_k, axis=1), jnp.stack(new_v, axis=1))
```

```python
import functools
import math

import numpy as np
import jax
import jax.numpy as jnp
from jax import lax
from jax.experimental import pallas as pl
from jax.experimental.pallas import tpu as pltpu
from jax.experimental.pallas import tpu_sc as plsc

D_MODEL = 1024
BATCH = 16
SEQ = 256
DEPTH = 2
DEC_BATCH = 4
DEC_SEQ = 2048
PAST_LEN = 512
GRID_W = 64
EPS = 1e-6
ROPE_BASE = 10000.0
NEG_INF = -1e30

FNET_GROUPS = 8
FNET_GROUP_DIM = 64
FNET_WIDTH = 512
MLA_HEADS = 8
MLA_Q_RANK = 384
MLA_KV_RANK = 128
MLA_NOPE = 64
MLA_ROPE = 32
MLA_V = 64
MLA_SCALE = (MLA_NOPE + MLA_ROPE) ** -0.5
LOG2E = math.log2(math.e)
SWA_KV_HEADS = 2
SWA_HEAD_DIM = 64
SWA_WINDOW = 128
SWA_SCALE = SWA_HEAD_DIM ** -0.5
N_MOD = 6
N_EXPERTS = 64
N_EXPERT_GROUPS = 8
TOPK_GROUPS = 4
TOP_K = 6
EXPERT_FF = 256
SHARED_FF = 256
ROUTED_SCALE = 2.5

LANES = 128
TM = 256
N_CTX = BATCH * SEQ
N_LAT = DEC_BATCH * DEC_SEQ
N_TOK = N_CTX + N_LAT
N_CACHE = DEC_BATCH * PAST_LEN
TB = 512
NB = N_TOK // TB
NB_CTX = N_CTX // TB
LAT_BLOCKS = DEC_SEQ // TB
MLA_LAT_TQ = 256
MLA_LAT_PAIRS = 4
FNET_ROWS = 1024
TE = 512
S_MAX = N_TOK * TOP_K + N_EXPERTS * TE
NTE = S_MAX // TE
EXPERT_ROWS = 256
VMEM_LIMIT = 56 * 1024 * 1024
PACKED = D_MODEL // 4
SC_ROWS = 128
SC_CORES = 2
SC_SUBCORES = 16

A_F = (0, 512)
A_QD = (512, 896)
A_KV = (896, 1024)
A_SQ = (1024, 1536)
A_SK = (1536, 1664)
A_SV = (1664, 1792)
A_KR = (1792, 1920)
W_IN_WIDE = 1920
TAB_W = 1280

F32 = jnp.float32
BF16 = jnp.bfloat16


def _cparams(n_axes):
    return pltpu.CompilerParams(dimension_semantics=("arbitrary",) * n_axes, vmem_limit_bytes=VMEM_LIMIT)


def _dot(a, b):
    return jnp.dot(a, b, preferred_element_type=F32)


def _dot_nt(a, b):
    return lax.dot_general(a, b, (((1,), (1,)), ((), ())), preferred_element_type=F32)


def _rms_rows(v, g):
    return v * lax.rsqrt(jnp.mean(v * v, axis=-1, keepdims=True) + EPS) * g


def _pack_rows(v):
    half = v.shape[1] // 2
    lo = lax.bitcast_convert_type(v[:, :half].astype(BF16).astype(F32), jnp.int32)
    hi = lax.bitcast_convert_type(v[:, half:].astype(BF16).astype(F32), jnp.int32)
    return jnp.bitwise_or(jnp.bitwise_and(hi, -65536), jnp.bitwise_and(jnp.right_shift(lo, 16), 65535))


def _unpack_rows(w):
    lo = lax.bitcast_convert_type(jnp.left_shift(w, 16), F32)
    hi = lax.bitcast_convert_type(jnp.bitwise_and(w, -65536), F32)
    return jnp.concatenate([lo, hi], axis=1)


def _pack_pair(v):
    half = v.shape[1] // 2
    return _pack_rows(v[:, :half]), _pack_rows(v[:, half:])


def _unpack_pair(a, b):
    return jnp.concatenate([_unpack_rows(a), _unpack_rows(b)], axis=1)


def _const_spec(shape):
    return pl.BlockSpec(shape, lambda *_: (0,) * len(shape))


def _layer_spec(shape, layer):
    return pl.BlockSpec((None,) + shape, lambda *_: (layer,) + (0,) * len(shape))


def _ctx_rows(width):
    return pl.BlockSpec((TB, width), lambda i: (jnp.minimum(i, NB_CTX - 1), 0))


def _lat_rows(width):
    return pl.BlockSpec((TB, width), lambda i: (jnp.maximum(i - NB_CTX, 0), 0))


def _tab_row_block(i):
    return jnp.where(i < NB_CTX, 0, 1 + (i - NB_CTX) % LAT_BLOCKS)


def _mod_kernel(cond_ref, w_ref, b_ref, o_ref):
    c = cond_ref[...]
    a = (c * jax.nn.sigmoid(c)).astype(BF16)
    o_ref[...] = _dot(a, w_ref[...].astype(BF16)) + b_ref[...]


def _modulation(cond8, ada_w, ada_b):
    tn = 512
    nj = N_MOD * D_MODEL // tn
    return pl.pallas_call(
        _mod_kernel,
        grid=(DEPTH, nj),
        in_specs=[
            pl.BlockSpec((8, D_MODEL), lambda l, j: (0, 0)),
            pl.BlockSpec((None, D_MODEL, tn), lambda l, j: (l, 0, j)),
            pl.BlockSpec((None, 1, tn), lambda l, j: (l, 0, j)),
        ],
        out_specs=pl.BlockSpec((None, 8, tn), lambda l, j: (l, 0, j)),
        out_shape=jax.ShapeDtypeStruct((DEPTH, 8, N_MOD * D_MODEL), F32),
        compiler_params=_cparams(2),
        name="modulation",
    )(cond8, ada_w, ada_b.reshape(DEPTH, 1, N_MOD * D_MODEL))


def _half_swap(x, half):
    n = x.shape[1]
    lane = lax.broadcasted_iota(jnp.int32, (1, n), 1)
    return jnp.where((lane & half) == 0, pltpu.roll(x, n - half, 1), pltpu.roll(x, half, 1))


def _mla_expand(rows, cq, ckv, kr, cos32, sin32, wqa_ref, wk_ref, e_ref, wv_ref,
                q_ref, k_ref, v_ref):
    if q_ref is not None:
        lane = lax.broadcasted_iota(jnp.int32, (1, LANES), 1)
        rope_lane = jnp.logical_and(lane >= MLA_NOPE, lane < MLA_NOPE + MLA_ROPE)
        cos_h = jnp.where(rope_lane, cos32, 1.0)
        sin_h = jnp.where(rope_lane, sin32, 0.0)
        for hd in range(MLA_HEADS):
            lo, hi = hd * LANES, (hd + 1) * LANES
            q = _dot(cq, wqa_ref[:, lo:hi])
            q = q * cos_h + _half_swap(q, MLA_ROPE // 4) * sin_h
            q_ref[rows, lo:hi] = (q * (MLA_SCALE * LOG2E)).astype(BF16)
    k_ref[rows, :] = (_dot(ckv, wk_ref[...]) + _dot(kr, e_ref[...])).astype(BF16)
    v_ref[rows, :] = _dot(ckv, wv_ref[...]).astype(BF16)


def _chunks():
    return [pl.ds(r * TM, TM) for r in range(TB // TM)]


def _stage_a_kernel(xc_ref, xl_ref, mod_ref, g_ref, win_ref, qn_ref, kvn_ref, tab_ref,
                    wqa_ref, wk_ref, e_ref, wv_ref,
                    fin_ref, ckv_ref, kr_ref, sq_ref, sk_ref, sv_ref, h_ref,
                    qm_ref, km_ref, vm_ref):
    is_ctx = pl.program_id(0) < NB_CTX
    for rows in _chunks():
        x = jnp.where(is_ctx, xc_ref[rows, :], xl_ref[rows, :])
        h = (_rms_rows(x, g_ref[...]) * (1.0 + mod_ref[:, 1024:2048]) + mod_ref[:, 0:1024]).astype(BF16)
        h_ref[rows, :] = h

        def proj(seg):
            return _dot(h, win_ref[:, seg[0]:seg[1]])

        fin_ref[rows, :] = proj(A_F).astype(BF16)
        cq = _rms_rows(proj(A_QD), qn_ref[...]).astype(BF16)
        ckv = _rms_rows(proj(A_KV), kvn_ref[...])
        ckv_ref[rows, :] = ckv
        cos64 = tab_ref[rows, 0:512]
        sin64 = tab_ref[rows, 512:1024]
        sq = proj(A_SQ)
        sq = sq * cos64 + _half_swap(sq, SWA_HEAD_DIM // 4) * sin64
        sq_ref[rows, :] = (sq * (SWA_SCALE * LOG2E)).astype(BF16)
        sk = proj(A_SK)
        sk_ref[rows, :] = sk * cos64[:, 0:128] + _half_swap(sk, SWA_HEAD_DIM // 4) * sin64[:, 0:128]
        sv_ref[rows, :] = proj(A_SV)
        cos32 = tab_ref[rows, 1024:1152]
        sin32 = tab_ref[rows, 1152:1280]
        kr = proj(A_KR)
        kr = kr * cos32 + _half_swap(kr, MLA_ROPE // 4) * sin32
        kr_ref[rows, :] = kr
        _mla_expand(rows, cq, ckv.astype(BF16), kr.astype(BF16), cos32, sin32,
                    wqa_ref, wk_ref, e_ref, wv_ref, qm_ref, km_ref, vm_ref)


def _stage_a(xc, xl, modt, g0, w_in_wide, q_norm, kv_norm, tab, wqa, wk, e_mat, wv):
    row = lambda w: pl.BlockSpec((TB, w), lambda i: (i, 0))
    every = lambda w, dt: (row(w), jax.ShapeDtypeStruct((N_TOK, w), dt))
    outs = [every(512, BF16), every(128, F32), every(128, F32), every(512, BF16), every(128, F32),
            every(128, F32), every(D_MODEL, BF16), every(1024, BF16), every(1024, BF16), every(512, BF16)]
    return pl.pallas_call(
        _stage_a_kernel,
        grid=(NB,),
        in_specs=[
            _ctx_rows(D_MODEL), _lat_rows(D_MODEL),
            pl.BlockSpec((None, 1, N_MOD * D_MODEL), lambda i: (i, 0, 0)),
            _const_spec((1, D_MODEL)),
            _const_spec((D_MODEL, W_IN_WIDE)),
            _const_spec((1, MLA_Q_RANK)),
            _const_spec((1, MLA_KV_RANK)),
            pl.BlockSpec((TB, TAB_W), lambda i: (_tab_row_block(i), 0)),
            _const_spec((MLA_Q_RANK, 1024)),
            _const_spec((128, 1024)), _const_spec((128, 1024)), _const_spec((128, 512)),
        ],
        out_specs=[s for s, _ in outs],
        out_shape=[s for _, s in outs],
        compiler_params=_cparams(1),
        name="stage_a",
    )(xc, xl, modt, g0, w_in_wide, q_norm, kv_norm, tab, wqa, wk, e_mat, wv)


def _fnet_kernel(t_len, scale, fin_ref, f_ref, bd_ref, o_ref, zz_ref):
    @pl.when(pl.program_id(1) == 0)
    def _():
        z = fin_ref[...]
        zz_ref[0:t_len, :] = _dot(z, bd_ref[:, 0:512]).astype(BF16)
        zz_ref[t_len:2 * t_len, :] = _dot(z, bd_ref[:, 512:1024]).astype(BF16)

    o_ref[...] = (_dot(f_ref[...], zz_ref[...]) * scale).astype(BF16)


def _fnet(fin, fmat, bd, n_batch, t_len, row_block0):
    scale = 1.0 / math.sqrt(t_len * FNET_GROUP_DIM)
    ft = min(t_len, FNET_ROWS)
    return pl.pallas_call(
        functools.partial(_fnet_kernel, t_len, scale),
        grid=(n_batch, t_len // ft),
        in_specs=[
            pl.BlockSpec((t_len, FNET_WIDTH), lambda b, i: (row_block0 + b, 0)),
            pl.BlockSpec((ft, 2 * t_len), lambda b, i: (i, 0)),
            _const_spec((FNET_WIDTH, 2 * FNET_WIDTH)),
        ],
        out_specs=pl.BlockSpec((ft, FNET_WIDTH), lambda b, i: (b * (t_len // ft) + i, 0)),
        out_shape=jax.ShapeDtypeStruct((n_batch * t_len, FNET_WIDTH), BF16),
        scratch_shapes=[pltpu.VMEM((2 * t_len, FNET_WIDTH), BF16)],
        compiler_params=_cparams(2),
        name=f"fnet_{t_len}",
    )(fin, fmat, bd)


def _mla_cache_kernel(ckv_ref, kr_ref, wk_ref, e_ref, wv_ref, k_ref, v_ref):
    _mla_expand(slice(None), None, ckv_ref[...].astype(BF16), kr_ref[...].astype(BF16), None, None,
                None, wk_ref, e_ref, wv_ref, None, k_ref, v_ref)


def _mla_cache_kv(ckv_cache, kr_cache, wk, e_mat, wv):
    row = lambda w: pl.BlockSpec((TM, w), lambda i: (i, 0))
    return pl.pallas_call(
        _mla_cache_kernel,
        grid=(N_CACHE // TM,),
        in_specs=[row(128), row(128),
                  _const_spec((128, 1024)), _const_spec((128, 1024)), _const_spec((128, 512))],
        out_specs=[row(1024), row(512)],
        out_shape=[jax.ShapeDtypeStruct((N_CACHE, 1024), BF16),
                   jax.ShapeDtypeStruct((N_CACHE, 512), BF16)],
        compiler_params=_cparams(1),
        name="mla_cache_kv",
    )(ckv_cache, kr_cache, wk, e_mat, wv)


def _mla_attn_kernel(n_seg, pairs, q_ref, *refs):
    k_refs = refs[0:n_seg]
    v_refs = refs[n_seg:2 * n_seg]
    o_ref = refs[2 * n_seg]
    lane = lax.broadcasted_iota(jnp.int32, (1, LANES), 1)
    low = lane < MLA_V
    for pr in range(pairs):
        outs = []
        for hh in range(2):
            hd = 2 * pr + hh
            q = q_ref[:, hd * LANES:(hd + 1) * LANES]
            ss = [_dot_nt(q, k[:, hd * LANES:(hd + 1) * LANES]) for k in k_refs]
            m = functools.reduce(jnp.maximum, [s.max(axis=-1, keepdims=True) for s in ss])
            keep = low if hh == 0 else jnp.logical_not(low)
            sum_lane = MLA_V if hh == 0 else 0
            po = None
            for s, v_ref in zip(ss, v_refs):
                v = v_ref[:, pr * LANES:(pr + 1) * LANES]
                vm = jnp.where(lane == sum_lane, jnp.ones_like(v), jnp.where(keep, v, jnp.zeros_like(v)))
                t = _dot(jnp.exp2(s - m).astype(BF16), vm)
                po = t if po is None else po + t
            outs.append(po / po[:, sum_lane:sum_lane + 1])
        o_ref[:, pr * LANES:(pr + 1) * LANES] = jnp.where(low, outs[0], outs[1]).astype(BF16)


def _mla_attn(q_all, k_all, v_all, k_cache, v_cache, latent):
    if latent:
        tq, pairs = MLA_LAT_TQ, MLA_LAT_PAIRS
        n_b, n_q = DEC_BATCH, DEC_SEQ // tq
        q0 = N_CTX // tq
        kv_specs = [
            pl.BlockSpec((PAST_LEN, 256 * pairs), lambda b, hp, i: (b, hp)),
            pl.BlockSpec((DEC_SEQ, 256 * pairs), lambda b, hp, i: (N_CTX // DEC_SEQ + b, hp)),
            pl.BlockSpec((PAST_LEN, 128 * pairs), lambda b, hp, i: (b, hp)),
            pl.BlockSpec((DEC_SEQ, 128 * pairs), lambda b, hp, i: (N_CTX // DEC_SEQ + b, hp)),
        ]
        args = (q_all, k_cache, k_all, v_cache, v_all)
        n_seg = 2
    else:
        tq, pairs = SEQ, MLA_HEADS // 2
        n_b, n_q = BATCH, 1
        q0 = 0
        kv_specs = [
            pl.BlockSpec((SEQ, 256 * pairs), lambda b, hp, i: (b, hp)),
            pl.BlockSpec((SEQ, 128 * pairs), lambda b, hp, i: (b, hp)),
        ]
        args = (q_all, k_all, v_all)
        n_seg = 1
    return pl.pallas_call(
        functools.partial(_mla_attn_kernel, n_seg, pairs),
        grid=(n_b, MLA_HEADS // (2 * pairs), n_q),
        in_specs=[pl.BlockSpec((tq, 256 * pairs), lambda b, hp, i: (q0 + b * n_q + i, hp))] + kv_specs,
        out_specs=pl.BlockSpec((tq, 128 * pairs), lambda b, hp, i: (b * n_q + i, hp)),
        out_shape=jax.ShapeDtypeStruct((n_b * n_q * tq, MLA_HEADS * MLA_V), BF16),
        compiler_params=_cparams(3),
        name="mla_attn_lat" if latent else "mla_attn_ctx",
    )(*args)


def _swa_kernel(windowed, n_steps, sink_ref, q_ref, *refs):
    n_seg = 4 if windowed else 1
    k_refs = refs[0:n_seg]
    v_refs = refs[n_seg:2 * n_seg]
    o_ref = refs[2 * n_seg]
    step = pl.program_id(1)
    lane = lax.broadcasted_iota(jnp.int32, (1, LANES), 1)
    low = lane < SWA_HEAD_DIM
    high = jnp.logical_not(low)

    k_all = jnp.concatenate([r[...] for r in k_refs], axis=0)
    v_all = jnp.concatenate([r[...] for r in v_refs], axis=0)
    k_sw = pltpu.roll(k_all, SWA_HEAD_DIM, 1)
    v_sw = pltpu.roll(v_all, SWA_HEAD_DIM, 1)

    if windowed:
        tq = SWA_WINDOW
        qi = lax.broadcasted_iota(jnp.int32, (2 * tq, tq), 0) % tq
        kj = lax.broadcasted_iota(jnp.int32, (2 * tq, tq), 1)
        after = kj >= qi
        before = kj <= qi
        biases = [(jnp.where(jnp.logical_and(after, step > 0), 0.0, NEG_INF),
                   jnp.where(before, 0.0, NEG_INF)),
                  (jnp.where(after, 0.0, NEG_INF),
                   jnp.where(jnp.logical_and(before, step < n_steps - 1), 0.0, NEG_INF))]
    else:
        tq = q_ref.shape[0]
    n_sub = q_ref.shape[0] // tq
    top_rows = lax.broadcasted_iota(jnp.int32, (2 * tq, 1), 0) < tq

    for g in range(SWA_KV_HEADS):
        kh, vh = [], []
        for half in range(2):
            keep = low if half == 0 else high
            sum_lane = SWA_HEAD_DIM if half == 0 else 0
            straight = (g == half)
            kh.append(jnp.where(keep, k_all if straight else k_sw, 0.0).astype(BF16))
            vh.append(jnp.where(lane == sum_lane, 1.0,
                                jnp.where(keep, v_all if straight else v_sw, 0.0)).astype(BF16))
        for sub in range(n_sub):
            rows = slice(sub * tq, (sub + 1) * tq)
            qs = jnp.concatenate([q_ref[rows, 256 * g:256 * g + 128],
                                  q_ref[rows, 256 * g + 128:256 * g + 256]], axis=0)
            halves = []
            for half in range(2):
                sum_lane = SWA_HEAD_DIM if half == 0 else 0
                ks, vs = kh[half], vh[half]
                if windowed:
                    w0 = PAST_LEN + sub * tq
                    if sub == 0:
                        ks, vs = ks[0:w0 + 3 * tq], vs[0:w0 + 3 * tq]
                    else:
                        ks = jnp.concatenate([ks[0:PAST_LEN], ks[w0:w0 + 3 * tq]], axis=0)
                        vs = jnp.concatenate([vs[0:PAST_LEN], vs[w0:w0 + 3 * tq]], axis=0)
                s = _dot_nt(qs, ks)
                if windowed:
                    c0, c1, c2 = PAST_LEN, PAST_LEN + tq, PAST_LEN + 2 * tq
                    s = jnp.concatenate([s[:, :c0], s[:, c0:c1] + biases[sub][0], s[:, c1:c2],
                                         s[:, c2:] + biases[sub][1]], axis=1)
                sink = jnp.where(top_rows, sink_ref[4 * g + half], sink_ref[4 * g + 2 + half]) * LOG2E
                m = jnp.maximum(s.max(axis=-1, keepdims=True), sink)
                po = _dot(jnp.exp2(s - m).astype(BF16), vs)
                halves.append(po / (po[:, sum_lane:sum_lane + 1] + jnp.exp2(sink - m)))
            out = jnp.where(low, halves[0], halves[1])
            o_ref[rows, 256 * g:256 * g + 128] = out[0:tq].astype(BF16)
            o_ref[rows, 256 * g + 128:256 * g + 256] = out[tq:2 * tq].astype(BF16)


def _swa_attn(sink, sq, sk, sv, cache_k, cache_v, latent):
    smem = pl.BlockSpec(memory_space=pltpu.SMEM)
    if latent:
        tq = 2 * SWA_WINDOW
        n_b, n_qb = DEC_BATCH, DEC_SEQ // tq
        base = N_CTX // tq
        last = DEC_SEQ // SWA_WINDOW - 1

        def prev(b, i):
            return (2 * (base + b * n_qb) + jnp.maximum(2 * i - 1, 0), 0)

        def cur(b, i):
            return (base + b * n_qb + i, 0)

        def nxt(b, i):
            return (2 * (base + b * n_qb) + jnp.minimum(2 * i + 2, last), 0)

        cache = pl.BlockSpec((None, PAST_LEN, 128), lambda b, i: (b, 0, 0))
        edge = lambda f: pl.BlockSpec((SWA_WINDOW, 128), f)
        kv_specs = [cache, edge(prev), pl.BlockSpec((tq, 128), cur), edge(nxt)] * 2
        args = (cache_k, sk, sk, sk, cache_v, sv, sv, sv)
        q_spec = pl.BlockSpec((tq, 512), cur)
        o_spec = pl.BlockSpec((tq, 512), lambda b, i: (b * n_qb + i, 0))
    else:
        tq = SEQ
        n_b, n_qb = BATCH, 1
        blk = pl.BlockSpec((tq, 128), lambda b, i: (b, 0))
        kv_specs = [blk, blk]
        args = (sk, sv)
        q_spec = pl.BlockSpec((tq, 512), lambda b, i: (b, 0))
        o_spec = q_spec
    return pl.pallas_call(
        functools.partial(_swa_kernel, latent, n_qb),
        grid=(n_b, n_qb),
        in_specs=[smem, q_spec] + kv_specs,
        out_specs=o_spec,
        out_shape=jax.ShapeDtypeStruct((n_b * n_qb * tq, 512), BF16),
        compiler_params=_cparams(2),
        name="swa_lat" if latent else "swa_ctx",
    )(sink, sq, *args)


def _route(h, rwt_ref, rb_ref, tri_ref, carry):
    n = h.shape[0]
    gsz = N_EXPERTS // N_EXPERT_GROUPS
    scores = jax.nn.sigmoid(_dot_nt(rwt_ref[...], h))
    biased = scores + rb_ref[...]
    mem = lax.broadcasted_iota(jnp.int32, (gsz, n), 0).astype(F32)
    gs_rows = []
    for g in range(N_EXPERT_GROUPS):
        bg = biased[g * gsz:(g + 1) * gsz, :]
        m1 = bg.max(axis=0, keepdims=True)
        first = jnp.min(jnp.where(bg == m1, mem, float(gsz)), axis=0, keepdims=True)
        m2 = jnp.where(mem == first, -jnp.inf, bg).max(axis=0, keepdims=True)
        gs_rows.append(m1 + m2)
    gs = jnp.concatenate(gs_rows, axis=0)
    gid = lax.broadcasted_iota(jnp.int32, gs.shape, 0).astype(F32)
    gsel = jnp.zeros(gs.shape, F32)
    for _ in range(TOPK_GROUPS):
        mx = gs.max(axis=0, keepdims=True)
        pick = gid == jnp.min(jnp.where(gs == mx, gid, float(N_EXPERT_GROUPS)), axis=0, keepdims=True)
        gsel = jnp.where(pick, 1.0, gsel)
        gs = jnp.where(pick, -jnp.inf, gs)
    emask = jnp.concatenate(
        [jnp.broadcast_to(gsel[g:g + 1, :], (gsz, n)) for g in range(N_EXPERT_GROUPS)], axis=0)
    cand = jnp.where(emask > 0.5, biased, NEG_INF)
    eid = lax.broadcasted_iota(jnp.int32, cand.shape, 0).astype(F32)
    picks = []
    self32 = jnp.zeros(cand.shape, F32)
    for _ in range(TOP_K):
        mx = cand.max(axis=0, keepdims=True)
        pick = eid == jnp.min(jnp.where(cand == mx, eid, float(N_EXPERTS)), axis=0, keepdims=True)
        picks.append(pick)
        self32 = jnp.where(pick, 1.0, self32)
        cand = jnp.where(pick, -jnp.inf, cand)
    pos = _dot(self32.astype(BF16), tri_ref[...]) + carry
    sel_scores = [jnp.sum(jnp.where(p, scores, 0.0), axis=0, keepdims=True) for p in picks]
    wsum = functools.reduce(lambda a, b: a + b, sel_scores)
    zero_f = jnp.zeros((2, n), F32)
    eidx = [jnp.sum(jnp.where(p, eid, 0.0), axis=0, keepdims=True) for p in picks]
    epos = [jnp.sum(jnp.where(p, pos, 0.0), axis=0, keepdims=True) for p in picks]
    ew = [s / wsum * ROUTED_SCALE for s in sel_scores]
    return (jnp.concatenate(eidx + [zero_f], axis=0).astype(jnp.int32),
            jnp.concatenate(epos + [zero_f], axis=0).astype(jnp.int32),
            jnp.concatenate(ew + [zero_f], axis=0),
            carry + jnp.sum(self32, axis=1, keepdims=True))


def _stage_e_kernel(xc_ref, xl_ref, mod_ref, g1_ref, g2_ref, fnc_ref, fnl_ref, omc_ref, oml_ref, osc_ref, osl_ref,
                    h_ref, wg_ref, bg_ref, wf_ref, wm_ref, ws_ref, wo_ref, rwt_ref, rb_ref, tri_ref,
                    x1_ref, h2_ref, eidx_ref, epos_ref, ew_ref, cnt_ref, carry_ref, h2b_ref):
    @pl.when(pl.program_id(0) == 0)
    def _():
        carry_ref[...] = jnp.zeros_like(carry_ref)

    is_ctx = pl.program_id(0) < NB_CTX
    carry = carry_ref[...]
    for rows in _chunks():
        fn = jnp.where(is_ctx, fnc_ref[rows, :], fnl_ref[rows, :])
        om = jnp.where(is_ctx, omc_ref[rows, :], oml_ref[rows, :])
        osw = jnp.where(is_ctx, osc_ref[rows, :], osl_ref[rows, :])
        h = h_ref[rows, :]

        def gate(c):
            lo, hi = c * D_MODEL, (c + 1) * D_MODEL
            return jax.nn.sigmoid(_dot(h, wg_ref[:, lo:hi]) + bg_ref[:, lo:hi])

        merged = (gate(0) * _dot(fn, wf_ref[...]) + gate(1) * _dot(om, wm_ref[...])
                  + gate(2) * _dot(osw, ws_ref[...]))
        mix = _dot(merged.astype(BF16), wo_ref[...])
        x = jnp.where(is_ctx, xc_ref[rows, :], xl_ref[rows, :])
        x1 = x + mod_ref[:, 2048:3072] * _rms_rows(mix, g1_ref[...])
        x1_ref[rows, :] = x1
        h2 = _rms_rows(x1, g2_ref[...]) * (1.0 + mod_ref[:, 4096:5120]) + mod_ref[:, 3072:4096]
        h2_ref[0, rows, :], h2_ref[1, rows, :] = _pack_pair(h2)
        h2b_ref[rows, :] = h2.astype(BF16)
    eidx_ref[...], epos_ref[...], ew_ref[...], carry = _route(h2b_ref[...], rwt_ref, rb_ref, tri_ref, carry)
    carry_ref[...] = carry
    cnt_ref[...] = jnp.broadcast_to(carry, cnt_ref.shape).astype(jnp.int32)


def _stage_e(layer, xc, xl, modt, g1, g2, mixed, h, w_gate, b_gate, wf, wm, ws, wo, rwt, rbias, tri):
    row = lambda w: pl.BlockSpec((TB, w), lambda i: (i, 0))
    ctx, lat = _ctx_rows(512), _lat_rows(512)
    col = lambda dt: (pl.BlockSpec((8, TB), lambda i: (0, i)), jax.ShapeDtypeStruct((8, N_TOK), dt))
    picks = [col(jnp.int32), col(jnp.int32), col(F32)]
    return pl.pallas_call(
        _stage_e_kernel,
        grid=(NB,),
        in_specs=[
            _ctx_rows(D_MODEL), _lat_rows(D_MODEL),
            pl.BlockSpec((None, 1, N_MOD * D_MODEL), lambda i: (i, 0, 0)),
            _const_spec((1, D_MODEL)), _const_spec((1, D_MODEL)),
            ctx, lat, ctx, lat, ctx, lat, row(D_MODEL),
            _layer_spec((D_MODEL, 3 * D_MODEL), layer), _const_spec((1, 3 * D_MODEL)),
            _layer_spec((512, D_MODEL), layer), _layer_spec((512, D_MODEL), layer),
            _layer_spec((512, D_MODEL), layer), _layer_spec((D_MODEL, D_MODEL), layer),
            _const_spec((N_EXPERTS, D_MODEL)), _const_spec((N_EXPERTS, 1)), _const_spec((TB, TB)),
        ],
        out_specs=[row(D_MODEL), pl.BlockSpec((2, TB, PACKED), lambda i: (0, i, 0))] + [s for s, _ in picks]
        + [_const_spec((N_EXPERTS, LANES))],
        out_shape=[jax.ShapeDtypeStruct((N_TOK, D_MODEL), F32),
                   jax.ShapeDtypeStruct((2, N_TOK, PACKED), jnp.int32)] + [s for _, s in picks]
        + [jax.ShapeDtypeStruct((N_EXPERTS, LANES), jnp.int32)],
        scratch_shapes=[pltpu.VMEM((N_EXPERTS, 1), F32), pltpu.VMEM((TB, D_MODEL), BF16)],
        compiler_params=_cparams(1),
        name="stage_e",
    )(xc, xl, modt, g1, g2, *mixed, h, w_gate, b_gate, wf, wm, ws, wo, rwt, rbias, tri)


def _expert_kernel(layer, te_ref, tv_ref, par_ref, nxt_ref, x_ref, w1_hbm, w3_hbm, w2_hbm, o_ref,
                   w1f_ref, w3f_ref, w2f_ref, w1b_ref, w3b_ref, w2b_ref, sem):
    j = pl.program_id(0)
    valid = tv_ref[j] > 0
    first = jnp.logical_and(valid, jnp.logical_or(j == 0, te_ref[j] != te_ref[jnp.maximum(j - 1, 0)]))

    def copies(expert, slot):
        return [pltpu.make_async_copy(w_hbm.at[layer, expert], buf.at[slot], sem.at[slot, i])
                for i, (w_hbm, buf) in enumerate(((w1_hbm, w1f_ref), (w3_hbm, w3f_ref), (w2_hbm, w2f_ref)))]

    @pl.when(jnp.logical_and(valid, j == 0))
    def _():
        for cp in copies(te_ref[0], par_ref[0]):
            cp.start()

    @pl.when(first)
    def _():
        slot = par_ref[j]
        for cp in copies(te_ref[j], slot):
            cp.wait()

        @pl.when(nxt_ref[j] >= 0)
        def _():
            for cp in copies(nxt_ref[j], 1 - slot):
                cp.start()

        w1b_ref[...] = w1f_ref[slot].astype(BF16)
        w3b_ref[...] = w3f_ref[slot].astype(BF16)
        w2b_ref[...] = w2f_ref[slot].astype(BF16)

    def run(n_chunks):
        for r in range(n_chunks):
            rows = pl.ds(r * EXPERT_ROWS, EXPERT_ROWS)
            x = _unpack_pair(x_ref[0, rows, :], x_ref[1, rows, :]).astype(BF16)
            hg = _dot(x, w1b_ref[...])
            hu = _dot(x, w3b_ref[...])
            act = (jax.nn.silu(hg) * hu).astype(BF16)
            o_ref[0, rows, :], o_ref[1, rows, :] = _pack_pair(_dot(act, w2b_ref[...]))

    for n_chunks in range(1, TE // EXPERT_ROWS + 1):
        pl.when(tv_ref[j] == n_chunks)(functools.partial(run, n_chunks))


def _experts(layer, tile_expert, tile_chunks, tile_slot, tile_next, xs, w1, w3, w2):
    slot_rows = pl.BlockSpec((2, TE, PACKED), lambda j, te, tv, par, nxt: (0, j, 0))
    anywhere = pl.BlockSpec(memory_space=pl.ANY)
    grid_spec = pltpu.PrefetchScalarGridSpec(
        num_scalar_prefetch=4,
        grid=(NTE,),
        in_specs=[slot_rows, anywhere, anywhere, anywhere],
        out_specs=slot_rows,
        scratch_shapes=[pltpu.VMEM((2, D_MODEL, EXPERT_FF), F32), pltpu.VMEM((2, D_MODEL, EXPERT_FF), F32),
                        pltpu.VMEM((2, EXPERT_FF, D_MODEL), F32),
                        pltpu.VMEM((D_MODEL, EXPERT_FF), BF16), pltpu.VMEM((D_MODEL, EXPERT_FF), BF16),
                        pltpu.VMEM((EXPERT_FF, D_MODEL), BF16),
                        pltpu.SemaphoreType.DMA((2, 3))],
    )
    return pl.pallas_call(
        functools.partial(_expert_kernel, layer),
        grid_spec=grid_spec,
        out_shape=jax.ShapeDtypeStruct((2, S_MAX, PACKED), jnp.int32),
        compiler_params=_cparams(1),
        name="experts",
    )(tile_expert, tile_chunks, tile_slot, tile_next, xs, w1, w3, w2)


def _sc_mesh():
    return plsc.VectorSubcoreMesh(core_axis_name="c", subcore_axis_name="s",
                                  num_cores=SC_CORES, num_subcores=SC_SUBCORES)


def _sc_scatter_rows(rows, slot8):
    @functools.partial(pl.kernel, mesh=_sc_mesh(), scratch_types=[pltpu.SemaphoreType.DMA],
                       out_type=jax.ShapeDtypeStruct((2, S_MAX, PACKED), jnp.int32))
    def scatter(x_hbm, i_hbm, o_hbm, sem):
        for h in range(2):
            dst = o_hbm.at[h]

            def body(x_vmem, i_vmem, dst=dst):
                copies = [pltpu.async_copy(x_vmem, dst.at[i_vmem.at[k]], sem) for k in range(TOP_K)]
                for cp in copies:
                    cp.wait()

            pltpu.emit_pipeline(
                body,
                grid=(N_TOK // SC_ROWS,),
                in_specs=[pl.BlockSpec((SC_ROWS, PACKED), lambda i: (i, 0)),
                          pl.BlockSpec((8, SC_ROWS), lambda i: (0, i))],
                out_specs=[],
                core_axis_name=("c", "s"),
                dimension_semantics=(pltpu.PARALLEL,),
            )(x_hbm.at[h], i_hbm)

    return scatter(rows, slot8)


def _sc_gather_rows(table, idx):
    n = idx.shape[1]

    @functools.partial(pl.kernel, mesh=_sc_mesh(), scratch_types=[],
                       out_type=jax.ShapeDtypeStruct((2, n, PACKED), jnp.int32))
    def gather(t_hbm, i_hbm, o_hbm):
        for h in range(2):
            src = t_hbm.at[h]

            def body(i_vmem, o_vmem, src=src):
                pltpu.sync_copy(src.at[i_vmem.at[0]], o_vmem)

            pltpu.emit_pipeline(
                body,
                grid=(n // SC_ROWS,),
                in_specs=[pl.BlockSpec((1, SC_ROWS), lambda i: (0, i))],
                out_specs=[pl.BlockSpec((SC_ROWS, PACKED), lambda i: (i, 0))],
                core_axis_name=("c", "s"),
                dimension_semantics=(pltpu.PARALLEL,),
            )(i_hbm, o_hbm.at[h])

    return gather(table, idx)


def _stage_g_kernel(x1_ref, mod_ref, g3_ref, yg_ref, ew_ref, h2_ref,
                    s1_ref, s3_ref, s2_ref, oc_ref, ol_ref):
    is_ctx = pl.program_id(0) < NB_CTX
    for rows in _chunks():
        h = _unpack_pair(h2_ref[0, rows, :], h2_ref[1, rows, :]).astype(BF16)
        act = jax.nn.silu(_dot(h, s1_ref[...])) * _dot(h, s3_ref[...])
        y = _dot(act.astype(BF16), s2_ref[...])
        for k in range(TOP_K):
            y = y + ew_ref[rows, k:k + 1] * _unpack_pair(yg_ref[0, k, rows, :], yg_ref[1, k, rows, :])
        out = x1_ref[rows, :] + mod_ref[:, 5120:6144] * _rms_rows(y, g3_ref[...])

        @pl.when(is_ctx)
        def _():
            oc_ref[rows, :] = out

        @pl.when(jnp.logical_not(is_ctx))
        def _():
            ol_ref[rows, :] = out


def _stage_g(layer, x1, modt, g3, yg, ew_rows, h2, s1, s3, s2):
    row = lambda w: pl.BlockSpec((TB, w), lambda i: (i, 0))
    picked = pl.BlockSpec((2, TOP_K, TB, PACKED), lambda i: (0, 0, i, 0))
    return pl.pallas_call(
        _stage_g_kernel,
        grid=(NB,),
        in_specs=[row(D_MODEL), pl.BlockSpec((None, 1, N_MOD * D_MODEL), lambda i: (i, 0, 0)),
                  _const_spec((1, D_MODEL)), picked, row(8),
                  pl.BlockSpec((2, TB, PACKED), lambda i: (0, i, 0)),
                  _layer_spec((D_MODEL, SHARED_FF), layer), _layer_spec((D_MODEL, SHARED_FF), layer),
                  _layer_spec((SHARED_FF, D_MODEL), layer)],
        out_specs=[_ctx_rows(D_MODEL), _lat_rows(D_MODEL)],
        out_shape=[jax.ShapeDtypeStruct((N_CTX, D_MODEL), F32),
                   jax.ShapeDtypeStruct((N_LAT, D_MODEL), F32)],
        compiler_params=_cparams(1),
        name="stage_g",
    )(x1, modt, g3, yg, ew_rows, h2, s1, s3, s2)


def _rope_tables():
    t = np.arange(DEC_SEQ)
    pos = np.stack([(t // GRID_W), (t % GRID_W)], axis=-1).astype(np.float32)

    def table(r):
        n_freq = r // 4
        inv = np.float32(ROPE_BASE) ** (-np.arange(n_freq, dtype=np.float32) / np.float32(n_freq))
        ang = pos[:, :, None] * inv.astype(np.float32)
        cos = np.cos(ang)
        sin = np.sin(ang)
        cos_t = np.stack([cos, cos], axis=2).reshape(DEC_SEQ, r)
        sin_t = np.stack([-sin, sin], axis=2).reshape(DEC_SEQ, r)
        return cos_t, sin_t

    c64, s64 = table(SWA_HEAD_DIM)
    c32, s32 = table(MLA_ROPE)
    lat = np.concatenate([np.tile(c64, (1, 8)), np.tile(s64, (1, 8)),
                          np.tile(c32, (1, 4)), np.tile(s32, (1, 4))], axis=1)
    ident = np.concatenate([np.ones((TB, 512)), np.zeros((TB, 512)),
                            np.ones((TB, 128)), np.zeros((TB, 128))], axis=1)
    return jnp.asarray(np.concatenate([ident, lat], axis=0).astype(np.float32))


def _dft_pair(n):
    k = np.arange(n, dtype=np.int64)
    ang = ((k[:, None] * k[None, :]) % n).astype(np.float64) * (2.0 * math.pi / n)
    return np.cos(ang), np.sin(ang)


def _fnet_tables():
    c64, s64 = _dft_pair(FNET_GROUP_DIM)
    eye = np.eye(FNET_GROUPS)
    bd = np.concatenate([np.kron(eye, c64), np.kron(eye, s64)], axis=1)
    mats = []
    for t_len in (SEQ, DEC_SEQ):
        c, s = _dft_pair(t_len)
        mats.append(np.concatenate([c, -s], axis=1))
    return tuple(jnp.asarray(m.astype(np.float32).astype(BF16)) for m in (bd, mats[0], mats[1]))


def _layer_weights(l, w_in, w_uq, w_ukv):
    w = w_in[l]
    wide = jnp.concatenate([w[:, 0:1024], w[:, 1056:1824], w[:, 1024:1056],
                            jnp.zeros((D_MODEL, 96), F32)], axis=1).astype(BF16)

    uq = w_uq[l].reshape(MLA_Q_RANK, MLA_HEADS, MLA_NOPE + MLA_ROPE)
    z32 = jnp.zeros((MLA_Q_RANK, MLA_HEADS, 32), F32)
    wqa = jnp.concatenate([uq, z32], axis=2).reshape(MLA_Q_RANK, 1024).astype(BF16)
    ukv = w_ukv[l].reshape(MLA_KV_RANK, MLA_HEADS, MLA_NOPE + MLA_V)
    wk = jnp.concatenate([ukv[:, :, :MLA_NOPE], jnp.zeros((MLA_KV_RANK, MLA_HEADS, 64), F32)],
                         axis=2).reshape(MLA_KV_RANK, 1024).astype(BF16)
    wv = ukv[:, :, MLA_NOPE:].reshape(MLA_KV_RANK, 512).astype(BF16)
    return wide, wqa, wk, wv


def _rope_placement():
    e = np.zeros((128, 1024), np.float32)
    for hd in range(MLA_HEADS):
        for i in range(MLA_ROPE):
            e[i, hd * 128 + MLA_NOPE + i] = 1.0
    return jnp.asarray(e, BF16)


def _moe_dispatch_plan(eidx, epos, counts):
    padded = ((counts + TE - 1) // TE) * TE
    ends = jnp.cumsum(padded)
    offs = ends - padded
    ids = jnp.arange(N_EXPERTS, dtype=jnp.int32)
    picked_off = jnp.sum(jnp.where(eidx[:, :, None] == ids, offs, 0), axis=-1)
    slot = picked_off + epos
    starts = jnp.arange(NTE, dtype=jnp.int32) * TE
    tile_expert = jnp.sum((ends[None, :] <= starts[:, None]).astype(jnp.int32), axis=1)
    tile_expert = jnp.minimum(tile_expert, N_EXPERTS - 1)
    pick = tile_expert[:, None] == ids[None, :]
    last_real = jnp.sum(jnp.where(pick, (offs + counts)[None, :], 0), axis=1)
    n_real = jnp.clip(last_real - starts, 0, TE)
    n_real = jnp.where(starts < ends[-1], n_real, 0)
    tile_chunks = ((n_real + EXPERT_ROWS - 1) // EXPERT_ROWS).astype(jnp.int32)
    used = counts > 0
    rank = jnp.cumsum(used.astype(jnp.int32)) - 1
    tile_slot = jnp.sum(jnp.where(pick, rank[None, :], 0), axis=1) % 2
    later_used = jnp.logical_and(ids[None, :] > ids[:, None], used[None, :])
    next_used = jnp.min(jnp.where(later_used, ids[None, :], N_EXPERTS), axis=1)
    next_used = jnp.where(next_used < N_EXPERTS, next_used, -1)
    tile_next = jnp.sum(jnp.where(pick, next_used[None, :], 0), axis=1)
    return slot, tile_expert, tile_chunks, tile_slot.astype(jnp.int32), tile_next.astype(jnp.int32)


def kernel(x_prompt, x_sample, cache_mla_ckv, cache_mla_krope, cache_swa_k, cache_swa_v, c, c_ctx,
           ada_w, ada_b, norm_g, w_in, q_norm, kv_norm, w_fnet, w_uq, w_ukv, w_mla_o, swa_sink,
           w_swa_o, w_gate, b_gate, w_out, router_w, router_bias, exp_w1, exp_w3, exp_w2,
           shared_w1, shared_w3, shared_w2):
    xc = x_prompt.reshape(N_CTX, D_MODEL)
    xl = x_sample.reshape(N_LAT, D_MODEL)

    cond8 = jnp.concatenate([c_ctx[None, :], c, jnp.zeros((3, D_MODEL), F32)], axis=0)
    mod = _modulation(cond8, ada_w, ada_b)
    tile_cond = np.concatenate([np.zeros(NB_CTX, np.int32),
                                1 + np.arange(NB - NB_CTX, dtype=np.int32) // LAT_BLOCKS])

    tab = _rope_tables()
    bd, f_ctx, f_lat = _fnet_tables()
    e_mat = _rope_placement()
    tri = jnp.asarray(np.triu(np.ones((TB, TB), np.float32), 1), BF16)
    w_gate_b, w_fnet_b, w_mla_o_b, w_swa_o_b, w_out_b, sw1_b, sw3_b, sw2_b = (
        w.astype(BF16) for w in (w_gate, w_fnet, w_mla_o, w_swa_o, w_out, shared_w1, shared_w3, shared_w2))

    new_ckv, new_kr, new_k, new_v = [], [], [], []
    for l in range(DEPTH):
        modt = mod[l][tile_cond][:, None, :]
        wide, wqa, wk, wv = _layer_weights(l, w_in, w_uq, w_ukv)
        ng = norm_g[l]

        fin, ckv, kr, sq, sk, sv, h1, q_m, k_m, v_m = _stage_a(
            xc, xl, modt, ng[0:1], wide, q_norm[l][None, :], kv_norm[l][None, :], tab, wqa, wk, e_mat, wv)

        new_ckv.append(ckv[:N_CTX].reshape(BATCH, SEQ, MLA_KV_RANK))
        new_kr.append(kr[:N_CTX, :MLA_ROPE].reshape(BATCH, SEQ, MLA_ROPE))
        new_k.append(sk[:N_CTX].reshape(BATCH, SEQ, SWA_KV_HEADS, SWA_HEAD_DIM))
        new_v.append(sv[:N_CTX].reshape(BATCH, SEQ, SWA_KV_HEADS, SWA_HEAD_DIM))

        fn = (_fnet(fin, f_ctx, bd, BATCH, SEQ, 0),
              _fnet(fin, f_lat, bd, DEC_BATCH, DEC_SEQ, N_CTX // DEC_SEQ))

        kr_cache = jnp.pad(cache_mla_krope[:, l].reshape(N_CACHE, MLA_ROPE), ((0, 0), (0, 96)))
        k_c, v_c = _mla_cache_kv(cache_mla_ckv[:, l].reshape(N_CACHE, MLA_KV_RANK), kr_cache,
                                 wk, e_mat, wv)
        om = (_mla_attn(q_m, k_m, v_m, k_c, v_c, latent=False),
              _mla_attn(q_m, k_m, v_m, k_c, v_c, latent=True))

        ck = cache_swa_k[:, l].reshape(DEC_BATCH, PAST_LEN, 128)
        cv = cache_swa_v[:, l].reshape(DEC_BATCH, PAST_LEN, 128)
        osw = (_swa_attn(swa_sink[l], sq, sk, sv, ck, cv, latent=False),
               _swa_attn(swa_sink[l], sq, sk, sv, ck, cv, latent=True))

        x1, h2, eidx, epos, ew, counts = _stage_e(
            l, xc, xl, modt, ng[1:2], ng[2:3], fn + om + osw, h1, w_gate_b, b_gate[l][None, :],
            w_fnet_b, w_mla_o_b, w_swa_o_b, w_out_b,
            router_w[l].T.astype(BF16), router_bias[l][:, None], tri)
        slot, tile_expert, tile_chunks, tile_slot, tile_next = _moe_dispatch_plan(eidx, epos, counts[:, 0])
        xs = _sc_scatter_rows(h2, slot)
        ys = _experts(l, tile_expert, tile_chunks, tile_slot, tile_next, xs, exp_w1, exp_w3, exp_w2)
        picks = slot[:TOP_K].reshape(1, TOP_K * N_TOK)
        yg = _sc_gather_rows(ys, picks).reshape(2, TOP_K, N_TOK, PACKED)
        xc, xl = _stage_g(l, x1, modt, ng[3:4], yg, ew.T, h2, sw1_b, sw3_b, sw2_b)

    y_p = xc.reshape(BATCH, SEQ, D_MODEL)
    y_s = xl.reshape(DEC_BATCH, DEC_SEQ, D_MODEL)
    return (y_p, y_s, jnp.stack(new_ckv, axis=1), jnp.stack(new_kr, axis=1),
            jnp.stack(new_k, axis=1), jnp.stack(new_v, axis=1))
```

```python
import functools
import math

import numpy as np
import jax
import jax.numpy as jnp
from jax import lax
from jax.experimental import pallas as pl
from jax.experimental.pallas import tpu as pltpu
from jax.experimental.pallas import tpu_sc as plsc

D_MODEL = 1024
BATCH = 16
SEQ = 256
DEPTH = 2
DEC_BATCH = 4
DEC_SEQ = 2048
PAST_LEN = 512
GRID_W = 64
EPS = 1e-6
ROPE_BASE = 10000.0
NEG_INF = -1e30

FNET_GROUPS = 8
FNET_GROUP_DIM = 64
FNET_WIDTH = 512
MLA_HEADS = 8
MLA_Q_RANK = 384
MLA_KV_RANK = 128
MLA_NOPE = 64
MLA_ROPE = 32
MLA_V = 64
MLA_SCALE = (MLA_NOPE + MLA_ROPE) ** -0.5
LOG2E = math.log2(math.e)
SWA_KV_HEADS = 2
SWA_HEAD_DIM = 64
SWA_WINDOW = 128
SWA_SCALE = SWA_HEAD_DIM ** -0.5
N_MOD = 6
N_EXPERTS = 64
N_EXPERT_GROUPS = 8
TOPK_GROUPS = 4
TOP_K = 6
EXPERT_FF = 256
SHARED_FF = 256
ROUTED_SCALE = 2.5

LANES = 128
TM = 256
N_CTX = BATCH * SEQ
N_LAT = DEC_BATCH * DEC_SEQ
N_TOK = N_CTX + N_LAT
N_CACHE = DEC_BATCH * PAST_LEN
TB = 512
NB = N_TOK // TB
NB_CTX = N_CTX // TB
LAT_BLOCKS = DEC_SEQ // TB
MLA_LAT_TQ = 256
MLA_LAT_PAIRS = 4
FNET_ROWS = 1024
SWA_QSUB = 4
TE = 512
S_MAX = N_TOK * TOP_K + N_EXPERTS * TE
NTE = S_MAX // TE
EXPERT_ROWS = 256
VMEM_LIMIT = 56 * 1024 * 1024
PACKED = D_MODEL // 4
SC_ROWS = 128
SC_CORES = 2
SC_SUBCORES = 16

A_F = (0, 512)
A_QD = (512, 896)
A_KV = (896, 1024)
A_SQ = (1024, 1536)
A_SK = (1536, 1664)
A_SV = (1664, 1792)
A_KR = (1792, 1920)
W_IN_WIDE = 1920
TAB_W = 1280

F32 = jnp.float32
BF16 = jnp.bfloat16


def _cparams(n_axes):
    return pltpu.CompilerParams(dimension_semantics=("arbitrary",) * n_axes, vmem_limit_bytes=VMEM_LIMIT)


def _dot(a, b):
    return jnp.dot(a, b, preferred_element_type=F32)


def _dot_nt(a, b):
    return lax.dot_general(a, b, (((1,), (1,)), ((), ())), preferred_element_type=F32)


def _rms_rows(v, g):
    return v * lax.rsqrt(jnp.mean(v * v, axis=-1, keepdims=True) + EPS) * g


def _pack_rows(v):
    half = v.shape[1] // 2
    lo = lax.bitcast_convert_type(v[:, :half].astype(BF16).astype(F32), jnp.int32)
    hi = lax.bitcast_convert_type(v[:, half:].astype(BF16).astype(F32), jnp.int32)
    return jnp.bitwise_or(jnp.bitwise_and(hi, -65536), jnp.bitwise_and(jnp.right_shift(lo, 16), 65535))


def _unpack_rows(w):
    lo = lax.bitcast_convert_type(jnp.left_shift(w, 16), F32)
    hi = lax.bitcast_convert_type(jnp.bitwise_and(w, -65536), F32)
    return jnp.concatenate([lo, hi], axis=1)


def _pack_pair(v):
    half = v.shape[1] // 2
    return _pack_rows(v[:, :half]), _pack_rows(v[:, half:])


def _unpack_pair(a, b):
    return jnp.concatenate([_unpack_rows(a), _unpack_rows(b)], axis=1)


def _const_spec(shape):
    return pl.BlockSpec(shape, lambda *_: (0,) * len(shape))


def _layer_spec(shape, layer):
    return pl.BlockSpec((None,) + shape, lambda *_: (layer,) + (0,) * len(shape))


def _ctx_rows(width):
    return pl.BlockSpec((TB, width), lambda i: (jnp.minimum(i, NB_CTX - 1), 0))


def _lat_rows(width):
    return pl.BlockSpec((TB, width), lambda i: (jnp.maximum(i - NB_CTX, 0), 0))


def _tab_row_block(i):
    return jnp.where(i < NB_CTX, 0, 1 + (i - NB_CTX) % LAT_BLOCKS)


def _mod_kernel(cond_ref, w_ref, b_ref, o_ref):
    c = cond_ref[...]
    a = (c * jax.nn.sigmoid(c)).astype(BF16)
    o_ref[...] = _dot(a, w_ref[...].astype(BF16)) + b_ref[...]


def _modulation(cond8, ada_w, ada_b):
    tn = 512
    nj = N_MOD * D_MODEL // tn
    return pl.pallas_call(
        _mod_kernel,
        grid=(DEPTH, nj),
        in_specs=[
            pl.BlockSpec((8, D_MODEL), lambda l, j: (0, 0)),
            pl.BlockSpec((None, D_MODEL, tn), lambda l, j: (l, 0, j)),
            pl.BlockSpec((None, 1, tn), lambda l, j: (l, 0, j)),
        ],
        out_specs=pl.BlockSpec((None, 8, tn), lambda l, j: (l, 0, j)),
        out_shape=jax.ShapeDtypeStruct((DEPTH, 8, N_MOD * D_MODEL), F32),
        compiler_params=_cparams(2),
        name="modulation",
    )(cond8, ada_w, ada_b.reshape(DEPTH, 1, N_MOD * D_MODEL))


def _half_swap(x, half):
    n = x.shape[1]
    lane = lax.broadcasted_iota(jnp.int32, (1, n), 1)
    return jnp.where((lane & half) == 0, pltpu.roll(x, n - half, 1), pltpu.roll(x, half, 1))


def _mla_expand(rows, cq, ckv, kr, cos32, sin32, wqa_ref, wk_ref, e_ref, wv_ref,
                q_ref, k_ref, v_ref):
    if q_ref is not None:
        lane = lax.broadcasted_iota(jnp.int32, (1, LANES), 1)
        rope_lane = jnp.logical_and(lane >= MLA_NOPE, lane < MLA_NOPE + MLA_ROPE)
        cos_h = jnp.where(rope_lane, cos32, 1.0)
        sin_h = jnp.where(rope_lane, sin32, 0.0)
        for hd in range(MLA_HEADS):
            lo, hi = hd * LANES, (hd + 1) * LANES
            q = _dot(cq, wqa_ref[:, lo:hi])
            q = q * cos_h + _half_swap(q, MLA_ROPE // 4) * sin_h
            q_ref[rows, lo:hi] = (q * (MLA_SCALE * LOG2E)).astype(BF16)
    k_ref[rows, :] = (_dot(ckv, wk_ref[...]) + _dot(kr, e_ref[...])).astype(BF16)
    v_ref[rows, :] = _dot(ckv, wv_ref[...]).astype(BF16)


def _chunks():
    return [pl.ds(r * TM, TM) for r in range(TB // TM)]


def _stage_a_kernel(xc_ref, xl_ref, mod_ref, g_ref, win_ref, qn_ref, kvn_ref, tab_ref,
                    wqa_ref, wk_ref, e_ref, wv_ref,
                    fin_ref, ckv_ref, kr_ref, sq_ref, sk_ref, sv_ref, h_ref,
                    qm_ref, km_ref, vm_ref):
    is_ctx = pl.program_id(0) < NB_CTX
    for rows in _chunks():
        x = jnp.where(is_ctx, xc_ref[rows, :], xl_ref[rows, :])
        h = (_rms_rows(x, g_ref[...]) * (1.0 + mod_ref[:, 1024:2048]) + mod_ref[:, 0:1024]).astype(BF16)
        h_ref[rows, :] = h

        def proj(seg):
            return _dot(h, win_ref[:, seg[0]:seg[1]])

        fin_ref[rows, :] = proj(A_F).astype(BF16)
        cq = _rms_rows(proj(A_QD), qn_ref[...]).astype(BF16)
        ckv = _rms_rows(proj(A_KV), kvn_ref[...])
        ckv_ref[rows, :] = ckv
        cos64 = tab_ref[rows, 0:512]
        sin64 = tab_ref[rows, 512:1024]
        sq = proj(A_SQ)
        sq = sq * cos64 + _half_swap(sq, SWA_HEAD_DIM // 4) * sin64
        sq_ref[rows, :] = (sq * (SWA_SCALE * LOG2E)).astype(BF16)
        sk = proj(A_SK)
        sk_ref[rows, :] = sk * cos64[:, 0:128] + _half_swap(sk, SWA_HEAD_DIM // 4) * sin64[:, 0:128]
        sv_ref[rows, :] = proj(A_SV)
        cos32 = tab_ref[rows, 1024:1152]
        sin32 = tab_ref[rows, 1152:1280]
        kr = proj(A_KR)
        kr = kr * cos32 + _half_swap(kr, MLA_ROPE // 4) * sin32
        kr_ref[rows, :] = kr
        _mla_expand(rows, cq, ckv.astype(BF16), kr.astype(BF16), cos32, sin32,
                    wqa_ref, wk_ref, e_ref, wv_ref, qm_ref, km_ref, vm_ref)


def _stage_a(xc, xl, modt, g0, w_in_wide, q_norm, kv_norm, tab, wqa, wk, e_mat, wv):
    row = lambda w: pl.BlockSpec((TB, w), lambda i: (i, 0))
    every = lambda w, dt: (row(w), jax.ShapeDtypeStruct((N_TOK, w), dt))
    outs = [every(512, BF16), every(128, F32), every(128, F32), every(512, BF16), every(128, F32),
            every(128, F32), every(D_MODEL, BF16), every(1024, BF16), every(1024, BF16), every(512, BF16)]
    return pl.pallas_call(
        _stage_a_kernel,
        grid=(NB,),
        in_specs=[
            _ctx_rows(D_MODEL), _lat_rows(D_MODEL),
            pl.BlockSpec((None, 1, N_MOD * D_MODEL), lambda i: (i, 0, 0)),
            _const_spec((1, D_MODEL)),
            _const_spec((D_MODEL, W_IN_WIDE)),
            _const_spec((1, MLA_Q_RANK)),
            _const_spec((1, MLA_KV_RANK)),
            pl.BlockSpec((TB, TAB_W), lambda i: (_tab_row_block(i), 0)),
            _const_spec((MLA_Q_RANK, 1024)),
            _const_spec((128, 1024)), _const_spec((128, 1024)), _const_spec((128, 512)),
        ],
        out_specs=[s for s, _ in outs],
        out_shape=[s for _, s in outs],
        compiler_params=_cparams(1),
        name="stage_a",
    )(xc, xl, modt, g0, w_in_wide, q_norm, kv_norm, tab, wqa, wk, e_mat, wv)


def _fnet_kernel(t_len, scale, fin_ref, f_ref, bd_ref, o_ref, zz_ref):
    @pl.when(pl.program_id(1) == 0)
    def _():
        z = fin_ref[...]
        zz_ref[0:t_len, :] = _dot(z, bd_ref[:, 0:512]).astype(BF16)
        zz_ref[t_len:2 * t_len, :] = _dot(z, bd_ref[:, 512:1024]).astype(BF16)

    o_ref[...] = (_dot(f_ref[...], zz_ref[...]) * scale).astype(BF16)


def _fnet(fin, fmat, bd, n_batch, t_len, row_block0):
    scale = 1.0 / math.sqrt(t_len * FNET_GROUP_DIM)
    ft = min(t_len, FNET_ROWS)
    return pl.pallas_call(
        functools.partial(_fnet_kernel, t_len, scale),
        grid=(n_batch, t_len // ft),
        in_specs=[
            pl.BlockSpec((t_len, FNET_WIDTH), lambda b, i: (row_block0 + b, 0)),
            pl.BlockSpec((ft, 2 * t_len), lambda b, i: (i, 0)),
            _const_spec((FNET_WIDTH, 2 * FNET_WIDTH)),
        ],
        out_specs=pl.BlockSpec((ft, FNET_WIDTH), lambda b, i: (b * (t_len // ft) + i, 0)),
        out_shape=jax.ShapeDtypeStruct((n_batch * t_len, FNET_WIDTH), BF16),
        scratch_shapes=[pltpu.VMEM((2 * t_len, FNET_WIDTH), BF16)],
        compiler_params=_cparams(2),
        name=f"fnet_{t_len}",
    )(fin, fmat, bd)


def _mla_cache_kernel(ckv_ref, kr_ref, wk_ref, e_ref, wv_ref, k_ref, v_ref):
    _mla_expand(slice(None), None, ckv_ref[...].astype(BF16), kr_ref[...].astype(BF16), None, None,
                None, wk_ref, e_ref, wv_ref, None, k_ref, v_ref)


def _mla_cache_kv(ckv_cache, kr_cache, wk, e_mat, wv):
    row = lambda w: pl.BlockSpec((TM, w), lambda i: (i, 0))
    return pl.pallas_call(
        _mla_cache_kernel,
        grid=(N_CACHE // TM,),
        in_specs=[row(128), row(128),
                  _const_spec((128, 1024)), _const_spec((128, 1024)), _const_spec((128, 512))],
        out_specs=[row(1024), row(512)],
        out_shape=[jax.ShapeDtypeStruct((N_CACHE, 1024), BF16),
                   jax.ShapeDtypeStruct((N_CACHE, 512), BF16)],
        compiler_params=_cparams(1),
        name="mla_cache_kv",
    )(ckv_cache, kr_cache, wk, e_mat, wv)


def _mla_attn_kernel(n_seg, pairs, q_ref, *refs):
    k_refs = refs[0:n_seg]
    v_refs = refs[n_seg:2 * n_seg]
    o_ref = refs[2 * n_seg]
    lane = lax.broadcasted_iota(jnp.int32, (1, LANES), 1)
    low = lane < MLA_V
    for pr in range(pairs):
        outs = []
        for hh in range(2):
            hd = 2 * pr + hh
            q = q_ref[:, hd * LANES:(hd + 1) * LANES]
            ss = [_dot_nt(q, k[:, hd * LANES:(hd + 1) * LANES]) for k in k_refs]
            m = functools.reduce(jnp.maximum, [s.max(axis=-1, keepdims=True) for s in ss])
            keep = low if hh == 0 else jnp.logical_not(low)
            sum_lane = MLA_V if hh == 0 else 0
            po = None
            for s, v_ref in zip(ss, v_refs):
                v = v_ref[:, pr * LANES:(pr + 1) * LANES]
                vm = jnp.where(lane == sum_lane, jnp.ones_like(v), jnp.where(keep, v, jnp.zeros_like(v)))
                t = _dot(jnp.exp2(s - m).astype(BF16), vm)
                po = t if po is None else po + t
            outs.append(po / po[:, sum_lane:sum_lane + 1])
        o_ref[:, pr * LANES:(pr + 1) * LANES] = jnp.where(low, outs[0], outs[1]).astype(BF16)


def _mla_attn(q_all, k_all, v_all, k_cache, v_cache, latent):
    if latent:
        tq, pairs = MLA_LAT_TQ, MLA_LAT_PAIRS
        n_b, n_q = DEC_BATCH, DEC_SEQ // tq
        q0 = N_CTX // tq
        kv_specs = [
            pl.BlockSpec((PAST_LEN, 256 * pairs), lambda b, hp, i: (b, hp)),
            pl.BlockSpec((DEC_SEQ, 256 * pairs), lambda b, hp, i: (N_CTX // DEC_SEQ + b, hp)),
            pl.BlockSpec((PAST_LEN, 128 * pairs), lambda b, hp, i: (b, hp)),
            pl.BlockSpec((DEC_SEQ, 128 * pairs), lambda b, hp, i: (N_CTX // DEC_SEQ + b, hp)),
        ]
        args = (q_all, k_cache, k_all, v_cache, v_all)
        n_seg = 2
    else:
        tq, pairs = SEQ, MLA_HEADS // 2
        n_b, n_q = BATCH, 1
        q0 = 0
        kv_specs = [
            pl.BlockSpec((SEQ, 256 * pairs), lambda b, hp, i: (b, hp)),
            pl.BlockSpec((SEQ, 128 * pairs), lambda b, hp, i: (b, hp)),
        ]
        args = (q_all, k_all, v_all)
        n_seg = 1
    return pl.pallas_call(
        functools.partial(_mla_attn_kernel, n_seg, pairs),
        grid=(n_b, MLA_HEADS // (2 * pairs), n_q),
        in_specs=[pl.BlockSpec((tq, 256 * pairs), lambda b, hp, i: (q0 + b * n_q + i, hp))] + kv_specs,
        out_specs=pl.BlockSpec((tq, 128 * pairs), lambda b, hp, i: (b * n_q + i, hp)),
        out_shape=jax.ShapeDtypeStruct((n_b * n_q * tq, MLA_HEADS * MLA_V), BF16),
        compiler_params=_cparams(3),
        name="mla_attn_lat" if latent else "mla_attn_ctx",
    )(*args)


def _swa_kernel(windowed, n_steps, sink_ref, q_ref, *refs):
    n_seg = 4 if windowed else 1
    k_refs = refs[0:n_seg]
    v_refs = refs[n_seg:2 * n_seg]
    o_ref = refs[2 * n_seg]
    step = pl.program_id(1)
    lane = lax.broadcasted_iota(jnp.int32, (1, LANES), 1)
    low = lane < SWA_HEAD_DIM
    high = jnp.logical_not(low)

    k_all = jnp.concatenate([r[...] for r in k_refs], axis=0)
    v_all = jnp.concatenate([r[...] for r in v_refs], axis=0)
    k_sw = pltpu.roll(k_all, SWA_HEAD_DIM, 1)
    v_sw = pltpu.roll(v_all, SWA_HEAD_DIM, 1)

    if windowed:
        tq = SWA_WINDOW
        qi = lax.broadcasted_iota(jnp.int32, (2 * tq, tq), 0) % tq
        kj = lax.broadcasted_iota(jnp.int32, (2 * tq, tq), 1)
        after = kj >= qi
        before = kj <= qi
        n_sub = q_ref.shape[0] // tq
        bias_lo = [jnp.where(jnp.logical_and(after, step > 0) if sub == 0 else after, 0.0, NEG_INF)
                   for sub in range(n_sub)]
        bias_hi = [jnp.where(jnp.logical_and(before, step < n_steps - 1) if sub == n_sub - 1 else before,
                             0.0, NEG_INF) for sub in range(n_sub)]
    else:
        tq = q_ref.shape[0]
        n_sub = 1
    top_rows = lax.broadcasted_iota(jnp.int32, (2 * tq, 1), 0) < tq

    for g in range(SWA_KV_HEADS):
        kh, vh = [], []
        for half in range(2):
            keep = low if half == 0 else high
            sum_lane = SWA_HEAD_DIM if half == 0 else 0
            straight = (g == half)
            kh.append(jnp.where(keep, k_all if straight else k_sw, 0.0).astype(BF16))
            vh.append(jnp.where(lane == sum_lane, 1.0,
                                jnp.where(keep, v_all if straight else v_sw, 0.0)).astype(BF16))
        for sub in range(n_sub):
            rows = slice(sub * tq, (sub + 1) * tq)
            qs = jnp.concatenate([q_ref[rows, 256 * g:256 * g + 128],
                                  q_ref[rows, 256 * g + 128:256 * g + 256]], axis=0)
            halves = []
            for half in range(2):
                sum_lane = SWA_HEAD_DIM if half == 0 else 0
                ks, vs = kh[half], vh[half]
                if windowed:
                    w0 = PAST_LEN + sub * tq
                    if sub == 0:
                        ks, vs = ks[0:w0 + 3 * tq], vs[0:w0 + 3 * tq]
                    else:
                        ks = jnp.concatenate([ks[0:PAST_LEN], ks[w0:w0 + 3 * tq]], axis=0)
                        vs = jnp.concatenate([vs[0:PAST_LEN], vs[w0:w0 + 3 * tq]], axis=0)
                s = _dot_nt(qs, ks)
                if windowed:
                    c0, c1, c2 = PAST_LEN, PAST_LEN + tq, PAST_LEN + 2 * tq
                    s = jnp.concatenate([s[:, :c0], s[:, c0:c1] + bias_lo[sub], s[:, c1:c2],
                                         s[:, c2:] + bias_hi[sub]], axis=1)
                sink = jnp.where(top_rows, sink_ref[4 * g + half], sink_ref[4 * g + 2 + half]) * LOG2E
                m = jnp.maximum(s.max(axis=-1, keepdims=True), sink)
                po = _dot(jnp.exp2(s - m).astype(BF16), vs)
                halves.append(po / (po[:, sum_lane:sum_lane + 1] + jnp.exp2(sink - m)))
            out = jnp.where(low, halves[0], halves[1])
            o_ref[rows, 256 * g:256 * g + 128] = out[0:tq].astype(BF16)
            o_ref[rows, 256 * g + 128:256 * g + 256] = out[tq:2 * tq].astype(BF16)


def _swa_attn(sink, sq, sk, sv, cache_k, cache_v, latent):
    smem = pl.BlockSpec(memory_space=pltpu.SMEM)
    if latent:
        tq = SWA_QSUB * SWA_WINDOW
        n_b, n_qb = DEC_BATCH, DEC_SEQ // tq
        base = N_CTX // tq
        last = DEC_SEQ // SWA_WINDOW - 1

        def prev(b, i):
            return (SWA_QSUB * (base + b * n_qb) + jnp.maximum(SWA_QSUB * i - 1, 0), 0)

        def cur(b, i):
            return (base + b * n_qb + i, 0)

        def nxt(b, i):
            return (SWA_QSUB * (base + b * n_qb) + jnp.minimum(SWA_QSUB * (i + 1), last), 0)

        cache = pl.BlockSpec((None, PAST_LEN, 128), lambda b, i: (b, 0, 0))
        edge = lambda f: pl.BlockSpec((SWA_WINDOW, 128), f)
        kv_specs = [cache, edge(prev), pl.BlockSpec((tq, 128), cur), edge(nxt)] * 2
        args = (cache_k, sk, sk, sk, cache_v, sv, sv, sv)
        q_spec = pl.BlockSpec((tq, 512), cur)
        o_spec = pl.BlockSpec((tq, 512), lambda b, i: (b * n_qb + i, 0))
    else:
        tq = SEQ
        n_b, n_qb = BATCH, 1
        blk = pl.BlockSpec((tq, 128), lambda b, i: (b, 0))
        kv_specs = [blk, blk]
        args = (sk, sv)
        q_spec = pl.BlockSpec((tq, 512), lambda b, i: (b, 0))
        o_spec = q_spec
    return pl.pallas_call(
        functools.partial(_swa_kernel, latent, n_qb),
        grid=(n_b, n_qb),
        in_specs=[smem, q_spec] + kv_specs,
        out_specs=o_spec,
        out_shape=jax.ShapeDtypeStruct((n_b * n_qb * tq, 512), BF16),
        compiler_params=_cparams(2),
        name="swa_lat" if latent else "swa_ctx",
    )(sink, sq, *args)


def _route(h, rwt_ref, rb_ref, tri_ref, carry):
    n = h.shape[0]
    gsz = N_EXPERTS // N_EXPERT_GROUPS
    scores = jax.nn.sigmoid(_dot_nt(rwt_ref[...], h))
    biased = scores + rb_ref[...]
    mem = lax.broadcasted_iota(jnp.int32, (gsz, n), 0).astype(F32)
    gs_rows = []
    for g in range(N_EXPERT_GROUPS):
        bg = biased[g * gsz:(g + 1) * gsz, :]
        m1 = bg.max(axis=0, keepdims=True)
        first = jnp.min(jnp.where(bg == m1, mem, float(gsz)), axis=0, keepdims=True)
        m2 = jnp.where(mem == first, -jnp.inf, bg).max(axis=0, keepdims=True)
        gs_rows.append(m1 + m2)
    gs = jnp.concatenate(gs_rows, axis=0)
    gid = lax.broadcasted_iota(jnp.int32, gs.shape, 0).astype(F32)
    gsel = jnp.zeros(gs.shape, F32)
    for _ in range(TOPK_GROUPS):
        mx = gs.max(axis=0, keepdims=True)
        pick = gid == jnp.min(jnp.where(gs == mx, gid, float(N_EXPERT_GROUPS)), axis=0, keepdims=True)
        gsel = jnp.where(pick, 1.0, gsel)
        gs = jnp.where(pick, -jnp.inf, gs)
    emask = jnp.concatenate(
        [jnp.broadcast_to(gsel[g:g + 1, :], (gsz, n)) for g in range(N_EXPERT_GROUPS)], axis=0)
    cand = jnp.where(emask > 0.5, biased, NEG_INF)
    eid = lax.broadcasted_iota(jnp.int32, cand.shape, 0).astype(F32)
    picks = []
    self32 = jnp.zeros(cand.shape, F32)
    for _ in range(TOP_K):
        mx = cand.max(axis=0, keepdims=True)
        pick = eid == jnp.min(jnp.where(cand == mx, eid, float(N_EXPERTS)), axis=0, keepdims=True)
        picks.append(pick)
        self32 = jnp.where(pick, 1.0, self32)
        cand = jnp.where(pick, -jnp.inf, cand)
    pos = _dot(self32.astype(BF16), tri_ref[...]) + carry
    sel_scores = [jnp.sum(jnp.where(p, scores, 0.0), axis=0, keepdims=True) for p in picks]
    wsum = functools.reduce(lambda a, b: a + b, sel_scores)
    zero_f = jnp.zeros((2, n), F32)
    eidx = [jnp.sum(jnp.where(p, eid, 0.0), axis=0, keepdims=True) for p in picks]
    epos = [jnp.sum(jnp.where(p, pos, 0.0), axis=0, keepdims=True) for p in picks]
    ew = [s / wsum * ROUTED_SCALE for s in sel_scores]
    return (jnp.concatenate(eidx + [zero_f], axis=0).astype(jnp.int32),
            jnp.concatenate(epos + [zero_f], axis=0).astype(jnp.int32),
            jnp.concatenate(ew + [zero_f], axis=0),
            carry + jnp.sum(self32, axis=1, keepdims=True))


def _stage_e_kernel(xc_ref, xl_ref, mod_ref, g1_ref, g2_ref, fnc_ref, fnl_ref, omc_ref, oml_ref, osc_ref, osl_ref,
                    h_ref, wg_ref, bg_ref, wf_ref, wm_ref, ws_ref, wo_ref, rwt_ref, rb_ref, tri_ref,
                    x1_ref, h2_ref, eidx_ref, epos_ref, ew_ref, cnt_ref, carry_ref, h2b_ref):
    @pl.when(pl.program_id(0) == 0)
    def _():
        carry_ref[...] = jnp.zeros_like(carry_ref)

    is_ctx = pl.program_id(0) < NB_CTX
    carry = carry_ref[...]
    for rows in _chunks():
        fn = jnp.where(is_ctx, fnc_ref[rows, :], fnl_ref[rows, :])
        om = jnp.where(is_ctx, omc_ref[rows, :], oml_ref[rows, :])
        osw = jnp.where(is_ctx, osc_ref[rows, :], osl_ref[rows, :])
        h = h_ref[rows, :]

        def gate(c):
            lo, hi = c * D_MODEL, (c + 1) * D_MODEL
            return jax.nn.sigmoid(_dot(h, wg_ref[:, lo:hi]) + bg_ref[:, lo:hi])

        merged = (gate(0) * _dot(fn, wf_ref[...]) + gate(1) * _dot(om, wm_ref[...])
                  + gate(2) * _dot(osw, ws_ref[...]))
        mix = _dot(merged.astype(BF16), wo_ref[...])
        x = jnp.where(is_ctx, xc_ref[rows, :], xl_ref[rows, :])
        x1 = x + mod_ref[:, 2048:3072] * _rms_rows(mix, g1_ref[...])
        x1_ref[rows, :] = x1
        h2 = _rms_rows(x1, g2_ref[...]) * (1.0 + mod_ref[:, 4096:5120]) + mod_ref[:, 3072:4096]
        h2_ref[0, rows, :], h2_ref[1, rows, :] = _pack_pair(h2)
        h2b_ref[rows, :] = h2.astype(BF16)
    eidx_ref[...], epos_ref[...], ew_ref[...], carry = _route(h2b_ref[...], rwt_ref, rb_ref, tri_ref, carry)
    carry_ref[...] = carry
    cnt_ref[...] = jnp.broadcast_to(carry, cnt_ref.shape).astype(jnp.int32)


def _stage_e(layer, xc, xl, modt, g1, g2, mixed, h, w_gate, b_gate, wf, wm, ws, wo, rwt, rbias, tri):
    row = lambda w: pl.BlockSpec((TB, w), lambda i: (i, 0))
    ctx, lat = _ctx_rows(512), _lat_rows(512)
    col = lambda dt: (pl.BlockSpec((8, TB), lambda i: (0, i)), jax.ShapeDtypeStruct((8, N_TOK), dt))
    picks = [col(jnp.int32), col(jnp.int32), col(F32)]
    return pl.pallas_call(
        _stage_e_kernel,
        grid=(NB,),
        in_specs=[
            _ctx_rows(D_MODEL), _lat_rows(D_MODEL),
            pl.BlockSpec((None, 1, N_MOD * D_MODEL), lambda i: (i, 0, 0)),
            _const_spec((1, D_MODEL)), _const_spec((1, D_MODEL)),
            ctx, lat, ctx, lat, ctx, lat, row(D_MODEL),
            _layer_spec((D_MODEL, 3 * D_MODEL), layer), _const_spec((1, 3 * D_MODEL)),
            _layer_spec((512, D_MODEL), layer), _layer_spec((512, D_MODEL), layer),
            _layer_spec((512, D_MODEL), layer), _layer_spec((D_MODEL, D_MODEL), layer),
            _const_spec((N_EXPERTS, D_MODEL)), _const_spec((N_EXPERTS, 1)), _const_spec((TB, TB)),
        ],
        out_specs=[row(D_MODEL), pl.BlockSpec((2, TB, PACKED), lambda i: (0, i, 0))] + [s for s, _ in picks]
        + [_const_spec((N_EXPERTS, LANES))],
        out_shape=[jax.ShapeDtypeStruct((N_TOK, D_MODEL), F32),
                   jax.ShapeDtypeStruct((2, N_TOK, PACKED), jnp.int32)] + [s for _, s in picks]
        + [jax.ShapeDtypeStruct((N_EXPERTS, LANES), jnp.int32)],
        scratch_shapes=[pltpu.VMEM((N_EXPERTS, 1), F32), pltpu.VMEM((TB, D_MODEL), BF16)],
        compiler_params=_cparams(1),
        name="stage_e",
    )(xc, xl, modt, g1, g2, *mixed, h, w_gate, b_gate, wf, wm, ws, wo, rwt, rbias, tri)


def _expert_kernel(layer, te_ref, tv_ref, par_ref, nxt_ref, x_ref, w1_hbm, w3_hbm, w2_hbm, o_ref,
                   w1f_ref, w3f_ref, w2f_ref, w1b_ref, w3b_ref, w2b_ref, sem):
    j = pl.program_id(0)
    valid = tv_ref[j] > 0
    first = jnp.logical_and(valid, jnp.logical_or(j == 0, te_ref[j] != te_ref[jnp.maximum(j - 1, 0)]))

    def copies(expert, slot):
        return [pltpu.make_async_copy(w_hbm.at[layer, expert], buf.at[slot], sem.at[slot, i])
                for i, (w_hbm, buf) in enumerate(((w1_hbm, w1f_ref), (w3_hbm, w3f_ref), (w2_hbm, w2f_ref)))]

    @pl.when(jnp.logical_and(valid, j == 0))
    def _():
        for cp in copies(te_ref[0], par_ref[0]):
            cp.start()

    @pl.when(first)
    def _():
        slot = par_ref[j]
        for cp in copies(te_ref[j], slot):
            cp.wait()

        @pl.when(nxt_ref[j] >= 0)
        def _():
            for cp in copies(nxt_ref[j], 1 - slot):
                cp.start()

        w1b_ref[...] = w1f_ref[slot].astype(BF16)
        w3b_ref[...] = w3f_ref[slot].astype(BF16)
        w2b_ref[...] = w2f_ref[slot].astype(BF16)

    def run(n_chunks):
        for r in range(n_chunks):
            rows = pl.ds(r * EXPERT_ROWS, EXPERT_ROWS)
            x = _unpack_pair(x_ref[0, rows, :], x_ref[1, rows, :]).astype(BF16)
            hg = _dot(x, w1b_ref[...])
            hu = _dot(x, w3b_ref[...])
            act = (jax.nn.silu(hg) * hu).astype(BF16)
            o_ref[0, rows, :], o_ref[1, rows, :] = _pack_pair(_dot(act, w2b_ref[...]))

    for n_chunks in range(1, TE // EXPERT_ROWS + 1):
        pl.when(tv_ref[j] == n_chunks)(functools.partial(run, n_chunks))


def _experts(layer, tile_expert, tile_chunks, tile_slot, tile_next, xs, w1, w3, w2):
    slot_rows = pl.BlockSpec((2, TE, PACKED), lambda j, te, tv, par, nxt: (0, j, 0))
    anywhere = pl.BlockSpec(memory_space=pl.ANY)
    grid_spec = pltpu.PrefetchScalarGridSpec(
        num_scalar_prefetch=4,
        grid=(NTE,),
        in_specs=[slot_rows, anywhere, anywhere, anywhere],
        out_specs=slot_rows,
        scratch_shapes=[pltpu.VMEM((2, D_MODEL, EXPERT_FF), F32), pltpu.VMEM((2, D_MODEL, EXPERT_FF), F32),
                        pltpu.VMEM((2, EXPERT_FF, D_MODEL), F32),
                        pltpu.VMEM((D_MODEL, EXPERT_FF), BF16), pltpu.VMEM((D_MODEL, EXPERT_FF), BF16),
                        pltpu.VMEM((EXPERT_FF, D_MODEL), BF16),
                        pltpu.SemaphoreType.DMA((2, 3))],
    )
    return pl.pallas_call(
        functools.partial(_expert_kernel, layer),
        grid_spec=grid_spec,
        out_shape=jax.ShapeDtypeStruct((2, S_MAX, PACKED), jnp.int32),
        compiler_params=_cparams(1),
        name="experts",
    )(tile_expert, tile_chunks, tile_slot, tile_next, xs, w1, w3, w2)


def _sc_mesh():
    return plsc.VectorSubcoreMesh(core_axis_name="c", subcore_axis_name="s",
                                  num_cores=SC_CORES, num_subcores=SC_SUBCORES)


def _sc_scatter_rows(rows, slot8):
    @functools.partial(pl.kernel, mesh=_sc_mesh(), scratch_types=[pltpu.SemaphoreType.DMA],
                       out_type=jax.ShapeDtypeStruct((2, S_MAX, PACKED), jnp.int32))
    def scatter(x_hbm, i_hbm, o_hbm, sem):
        for h in range(2):
            dst = o_hbm.at[h]

            def body(x_vmem, i_vmem, dst=dst):
                copies = [pltpu.async_copy(x_vmem, dst.at[i_vmem.at[k]], sem) for k in range(TOP_K)]
                for cp in copies:
                    cp.wait()

            pltpu.emit_pipeline(
                body,
                grid=(N_TOK // SC_ROWS,),
                in_specs=[pl.BlockSpec((SC_ROWS, PACKED), lambda i: (i, 0)),
                          pl.BlockSpec((8, SC_ROWS), lambda i: (0, i))],
                out_specs=[],
                core_axis_name=("c", "s"),
                dimension_semantics=(pltpu.PARALLEL,),
            )(x_hbm.at[h], i_hbm)

    return scatter(rows, slot8)


def _sc_gather_rows(table, idx):
    n = idx.shape[1]

    @functools.partial(pl.kernel, mesh=_sc_mesh(), scratch_types=[],
                       out_type=jax.ShapeDtypeStruct((2, n, PACKED), jnp.int32))
    def gather(t_hbm, i_hbm, o_hbm):
        for h in range(2):
            src = t_hbm.at[h]

            def body(i_vmem, o_vmem, src=src):
                pltpu.sync_copy(src.at[i_vmem.at[0]], o_vmem)

            pltpu.emit_pipeline(
                body,
                grid=(n // SC_ROWS,),
                in_specs=[pl.BlockSpec((1, SC_ROWS), lambda i: (0, i))],
                out_specs=[pl.BlockSpec((SC_ROWS, PACKED), lambda i: (i, 0))],
                core_axis_name=("c", "s"),
                dimension_semantics=(pltpu.PARALLEL,),
            )(i_hbm, o_hbm.at[h])

    return gather(table, idx)


def _stage_g_kernel(x1_ref, mod_ref, g3_ref, yg_ref, ew_ref, h2_ref,
                    s1_ref, s3_ref, s2_ref, oc_ref, ol_ref):
    is_ctx = pl.program_id(0) < NB_CTX
    for rows in _chunks():
        h = _unpack_pair(h2_ref[0, rows, :], h2_ref[1, rows, :]).astype(BF16)
        act = jax.nn.silu(_dot(h, s1_ref[...])) * _dot(h, s3_ref[...])
        y = _dot(act.astype(BF16), s2_ref[...])
        for k in range(TOP_K):
            y = y + ew_ref[rows, k:k + 1] * _unpack_pair(yg_ref[0, k, rows, :], yg_ref[1, k, rows, :])
        out = x1_ref[rows, :] + mod_ref[:, 5120:6144] * _rms_rows(y, g3_ref[...])

        @pl.when(is_ctx)
        def _():
            oc_ref[rows, :] = out

        @pl.when(jnp.logical_not(is_ctx))
        def _():
            ol_ref[rows, :] = out


def _stage_g(layer, x1, modt, g3, yg, ew_rows, h2, s1, s3, s2):
    row = lambda w: pl.BlockSpec((TB, w), lambda i: (i, 0))
    picked = pl.BlockSpec((2, TOP_K, TB, PACKED), lambda i: (0, 0, i, 0))
    return pl.pallas_call(
        _stage_g_kernel,
        grid=(NB,),
        in_specs=[row(D_MODEL), pl.BlockSpec((None, 1, N_MOD * D_MODEL), lambda i: (i, 0, 0)),
                  _const_spec((1, D_MODEL)), picked, row(8),
                  pl.BlockSpec((2, TB, PACKED), lambda i: (0, i, 0)),
                  _layer_spec((D_MODEL, SHARED_FF), layer), _layer_spec((D_MODEL, SHARED_FF), layer),
                  _layer_spec((SHARED_FF, D_MODEL), layer)],
        out_specs=[_ctx_rows(D_MODEL), _lat_rows(D_MODEL)],
        out_shape=[jax.ShapeDtypeStruct((N_CTX, D_MODEL), F32),
                   jax.ShapeDtypeStruct((N_LAT, D_MODEL), F32)],
        compiler_params=_cparams(1),
        name="stage_g",
    )(x1, modt, g3, yg, ew_rows, h2, s1, s3, s2)


def _rope_tables():
    t = np.arange(DEC_SEQ)
    pos = np.stack([(t // GRID_W), (t % GRID_W)], axis=-1).astype(np.float32)

    def table(r):
        n_freq = r // 4
        inv = np.float32(ROPE_BASE) ** (-np.arange(n_freq, dtype=np.float32) / np.float32(n_freq))
        ang = pos[:, :, None] * inv.astype(np.float32)
        cos = np.cos(ang)
        sin = np.sin(ang)
        cos_t = np.stack([cos, cos], axis=2).reshape(DEC_SEQ, r)
        sin_t = np.stack([-sin, sin], axis=2).reshape(DEC_SEQ, r)
        return cos_t, sin_t

    c64, s64 = table(SWA_HEAD_DIM)
    c32, s32 = table(MLA_ROPE)
    lat = np.concatenate([np.tile(c64, (1, 8)), np.tile(s64, (1, 8)),
                          np.tile(c32, (1, 4)), np.tile(s32, (1, 4))], axis=1)
    ident = np.concatenate([np.ones((TB, 512)), np.zeros((TB, 512)),
                            np.ones((TB, 128)), np.zeros((TB, 128))], axis=1)
    return jnp.asarray(np.concatenate([ident, lat], axis=0).astype(np.float32))


def _dft_pair(n):
    k = np.arange(n, dtype=np.int64)
    ang = ((k[:, None] * k[None, :]) % n).astype(np.float64) * (2.0 * math.pi / n)
    return np.cos(ang), np.sin(ang)


def _fnet_tables():
    c64, s64 = _dft_pair(FNET_GROUP_DIM)
    eye = np.eye(FNET_GROUPS)
    bd = np.concatenate([np.kron(eye, c64), np.kron(eye, s64)], axis=1)
    mats = []
    for t_len in (SEQ, DEC_SEQ):
        c, s = _dft_pair(t_len)
        mats.append(np.concatenate([c, -s], axis=1))
    return tuple(jnp.asarray(m.astype(np.float32).astype(BF16)) for m in (bd, mats[0], mats[1]))


def _layer_weights(l, w_in, w_uq, w_ukv):
    w = w_in[l]
    wide = jnp.concatenate([w[:, 0:1024], w[:, 1056:1824], w[:, 1024:1056],
                            jnp.zeros((D_MODEL, 96), F32)], axis=1).astype(BF16)

    uq = w_uq[l].reshape(MLA_Q_RANK, MLA_HEADS, MLA_NOPE + MLA_ROPE)
    z32 = jnp.zeros((MLA_Q_RANK, MLA_HEADS, 32), F32)
    wqa = jnp.concatenate([uq, z32], axis=2).reshape(MLA_Q_RANK, 1024).astype(BF16)
    ukv = w_ukv[l].reshape(MLA_KV_RANK, MLA_HEADS, MLA_NOPE + MLA_V)
    wk = jnp.concatenate([ukv[:, :, :MLA_NOPE], jnp.zeros((MLA_KV_RANK, MLA_HEADS, 64), F32)],
                         axis=2).reshape(MLA_KV_RANK, 1024).astype(BF16)
    wv = ukv[:, :, MLA_NOPE:].reshape(MLA_KV_RANK, 512).astype(BF16)
    return wide, wqa, wk, wv


def _rope_placement():
    e = np.zeros((128, 1024), np.float32)
    for hd in range(MLA_HEADS):
        for i in range(MLA_ROPE):
            e[i, hd * 128 + MLA_NOPE + i] = 1.0
    return jnp.asarray(e, BF16)


def _moe_dispatch_plan(eidx, epos, counts):
    padded = ((counts + TE - 1) // TE) * TE
    ends = jnp.cumsum(padded)
    offs = ends - padded
    ids = jnp.arange(N_EXPERTS, dtype=jnp.int32)
    picked_off = jnp.sum(jnp.where(eidx[:, :, None] == ids, offs, 0), axis=-1)
    slot = picked_off + epos
    starts = jnp.arange(NTE, dtype=jnp.int32) * TE
    tile_expert = jnp.sum((ends[None, :] <= starts[:, None]).astype(jnp.int32), axis=1)
    tile_expert = jnp.minimum(tile_expert, N_EXPERTS - 1)
    pick = tile_expert[:, None] == ids[None, :]
    last_real = jnp.sum(jnp.where(pick, (offs + counts)[None, :], 0), axis=1)
    n_real = jnp.clip(last_real - starts, 0, TE)
    n_real = jnp.where(starts < ends[-1], n_real, 0)
    tile_chunks = ((n_real + EXPERT_ROWS - 1) // EXPERT_ROWS).astype(jnp.int32)
    used = counts > 0
    rank = jnp.cumsum(used.astype(jnp.int32)) - 1
    tile_slot = jnp.sum(jnp.where(pick, rank[None, :], 0), axis=1) % 2
    later_used = jnp.logical_and(ids[None, :] > ids[:, None], used[None, :])
    next_used = jnp.min(jnp.where(later_used, ids[None, :], N_EXPERTS), axis=1)
    next_used = jnp.where(next_used < N_EXPERTS, next_used, -1)
    tile_next = jnp.sum(jnp.where(pick, next_used[None, :], 0), axis=1)
    return slot, tile_expert, tile_chunks, tile_slot.astype(jnp.int32), tile_next.astype(jnp.int32)


def kernel(x_prompt, x_sample, cache_mla_ckv, cache_mla_krope, cache_swa_k, cache_swa_v, c, c_ctx,
           ada_w, ada_b, norm_g, w_in, q_norm, kv_norm, w_fnet, w_uq, w_ukv, w_mla_o, swa_sink,
           w_swa_o, w_gate, b_gate, w_out, router_w, router_bias, exp_w1, exp_w3, exp_w2,
           shared_w1, shared_w3, shared_w2):
    xc = x_prompt.reshape(N_CTX, D_MODEL)
    xl = x_sample.reshape(N_LAT, D_MODEL)

    cond8 = jnp.concatenate([c_ctx[None, :], c, jnp.zeros((3, D_MODEL), F32)], axis=0)
    mod = _modulation(cond8, ada_w, ada_b)
    tile_cond = np.concatenate([np.zeros(NB_CTX, np.int32),
                                1 + np.arange(NB - NB_CTX, dtype=np.int32) // LAT_BLOCKS])

    tab = _rope_tables()
    bd, f_ctx, f_lat = _fnet_tables()
    e_mat = _rope_placement()
    tri = jnp.asarray(np.triu(np.ones((TB, TB), np.float32), 1), BF16)
    w_gate_b, w_fnet_b, w_mla_o_b, w_swa_o_b, w_out_b, sw1_b, sw3_b, sw2_b = (
        w.astype(BF16) for w in (w_gate, w_fnet, w_mla_o, w_swa_o, w_out, shared_w1, shared_w3, shared_w2))

    new_ckv, new_kr, new_k, new_v = [], [], [], []
    for l in range(DEPTH):
        modt = mod[l][tile_cond][:, None, :]
        wide, wqa, wk, wv = _layer_weights(l, w_in, w_uq, w_ukv)
        ng = norm_g[l]

        fin, ckv, kr, sq, sk, sv, h1, q_m, k_m, v_m = _stage_a(
            xc, xl, modt, ng[0:1], wide, q_norm[l][None, :], kv_norm[l][None, :], tab, wqa, wk, e_mat, wv)

        new_ckv.append(ckv[:N_CTX].reshape(BATCH, SEQ, MLA_KV_RANK))
        new_kr.append(kr[:N_CTX, :MLA_ROPE].reshape(BATCH, SEQ, MLA_ROPE))
        new_k.append(sk[:N_CTX].reshape(BATCH, SEQ, SWA_KV_HEADS, SWA_HEAD_DIM))
        new_v.append(sv[:N_CTX].reshape(BATCH, SEQ, SWA_KV_HEADS, SWA_HEAD_DIM))

        fn = (_fnet(fin, f_ctx, bd, BATCH, SEQ, 0),
              _fnet(fin, f_lat, bd, DEC_BATCH, DEC_SEQ, N_CTX // DEC_SEQ))

        kr_cache = jnp.pad(cache_mla_krope[:, l].reshape(N_CACHE, MLA_ROPE), ((0, 0), (0, 96)))
        k_c, v_c = _mla_cache_kv(cache_mla_ckv[:, l].reshape(N_CACHE, MLA_KV_RANK), kr_cache,
                                 wk, e_mat, wv)
        om = (_mla_attn(q_m, k_m, v_m, k_c, v_c, latent=False),
              _mla_attn(q_m, k_m, v_m, k_c, v_c, latent=True))

        ck = cache_swa_k[:, l].reshape(DEC_BATCH, PAST_LEN, 128)
        cv = cache_swa_v[:, l].reshape(DEC_BATCH, PAST_LEN, 128)
        osw = (_swa_attn(swa_sink[l], sq, sk, sv, ck, cv, latent=False),
               _swa_attn(swa_sink[l], sq, sk, sv, ck, cv, latent=True))

        x1, h2, eidx, epos, ew, counts = _stage_e(
            l, xc, xl, modt, ng[1:2], ng[2:3], fn + om + osw, h1, w_gate_b, b_gate[l][None, :],
            w_fnet_b, w_mla_o_b, w_swa_o_b, w_out_b,
            router_w[l].T.astype(BF16), router_bias[l][:, None], tri)
        slot, tile_expert, tile_chunks, tile_slot, tile_next = _moe_dispatch_plan(eidx, epos, counts[:, 0])
        xs = _sc_scatter_rows(h2, slot)
        ys = _experts(l, tile_expert, tile_chunks, tile_slot, tile_next, xs, exp_w1, exp_w3, exp_w2)
        picks = slot[:TOP_K].reshape(1, TOP_K * N_TOK)
        yg = _sc_gather_rows(ys, picks).reshape(2, TOP_K, N_TOK, PACKED)
        xc, xl = _stage_g(l, x1, modt, ng[3:4], yg, ew.T, h2, sw1_b, sw3_b, sw2_b)

    y_p = xc.reshape(BATCH, SEQ, D_MODEL)
    y_s = xl.reshape(DEC_BATCH, DEC_SEQ, D_MODEL)
    return (y_p, y_s, jnp.stack(new_ckv, axis=1), jnp.stack(new_kr, axis=1),
            jnp.stack(new_k, axis=1), jnp.stack(new_v, axis=1))
```

```python
import functools
import math

import numpy as np
import jax
import jax.numpy as jnp
from jax import lax
from jax.experimental import pallas as pl
from jax.experimental.pallas import tpu as pltpu
from jax.experimental.pallas import tpu_sc as plsc

D_MODEL = 1024
BATCH = 16
SEQ = 256
DEPTH = 2
DEC_BATCH = 4
DEC_SEQ = 2048
PAST_LEN = 512
GRID_W = 64
EPS = 1e-6
ROPE_BASE = 10000.0
NEG_INF = -1e30

FNET_GROUPS = 8
FNET_GROUP_DIM = 64
FNET_WIDTH = 512
MLA_HEADS = 8
MLA_Q_RANK = 384
MLA_KV_RANK = 128
MLA_NOPE = 64
MLA_ROPE = 32
MLA_V = 64
MLA_SCALE = (MLA_NOPE + MLA_ROPE) ** -0.5
LOG2E = math.log2(math.e)
SWA_KV_HEADS = 2
SWA_HEAD_DIM = 64
SWA_WINDOW = 128
SWA_SCALE = SWA_HEAD_DIM ** -0.5
N_MOD = 6
N_EXPERTS = 64
N_EXPERT_GROUPS = 8
TOPK_GROUPS = 4
TOP_K = 6
EXPERT_FF = 256
SHARED_FF = 256
ROUTED_SCALE = 2.5

LANES = 128
TM = 256
N_CTX = BATCH * SEQ
N_LAT = DEC_BATCH * DEC_SEQ
N_TOK = N_CTX + N_LAT
N_CACHE = DEC_BATCH * PAST_LEN
TB = 512
NB = N_TOK // TB
NB_CTX = N_CTX // TB
LAT_BLOCKS = DEC_SEQ // TB
MLA_LAT_TQ = 256
MLA_LAT_PAIRS = 4
FNET_ROWS = 1024
SWA_QSUB = 4
TE = 512
S_MAX = N_TOK * TOP_K + N_EXPERTS * TE
NTE = S_MAX // TE
EXPERT_ROWS = 256
VMEM_LIMIT = 56 * 1024 * 1024
PACKED = D_MODEL // 4
SC_ROWS = 128
SC_CORES = 2
SC_SUBCORES = 16

A_F = (0, 512)
A_QD = (512, 896)
A_KV = (896, 1024)
A_SQ = (1024, 1536)
A_SK = (1536, 1664)
A_SV = (1664, 1792)
A_KR = (1792, 1920)
W_IN_WIDE = 1920
TAB_W = 1280

F32 = jnp.float32
BF16 = jnp.bfloat16


def _cparams(n_axes):
    return pltpu.CompilerParams(dimension_semantics=("arbitrary",) * n_axes, vmem_limit_bytes=VMEM_LIMIT)


def _dot(a, b):
    return jnp.dot(a, b, preferred_element_type=F32)


def _dot_nt(a, b):
    return lax.dot_general(a, b, (((1,), (1,)), ((), ())), preferred_element_type=F32)


def _rms_rows(v, g):
    return v * lax.rsqrt(jnp.mean(v * v, axis=-1, keepdims=True) + EPS) * g


def _pack_rows(v):
    half = v.shape[1] // 2
    lo = lax.bitcast_convert_type(v[:, :half].astype(BF16).astype(F32), jnp.int32)
    hi = lax.bitcast_convert_type(v[:, half:].astype(BF16).astype(F32), jnp.int32)
    return jnp.bitwise_or(jnp.bitwise_and(hi, -65536), jnp.bitwise_and(jnp.right_shift(lo, 16), 65535))


def _unpack_rows(w):
    lo = lax.bitcast_convert_type(jnp.left_shift(w, 16), F32)
    hi = lax.bitcast_convert_type(jnp.bitwise_and(w, -65536), F32)
    return jnp.concatenate([lo, hi], axis=1)


def _pack_pair(v):
    half = v.shape[1] // 2
    return _pack_rows(v[:, :half]), _pack_rows(v[:, half:])


def _unpack_pair(a, b):
    return jnp.concatenate([_unpack_rows(a), _unpack_rows(b)], axis=1)


def _const_spec(shape):
    return pl.BlockSpec(shape, lambda *_: (0,) * len(shape))


def _layer_spec(shape, layer):
    return pl.BlockSpec((None,) + shape, lambda *_: (layer,) + (0,) * len(shape))


def _ctx_rows(width):
    return pl.BlockSpec((TB, width), lambda i: (jnp.minimum(i, NB_CTX - 1), 0))


def _lat_rows(width):
    return pl.BlockSpec((TB, width), lambda i: (jnp.maximum(i - NB_CTX, 0), 0))


def _tab_row_block(i):
    return jnp.where(i < NB_CTX, 0, 1 + (i - NB_CTX) % LAT_BLOCKS)


def _mod_kernel(cond_ref, w_ref, b_ref, o_ref):
    c = cond_ref[...]
    a = (c * jax.nn.sigmoid(c)).astype(BF16)
    o_ref[...] = _dot(a, w_ref[...].astype(BF16)) + b_ref[...]


def _modulation(cond8, ada_w, ada_b):
    tn = 512
    nj = N_MOD * D_MODEL // tn
    return pl.pallas_call(
        _mod_kernel,
        grid=(DEPTH, nj),
        in_specs=[
            pl.BlockSpec((8, D_MODEL), lambda l, j: (0, 0)),
            pl.BlockSpec((None, D_MODEL, tn), lambda l, j: (l, 0, j)),
            pl.BlockSpec((None, 1, tn), lambda l, j: (l, 0, j)),
        ],
        out_specs=pl.BlockSpec((None, 8, tn), lambda l, j: (l, 0, j)),
        out_shape=jax.ShapeDtypeStruct((DEPTH, 8, N_MOD * D_MODEL), F32),
        compiler_params=_cparams(2),
        name="modulation",
    )(cond8, ada_w, ada_b.reshape(DEPTH, 1, N_MOD * D_MODEL))


def _half_swap(x, half):
    n = x.shape[1]
    lane = lax.broadcasted_iota(jnp.int32, (1, n), 1)
    return jnp.where((lane & half) == 0, pltpu.roll(x, n - half, 1), pltpu.roll(x, half, 1))


def _mla_expand(rows, cq, ckv, kr, cos32, sin32, wqa_ref, wk_ref, e_ref, wv_ref,
                q_ref, k_ref, v_ref):
    if q_ref is not None:
        lane = lax.broadcasted_iota(jnp.int32, (1, LANES), 1)
        rope_lane = jnp.logical_and(lane >= MLA_NOPE, lane < MLA_NOPE + MLA_ROPE)
        cos_h = jnp.where(rope_lane, cos32, 1.0)
        sin_h = jnp.where(rope_lane, sin32, 0.0)
        for hd in range(MLA_HEADS):
            lo, hi = hd * LANES, (hd + 1) * LANES
            q = _dot(cq, wqa_ref[:, lo:hi])
            q = q * cos_h + _half_swap(q, MLA_ROPE // 4) * sin_h
            q_ref[rows, lo:hi] = (q * (MLA_SCALE * LOG2E)).astype(BF16)
    k_ref[rows, :] = (_dot(ckv, wk_ref[...]) + _dot(kr, e_ref[...])).astype(BF16)
    v_ref[rows, :] = _dot(ckv, wv_ref[...]).astype(BF16)


def _chunks():
    return [pl.ds(r * TM, TM) for r in range(TB // TM)]


def _stage_a_kernel(xc_ref, xl_ref, mod_ref, g_ref, win_ref, qn_ref, kvn_ref, tab_ref,
                    wqa_ref, wk_ref, e_ref, wv_ref,
                    fin_ref, ckv_ref, kr_ref, sq_ref, sk_ref, sv_ref, h_ref,
                    qm_ref, km_ref, vm_ref):
    is_ctx = pl.program_id(0) < NB_CTX
    for rows in _chunks():
        x = jnp.where(is_ctx, xc_ref[rows, :], xl_ref[rows, :])
        h = (_rms_rows(x, g_ref[...]) * (1.0 + mod_ref[:, 1024:2048]) + mod_ref[:, 0:1024]).astype(BF16)
        h_ref[rows, :] = h

        def proj(seg):
            return _dot(h, win_ref[:, seg[0]:seg[1]])

        fin_ref[rows, :] = proj(A_F).astype(BF16)
        cq = _rms_rows(proj(A_QD), qn_ref[...]).astype(BF16)
        ckv = _rms_rows(proj(A_KV), kvn_ref[...])
        ckv_ref[rows, :] = ckv
        cos64 = tab_ref[rows, 0:512]
        sin64 = tab_ref[rows, 512:1024]
        sq = proj(A_SQ)
        sq = sq * cos64 + _half_swap(sq, SWA_HEAD_DIM // 4) * sin64
        sq_ref[rows, :] = (sq * (SWA_SCALE * LOG2E)).astype(BF16)
        sk = proj(A_SK)
        sk_ref[rows, :] = sk * cos64[:, 0:128] + _half_swap(sk, SWA_HEAD_DIM // 4) * sin64[:, 0:128]
        sv_ref[rows, :] = proj(A_SV)
        cos32 = tab_ref[rows, 1024:1152]
        sin32 = tab_ref[rows, 1152:1280]
        kr = proj(A_KR)
        kr = kr * cos32 + _half_swap(kr, MLA_ROPE // 4) * sin32
        kr_ref[rows, :] = kr
        _mla_expand(rows, cq, ckv.astype(BF16), kr.astype(BF16), cos32, sin32,
                    wqa_ref, wk_ref, e_ref, wv_ref, qm_ref, km_ref, vm_ref)


def _stage_a(xc, xl, modt, g0, w_in_wide, q_norm, kv_norm, tab, wqa, wk, e_mat, wv):
    row = lambda w: pl.BlockSpec((TB, w), lambda i: (i, 0))
    every = lambda w, dt: (row(w), jax.ShapeDtypeStruct((N_TOK, w), dt))
    outs = [every(512, BF16), every(128, F32), every(128, F32), every(512, BF16), every(128, F32),
            every(128, F32), every(D_MODEL, BF16), every(1024, BF16), every(1024, BF16), every(512, BF16)]
    return pl.pallas_call(
        _stage_a_kernel,
        grid=(NB,),
        in_specs=[
            _ctx_rows(D_MODEL), _lat_rows(D_MODEL),
            pl.BlockSpec((None, 1, N_MOD * D_MODEL), lambda i: (i, 0, 0)),
            _const_spec((1, D_MODEL)),
            _const_spec((D_MODEL, W_IN_WIDE)),
            _const_spec((1, MLA_Q_RANK)),
            _const_spec((1, MLA_KV_RANK)),
            pl.BlockSpec((TB, TAB_W), lambda i: (_tab_row_block(i), 0)),
            _const_spec((MLA_Q_RANK, 1024)),
            _const_spec((128, 1024)), _const_spec((128, 1024)), _const_spec((128, 512)),
        ],
        out_specs=[s for s, _ in outs],
        out_shape=[s for _, s in outs],
        compiler_params=_cparams(1),
        name="stage_a",
    )(xc, xl, modt, g0, w_in_wide, q_norm, kv_norm, tab, wqa, wk, e_mat, wv)


def _fnet_kernel(t_len, scale, fin_ref, f_ref, bd_ref, o_ref, zz_ref):
    @pl.when(pl.program_id(1) == 0)
    def _():
        z = fin_ref[...]
        zz_ref[0:t_len, :] = _dot(z, bd_ref[:, 0:512]).astype(BF16)
        zz_ref[t_len:2 * t_len, :] = _dot(z, bd_ref[:, 512:1024]).astype(BF16)

    o_ref[...] = (_dot(f_ref[...], zz_ref[...]) * scale).astype(BF16)


def _fnet(fin, fmat, bd, n_batch, t_len, row_block0):
    scale = 1.0 / math.sqrt(t_len * FNET_GROUP_DIM)
    ft = min(t_len, FNET_ROWS)
    return pl.pallas_call(
        functools.partial(_fnet_kernel, t_len, scale),
        grid=(n_batch, t_len // ft),
        in_specs=[
            pl.BlockSpec((t_len, FNET_WIDTH), lambda b, i: (row_block0 + b, 0)),
            pl.BlockSpec((ft, 2 * t_len), lambda b, i: (i, 0)),
            _const_spec((FNET_WIDTH, 2 * FNET_WIDTH)),
        ],
        out_specs=pl.BlockSpec((ft, FNET_WIDTH), lambda b, i: (b * (t_len // ft) + i, 0)),
        out_shape=jax.ShapeDtypeStruct((n_batch * t_len, FNET_WIDTH), BF16),
        scratch_shapes=[pltpu.VMEM((2 * t_len, FNET_WIDTH), BF16)],
        compiler_params=_cparams(2),
        name=f"fnet_{t_len}",
    )(fin, fmat, bd)


def _mla_cache_kernel(ckv_ref, kr_ref, wk_ref, e_ref, wv_ref, k_ref, v_ref):
    _mla_expand(slice(None), None, ckv_ref[...].astype(BF16), kr_ref[...].astype(BF16), None, None,
                None, wk_ref, e_ref, wv_ref, None, k_ref, v_ref)


def _mla_cache_kv(ckv_cache, kr_cache, wk, e_mat, wv):
    row = lambda w: pl.BlockSpec((TM, w), lambda i: (i, 0))
    return pl.pallas_call(
        _mla_cache_kernel,
        grid=(N_CACHE // TM,),
        in_specs=[row(128), row(128),
                  _const_spec((128, 1024)), _const_spec((128, 1024)), _const_spec((128, 512))],
        out_specs=[row(1024), row(512)],
        out_shape=[jax.ShapeDtypeStruct((N_CACHE, 1024), BF16),
                   jax.ShapeDtypeStruct((N_CACHE, 512), BF16)],
        compiler_params=_cparams(1),
        name="mla_cache_kv",
    )(ckv_cache, kr_cache, wk, e_mat, wv)


def _mla_attn_kernel(n_seg, pairs, q_ref, *refs):
    k_refs = refs[0:n_seg]
    v_refs = refs[n_seg:2 * n_seg]
    o_ref = refs[2 * n_seg]
    lane = lax.broadcasted_iota(jnp.int32, (1, LANES), 1)
    low = lane < MLA_V
    for pr in range(pairs):
        outs = []
        for hh in range(2):
            hd = 2 * pr + hh
            q = q_ref[:, hd * LANES:(hd + 1) * LANES]
            ss = [_dot_nt(q, k[:, hd * LANES:(hd + 1) * LANES]) for k in k_refs]
            m = functools.reduce(jnp.maximum, [s.max(axis=-1, keepdims=True) for s in ss])
            keep = low if hh == 0 else jnp.logical_not(low)
            sum_lane = MLA_V if hh == 0 else 0
            po = None
            for s, v_ref in zip(ss, v_refs):
                v = v_ref[:, pr * LANES:(pr + 1) * LANES]
                vm = jnp.where(lane == sum_lane, jnp.ones_like(v), jnp.where(keep, v, jnp.zeros_like(v)))
                t = _dot(jnp.exp2(s - m).astype(BF16), vm)
                po = t if po is None else po + t
            outs.append(po / po[:, sum_lane:sum_lane + 1])
        o_ref[:, pr * LANES:(pr + 1) * LANES] = jnp.where(low, outs[0], outs[1]).astype(BF16)


def _mla_attn(q_all, k_all, v_all, k_cache, v_cache, latent):
    if latent:
        tq, pairs = MLA_LAT_TQ, MLA_LAT_PAIRS
        n_b, n_q = DEC_BATCH, DEC_SEQ // tq
        q0 = N_CTX // tq
        kv_specs = [
            pl.BlockSpec((PAST_LEN, 256 * pairs), lambda b, hp, i: (b, hp)),
            pl.BlockSpec((DEC_SEQ, 256 * pairs), lambda b, hp, i: (N_CTX // DEC_SEQ + b, hp)),
            pl.BlockSpec((PAST_LEN, 128 * pairs), lambda b, hp, i: (b, hp)),
            pl.BlockSpec((DEC_SEQ, 128 * pairs), lambda b, hp, i: (N_CTX // DEC_SEQ + b, hp)),
        ]
        args = (q_all, k_cache, k_all, v_cache, v_all)
        n_seg = 2
    else:
        tq, pairs = SEQ, MLA_HEADS // 2
        n_b, n_q = BATCH, 1
        q0 = 0
        kv_specs = [
            pl.BlockSpec((SEQ, 256 * pairs), lambda b, hp, i: (b, hp)),
            pl.BlockSpec((SEQ, 128 * pairs), lambda b, hp, i: (b, hp)),
        ]
        args = (q_all, k_all, v_all)
        n_seg = 1
    return pl.pallas_call(
        functools.partial(_mla_attn_kernel, n_seg, pairs),
        grid=(n_b, MLA_HEADS // (2 * pairs), n_q),
        in_specs=[pl.BlockSpec((tq, 256 * pairs), lambda b, hp, i: (q0 + b * n_q + i, hp))] + kv_specs,
        out_specs=pl.BlockSpec((tq, 128 * pairs), lambda b, hp, i: (b * n_q + i, hp)),
        out_shape=jax.ShapeDtypeStruct((n_b * n_q * tq, MLA_HEADS * MLA_V), BF16),
        compiler_params=_cparams(3),
        name="mla_attn_lat" if latent else "mla_attn_ctx",
    )(*args)


def _swa_kernel(windowed, n_steps, sink_ref, q_ref, *refs):
    n_seg = 4 if windowed else 1
    k_refs = refs[0:n_seg]
    v_refs = refs[n_seg:2 * n_seg]
    o_ref = refs[2 * n_seg]
    step = pl.program_id(1)
    lane = lax.broadcasted_iota(jnp.int32, (1, LANES), 1)
    low = lane < SWA_HEAD_DIM
    high = jnp.logical_not(low)

    k_all = jnp.concatenate([r[...] for r in k_refs], axis=0)
    v_all = jnp.concatenate([r[...] for r in v_refs], axis=0)
    k_sw = pltpu.roll(k_all, SWA_HEAD_DIM, 1)
    v_sw = pltpu.roll(v_all, SWA_HEAD_DIM, 1)

    if windowed:
        tq = SWA_WINDOW
        qi = lax.broadcasted_iota(jnp.int32, (2 * tq, tq), 0) % tq
        kj = lax.broadcasted_iota(jnp.int32, (2 * tq, tq), 1)
        after = kj >= qi
        before = kj <= qi
        n_sub = q_ref.shape[0] // tq
        bias_lo = [jnp.where(jnp.logical_and(after, step > 0) if sub == 0 else after, 0.0, NEG_INF)
                   for sub in range(n_sub)]
        bias_hi = [jnp.where(jnp.logical_and(before, step < n_steps - 1) if sub == n_sub - 1 else before,
                             0.0, NEG_INF) for sub in range(n_sub)]
    else:
        tq = q_ref.shape[0]
        n_sub = 1
    top_rows = lax.broadcasted_iota(jnp.int32, (2 * tq, 1), 0) < tq

    for g in range(SWA_KV_HEADS):
        kh, vh = [], []
        for half in range(2):
            keep = low if half == 0 else high
            sum_lane = SWA_HEAD_DIM if half == 0 else 0
            straight = (g == half)
            kh.append(jnp.where(keep, k_all if straight else k_sw, 0.0).astype(BF16))
            vh.append(jnp.where(lane == sum_lane, 1.0,
                                jnp.where(keep, v_all if straight else v_sw, 0.0)).astype(BF16))
        for sub in range(n_sub):
            rows = slice(sub * tq, (sub + 1) * tq)
            qs = jnp.concatenate([q_ref[rows, 256 * g:256 * g + 128],
                                  q_ref[rows, 256 * g + 128:256 * g + 256]], axis=0)
            halves = []
            for half in range(2):
                sum_lane = SWA_HEAD_DIM if half == 0 else 0
                ks, vs = kh[half], vh[half]
                if windowed:
                    w0 = PAST_LEN + sub * tq
                    if sub == 0:
                        ks, vs = ks[0:w0 + 3 * tq], vs[0:w0 + 3 * tq]
                    else:
                        ks = jnp.concatenate([ks[0:PAST_LEN], ks[w0:w0 + 3 * tq]], axis=0)
                        vs = jnp.concatenate([vs[0:PAST_LEN], vs[w0:w0 + 3 * tq]], axis=0)
                s = _dot_nt(qs, ks)
                if windowed:
                    c0, c1, c2 = PAST_LEN, PAST_LEN + tq, PAST_LEN + 2 * tq
                    s = jnp.concatenate([s[:, :c0], s[:, c0:c1] + bias_lo[sub], s[:, c1:c2],
                                         s[:, c2:] + bias_hi[sub]], axis=1)
                sink = jnp.where(top_rows, sink_ref[4 * g + half], sink_ref[4 * g + 2 + half]) * LOG2E
                m = jnp.maximum(s.max(axis=-1, keepdims=True), sink)
                po = _dot(jnp.exp2(s - m).astype(BF16), vs)
                halves.append(po / (po[:, sum_lane:sum_lane + 1] + jnp.exp2(sink - m)))
            out = jnp.where(low, halves[0], halves[1])
            o_ref[rows, 256 * g:256 * g + 128] = out[0:tq].astype(BF16)
            o_ref[rows, 256 * g + 128:256 * g + 256] = out[tq:2 * tq].astype(BF16)


def _swa_attn(sink, sq, sk, sv, cache_k, cache_v, latent):
    smem = pl.BlockSpec(memory_space=pltpu.SMEM)
    if latent:
        tq = SWA_QSUB * SWA_WINDOW
        n_b, n_qb = DEC_BATCH, DEC_SEQ // tq
        base = N_CTX // tq
        last = DEC_SEQ // SWA_WINDOW - 1

        def prev(b, i):
            return (SWA_QSUB * (base + b * n_qb) + jnp.maximum(SWA_QSUB * i - 1, 0), 0)

        def cur(b, i):
            return (base + b * n_qb + i, 0)

        def nxt(b, i):
            return (SWA_QSUB * (base + b * n_qb) + jnp.minimum(SWA_QSUB * (i + 1), last), 0)

        cache = pl.BlockSpec((None, PAST_LEN, 128), lambda b, i: (b, 0, 0))
        edge = lambda f: pl.BlockSpec((SWA_WINDOW, 128), f)
        kv_specs = [cache, edge(prev), pl.BlockSpec((tq, 128), cur), edge(nxt)] * 2
        args = (cache_k, sk, sk, sk, cache_v, sv, sv, sv)
        q_spec = pl.BlockSpec((tq, 512), cur)
        o_spec = pl.BlockSpec((tq, 512), lambda b, i: (b * n_qb + i, 0))
    else:
        tq = SEQ
        n_b, n_qb = BATCH, 1
        blk = pl.BlockSpec((tq, 128), lambda b, i: (b, 0))
        kv_specs = [blk, blk]
        args = (sk, sv)
        q_spec = pl.BlockSpec((tq, 512), lambda b, i: (b, 0))
        o_spec = q_spec
    return pl.pallas_call(
        functools.partial(_swa_kernel, latent, n_qb),
        grid=(n_b, n_qb),
        in_specs=[smem, q_spec] + kv_specs,
        out_specs=o_spec,
        out_shape=jax.ShapeDtypeStruct((n_b * n_qb * tq, 512), BF16),
        compiler_params=_cparams(2),
        name="swa_lat" if latent else "swa_ctx",
    )(sink, sq, *args)


def _route(h, rwt_ref, rb_ref, tri_ref, carry):
    n = h.shape[0]
    gsz = N_EXPERTS // N_EXPERT_GROUPS
    scores = jax.nn.sigmoid(_dot_nt(rwt_ref[...], h))
    biased = scores + rb_ref[...]
    mem = lax.broadcasted_iota(jnp.int32, (gsz, n), 0).astype(F32)
    gs_rows = []
    for g in range(N_EXPERT_GROUPS):
        bg = biased[g * gsz:(g + 1) * gsz, :]
        m1 = bg.max(axis=0, keepdims=True)
        first = jnp.min(jnp.where(bg == m1, mem, float(gsz)), axis=0, keepdims=True)
        m2 = jnp.where(mem == first, -jnp.inf, bg).max(axis=0, keepdims=True)
        gs_rows.append(m1 + m2)
    gs = jnp.concatenate(gs_rows, axis=0)
    gid = lax.broadcasted_iota(jnp.int32, gs.shape, 0).astype(F32)
    gsel = jnp.zeros(gs.shape, F32)
    for _ in range(TOPK_GROUPS):
        mx = gs.max(axis=0, keepdims=True)
        pick = gid == jnp.min(jnp.where(gs == mx, gid, float(N_EXPERT_GROUPS)), axis=0, keepdims=True)
        gsel = jnp.where(pick, 1.0, gsel)
        gs = jnp.where(pick, -jnp.inf, gs)
    emask = jnp.concatenate(
        [jnp.broadcast_to(gsel[g:g + 1, :], (gsz, n)) for g in range(N_EXPERT_GROUPS)], axis=0)
    cand = jnp.where(emask > 0.5, biased, NEG_INF)
    eid = lax.broadcasted_iota(jnp.int32, cand.shape, 0).astype(F32)
    picks = []
    self32 = jnp.zeros(cand.shape, F32)
    for _ in range(TOP_K):
        mx = cand.max(axis=0, keepdims=True)
        pick = eid == jnp.min(jnp.where(cand == mx, eid, float(N_EXPERTS)), axis=0, keepdims=True)
        picks.append(pick)
        self32 = jnp.where(pick, 1.0, self32)
        cand = jnp.where(pick, -jnp.inf, cand)
    pos = _dot(self32.astype(BF16), tri_ref[...]) + carry
    sel_scores = [jnp.sum(jnp.where(p, scores, 0.0), axis=0, keepdims=True) for p in picks]
    wsum = functools.reduce(lambda a, b: a + b, sel_scores)
    zero_f = jnp.zeros((2, n), F32)
    eidx = [jnp.sum(jnp.where(p, eid, 0.0), axis=0, keepdims=True) for p in picks]
    epos = [jnp.sum(jnp.where(p, pos, 0.0), axis=0, keepdims=True) for p in picks]
    ew = [s / wsum * ROUTED_SCALE for s in sel_scores]
    return (jnp.concatenate(eidx + [zero_f], axis=0).astype(jnp.int32),
            jnp.concatenate(epos + [zero_f], axis=0).astype(jnp.int32),
            jnp.concatenate(ew + [zero_f], axis=0),
            carry + jnp.sum(self32, axis=1, keepdims=True))


def _stage_e_kernel(xc_ref, xl_ref, mod_ref, g1_ref, g2_ref, fnc_ref, fnl_ref, omc_ref, oml_ref, osc_ref, osl_ref,
                    h_ref, wg_ref, bg_ref, wf_ref, wm_ref, ws_ref, wo_ref, rwt_ref, rb_ref, tri_ref,
                    x1_ref, h2_ref, eidx_ref, epos_ref, ew_ref, cnt_ref, carry_ref, h2b_ref):
    @pl.when(pl.program_id(0) == 0)
    def _():
        carry_ref[...] = jnp.zeros_like(carry_ref)

    is_ctx = pl.program_id(0) < NB_CTX
    carry = carry_ref[...]
    for rows in _chunks():
        fn = jnp.where(is_ctx, fnc_ref[rows, :], fnl_ref[rows, :])
        om = jnp.where(is_ctx, omc_ref[rows, :], oml_ref[rows, :])
        osw = jnp.where(is_ctx, osc_ref[rows, :], osl_ref[rows, :])
        h = h_ref[rows, :]

        def gate(c):
            lo, hi = c * D_MODEL, (c + 1) * D_MODEL
            return jax.nn.sigmoid(_dot(h, wg_ref[:, lo:hi]) + bg_ref[:, lo:hi])

        merged = (gate(0) * _dot(fn, wf_ref[...]) + gate(1) * _dot(om, wm_ref[...])
                  + gate(2) * _dot(osw, ws_ref[...]))
        mix = _dot(merged.astype(BF16), wo_ref[...])
        x = jnp.where(is_ctx, xc_ref[rows, :], xl_ref[rows, :])
        x1 = x + mod_ref[:, 2048:3072] * _rms_rows(mix, g1_ref[...])
        x1_ref[rows, :] = x1
        h2 = _rms_rows(x1, g2_ref[...]) * (1.0 + mod_ref[:, 4096:5120]) + mod_ref[:, 3072:4096]
        h2_ref[0, rows, :], h2_ref[1, rows, :] = _pack_pair(h2)
        h2b_ref[rows, :] = h2.astype(BF16)
    eidx_ref[...], epos_ref[...], ew_ref[...], carry = _route(h2b_ref[...], rwt_ref, rb_ref, tri_ref, carry)
    carry_ref[...] = carry
    cnt_ref[...] = jnp.broadcast_to(carry, cnt_ref.shape).astype(jnp.int32)


def _stage_e(layer, xc, xl, modt, g1, g2, mixed, h, w_gate, b_gate, wf, wm, ws, wo, rwt, rbias, tri):
    row = lambda w: pl.BlockSpec((TB, w), lambda i: (i, 0))
    ctx, lat = _ctx_rows(512), _lat_rows(512)
    col = lambda dt: (pl.BlockSpec((8, TB), lambda i: (0, i)), jax.ShapeDtypeStruct((8, N_TOK), dt))
    picks = [col(jnp.int32), col(jnp.int32), col(F32)]
    return pl.pallas_call(
        _stage_e_kernel,
        grid=(NB,),
        in_specs=[
            _ctx_rows(D_MODEL), _lat_rows(D_MODEL),
            pl.BlockSpec((None, 1, N_MOD * D_MODEL), lambda i: (i, 0, 0)),
            _const_spec((1, D_MODEL)), _const_spec((1, D_MODEL)),
            ctx, lat, ctx, lat, ctx, lat, row(D_MODEL),
            _layer_spec((D_MODEL, 3 * D_MODEL), layer), _const_spec((1, 3 * D_MODEL)),
            _layer_spec((512, D_MODEL), layer), _layer_spec((512, D_MODEL), layer),
            _layer_spec((512, D_MODEL), layer), _layer_spec((D_MODEL, D_MODEL), layer),
            _const_spec((N_EXPERTS, D_MODEL)), _const_spec((N_EXPERTS, 1)), _const_spec((TB, TB)),
        ],
        out_specs=[row(D_MODEL), pl.BlockSpec((2, TB, PACKED), lambda i: (0, i, 0))] + [s for s, _ in picks]
        + [_const_spec((N_EXPERTS, LANES))],
        out_shape=[jax.ShapeDtypeStruct((N_TOK, D_MODEL), F32),
                   jax.ShapeDtypeStruct((2, N_TOK, PACKED), jnp.int32)] + [s for _, s in picks]
        + [jax.ShapeDtypeStruct((N_EXPERTS, LANES), jnp.int32)],
        scratch_shapes=[pltpu.VMEM((N_EXPERTS, 1), F32), pltpu.VMEM((TB, D_MODEL), BF16)],
        compiler_params=_cparams(1),
        name="stage_e",
    )(xc, xl, modt, g1, g2, *mixed, h, w_gate, b_gate, wf, wm, ws, wo, rwt, rbias, tri)


def _expert_kernel(layer, te_ref, tv_ref, par_ref, nxt_ref, x_ref, w1_hbm, w3_hbm, w2_hbm, o_ref,
                   w1f_ref, w3f_ref, w2f_ref, w1b_ref, w3b_ref, w2b_ref, sem):
    j = pl.program_id(0)
    valid = tv_ref[j] > 0
    first = jnp.logical_and(valid, jnp.logical_or(j == 0, te_ref[j] != te_ref[jnp.maximum(j - 1, 0)]))

    def copies(expert, slot):
        return [pltpu.make_async_copy(w_hbm.at[layer, expert], buf.at[slot], sem.at[slot, i])
                for i, (w_hbm, buf) in enumerate(((w1_hbm, w1f_ref), (w3_hbm, w3f_ref), (w2_hbm, w2f_ref)))]

    @pl.when(jnp.logical_and(valid, j == 0))
    def _():
        for cp in copies(te_ref[0], par_ref[0]):
            cp.start()

    @pl.when(first)
    def _():
        slot = par_ref[j]
        for cp in copies(te_ref[j], slot):
            cp.wait()

        @pl.when(nxt_ref[j] >= 0)
        def _():
            for cp in copies(nxt_ref[j], 1 - slot):
                cp.start()

        w1b_ref[...] = w1f_ref[slot].astype(BF16)
        w3b_ref[...] = w3f_ref[slot].astype(BF16)
        w2b_ref[...] = w2f_ref[slot].astype(BF16)

    def run(n_chunks):
        for r in range(n_chunks):
            rows = pl.ds(r * EXPERT_ROWS, EXPERT_ROWS)
            x = _unpack_pair(x_ref[0, rows, :], x_ref[1, rows, :]).astype(BF16)
            hg = _dot(x, w1b_ref[...])
            hu = _dot(x, w3b_ref[...])
            act = (jax.nn.silu(hg) * hu).astype(BF16)
            o_ref[0, rows, :], o_ref[1, rows, :] = _pack_pair(_dot(act, w2b_ref[...]))

    for n_chunks in range(1, TE // EXPERT_ROWS + 1):
        pl.when(tv_ref[j] == n_chunks)(functools.partial(run, n_chunks))


def _experts(layer, tile_expert, tile_chunks, tile_slot, tile_next, xs, w1, w3, w2):
    slot_rows = pl.BlockSpec((2, TE, PACKED), lambda j, te, tv, par, nxt: (0, j, 0))
    anywhere = pl.BlockSpec(memory_space=pl.ANY)
    grid_spec = pltpu.PrefetchScalarGridSpec(
        num_scalar_prefetch=4,
        grid=(NTE,),
        in_specs=[slot_rows, anywhere, anywhere, anywhere],
        out_specs=slot_rows,
        scratch_shapes=[pltpu.VMEM((2, D_MODEL, EXPERT_FF), F32), pltpu.VMEM((2, D_MODEL, EXPERT_FF), F32),
                        pltpu.VMEM((2, EXPERT_FF, D_MODEL), F32),
                        pltpu.VMEM((D_MODEL, EXPERT_FF), BF16), pltpu.VMEM((D_MODEL, EXPERT_FF), BF16),
                        pltpu.VMEM((EXPERT_FF, D_MODEL), BF16),
                        pltpu.SemaphoreType.DMA((2, 3))],
    )
    return pl.pallas_call(
        functools.partial(_expert_kernel, layer),
        grid_spec=grid_spec,
        out_shape=jax.ShapeDtypeStruct((2, S_MAX, PACKED), jnp.int32),
        compiler_params=_cparams(1),
        name="experts",
    )(tile_expert, tile_chunks, tile_slot, tile_next, xs, w1, w3, w2)


def _sc_mesh():
    return plsc.VectorSubcoreMesh(core_axis_name="c", subcore_axis_name="s",
                                  num_cores=SC_CORES, num_subcores=SC_SUBCORES)


def _sc_scatter_rows(rows, slot8):
    @functools.partial(pl.kernel, mesh=_sc_mesh(), scratch_types=[pltpu.SemaphoreType.DMA],
                       out_type=jax.ShapeDtypeStruct((2, S_MAX, PACKED), jnp.int32))
    def scatter(x_hbm, i_hbm, o_hbm, sem):
        for h in range(2):
            dst = o_hbm.at[h]

            def body(x_vmem, i_vmem, dst=dst):
                copies = [pltpu.async_copy(x_vmem, dst.at[i_vmem.at[k]], sem) for k in range(TOP_K)]
                for cp in copies:
                    cp.wait()

            pltpu.emit_pipeline(
                body,
                grid=(N_TOK // SC_ROWS,),
                in_specs=[pl.BlockSpec((SC_ROWS, PACKED), lambda i: (i, 0)),
                          pl.BlockSpec((8, SC_ROWS), lambda i: (0, i))],
                out_specs=[],
                core_axis_name=("c", "s"),
                dimension_semantics=(pltpu.PARALLEL,),
            )(x_hbm.at[h], i_hbm)

    return scatter(rows, slot8)


def _sc_gather_rows(table, idx):
    n = idx.shape[1]

    @functools.partial(pl.kernel, mesh=_sc_mesh(), scratch_types=[],
                       out_type=jax.ShapeDtypeStruct((2, n, PACKED), jnp.int32))
    def gather(t_hbm, i_hbm, o_hbm):
        for h in range(2):
            src = t_hbm.at[h]

            def body(i_vmem, o_vmem, src=src):
                pltpu.sync_copy(src.at[i_vmem.at[0]], o_vmem)

            pltpu.emit_pipeline(
                body,
                grid=(n // SC_ROWS,),
                in_specs=[pl.BlockSpec((1, SC_ROWS), lambda i: (0, i))],
                out_specs=[pl.BlockSpec((SC_ROWS, PACKED), lambda i: (i, 0))],
                core_axis_name=("c", "s"),
                dimension_semantics=(pltpu.PARALLEL,),
            )(i_hbm, o_hbm.at[h])

    return gather(table, idx)


def _stage_g_kernel(x1_ref, mod_ref, g3_ref, yg_ref, ew_ref, h2_ref,
                    s1_ref, s3_ref, s2_ref, oc_ref, ol_ref, out_ref):
    is_ctx = pl.program_id(0) < NB_CTX
    for rows in _chunks():
        h = _unpack_pair(h2_ref[0, rows, :], h2_ref[1, rows, :]).astype(BF16)
        act = jax.nn.silu(_dot(h, s1_ref[...])) * _dot(h, s3_ref[...])
        y = _dot(act.astype(BF16), s2_ref[...])
        for k in range(TOP_K):
            y = y + ew_ref[rows, k:k + 1] * _unpack_pair(yg_ref[0, k, rows, :], yg_ref[1, k, rows, :])
        out_ref[rows, :] = x1_ref[rows, :] + mod_ref[:, 5120:6144] * _rms_rows(y, g3_ref[...])

    @pl.when(is_ctx)
    def _():
        oc_ref[...] = out_ref[...]

    @pl.when(jnp.logical_not(is_ctx))
    def _():
        ol_ref[...] = out_ref[...]


def _stage_g(layer, x1, modt, g3, yg, ew_rows, h2, s1, s3, s2):
    row = lambda w: pl.BlockSpec((TB, w), lambda i: (i, 0))
    picked = pl.BlockSpec((2, TOP_K, TB, PACKED), lambda i: (0, 0, i, 0))
    return pl.pallas_call(
        _stage_g_kernel,
        grid=(NB,),
        in_specs=[row(D_MODEL), pl.BlockSpec((None, 1, N_MOD * D_MODEL), lambda i: (i, 0, 0)),
                  _const_spec((1, D_MODEL)), picked, row(8),
                  pl.BlockSpec((2, TB, PACKED), lambda i: (0, i, 0)),
                  _layer_spec((D_MODEL, SHARED_FF), layer), _layer_spec((D_MODEL, SHARED_FF), layer),
                  _layer_spec((SHARED_FF, D_MODEL), layer)],
        out_specs=[_ctx_rows(D_MODEL), _lat_rows(D_MODEL)],
        out_shape=[jax.ShapeDtypeStruct((N_CTX, D_MODEL), F32),
                   jax.ShapeDtypeStruct((N_LAT, D_MODEL), F32)],
        scratch_shapes=[pltpu.VMEM((TB, D_MODEL), F32)],
        compiler_params=_cparams(1),
        name="stage_g",
    )(x1, modt, g3, yg, ew_rows, h2, s1, s3, s2)


def _rope_tables():
    t = np.arange(DEC_SEQ)
    pos = np.stack([(t // GRID_W), (t % GRID_W)], axis=-1).astype(np.float32)

    def table(r):
        n_freq = r // 4
        inv = np.float32(ROPE_BASE) ** (-np.arange(n_freq, dtype=np.float32) / np.float32(n_freq))
        ang = pos[:, :, None] * inv.astype(np.float32)
        cos = np.cos(ang)
        sin = np.sin(ang)
        cos_t = np.stack([cos, cos], axis=2).reshape(DEC_SEQ, r)
        sin_t = np.stack([-sin, sin], axis=2).reshape(DEC_SEQ, r)
        return cos_t, sin_t

    c64, s64 = table(SWA_HEAD_DIM)
    c32, s32 = table(MLA_ROPE)
    lat = np.concatenate([np.tile(c64, (1, 8)), np.tile(s64, (1, 8)),
                          np.tile(c32, (1, 4)), np.tile(s32, (1, 4))], axis=1)
    ident = np.concatenate([np.ones((TB, 512)), np.zeros((TB, 512)),
                            np.ones((TB, 128)), np.zeros((TB, 128))], axis=1)
    return jnp.asarray(np.concatenate([ident, lat], axis=0).astype(np.float32))


def _dft_pair(n):
    k = np.arange(n, dtype=np.int64)
    ang = ((k[:, None] * k[None, :]) % n).astype(np.float64) * (2.0 * math.pi / n)
    return np.cos(ang), np.sin(ang)


def _fnet_tables():
    c64, s64 = _dft_pair(FNET_GROUP_DIM)
    eye = np.eye(FNET_GROUPS)
    bd = np.concatenate([np.kron(eye, c64), np.kron(eye, s64)], axis=1)
    mats = []
    for t_len in (SEQ, DEC_SEQ):
        c, s = _dft_pair(t_len)
        mats.append(np.concatenate([c, -s], axis=1))
    return tuple(jnp.asarray(m.astype(np.float32).astype(BF16)) for m in (bd, mats[0], mats[1]))


def _layer_weights(l, w_in, w_uq, w_ukv):
    w = w_in[l]
    wide = jnp.concatenate([w[:, 0:1024], w[:, 1056:1824], w[:, 1024:1056],
                            jnp.zeros((D_MODEL, 96), F32)], axis=1).astype(BF16)

    uq = w_uq[l].reshape(MLA_Q_RANK, MLA_HEADS, MLA_NOPE + MLA_ROPE)
    z32 = jnp.zeros((MLA_Q_RANK, MLA_HEADS, 32), F32)
    wqa = jnp.concatenate([uq, z32], axis=2).reshape(MLA_Q_RANK, 1024).astype(BF16)
    ukv = w_ukv[l].reshape(MLA_KV_RANK, MLA_HEADS, MLA_NOPE + MLA_V)
    wk = jnp.concatenate([ukv[:, :, :MLA_NOPE], jnp.zeros((MLA_KV_RANK, MLA_HEADS, 64), F32)],
                         axis=2).reshape(MLA_KV_RANK, 1024).astype(BF16)
    wv = ukv[:, :, MLA_NOPE:].reshape(MLA_KV_RANK, 512).astype(BF16)
    return wide, wqa, wk, wv


def _rope_placement():
    e = np.zeros((128, 1024), np.float32)
    for hd in range(MLA_HEADS):
        for i in range(MLA_ROPE):
            e[i, hd * 128 + MLA_NOPE + i] = 1.0
    return jnp.asarray(e, BF16)


def _moe_dispatch_plan(eidx, epos, counts):
    padded = ((counts + TE - 1) // TE) * TE
    ends = jnp.cumsum(padded)
    offs = ends - padded
    ids = jnp.arange(N_EXPERTS, dtype=jnp.int32)
    picked_off = jnp.sum(jnp.where(eidx[:, :, None] == ids, offs, 0), axis=-1)
    slot = picked_off + epos
    starts = jnp.arange(NTE, dtype=jnp.int32) * TE
    tile_expert = jnp.sum((ends[None, :] <= starts[:, None]).astype(jnp.int32), axis=1)
    tile_expert = jnp.minimum(tile_expert, N_EXPERTS - 1)
    pick = tile_expert[:, None] == ids[None, :]
    last_real = jnp.sum(jnp.where(pick, (offs + counts)[None, :], 0), axis=1)
    n_real = jnp.clip(last_real - starts, 0, TE)
    n_real = jnp.where(starts < ends[-1], n_real, 0)
    tile_chunks = ((n_real + EXPERT_ROWS - 1) // EXPERT_ROWS).astype(jnp.int32)
    used = counts > 0
    rank = jnp.cumsum(used.astype(jnp.int32)) - 1
    tile_slot = jnp.sum(jnp.where(pick, rank[None, :], 0), axis=1) % 2
    later_used = jnp.logical_and(ids[None, :] > ids[:, None], used[None, :])
    next_used = jnp.min(jnp.where(later_used, ids[None, :], N_EXPERTS), axis=1)
    next_used = jnp.where(next_used < N_EXPERTS, next_used, -1)
    tile_next = jnp.sum(jnp.where(pick, next_used[None, :], 0), axis=1)
    return slot, tile_expert, tile_chunks, tile_slot.astype(jnp.int32), tile_next.astype(jnp.int32)


def kernel(x_prompt, x_sample, cache_mla_ckv, cache_mla_krope, cache_swa_k, cache_swa_v, c, c_ctx,
           ada_w, ada_b, norm_g, w_in, q_norm, kv_norm, w_fnet, w_uq, w_ukv, w_mla_o, swa_sink,
           w_swa_o, w_gate, b_gate, w_out, router_w, router_bias, exp_w1, exp_w3, exp_w2,
           shared_w1, shared_w3, shared_w2):
    xc = x_prompt.reshape(N_CTX, D_MODEL)
    xl = x_sample.reshape(N_LAT, D_MODEL)

    cond8 = jnp.concatenate([c_ctx[None, :], c, jnp.zeros((3, D_MODEL), F32)], axis=0)
    mod = _modulation(cond8, ada_w, ada_b)
    tile_cond = np.concatenate([np.zeros(NB_CTX, np.int32),
                                1 + np.arange(NB - NB_CTX, dtype=np.int32) // LAT_BLOCKS])

    tab = _rope_tables()
    bd, f_ctx, f_lat = _fnet_tables()
    e_mat = _rope_placement()
    tri = jnp.asarray(np.triu(np.ones((TB, TB), np.float32), 1), BF16)
    w_gate_b, w_fnet_b, w_mla_o_b, w_swa_o_b, w_out_b, sw1_b, sw3_b, sw2_b = (
        w.astype(BF16) for w in (w_gate, w_fnet, w_mla_o, w_swa_o, w_out, shared_w1, shared_w3, shared_w2))

    def prepare(l, w_in_, w_uq_, w_ukv_, ckv_, krope_, router_w_):
        wide, wqa, wk, wv = _layer_weights(l, w_in_, w_uq_, w_ukv_)
        kr_cache = jnp.pad(krope_[:, l].reshape(N_CACHE, MLA_ROPE), ((0, 0), (0, 96)))
        k_c, v_c = _mla_cache_kv(ckv_[:, l].reshape(N_CACHE, MLA_KV_RANK), kr_cache, wk, e_mat, wv)
        return wide, wqa, wk, wv, k_c, v_c, router_w_[l].T.astype(BF16)

    raw = (w_in, w_uq, w_ukv, cache_mla_ckv, cache_mla_krope, router_w)
    ready = prepare(0, *raw)

    new_ckv, new_kr, new_k, new_v = [], [], [], []
    for l in range(DEPTH):
        modt = mod[l][tile_cond][:, None, :]
        wide, wqa, wk, wv, k_c, v_c, rwt = ready
        ng = norm_g[l]

        fin, ckv, kr, sq, sk, sv, h1, q_m, k_m, v_m = _stage_a(
            xc, xl, modt, ng[0:1], wide, q_norm[l][None, :], kv_norm[l][None, :], tab, wqa, wk, e_mat, wv)

        new_ckv.append(ckv[:N_CTX].reshape(BATCH, SEQ, MLA_KV_RANK))
        new_kr.append(kr[:N_CTX, :MLA_ROPE].reshape(BATCH, SEQ, MLA_ROPE))
        new_k.append(sk[:N_CTX].reshape(BATCH, SEQ, SWA_KV_HEADS, SWA_HEAD_DIM))
        new_v.append(sv[:N_CTX].reshape(BATCH, SEQ, SWA_KV_HEADS, SWA_HEAD_DIM))

        fn = (_fnet(fin, f_ctx, bd, BATCH, SEQ, 0),
              _fnet(fin, f_lat, bd, DEC_BATCH, DEC_SEQ, N_CTX // DEC_SEQ))

        om = (_mla_attn(q_m, k_m, v_m, k_c, v_c, latent=False),
              _mla_attn(q_m, k_m, v_m, k_c, v_c, latent=True))

        ck = cache_swa_k[:, l].reshape(DEC_BATCH, PAST_LEN, 128)
        cv = cache_swa_v[:, l].reshape(DEC_BATCH, PAST_LEN, 128)
        osw = (_swa_attn(swa_sink[l], sq, sk, sv, ck, cv, latent=False),
               _swa_attn(swa_sink[l], sq, sk, sv, ck, cv, latent=True))

        x1, h2, eidx, epos, ew, counts = _stage_e(
            l, xc, xl, modt, ng[1:2], ng[2:3], fn + om + osw, h1, w_gate_b, b_gate[l][None, :],
            w_fnet_b, w_mla_o_b, w_swa_o_b, w_out_b, rwt, router_bias[l][:, None], tri)
        if l + 1 < DEPTH:
            raw_next, h2 = lax.optimization_barrier((raw, h2))
            ready = prepare(l + 1, *raw_next)
        slot, tile_expert, tile_chunks, tile_slot, tile_next = _moe_dispatch_plan(eidx, epos, counts[:, 0])
        xs = _sc_scatter_rows(h2, slot)
        ys = _experts(l, tile_expert, tile_chunks, tile_slot, tile_next, xs, exp_w1, exp_w3, exp_w2)
        picks = slot[:TOP_K].reshape(1, TOP_K * N_TOK)
        yg = _sc_gather_rows(ys, picks).reshape(2, TOP_K, N_TOK, PACKED)
        xc, xl = _stage_g(l, x1, modt, ng[3:4], yg, ew.T, h2, sw1_b, sw3_b, sw2_b)

    y_p = xc.reshape(BATCH, SEQ, D_MODEL)
    y_s = xl.reshape(DEC_BATCH, DEC_SEQ, D_MODEL)
    return (y_p, y_s, jnp.stack(new_ckv, axis=1), jnp.stack(new_kr, axis=1),
            jnp.stack(new_k, axis=1), jnp.stack(new_v, axis=1))
```

```python
import functools
import math

import numpy as np
import jax
import jax.numpy as jnp
from jax import lax
from jax.experimental import pallas as pl
from jax.experimental.pallas import tpu as pltpu
from jax.experimental.pallas import tpu_sc as plsc

D_MODEL = 1024
BATCH = 16
SEQ = 256
DEPTH = 2
DEC_BATCH = 4
DEC_SEQ = 2048
PAST_LEN = 512
GRID_W = 64
EPS = 1e-6
ROPE_BASE = 10000.0
NEG_INF = -1e30

FNET_GROUPS = 8
FNET_GROUP_DIM = 64
FNET_WIDTH = 512
MLA_HEADS = 8
MLA_Q_RANK = 384
MLA_KV_RANK = 128
MLA_NOPE = 64
MLA_ROPE = 32
MLA_V = 64
MLA_SCALE = (MLA_NOPE + MLA_ROPE) ** -0.5
LOG2E = math.log2(math.e)
SWA_KV_HEADS = 2
SWA_HEAD_DIM = 64
SWA_WINDOW = 128
SWA_SCALE = SWA_HEAD_DIM ** -0.5
N_MOD = 6
N_EXPERTS = 64
N_EXPERT_GROUPS = 8
TOPK_GROUPS = 4
TOP_K = 6
EXPERT_FF = 256
SHARED_FF = 256
ROUTED_SCALE = 2.5

LANES = 128
TM = 256
N_CTX = BATCH * SEQ
N_LAT = DEC_BATCH * DEC_SEQ
N_TOK = N_CTX + N_LAT
N_CACHE = DEC_BATCH * PAST_LEN
TB = 512
NB = N_TOK // TB
NB_CTX = N_CTX // TB
LAT_BLOCKS = DEC_SEQ // TB
MLA_LAT_TQ = 256
MLA_LAT_PAIRS = 4
FNET_ROWS = 1024
SWA_QSUB = 4
TE = 512
S_MAX = N_TOK * TOP_K + N_EXPERTS * TE
NTE = S_MAX // TE
EXPERT_ROWS = 256
VMEM_LIMIT = 56 * 1024 * 1024
PACKED = D_MODEL // 4
SC_ROWS = 128
SC_CORES = 2
SC_SUBCORES = 16

A_F = (0, 512)
A_QD = (512, 896)
A_KV = (896, 1024)
A_SQ = (1024, 1536)
A_SK = (1536, 1664)
A_SV = (1664, 1792)
A_KR = (1792, 1920)
W_IN_WIDE = 1920
TAB_W = 1280

F32 = jnp.float32
BF16 = jnp.bfloat16


def _cparams(n_axes):
    return pltpu.CompilerParams(dimension_semantics=("arbitrary",) * n_axes, vmem_limit_bytes=VMEM_LIMIT)


def _dot(a, b):
    return jnp.dot(a, b, preferred_element_type=F32)


def _dot_nt(a, b):
    return lax.dot_general(a, b, (((1,), (1,)), ((), ())), preferred_element_type=F32)


def _rms_rows(v, g):
    return v * lax.rsqrt(jnp.mean(v * v, axis=-1, keepdims=True) + EPS) * g


def _pack_rows(v):
    half = v.shape[1] // 2
    lo = lax.bitcast_convert_type(v[:, :half].astype(BF16).astype(F32), jnp.int32)
    hi = lax.bitcast_convert_type(v[:, half:].astype(BF16).astype(F32), jnp.int32)
    return jnp.bitwise_or(jnp.bitwise_and(hi, -65536), jnp.bitwise_and(jnp.right_shift(lo, 16), 65535))


def _unpack_rows(w):
    lo = lax.bitcast_convert_type(jnp.left_shift(w, 16), F32)
    hi = lax.bitcast_convert_type(jnp.bitwise_and(w, -65536), F32)
    return jnp.concatenate([lo, hi], axis=1)


def _pack_pair(v):
    half = v.shape[1] // 2
    return _pack_rows(v[:, :half]), _pack_rows(v[:, half:])


def _unpack_pair(a, b):
    return jnp.concatenate([_unpack_rows(a), _unpack_rows(b)], axis=1)


def _const_spec(shape):
    return pl.BlockSpec(shape, lambda *_: (0,) * len(shape))


def _layer_spec(shape, layer):
    return pl.BlockSpec((None,) + shape, lambda *_: (layer,) + (0,) * len(shape))


def _ctx_rows(width):
    return pl.BlockSpec((TB, width), lambda i: (jnp.minimum(i, NB_CTX - 1), 0))


def _lat_rows(width):
    return pl.BlockSpec((TB, width), lambda i: (jnp.maximum(i - NB_CTX, 0), 0))


def _tab_row_block(i):
    return jnp.where(i < NB_CTX, 0, 1 + (i - NB_CTX) % LAT_BLOCKS)


def _mod_kernel(cond_ref, w_ref, b_ref, o_ref):
    c = cond_ref[...]
    a = (c * jax.nn.sigmoid(c)).astype(BF16)
    o_ref[...] = _dot(a, w_ref[...].astype(BF16)) + b_ref[...]


def _modulation(cond8, ada_w, ada_b):
    tn = 512
    nj = N_MOD * D_MODEL // tn
    return pl.pallas_call(
        _mod_kernel,
        grid=(DEPTH, nj),
        in_specs=[
            pl.BlockSpec((8, D_MODEL), lambda l, j: (0, 0)),
            pl.BlockSpec((None, D_MODEL, tn), lambda l, j: (l, 0, j)),
            pl.BlockSpec((None, 1, tn), lambda l, j: (l, 0, j)),
        ],
        out_specs=pl.BlockSpec((None, 8, tn), lambda l, j: (l, 0, j)),
        out_shape=jax.ShapeDtypeStruct((DEPTH, 8, N_MOD * D_MODEL), F32),
        compiler_params=_cparams(2),
        name="modulation",
    )(cond8, ada_w, ada_b.reshape(DEPTH, 1, N_MOD * D_MODEL))


def _half_swap(x, half):
    n = x.shape[1]
    lane = lax.broadcasted_iota(jnp.int32, (1, n), 1)
    return jnp.where((lane & half) == 0, pltpu.roll(x, n - half, 1), pltpu.roll(x, half, 1))


def _mla_expand(rows, cq, ckv, kr, cos32, sin32, wqa_ref, wk_ref, e_ref, wv_ref,
                q_ref, k_ref, v_ref):
    if q_ref is not None:
        lane = lax.broadcasted_iota(jnp.int32, (1, LANES), 1)
        rope_lane = jnp.logical_and(lane >= MLA_NOPE, lane < MLA_NOPE + MLA_ROPE)
        cos_h = jnp.where(rope_lane, cos32, 1.0)
        sin_h = jnp.where(rope_lane, sin32, 0.0)
        for hd in range(MLA_HEADS):
            lo, hi = hd * LANES, (hd + 1) * LANES
            q = _dot(cq, wqa_ref[:, lo:hi])
            q = q * cos_h + _half_swap(q, MLA_ROPE // 4) * sin_h
            q_ref[rows, lo:hi] = (q * (MLA_SCALE * LOG2E)).astype(BF16)
    k_ref[rows, :] = (_dot(ckv, wk_ref[...]) + _dot(kr, e_ref[...])).astype(BF16)
    v_ref[rows, :] = _dot(ckv, wv_ref[...]).astype(BF16)


def _chunks():
    return [pl.ds(r * TM, TM) for r in range(TB // TM)]


def _stage_a_kernel(xc_ref, xl_ref, mod_ref, g_ref, win_ref, qn_ref, kvn_ref, tab_ref,
                    wqa_ref, wk_ref, e_ref, wv_ref,
                    fin_ref, ckv_ref, kr_ref, sq_ref, sk_ref, sv_ref, h_ref,
                    qm_ref, km_ref, vm_ref):
    is_ctx = pl.program_id(0) < NB_CTX
    for rows in _chunks():
        x = jnp.where(is_ctx, xc_ref[rows, :], xl_ref[rows, :])
        h = (_rms_rows(x, g_ref[...]) * (1.0 + mod_ref[:, 1024:2048]) + mod_ref[:, 0:1024]).astype(BF16)
        h_ref[rows, :] = h

        def proj(seg):
            return _dot(h, win_ref[:, seg[0]:seg[1]])

        fin_ref[rows, :] = proj(A_F).astype(BF16)
        cq = _rms_rows(proj(A_QD), qn_ref[...]).astype(BF16)
        ckv = _rms_rows(proj(A_KV), kvn_ref[...])
        ckv_ref[rows, :] = ckv
        cos64 = tab_ref[rows, 0:512]
        sin64 = tab_ref[rows, 512:1024]
        sq = proj(A_SQ)
        sq = sq * cos64 + _half_swap(sq, SWA_HEAD_DIM // 4) * sin64
        sq_ref[rows, :] = (sq * (SWA_SCALE * LOG2E)).astype(BF16)
        sk = proj(A_SK)
        sk_ref[rows, :] = sk * cos64[:, 0:128] + _half_swap(sk, SWA_HEAD_DIM // 4) * sin64[:, 0:128]
        sv_ref[rows, :] = proj(A_SV)
        cos32 = tab_ref[rows, 1024:1152]
        sin32 = tab_ref[rows, 1152:1280]
        kr = proj(A_KR)
        kr = kr * cos32 + _half_swap(kr, MLA_ROPE // 4) * sin32
        kr_ref[rows, :] = kr
        _mla_expand(rows, cq, ckv.astype(BF16), kr.astype(BF16), cos32, sin32,
                    wqa_ref, wk_ref, e_ref, wv_ref, qm_ref, km_ref, vm_ref)


def _stage_a(xc, xl, modt, g0, w_in_wide, q_norm, kv_norm, tab, wqa, wk, e_mat, wv):
    row = lambda w: pl.BlockSpec((TB, w), lambda i: (i, 0))
    every = lambda w, dt: (row(w), jax.ShapeDtypeStruct((N_TOK, w), dt))
    outs = [every(512, BF16), every(128, F32), every(128, F32), every(512, BF16), every(128, F32),
            every(128, F32), every(D_MODEL, BF16), every(1024, BF16), every(1024, BF16), every(512, BF16)]
    return pl.pallas_call(
        _stage_a_kernel,
        grid=(NB,),
        in_specs=[
            _ctx_rows(D_MODEL), _lat_rows(D_MODEL),
            pl.BlockSpec((None, 1, N_MOD * D_MODEL), lambda i: (i, 0, 0)),
            _const_spec((1, D_MODEL)),
            _const_spec((D_MODEL, W_IN_WIDE)),
            _const_spec((1, MLA_Q_RANK)),
            _const_spec((1, MLA_KV_RANK)),
            pl.BlockSpec((TB, TAB_W), lambda i: (_tab_row_block(i), 0)),
            _const_spec((MLA_Q_RANK, 1024)),
            _const_spec((128, 1024)), _const_spec((128, 1024)), _const_spec((128, 512)),
        ],
        out_specs=[s for s, _ in outs],
        out_shape=[s for _, s in outs],
        compiler_params=_cparams(1),
        name="stage_a",
    )(xc, xl, modt, g0, w_in_wide, q_norm, kv_norm, tab, wqa, wk, e_mat, wv)


def _fnet_kernel(t_len, scale, fin_ref, f_ref, bd_ref, o_ref, zz_ref):
    @pl.when(pl.program_id(1) == 0)
    def _():
        z = fin_ref[...]
        zz_ref[0:t_len, :] = _dot(z, bd_ref[:, 0:512]).astype(BF16)
        zz_ref[t_len:2 * t_len, :] = _dot(z, bd_ref[:, 512:1024]).astype(BF16)

    o_ref[...] = (_dot(f_ref[...], zz_ref[...]) * scale).astype(BF16)


def _fnet(fin, fmat, bd, n_batch, t_len, row_block0):
    scale = 1.0 / math.sqrt(t_len * FNET_GROUP_DIM)
    ft = min(t_len, FNET_ROWS)
    return pl.pallas_call(
        functools.partial(_fnet_kernel, t_len, scale),
        grid=(n_batch, t_len // ft),
        in_specs=[
            pl.BlockSpec((t_len, FNET_WIDTH), lambda b, i: (row_block0 + b, 0)),
            pl.BlockSpec((ft, 2 * t_len), lambda b, i: (i, 0)),
            _const_spec((FNET_WIDTH, 2 * FNET_WIDTH)),
        ],
        out_specs=pl.BlockSpec((ft, FNET_WIDTH), lambda b, i: (b * (t_len // ft) + i, 0)),
        out_shape=jax.ShapeDtypeStruct((n_batch * t_len, FNET_WIDTH), BF16),
        scratch_shapes=[pltpu.VMEM((2 * t_len, FNET_WIDTH), BF16)],
        compiler_params=_cparams(2),
        name=f"fnet_{t_len}",
    )(fin, fmat, bd)


def _mla_cache_kernel(ckv_ref, kr_ref, wk_ref, e_ref, wv_ref, k_ref, v_ref):
    _mla_expand(slice(None), None, ckv_ref[...].astype(BF16), kr_ref[...].astype(BF16), None, None,
                None, wk_ref, e_ref, wv_ref, None, k_ref, v_ref)


def _mla_cache_kv(ckv_cache, kr_cache, wk, e_mat, wv):
    row = lambda w: pl.BlockSpec((TM, w), lambda i: (i, 0))
    return pl.pallas_call(
        _mla_cache_kernel,
        grid=(N_CACHE // TM,),
        in_specs=[row(128), row(128),
                  _const_spec((128, 1024)), _const_spec((128, 1024)), _const_spec((128, 512))],
        out_specs=[row(1024), row(512)],
        out_shape=[jax.ShapeDtypeStruct((N_CACHE, 1024), BF16),
                   jax.ShapeDtypeStruct((N_CACHE, 512), BF16)],
        compiler_params=_cparams(1),
        name="mla_cache_kv",
    )(ckv_cache, kr_cache, wk, e_mat, wv)


def _mla_attn_kernel(n_seg, pairs, q_ref, *refs):
    k_refs = refs[0:n_seg]
    v_refs = refs[n_seg:2 * n_seg]
    o_ref = refs[2 * n_seg]
    lane = lax.broadcasted_iota(jnp.int32, (1, LANES), 1)
    low = lane < MLA_V
    for pr in range(pairs):
        outs = []
        for hh in range(2):
            hd = 2 * pr + hh
            q = q_ref[:, hd * LANES:(hd + 1) * LANES]
            ss = [_dot_nt(q, k[:, hd * LANES:(hd + 1) * LANES]) for k in k_refs]
            m = functools.reduce(jnp.maximum, [s.max(axis=-1, keepdims=True) for s in ss])
            keep = low if hh == 0 else jnp.logical_not(low)
            sum_lane = MLA_V if hh == 0 else 0
            po = None
            for s, v_ref in zip(ss, v_refs):
                v = v_ref[:, pr * LANES:(pr + 1) * LANES]
                vm = jnp.where(lane == sum_lane, jnp.ones_like(v), jnp.where(keep, v, jnp.zeros_like(v)))
                t = _dot(jnp.exp2(s - m).astype(BF16), vm)
                po = t if po is None else po + t
            outs.append(po / po[:, sum_lane:sum_lane + 1])
        o_ref[:, pr * LANES:(pr + 1) * LANES] = jnp.where(low, outs[0], outs[1]).astype(BF16)


def _mla_attn(q_all, k_all, v_all, k_cache, v_cache, latent):
    if latent:
        tq, pairs = MLA_LAT_TQ, MLA_LAT_PAIRS
        n_b, n_q = DEC_BATCH, DEC_SEQ // tq
        q0 = N_CTX // tq
        kv_specs = [
            pl.BlockSpec((PAST_LEN, 256 * pairs), lambda b, hp, i: (b, hp)),
            pl.BlockSpec((DEC_SEQ, 256 * pairs), lambda b, hp, i: (N_CTX // DEC_SEQ + b, hp)),
            pl.BlockSpec((PAST_LEN, 128 * pairs), lambda b, hp, i: (b, hp)),
            pl.BlockSpec((DEC_SEQ, 128 * pairs), lambda b, hp, i: (N_CTX // DEC_SEQ + b, hp)),
        ]
        args = (q_all, k_cache, k_all, v_cache, v_all)
        n_seg = 2
    else:
        tq, pairs = SEQ, MLA_HEADS // 2
        n_b, n_q = BATCH, 1
        q0 = 0
        kv_specs = [
            pl.BlockSpec((SEQ, 256 * pairs), lambda b, hp, i: (b, hp)),
            pl.BlockSpec((SEQ, 128 * pairs), lambda b, hp, i: (b, hp)),
        ]
        args = (q_all, k_all, v_all)
        n_seg = 1
    return pl.pallas_call(
        functools.partial(_mla_attn_kernel, n_seg, pairs),
        grid=(n_b, MLA_HEADS // (2 * pairs), n_q),
        in_specs=[pl.BlockSpec((tq, 256 * pairs), lambda b, hp, i: (q0 + b * n_q + i, hp))] + kv_specs,
        out_specs=pl.BlockSpec((tq, 128 * pairs), lambda b, hp, i: (b * n_q + i, hp)),
        out_shape=jax.ShapeDtypeStruct((n_b * n_q * tq, MLA_HEADS * MLA_V), BF16),
        compiler_params=_cparams(3),
        name="mla_attn_lat" if latent else "mla_attn_ctx",
    )(*args)


def _swa_kernel(windowed, n_steps, sink_ref, q_ref, *refs):
    n_seg = 4 if windowed else 1
    k_refs = refs[0:n_seg]
    v_refs = refs[n_seg:2 * n_seg]
    o_ref = refs[2 * n_seg]
    step = pl.program_id(1)
    lane = lax.broadcasted_iota(jnp.int32, (1, LANES), 1)
    low = lane < SWA_HEAD_DIM
    high = jnp.logical_not(low)

    k_all = jnp.concatenate([r[...] for r in k_refs], axis=0)
    v_all = jnp.concatenate([r[...] for r in v_refs], axis=0)
    k_sw = pltpu.roll(k_all, SWA_HEAD_DIM, 1)
    v_sw = pltpu.roll(v_all, SWA_HEAD_DIM, 1)

    if windowed:
        tq = SWA_WINDOW
        qi = lax.broadcasted_iota(jnp.int32, (2 * tq, tq), 0) % tq
        kj = lax.broadcasted_iota(jnp.int32, (2 * tq, tq), 1)
        after = kj >= qi
        before = kj <= qi
        n_sub = q_ref.shape[0] // tq
        bias_lo = [jnp.where(jnp.logical_and(after, step > 0) if sub == 0 else after, 0.0, NEG_INF)
                   for sub in range(n_sub)]
        bias_hi = [jnp.where(jnp.logical_and(before, step < n_steps - 1) if sub == n_sub - 1 else before,
                             0.0, NEG_INF) for sub in range(n_sub)]
    else:
        tq = q_ref.shape[0]
        n_sub = 1
    top_rows = lax.broadcasted_iota(jnp.int32, (2 * tq, 1), 0) < tq

    for g in range(SWA_KV_HEADS):
        kh, vh = [], []
        for half in range(2):
            keep = low if half == 0 else high
            sum_lane = SWA_HEAD_DIM if half == 0 else 0
            straight = (g == half)
            kh.append(jnp.where(keep, k_all if straight else k_sw, 0.0).astype(BF16))
            vh.append(jnp.where(lane == sum_lane, 1.0,
                                jnp.where(keep, v_all if straight else v_sw, 0.0)).astype(BF16))
        for sub in range(n_sub):
            rows = slice(sub * tq, (sub + 1) * tq)
            qs = jnp.concatenate([q_ref[rows, 256 * g:256 * g + 128],
                                  q_ref[rows, 256 * g + 128:256 * g + 256]], axis=0)
            halves = []
            for half in range(2):
                sum_lane = SWA_HEAD_DIM if half == 0 else 0
                ks, vs = kh[half], vh[half]
                if windowed:
                    w0 = PAST_LEN + sub * tq
                    if sub == 0:
                        ks, vs = ks[0:w0 + 3 * tq], vs[0:w0 + 3 * tq]
                    else:
                        ks = jnp.concatenate([ks[0:PAST_LEN], ks[w0:w0 + 3 * tq]], axis=0)
                        vs = jnp.concatenate([vs[0:PAST_LEN], vs[w0:w0 + 3 * tq]], axis=0)
                s = _dot_nt(qs, ks)
                if windowed:
                    c0, c1, c2 = PAST_LEN, PAST_LEN + tq, PAST_LEN + 2 * tq
                    s = jnp.concatenate([s[:, :c0], s[:, c0:c1] + bias_lo[sub], s[:, c1:c2],
                                         s[:, c2:] + bias_hi[sub]], axis=1)
                sink = jnp.where(top_rows, sink_ref[4 * g + half], sink_ref[4 * g + 2 + half]) * LOG2E
                m = jnp.maximum(s.max(axis=-1, keepdims=True), sink)
                po = _dot(jnp.exp2(s - m).astype(BF16), vs)
                halves.append(po / (po[:, sum_lane:sum_lane + 1] + jnp.exp2(sink - m)))
            out = jnp.where(low, halves[0], halves[1])
            o_ref[rows, 256 * g:256 * g + 128] = out[0:tq].astype(BF16)
            o_ref[rows, 256 * g + 128:256 * g + 256] = out[tq:2 * tq].astype(BF16)


def _swa_attn(sink, sq, sk, sv, cache_k, cache_v, latent):
    smem = pl.BlockSpec(memory_space=pltpu.SMEM)
    if latent:
        tq = SWA_QSUB * SWA_WINDOW
        n_b, n_qb = DEC_BATCH, DEC_SEQ // tq
        base = N_CTX // tq
        last = DEC_SEQ // SWA_WINDOW - 1

        def prev(b, i):
            return (SWA_QSUB * (base + b * n_qb) + jnp.maximum(SWA_QSUB * i - 1, 0), 0)

        def cur(b, i):
            return (base + b * n_qb + i, 0)

        def nxt(b, i):
            return (SWA_QSUB * (base + b * n_qb) + jnp.minimum(SWA_QSUB * (i + 1), last), 0)

        cache = pl.BlockSpec((None, PAST_LEN, 128), lambda b, i: (b, 0, 0))
        edge = lambda f: pl.BlockSpec((SWA_WINDOW, 128), f)
        kv_specs = [cache, edge(prev), pl.BlockSpec((tq, 128), cur), edge(nxt)] * 2
        args = (cache_k, sk, sk, sk, cache_v, sv, sv, sv)
        q_spec = pl.BlockSpec((tq, 512), cur)
        o_spec = pl.BlockSpec((tq, 512), lambda b, i: (b * n_qb + i, 0))
    else:
        tq = SEQ
        n_b, n_qb = BATCH, 1
        blk = pl.BlockSpec((tq, 128), lambda b, i: (b, 0))
        kv_specs = [blk, blk]
        args = (sk, sv)
        q_spec = pl.BlockSpec((tq, 512), lambda b, i: (b, 0))
        o_spec = q_spec
    return pl.pallas_call(
        functools.partial(_swa_kernel, latent, n_qb),
        grid=(n_b, n_qb),
        in_specs=[smem, q_spec] + kv_specs,
        out_specs=o_spec,
        out_shape=jax.ShapeDtypeStruct((n_b * n_qb * tq, 512), BF16),
        compiler_params=_cparams(2),
        name="swa_lat" if latent else "swa_ctx",
    )(sink, sq, *args)


def _route(h, rwt_ref, rb_ref, tri_ref, carry):
    n = h.shape[0]
    gsz = N_EXPERTS // N_EXPERT_GROUPS
    scores = jax.nn.sigmoid(_dot_nt(rwt_ref[...], h))
    biased = scores + rb_ref[...]
    mem = lax.broadcasted_iota(jnp.int32, (gsz, n), 0).astype(F32)
    gs_rows = []
    for g in range(N_EXPERT_GROUPS):
        bg = biased[g * gsz:(g + 1) * gsz, :]
        m1 = bg.max(axis=0, keepdims=True)
        first = jnp.min(jnp.where(bg == m1, mem, float(gsz)), axis=0, keepdims=True)
        m2 = jnp.where(mem == first, -jnp.inf, bg).max(axis=0, keepdims=True)
        gs_rows.append(m1 + m2)
    gs = jnp.concatenate(gs_rows, axis=0)
    gid = lax.broadcasted_iota(jnp.int32, gs.shape, 0).astype(F32)
    gsel = jnp.zeros(gs.shape, F32)
    for _ in range(TOPK_GROUPS):
        mx = gs.max(axis=0, keepdims=True)
        pick = gid == jnp.min(jnp.where(gs == mx, gid, float(N_EXPERT_GROUPS)), axis=0, keepdims=True)
        gsel = jnp.where(pick, 1.0, gsel)
        gs = jnp.where(pick, -jnp.inf, gs)
    emask = jnp.concatenate(
        [jnp.broadcast_to(gsel[g:g + 1, :], (gsz, n)) for g in range(N_EXPERT_GROUPS)], axis=0)
    cand = jnp.where(emask > 0.5, biased, NEG_INF)
    eid = lax.broadcasted_iota(jnp.int32, cand.shape, 0).astype(F32)
    picks = []
    self32 = jnp.zeros(cand.shape, F32)
    for _ in range(TOP_K):
        mx = cand.max(axis=0, keepdims=True)
        pick = eid == jnp.min(jnp.where(cand == mx, eid, float(N_EXPERTS)), axis=0, keepdims=True)
        picks.append(pick)
        self32 = jnp.where(pick, 1.0, self32)
        cand = jnp.where(pick, -jnp.inf, cand)
    pos = _dot(self32.astype(BF16), tri_ref[...]) + carry
    sel_scores = [jnp.sum(jnp.where(p, scores, 0.0), axis=0, keepdims=True) for p in picks]
    wsum = functools.reduce(lambda a, b: a + b, sel_scores)
    zero_f = jnp.zeros((2, n), F32)
    eidx = [jnp.sum(jnp.where(p, eid, 0.0), axis=0, keepdims=True) for p in picks]
    epos = [jnp.sum(jnp.where(p, pos, 0.0), axis=0, keepdims=True) for p in picks]
    ew = [s / wsum * ROUTED_SCALE for s in sel_scores]
    return (jnp.concatenate(eidx + [zero_f], axis=0).astype(jnp.int32),
            jnp.concatenate(epos + [zero_f], axis=0).astype(jnp.int32),
            jnp.concatenate(ew + [zero_f], axis=0),
            carry + jnp.sum(self32, axis=1, keepdims=True))


def _stage_e_kernel(xc_ref, xl_ref, mod_ref, g1_ref, g2_ref, fnc_ref, fnl_ref, omc_ref, oml_ref, osc_ref, osl_ref,
                    h_ref, wg_ref, bg_ref, wf_ref, wm_ref, ws_ref, wo_ref, rwt_ref, rb_ref, tri_ref,
                    x1_ref, h2_ref, eidx_ref, epos_ref, ew_ref, cnt_ref, carry_ref, h2b_ref, wgb_ref):
    @pl.when(pl.program_id(0) == 0)
    def _():
        carry_ref[...] = jnp.zeros_like(carry_ref)
        wgb_ref[...] = wg_ref[...].astype(BF16)

    is_ctx = pl.program_id(0) < NB_CTX
    carry = carry_ref[...]
    for rows in _chunks():
        fn = jnp.where(is_ctx, fnc_ref[rows, :], fnl_ref[rows, :])
        om = jnp.where(is_ctx, omc_ref[rows, :], oml_ref[rows, :])
        osw = jnp.where(is_ctx, osc_ref[rows, :], osl_ref[rows, :])
        h = h_ref[rows, :]

        def gate(c):
            lo, hi = c * D_MODEL, (c + 1) * D_MODEL
            return jax.nn.sigmoid(_dot(h, wgb_ref[:, lo:hi]) + bg_ref[:, lo:hi])

        merged = (gate(0) * _dot(fn, wf_ref[...]) + gate(1) * _dot(om, wm_ref[...])
                  + gate(2) * _dot(osw, ws_ref[...]))
        mix = _dot(merged.astype(BF16), wo_ref[...])
        x = jnp.where(is_ctx, xc_ref[rows, :], xl_ref[rows, :])
        x1 = x + mod_ref[:, 2048:3072] * _rms_rows(mix, g1_ref[...])
        x1_ref[rows, :] = x1
        h2 = _rms_rows(x1, g2_ref[...]) * (1.0 + mod_ref[:, 4096:5120]) + mod_ref[:, 3072:4096]
        h2_ref[0, rows, :], h2_ref[1, rows, :] = _pack_pair(h2)
        h2b_ref[rows, :] = h2.astype(BF16)
    eidx_ref[...], epos_ref[...], ew_ref[...], carry = _route(h2b_ref[...], rwt_ref, rb_ref, tri_ref, carry)
    carry_ref[...] = carry
    cnt_ref[...] = jnp.broadcast_to(carry, cnt_ref.shape).astype(jnp.int32)


def _stage_e(layer, xc, xl, modt, g1, g2, mixed, h, w_gate, b_gate, wf, wm, ws, wo, rwt, rbias, tri):
    row = lambda w: pl.BlockSpec((TB, w), lambda i: (i, 0))
    ctx, lat = _ctx_rows(512), _lat_rows(512)
    col = lambda dt: (pl.BlockSpec((8, TB), lambda i: (0, i)), jax.ShapeDtypeStruct((8, N_TOK), dt))
    picks = [col(jnp.int32), col(jnp.int32), col(F32)]
    return pl.pallas_call(
        _stage_e_kernel,
        grid=(NB,),
        in_specs=[
            _ctx_rows(D_MODEL), _lat_rows(D_MODEL),
            pl.BlockSpec((None, 1, N_MOD * D_MODEL), lambda i: (i, 0, 0)),
            _const_spec((1, D_MODEL)), _const_spec((1, D_MODEL)),
            ctx, lat, ctx, lat, ctx, lat, row(D_MODEL),
            pl.BlockSpec((None, D_MODEL, 3 * D_MODEL), lambda i: (layer, 0, 0), pipeline_mode=pl.Buffered(1)),
            _const_spec((1, 3 * D_MODEL)),
            _layer_spec((512, D_MODEL), layer), _layer_spec((512, D_MODEL), layer),
            _layer_spec((512, D_MODEL), layer), _layer_spec((D_MODEL, D_MODEL), layer),
            _const_spec((N_EXPERTS, D_MODEL)), _const_spec((N_EXPERTS, 1)), _const_spec((TB, TB)),
        ],
        out_specs=[row(D_MODEL), pl.BlockSpec((2, TB, PACKED), lambda i: (0, i, 0))] + [s for s, _ in picks]
        + [_const_spec((N_EXPERTS, LANES))],
        out_shape=[jax.ShapeDtypeStruct((N_TOK, D_MODEL), F32),
                   jax.ShapeDtypeStruct((2, N_TOK, PACKED), jnp.int32)] + [s for _, s in picks]
        + [jax.ShapeDtypeStruct((N_EXPERTS, LANES), jnp.int32)],
        scratch_shapes=[pltpu.VMEM((N_EXPERTS, 1), F32), pltpu.VMEM((TB, D_MODEL), BF16),
                        pltpu.VMEM((D_MODEL, 3 * D_MODEL), BF16)],
        compiler_params=_cparams(1),
        name="stage_e",
    )(xc, xl, modt, g1, g2, *mixed, h, w_gate, b_gate, wf, wm, ws, wo, rwt, rbias, tri)


def _expert_kernel(layer, te_ref, tv_ref, par_ref, nxt_ref, x_ref, w1_hbm, w3_hbm, w2_hbm, o_ref,
                   w1f_ref, w3f_ref, w2f_ref, w1b_ref, w3b_ref, w2b_ref, sem):
    j = pl.program_id(0)
    valid = tv_ref[j] > 0
    first = jnp.logical_and(valid, jnp.logical_or(j == 0, te_ref[j] != te_ref[jnp.maximum(j - 1, 0)]))

    def copies(expert, slot):
        return [pltpu.make_async_copy(w_hbm.at[layer, expert], buf.at[slot], sem.at[slot, i])
                for i, (w_hbm, buf) in enumerate(((w1_hbm, w1f_ref), (w3_hbm, w3f_ref), (w2_hbm, w2f_ref)))]

    @pl.when(jnp.logical_and(valid, j == 0))
    def _():
        for cp in copies(te_ref[0], par_ref[0]):
            cp.start()

    @pl.when(first)
    def _():
        slot = par_ref[j]
        for cp in copies(te_ref[j], slot):
            cp.wait()

        @pl.when(nxt_ref[j] >= 0)
        def _():
            for cp in copies(nxt_ref[j], 1 - slot):
                cp.start()

        w1b_ref[...] = w1f_ref[slot].astype(BF16)
        w3b_ref[...] = w3f_ref[slot].astype(BF16)
        w2b_ref[...] = w2f_ref[slot].astype(BF16)

    def run(n_chunks):
        for r in range(n_chunks):
            rows = pl.ds(r * EXPERT_ROWS, EXPERT_ROWS)
            x = _unpack_pair(x_ref[0, rows, :], x_ref[1, rows, :]).astype(BF16)
            hg = _dot(x, w1b_ref[...])
            hu = _dot(x, w3b_ref[...])
            act = (jax.nn.silu(hg) * hu).astype(BF16)
            o_ref[0, rows, :], o_ref[1, rows, :] = _pack_pair(_dot(act, w2b_ref[...]))

    for n_chunks in range(1, TE // EXPERT_ROWS + 1):
        pl.when(tv_ref[j] == n_chunks)(functools.partial(run, n_chunks))


def _experts(layer, tile_expert, tile_chunks, tile_slot, tile_next, xs, w1, w3, w2):
    slot_rows = pl.BlockSpec((2, TE, PACKED), lambda j, te, tv, par, nxt: (0, j, 0))
    anywhere = pl.BlockSpec(memory_space=pl.ANY)
    grid_spec = pltpu.PrefetchScalarGridSpec(
        num_scalar_prefetch=4,
        grid=(NTE,),
        in_specs=[slot_rows, anywhere, anywhere, anywhere],
        out_specs=slot_rows,
        scratch_shapes=[pltpu.VMEM((2, D_MODEL, EXPERT_FF), F32), pltpu.VMEM((2, D_MODEL, EXPERT_FF), F32),
                        pltpu.VMEM((2, EXPERT_FF, D_MODEL), F32),
                        pltpu.VMEM((D_MODEL, EXPERT_FF), BF16), pltpu.VMEM((D_MODEL, EXPERT_FF), BF16),
                        pltpu.VMEM((EXPERT_FF, D_MODEL), BF16),
                        pltpu.SemaphoreType.DMA((2, 3))],
    )
    return pl.pallas_call(
        functools.partial(_expert_kernel, layer),
        grid_spec=grid_spec,
        out_shape=jax.ShapeDtypeStruct((2, S_MAX, PACKED), jnp.int32),
        compiler_params=_cparams(1),
        name="experts",
    )(tile_expert, tile_chunks, tile_slot, tile_next, xs, w1, w3, w2)


def _sc_mesh():
    return plsc.VectorSubcoreMesh(core_axis_name="c", subcore_axis_name="s",
                                  num_cores=SC_CORES, num_subcores=SC_SUBCORES)


def _sc_scatter_rows(rows, slot8):
    @functools.partial(pl.kernel, mesh=_sc_mesh(), scratch_types=[pltpu.SemaphoreType.DMA],
                       out_type=jax.ShapeDtypeStruct((2, S_MAX, PACKED), jnp.int32))
    def scatter(x_hbm, i_hbm, o_hbm, sem):
        for h in range(2):
            dst = o_hbm.at[h]

            def body(x_vmem, i_vmem, dst=dst):
                copies = [pltpu.async_copy(x_vmem, dst.at[i_vmem.at[k]], sem) for k in range(TOP_K)]
                for cp in copies:
                    cp.wait()

            pltpu.emit_pipeline(
                body,
                grid=(N_TOK // SC_ROWS,),
                in_specs=[pl.BlockSpec((SC_ROWS, PACKED), lambda i: (i, 0)),
                          pl.BlockSpec((8, SC_ROWS), lambda i: (0, i))],
                out_specs=[],
                core_axis_name=("c", "s"),
                dimension_semantics=(pltpu.PARALLEL,),
            )(x_hbm.at[h], i_hbm)

    return scatter(rows, slot8)


def _sc_gather_rows(table, idx):
    n = idx.shape[1]

    @functools.partial(pl.kernel, mesh=_sc_mesh(), scratch_types=[],
                       out_type=jax.ShapeDtypeStruct((2, n, PACKED), jnp.int32))
    def gather(t_hbm, i_hbm, o_hbm):
        for h in range(2):
            src = t_hbm.at[h]

            def body(i_vmem, o_vmem, src=src):
                pltpu.sync_copy(src.at[i_vmem.at[0]], o_vmem)

            pltpu.emit_pipeline(
                body,
                grid=(n // SC_ROWS,),
                in_specs=[pl.BlockSpec((1, SC_ROWS), lambda i: (0, i))],
                out_specs=[pl.BlockSpec((SC_ROWS, PACKED), lambda i: (i, 0))],
                core_axis_name=("c", "s"),
                dimension_semantics=(pltpu.PARALLEL,),
            )(i_hbm, o_hbm.at[h])

    return gather(table, idx)


def _stage_g_kernel(x1_ref, mod_ref, g3_ref, yg_ref, ew_ref, h2_ref,
                    s1_ref, s3_ref, s2_ref, oc_ref, ol_ref):
    is_ctx = pl.program_id(0) < NB_CTX
    for rows in _chunks():
        h = _unpack_pair(h2_ref[0, rows, :], h2_ref[1, rows, :]).astype(BF16)
        act = jax.nn.silu(_dot(h, s1_ref[...])) * _dot(h, s3_ref[...])
        y = _dot(act.astype(BF16), s2_ref[...])
        for k in range(TOP_K):
            y = y + ew_ref[rows, k:k + 1] * _unpack_pair(yg_ref[0, k, rows, :], yg_ref[1, k, rows, :])
        out = x1_ref[rows, :] + mod_ref[:, 5120:6144] * _rms_rows(y, g3_ref[...])

        @pl.when(is_ctx)
        def _():
            oc_ref[rows, :] = out

        @pl.when(jnp.logical_not(is_ctx))
        def _():
            ol_ref[rows, :] = out


def _stage_g(layer, x1, modt, g3, yg, ew_rows, h2, s1, s3, s2):
    row = lambda w: pl.BlockSpec((TB, w), lambda i: (i, 0))
    picked = pl.BlockSpec((2, TOP_K, TB, PACKED), lambda i: (0, 0, i, 0))
    return pl.pallas_call(
        _stage_g_kernel,
        grid=(NB,),
        in_specs=[row(D_MODEL), pl.BlockSpec((None, 1, N_MOD * D_MODEL), lambda i: (i, 0, 0)),
                  _const_spec((1, D_MODEL)), picked, row(8),
                  pl.BlockSpec((2, TB, PACKED), lambda i: (0, i, 0)),
                  _layer_spec((D_MODEL, SHARED_FF), layer), _layer_spec((D_MODEL, SHARED_FF), layer),
                  _layer_spec((SHARED_FF, D_MODEL), layer)],
        out_specs=[_ctx_rows(D_MODEL), _lat_rows(D_MODEL)],
        out_shape=[jax.ShapeDtypeStruct((N_CTX, D_MODEL), F32),
                   jax.ShapeDtypeStruct((N_LAT, D_MODEL), F32)],
        compiler_params=_cparams(1),
        name="stage_g",
    )(x1, modt, g3, yg, ew_rows, h2, s1, s3, s2)


def _rope_tables():
    t = np.arange(DEC_SEQ)
    pos = np.stack([(t // GRID_W), (t % GRID_W)], axis=-1).astype(np.float32)

    def table(r):
        n_freq = r // 4
        inv = np.float32(ROPE_BASE) ** (-np.arange(n_freq, dtype=np.float32) / np.float32(n_freq))
        ang = pos[:, :, None] * inv.astype(np.float32)
        cos = np.cos(ang)
        sin = np.sin(ang)
        cos_t = np.stack([cos, cos], axis=2).reshape(DEC_SEQ, r)
        sin_t = np.stack([-sin, sin], axis=2).reshape(DEC_SEQ, r)
        return cos_t, sin_t

    c64, s64 = table(SWA_HEAD_DIM)
    c32, s32 = table(MLA_ROPE)
    lat = np.concatenate([np.tile(c64, (1, 8)), np.tile(s64, (1, 8)),
                          np.tile(c32, (1, 4)), np.tile(s32, (1, 4))], axis=1)
    ident = np.concatenate([np.ones((TB, 512)), np.zeros((TB, 512)),
                            np.ones((TB, 128)), np.zeros((TB, 128))], axis=1)
    return jnp.asarray(np.concatenate([ident, lat], axis=0).astype(np.float32))


def _dft_pair(n):
    k = np.arange(n, dtype=np.int64)
    ang = ((k[:, None] * k[None, :]) % n).astype(np.float64) * (2.0 * math.pi / n)
    return np.cos(ang), np.sin(ang)


def _fnet_tables():
    c64, s64 = _dft_pair(FNET_GROUP_DIM)
    eye = np.eye(FNET_GROUPS)
    bd = np.concatenate([np.kron(eye, c64), np.kron(eye, s64)], axis=1)
    mats = []
    for t_len in (SEQ, DEC_SEQ):
        c, s = _dft_pair(t_len)
        mats.append(np.concatenate([c, -s], axis=1))
    return tuple(jnp.asarray(m.astype(np.float32).astype(BF16)) for m in (bd, mats[0], mats[1]))


def _layer_weights(l, w_in, w_uq, w_ukv):
    w = w_in[l]
    wide = jnp.concatenate([w[:, 0:1024], w[:, 1056:1824], w[:, 1024:1056],
                            jnp.zeros((D_MODEL, 96), F32)], axis=1).astype(BF16)

    uq = w_uq[l].reshape(MLA_Q_RANK, MLA_HEADS, MLA_NOPE + MLA_ROPE)
    z32 = jnp.zeros((MLA_Q_RANK, MLA_HEADS, 32), F32)
    wqa = jnp.concatenate([uq, z32], axis=2).reshape(MLA_Q_RANK, 1024).astype(BF16)
    ukv = w_ukv[l].reshape(MLA_KV_RANK, MLA_HEADS, MLA_NOPE + MLA_V)
    wk = jnp.concatenate([ukv[:, :, :MLA_NOPE], jnp.zeros((MLA_KV_RANK, MLA_HEADS, 64), F32)],
                         axis=2).reshape(MLA_KV_RANK, 1024).astype(BF16)
    wv = ukv[:, :, MLA_NOPE:].reshape(MLA_KV_RANK, 512).astype(BF16)
    return wide, wqa, wk, wv


def _rope_placement():
    e = np.zeros((128, 1024), np.float32)
    for hd in range(MLA_HEADS):
        for i in range(MLA_ROPE):
            e[i, hd * 128 + MLA_NOPE + i] = 1.0
    return jnp.asarray(e, BF16)


def _moe_dispatch_plan(eidx, epos, counts):
    padded = ((counts + TE - 1) // TE) * TE
    ends = jnp.cumsum(padded)
    offs = ends - padded
    ids = jnp.arange(N_EXPERTS, dtype=jnp.int32)
    picked_off = jnp.sum(jnp.where(eidx[:, :, None] == ids, offs, 0), axis=-1)
    slot = picked_off + epos
    starts = jnp.arange(NTE, dtype=jnp.int32) * TE
    tile_expert = jnp.sum((ends[None, :] <= starts[:, None]).astype(jnp.int32), axis=1)
    tile_expert = jnp.minimum(tile_expert, N_EXPERTS - 1)
    pick = tile_expert[:, None] == ids[None, :]
    last_real = jnp.sum(jnp.where(pick, (offs + counts)[None, :], 0), axis=1)
    n_real = jnp.clip(last_real - starts, 0, TE)
    n_real = jnp.where(starts < ends[-1], n_real, 0)
    tile_chunks = ((n_real + EXPERT_ROWS - 1) // EXPERT_ROWS).astype(jnp.int32)
    used = counts > 0
    rank = jnp.cumsum(used.astype(jnp.int32)) - 1
    tile_slot = jnp.sum(jnp.where(pick, rank[None, :], 0), axis=1) % 2
    later_used = jnp.logical_and(ids[None, :] > ids[:, None], used[None, :])
    next_used = jnp.min(jnp.where(later_used, ids[None, :], N_EXPERTS), axis=1)
    next_used = jnp.where(next_used < N_EXPERTS, next_used, -1)
    tile_next = jnp.sum(jnp.where(pick, next_used[None, :], 0), axis=1)
    return slot, tile_expert, tile_chunks, tile_slot.astype(jnp.int32), tile_next.astype(jnp.int32)


def kernel(x_prompt, x_sample, cache_mla_ckv, cache_mla_krope, cache_swa_k, cache_swa_v, c, c_ctx,
           ada_w, ada_b, norm_g, w_in, q_norm, kv_norm, w_fnet, w_uq, w_ukv, w_mla_o, swa_sink,
           w_swa_o, w_gate, b_gate, w_out, router_w, router_bias, exp_w1, exp_w3, exp_w2,
           shared_w1, shared_w3, shared_w2):
    xc = x_prompt.reshape(N_CTX, D_MODEL)
    xl = x_sample.reshape(N_LAT, D_MODEL)

    cond8 = jnp.concatenate([c_ctx[None, :], c, jnp.zeros((3, D_MODEL), F32)], axis=0)
    mod = _modulation(cond8, ada_w, ada_b)
    tile_cond = np.concatenate([np.zeros(NB_CTX, np.int32),
                                1 + np.arange(NB - NB_CTX, dtype=np.int32) // LAT_BLOCKS])

    tab = _rope_tables()
    bd, f_ctx, f_lat = _fnet_tables()
    e_mat = _rope_placement()
    tri = jnp.asarray(np.triu(np.ones((TB, TB), np.float32), 1), BF16)
    w_fnet_b, w_mla_o_b, w_swa_o_b, w_out_b, sw1_b, sw3_b, sw2_b = (
        w.astype(BF16) for w in (w_fnet, w_mla_o, w_swa_o, w_out, shared_w1, shared_w3, shared_w2))

    new_ckv, new_kr, new_k, new_v = [], [], [], []
    for l in range(DEPTH):
        modt = mod[l][tile_cond][:, None, :]
        wide, wqa, wk, wv = _layer_weights(l, w_in, w_uq, w_ukv)
        ng = norm_g[l]

        fin, ckv, kr, sq, sk, sv, h1, q_m, k_m, v_m = _stage_a(
            xc, xl, modt, ng[0:1], wide, q_norm[l][None, :], kv_norm[l][None, :], tab, wqa, wk, e_mat, wv)

        new_ckv.append(ckv[:N_CTX].reshape(BATCH, SEQ, MLA_KV_RANK))
        new_kr.append(kr[:N_CTX, :MLA_ROPE].reshape(BATCH, SEQ, MLA_ROPE))
        new_k.append(sk[:N_CTX].reshape(BATCH, SEQ, SWA_KV_HEADS, SWA_HEAD_DIM))
        new_v.append(sv[:N_CTX].reshape(BATCH, SEQ, SWA_KV_HEADS, SWA_HEAD_DIM))

        fn = (_fnet(fin, f_ctx, bd, BATCH, SEQ, 0),
              _fnet(fin, f_lat, bd, DEC_BATCH, DEC_SEQ, N_CTX // DEC_SEQ))

        kr_cache = jnp.pad(cache_mla_krope[:, l].reshape(N_CACHE, MLA_ROPE), ((0, 0), (0, 96)))
        k_c, v_c = _mla_cache_kv(cache_mla_ckv[:, l].reshape(N_CACHE, MLA_KV_RANK), kr_cache,
                                 wk, e_mat, wv)
        om = (_mla_attn(q_m, k_m, v_m, k_c, v_c, latent=False),
              _mla_attn(q_m, k_m, v_m, k_c, v_c, latent=True))

        ck = cache_swa_k[:, l].reshape(DEC_BATCH, PAST_LEN, 128)
        cv = cache_swa_v[:, l].reshape(DEC_BATCH, PAST_LEN, 128)
        osw = (_swa_attn(swa_sink[l], sq, sk, sv, ck, cv, latent=False),
               _swa_attn(swa_sink[l], sq, sk, sv, ck, cv, latent=True))

        x1, h2, eidx, epos, ew, counts = _stage_e(
            l, xc, xl, modt, ng[1:2], ng[2:3], fn + om + osw, h1, w_gate, b_gate[l][None, :],
            w_fnet_b, w_mla_o_b, w_swa_o_b, w_out_b,
            router_w[l].T.astype(BF16), router_bias[l][:, None], tri)
        slot, tile_expert, tile_chunks, tile_slot, tile_next = _moe_dispatch_plan(eidx, epos, counts[:, 0])
        xs = _sc_scatter_rows(h2, slot)
        ys = _experts(l, tile_expert, tile_chunks, tile_slot, tile_next, xs, exp_w1, exp_w3, exp_w2)
        picks = slot[:TOP_K].reshape(1, TOP_K * N_TOK)
        yg = _sc_gather_rows(ys, picks).reshape(2, TOP_K, N_TOK, PACKED)
        xc, xl = _stage_g(l, x1, modt, ng[3:4], yg, ew.T, h2, sw1_b, sw3_b, sw2_b)

    y_p = xc.reshape(BATCH, SEQ, D_MODEL)
    y_s = xl.reshape(DEC_BATCH, DEC_SEQ, D_MODEL)
    return (y_p, y_s, jnp.stack(new_ckv, axis=1), jnp.stack(new_kr, axis=1),
            jnp.stack(new_k, axis=1), jnp.stack(new_v, axis=1))
```
